```python
import math
import jax
import jax.numpy as jnp
from jax import lax
import numpy as np

D_MODEL = 2048
BATCH = 8
SEQ = 8192
DEPTH = 2

NUM_MIXERS = 2
EPS = 1e-6

ATT_HEAD_DIM = 64
ATT_Q_HEADS = D_MODEL // ATT_HEAD_DIM
ATT_KV_HEADS = ATT_Q_HEADS // 8
ATT_GROUP = ATT_Q_HEADS // ATT_KV_HEADS
WINDOW = 128
ATT_BLOCK = WINDOW
ROPE_THETA = 10000.0
QKV_WIDTH = (ATT_Q_HEADS + 2 * ATT_KV_HEADS) * ATT_HEAD_DIM

SSM_D_INNER = 2 * D_MODEL
SSM_HEAD_DIM = 64
SSM_HEADS = SSM_D_INNER // SSM_HEAD_DIM
SSM_GROUPS = 8
SSM_HEADS_PER_GROUP = SSM_HEADS // SSM_GROUPS
SSM_STATE = 128
SSM_CONV = 4
SSM_CHUNK = 256
SSM_CONV_DIM = SSM_D_INNER + 2 * SSM_GROUPS * SSM_STATE
SSM_IN_WIDTH = SSM_D_INNER + SSM_CONV_DIM + SSM_HEADS
SSM_NORM_GROUP = SSM_D_INNER // SSM_GROUPS

D_FF = -(-(8 * D_MODEL) // (3 * 256)) * 256

N_ATTN_LAYERS = (DEPTH + NUM_MIXERS - 1) // NUM_MIXERS
N_SSM_LAYERS = DEPTH // NUM_MIXERS

kernel_name = 'hybrid_swa_sink_mamba2_swiglu'


def rms_norm(x, gain):
    xf = x.astype(jnp.float32)
    y = xf * lax.rsqrt(jnp.mean(xf * xf, axis=-1, keepdims=True) + EPS)
    return (y * gain.astype(jnp.float32)).astype(x.dtype)


def rope_tables(positions):
    inv_freq = ROPE_THETA ** (-jnp.arange(0, ATT_HEAD_DIM, 2, dtype=jnp.float32) / ATT_HEAD_DIM)
    ang = positions.astype(jnp.float32)[..., None] * inv_freq
    return jnp.cos(ang)[:, :, None, :], jnp.sin(ang)[:, :, None, :]


def apply_rope(t, cos, sin):
    tf = t.astype(jnp.float32)
    t1, t2 = jnp.split(tf, 2, axis=-1)
    return jnp.concatenate([t1 * cos - t2 * sin, t2 * cos + t1 * sin], axis=-1).astype(t.dtype)


def sliding_window_attention(h, positions, w_qkv, q_norm, k_norm, sinks, w_o):
    b, s, _ = h.shape
    qkv = h @ w_qkv
    q, k, v = jnp.split(qkv, [ATT_Q_HEADS * ATT_HEAD_DIM, (ATT_Q_HEADS + ATT_KV_HEADS) * ATT_HEAD_DIM], axis=-1)
    q = q.reshape(b, s, ATT_Q_HEADS, ATT_HEAD_DIM)
    k = k.reshape(b, s, ATT_KV_HEADS, ATT_HEAD_DIM)
    v = v.reshape(b, s, ATT_KV_HEADS, ATT_HEAD_DIM)
    q = rms_norm(q, q_norm)
    k = rms_norm(k, k_norm)
    cos, sin = rope_tables(positions)
    q = apply_rope(q, cos, sin)
    k = apply_rope(k, cos, sin)

    nb = s // ATT_BLOCK
    qb = q.reshape(b, nb, ATT_BLOCK, ATT_KV_HEADS, ATT_GROUP, ATT_HEAD_DIM)
    kb = k.reshape(b, nb, ATT_BLOCK, ATT_KV_HEADS, ATT_HEAD_DIM)
    vb = v.reshape(b, nb, ATT_BLOCK, ATT_KV_HEADS, ATT_HEAD_DIM)
    shift = ((0, 0), (1, 0), (0, 0), (0, 0), (0, 0))
    kw = jnp.concatenate([jnp.pad(kb, shift)[:, :-1], kb], axis=2)
    vw = jnp.concatenate([jnp.pad(vb, shift)[:, :-1], vb], axis=2)

    scale = ATT_HEAD_DIM ** -0.5
    scores = jnp.einsum('bnqhgd,bnkhd->bnhgqk', qb, kw).astype(jnp.float32) * scale
    qi = jnp.arange(ATT_BLOCK)[:, None]
    kj = jnp.arange(2 * ATT_BLOCK)[None, :]
    band = (kj > qi) & (kj <= qi + ATT_BLOCK)
    not_before_start = (jnp.arange(nb) > 0)[:, None, None] | (kj >= ATT_BLOCK)[None]
    valid = band[None] & not_before_start
    scores = jnp.where(valid[None, :, None, None], scores, -jnp.inf)

    sink = sinks.astype(jnp.float32).reshape(ATT_KV_HEADS, ATT_GROUP)[None, None, :, :, None, None]
    m = jnp.maximum(jnp.max(scores, axis=-1, keepdims=True), sink)
    p = jnp.exp(scores - m)
    probs = p / (jnp.sum(p, axis=-1, keepdims=True) + jnp.exp(sink - m))
    out = jnp.einsum('bnhgqk,bnkhd->bnqhgd', probs.astype(vw.dtype), vw)
    return out.reshape(b, s, ATT_Q_HEADS * ATT_HEAD_DIM) @ w_o


def causal_depthwise_conv(u, w, bias):
    out = lax.conv_general_dilated(u, w[:, None, :].astype(u.dtype), window_strides=(1,),
                                   padding=[(SSM_CONV - 1, 0)],
                                   dimension_numbers=('NWC', 'WIO', 'NWC'),
                                   feature_group_count=u.shape[-1])
    return out + bias.astype(u.dtype)


def mamba2_ssd(h, w_in, conv_w, conv_b, dt_bias, a_log, d_skip, norm_g, w_out):
    b, s, _ = h.shape
    G, R, P, N, L = SSM_GROUPS, SSM_HEADS_PER_GROUP, SSM_HEAD_DIM, SSM_STATE, SSM_CHUNK
    zxbcdt = h @ w_in
    z = zxbcdt[..., :SSM_D_INNER]
    xbc = zxbcdt[..., SSM_D_INNER:SSM_D_INNER + SSM_CONV_DIM]
    dt = zxbcdt[..., SSM_D_INNER + SSM_CONV_DIM:]
    xbc = jax.nn.silu(causal_depthwise_conv(xbc, conv_w, conv_b))
    xs, bm, cm = jnp.split(xbc, [SSM_D_INNER, SSM_D_INNER + G * N], axis=-1)

    xs = xs.astype(jnp.float32).reshape(b, s, G, R, P)
    bm = bm.astype(jnp.float32).reshape(b, s, G, N)
    cm = cm.astype(jnp.float32).reshape(b, s, G, N)
    dt = jax.nn.softplus(dt.astype(jnp.float32) + dt_bias.astype(jnp.float32)).reshape(b, s, G, R)
    a = -jnp.exp(a_log.astype(jnp.float32)).reshape(G, R)

    pad = (-s) % L
    nc = (s + pad) // L

    def to_chunks(t):
        t = jnp.pad(t, [(0, 0), (0, pad)] + [(0, 0)] * (t.ndim - 2))
        return jnp.moveaxis(t.reshape((b, nc, L) + t.shape[2:]), 1, 0)

    causal = jnp.tril(jnp.ones((L, L), dtype=bool))[None, :, :, None, None]

    def chunk_step(state, inp):
        xc, dtc, bc, cc = inp
        acum = jnp.cumsum(dtc * a, axis=1)
        seg = acum[:, :, None] - acum[:, None, :]
        decay = jnp.exp(jnp.where(causal, seg, -jnp.inf))
        xdt = xc * dtc[..., None]
        cb = jnp.einsum('blgn,bsgn->blsg', cc, bc)
        y_diag = jnp.einsum('blsgr,bsgrp->blgrp', cb[..., None] * decay, xdt)
        y_off = jnp.einsum('blgn,bgrpn->blgrp', cc, state) * jnp.exp(acum)[..., None]
        to_end = jnp.exp(acum[:, -1:] - acum)
        new_state = (state * jnp.exp(acum[:, -1])[..., None, None]
                     + jnp.einsum('bsgn,bsgr,bsgrp->bgrpn', bc, to_end, xdt))
        return new_state, y_diag + y_off

    init = jnp.zeros((b, G, R, P, N), jnp.float32)
    _, y = lax.scan(chunk_step, init, (to_chunks(xs), to_chunks(dt), to_chunks(bm), to_chunks(cm)))
    y = jnp.moveaxis(y, 0, 1).reshape(b, nc * L, G, R, P)[:, :s]
    y = y + d_skip.astype(jnp.float32).reshape(G, R)[:, :, None] * xs
    y = y.reshape(b, s, SSM_D_INNER) * jax.nn.silu(z.astype(jnp.float32))
    y = y.reshape(b, s, G, SSM_NORM_GROUP)
    y = y * lax.rsqrt(jnp.mean(y * y, axis=-1, keepdims=True) + EPS)
    y = y.reshape(b, s, SSM_D_INNER) * norm_g.astype(jnp.float32)
    return y.astype(h.dtype) @ w_out


def swiglu(h, w_gate, w_up, w_down):
    return (jax.nn.silu(h @ w_gate) * (h @ w_up)) @ w_down


def _fwd_setup_inputs(seed: int = 0) -> dict:
    key = jax.random.key(seed)
    ks = jax.random.split(key, 24)
    f32 = jnp.float32
    resid = (2 * DEPTH) ** -0.5

    def nrm(k, shape, fan_in, scale=1.0):
        return jax.random.normal(k, shape, f32) * (scale * fan_in ** -0.5)

    def gain(k, shape):
        return 1.0 + 0.02 * jax.random.normal(k, shape, f32)

    x = jax.random.normal(ks[0], (BATCH, SEQ, D_MODEL), f32)
    start = jax.random.randint(ks[1], (BATCH, 1), 0, 4096)
    positions = (start + jnp.arange(SEQ)[None, :]).astype(jnp.int32)

    dt0 = jnp.exp(jax.random.uniform(ks[14], (N_SSM_LAYERS, SSM_HEADS), f32)
                  * (math.log(0.1) - math.log(0.001)) + math.log(0.001))
    return {
        'x': x,
        'positions': positions,
        'mixer_norm': gain(ks[2], (DEPTH, D_MODEL)),
        'ffn_norm': gain(ks[3], (DEPTH, D_MODEL)),
        'attn_w_qkv': nrm(ks[4], (N_ATTN_LAYERS, D_MODEL, QKV_WIDTH), D_MODEL),
        'attn_q_norm': gain(ks[5], (N_ATTN_LAYERS, ATT_HEAD_DIM)),
        'attn_k_norm': gain(ks[6], (N_ATTN_LAYERS, ATT_HEAD_DIM)),
        'attn_sinks': 0.5 * jax.random.normal(ks[7], (N_ATTN_LAYERS, ATT_Q_HEADS), f32),
        'attn_w_o': nrm(ks[8], (N_ATTN_LAYERS, ATT_Q_HEADS * ATT_HEAD_DIM, D_MODEL), ATT_Q_HEADS * ATT_HEAD_DIM, resid),
        'ssm_w_in': nrm(ks[9], (N_SSM_LAYERS, D_MODEL, SSM_IN_WIDTH), D_MODEL),
        'ssm_conv_w': nrm(ks[10], (N_SSM_LAYERS, SSM_CONV, SSM_CONV_DIM), SSM_CONV),
        'ssm_conv_b': 0.02 * jax.random.normal(ks[11], (N_SSM_LAYERS, SSM_CONV_DIM), f32),
        'ssm_dt_bias': dt0 + jnp.log(-jnp.expm1(-dt0)),
        'ssm_a_log': jnp.log(jax.random.uniform(ks[12], (N_SSM_LAYERS, SSM_HEADS), f32, 1.0, 16.0)),
        'ssm_d': gain(ks[13], (N_SSM_LAYERS, SSM_HEADS)),
        'ssm_norm': gain(ks[15], (N_SSM_LAYERS, SSM_D_INNER)),
        'ssm_w_out': nrm(ks[16], (N_SSM_LAYERS, SSM_D_INNER, D_MODEL), SSM_D_INNER, resid),
        'ffn_w_gate': nrm(ks[17], (DEPTH, D_MODEL, D_FF), D_MODEL),
        'ffn_w_up': nrm(ks[18], (DEPTH, D_MODEL, D_FF), D_MODEL),
        'ffn_w_down': nrm(ks[19], (DEPTH, D_FF, D_MODEL), D_FF, resid),
    }


def _fwd_reference(x, positions, mixer_norm, ffn_norm, attn_w_qkv, attn_q_norm, attn_k_norm, attn_sinks,
              attn_w_o, ssm_w_in, ssm_conv_w, ssm_conv_b, ssm_dt_bias, ssm_a_log, ssm_d, ssm_norm,
              ssm_w_out, ffn_w_gate, ffn_w_up, ffn_w_down):
    for i in range(DEPTH):
        h = rms_norm(x, mixer_norm[i])
        j = i // NUM_MIXERS
        if i % NUM_MIXERS == 0:
            x = x + sliding_window_attention(h, positions, attn_w_qkv[j], attn_q_norm[j], attn_k_norm[j],
                                             attn_sinks[j], attn_w_o[j])
        else:
            x = x + mamba2_ssd(h, ssm_w_in[j], ssm_conv_w[j], ssm_conv_b[j], ssm_dt_bias[j], ssm_a_log[j],
                               ssm_d[j], ssm_norm[j], ssm_w_out[j])
        x = x + swiglu(rms_norm(x, ffn_norm[i]), ffn_w_gate[i], ffn_w_up[i], ffn_w_down[i])
    return x


import jax as _jax
import jax.numpy as _jnp

TWIN_FORMAT = 'train_step'
FWD_PARAMS = ['x', 'positions', 'mixer_norm', 'ffn_norm', 'attn_w_qkv', 'attn_q_norm', 'attn_k_norm', 'attn_sinks', 'attn_w_o', 'ssm_w_in', 'ssm_conv_w', 'ssm_conv_b', 'ssm_dt_bias', 'ssm_a_log', 'ssm_d', 'ssm_norm', 'ssm_w_out', 'ffn_w_gate', 'ffn_w_up', 'ffn_w_down']
TWIN_WEIGHTS = ['mixer_norm', 'ffn_norm', 'attn_w_qkv', 'attn_q_norm', 'attn_k_norm', 'attn_sinks', 'attn_w_o', 'ssm_w_in', 'ssm_conv_w', 'ssm_conv_b', 'ssm_dt_bias', 'ssm_a_log', 'ssm_d', 'ssm_norm', 'ssm_w_out', 'ffn_w_gate', 'ffn_w_up', 'ffn_w_down']
TWIN_DIFF_INPUT = 'x'
TWIN_INPUTS = ['x', 'positions', 'mixer_norm', 'ffn_norm', 'attn_w_qkv', 'attn_q_norm', 'attn_k_norm', 'attn_sinks', 'attn_w_o', 'ssm_w_in', 'ssm_conv_w', 'ssm_conv_b', 'ssm_dt_bias', 'ssm_a_log', 'ssm_d', 'ssm_norm', 'ssm_w_out', 'ffn_w_gate', 'ffn_w_up', 'ffn_w_down', 'loss_target', 'm_mixer_norm', 'm_ffn_norm', 'm_attn_w_qkv', 'm_attn_q_norm', 'm_attn_k_norm', 'm_attn_sinks', 'm_attn_w_o', 'm_ssm_w_in', 'm_ssm_conv_w', 'm_ssm_conv_b', 'm_ssm_dt_bias', 'm_ssm_a_log', 'm_ssm_d', 'm_ssm_norm', 'm_ssm_w_out', 'm_ffn_w_gate', 'm_ffn_w_up', 'm_ffn_w_down', 'v_mixer_norm', 'v_ffn_norm', 'v_attn_w_qkv', 'v_attn_q_norm', 'v_attn_k_norm', 'v_attn_sinks', 'v_attn_w_o', 'v_ssm_w_in', 'v_ssm_conv_w', 'v_ssm_conv_b', 'v_ssm_dt_bias', 'v_ssm_a_log', 'v_ssm_d', 'v_ssm_norm', 'v_ssm_w_out', 'v_ffn_w_gate', 'v_ffn_w_up', 'v_ffn_w_down']
TWIN_OUTPUTS = ['loss', 'grad_x', 'grad_mixer_norm', 'grad_ffn_norm', 'grad_attn_w_qkv', 'grad_attn_q_norm', 'grad_attn_k_norm', 'grad_attn_sinks', 'grad_attn_w_o', 'grad_ssm_w_in', 'grad_ssm_conv_w', 'grad_ssm_conv_b', 'grad_ssm_dt_bias', 'grad_ssm_a_log', 'grad_ssm_d', 'grad_ssm_norm', 'grad_ssm_w_out', 'grad_ffn_w_gate', 'grad_ffn_w_up', 'grad_ffn_w_down', 'delta_mixer_norm', 'delta_ffn_norm', 'delta_attn_w_qkv', 'delta_attn_q_norm', 'delta_attn_k_norm', 'delta_attn_sinks', 'delta_attn_w_o', 'delta_ssm_w_in', 'delta_ssm_conv_w', 'delta_ssm_conv_b', 'delta_ssm_dt_bias', 'delta_ssm_a_log', 'delta_ssm_d', 'delta_ssm_norm', 'delta_ssm_w_out', 'delta_ffn_w_gate', 'delta_ffn_w_up', 'delta_ffn_w_down', 'new_m_mixer_norm', 'new_m_ffn_norm', 'new_m_attn_w_qkv', 'new_m_attn_q_norm', 'new_m_attn_k_norm', 'new_m_attn_sinks', 'new_m_attn_w_o', 'new_m_ssm_w_in', 'new_m_ssm_conv_w', 'new_m_ssm_conv_b', 'new_m_ssm_dt_bias', 'new_m_ssm_a_log', 'new_m_ssm_d', 'new_m_ssm_norm', 'new_m_ssm_w_out', 'new_m_ffn_w_gate', 'new_m_ffn_w_up', 'new_m_ffn_w_down', 'new_v_mixer_norm', 'new_v_ffn_norm', 'new_v_attn_w_qkv', 'new_v_attn_q_norm', 'new_v_attn_k_norm', 'new_v_attn_sinks', 'new_v_attn_w_o', 'new_v_ssm_w_in', 'new_v_ssm_conv_w', 'new_v_ssm_conv_b', 'new_v_ssm_dt_bias', 'new_v_ssm_a_log', 'new_v_ssm_d', 'new_v_ssm_norm', 'new_v_ssm_w_out', 'new_v_ffn_w_gate', 'new_v_ffn_w_up', 'new_v_ffn_w_down']
TWIN_LEAF_KINDS = {'loss': 'loss', 'grad_x': 'grad_x', 'grad_mixer_norm': 'grad_w', 'grad_ffn_norm': 'grad_w', 'grad_attn_w_qkv': 'grad_w', 'grad_attn_q_norm': 'grad_w', 'grad_attn_k_norm': 'grad_w', 'grad_attn_sinks': 'grad_w', 'grad_attn_w_o': 'grad_w', 'grad_ssm_w_in': 'grad_w', 'grad_ssm_conv_w': 'grad_w', 'grad_ssm_conv_b': 'grad_w', 'grad_ssm_dt_bias': 'grad_w', 'grad_ssm_a_log': 'grad_w', 'grad_ssm_d': 'grad_w', 'grad_ssm_norm': 'grad_w', 'grad_ssm_w_out': 'grad_w', 'grad_ffn_w_gate': 'grad_w', 'grad_ffn_w_up': 'grad_w', 'grad_ffn_w_down': 'grad_w', 'delta_mixer_norm': 'delta_w', 'delta_ffn_norm': 'delta_w', 'delta_attn_w_qkv': 'delta_w', 'delta_attn_q_norm': 'delta_w', 'delta_attn_k_norm': 'delta_w', 'delta_attn_sinks': 'delta_w', 'delta_attn_w_o': 'delta_w', 'delta_ssm_w_in': 'delta_w', 'delta_ssm_conv_w': 'delta_w', 'delta_ssm_conv_b': 'delta_w', 'delta_ssm_dt_bias': 'delta_w', 'delta_ssm_a_log': 'delta_w', 'delta_ssm_d': 'delta_w', 'delta_ssm_norm': 'delta_w', 'delta_ssm_w_out': 'delta_w', 'delta_ffn_w_gate': 'delta_w', 'delta_ffn_w_up': 'delta_w', 'delta_ffn_w_down': 'delta_w', 'new_m_mixer_norm': 'new_m', 'new_m_ffn_norm': 'new_m', 'new_m_attn_w_qkv': 'new_m', 'new_m_attn_q_norm': 'new_m', 'new_m_attn_k_norm': 'new_m', 'new_m_attn_sinks': 'new_m', 'new_m_attn_w_o': 'new_m', 'new_m_ssm_w_in': 'new_m', 'new_m_ssm_conv_w': 'new_m', 'new_m_ssm_conv_b': 'new_m', 'new_m_ssm_dt_bias': 'new_m', 'new_m_ssm_a_log': 'new_m', 'new_m_ssm_d': 'new_m', 'new_m_ssm_norm': 'new_m', 'new_m_ssm_w_out': 'new_m', 'new_m_ffn_w_gate': 'new_m', 'new_m_ffn_w_up': 'new_m', 'new_m_ffn_w_down': 'new_m', 'new_v_mixer_norm': 'new_v', 'new_v_ffn_norm': 'new_v', 'new_v_attn_w_qkv': 'new_v', 'new_v_attn_q_norm': 'new_v', 'new_v_attn_k_norm': 'new_v', 'new_v_attn_sinks': 'new_v', 'new_v_attn_w_o': 'new_v', 'new_v_ssm_w_in': 'new_v', 'new_v_ssm_conv_w': 'new_v', 'new_v_ssm_conv_b': 'new_v', 'new_v_ssm_dt_bias': 'new_v', 'new_v_ssm_a_log': 'new_v', 'new_v_ssm_d': 'new_v', 'new_v_ssm_norm': 'new_v', 'new_v_ssm_w_out': 'new_v', 'new_v_ffn_w_gate': 'new_v', 'new_v_ffn_w_up': 'new_v', 'new_v_ffn_w_down': 'new_v'}


def _forward(args):
    return _fwd_reference(*[args[k] for k in FWD_PARAMS])


def _output_shape():
    def fwd():
        inp = _fwd_setup_inputs(0)
        return _fwd_reference(*[inp[k] for k in FWD_PARAMS])
    out = _jax.eval_shape(fwd)
    return out.shape, out.dtype

N_MICROBATCH = 1
ADAM_LR = 0.001
ADAM_B1 = 0.9
ADAM_B2 = 0.999
ADAM_EPS = 1e-08
ADAM_WD = 0.01
ADAM_STEP = 10
PER_EXAMPLE_BATCH_AXIS = {'x': 0, 'positions': 0, 'loss_target': 0}
SHARED_INPUTS = []
_WEIGHT_DTYPES = {'mixer_norm': _jnp.float32, 'ffn_norm': _jnp.float32, 'attn_w_qkv': _jnp.float32, 'attn_q_norm': _jnp.float32, 'attn_k_norm': _jnp.float32, 'attn_sinks': _jnp.float32, 'attn_w_o': _jnp.float32, 'ssm_w_in': _jnp.float32, 'ssm_conv_w': _jnp.float32, 'ssm_conv_b': _jnp.float32, 'ssm_dt_bias': _jnp.float32, 'ssm_a_log': _jnp.float32, 'ssm_d': _jnp.float32, 'ssm_norm': _jnp.float32, 'ssm_w_out': _jnp.float32, 'ffn_w_gate': _jnp.float32, 'ffn_w_up': _jnp.float32, 'ffn_w_down': _jnp.float32}
MOMENT_SCALE = {'mixer_norm': 1.515871e-01, 'ffn_norm': 6.174965e+00, 'attn_w_qkv': 4.486520e-02, 'attn_q_norm': 2.895645e+00, 'attn_k_norm': 2.904616e+00, 'attn_sinks': 1.702440e-01, 'attn_w_o': 6.302226e-02, 'ssm_w_in': 8.169029e-02, 'ssm_conv_w': 1.249057e-01, 'ssm_conv_b': 3.956674e-01, 'ssm_dt_bias': 1.602329e-01, 'ssm_a_log': 7.488287e-01, 'ssm_d': 8.340755e-01, 'ssm_norm': 5.344284e+00, 'ssm_w_out': 5.610258e-01, 'ffn_w_gate': 5.339361e-02, 'ffn_w_up': 5.333678e-02, 'ffn_w_down': 1.729728e-01}


def _to_microbatches(a, axis):
    t = _jnp.moveaxis(a, axis, 0)
    t = t.reshape((N_MICROBATCH, t.shape[0] // N_MICROBATCH) + t.shape[1:])
    return _jnp.moveaxis(t, 1, axis + 1)


def setup_inputs(seed: int = 0) -> dict:
    inp = _fwd_setup_inputs(seed)
    key = _jax.random.fold_in(_jax.random.key(seed), 7919)
    shape, _ = _output_shape()
    out = dict(inp)
    out["loss_target"] = _jax.random.normal(_jax.random.fold_in(key, 0), shape, _jnp.float32)
    for i, name in enumerate(TWIN_WEIGHTS):
        w = inp[name].astype(_jnp.float32)
        if MOMENT_SCALE is None:
            s = _jnp.sqrt(_jnp.mean(_jnp.square(w)) + 1e-30)
        else:
            s = MOMENT_SCALE[name]
        km, kv = _jax.random.split(_jax.random.fold_in(key, i + 1))
        out[name] = w
        out["m_" + name] = s * _jax.random.normal(km, w.shape, _jnp.float32)
        out["v_" + name] = (s * s) * _jax.random.uniform(kv, w.shape, _jnp.float32, 0.5, 1.5)
    if N_MICROBATCH > 1:
        for name, axis in PER_EXAMPLE_BATCH_AXIS.items():
            out[name] = _to_microbatches(out[name], axis)
    return {'x': out['x'], 'positions': out['positions'], 'mixer_norm': out['mixer_norm'], 'ffn_norm': out['ffn_norm'], 'attn_w_qkv': out['attn_w_qkv'], 'attn_q_norm': out['attn_q_norm'], 'attn_k_norm': out['attn_k_norm'], 'attn_sinks': out['attn_sinks'], 'attn_w_o': out['attn_w_o'], 'ssm_w_in': out['ssm_w_in'], 'ssm_conv_w': out['ssm_conv_w'], 'ssm_conv_b': out['ssm_conv_b'], 'ssm_dt_bias': out['ssm_dt_bias'], 'ssm_a_log': out['ssm_a_log'], 'ssm_d': out['ssm_d'], 'ssm_norm': out['ssm_norm'], 'ssm_w_out': out['ssm_w_out'], 'ffn_w_gate': out['ffn_w_gate'], 'ffn_w_up': out['ffn_w_up'], 'ffn_w_down': out['ffn_w_down'], 'loss_target': out['loss_target'], 'm_mixer_norm': out['m_mixer_norm'], 'm_ffn_norm': out['m_ffn_norm'], 'm_attn_w_qkv': out['m_attn_w_qkv'], 'm_attn_q_norm': out['m_attn_q_norm'], 'm_attn_k_norm': out['m_attn_k_norm'], 'm_attn_sinks': out['m_attn_sinks'], 'm_attn_w_o': out['m_attn_w_o'], 'm_ssm_w_in': out['m_ssm_w_in'], 'm_ssm_conv_w': out['m_ssm_conv_w'], 'm_ssm_conv_b': out['m_ssm_conv_b'], 'm_ssm_dt_bias': out['m_ssm_dt_bias'], 'm_ssm_a_log': out['m_ssm_a_log'], 'm_ssm_d': out['m_ssm_d'], 'm_ssm_norm': out['m_ssm_norm'], 'm_ssm_w_out': out['m_ssm_w_out'], 'm_ffn_w_gate': out['m_ffn_w_gate'], 'm_ffn_w_up': out['m_ffn_w_up'], 'm_ffn_w_down': out['m_ffn_w_down'], 'v_mixer_norm': out['v_mixer_norm'], 'v_ffn_norm': out['v_ffn_norm'], 'v_attn_w_qkv': out['v_attn_w_qkv'], 'v_attn_q_norm': out['v_attn_q_norm'], 'v_attn_k_norm': out['v_attn_k_norm'], 'v_attn_sinks': out['v_attn_sinks'], 'v_attn_w_o': out['v_attn_w_o'], 'v_ssm_w_in': out['v_ssm_w_in'], 'v_ssm_conv_w': out['v_ssm_conv_w'], 'v_ssm_conv_b': out['v_ssm_conv_b'], 'v_ssm_dt_bias': out['v_ssm_dt_bias'], 'v_ssm_a_log': out['v_ssm_a_log'], 'v_ssm_d': out['v_ssm_d'], 'v_ssm_norm': out['v_ssm_norm'], 'v_ssm_w_out': out['v_ssm_w_out'], 'v_ffn_w_gate': out['v_ffn_w_gate'], 'v_ffn_w_up': out['v_ffn_w_up'], 'v_ffn_w_down': out['v_ffn_w_down']}


def _loss(weights, diff, rest, loss_target):
    with _jax.named_scope("forward"):
        args = {**rest, TWIN_DIFF_INPUT: diff, **{k: w.astype(_WEIGHT_DTYPES[k]) for k, w in weights.items()}}
        y = _forward(args)
    with _jax.named_scope("loss_head"):
        err = _jnp.square(y.astype(_jnp.float32) - loss_target)
        return 0.5 * _jnp.sum(_jnp.mean(err, axis=-1)) if err.ndim else 0.5 * err


def _adamw(w, g, m, v):
    m = ADAM_B1 * m + (1.0 - ADAM_B1) * g
    v = ADAM_B2 * v + (1.0 - ADAM_B2) * _jnp.square(g)
    m_hat = m / (1.0 - ADAM_B1 ** ADAM_STEP)
    v_hat = v / (1.0 - ADAM_B2 ** ADAM_STEP)
    delta = -ADAM_LR * (m_hat / (_jnp.sqrt(v_hat) + ADAM_EPS) + ADAM_WD * w)
    return delta, m, v


def reference(x, positions, mixer_norm, ffn_norm, attn_w_qkv, attn_q_norm, attn_k_norm, attn_sinks, attn_w_o, ssm_w_in, ssm_conv_w, ssm_conv_b, ssm_dt_bias, ssm_a_log, ssm_d, ssm_norm, ssm_w_out, ffn_w_gate, ffn_w_up, ffn_w_down, loss_target, m_mixer_norm, m_ffn_norm, m_attn_w_qkv, m_attn_q_norm, m_attn_k_norm, m_attn_sinks, m_attn_w_o, m_ssm_w_in, m_ssm_conv_w, m_ssm_conv_b, m_ssm_dt_bias, m_ssm_a_log, m_ssm_d, m_ssm_norm, m_ssm_w_out, m_ffn_w_gate, m_ffn_w_up, m_ffn_w_down, v_mixer_norm, v_ffn_norm, v_attn_w_qkv, v_attn_q_norm, v_attn_k_norm, v_attn_sinks, v_attn_w_o, v_ssm_w_in, v_ssm_conv_w, v_ssm_conv_b, v_ssm_dt_bias, v_ssm_a_log, v_ssm_d, v_ssm_norm, v_ssm_w_out, v_ffn_w_gate, v_ffn_w_up, v_ffn_w_down):
    given = dict(x=x, positions=positions, mixer_norm=mixer_norm, ffn_norm=ffn_norm, attn_w_qkv=attn_w_qkv, attn_q_norm=attn_q_norm, attn_k_norm=attn_k_norm, attn_sinks=attn_sinks, attn_w_o=attn_w_o, ssm_w_in=ssm_w_in, ssm_conv_w=ssm_conv_w, ssm_conv_b=ssm_conv_b, ssm_dt_bias=ssm_dt_bias, ssm_a_log=ssm_a_log, ssm_d=ssm_d, ssm_norm=ssm_norm, ssm_w_out=ssm_w_out, ffn_w_gate=ffn_w_gate, ffn_w_up=ffn_w_up, ffn_w_down=ffn_w_down, loss_target=loss_target, m_mixer_norm=m_mixer_norm, m_ffn_norm=m_ffn_norm, m_attn_w_qkv=m_attn_w_qkv, m_attn_q_norm=m_attn_q_norm, m_attn_k_norm=m_attn_k_norm, m_attn_sinks=m_attn_sinks, m_attn_w_o=m_attn_w_o, m_ssm_w_in=m_ssm_w_in, m_ssm_conv_w=m_ssm_conv_w, m_ssm_conv_b=m_ssm_conv_b, m_ssm_dt_bias=m_ssm_dt_bias, m_ssm_a_log=m_ssm_a_log, m_ssm_d=m_ssm_d, m_ssm_norm=m_ssm_norm, m_ssm_w_out=m_ssm_w_out, m_ffn_w_gate=m_ffn_w_gate, m_ffn_w_up=m_ffn_w_up, m_ffn_w_down=m_ffn_w_down, v_mixer_norm=v_mixer_norm, v_ffn_norm=v_ffn_norm, v_attn_w_qkv=v_attn_w_qkv, v_attn_q_norm=v_attn_q_norm, v_attn_k_norm=v_attn_k_norm, v_attn_sinks=v_attn_sinks, v_attn_w_o=v_attn_w_o, v_ssm_w_in=v_ssm_w_in, v_ssm_conv_w=v_ssm_conv_w, v_ssm_conv_b=v_ssm_conv_b, v_ssm_dt_bias=v_ssm_dt_bias, v_ssm_a_log=v_ssm_a_log, v_ssm_d=v_ssm_d, v_ssm_norm=v_ssm_norm, v_ssm_w_out=v_ssm_w_out, v_ffn_w_gate=v_ffn_w_gate, v_ffn_w_up=v_ffn_w_up, v_ffn_w_down=v_ffn_w_down)
    weights = {n: given[n] for n in TWIN_WEIGHTS}
    shared = {n: given[n] for n in SHARED_INPUTS}
    per_example = {n: given[n] for n in ['x', 'positions']}
    grad_fn = _jax.value_and_grad(_loss, argnums=(0, 1))

    def one_microbatch(ex, loss_target):
        ex = dict(ex)
        diff = ex.pop(TWIN_DIFF_INPUT)
        return grad_fn(weights, diff, {**shared, **ex}, loss_target)

    if N_MICROBATCH == 1:
        loss, (grad_w, grad_x) = one_microbatch(per_example, given["loss_target"])
    else:
        def body(carry, xs):
            loss_sum, grad_sum = carry
            l_k, (gw_k, gx_k) = one_microbatch(xs[0], xs[1])
            with _jax.named_scope("update"):
                return (loss_sum + l_k, _jax.tree.map(_jnp.add, grad_sum, gw_k)), gx_k

        init = (_jnp.zeros((), _jnp.float32), _jax.tree.map(_jnp.zeros_like, weights))
        (loss, grad_w), grad_x = _jax.lax.scan(body, init, (per_example, given["loss_target"]))
    with _jax.named_scope("update"):
        delta_w, new_m, new_v = {}, {}, {}
        for n in TWIN_WEIGHTS:
            delta_w[n], new_m[n], new_v[n] = _adamw(weights[n], grad_w[n], given["m_" + n], given["v_" + n])
    return (loss, grad_x, *[grad_w[n] for n in TWIN_WEIGHTS], *[delta_w[n] for n in TWIN_WEIGHTS],
            *[new_m[n] for n in TWIN_WEIGHTS], *[new_v[n] for n in TWIN_WEIGHTS])
```

```python
import math

import jax
import jax.numpy as jnp
from jax import lax
from jax.experimental import pallas as pl
from jax.experimental.pallas import tpu as pltpu

F32 = jnp.float32
BF16 = jnp.bfloat16

N_DEV = 8
EPS = 1e-6
LANES = 128
HEAD = 64
ATT_GROUP = 8
ATT_GW = ATT_GROUP * HEAD
WINDOW = 128
ROPE_THETA = 10000.0
SSM_GROUPS = 8
SSM_STATE = 128
SSM_CONV = 4
SSM_CHUNK = 256
HALO = 8
ADAM_LR, ADAM_B1, ADAM_B2, ADAM_EPS, ADAM_WD, ADAM_STEP = 0.001, 0.9, 0.999, 1e-08, 0.01, 10
VMEM_LIMIT = 56 * 1024 * 1024
MESH = pl.DeviceIdType.MESH

_NN = (((1,), (0,)), ((), ()))
_NT = (((1,), (1,)), ((), ()))
_TN = (((0,), (0,)), ((), ()))


def _dot(a, b, dims=_NN):
    return lax.dot_general(a, b, dims, preferred_element_type=F32)


def _tile(n, cap):
    if n % LANES:
        return n
    best = LANES
    for t in range(LANES, min(n, cap) + 1, LANES):
        if n % t == 0:
            best = t
    return best


def _params(sem):
    return pltpu.CompilerParams(dimension_semantics=sem, vmem_limit_bytes=VMEM_LIMIT)


def _matmul(a, b, *, mode, out_dtype, name, add=None):
    if mode == "nn":
        (M, K), N = a.shape, b.shape[1]
    elif mode == "nt":
        (M, K), N = a.shape, b.shape[0]
    else:
        (K, M), N = a.shape, b.shape[1]
    tm, tn, tk = _tile(M, 1024), _tile(N, 512), _tile(K, 512)
    nk = K // tk
    dims = {"nn": _NN, "nt": _NT, "tn": _TN}[mode]
    a_spec = pl.BlockSpec((tk, tm), lambda i, j, k: (k, i)) if mode == "tn" else pl.BlockSpec((tm, tk), lambda i, j, k: (i, k))
    b_spec = pl.BlockSpec((tn, tk), lambda i, j, k: (j, k)) if mode == "nt" else pl.BlockSpec((tk, tn), lambda i, j, k: (k, j))
    o_spec = pl.BlockSpec((tm, tn), lambda i, j, k: (i, j))
    has_add = add is not None

    def body(*refs):
        if has_add:
            a_ref, b_ref, add_ref, o_ref, acc = refs
        else:
            a_ref, b_ref, o_ref, acc = refs
        k = pl.program_id(2)

        @pl.when(k == 0)
        def _():
            acc[...] = jnp.zeros_like(acc)

        acc[...] += _dot(a_ref[...].astype(BF16), b_ref[...].astype(BF16), dims)

        @pl.when(k == nk - 1)
        def _():
            r = acc[...]
            if has_add:
                r = r + add_ref[...].astype(F32)
            o_ref[...] = r.astype(out_dtype)

    return pl.pallas_call(
        body, name=name, grid=(M // tm, N // tn, nk),
        in_specs=[a_spec, b_spec] + ([o_spec] if has_add else []),
        out_specs=o_spec, out_shape=jax.ShapeDtypeStruct((M, N), out_dtype),
        scratch_shapes=[pltpu.VMEM((tm, tn), F32)],
        compiler_params=_params(("parallel", "parallel", "arbitrary")),
    )(*((a, b, add) if has_add else (a, b)))


def _rms_fwd(x, gain, name):
    T, D = x.shape
    tr = 256

    def body(x_ref, g_ref, h_ref):
        xv = x_ref[...]
        rstd = lax.rsqrt(jnp.mean(xv * xv, axis=1, keepdims=True) + EPS)
        h_ref[...] = (xv * rstd * g_ref[...]).astype(BF16)

    return pl.pallas_call(
        body, name=name, grid=(T // tr,),
        in_specs=[pl.BlockSpec((tr, D), lambda i: (i, 0)), pl.BlockSpec((1, D), lambda i: (0, 0))],
        out_specs=pl.BlockSpec((tr, D), lambda i: (i, 0)),
        out_shape=jax.ShapeDtypeStruct((T, D), BF16),
        compiler_params=_params(("parallel",)),
    )(x, gain)


def _rms_bwd(x, gain, dh, dres, name):
    T, D = x.shape
    tr = 256

    def body(x_ref, g_ref, dh_ref, dr_ref, dx_ref, dg_ref):
        @pl.when(pl.program_id(0) == 0)
        def _():
            dg_ref[...] = jnp.zeros_like(dg_ref)

        xv = x_ref[...]
        rstd = lax.rsqrt(jnp.mean(xv * xv, axis=1, keepdims=True) + EPS)
        xhat = xv * rstd
        dy = dh_ref[...].astype(F32)
        dg_ref[...] += jnp.sum(dy * xhat, axis=0, keepdims=True)
        dxh = dy * g_ref[...]
        dx_ref[...] = dr_ref[...] + rstd * (dxh - xhat * jnp.mean(dxh * xhat, axis=1, keepdims=True))

    row = pl.BlockSpec((tr, D), lambda i: (i, 0))
    vec = pl.BlockSpec((1, D), lambda i: (0, 0))
    return pl.pallas_call(
        body, name=name, grid=(T // tr,), in_specs=[row, vec, row, row], out_specs=[row, vec],
        out_shape=[jax.ShapeDtypeStruct((T, D), F32), jax.ShapeDtypeStruct((1, D), F32)],
        compiler_params=_params(("arbitrary",)),
    )(x, gain, dh, dres)


def _loss_head(y, target):
    T, D = y.shape
    tr = 256

    def body(y_ref, t_ref, s_ref, d_ref):
        @pl.when(pl.program_id(0) == 0)
        def _():
            s_ref[...] = jnp.zeros_like(s_ref)

        e = y_ref[...] - t_ref[...]
        s_ref[...] += jnp.sum(jnp.sum(e * e, axis=1, keepdims=True), axis=0, keepdims=True)
        d_ref[...] = e * (1.0 / D)

    row = pl.BlockSpec((tr, D), lambda i: (i, 0))
    return pl.pallas_call(
        body, name="loss_head", grid=(T // tr,), in_specs=[row, row],
        out_specs=[pl.BlockSpec((1, 1), lambda i: (0, 0)), row],
        out_shape=[jax.ShapeDtypeStruct((1, 1), F32), jax.ShapeDtypeStruct((T, D), F32)],
        compiler_params=_params(("arbitrary",)),
    )(y, target)


def _ffn_up(h, wg, wu, name):
    (T, D), Fd = h.shape, wg.shape[1]
    tm, tn, tk = _tile(T, 1024), _tile(Fd, 512), _tile(D, 512)
    nk = D // tk

    def body(h_ref, wg_ref, wu_ref, g_ref, u_ref, a_ref, accg, accu):
        k = pl.program_id(2)

        @pl.when(k == 0)
        def _():
            accg[...] = jnp.zeros_like(accg)
            accu[...] = jnp.zeros_like(accu)

        hv = h_ref[...]
        accg[...] += _dot(hv, wg_ref[...])
        accu[...] += _dot(hv, wu_ref[...])

        @pl.when(k == nk - 1)
        def _():
            g, u = accg[...], accu[...]
            g_ref[...] = g
            u_ref[...] = u
            a_ref[...] = (g * jax.nn.sigmoid(g) * u).astype(BF16)

    w_spec = pl.BlockSpec((tk, tn), lambda i, j, k: (k, j))
    o_spec = pl.BlockSpec((tm, tn), lambda i, j, k: (i, j))
    return pl.pallas_call(
        body, name=name, grid=(T // tm, Fd // tn, nk),
        in_specs=[pl.BlockSpec((tm, tk), lambda i, j, k: (i, k)), w_spec, w_spec],
        out_specs=[o_spec, o_spec, o_spec],
        out_shape=[jax.ShapeDtypeStruct((T, Fd), F32), jax.ShapeDtypeStruct((T, Fd), F32),
                   jax.ShapeDtypeStruct((T, Fd), BF16)],
        scratch_shapes=[pltpu.VMEM((tm, tn), F32), pltpu.VMEM((tm, tn), F32)],
        compiler_params=_params(("parallel", "parallel", "arbitrary")),
    )(h, wg, wu)


def _ffn_dact(dy, wd, gate, up, name):
    (T, D), Fd = dy.shape, wd.shape[0]
    tm, tn, tk = _tile(T, 1024), _tile(Fd, 512), _tile(D, 512)
    nk = D // tk

    def body(dy_ref, wd_ref, g_ref, u_ref, dg_ref, du_ref, acc):
        k = pl.program_id(2)

        @pl.when(k == 0)
        def _():
            acc[...] = jnp.zeros_like(acc)

        acc[...] += _dot(dy_ref[...].astype(BF16), wd_ref[...], _NT)

        @pl.when(k == nk - 1)
        def _():
            da, g, u = acc[...], g_ref[...], u_ref[...]
            sg = jax.nn.sigmoid(g)
            du_ref[...] = (da * g * sg).astype(BF16)
            dg_ref[...] = (da * u * sg * (1.0 + g * (1.0 - sg))).astype(BF16)

    o_spec = pl.BlockSpec((tm, tn), lambda i, j, k: (i, j))
    return pl.pallas_call(
        body, name=name, grid=(T // tm, Fd // tn, nk),
        in_specs=[pl.BlockSpec((tm, tk), lambda i, j, k: (i, k)), pl.BlockSpec((tn, tk), lambda i, j, k: (j, k)),
                  o_spec, o_spec],
        out_specs=[o_spec, o_spec],
        out_shape=[jax.ShapeDtypeStruct((T, Fd), BF16), jax.ShapeDtypeStruct((T, Fd), BF16)],
        scratch_shapes=[pltpu.VMEM((tm, tn), F32)],
        compiler_params=_params(("parallel", "parallel", "arbitrary")),
    )(dy, wd, gate, up)


def _is_a(shape):
    return lax.broadcasted_iota(jnp.int32, shape, 1) % LANES < HEAD


def _split2(v):
    hi = v.astype(BF16)
    return hi, (v - hi.astype(F32)).astype(BF16)


def _split3(v):
    hi = v.astype(BF16)
    r = v - hi.astype(F32)
    mid = r.astype(BF16)
    return hi, mid, (r - mid.astype(F32)).astype(BF16)


def _dot_split(v, m, pieces, dims=_NN):
    parts = _split3(v) if pieces == 3 else _split2(v)
    out = _dot(parts[0], m, dims)
    for p in parts[1:]:
        out = out + _dot(p, m, dims)
    return out


def _head_blockdiag():
    r = lax.broadcasted_iota(jnp.int32, (LANES, LANES), 0) // HEAD
    c = lax.broadcasted_iota(jnp.int32, (LANES, LANES), 1) // HEAD
    return (r == c).astype(BF16)


def _swap_half(v):
    lane = lax.broadcasted_iota(jnp.int32, v.shape, 1)
    return jnp.where(lane % HEAD < HEAD // 2, pltpu.roll(v, LANES - HEAD // 2, axis=1), pltpu.roll(v, HEAD // 2, axis=1))


def _attn_prep_fwd(qkv, cos, sin, gq, gk, D, HKV):
    T, QW = qkv.shape
    tr = 256
    nq, nk = D // LANES, HKV // 2
    KW = HKV * LANES

    def body(x_ref, cos_ref, sin_ref, gq_ref, gk_ref, q_ref, k_ref, v_ref):
        bd = _head_blockdiag()
        cs, sn = cos_ref[...], sin_ref[...]
        isa = _is_a((tr, LANES))

        def normrope(xv, g):
            ms = _dot_split(xv * xv, bd, 2) * (1.0 / HEAD)
            xn = xv * lax.rsqrt(ms + EPS) * g
            return xn * cs + _swap_half(xn) * sn

        def dup(v):
            r = pltpu.roll(v, HEAD, axis=1)
            return jnp.where(isa, v, r), jnp.where(isa, r, v)

        for s in range(nq):
            sl = slice(s * LANES, (s + 1) * LANES)
            q_ref[:, sl] = normrope(x_ref[:, sl], gq_ref[...]).astype(BF16)
        for s in range(nk):
            ka, kb = dup(normrope(x_ref[:, D + s * LANES:D + (s + 1) * LANES], gk_ref[...]))
            k_ref[:, 2 * s * LANES:(2 * s + 1) * LANES] = ka.astype(BF16)
            k_ref[:, (2 * s + 1) * LANES:(2 * s + 2) * LANES] = kb.astype(BF16)
            va, vb = dup(x_ref[:, D + (nk + s) * LANES:D + (nk + s + 1) * LANES])
            v_ref[:, 2 * s * LANES:(2 * s + 1) * LANES] = va.astype(BF16)
            v_ref[:, (2 * s + 1) * LANES:(2 * s + 2) * LANES] = vb.astype(BF16)

    tab = pl.BlockSpec((tr, LANES), lambda i: (i, 0))
    vec = pl.BlockSpec((1, LANES), lambda i: (0, 0))
    return pl.pallas_call(
        body, name="attn_prep_fwd", grid=(T // tr,),
        in_specs=[pl.BlockSpec((tr, QW), lambda i: (i, 0)), tab, tab, vec, vec],
        out_specs=[pl.BlockSpec((tr, D), lambda i: (i, 0)), pl.BlockSpec((tr, KW), lambda i: (i, 0)),
                   pl.BlockSpec((tr, KW), lambda i: (i, 0))],
        out_shape=[jax.ShapeDtypeStruct((T, D), BF16), jax.ShapeDtypeStruct((T, KW), BF16),
                   jax.ShapeDtypeStruct((T, KW), BF16)],
        compiler_params=_params(("parallel",)),
    )(qkv, cos, sin, gq, gk)


def _attn_prep_bwd(qkv, dq, dkd, dvd, cos, sin, gq, gk, D, HKV):
    T, QW = qkv.shape
    tr = 256
    nq, nk = D // LANES, HKV // 2
    KW = HKV * LANES

    def body(x_ref, dq_ref, dk_ref, dv_ref, cos_ref, sin_ref, gq_ref, gk_ref, o_ref, dgq_ref, dgk_ref):
        @pl.when(pl.program_id(0) == 0)
        def _():
            dgq_ref[...] = jnp.zeros_like(dgq_ref)
            dgk_ref[...] = jnp.zeros_like(dgk_ref)

        bd = _head_blockdiag()
        cs, sn = cos_ref[...], sin_ref[...]
        isa = _is_a((tr, LANES))

        def back(xv, dy, g):
            rstd = lax.rsqrt(_dot_split(xv * xv, bd, 2) * (1.0 / HEAD) + EPS)
            xhat = xv * rstd
            dxn = dy * cs + _swap_half(dy * sn)
            dxh = dxn * g
            mean = _dot_split(dxh * xhat, bd, 2) * (1.0 / HEAD)
            return rstd * (dxh - xhat * mean), jnp.sum(dxn * xhat, axis=0, keepdims=True)

        def fold(s):
            a = dk_ref[:, 2 * s * LANES:(2 * s + 1) * LANES]
            b = dk_ref[:, (2 * s + 1) * LANES:(2 * s + 2) * LANES]
            return jnp.where(isa, a + pltpu.roll(a, HEAD, axis=1), b + pltpu.roll(b, HEAD, axis=1))

        def foldv(s):
            a = dv_ref[:, 2 * s * LANES:(2 * s + 1) * LANES]
            b = dv_ref[:, (2 * s + 1) * LANES:(2 * s + 2) * LANES]
            return jnp.where(isa, a + pltpu.roll(a, HEAD, axis=1), b + pltpu.roll(b, HEAD, axis=1))

        dgq = jnp.zeros((1, LANES), F32)
        for s in range(nq):
            sl = slice(s * LANES, (s + 1) * LANES)
            dx, dg = back(x_ref[:, sl], dq_ref[:, sl], gq_ref[...])
            o_ref[:, sl] = dx.astype(BF16)
            dgq = dgq + dg
        dgq_ref[...] += dgq
        dgk = jnp.zeros((1, LANES), F32)
        for s in range(nk):
            sl = slice(D + s * LANES, D + (s + 1) * LANES)
            dx, dg = back(x_ref[:, sl], fold(s), gk_ref[...])
            o_ref[:, sl] = dx.astype(BF16)
            dgk = dgk + dg
            o_ref[:, D + (nk + s) * LANES:D + (nk + s + 1) * LANES] = foldv(s).astype(BF16)
        dgk_ref[...] += dgk

    tab = pl.BlockSpec((tr, LANES), lambda i: (i, 0))
    vec = pl.BlockSpec((1, LANES), lambda i: (0, 0))
    kv = pl.BlockSpec((tr, KW), lambda i: (i, 0))
    return pl.pallas_call(
        body, name="attn_prep_bwd", grid=(T // tr,),
        in_specs=[pl.BlockSpec((tr, QW), lambda i: (i, 0)), pl.BlockSpec((tr, D), lambda i: (i, 0)), kv, kv,
                  tab, tab, vec, vec],
        out_specs=[pl.BlockSpec((tr, QW), lambda i: (i, 0)), vec, vec],
        out_shape=[jax.ShapeDtypeStruct((T, QW), BF16), jax.ShapeDtypeStruct((1, LANES), F32),
                   jax.ShapeDtypeStruct((1, LANES), F32)],
        compiler_params=_params(("arbitrary",)),
    )(qkv, dq, dkd, dvd, cos, sin, gq, gk)


def _attn_probs(qs, kw, sink_ref, n, scale):
    rows = qs.shape[0]
    qi = lax.broadcasted_iota(jnp.int32, (rows, 2 * WINDOW), 0) % WINDOW
    kj = lax.broadcasted_iota(jnp.int32, (rows, 2 * WINDOW), 1)
    valid = (kj > qi) & (kj <= qi + WINDOW) & jnp.logical_or(n > 0, kj >= WINDOW)
    isa = _is_a(kw.shape)
    out = []
    for pos in (0, 1):
        kp = jnp.where(isa if pos == 0 else ~isa, kw, jnp.zeros_like(kw))
        s = jnp.where(valid, _dot(qs, kp, _NT) * scale, -jnp.inf)
        sink = sink_ref[0, pos]
        m = jnp.maximum(jnp.max(s, axis=1, keepdims=True), sink)
        p = jnp.exp(s - m)
        ps = jnp.exp(sink - m)
        inv = 1.0 / (jnp.sum(p, axis=1, keepdims=True) + ps)
        out.append((p * inv, ps * inv, kp))
    return out


def _attn_specs(HKV):
    q = pl.BlockSpec((WINDOW, ATT_GW), lambda g, n: (n, g))
    cur = pl.BlockSpec((WINDOW, LANES), lambda g, n: (n, g))
    prev = pl.BlockSpec((WINDOW, LANES), lambda g, n: (jnp.maximum(n - 1, 0), g))
    sink = pl.BlockSpec((1, 2, ATT_GW, 1), lambda g, n: (g, 0, 0, 0))
    return q, cur, prev, sink


def _stack(ref):
    return jnp.concatenate([ref[:, i * LANES:(i + 1) * LANES] for i in range(ATT_GW // LANES)], axis=0)


def _attn_fwd(q, kd, vd, sinkcol, HKV):
    T, D = q.shape
    nb = T // WINDOW
    scale = HEAD ** -0.5

    def body(q_ref, kp_ref, kc_ref, vp_ref, vc_ref, sink_ref, o_ref):
        n = pl.program_id(1)
        qs = _stack(q_ref)
        kw = jnp.concatenate([kp_ref[...], kc_ref[...]], axis=0)
        vw = jnp.concatenate([vp_ref[...], vc_ref[...]], axis=0)
        isa = _is_a(vw.shape)
        o = jnp.zeros((ATT_GW, LANES), F32)
        for pos, (probs, _, _) in enumerate(_attn_probs(qs, kw, sink_ref, n, scale)):
            vp = jnp.where(isa if pos == 0 else ~isa, vw, jnp.zeros_like(vw))
            o = o + _dot(probs.astype(BF16), vp)
        for i in range(ATT_GW // LANES):
            o_ref[:, i * LANES:(i + 1) * LANES] = o[i * WINDOW:(i + 1) * WINDOW].astype(BF16)

    qs_, cur, prev, sink = _attn_specs(HKV)
    return pl.pallas_call(
        body, name="attn_fwd", grid=(HKV, nb), in_specs=[qs_, prev, cur, prev, cur, sink], out_specs=qs_,
        out_shape=jax.ShapeDtypeStruct((T, D), BF16),
        compiler_params=_params(("parallel", "parallel")),
    )(q, kd, kd, vd, vd, sinkcol)


def _attn_bwd(q, kd, vd, o, do, sinkcol, HKV):
    T, D = q.shape
    nb = T // WINDOW
    scale = HEAD ** -0.5
    KW = HKV * LANES

    def body(q_ref, kp_ref, kc_ref, vp_ref, vc_ref, o_ref, do_ref, sink_ref, dq_ref, dk_ref, dv_ref, ds_ref):
        n = pl.program_id(1)

        @pl.when(n == 0)
        def _():
            dk_ref[...] = jnp.zeros_like(dk_ref)
            dv_ref[...] = jnp.zeros_like(dv_ref)
            ds_ref[...] = jnp.zeros_like(ds_ref)

        qs = _stack(q_ref)
        dos = _stack(do_ref)
        os_ = _stack(o_ref).astype(F32)
        kw = jnp.concatenate([kp_ref[...], kc_ref[...]], axis=0)
        vw = jnp.concatenate([vp_ref[...], vc_ref[...]], axis=0)
        isa_w = _is_a(vw.shape)
        isa_q = _is_a(dos.shape)
        dd = dos * os_
        dob = dos.astype(BF16)
        dqs = jnp.zeros((ATT_GW, LANES), F32)
        dkw, dvw, dsk = [], [], []
        for pos, (probs, psink, kp) in enumerate(_attn_probs(qs, kw, sink_ref, n, scale)):
            sel_w = isa_w if pos == 0 else ~isa_w
            sel_q = isa_q if pos == 0 else ~isa_q
            delta = jnp.sum(jnp.where(sel_q, dd, 0.0), axis=1, keepdims=True)
            vp = jnp.where(sel_w, vw, jnp.zeros_like(vw))
            dp = _dot(dob, vp, _NT)
            dsb = (probs * (dp - delta) * scale).astype(BF16)
            dqs = dqs + _dot(dsb, kp)
            dkw.append(_dot(dsb, qs, _TN))
            dvw.append(_dot(probs.astype(BF16), dob, _TN))
            dsk.append(-psink * delta)
        dkw = jnp.where(isa_w, dkw[0], dkw[1])
        dvw = jnp.where(isa_w, dvw[0], dvw[1])

        @pl.when(n == 0)
        def _():
            dk_ref[0:WINDOW, :] += dkw[WINDOW:]
            dv_ref[0:WINDOW, :] += dvw[WINDOW:]

        @pl.when(n > 0)
        def _():
            start = pl.multiple_of((n - 1) * WINDOW, WINDOW)
            dk_ref[pl.ds(start, 2 * WINDOW), :] += dkw
            dv_ref[pl.ds(start, 2 * WINDOW), :] += dvw

        rows = []
        for i in range(ATT_GW // LANES):
            dq_ref[:, i * LANES:(i + 1) * LANES] = dqs[i * WINDOW:(i + 1) * WINDOW]
            for pos in (0, 1):
                s = jnp.sum(dsk[pos][i * WINDOW:(i + 1) * WINDOW], axis=0, keepdims=True)
                rows.append(jnp.broadcast_to(s, (1, LANES)))
        ds_ref[0] += jnp.concatenate(rows, axis=0)

    qs_, cur, prev, sink = _attn_specs(HKV)
    dqo = pl.BlockSpec((WINDOW, ATT_GW), lambda g, n: (n, g))
    dkv = pl.BlockSpec((T, LANES), lambda g, n: (0, g))
    return pl.pallas_call(
        body, name="attn_bwd", grid=(HKV, nb),
        in_specs=[qs_, prev, cur, prev, cur, qs_, dqo, sink],
        out_specs=[dqo, dkv, dkv, pl.BlockSpec((1, ATT_GROUP, LANES), lambda g, n: (g, 0, 0))],
        out_shape=[jax.ShapeDtypeStruct((T, D), F32), jax.ShapeDtypeStruct((T, KW), F32),
                   jax.ShapeDtypeStruct((T, KW), F32), jax.ShapeDtypeStruct((HKV, ATT_GROUP, LANES), F32)],
        compiler_params=_params(("parallel", "arbitrary")),
    )(q, kd, kd, vd, vd, o, do, sinkcol)


def _conv_fwd(zx, w, b, DI, CD):
    T = zx.shape[0]
    cw, tc = _tile(math.gcd(DI, CD), 512), 512
    off = DI // cw

    def body(cur_ref, halo_ref, w_ref, b_ref, o_ref):
        i = pl.program_id(1)
        halo = jnp.where(i > 0, halo_ref[...], 0.0)
        ext = jnp.concatenate([halo, cur_ref[...]], axis=0)
        acc = b_ref[...] + w_ref[SSM_CONV - 1:SSM_CONV, :] * ext[HALO:]
        for k in range(SSM_CONV - 1):
            acc = acc + w_ref[k:k + 1, :] * pltpu.roll(ext, SSM_CONV - 1 - k, axis=0)[HALO:]
        o_ref[...] = acc * jax.nn.sigmoid(acc)

    return pl.pallas_call(
        body, name="ssm_conv_fwd", grid=(CD // cw, T // tc),
        in_specs=[pl.BlockSpec((tc, cw), lambda j, i: (i, off + j)),
                  pl.BlockSpec((HALO, cw), lambda j, i: (jnp.maximum(i * (tc // HALO) - 1, 0), off + j)),
                  pl.BlockSpec((SSM_CONV, cw), lambda j, i: (0, j)), pl.BlockSpec((1, cw), lambda j, i: (0, j))],
        out_specs=pl.BlockSpec((tc, cw), lambda j, i: (i, j)),
        out_shape=jax.ShapeDtypeStruct((T, CD), F32),
        compiler_params=_params(("parallel", "parallel")),
    )(zx, zx, w, b)


def _conv_bwd(zx, dxc, w, b, DI, CD):
    T = zx.shape[0]
    cw, tc = _tile(math.gcd(DI, CD), 512), 512
    off = DI // cw
    nt = T // tc
    hb = tc // HALO

    def body(cur_ref, prev_ref, next_ref, d_ref, dnext_ref, w_ref, b_ref, o_ref, dw_ref, db_ref):
        i = pl.program_id(1)

        @pl.when(i == 0)
        def _():
            dw_ref[...] = jnp.zeros_like(dw_ref)
            db_ref[...] = jnp.zeros_like(db_ref)

        prev = jnp.where(i > 0, prev_ref[...], 0.0)
        ext = jnp.concatenate([prev, cur_ref[...], next_ref[...]], axis=0)
        u = b_ref[...] + w_ref[SSM_CONV - 1:SSM_CONV, :] * ext
        for k in range(SSM_CONV - 1):
            u = u + w_ref[k:k + 1, :] * pltpu.roll(ext, SSM_CONV - 1 - k, axis=0)
        u = u[HALO:]
        dnext = jnp.where(i < nt - 1, dnext_ref[...], 0.0)
        dxe = jnp.concatenate([d_ref[...], dnext], axis=0)
        sg = jax.nn.sigmoid(u)
        du = dxe * sg * (1.0 + u * (1.0 - sg))
        n_e = tc + HALO
        dx = w_ref[SSM_CONV - 1:SSM_CONV, :] * du
        for k in range(SSM_CONV - 1):
            dx = dx + w_ref[k:k + 1, :] * pltpu.roll(du, n_e - (SSM_CONV - 1 - k), axis=0)
        o_ref[...] = dx[:tc].astype(BF16)
        duc = du[:tc]
        db_ref[...] += jnp.sum(duc, axis=0, keepdims=True)
        xs = ext[:n_e]
        dws = []
        for k in range(SSM_CONV):
            sh = xs if k == SSM_CONV - 1 else pltpu.roll(xs, SSM_CONV - 1 - k, axis=0)
            dws.append(jnp.sum(duc * sh[HALO:], axis=0, keepdims=True))
        dw_ref[...] += jnp.concatenate(dws, axis=0)

    return pl.pallas_call(
        body, name="ssm_conv_bwd", grid=(CD // cw, nt),
        in_specs=[pl.BlockSpec((tc, cw), lambda j, i: (i, off + j)),
                  pl.BlockSpec((HALO, cw), lambda j, i: (jnp.maximum(i * hb - 1, 0), off + j)),
                  pl.BlockSpec((HALO, cw), lambda j, i: (jnp.minimum((i + 1) * hb, nt * hb - 1), off + j)),
                  pl.BlockSpec((tc, cw), lambda j, i: (i, j)),
                  pl.BlockSpec((HALO, cw), lambda j, i: (jnp.minimum((i + 1) * hb, nt * hb - 1), j)),
                  pl.BlockSpec((SSM_CONV, cw), lambda j, i: (0, j)), pl.BlockSpec((1, cw), lambda j, i: (0, j))],
        out_specs=[pl.BlockSpec((tc, cw), lambda j, i: (i, j)), pl.BlockSpec((SSM_CONV, cw), lambda j, i: (0, j)),
                   pl.BlockSpec((1, cw), lambda j, i: (0, j))],
        out_shape=[jax.ShapeDtypeStruct((T, CD), BF16), jax.ShapeDtypeStruct((SSM_CONV, CD), F32),
                   jax.ShapeDtypeStruct((1, CD), F32)],
        compiler_params=_params(("parallel", "arbitrary")),
    )(zx, zx, zx, dxc, dxc, w, b)


def _tri_dot(v, upper):
    L = v.shape[0]
    r = lax.broadcasted_iota(jnp.int32, (L, L), 0)
    c = lax.broadcasted_iota(jnp.int32, (L, L), 1)
    tri = ((r <= c) if upper else (r >= c)).astype(BF16)
    p = _split3(v)
    return _dot(tri, p[0]) + _dot(tri, p[1]) + _dot(tri, p[2])


def _ssd_time2(dtraw_ref, bias_ref, alog_ref, sel):
    dt = jax.nn.softplus(dtraw_ref[...] + bias_ref[...])
    acum = _tri_dot(dt * (-jnp.exp(alog_ref[...])), False)
    return dt, _dot_split(dt, sel, 3), _dot_split(acum, sel, 3)


def _decay(acs, acs_t, pos, transposed):
    L = acs.shape[0]
    r = lax.broadcasted_iota(jnp.int32, (L, L), 0)
    c = lax.broadcasted_iota(jnp.int32, (L, L), 1)
    col = acs[:, HEAD * pos:HEAD * pos + 1]
    row = acs_t[HEAD * pos:HEAD * pos + 1, :]
    if transposed:
        return jnp.exp(jnp.where(r <= c, row - col, -jnp.inf))
    return jnp.exp(jnp.where(r >= c, col - row, -jnp.inf))


def _ssd_specs(G, GW, DI, ZW):
    L = SSM_CHUNK
    grp = lambda f: pl.BlockSpec((L, GW), lambda g, c: (f(c), g))
    return dict(
        grp=grp,
        bmat=lambda f: pl.BlockSpec((L, SSM_STATE), lambda g, c: (f(c), DI // SSM_STATE + g)),
        cmat=lambda f: pl.BlockSpec((L, SSM_STATE), lambda g, c: (f(c), DI // SSM_STATE + G + g)),
        dtraw=lambda f: pl.BlockSpec((L, LANES), lambda g, c: (f(c), ZW // LANES - 1)),
        vec=pl.BlockSpec((1, LANES), lambda g, c: (0, 0)),
        gvec=pl.BlockSpec((1, GW), lambda g, c: (0, g)),
        sel=pl.BlockSpec((1, LANES, GW), lambda g, c: (g, 0, 0)),
        selt=pl.BlockSpec((1, GW, LANES), lambda g, c: (g, 0, 0)),
    )


def _ssd_fwd(zx, xc, bias, alog, sel, dskip, ng, DI):
    T, ZW = zx.shape
    G, L = SSM_GROUPS, SSM_CHUNK
    GW = DI // G
    NS = GW // LANES
    nc = T // L
    sp = _ssd_specs(G, GW, DI, ZW)
    ident = lambda c: c

    def body(x_ref, b_ref, c_ref, z_ref, dtraw_ref, bias_ref, alog_ref, sel_ref, d_ref, ng_ref,
             y_ref, yo_ref, st_ref, state):
        c = pl.program_id(1)

        @pl.when(c == 0)
        def _():
            state[...] = jnp.zeros_like(state)

        x = x_ref[...]
        bb, cb_ = b_ref[...].astype(BF16), c_ref[...].astype(BF16)
        cbm = _dot(cb_, bb, _NT)
        _, dtx, acx = _ssd_time2(dtraw_ref, bias_ref, alog_ref, sel_ref[0])
        xdt = x * dtx
        ex = jnp.exp(acx)
        last = acx[L - 1:L, :]
        te = jnp.exp(last - acx)
        dlast = jnp.exp(last)
        isa = _is_a((L, LANES))
        for i in range(NS):
            sl = slice(i * LANES, (i + 1) * LANES)
            acs = acx[:, sl]
            acs_t = acs.T
            xs = xdt[:, sl]
            y = jnp.zeros((L, LANES), F32)
            for pos in (0, 1):
                m = (cbm * _decay(acs, acs_t, pos, False)).astype(BF16)
                y = y + _dot(m, jnp.where(isa if pos == 0 else ~isa, xs, 0.0).astype(BF16))
            st = state[i]
            st_ref[0, i] = st
            y = y + _dot(cb_, st.astype(BF16)) * ex[:, sl]
            state[i] = st * dlast[:, sl] + _dot(bb, (xs * te[:, sl]).astype(BF16), _TN)
            y_ref[:, sl] = y + d_ref[:, sl] * x[:, sl]
        z = z_ref[...]
        gated = y_ref[...] * (z * jax.nn.sigmoid(z))
        rstd = lax.rsqrt(jnp.mean(gated * gated, axis=1, keepdims=True) + EPS)
        yo_ref[...] = (gated * rstd * ng_ref[...]).astype(BF16)

    return pl.pallas_call(
        body, name="ssd_fwd", grid=(G, nc),
        in_specs=[sp["grp"](ident), sp["bmat"](ident), sp["cmat"](ident), sp["grp"](ident), sp["dtraw"](ident),
                  sp["vec"], sp["vec"], sp["sel"], sp["gvec"], sp["gvec"]],
        out_specs=[sp["grp"](ident), sp["grp"](ident),
                   pl.BlockSpec((1, NS, SSM_STATE, LANES), lambda g, c: (c, g, 0, 0))],
        out_shape=[jax.ShapeDtypeStruct((T, DI), F32), jax.ShapeDtypeStruct((T, DI), BF16),
                   jax.ShapeDtypeStruct((nc, G * NS, SSM_STATE, LANES), F32)],
        scratch_shapes=[pltpu.VMEM((NS, SSM_STATE, LANES), F32)],
        compiler_params=_params(("parallel", "arbitrary")),
    )(xc, xc, xc, zx, zx, bias, alog, sel, dskip, ng)


def _ssd_bwd(zx, xc, yssd, dyo, states, bias, alog, sel, selt, hsel, dskip, ng, DI):
    T, ZW = zx.shape
    G, L = SSM_GROUPS, SSM_CHUNK
    GW = DI // G
    NS = GW // LANES
    nc = T // L
    sp = _ssd_specs(G, GW, DI, ZW)
    rev = lambda c: nc - 1 - c

    def body(x_ref, b_ref, c_ref, z_ref, dtraw_ref, y_ref, dyo_ref, st_ref, bias_ref, alog_ref, sel_ref, selt_ref,
             hsel_ref, d_ref, ng_ref, dz_ref, dx_ref, db_ref, dc_ref, ddt_ref, dac_ref, dd_ref, dng_ref, dstate):
        c = pl.program_id(1)

        @pl.when(c == 0)
        def _():
            dstate[...] = jnp.zeros_like(dstate)
            dd_ref[...] = jnp.zeros_like(dd_ref)
            dng_ref[...] = jnp.zeros_like(dng_ref)

        z, ys, dyo = z_ref[...], y_ref[...], dyo_ref[...]
        sg = jax.nn.sigmoid(z)
        sz = z * sg
        gated = ys * sz
        rstd = lax.rsqrt(jnp.mean(gated * gated, axis=1, keepdims=True) + EPS)
        yn = gated * rstd
        dng_ref[0] += jnp.sum(dyo * yn, axis=0, keepdims=True)
        dyn = dyo * ng_ref[...]
        dgated = rstd * (dyn - yn * jnp.mean(dyn * yn, axis=1, keepdims=True))
        g = dgated * sz
        dz_ref[...] = (dgated * ys * sg * (1.0 + z * (1.0 - sg))).astype(BF16)

        x = x_ref[...]
        dsk = d_ref[...]
        dd_ref[0] += jnp.sum(g * x, axis=0, keepdims=True)
        bb, cb_ = b_ref[...].astype(BF16), c_ref[...].astype(BF16)
        cbm = _dot(cb_, bb, _NT)
        cbt = _dot(bb, cb_, _NT)
        _, dtx, acx = _ssd_time2(dtraw_ref, bias_ref, alog_ref, sel_ref[0])
        xdt = x * dtx
        ex = jnp.exp(acx)
        last = acx[L - 1:L, :]
        te = jnp.exp(last - acx)
        dlast = jnp.exp(last)
        isa = _is_a((L, LANES))
        is_last = lax.broadcasted_iota(jnp.int32, (L, LANES), 0) == L - 1
        strict = lax.broadcasted_iota(jnp.int32, (L, L), 0) > lax.broadcasted_iota(jnp.int32, (L, L), 1)
        dcb = jnp.zeros((L, L), F32)
        dcm = jnp.zeros((L, SSM_STATE), F32)
        dbm = jnp.zeros((L, SSM_STATE), F32)
        for i in range(NS):
            sl = slice(i * LANES, (i + 1) * LANES)
            acs = acx[:, sl]
            acs_t = acs.T
            xs, gs = xdt[:, sl], g[:, sl]
            xsb = xs.astype(BF16)
            dxd = jnp.zeros((L, LANES), F32)
            dac_c = jnp.zeros((L, LANES), F32)
            for pos in (0, 1):
                gp = jnp.where(isa if pos == 0 else ~isa, gs, 0.0).astype(BF16)
                mt = (cbt * _decay(acs, acs_t, pos, True)).astype(BF16)
                dxd = dxd + _dot(mt, gp)
                dmd = _dot(gp, xsb, _NT) * _decay(acs, acs_t, pos, False)
                dcb = dcb + dmd
                q = jnp.where(strict, dmd * cbm, 0.0)
                hot = jnp.broadcast_to(hsel_ref[0, 2 * i + pos:2 * i + pos + 1, :], (L, LANES)).astype(BF16)
                dac_c = dac_c + _dot_split(q, hot, 2) - _dot_split(q, hot, 2, _TN)
            st = st_ref[0, i]
            dst = dstate[i]
            stb, dstb = st.astype(BF16), dst.astype(BF16)
            eg = (ex[:, sl] * gs).astype(BF16)
            dcm = dcm + _dot(eg, stb, _NT)
            yoff = _dot(cb_, stb) * ex[:, sl]
            w = xs * te[:, sl]
            wb = w.astype(BF16)
            dw = _dot(bb, dstb)
            dbm = dbm + _dot(wb, dstb, _NT)
            dxt = dxd + dw * te[:, sl]
            dal = dlast[:, sl] * jnp.sum(dst * st, axis=0, keepdims=True) + jnp.sum(dw * w, axis=0, keepdims=True)
            dac_l = gs * yoff - w * dw + jnp.where(is_last, dal, 0.0)
            ddt_l = dxt * x[:, sl]
            dstate[i] = dst * dlast[:, sl] + _dot(cb_, eg, _TN)
            dx_ref[:, sl] = dxt * dtx[:, sl] + dsk[:, sl] * gs
            part = _dot_split(ddt_l, selt_ref[0, sl, :], 2)
            parta = dac_c + _dot_split(dac_l, selt_ref[0, sl, :], 2)
            if i == 0:
                ddt_ref[0] = part
                dac_ref[0] = parta
            else:
                ddt_ref[0] += part
                dac_ref[0] += parta
        dcbb = dcb.astype(BF16)
        dc_ref[...] = dcm + _dot(dcbb, bb)
        db_ref[...] = dbm + _dot(dcbb, cb_, _TN)

    part_spec = pl.BlockSpec((1, L, LANES), lambda g, c: (g, rev(c), 0))
    lane_spec = pl.BlockSpec((1, 1, GW), lambda g, c: (g, 0, 0))
    bc_out = pl.BlockSpec((L, SSM_STATE), lambda g, c: (rev(c), g))
    return pl.pallas_call(
        body, name="ssd_bwd", grid=(G, nc),
        in_specs=[sp["grp"](rev), sp["bmat"](rev), sp["cmat"](rev), sp["grp"](rev), sp["dtraw"](rev), sp["grp"](rev),
                  sp["grp"](rev), pl.BlockSpec((1, NS, SSM_STATE, LANES), lambda g, c: (rev(c), g, 0, 0)),
                  sp["vec"], sp["vec"], sp["sel"], sp["selt"], pl.BlockSpec((1, 8, LANES), lambda g, c: (g, 0, 0)),
                  sp["gvec"], sp["gvec"]],
        out_specs=[sp["grp"](rev), sp["grp"](rev), bc_out, bc_out, part_spec, part_spec, lane_spec, lane_spec],
        out_shape=[jax.ShapeDtypeStruct((T, DI), BF16), jax.ShapeDtypeStruct((T, DI), F32),
                   jax.ShapeDtypeStruct((T, G * SSM_STATE), F32), jax.ShapeDtypeStruct((T, G * SSM_STATE), F32),
                   jax.ShapeDtypeStruct((G, T, LANES), F32), jax.ShapeDtypeStruct((G, T, LANES), F32),
                   jax.ShapeDtypeStruct((G, 1, GW), F32), jax.ShapeDtypeStruct((G, 1, GW), F32)],
        scratch_shapes=[pltpu.VMEM((NS, SSM_STATE, LANES), F32)],
        compiler_params=_params(("parallel", "arbitrary")),
    )(xc, xc, xc, zx, zx, yssd, dyo, states, bias, alog, sel, selt, hsel, dskip, ng)


def _ssd_dt_bwd(zx, ddt_part, dac_part, bias, alog):
    T, ZW = zx.shape
    G, L = SSM_GROUPS, SSM_CHUNK
    nc = T // L

    def body(dtraw_ref, ddt_ref, dac_ref, bias_ref, alog_ref, o_ref, dal_ref, dbias_ref):
        @pl.when(pl.program_id(0) == 0)
        def _():
            dal_ref[...] = jnp.zeros_like(dal_ref)
            dbias_ref[...] = jnp.zeros_like(dbias_ref)

        raw = dtraw_ref[...] + bias_ref[...]
        dt = jax.nn.softplus(raw)
        a = -jnp.exp(alog_ref[...])
        dac, ddt = dac_ref[0], ddt_ref[0]
        for gi in range(1, G):
            dac = dac + dac_ref[gi]
            ddt = ddt + ddt_ref[gi]
        dda = _tri_dot(dac, True)
        dal_ref[...] += jnp.sum(dda * dt, axis=0, keepdims=True) * a
        draw = (dda * a + ddt) * jax.nn.sigmoid(raw)
        dbias_ref[...] += jnp.sum(draw, axis=0, keepdims=True)
        o_ref[...] = draw.astype(BF16)

    vec = pl.BlockSpec((1, LANES), lambda c: (0, 0))
    part = pl.BlockSpec((G, L, LANES), lambda c: (0, c, 0))
    return pl.pallas_call(
        body, name="ssd_dt_bwd", grid=(nc,),
        in_specs=[pl.BlockSpec((L, LANES), lambda c: (c, ZW // LANES - 1)), part, part, vec, vec],
        out_specs=[pl.BlockSpec((L, LANES), lambda c: (c, 0)), vec, vec],
        out_shape=[jax.ShapeDtypeStruct((T, LANES), BF16), jax.ShapeDtypeStruct((1, LANES), F32),
                   jax.ShapeDtypeStruct((1, LANES), F32)],
        compiler_params=_params(("arbitrary",)),
    )(zx, ddt_part, dac_part, bias, alog)


def _slot(px, py, pc):
    return 4 * px + 2 * py + pc


def _all_gather(shards, name):
    n = len(shards)

    def body(*refs):
        ins, outs = refs[:n], refs[n:2 * n]
        send_sems, recv_sems, local_sems = refs[2 * n:]
        x, y, c = lax.axis_index("x"), lax.axis_index("y"), lax.axis_index("c")
        me, sibling = (x, y, c), (x, y, 1 - c)
        chips = [(1 - x, y), (x, 1 - y), (1 - x, 1 - y)]

        def copy(w, k, block, to, src=None):
            dst = outs[w].at[_slot(*block)]
            return pltpu.make_async_remote_copy(
                src_ref=dst if src is None else src, dst_ref=dst, send_sem=send_sems.at[w, k],
                recv_sem=recv_sems.at[w, k], device_id=to, device_id_type=MESH)

        mine, first, passed = [], [], []
        for w in range(n):
            cp = pltpu.make_async_copy(ins[w], outs[w].at[_slot(*me)], local_sems.at[w])
            cp.start()
            mine.append(cp)
            fw = [copy(w, 0, me, sibling, src=ins[w])]
            fw += [copy(w, 1 + j, me, (*chip, c), src=ins[w]) for j, chip in enumerate(chips)]
            for cp in fw:
                cp.start()
            first += fw
        for w in range(n):
            for j, chip in enumerate(chips):
                copy(w, 1 + j, (*chip, c), me).wait_recv()
                cp = copy(w, 4 + j, (*chip, c), sibling)
                cp.start()
                passed.append(cp)
        for w in range(n):
            copy(w, 0, sibling, me).wait_recv()
            for j, chip in enumerate(chips):
                copy(w, 4 + j, (*chip, 1 - c), me).wait_recv()
        for cp in first + passed:
            cp.wait_send()
        for cp in mine:
            cp.wait()

    any_spec = pl.BlockSpec(memory_space=pl.ANY)
    return pl.pallas_call(
        body, name=name, in_specs=[any_spec] * n, out_specs=[any_spec] * n,
        out_shape=[jax.ShapeDtypeStruct((N_DEV,) + s.shape, s.dtype) for s in shards],
        scratch_shapes=[pltpu.SemaphoreType.DMA((n, 7)), pltpu.SemaphoreType.DMA((n, 7)),
                        pltpu.SemaphoreType.DMA((n,))],
    )(*shards)


def _exchange(blocks, name):
    n = len(blocks)

    def body(*refs):
        ins, outs = refs[:n], refs[n:2 * n]
        send_sems, recv_sems, local_sems = refs[2 * n:]
        x, y, c = lax.axis_index("x"), lax.axis_index("y"), lax.axis_index("c")
        me = _slot(x, y, c)
        peers = [(x ^ (m >> 2), y ^ ((m >> 1) & 1), c ^ (m & 1)) for m in range(1, N_DEV)]
        copies, mine = [], []
        for w in range(n):
            cp = pltpu.make_async_copy(ins[w].at[me], outs[w].at[me], local_sems.at[w])
            cp.start()
            mine.append(cp)
            for k, peer in enumerate(peers):
                cp = pltpu.make_async_remote_copy(
                    src_ref=ins[w].at[_slot(*peer)], dst_ref=outs[w].at[me], send_sem=send_sems.at[w, k],
                    recv_sem=recv_sems.at[w, k], device_id=peer, device_id_type=MESH)
                cp.start()
                copies.append((cp, w, k, peer))
        for cp, w, k, peer in copies:
            pltpu.make_async_remote_copy(
                src_ref=ins[w].at[me], dst_ref=outs[w].at[_slot(*peer)], send_sem=send_sems.at[w, k],
                recv_sem=recv_sems.at[w, k], device_id=peer, device_id_type=MESH).wait_recv()
        for cp, w, k, peer in copies:
            cp.wait_send()
        for cp in mine:
            cp.wait()

    any_spec = pl.BlockSpec(memory_space=pl.ANY)
    return pl.pallas_call(
        body, name=name, in_specs=[any_spec] * n, out_specs=[any_spec] * n,
        out_shape=[jax.ShapeDtypeStruct(b.shape, b.dtype) for b in blocks],
        scratch_shapes=[pltpu.SemaphoreType.DMA((n, 7)), pltpu.SemaphoreType.DMA((n, 7)),
                        pltpu.SemaphoreType.DMA((n,))],
    )(*blocks)


def _adamw(parts, w, m, v, name):
    R, C = w.shape
    per_row = C * (N_DEV * parts.dtype.itemsize + 7 * 4) * 2
    tr = R
    if R % 8 == 0:
        tr = 8
        for t in (16, 32, 64, 128, 256, 512):
            if R % t == 0 and t * per_row <= 24 * 1024 * 1024:
                tr = t
    c1 = 1.0 - ADAM_B1 ** ADAM_STEP
    c2 = 1.0 - ADAM_B2 ** ADAM_STEP

    def body(p_ref, w_ref, m_ref, v_ref, g_ref, d_ref, nm_ref, nv_ref):
        g = p_ref[0].astype(F32)
        for k in range(1, N_DEV):
            g = g + p_ref[k].astype(F32)
        nm = ADAM_B1 * m_ref[...] + (1.0 - ADAM_B1) * g
        nv = ADAM_B2 * v_ref[...] + (1.0 - ADAM_B2) * (g * g)
        g_ref[...] = g
        nm_ref[...] = nm
        nv_ref[...] = nv
        d_ref[...] = -ADAM_LR * ((nm / c1) / (jnp.sqrt(nv / c2) + ADAM_EPS) + ADAM_WD * w_ref[...])

    blk = pl.BlockSpec((tr, C), lambda i: (i, 0))
    out = jax.ShapeDtypeStruct((R, C), F32)
    return pl.pallas_call(
        body, name=name, grid=(R // tr,),
        in_specs=[pl.BlockSpec((N_DEV, tr, C), lambda i: (0, i, 0)), blk, blk, blk],
        out_specs=[blk, blk, blk, blk], out_shape=[out, out, out, out],
        compiler_params=_params(("parallel",)),
    )(parts, w, m, v)


def _pad_cols(a, n):
    return jnp.pad(a, ((0, 0), (0, n - a.shape[1])))


def kernel(x, positions, mixer_norm, ffn_norm, attn_w_qkv, attn_q_norm, attn_k_norm, attn_sinks, attn_w_o, ssm_w_in, ssm_conv_w, ssm_conv_b, ssm_dt_bias, ssm_a_log, ssm_d, ssm_norm, ssm_w_out, ffn_w_gate, ffn_w_up, ffn_w_down, loss_target, m_mixer_norm, m_ffn_norm, m_attn_w_qkv, m_attn_q_norm, m_attn_k_norm, m_attn_sinks, m_attn_w_o, m_ssm_w_in, m_ssm_conv_w, m_ssm_conv_b, m_ssm_dt_bias, m_ssm_a_log, m_ssm_d, m_ssm_norm, m_ssm_w_out, m_ffn_w_gate, m_ffn_w_up, m_ffn_w_down, v_mixer_norm, v_ffn_norm, v_attn_w_qkv, v_attn_q_norm, v_attn_k_norm, v_attn_sinks, v_attn_w_o, v_ssm_w_in, v_ssm_conv_w, v_ssm_conv_b, v_ssm_dt_bias, v_ssm_a_log, v_ssm_d, v_ssm_norm, v_ssm_w_out, v_ffn_w_gate, v_ffn_w_up, v_ffn_w_down):
    T, D = x.shape[1], x.shape[2]
    HQ = D // HEAD
    HKV = HQ // ATT_GROUP
    QW = (HQ + 2 * HKV) * HEAD
    DI = 2 * D
    H = DI // HEAD
    G = SSM_GROUPS
    GW = DI // G
    CD = DI + 2 * G * SSM_STATE
    ZW = DI + CD + LANES
    FF = ffn_w_gate.shape[2] * N_DEV
    IW = DI + CD + H
    assert T % 512 == 0 and D % 256 == 0 and HKV % 2 == 0 and GW % LANES == 0 and H <= LANES

    weights = dict(mixer_norm=mixer_norm, ffn_norm=ffn_norm, attn_w_qkv=attn_w_qkv, attn_q_norm=attn_q_norm,
                   attn_k_norm=attn_k_norm, attn_sinks=attn_sinks, attn_w_o=attn_w_o, ssm_w_in=ssm_w_in,
                   ssm_conv_w=ssm_conv_w, ssm_conv_b=ssm_conv_b, ssm_dt_bias=ssm_dt_bias, ssm_a_log=ssm_a_log,
                   ssm_d=ssm_d, ssm_norm=ssm_norm, ssm_w_out=ssm_w_out, ffn_w_gate=ffn_w_gate, ffn_w_up=ffn_w_up,
                   ffn_w_down=ffn_w_down)
    mom_m = dict(mixer_norm=m_mixer_norm, ffn_norm=m_ffn_norm, attn_w_qkv=m_attn_w_qkv, attn_q_norm=m_attn_q_norm,
                 attn_k_norm=m_attn_k_norm, attn_sinks=m_attn_sinks, attn_w_o=m_attn_w_o, ssm_w_in=m_ssm_w_in,
                 ssm_conv_w=m_ssm_conv_w, ssm_conv_b=m_ssm_conv_b, ssm_dt_bias=m_ssm_dt_bias, ssm_a_log=m_ssm_a_log,
                 ssm_d=m_ssm_d, ssm_norm=m_ssm_norm, ssm_w_out=m_ssm_w_out, ffn_w_gate=m_ffn_w_gate,
                 ffn_w_up=m_ffn_w_up, ffn_w_down=m_ffn_w_down)
    mom_v = dict(mixer_norm=v_mixer_norm, ffn_norm=v_ffn_norm, attn_w_qkv=v_attn_w_qkv, attn_q_norm=v_attn_q_norm,
                 attn_k_norm=v_attn_k_norm, attn_sinks=v_attn_sinks, attn_w_o=v_attn_w_o, ssm_w_in=v_ssm_w_in,
                 ssm_conv_w=v_ssm_conv_w, ssm_conv_b=v_ssm_conv_b, ssm_dt_bias=v_ssm_dt_bias, ssm_a_log=v_ssm_a_log,
                 ssm_d=v_ssm_d, ssm_norm=v_ssm_norm, ssm_w_out=v_ssm_w_out, ffn_w_gate=v_ffn_w_gate,
                 ffn_w_up=v_ffn_w_up, ffn_w_down=v_ffn_w_down)
    big = ["attn_w_qkv", "attn_w_o", "ssm_w_in", "ssm_w_out", "ffn_w_gate", "ffn_w_up", "ffn_w_down"]
    col_split = {"attn_w_qkv", "ssm_w_in", "ffn_w_gate", "ffn_w_up"}

    def flat2(a):
        return a.reshape(-1, a.shape[-1])

    gathered = _all_gather([flat2(weights[n]).astype(BF16) for n in big], "gather_weights")
    full = {}
    for n, gth in zip(big, gathered):
        nl = weights[n].shape[0]
        if n in col_split:
            r, c = weights[n].shape[1:]
            full[n] = gth.reshape(N_DEV, nl, r, c).transpose(1, 2, 0, 3).reshape(nl, r, N_DEV * c)
        else:
            r, c = weights[n].shape[1:]
            full[n] = gth.reshape(N_DEV, nl, r, c).transpose(1, 0, 2, 3).reshape(nl, N_DEV * r, c)
    w_qkv, w_o, w_out = full["attn_w_qkv"][0], full["attn_w_o"][0], full["ssm_w_out"][0]
    w_in = _pad_cols(full["ssm_w_in"][0], ZW)

    xs = x[0]
    tgt = loss_target[0]
    inv_freq = ROPE_THETA ** (-jnp.arange(0, HEAD, 2, dtype=F32) / HEAD)
    ang = positions[0].astype(F32)[:, None] * inv_freq
    cos = jnp.tile(jnp.cos(ang), (1, 4))
    sin = jnp.tile(jnp.concatenate([-jnp.sin(ang), jnp.sin(ang)], axis=1), (1, 2))
    gq = jnp.tile(attn_q_norm, (1, 2))
    gk = jnp.tile(attn_k_norm, (1, 2))
    sinkcol = jnp.repeat(attn_sinks.reshape(HKV, ATT_GROUP // 2, 2).transpose(0, 2, 1), WINDOW, axis=2)[..., None]
    bias_p = _pad_cols(ssm_dt_bias, LANES)
    alog_p = _pad_cols(ssm_a_log, LANES)
    dskip = jnp.repeat(ssm_d, HEAD, axis=1)
    lane_head = jnp.arange(DI) // HEAD
    sel = (jnp.arange(LANES)[None, :, None] == lane_head.reshape(G, 1, GW)).astype(BF16)
    selt = sel.transpose(0, 2, 1)
    hsel = (jnp.arange(LANES)[None, None, :] == (jnp.arange(G)[:, None, None] * (H // G) + jnp.arange(8)[None, :, None])
            ).astype(BF16) * (jnp.arange(8)[None, :, None] < H // G)
    vec_w = CD // N_DEV
    small = jnp.concatenate([ssm_conv_w[0], ssm_conv_b, _pad_cols(ssm_norm, vec_w),
                             jnp.zeros((2, vec_w), F32)], axis=0)
    small_all = _all_gather([small], "gather_vectors")[0]
    conv_w = small_all[:, :SSM_CONV].transpose(1, 0, 2).reshape(SSM_CONV, CD)
    conv_b = small_all[:, SSM_CONV].reshape(1, CD)
    ng = small_all[:, SSM_CONV + 1, :DI // N_DEV].reshape(1, DI)

    hm0 = _rms_fwd(xs, mixer_norm[0:1], "rms_fwd_m0")
    qkv = _matmul(hm0, w_qkv, mode="nn", out_dtype=F32, name="mm_qkv")
    qr, kd, vd = _attn_prep_fwd(qkv, cos, sin, gq, gk, D, HKV)
    o = _attn_fwd(qr, kd, vd, sinkcol, HKV)
    x1 = _matmul(o, w_o, mode="nn", out_dtype=F32, name="mm_attn_out", add=xs)
    hf0 = _rms_fwd(x1, ffn_norm[0:1], "rms_fwd_f0")
    gate0, up0, act0 = _ffn_up(hf0, full["ffn_w_gate"][0], full["ffn_w_up"][0], "ffn_up_0")
    x2 = _matmul(act0, full["ffn_w_down"][0], mode="nn", out_dtype=F32, name="mm_ffn_down_0", add=x1)
    hm1 = _rms_fwd(x2, mixer_norm[1:2], "rms_fwd_m1")
    zx = _matmul(hm1, w_in, mode="nn", out_dtype=F32, name="mm_ssm_in")
    xc = _conv_fwd(zx, conv_w, conv_b, DI, CD)
    yssd, yout, states = _ssd_fwd(zx, xc, bias_p, alog_p, sel, dskip, ng, DI)
    x3 = _matmul(yout, w_out, mode="nn", out_dtype=F32, name="mm_ssm_out", add=x2)
    hf1 = _rms_fwd(x3, ffn_norm[1:2], "rms_fwd_f1")
    gate1, up1, act1 = _ffn_up(hf1, full["ffn_w_gate"][1], full["ffn_w_up"][1], "ffn_up_1")
    x4 = _matmul(act1, full["ffn_w_down"][1], mode="nn", out_dtype=F32, name="mm_ffn_down_1", add=x3)
    sq, dx4 = _loss_head(x4, tgt)
    loss = lax.psum(sq[0, 0] * (0.5 / D), ("x", "y", "c"))

    def ffn_bwd(dy, hf, gate, up, act, layer, xin, gain):
        dg, du = _ffn_dact(dy, full["ffn_w_down"][layer], gate, up, f"ffn_dact_{layer}")
        g_down = _matmul(act, dy, mode="tn", out_dtype=BF16, name=f"mm_dw_down_{layer}")
        g_gate = _matmul(hf, dg, mode="tn", out_dtype=BF16, name=f"mm_dw_gate_{layer}")
        g_up = _matmul(hf, du, mode="tn", out_dtype=BF16, name=f"mm_dw_up_{layer}")
        dh = _matmul(dg, full["ffn_w_gate"][layer], mode="nt", out_dtype=F32, name=f"mm_dh_gate_{layer}")
        dh = _matmul(du, full["ffn_w_up"][layer], mode="nt", out_dtype=F32, name=f"mm_dh_up_{layer}", add=dh)
        dx, dgain = _rms_bwd(xin, gain, dh, dy, f"rms_bwd_f{layer}")
        return dx, dgain, g_gate, g_up, g_down

    dx3, d_fn1, g_gate1, g_up1, g_down1 = ffn_bwd(dx4, hf1, gate1, up1, act1, 1, x3, ffn_norm[1:2])
    dyo = _matmul(dx3, w_out, mode="nt", out_dtype=F32, name="mm_dyout")
    g_wout = _matmul(yout, dx3, mode="tn", out_dtype=BF16, name="mm_dw_ssm_out")
    dz, dxx, dbm, dcm, ddt_p, dac_p, dd_l, dng_l = _ssd_bwd(zx, xc, yssd, dyo, states, bias_p, alog_p, sel, selt,
                                                            hsel, dskip, ng, DI)
    ddt_raw, d_alog, d_bias = _ssd_dt_bwd(zx, ddt_p, dac_p, bias_p, alog_p)
    dxbc, d_convw, d_convb = _conv_bwd(zx, jnp.concatenate([dxx, dbm, dcm], axis=1), conv_w, conv_b, DI, CD)
    dzx = jnp.concatenate([dz, dxbc, ddt_raw], axis=1)
    g_win = _matmul(hm1, dzx, mode="tn", out_dtype=BF16, name="mm_dw_ssm_in")[:, :IW]
    dh = _matmul(dzx, w_in, mode="nt", out_dtype=F32, name="mm_dh_ssm_in")
    dx2, d_mn1 = _rms_bwd(x2, mixer_norm[1:2], dh, dx3, "rms_bwd_m1")
    dx1, d_fn0, g_gate0, g_up0, g_down0 = ffn_bwd(dx2, hf0, gate0, up0, act0, 0, x1, ffn_norm[0:1])
    do = _matmul(dx1, w_o, mode="nt", out_dtype=F32, name="mm_do")
    g_wo = _matmul(o, dx1, mode="tn", out_dtype=BF16, name="mm_dw_attn_out")
    dq, dkd, dvd, dsink = _attn_bwd(qr, kd, vd, o, do, sinkcol, HKV)
    dqkv, dgq_l, dgk_l = _attn_prep_bwd(qkv, dq, dkd, dvd, cos, sin, gq, gk, D, HKV)
    g_wqkv = _matmul(hm0, dqkv, mode="tn", out_dtype=BF16, name="mm_dw_qkv")
    dh = _matmul(dqkv, w_qkv, mode="nt", out_dtype=F32, name="mm_dh_qkv")
    dx0, d_mn0 = _rms_bwd(xs, mixer_norm[0:1], dh, dx1, "rms_bwd_m0")

    def rows_to_blocks(parts):
        st = jnp.stack([p.reshape(N_DEV, p.shape[0] // N_DEV, p.shape[1]) for p in parts], axis=1)
        return st.reshape(N_DEV, -1, st.shape[-1])

    def cols_to_blocks(parts):
        st = jnp.stack([p.reshape(p.shape[0], N_DEV, p.shape[1] // N_DEV).transpose(1, 0, 2) for p in parts], axis=1)
        return st.reshape(N_DEV, -1, st.shape[-1])

    send = {
        "attn_w_qkv": cols_to_blocks([g_wqkv]), "attn_w_o": rows_to_blocks([g_wo]),
        "ssm_w_in": cols_to_blocks([g_win]), "ssm_w_out": rows_to_blocks([g_wout]),
        "ffn_w_gate": cols_to_blocks([g_gate0, g_gate1]), "ffn_w_up": cols_to_blocks([g_up0, g_up1]),
        "ffn_w_down": rows_to_blocks([g_down0, g_down1]),
    }
    d_ng = dng_l.reshape(1, DI)
    vec_send = jnp.concatenate([
        d_convw.reshape(SSM_CONV, N_DEV, vec_w).transpose(1, 0, 2), d_convb.reshape(1, N_DEV, vec_w).transpose(1, 0, 2),
        _pad_cols(d_ng.reshape(N_DEV, DI // N_DEV), vec_w)[:, None, :], jnp.zeros((N_DEV, 2, vec_w), F32)], axis=1)
    d_sinks = dsink[:, :, 0].reshape(1, HQ)
    d_gq = dgq_l[:, :HEAD] + dgq_l[:, HEAD:]
    d_gk = dgk_l[:, :HEAD] + dgk_l[:, HEAD:]
    d_dskip = dd_l.reshape(H, HEAD).sum(axis=1).reshape(1, H)
    rep_names = ["mixer_norm", "ffn_norm", "attn_q_norm", "attn_k_norm", "attn_sinks", "ssm_dt_bias", "ssm_a_log",
                 "ssm_d"]
    rep_grads = [jnp.concatenate([d_mn0, d_mn1], axis=0), jnp.concatenate([d_fn0, d_fn1], axis=0), d_gq, d_gk,
                 d_sinks, d_bias[:, :H], d_alog[:, :H], d_dskip]
    rep_sizes = [weights[n].size for n in rep_names]
    rep_len = -(-sum(rep_sizes) // (8 * LANES)) * 8 * LANES

    def pack(arrs):
        flat = jnp.concatenate([a.reshape(-1) for a in arrs])
        return jnp.pad(flat, (0, rep_len - flat.shape[0])).reshape(rep_len // LANES, LANES)

    rep_send = jnp.broadcast_to(pack(rep_grads)[None], (N_DEV, rep_len // LANES, LANES))
    recv = _exchange([send[n] for n in big] + [vec_send, rep_send], "exchange_grads")

    out = {}
    for n, parts in zip(big, recv[:len(big)]):
        shp = weights[n].shape
        res = _adamw(parts, flat2(weights[n]), flat2(mom_m[n]), flat2(mom_v[n]), f"adamw_{n}")
        out[n] = [r.reshape(shp) for r in res]
    def vec_block(d):
        return jnp.concatenate([d["ssm_conv_w"][0], d["ssm_conv_b"], _pad_cols(d["ssm_norm"], vec_w),
                                jnp.zeros((2, vec_w), F32)], axis=0)

    res = _adamw(recv[len(big)], vec_block(weights), vec_block(mom_m), vec_block(mom_v), "adamw_vectors")
    out["ssm_conv_w"] = [r[:SSM_CONV][None] for r in res]
    out["ssm_conv_b"] = [r[SSM_CONV:SSM_CONV + 1] for r in res]
    out["ssm_norm"] = [r[SSM_CONV + 1:SSM_CONV + 2, :DI // N_DEV] for r in res]
    res = _adamw(recv[len(big) + 1], pack([weights[n] for n in rep_names]), pack([mom_m[n] for n in rep_names]),
                 pack([mom_v[n] for n in rep_names]), "adamw_replicated")
    offs = 0
    for n, sz in zip(rep_names, rep_sizes):
        out[n] = [r.reshape(-1)[offs:offs + sz].reshape(weights[n].shape) for r in res]
        offs += sz

    names = list(weights)
    return (loss, dx0[None], *[out[n][0] for n in names], *[out[n][1] for n in names],
            *[out[n][2] for n in names], *[out[n][3] for n in names])
```

```python
import math

import jax
import jax.numpy as jnp
from jax import lax
from jax.experimental import pallas as pl
from jax.experimental.pallas import tpu as pltpu

F32 = jnp.float32
BF16 = jnp.bfloat16

N_DEV = 8
EPS = 1e-6
LANES = 128
HEAD = 64
ATT_GROUP = 8
ATT_GW = ATT_GROUP * HEAD
WINDOW = 128
ROPE_THETA = 10000.0
SSM_GROUPS = 8
SSM_STATE = 128
SSM_CONV = 4
SSM_CHUNK = 256
HALO = 8
ADAM_LR, ADAM_B1, ADAM_B2, ADAM_EPS, ADAM_WD, ADAM_STEP = 0.001, 0.9, 0.999, 1e-08, 0.01, 10
VMEM_LIMIT = 56 * 1024 * 1024
MATMUL_VMEM = 44 * 1024 * 1024
MESH = pl.DeviceIdType.MESH

_NN = (((1,), (0,)), ((), ()))
_NT = (((1,), (1,)), ((), ()))
_TN = (((0,), (0,)), ((), ()))


def _dot(a, b, dims=_NN):
    return lax.dot_general(a, b, dims, preferred_element_type=F32)


def _tile(n, cap):
    if n % LANES:
        return n
    best = LANES
    for t in range(LANES, min(n, cap) + 1, LANES):
        if n % t == 0:
            best = t
    return best


def _params(sem):
    return pltpu.CompilerParams(dimension_semantics=sem, vmem_limit_bytes=VMEM_LIMIT)


def _matmul(a, b, *, mode, out_dtype, name, add=None):
    if mode == "nn":
        (M, K), N = a.shape, b.shape[1]
    elif mode == "nt":
        (M, K), N = a.shape, b.shape[0]
    else:
        (K, M), N = a.shape, b.shape[1]
    assert a.dtype == BF16 and b.dtype == BF16
    has_add = add is not None
    tm, tn = (_tile(M, 512), _tile(N, 512)) if mode == "tn" else (_tile(M, 1024), _tile(N, 512))
    fixed = 2 * tm * tn * (jnp.dtype(out_dtype).itemsize + (4 if has_add else 0)) + tm * tn * 4
    per_k = 2 * 2 * (tm + tn) + (2 * tm if mode == "tn" else 0)
    tk = _tile(K, max(LANES, (MATMUL_VMEM - fixed) // per_k))
    nk = K // tk
    dims = _NT if mode == "nt" else _NN
    if mode == "tn":
        a_spec = pl.BlockSpec((tk, tm), lambda i, j, k: (jnp.where(j == 0, k, 0), i))
    else:
        a_spec = pl.BlockSpec((tm, tk), lambda i, j, k: (i, k))
    b_spec = pl.BlockSpec((tn, tk), lambda i, j, k: (j, k)) if mode == "nt" else pl.BlockSpec((tk, tn), lambda i, j, k: (k, j))
    o_spec = pl.BlockSpec((tm, tn), lambda i, j, k: (i, j))

    def body(*refs):
        a_ref, b_ref = refs[:2]
        add_ref = refs[2] if has_add else None
        o_ref = refs[2 + has_add]
        scratch = list(refs[3 + has_add:])
        at = scratch.pop(0) if mode == "tn" else None
        acc = scratch.pop(0) if nk > 1 else None
        j, k = pl.program_id(1), pl.program_id(2)
        if mode == "tn":
            @pl.when(j == 0)
            def _():
                at[k] = a_ref[...].T

            part = _dot(at[k], b_ref[...], dims)
        else:
            part = _dot(a_ref[...], b_ref[...], dims)

        def finish(r):
            if has_add:
                r = r + add_ref[...]
            o_ref[...] = r.astype(out_dtype)

        if nk == 1:
            finish(part)
        else:
            @pl.when(k == 0)
            def _():
                acc[...] = part

            @pl.when(jnp.logical_and(k > 0, k < nk - 1))
            def _():
                acc[...] += part

            @pl.when(k == nk - 1)
            def _():
                finish(acc[...] + part)

    scratch = ([pltpu.VMEM((nk, tm, tk), BF16)] if mode == "tn" else []) + ([pltpu.VMEM((tm, tn), F32)] if nk > 1 else [])
    return pl.pallas_call(
        body, name=name, grid=(M // tm, N // tn, nk),
        in_specs=[a_spec, b_spec] + ([o_spec] if has_add else []),
        out_specs=o_spec, out_shape=jax.ShapeDtypeStruct((M, N), out_dtype),
        scratch_shapes=scratch,
        compiler_params=_params(("parallel", "arbitrary", "arbitrary")),
    )(*((a, b, add) if has_add else (a, b)))


def _rms_fwd(x, gain, name):
    T, D = x.shape
    tr = 256

    def body(x_ref, g_ref, h_ref):
        xv = x_ref[...]
        rstd = lax.rsqrt(jnp.mean(xv * xv, axis=1, keepdims=True) + EPS)
        h_ref[...] = (xv * rstd * g_ref[...]).astype(BF16)

    return pl.pallas_call(
        body, name=name, grid=(T // tr,),
        in_specs=[pl.BlockSpec((tr, D), lambda i: (i, 0)), pl.BlockSpec((1, D), lambda i: (0, 0))],
        out_specs=pl.BlockSpec((tr, D), lambda i: (i, 0)),
        out_shape=jax.ShapeDtypeStruct((T, D), BF16),
        compiler_params=_params(("parallel",)),
    )(x, gain)


def _rms_bwd(x, gain, dh, dres, name):
    T, D = x.shape
    tr = 256

    def body(x_ref, g_ref, dh_ref, dr_ref, dx_ref, dxb_ref, dg_ref):
        @pl.when(pl.program_id(0) == 0)
        def _():
            dg_ref[...] = jnp.zeros_like(dg_ref)

        xv = x_ref[...]
        rstd = lax.rsqrt(jnp.mean(xv * xv, axis=1, keepdims=True) + EPS)
        xhat = xv * rstd
        dy = dh_ref[...].astype(F32)
        dg_ref[...] += jnp.sum(dy * xhat, axis=0, keepdims=True)
        dxh = dy * g_ref[...]
        dx = dr_ref[...] + rstd * (dxh - xhat * jnp.mean(dxh * xhat, axis=1, keepdims=True))
        dx_ref[...] = dx
        dxb_ref[...] = dx.astype(BF16)

    row = pl.BlockSpec((tr, D), lambda i: (i, 0))
    vec = pl.BlockSpec((1, D), lambda i: (0, 0))
    return pl.pallas_call(
        body, name=name, grid=(T // tr,), in_specs=[row, vec, row, row], out_specs=[row, row, vec],
        out_shape=[jax.ShapeDtypeStruct((T, D), F32), jax.ShapeDtypeStruct((T, D), BF16),
                   jax.ShapeDtypeStruct((1, D), F32)],
        compiler_params=_params(("arbitrary",)),
    )(x, gain, dh, dres)


def _loss_head(y, target):
    T, D = y.shape
    tr = 256

    def body(y_ref, t_ref, s_ref, d_ref, db_ref):
        @pl.when(pl.program_id(0) == 0)
        def _():
            s_ref[...] = jnp.zeros_like(s_ref)

        e = y_ref[...] - t_ref[...]
        s_ref[...] += jnp.sum(jnp.sum(e * e, axis=1, keepdims=True), axis=0, keepdims=True)
        d = e * (1.0 / D)
        d_ref[...] = d
        db_ref[...] = d.astype(BF16)

    row = pl.BlockSpec((tr, D), lambda i: (i, 0))
    return pl.pallas_call(
        body, name="loss_head", grid=(T // tr,), in_specs=[row, row],
        out_specs=[pl.BlockSpec((1, 1), lambda i: (0, 0)), row, row],
        out_shape=[jax.ShapeDtypeStruct((1, 1), F32), jax.ShapeDtypeStruct((T, D), F32),
                   jax.ShapeDtypeStruct((T, D), BF16)],
        compiler_params=_params(("arbitrary",)),
    )(y, target)


def _ffn_up(h, wg, wu, name):
    (T, D), Fd = h.shape, wg.shape[1]
    tm, tn = _tile(T, 1024), _tile(Fd, 512)

    def body(h_ref, wg_ref, wu_ref, g_ref, u_ref, a_ref):
        hv = h_ref[...]
        g = _dot(hv, wg_ref[...])
        g_ref[...] = g
        sg = g * jax.nn.sigmoid(g)
        u = _dot(hv, wu_ref[...])
        u_ref[...] = u
        a_ref[...] = (sg * u).astype(BF16)

    w_spec = pl.BlockSpec((D, tn), lambda i, j: (0, j))
    o_spec = pl.BlockSpec((tm, tn), lambda i, j: (i, j))
    return pl.pallas_call(
        body, name=name, grid=(T // tm, Fd // tn),
        in_specs=[pl.BlockSpec((tm, D), lambda i, j: (i, 0)), w_spec, w_spec],
        out_specs=[o_spec, o_spec, o_spec],
        out_shape=[jax.ShapeDtypeStruct((T, Fd), F32), jax.ShapeDtypeStruct((T, Fd), F32),
                   jax.ShapeDtypeStruct((T, Fd), BF16)],
        compiler_params=_params(("parallel", "arbitrary")),
    )(h, wg, wu)


def _ffn_dact(dy, wd, gate, up, name):
    (T, D), Fd = dy.shape, wd.shape[0]
    tm, tn = _tile(T, 1024), _tile(Fd, 512)

    def body(dy_ref, wd_ref, g_ref, u_ref, dg_ref, du_ref):
        da = _dot(dy_ref[...], wd_ref[...], _NT)
        g = g_ref[...]
        sg = jax.nn.sigmoid(g)
        du_ref[...] = (da * g * sg).astype(BF16)
        dg_ref[...] = (da * u_ref[...] * sg * (1.0 + g * (1.0 - sg))).astype(BF16)

    o_spec = pl.BlockSpec((tm, tn), lambda i, j: (i, j))
    return pl.pallas_call(
        body, name=name, grid=(T // tm, Fd // tn),
        in_specs=[pl.BlockSpec((tm, D), lambda i, j: (i, 0)), pl.BlockSpec((tn, D), lambda i, j: (j, 0)),
                  o_spec, o_spec],
        out_specs=[o_spec, o_spec],
        out_shape=[jax.ShapeDtypeStruct((T, Fd), BF16), jax.ShapeDtypeStruct((T, Fd), BF16)],
        compiler_params=_params(("parallel", "arbitrary")),
    )(dy, wd, gate, up)


def _is_a(shape):
    return lax.broadcasted_iota(jnp.int32, shape, 1) % LANES < HEAD


def _split2(v):
    hi = v.astype(BF16)
    return hi, (v - hi.astype(F32)).astype(BF16)


def _split3(v):
    hi = v.astype(BF16)
    r = v - hi.astype(F32)
    mid = r.astype(BF16)
    return hi, mid, (r - mid.astype(F32)).astype(BF16)


def _dot_split(v, m, pieces, dims=_NN):
    parts = _split3(v) if pieces == 3 else _split2(v)
    out = _dot(parts[0], m, dims)
    for p in parts[1:]:
        out = out + _dot(p, m, dims)
    return out


def _head_blockdiag():
    r = lax.broadcasted_iota(jnp.int32, (LANES, LANES), 0) // HEAD
    c = lax.broadcasted_iota(jnp.int32, (LANES, LANES), 1) // HEAD
    return (r == c).astype(BF16)


def _swap_half(v):
    lane = lax.broadcasted_iota(jnp.int32, v.shape, 1)
    return jnp.where(lane % HEAD < HEAD // 2, pltpu.roll(v, LANES - HEAD // 2, axis=1), pltpu.roll(v, HEAD // 2, axis=1))


def _attn_prep_fwd(qkv, cos, sin, gq, gk, D, HKV):
    T, QW = qkv.shape
    tr = 256
    nq, nk = D // LANES, HKV // 2
    KW = HKV * LANES

    def body(x_ref, cos_ref, sin_ref, gq_ref, gk_ref, q_ref, k_ref, v_ref):
        bd = _head_blockdiag()
        cs, sn = cos_ref[...], sin_ref[...]
        isa = _is_a((tr, LANES))

        def normrope(xv, g):
            ms = _dot_split(xv * xv, bd, 2) * (1.0 / HEAD)
            xn = xv * lax.rsqrt(ms + EPS) * g
            return xn * cs + _swap_half(xn) * sn

        def dup(v):
            r = pltpu.roll(v, HEAD, axis=1)
            return jnp.where(isa, v, r), jnp.where(isa, r, v)

        for s in range(nq):
            sl = slice(s * LANES, (s + 1) * LANES)
            q_ref[:, sl] = normrope(x_ref[:, sl], gq_ref[...]).astype(BF16)
        for s in range(nk):
            ka, kb = dup(normrope(x_ref[:, D + s * LANES:D + (s + 1) * LANES], gk_ref[...]))
            k_ref[:, 2 * s * LANES:(2 * s + 1) * LANES] = ka.astype(BF16)
            k_ref[:, (2 * s + 1) * LANES:(2 * s + 2) * LANES] = kb.astype(BF16)
            va, vb = dup(x_ref[:, D + (nk + s) * LANES:D + (nk + s + 1) * LANES])
            v_ref[:, 2 * s * LANES:(2 * s + 1) * LANES] = va.astype(BF16)
            v_ref[:, (2 * s + 1) * LANES:(2 * s + 2) * LANES] = vb.astype(BF16)

    tab = pl.BlockSpec((tr, LANES), lambda i: (i, 0))
    vec = pl.BlockSpec((1, LANES), lambda i: (0, 0))
    return pl.pallas_call(
        body, name="attn_prep_fwd", grid=(T // tr,),
        in_specs=[pl.BlockSpec((tr, QW), lambda i: (i, 0)), tab, tab, vec, vec],
        out_specs=[pl.BlockSpec((tr, D), lambda i: (i, 0)), pl.BlockSpec((tr, KW), lambda i: (i, 0)),
                   pl.BlockSpec((tr, KW), lambda i: (i, 0))],
        out_shape=[jax.ShapeDtypeStruct((T, D), BF16), jax.ShapeDtypeStruct((T, KW), BF16),
                   jax.ShapeDtypeStruct((T, KW), BF16)],
        compiler_params=_params(("parallel",)),
    )(qkv, cos, sin, gq, gk)


def _attn_prep_bwd(qkv, dq, dkd, dvd, cos, sin, gq, gk, D, HKV):
    T, QW = qkv.shape
    tr = 256
    nq, nk = D // LANES, HKV // 2
    KW = HKV * LANES

    def body(x_ref, dq_ref, dk_ref, dv_ref, cos_ref, sin_ref, gq_ref, gk_ref, o_ref, dgq_ref, dgk_ref):
        @pl.when(pl.program_id(0) == 0)
        def _():
            dgq_ref[...] = jnp.zeros_like(dgq_ref)
            dgk_ref[...] = jnp.zeros_like(dgk_ref)

        bd = _head_blockdiag()
        cs, sn = cos_ref[...], sin_ref[...]
        isa = _is_a((tr, LANES))

        def back(xv, dy, g):
            rstd = lax.rsqrt(_dot_split(xv * xv, bd, 2) * (1.0 / HEAD) + EPS)
            xhat = xv * rstd
            dxn = dy * cs + _swap_half(dy * sn)
            dxh = dxn * g
            mean = _dot_split(dxh * xhat, bd, 2) * (1.0 / HEAD)
            return rstd * (dxh - xhat * mean), jnp.sum(dxn * xhat, axis=0, keepdims=True)

        def fold(s):
            a = dk_ref[:, 2 * s * LANES:(2 * s + 1) * LANES]
            b = dk_ref[:, (2 * s + 1) * LANES:(2 * s + 2) * LANES]
            return jnp.where(isa, a + pltpu.roll(a, HEAD, axis=1), b + pltpu.roll(b, HEAD, axis=1))

        def foldv(s):
            a = dv_ref[:, 2 * s * LANES:(2 * s + 1) * LANES]
            b = dv_ref[:, (2 * s + 1) * LANES:(2 * s + 2) * LANES]
            return jnp.where(isa, a + pltpu.roll(a, HEAD, axis=1), b + pltpu.roll(b, HEAD, axis=1))

        dgq = jnp.zeros((1, LANES), F32)
        for s in range(nq):
            sl = slice(s * LANES, (s + 1) * LANES)
            dx, dg = back(x_ref[:, sl], dq_ref[:, sl], gq_ref[...])
            o_ref[:, sl] = dx.astype(BF16)
            dgq = dgq + dg
        dgq_ref[...] += dgq
        dgk = jnp.zeros((1, LANES), F32)
        for s in range(nk):
            sl = slice(D + s * LANES, D + (s + 1) * LANES)
            dx, dg = back(x_ref[:, sl], fold(s), gk_ref[...])
            o_ref[:, sl] = dx.astype(BF16)
            dgk = dgk + dg
            o_ref[:, D + (nk + s) * LANES:D + (nk + s + 1) * LANES] = foldv(s).astype(BF16)
        dgk_ref[...] += dgk

    tab = pl.BlockSpec((tr, LANES), lambda i: (i, 0))
    vec = pl.BlockSpec((1, LANES), lambda i: (0, 0))
    kv = pl.BlockSpec((tr, KW), lambda i: (i, 0))
    return pl.pallas_call(
        body, name="attn_prep_bwd", grid=(T // tr,),
        in_specs=[pl.BlockSpec((tr, QW), lambda i: (i, 0)), pl.BlockSpec((tr, D), lambda i: (i, 0)), kv, kv,
                  tab, tab, vec, vec],
        out_specs=[pl.BlockSpec((tr, QW), lambda i: (i, 0)), vec, vec],
        out_shape=[jax.ShapeDtypeStruct((T, QW), BF16), jax.ShapeDtypeStruct((1, LANES), F32),
                   jax.ShapeDtypeStruct((1, LANES), F32)],
        compiler_params=_params(("arbitrary",)),
    )(qkv, dq, dkd, dvd, cos, sin, gq, gk)


def _attn_probs(qs, kw, sink_ref, n, scale):
    rows = qs.shape[0]
    qi = lax.broadcasted_iota(jnp.int32, (rows, 2 * WINDOW), 0) % WINDOW
    kj = lax.broadcasted_iota(jnp.int32, (rows, 2 * WINDOW), 1)
    valid = (kj > qi) & (kj <= qi + WINDOW) & jnp.logical_or(n > 0, kj >= WINDOW)
    isa = _is_a(kw.shape)
    out = []
    for pos in (0, 1):
        kp = jnp.where(isa if pos == 0 else ~isa, kw, jnp.zeros_like(kw))
        s = jnp.where(valid, _dot(qs, kp, _NT) * scale, -jnp.inf)
        sink = sink_ref[0, pos]
        m = jnp.maximum(jnp.max(s, axis=1, keepdims=True), sink)
        p = jnp.exp(s - m)
        ps = jnp.exp(sink - m)
        inv = 1.0 / (jnp.sum(p, axis=1, keepdims=True) + ps)
        out.append((p * inv, ps * inv, kp))
    return out


def _attn_specs(HKV):
    q = pl.BlockSpec((WINDOW, ATT_GW), lambda g, n: (n, g))
    cur = pl.BlockSpec((WINDOW, LANES), lambda g, n: (n, g))
    prev = pl.BlockSpec((WINDOW, LANES), lambda g, n: (jnp.maximum(n - 1, 0), g))
    sink = pl.BlockSpec((1, 2, ATT_GW, 1), lambda g, n: (g, 0, 0, 0))
    return q, cur, prev, sink


def _stack(ref):
    return jnp.concatenate([ref[:, i * LANES:(i + 1) * LANES] for i in range(ATT_GW // LANES)], axis=0)


def _attn_fwd(q, kd, vd, sinkcol, HKV):
    T, D = q.shape
    nb = T // WINDOW
    scale = HEAD ** -0.5

    def body(q_ref, kp_ref, kc_ref, vp_ref, vc_ref, sink_ref, o_ref):
        n = pl.program_id(1)
        qs = _stack(q_ref)
        kw = jnp.concatenate([kp_ref[...], kc_ref[...]], axis=0)
        vw = jnp.concatenate([vp_ref[...], vc_ref[...]], axis=0)
        isa = _is_a(vw.shape)
        o = jnp.zeros((ATT_GW, LANES), F32)
        for pos, (probs, _, _) in enumerate(_attn_probs(qs, kw, sink_ref, n, scale)):
            vp = jnp.where(isa if pos == 0 else ~isa, vw, jnp.zeros_like(vw))
            o = o + _dot(probs.astype(BF16), vp)
        for i in range(ATT_GW // LANES):
            o_ref[:, i * LANES:(i + 1) * LANES] = o[i * WINDOW:(i + 1) * WINDOW].astype(BF16)

    qs_, cur, prev, sink = _attn_specs(HKV)
    return pl.pallas_call(
        body, name="attn_fwd", grid=(HKV, nb), in_specs=[qs_, prev, cur, prev, cur, sink], out_specs=qs_,
        out_shape=jax.ShapeDtypeStruct((T, D), BF16),
        compiler_params=_params(("parallel", "parallel")),
    )(q, kd, kd, vd, vd, sinkcol)


def _attn_bwd(q, kd, vd, o, do, sinkcol, HKV):
    T, D = q.shape
    nb = T // WINDOW
    scale = HEAD ** -0.5
    KW = HKV * LANES

    def body(q_ref, kp_ref, kc_ref, vp_ref, vc_ref, o_ref, do_ref, sink_ref, dq_ref, dk_ref, dv_ref, ds_ref):
        n = pl.program_id(1)

        @pl.when(n == 0)
        def _():
            dk_ref[...] = jnp.zeros_like(dk_ref)
            dv_ref[...] = jnp.zeros_like(dv_ref)
            ds_ref[...] = jnp.zeros_like(ds_ref)

        qs = _stack(q_ref)
        dos = _stack(do_ref)
        os_ = _stack(o_ref).astype(F32)
        kw = jnp.concatenate([kp_ref[...], kc_ref[...]], axis=0)
        vw = jnp.concatenate([vp_ref[...], vc_ref[...]], axis=0)
        isa_w = _is_a(vw.shape)
        isa_q = _is_a(dos.shape)
        dd = dos * os_
        dob = dos.astype(BF16)
        dqs = jnp.zeros((ATT_GW, LANES), F32)
        dkw, dvw, dsk = [], [], []
        for pos, (probs, psink, kp) in enumerate(_attn_probs(qs, kw, sink_ref, n, scale)):
            sel_w = isa_w if pos == 0 else ~isa_w
            sel_q = isa_q if pos == 0 else ~isa_q
            delta = jnp.sum(jnp.where(sel_q, dd, 0.0), axis=1, keepdims=True)
            vp = jnp.where(sel_w, vw, jnp.zeros_like(vw))
            dp = _dot(dob, vp, _NT)
            dsb = (probs * (dp - delta) * scale).astype(BF16)
            dqs = dqs + _dot(dsb, kp)
            dkw.append(_dot(dsb, qs, _TN))
            dvw.append(_dot(probs.astype(BF16), dob, _TN))
            dsk.append(-psink * delta)
        dkw = jnp.where(isa_w, dkw[0], dkw[1])
        dvw = jnp.where(isa_w, dvw[0], dvw[1])

        @pl.when(n == 0)
        def _():
            dk_ref[0:WINDOW, :] += dkw[WINDOW:]
            dv_ref[0:WINDOW, :] += dvw[WINDOW:]

        @pl.when(n > 0)
        def _():
            start = pl.multiple_of((n - 1) * WINDOW, WINDOW)
            dk_ref[pl.ds(start, 2 * WINDOW), :] += dkw
            dv_ref[pl.ds(start, 2 * WINDOW), :] += dvw

        rows = []
        for i in range(ATT_GW // LANES):
            dq_ref[:, i * LANES:(i + 1) * LANES] = dqs[i * WINDOW:(i + 1) * WINDOW]
            for pos in (0, 1):
                s = jnp.sum(dsk[pos][i * WINDOW:(i + 1) * WINDOW], axis=0, keepdims=True)
                rows.append(jnp.broadcast_to(s, (1, LANES)))
        ds_ref[0] += jnp.concatenate(rows, axis=0)

    qs_, cur, prev, sink = _attn_specs(HKV)
    dqo = pl.BlockSpec((WINDOW, ATT_GW), lambda g, n: (n, g))
    dkv = pl.BlockSpec((T, LANES), lambda g, n: (0, g))
    return pl.pallas_call(
        body, name="attn_bwd", grid=(HKV, nb),
        in_specs=[qs_, prev, cur, prev, cur, qs_, dqo, sink],
        out_specs=[dqo, dkv, dkv, pl.BlockSpec((1, ATT_GROUP, LANES), lambda g, n: (g, 0, 0))],
        out_shape=[jax.ShapeDtypeStruct((T, D), F32), jax.ShapeDtypeStruct((T, KW), F32),
                   jax.ShapeDtypeStruct((T, KW), F32), jax.ShapeDtypeStruct((HKV, ATT_GROUP, LANES), F32)],
        compiler_params=_params(("parallel", "arbitrary")),
    )(q, kd, kd, vd, vd, o, do, sinkcol)


def _conv_fwd(zx, w, b, DI, CD):
    T = zx.shape[0]
    cw, tc = _tile(math.gcd(DI, CD), 512), 512
    off = DI // cw

    def body(cur_ref, halo_ref, w_ref, b_ref, o_ref):
        i = pl.program_id(1)
        halo = jnp.where(i > 0, halo_ref[...], 0.0)
        ext = jnp.concatenate([halo, cur_ref[...]], axis=0)
        acc = b_ref[...] + w_ref[SSM_CONV - 1:SSM_CONV, :] * ext[HALO:]
        for k in range(SSM_CONV - 1):
            acc = acc + w_ref[k:k + 1, :] * pltpu.roll(ext, SSM_CONV - 1 - k, axis=0)[HALO:]
        o_ref[...] = acc * jax.nn.sigmoid(acc)

    return pl.pallas_call(
        body, name="ssm_conv_fwd", grid=(CD // cw, T // tc),
        in_specs=[pl.BlockSpec((tc, cw), lambda j, i: (i, off + j)),
                  pl.BlockSpec((HALO, cw), lambda j, i: (jnp.maximum(i * (tc // HALO) - 1, 0), off + j)),
                  pl.BlockSpec((SSM_CONV, cw), lambda j, i: (0, j)), pl.BlockSpec((1, cw), lambda j, i: (0, j))],
        out_specs=pl.BlockSpec((tc, cw), lambda j, i: (i, j)),
        out_shape=jax.ShapeDtypeStruct((T, CD), F32),
        compiler_params=_params(("parallel", "parallel")),
    )(zx, zx, w, b)


def _conv_bwd(zx, dxc, w, b, DI, CD):
    T = zx.shape[0]
    cw, tc = _tile(math.gcd(DI, CD), 512), 512
    off = DI // cw
    nt = T // tc
    hb = tc // HALO

    def body(cur_ref, prev_ref, next_ref, d_ref, dnext_ref, w_ref, b_ref, o_ref, dw_ref, db_ref):
        i = pl.program_id(1)

        @pl.when(i == 0)
        def _():
            dw_ref[...] = jnp.zeros_like(dw_ref)
            db_ref[...] = jnp.zeros_like(db_ref)

        prev = jnp.where(i > 0, prev_ref[...], 0.0)
        ext = jnp.concatenate([prev, cur_ref[...], next_ref[...]], axis=0)
        u = b_ref[...] + w_ref[SSM_CONV - 1:SSM_CONV, :] * ext
        for k in range(SSM_CONV - 1):
            u = u + w_ref[k:k + 1, :] * pltpu.roll(ext, SSM_CONV - 1 - k, axis=0)
        u = u[HALO:]
        dnext = jnp.where(i < nt - 1, dnext_ref[...], 0.0)
        dxe = jnp.concatenate([d_ref[...], dnext], axis=0)
        sg = jax.nn.sigmoid(u)
        du = dxe * sg * (1.0 + u * (1.0 - sg))
        n_e = tc + HALO
        dx = w_ref[SSM_CONV - 1:SSM_CONV, :] * du
        for k in range(SSM_CONV - 1):
            dx = dx + w_ref[k:k + 1, :] * pltpu.roll(du, n_e - (SSM_CONV - 1 - k), axis=0)
        o_ref[...] = dx[:tc].astype(BF16)
        duc = du[:tc]
        db_ref[...] += jnp.sum(duc, axis=0, keepdims=True)
        xs = ext[:n_e]
        dws = []
        for k in range(SSM_CONV):
            sh = xs if k == SSM_CONV - 1 else pltpu.roll(xs, SSM_CONV - 1 - k, axis=0)
            dws.append(jnp.sum(duc * sh[HALO:], axis=0, keepdims=True))
        dw_ref[...] += jnp.concatenate(dws, axis=0)

    return pl.pallas_call(
        body, name="ssm_conv_bwd", grid=(CD // cw, nt),
        in_specs=[pl.BlockSpec((tc, cw), lambda j, i: (i, off + j)),
                  pl.BlockSpec((HALO, cw), lambda j, i: (jnp.maximum(i * hb - 1, 0), off + j)),
                  pl.BlockSpec((HALO, cw), lambda j, i: (jnp.minimum((i + 1) * hb, nt * hb - 1), off + j)),
                  pl.BlockSpec((tc, cw), lambda j, i: (i, j)),
                  pl.BlockSpec((HALO, cw), lambda j, i: (jnp.minimum((i + 1) * hb, nt * hb - 1), j)),
                  pl.BlockSpec((SSM_CONV, cw), lambda j, i: (0, j)), pl.BlockSpec((1, cw), lambda j, i: (0, j))],
        out_specs=[pl.BlockSpec((tc, cw), lambda j, i: (i, j)), pl.BlockSpec((SSM_CONV, cw), lambda j, i: (0, j)),
                   pl.BlockSpec((1, cw), lambda j, i: (0, j))],
        out_shape=[jax.ShapeDtypeStruct((T, CD), BF16), jax.ShapeDtypeStruct((SSM_CONV, CD), F32),
                   jax.ShapeDtypeStruct((1, CD), F32)],
        compiler_params=_params(("parallel", "arbitrary")),
    )(zx, zx, zx, dxc, dxc, w, b)


def _tri_dot(v, upper):
    L = v.shape[0]
    r = lax.broadcasted_iota(jnp.int32, (L, L), 0)
    c = lax.broadcasted_iota(jnp.int32, (L, L), 1)
    tri = ((r <= c) if upper else (r >= c)).astype(BF16)
    p = _split3(v)
    return _dot(tri, p[0]) + _dot(tri, p[1]) + _dot(tri, p[2])


def _ssd_time2(dtraw_ref, bias_ref, alog_ref, sel):
    dt = jax.nn.softplus(dtraw_ref[...] + bias_ref[...])
    acum = _tri_dot(dt * (-jnp.exp(alog_ref[...])), False)
    return dt, _dot_split(dt, sel, 3), _dot_split(acum, sel, 3)


def _decay(acs, acs_t, pos, transposed):
    L = acs.shape[0]
    r = lax.broadcasted_iota(jnp.int32, (L, L), 0)
    c = lax.broadcasted_iota(jnp.int32, (L, L), 1)
    col = acs[:, HEAD * pos:HEAD * pos + 1]
    row = acs_t[HEAD * pos:HEAD * pos + 1, :]
    if transposed:
        return jnp.exp(jnp.where(r <= c, row - col, -jnp.inf))
    return jnp.exp(jnp.where(r >= c, col - row, -jnp.inf))


def _ssd_specs(G, GW, DI, ZW):
    L = SSM_CHUNK
    grp = lambda f: pl.BlockSpec((L, GW), lambda g, c: (f(c), g))
    return dict(
        grp=grp,
        bmat=lambda f: pl.BlockSpec((L, SSM_STATE), lambda g, c: (f(c), DI // SSM_STATE + g)),
        cmat=lambda f: pl.BlockSpec((L, SSM_STATE), lambda g, c: (f(c), DI // SSM_STATE + G + g)),
        dtraw=lambda f: pl.BlockSpec((L, LANES), lambda g, c: (f(c), ZW // LANES - 1)),
        vec=pl.BlockSpec((1, LANES), lambda g, c: (0, 0)),
        gvec=pl.BlockSpec((1, GW), lambda g, c: (0, g)),
        sel=pl.BlockSpec((1, LANES, GW), lambda g, c: (g, 0, 0)),
        selt=pl.BlockSpec((1, GW, LANES), lambda g, c: (g, 0, 0)),
    )


def _ssd_fwd(zx, xc, bias, alog, sel, dskip, ng, DI):
    T, ZW = zx.shape
    G, L = SSM_GROUPS, SSM_CHUNK
    GW = DI // G
    NS = GW // LANES
    nc = T // L
    sp = _ssd_specs(G, GW, DI, ZW)
    ident = lambda c: c

    def body(x_ref, b_ref, c_ref, z_ref, dtraw_ref, bias_ref, alog_ref, sel_ref, d_ref, ng_ref,
             y_ref, yo_ref, st_ref, state):
        c = pl.program_id(1)

        @pl.when(c == 0)
        def _():
            state[...] = jnp.zeros_like(state)

        x = x_ref[...]
        bb, cb_ = b_ref[...].astype(BF16), c_ref[...].astype(BF16)
        cbm = _dot(cb_, bb, _NT)
        _, dtx, acx = _ssd_time2(dtraw_ref, bias_ref, alog_ref, sel_ref[0])
        xdt = x * dtx
        ex = jnp.exp(acx)
        last = acx[L - 1:L, :]
        te = jnp.exp(last - acx)
        dlast = jnp.exp(last)
        isa = _is_a((L, LANES))
        for i in range(NS):
            sl = slice(i * LANES, (i + 1) * LANES)
            acs = acx[:, sl]
            acs_t = acs.T
            xs = xdt[:, sl]
            y = jnp.zeros((L, LANES), F32)
            for pos in (0, 1):
                m = (cbm * _decay(acs, acs_t, pos, False)).astype(BF16)
                y = y + _dot(m, jnp.where(isa if pos == 0 else ~isa, xs, 0.0).astype(BF16))
            st = state[i]
            st_ref[0, i] = st
            y = y + _dot(cb_, st.astype(BF16)) * ex[:, sl]
            state[i] = st * dlast[:, sl] + _dot(bb, (xs * te[:, sl]).astype(BF16), _TN)
            y_ref[:, sl] = y + d_ref[:, sl] * x[:, sl]
        z = z_ref[...]
        gated = y_ref[...] * (z * jax.nn.sigmoid(z))
        rstd = lax.rsqrt(jnp.mean(gated * gated, axis=1, keepdims=True) + EPS)
        yo_ref[...] = (gated * rstd * ng_ref[...]).astype(BF16)

    return pl.pallas_call(
        body, name="ssd_fwd", grid=(G, nc),
        in_specs=[sp["grp"](ident), sp["bmat"](ident), sp["cmat"](ident), sp["grp"](ident), sp["dtraw"](ident),
                  sp["vec"], sp["vec"], sp["sel"], sp["gvec"], sp["gvec"]],
        out_specs=[sp["grp"](ident), sp["grp"](ident),
                   pl.BlockSpec((1, NS, SSM_STATE, LANES), lambda g, c: (c, g, 0, 0))],
        out_shape=[jax.ShapeDtypeStruct((T, DI), F32), jax.ShapeDtypeStruct((T, DI), BF16),
                   jax.ShapeDtypeStruct((nc, G * NS, SSM_STATE, LANES), F32)],
        scratch_shapes=[pltpu.VMEM((NS, SSM_STATE, LANES), F32)],
        compiler_params=_params(("parallel", "arbitrary")),
    )(xc, xc, xc, zx, zx, bias, alog, sel, dskip, ng)


def _ssd_bwd(zx, xc, yssd, dyo, states, bias, alog, sel, selt, hsel, dskip, ng, DI):
    T, ZW = zx.shape
    G, L = SSM_GROUPS, SSM_CHUNK
    GW = DI // G
    NS = GW // LANES
    nc = T // L
    sp = _ssd_specs(G, GW, DI, ZW)
    rev = lambda c: nc - 1 - c

    def body(x_ref, b_ref, c_ref, z_ref, dtraw_ref, y_ref, dyo_ref, st_ref, bias_ref, alog_ref, sel_ref, selt_ref,
             hsel_ref, d_ref, ng_ref, dz_ref, dx_ref, db_ref, dc_ref, ddt_ref, dac_ref, dd_ref, dng_ref, dstate):
        c = pl.program_id(1)

        @pl.when(c == 0)
        def _():
            dstate[...] = jnp.zeros_like(dstate)
            dd_ref[...] = jnp.zeros_like(dd_ref)
            dng_ref[...] = jnp.zeros_like(dng_ref)

        z, ys, dyo = z_ref[...], y_ref[...], dyo_ref[...]
        sg = jax.nn.sigmoid(z)
        sz = z * sg
        gated = ys * sz
        rstd = lax.rsqrt(jnp.mean(gated * gated, axis=1, keepdims=True) + EPS)
        yn = gated * rstd
        dng_ref[0] += jnp.sum(dyo * yn, axis=0, keepdims=True)
        dyn = dyo * ng_ref[...]
        dgated = rstd * (dyn - yn * jnp.mean(dyn * yn, axis=1, keepdims=True))
        g = dgated * sz
        dz_ref[...] = (dgated * ys * sg * (1.0 + z * (1.0 - sg))).astype(BF16)

        x = x_ref[...]
        dsk = d_ref[...]
        dd_ref[0] += jnp.sum(g * x, axis=0, keepdims=True)
        bb, cb_ = b_ref[...].astype(BF16), c_ref[...].astype(BF16)
        cbm = _dot(cb_, bb, _NT)
        cbt = _dot(bb, cb_, _NT)
        _, dtx, acx = _ssd_time2(dtraw_ref, bias_ref, alog_ref, sel_ref[0])
        xdt = x * dtx
        ex = jnp.exp(acx)
        last = acx[L - 1:L, :]
        te = jnp.exp(last - acx)
        dlast = jnp.exp(last)
        isa = _is_a((L, LANES))
        is_last = lax.broadcasted_iota(jnp.int32, (L, LANES), 0) == L - 1
        strict = lax.broadcasted_iota(jnp.int32, (L, L), 0) > lax.broadcasted_iota(jnp.int32, (L, L), 1)
        dcb = jnp.zeros((L, L), F32)
        dcm = jnp.zeros((L, SSM_STATE), F32)
        dbm = jnp.zeros((L, SSM_STATE), F32)
        for i in range(NS):
            sl = slice(i * LANES, (i + 1) * LANES)
            acs = acx[:, sl]
            acs_t = acs.T
            xs, gs = xdt[:, sl], g[:, sl]
            xsb = xs.astype(BF16)
            dxd = jnp.zeros((L, LANES), F32)
            dac_c = jnp.zeros((L, LANES), F32)
            for pos in (0, 1):
                gp = jnp.where(isa if pos == 0 else ~isa, gs, 0.0).astype(BF16)
                mt = (cbt * _decay(acs, acs_t, pos, True)).astype(BF16)
                dxd = dxd + _dot(mt, gp)
                dmd = _dot(gp, xsb, _NT) * _decay(acs, acs_t, pos, False)
                dcb = dcb + dmd
                q = jnp.where(strict, dmd * cbm, 0.0)
                hot = jnp.broadcast_to(hsel_ref[0, 2 * i + pos:2 * i + pos + 1, :], (L, LANES)).astype(BF16)
                dac_c = dac_c + _dot_split(q, hot, 2) - _dot_split(q, hot, 2, _TN)
            st = st_ref[0, i]
            dst = dstate[i]
            stb, dstb = st.astype(BF16), dst.astype(BF16)
            eg = (ex[:, sl] * gs).astype(BF16)
            dcm = dcm + _dot(eg, stb, _NT)
            yoff = _dot(cb_, stb) * ex[:, sl]
            w = xs * te[:, sl]
            wb = w.astype(BF16)
            dw = _dot(bb, dstb)
            dbm = dbm + _dot(wb, dstb, _NT)
            dxt = dxd + dw * te[:, sl]
            dal = dlast[:, sl] * jnp.sum(dst * st, axis=0, keepdims=True) + jnp.sum(dw * w, axis=0, keepdims=True)
            dac_l = gs * yoff - w * dw + jnp.where(is_last, dal, 0.0)
            ddt_l = dxt * x[:, sl]
            dstate[i] = dst * dlast[:, sl] + _dot(cb_, eg, _TN)
            dx_ref[:, sl] = dxt * dtx[:, sl] + dsk[:, sl] * gs
            part = _dot_split(ddt_l, selt_ref[0, sl, :], 2)
            parta = dac_c + _dot_split(dac_l, selt_ref[0, sl, :], 2)
            if i == 0:
                ddt_ref[0] = part
                dac_ref[0] = parta
            else:
                ddt_ref[0] += part
                dac_ref[0] += parta
        dcbb = dcb.astype(BF16)
        dc_ref[...] = dcm + _dot(dcbb, bb)
        db_ref[...] = dbm + _dot(dcbb, cb_, _TN)

    part_spec = pl.BlockSpec((1, L, LANES), lambda g, c: (g, rev(c), 0))
    lane_spec = pl.BlockSpec((1, 1, GW), lambda g, c: (g, 0, 0))
    bc_out = pl.BlockSpec((L, SSM_STATE), lambda g, c: (rev(c), g))
    return pl.pallas_call(
        body, name="ssd_bwd", grid=(G, nc),
        in_specs=[sp["grp"](rev), sp["bmat"](rev), sp["cmat"](rev), sp["grp"](rev), sp["dtraw"](rev), sp["grp"](rev),
                  sp["grp"](rev), pl.BlockSpec((1, NS, SSM_STATE, LANES), lambda g, c: (rev(c), g, 0, 0)),
                  sp["vec"], sp["vec"], sp["sel"], sp["selt"], pl.BlockSpec((1, 8, LANES), lambda g, c: (g, 0, 0)),
                  sp["gvec"], sp["gvec"]],
        out_specs=[sp["grp"](rev), sp["grp"](rev), bc_out, bc_out, part_spec, part_spec, lane_spec, lane_spec],
        out_shape=[jax.ShapeDtypeStruct((T, DI), BF16), jax.ShapeDtypeStruct((T, DI), F32),
                   jax.ShapeDtypeStruct((T, G * SSM_STATE), F32), jax.ShapeDtypeStruct((T, G * SSM_STATE), F32),
                   jax.ShapeDtypeStruct((G, T, LANES), F32), jax.ShapeDtypeStruct((G, T, LANES), F32),
                   jax.ShapeDtypeStruct((G, 1, GW), F32), jax.ShapeDtypeStruct((G, 1, GW), F32)],
        scratch_shapes=[pltpu.VMEM((NS, SSM_STATE, LANES), F32)],
        compiler_params=_params(("parallel", "arbitrary")),
    )(xc, xc, xc, zx, zx, yssd, dyo, states, bias, alog, sel, selt, hsel, dskip, ng)


def _ssd_dt_bwd(zx, ddt_part, dac_part, bias, alog):
    T, ZW = zx.shape
    G, L = SSM_GROUPS, SSM_CHUNK
    nc = T // L

    def body(dtraw_ref, ddt_ref, dac_ref, bias_ref, alog_ref, o_ref, dal_ref, dbias_ref):
        @pl.when(pl.program_id(0) == 0)
        def _():
            dal_ref[...] = jnp.zeros_like(dal_ref)
            dbias_ref[...] = jnp.zeros_like(dbias_ref)

        raw = dtraw_ref[...] + bias_ref[...]
        dt = jax.nn.softplus(raw)
        a = -jnp.exp(alog_ref[...])
        dac, ddt = dac_ref[0], ddt_ref[0]
        for gi in range(1, G):
            dac = dac + dac_ref[gi]
            ddt = ddt + ddt_ref[gi]
        dda = _tri_dot(dac, True)
        dal_ref[...] += jnp.sum(dda * dt, axis=0, keepdims=True) * a
        draw = (dda * a + ddt) * jax.nn.sigmoid(raw)
        dbias_ref[...] += jnp.sum(draw, axis=0, keepdims=True)
        o_ref[...] = draw.astype(BF16)

    vec = pl.BlockSpec((1, LANES), lambda c: (0, 0))
    part = pl.BlockSpec((G, L, LANES), lambda c: (0, c, 0))
    return pl.pallas_call(
        body, name="ssd_dt_bwd", grid=(nc,),
        in_specs=[pl.BlockSpec((L, LANES), lambda c: (c, ZW // LANES - 1)), part, part, vec, vec],
        out_specs=[pl.BlockSpec((L, LANES), lambda c: (c, 0)), vec, vec],
        out_shape=[jax.ShapeDtypeStruct((T, LANES), BF16), jax.ShapeDtypeStruct((1, LANES), F32),
                   jax.ShapeDtypeStruct((1, LANES), F32)],
        compiler_params=_params(("arbitrary",)),
    )(zx, ddt_part, dac_part, bias, alog)


def _slot(px, py, pc):
    return 4 * px + 2 * py + pc


def _all_gather(shards, name):
    n = len(shards)

    def body(*refs):
        ins, outs = refs[:n], refs[n:2 * n]
        send_sems, recv_sems, local_sems = refs[2 * n:]
        x, y, c = lax.axis_index("x"), lax.axis_index("y"), lax.axis_index("c")
        me, sibling = (x, y, c), (x, y, 1 - c)
        chips = [(1 - x, y), (x, 1 - y), (1 - x, 1 - y)]

        def copy(w, k, block, to, src=None):
            dst = outs[w].at[_slot(*block)]
            return pltpu.make_async_remote_copy(
                src_ref=dst if src is None else src, dst_ref=dst, send_sem=send_sems.at[w, k],
                recv_sem=recv_sems.at[w, k], device_id=to, device_id_type=MESH)

        mine, first, passed = [], [], []
        for w in range(n):
            cp = pltpu.make_async_copy(ins[w], outs[w].at[_slot(*me)], local_sems.at[w])
            cp.start()
            mine.append(cp)
            fw = [copy(w, 0, me, sibling, src=ins[w])]
            fw += [copy(w, 1 + j, me, (*chip, c), src=ins[w]) for j, chip in enumerate(chips)]
            for cp in fw:
                cp.start()
            first += fw
        for w in range(n):
            for j, chip in enumerate(chips):
                copy(w, 1 + j, (*chip, c), me).wait_recv()
                cp = copy(w, 4 + j, (*chip, c), sibling)
                cp.start()
                passed.append(cp)
        for w in range(n):
            copy(w, 0, sibling, me).wait_recv()
            for j, chip in enumerate(chips):
                copy(w, 4 + j, (*chip, 1 - c), me).wait_recv()
        for cp in first + passed:
            cp.wait_send()
        for cp in mine:
            cp.wait()

    any_spec = pl.BlockSpec(memory_space=pl.ANY)
    return pl.pallas_call(
        body, name=name, in_specs=[any_spec] * n, out_specs=[any_spec] * n,
        out_shape=[jax.ShapeDtypeStruct((N_DEV,) + s.shape, s.dtype) for s in shards],
        scratch_shapes=[pltpu.SemaphoreType.DMA((n, 7)), pltpu.SemaphoreType.DMA((n, 7)),
                        pltpu.SemaphoreType.DMA((n,))],
    )(*shards)


def _exchange(blocks, name):
    n = len(blocks)

    def body(*refs):
        ins, outs = refs[:n], refs[n:2 * n]
        send_sems, recv_sems, local_sems = refs[2 * n:]
        x, y, c = lax.axis_index("x"), lax.axis_index("y"), lax.axis_index("c")
        me = _slot(x, y, c)
        peers = [(x ^ (m >> 2), y ^ ((m >> 1) & 1), c ^ (m & 1)) for m in range(1, N_DEV)]
        copies, mine = [], []
        for w in range(n):
            cp = pltpu.make_async_copy(ins[w].at[me], outs[w].at[me], local_sems.at[w])
            cp.start()
            mine.append(cp)
            for k, peer in enumerate(peers):
                cp = pltpu.make_async_remote_copy(
                    src_ref=ins[w].at[_slot(*peer)], dst_ref=outs[w].at[me], send_sem=send_sems.at[w, k],
                    recv_sem=recv_sems.at[w, k], device_id=peer, device_id_type=MESH)
                cp.start()
                copies.append((cp, w, k, peer))
        for cp, w, k, peer in copies:
            pltpu.make_async_remote_copy(
                src_ref=ins[w].at[me], dst_ref=outs[w].at[_slot(*peer)], send_sem=send_sems.at[w, k],
                recv_sem=recv_sems.at[w, k], device_id=peer, device_id_type=MESH).wait_recv()
        for cp, w, k, peer in copies:
            cp.wait_send()
        for cp in mine:
            cp.wait()

    any_spec = pl.BlockSpec(memory_space=pl.ANY)
    return pl.pallas_call(
        body, name=name, in_specs=[any_spec] * n, out_specs=[any_spec] * n,
        out_shape=[jax.ShapeDtypeStruct(b.shape, b.dtype) for b in blocks],
        scratch_shapes=[pltpu.SemaphoreType.DMA((n, 7)), pltpu.SemaphoreType.DMA((n, 7)),
                        pltpu.SemaphoreType.DMA((n,))],
    )(*blocks)


def _adamw(parts, w, m, v, name):
    R, C = w.shape
    per_row = C * (N_DEV * parts.dtype.itemsize + 7 * 4) * 2
    tr = R
    if R % 8 == 0:
        tr = 8
        for t in (16, 32, 64, 128, 256, 512):
            if R % t == 0 and t * per_row <= 24 * 1024 * 1024:
                tr = t
    c1 = 1.0 - ADAM_B1 ** ADAM_STEP
    c2 = 1.0 - ADAM_B2 ** ADAM_STEP

    def body(p_ref, w_ref, m_ref, v_ref, g_ref, d_ref, nm_ref, nv_ref):
        g = p_ref[0].astype(F32)
        for k in range(1, N_DEV):
            g = g + p_ref[k].astype(F32)
        nm = ADAM_B1 * m_ref[...] + (1.0 - ADAM_B1) * g
        nv = ADAM_B2 * v_ref[...] + (1.0 - ADAM_B2) * (g * g)
        g_ref[...] = g
        nm_ref[...] = nm
        nv_ref[...] = nv
        d_ref[...] = -ADAM_LR * ((nm / c1) / (jnp.sqrt(nv / c2) + ADAM_EPS) + ADAM_WD * w_ref[...])

    blk = pl.BlockSpec((tr, C), lambda i: (i, 0))
    out = jax.ShapeDtypeStruct((R, C), F32)
    return pl.pallas_call(
        body, name=name, grid=(R // tr,),
        in_specs=[pl.BlockSpec((N_DEV, tr, C), lambda i: (0, i, 0)), blk, blk, blk],
        out_specs=[blk, blk, blk, blk], out_shape=[out, out, out, out],
        compiler_params=_params(("parallel",)),
    )(parts, w, m, v)


def _pad_cols(a, n):
    return jnp.pad(a, ((0, 0), (0, n - a.shape[1])))


def kernel(x, positions, mixer_norm, ffn_norm, attn_w_qkv, attn_q_norm, attn_k_norm, attn_sinks, attn_w_o, ssm_w_in, ssm_conv_w, ssm_conv_b, ssm_dt_bias, ssm_a_log, ssm_d, ssm_norm, ssm_w_out, ffn_w_gate, ffn_w_up, ffn_w_down, loss_target, m_mixer_norm, m_ffn_norm, m_attn_w_qkv, m_attn_q_norm, m_attn_k_norm, m_attn_sinks, m_attn_w_o, m_ssm_w_in, m_ssm_conv_w, m_ssm_conv_b, m_ssm_dt_bias, m_ssm_a_log, m_ssm_d, m_ssm_norm, m_ssm_w_out, m_ffn_w_gate, m_ffn_w_up, m_ffn_w_down, v_mixer_norm, v_ffn_norm, v_attn_w_qkv, v_attn_q_norm, v_attn_k_norm, v_attn_sinks, v_attn_w_o, v_ssm_w_in, v_ssm_conv_w, v_ssm_conv_b, v_ssm_dt_bias, v_ssm_a_log, v_ssm_d, v_ssm_norm, v_ssm_w_out, v_ffn_w_gate, v_ffn_w_up, v_ffn_w_down):
    T, D = x.shape[1], x.shape[2]
    HQ = D // HEAD
    HKV = HQ // ATT_GROUP
    QW = (HQ + 2 * HKV) * HEAD
    DI = 2 * D
    H = DI // HEAD
    G = SSM_GROUPS
    GW = DI // G
    CD = DI + 2 * G * SSM_STATE
    ZW = DI + CD + LANES
    FF = ffn_w_gate.shape[2] * N_DEV
    IW = DI + CD + H
    assert T % 512 == 0 and D % 256 == 0 and HKV % 2 == 0 and GW % LANES == 0 and H <= LANES

    weights = dict(mixer_norm=mixer_norm, ffn_norm=ffn_norm, attn_w_qkv=attn_w_qkv, attn_q_norm=attn_q_norm,
                   attn_k_norm=attn_k_norm, attn_sinks=attn_sinks, attn_w_o=attn_w_o, ssm_w_in=ssm_w_in,
                   ssm_conv_w=ssm_conv_w, ssm_conv_b=ssm_conv_b, ssm_dt_bias=ssm_dt_bias, ssm_a_log=ssm_a_log,
                   ssm_d=ssm_d, ssm_norm=ssm_norm, ssm_w_out=ssm_w_out, ffn_w_gate=ffn_w_gate, ffn_w_up=ffn_w_up,
                   ffn_w_down=ffn_w_down)
    mom_m = dict(mixer_norm=m_mixer_norm, ffn_norm=m_ffn_norm, attn_w_qkv=m_attn_w_qkv, attn_q_norm=m_attn_q_norm,
                 attn_k_norm=m_attn_k_norm, attn_sinks=m_attn_sinks, attn_w_o=m_attn_w_o, ssm_w_in=m_ssm_w_in,
                 ssm_conv_w=m_ssm_conv_w, ssm_conv_b=m_ssm_conv_b, ssm_dt_bias=m_ssm_dt_bias, ssm_a_log=m_ssm_a_log,
                 ssm_d=m_ssm_d, ssm_norm=m_ssm_norm, ssm_w_out=m_ssm_w_out, ffn_w_gate=m_ffn_w_gate,
                 ffn_w_up=m_ffn_w_up, ffn_w_down=m_ffn_w_down)
    mom_v = dict(mixer_norm=v_mixer_norm, ffn_norm=v_ffn_norm, attn_w_qkv=v_attn_w_qkv, attn_q_norm=v_attn_q_norm,
                 attn_k_norm=v_attn_k_norm, attn_sinks=v_attn_sinks, attn_w_o=v_attn_w_o, ssm_w_in=v_ssm_w_in,
                 ssm_conv_w=v_ssm_conv_w, ssm_conv_b=v_ssm_conv_b, ssm_dt_bias=v_ssm_dt_bias, ssm_a_log=v_ssm_a_log,
                 ssm_d=v_ssm_d, ssm_norm=v_ssm_norm, ssm_w_out=v_ssm_w_out, ffn_w_gate=v_ffn_w_gate,
                 ffn_w_up=v_ffn_w_up, ffn_w_down=v_ffn_w_down)
    big = ["attn_w_qkv", "attn_w_o", "ssm_w_in", "ssm_w_out", "ffn_w_gate", "ffn_w_up", "ffn_w_down"]
    col_split = {"attn_w_qkv", "ssm_w_in", "ffn_w_gate", "ffn_w_up"}

    def flat2(a):
        return a.reshape(-1, a.shape[-1])

    gathered = _all_gather([flat2(weights[n]).astype(BF16) for n in big], "gather_weights")
    full = {}
    for n, gth in zip(big, gathered):
        nl = weights[n].shape[0]
        if n in col_split:
            r, c = weights[n].shape[1:]
            full[n] = gth.reshape(N_DEV, nl, r, c).transpose(1, 2, 0, 3).reshape(nl, r, N_DEV * c)
        else:
            r, c = weights[n].shape[1:]
            full[n] = gth.reshape(N_DEV, nl, r, c).transpose(1, 0, 2, 3).reshape(nl, N_DEV * r, c)
    w_qkv, w_o, w_out = full["attn_w_qkv"][0], full["attn_w_o"][0], full["ssm_w_out"][0]
    w_in = _pad_cols(full["ssm_w_in"][0], ZW)

    xs = x[0]
    tgt = loss_target[0]
    inv_freq = ROPE_THETA ** (-jnp.arange(0, HEAD, 2, dtype=F32) / HEAD)
    ang = positions[0].astype(F32)[:, None] * inv_freq
    cos = jnp.tile(jnp.cos(ang), (1, 4))
    sin = jnp.tile(jnp.concatenate([-jnp.sin(ang), jnp.sin(ang)], axis=1), (1, 2))
    gq = jnp.tile(attn_q_norm, (1, 2))
    gk = jnp.tile(attn_k_norm, (1, 2))
    sinkcol = jnp.repeat(attn_sinks.reshape(HKV, ATT_GROUP // 2, 2).transpose(0, 2, 1), WINDOW, axis=2)[..., None]
    bias_p = _pad_cols(ssm_dt_bias, LANES)
    alog_p = _pad_cols(ssm_a_log, LANES)
    dskip = jnp.repeat(ssm_d, HEAD, axis=1)
    lane_head = jnp.arange(DI) // HEAD
    sel = (jnp.arange(LANES)[None, :, None] == lane_head.reshape(G, 1, GW)).astype(BF16)
    selt = sel.transpose(0, 2, 1)
    hsel = (jnp.arange(LANES)[None, None, :] == (jnp.arange(G)[:, None, None] * (H // G) + jnp.arange(8)[None, :, None])
            ).astype(BF16) * (jnp.arange(8)[None, :, None] < H // G)
    vec_w = CD // N_DEV
    small = jnp.concatenate([ssm_conv_w[0], ssm_conv_b, _pad_cols(ssm_norm, vec_w),
                             jnp.zeros((2, vec_w), F32)], axis=0)
    small_all = _all_gather([small], "gather_vectors")[0]
    conv_w = small_all[:, :SSM_CONV].transpose(1, 0, 2).reshape(SSM_CONV, CD)
    conv_b = small_all[:, SSM_CONV].reshape(1, CD)
    ng = small_all[:, SSM_CONV + 1, :DI // N_DEV].reshape(1, DI)

    hm0 = _rms_fwd(xs, mixer_norm[0:1], "rms_fwd_m0")
    qkv = _matmul(hm0, w_qkv, mode="nn", out_dtype=F32, name="mm_qkv")
    qr, kd, vd = _attn_prep_fwd(qkv, cos, sin, gq, gk, D, HKV)
    o = _attn_fwd(qr, kd, vd, sinkcol, HKV)
    x1 = _matmul(o, w_o, mode="nn", out_dtype=F32, name="mm_attn_out", add=xs)
    hf0 = _rms_fwd(x1, ffn_norm[0:1], "rms_fwd_f0")
    gate0, up0, act0 = _ffn_up(hf0, full["ffn_w_gate"][0], full["ffn_w_up"][0], "ffn_up_0")
    x2 = _matmul(act0, full["ffn_w_down"][0], mode="nn", out_dtype=F32, name="mm_ffn_down_0", add=x1)
    hm1 = _rms_fwd(x2, mixer_norm[1:2], "rms_fwd_m1")
    zx = _matmul(hm1, w_in, mode="nn", out_dtype=F32, name="mm_ssm_in")
    xc = _conv_fwd(zx, conv_w, conv_b, DI, CD)
    yssd, yout, states = _ssd_fwd(zx, xc, bias_p, alog_p, sel, dskip, ng, DI)
    x3 = _matmul(yout, w_out, mode="nn", out_dtype=F32, name="mm_ssm_out", add=x2)
    hf1 = _rms_fwd(x3, ffn_norm[1:2], "rms_fwd_f1")
    gate1, up1, act1 = _ffn_up(hf1, full["ffn_w_gate"][1], full["ffn_w_up"][1], "ffn_up_1")
    x4 = _matmul(act1, full["ffn_w_down"][1], mode="nn", out_dtype=F32, name="mm_ffn_down_1", add=x3)
    sq, dx4, dx4b = _loss_head(x4, tgt)
    loss = lax.psum(sq[0, 0] * (0.5 / D), ("x", "y", "c"))

    def ffn_bwd(dy, dyb, hf, gate, up, act, layer, xin, gain):
        dg, du = _ffn_dact(dyb, full["ffn_w_down"][layer], gate, up, f"ffn_dact_{layer}")
        g_down = _matmul(act, dyb, mode="tn", out_dtype=BF16, name=f"mm_dw_down_{layer}")
        g_gate = _matmul(hf, dg, mode="tn", out_dtype=BF16, name=f"mm_dw_gate_{layer}")
        g_up = _matmul(hf, du, mode="tn", out_dtype=BF16, name=f"mm_dw_up_{layer}")
        dh = _matmul(dg, full["ffn_w_gate"][layer], mode="nt", out_dtype=F32, name=f"mm_dh_gate_{layer}")
        dh = _matmul(du, full["ffn_w_up"][layer], mode="nt", out_dtype=F32, name=f"mm_dh_up_{layer}", add=dh)
        dx, dxb, dgain = _rms_bwd(xin, gain, dh, dy, f"rms_bwd_f{layer}")
        return dx, dxb, dgain, g_gate, g_up, g_down

    dx3, dx3b, d_fn1, g_gate1, g_up1, g_down1 = ffn_bwd(dx4, dx4b, hf1, gate1, up1, act1, 1, x3, ffn_norm[1:2])
    dyo = _matmul(dx3b, w_out, mode="nt", out_dtype=F32, name="mm_dyout")
    g_wout = _matmul(yout, dx3b, mode="tn", out_dtype=BF16, name="mm_dw_ssm_out")
    dz, dxx, dbm, dcm, ddt_p, dac_p, dd_l, dng_l = _ssd_bwd(zx, xc, yssd, dyo, states, bias_p, alog_p, sel, selt,
                                                            hsel, dskip, ng, DI)
    ddt_raw, d_alog, d_bias = _ssd_dt_bwd(zx, ddt_p, dac_p, bias_p, alog_p)
    dxbc, d_convw, d_convb = _conv_bwd(zx, jnp.concatenate([dxx, dbm, dcm], axis=1), conv_w, conv_b, DI, CD)
    dzx = jnp.concatenate([dz, dxbc, ddt_raw], axis=1)
    g_win = _matmul(hm1, dzx, mode="tn", out_dtype=BF16, name="mm_dw_ssm_in")[:, :IW]
    dh = _matmul(dzx, w_in, mode="nt", out_dtype=F32, name="mm_dh_ssm_in")
    dx2, dx2b, d_mn1 = _rms_bwd(x2, mixer_norm[1:2], dh, dx3, "rms_bwd_m1")
    dx1, dx1b, d_fn0, g_gate0, g_up0, g_down0 = ffn_bwd(dx2, dx2b, hf0, gate0, up0, act0, 0, x1, ffn_norm[0:1])
    do = _matmul(dx1b, w_o, mode="nt", out_dtype=F32, name="mm_do")
    g_wo = _matmul(o, dx1b, mode="tn", out_dtype=BF16, name="mm_dw_attn_out")
    dq, dkd, dvd, dsink = _attn_bwd(qr, kd, vd, o, do, sinkcol, HKV)
    dqkv, dgq_l, dgk_l = _attn_prep_bwd(qkv, dq, dkd, dvd, cos, sin, gq, gk, D, HKV)
    g_wqkv = _matmul(hm0, dqkv, mode="tn", out_dtype=BF16, name="mm_dw_qkv")
    dh = _matmul(dqkv, w_qkv, mode="nt", out_dtype=F32, name="mm_dh_qkv")
    dx0, _, d_mn0 = _rms_bwd(xs, mixer_norm[0:1], dh, dx1, "rms_bwd_m0")

    def rows_to_blocks(parts):
        st = jnp.stack([p.reshape(N_DEV, p.shape[0] // N_DEV, p.shape[1]) for p in parts], axis=1)
        return st.reshape(N_DEV, -1, st.shape[-1])

    def cols_to_blocks(parts):
        st = jnp.stack([p.reshape(p.shape[0], N_DEV, p.shape[1] // N_DEV).transpose(1, 0, 2) for p in parts], axis=1)
        return st.reshape(N_DEV, -1, st.shape[-1])

    send = {
        "attn_w_qkv": cols_to_blocks([g_wqkv]), "attn_w_o": rows_to_blocks([g_wo]),
        "ssm_w_in": cols_to_blocks([g_win]), "ssm_w_out": rows_to_blocks([g_wout]),
        "ffn_w_gate": cols_to_blocks([g_gate0, g_gate1]), "ffn_w_up": cols_to_blocks([g_up0, g_up1]),
        "ffn_w_down": rows_to_blocks([g_down0, g_down1]),
    }
    d_ng = dng_l.reshape(1, DI)
    vec_send = jnp.concatenate([
        d_convw.reshape(SSM_CONV, N_DEV, vec_w).transpose(1, 0, 2), d_convb.reshape(1, N_DEV, vec_w).transpose(1, 0, 2),
        _pad_cols(d_ng.reshape(N_DEV, DI // N_DEV), vec_w)[:, None, :], jnp.zeros((N_DEV, 2, vec_w), F32)], axis=1)
    d_sinks = dsink[:, :, 0].reshape(1, HQ)
    d_gq = dgq_l[:, :HEAD] + dgq_l[:, HEAD:]
    d_gk = dgk_l[:, :HEAD] + dgk_l[:, HEAD:]
    d_dskip = dd_l.reshape(H, HEAD).sum(axis=1).reshape(1, H)
    rep_names = ["mixer_norm", "ffn_norm", "attn_q_norm", "attn_k_norm", "attn_sinks", "ssm_dt_bias", "ssm_a_log",
                 "ssm_d"]
    rep_grads = [jnp.concatenate([d_mn0, d_mn1], axis=0), jnp.concatenate([d_fn0, d_fn1], axis=0), d_gq, d_gk,
                 d_sinks, d_bias[:, :H], d_alog[:, :H], d_dskip]
    rep_sizes = [weights[n].size for n in rep_names]
    rep_len = -(-sum(rep_sizes) // (8 * LANES)) * 8 * LANES

    def pack(arrs):
        flat = jnp.concatenate([a.reshape(-1) for a in arrs])
        return jnp.pad(flat, (0, rep_len - flat.shape[0])).reshape(rep_len // LANES, LANES)

    rep_send = jnp.broadcast_to(pack(rep_grads)[None], (N_DEV, rep_len // LANES, LANES))
    recv = _exchange([send[n] for n in big] + [vec_send, rep_send], "exchange_grads")

    out = {}
    for n, parts in zip(big, recv[:len(big)]):
        shp = weights[n].shape
        res = _adamw(parts, flat2(weights[n]), flat2(mom_m[n]), flat2(mom_v[n]), f"adamw_{n}")
        out[n] = [r.reshape(shp) for r in res]
    def vec_block(d):
        return jnp.concatenate([d["ssm_conv_w"][0], d["ssm_conv_b"], _pad_cols(d["ssm_norm"], vec_w),
                                jnp.zeros((2, vec_w), F32)], axis=0)

    res = _adamw(recv[len(big)], vec_block(weights), vec_block(mom_m), vec_block(mom_v), "adamw_vectors")
    out["ssm_conv_w"] = [r[:SSM_CONV][None] for r in res]
    out["ssm_conv_b"] = [r[SSM_CONV:SSM_CONV + 1] for r in res]
    out["ssm_norm"] = [r[SSM_CONV + 1:SSM_CONV + 2, :DI // N_DEV] for r in res]
    res = _adamw(recv[len(big) + 1], pack([weights[n] for n in rep_names]), pack([mom_m[n] for n in rep_names]),
                 pack([mom_v[n] for n in rep_names]), "adamw_replicated")
    offs = 0
    for n, sz in zip(rep_names, rep_sizes):
        out[n] = [r.reshape(-1)[offs:offs + sz].reshape(weights[n].shape) for r in res]
        offs += sz

    names = list(weights)
    return (loss, dx0[None], *[out[n][0] for n in names], *[out[n][1] for n in names],
            *[out[n][2] for n in names], *[out[n][3] for n in names])
```

```python
import functools
import math

import jax
import jax.numpy as jnp
from jax import lax
from jax.experimental import pallas as pl
from jax.experimental.pallas import tpu as pltpu

F32 = jnp.float32
BF16 = jnp.bfloat16

N_DEV = 8
EPS = 1e-6
LANES = 128
HEAD = 64
ATT_GROUP = 8
ATT_GW = ATT_GROUP * HEAD
WINDOW = 128
ROPE_THETA = 10000.0
SSM_GROUPS = 8
SSM_STATE = 128
SSM_CONV = 4
SSM_CHUNK = 256
HALO = 8
ADAM_LR, ADAM_B1, ADAM_B2, ADAM_EPS, ADAM_WD, ADAM_STEP = 0.001, 0.9, 0.999, 1e-08, 0.01, 10
VMEM_LIMIT = 56 * 1024 * 1024
MATMUL_VMEM = 44 * 1024 * 1024
MESH = pl.DeviceIdType.MESH

_NN = (((1,), (0,)), ((), ()))
_NT = (((1,), (1,)), ((), ()))
_TN = (((0,), (0,)), ((), ()))


def _dot(a, b, dims=_NN):
    return lax.dot_general(a, b, dims, preferred_element_type=F32)


def _tile(n, cap):
    if n % LANES:
        return n
    best = LANES
    for t in range(LANES, min(n, cap) + 1, LANES):
        if n % t == 0:
            best = t
    return best


def _params(sem):
    return pltpu.CompilerParams(dimension_semantics=sem, vmem_limit_bytes=VMEM_LIMIT)


def _slot(px, py, pc):
    return 4 * px + 2 * py + pc


def _direct_copies(srcs, dsts, send_sems, recv_sems, local_sems):
    x, y, c = lax.axis_index("x"), lax.axis_index("y"), lax.axis_index("c")
    me = _slot(x, y, c)
    peers = [(x ^ (m >> 2), y ^ ((m >> 1) & 1), c ^ (m & 1)) for m in range(1, N_DEV)]
    local, sends, arrivals = [], [], []
    for w, (src, dst) in enumerate(zip(srcs, dsts)):
        sliced = src.shape == dst.shape
        local.append(pltpu.make_async_copy(src.at[me] if sliced else src, dst.at[me], local_sems.at[w]))
        for k, peer in enumerate(peers):
            sems = dict(send_sem=send_sems.at[w, k], recv_sem=recv_sems.at[w, k], device_id=peer, device_id_type=MESH)
            sends.append(pltpu.make_async_remote_copy(
                src_ref=src.at[_slot(*peer)] if sliced else src, dst_ref=dst.at[me], **sems))
            arrivals.append(pltpu.make_async_remote_copy(
                src_ref=src.at[me] if sliced else src, dst_ref=dst.at[_slot(*peer)], **sems))
    return local, sends, arrivals


def _call(body, *, name, grid, in_specs, out_specs, out_shape, sem, args, scratch_shapes=(), carry=None):
    if carry is None:
        return pl.pallas_call(body, name=name, grid=grid, in_specs=in_specs, out_specs=out_specs, out_shape=out_shape,
                              scratch_shapes=list(scratch_shapes), compiler_params=_params(sem))(*args)
    n_in, n_out, n_sc, n_c = len(in_specs), len(out_specs), len(scratch_shapes), len(carry)
    recv_shape = [jax.ShapeDtypeStruct(a.shape if a.shape[0] == N_DEV and a.ndim == 3 else (N_DEV,) + a.shape, a.dtype)
                  for a in carry]

    def wrapped(*refs):
        ins, c_in = refs[:n_in], refs[n_in:n_in + n_c]
        outs, c_out = refs[n_in + n_c:n_in + n_c + n_out], refs[n_in + n_c + n_out:n_in + 2 * n_c + n_out]
        scr = refs[n_in + 2 * n_c + n_out:n_in + 2 * n_c + n_out + n_sc]
        send_sems, recv_sems, local_sems = refs[-3:]
        first = functools.reduce(jnp.logical_and, [pl.program_id(d) == 0 for d in range(len(grid))])
        last = functools.reduce(jnp.logical_and, [pl.program_id(d) == n - 1 for d, n in enumerate(grid)])

        @pl.when(first)
        def _():
            local, sends, _ = _direct_copies(c_in, c_out, send_sems, recv_sems, local_sems)
            for cp in local + sends:
                cp.start()

        body(*ins, *outs, *scr)

        @pl.when(last)
        def _():
            local, sends, arrivals = _direct_copies(c_in, c_out, send_sems, recv_sems, local_sems)
            for cp in arrivals:
                cp.wait_recv()
            for cp in sends:
                cp.wait_send()
            for cp in local:
                cp.wait()

    any_spec = pl.BlockSpec(memory_space=pl.ANY)
    res = pl.pallas_call(
        wrapped, name=name, grid=grid, in_specs=list(in_specs) + [any_spec] * n_c,
        out_specs=list(out_specs) + [any_spec] * n_c, out_shape=list(out_shape) + recv_shape,
        scratch_shapes=list(scratch_shapes) + [pltpu.SemaphoreType.DMA((n_c, N_DEV - 1)),
                                               pltpu.SemaphoreType.DMA((n_c, N_DEV - 1)), pltpu.SemaphoreType.DMA((n_c,))],
        compiler_params=_params(("arbitrary",) * len(grid)),
    )(*args, *carry)
    return res


def _matmul(a, b, *, mode, out_dtype, name, add=None, carry=None):
    if mode == "nn":
        (M, K), N = a.shape, b.shape[1]
    elif mode == "nt":
        (M, K), N = a.shape, b.shape[0]
    else:
        (K, M), N = a.shape, b.shape[1]
    assert a.dtype == BF16 and b.dtype == BF16
    has_add = add is not None
    tm, tn = (_tile(M, 512), _tile(N, 512)) if mode == "tn" else (_tile(M, 1024), _tile(N, 512))
    fixed = 2 * tm * tn * (jnp.dtype(out_dtype).itemsize + (4 if has_add else 0)) + tm * tn * 4
    per_k = 2 * 2 * (tm + tn) + (2 * tm if mode == "tn" else 0)
    tk = _tile(K, max(LANES, (MATMUL_VMEM - fixed) // per_k))
    nk = K // tk
    dims = _NT if mode == "nt" else _NN
    if mode == "tn":
        a_spec = pl.BlockSpec((tk, tm), lambda i, j, k: (jnp.where(j == 0, k, 0), i))
    else:
        a_spec = pl.BlockSpec((tm, tk), lambda i, j, k: (i, k))
    b_spec = pl.BlockSpec((tn, tk), lambda i, j, k: (j, k)) if mode == "nt" else pl.BlockSpec((tk, tn), lambda i, j, k: (k, j))
    o_spec = pl.BlockSpec((tm, tn), lambda i, j, k: (i, j))

    def body(*refs):
        a_ref, b_ref = refs[:2]
        add_ref = refs[2] if has_add else None
        o_ref = refs[2 + has_add]
        scratch = list(refs[3 + has_add:])
        at = scratch.pop(0) if mode == "tn" else None
        acc = scratch.pop(0) if nk > 1 else None
        j, k = pl.program_id(1), pl.program_id(2)
        if mode == "tn":
            @pl.when(j == 0)
            def _():
                at[k] = a_ref[...].T

            part = _dot(at[k], b_ref[...], dims)
        else:
            part = _dot(a_ref[...], b_ref[...], dims)

        def finish(r):
            if has_add:
                r = r + add_ref[...]
            o_ref[...] = r.astype(out_dtype)

        if nk == 1:
            finish(part)
        else:
            @pl.when(k == 0)
            def _():
                acc[...] = part

            @pl.when(jnp.logical_and(k > 0, k < nk - 1))
            def _():
                acc[...] += part

            @pl.when(k == nk - 1)
            def _():
                finish(acc[...] + part)

    scratch = ([pltpu.VMEM((nk, tm, tk), BF16)] if mode == "tn" else []) + ([pltpu.VMEM((tm, tn), F32)] if nk > 1 else [])
    res = _call(
        body, name=name, grid=(M // tm, N // tn, nk),
        in_specs=[a_spec, b_spec] + ([o_spec] if has_add else []),
        out_specs=[o_spec], out_shape=[jax.ShapeDtypeStruct((M, N), out_dtype)],
        scratch_shapes=scratch, sem=("parallel", "arbitrary", "arbitrary"),
        args=(a, b, add) if has_add else (a, b), carry=carry)
    return res[0] if carry is None else (res[0], res[1:])


def _rms_fwd(x, gain, name):
    T, D = x.shape
    tr = 256

    def body(x_ref, g_ref, h_ref):
        xv = x_ref[...]
        rstd = lax.rsqrt(jnp.mean(xv * xv, axis=1, keepdims=True) + EPS)
        h_ref[...] = (xv * rstd * g_ref[...]).astype(BF16)

    return pl.pallas_call(
        body, name=name, grid=(T // tr,),
        in_specs=[pl.BlockSpec((tr, D), lambda i: (i, 0)), pl.BlockSpec((1, D), lambda i: (0, 0))],
        out_specs=pl.BlockSpec((tr, D), lambda i: (i, 0)),
        out_shape=jax.ShapeDtypeStruct((T, D), BF16),
        compiler_params=_params(("parallel",)),
    )(x, gain)


def _rms_bwd(x, gain, dh, dres, name):
    T, D = x.shape
    tr = 256

    def body(x_ref, g_ref, dh_ref, dr_ref, dx_ref, dxb_ref, dg_ref):
        @pl.when(pl.program_id(0) == 0)
        def _():
            dg_ref[...] = jnp.zeros_like(dg_ref)

        xv = x_ref[...]
        rstd = lax.rsqrt(jnp.mean(xv * xv, axis=1, keepdims=True) + EPS)
        xhat = xv * rstd
        dy = dh_ref[...].astype(F32)
        dg_ref[...] += jnp.sum(dy * xhat, axis=0, keepdims=True)
        dxh = dy * g_ref[...]
        dx = dr_ref[...] + rstd * (dxh - xhat * jnp.mean(dxh * xhat, axis=1, keepdims=True))
        dx_ref[...] = dx
        dxb_ref[...] = dx.astype(BF16)

    row = pl.BlockSpec((tr, D), lambda i: (i, 0))
    vec = pl.BlockSpec((1, D), lambda i: (0, 0))
    return pl.pallas_call(
        body, name=name, grid=(T // tr,), in_specs=[row, vec, row, row], out_specs=[row, row, vec],
        out_shape=[jax.ShapeDtypeStruct((T, D), F32), jax.ShapeDtypeStruct((T, D), BF16),
                   jax.ShapeDtypeStruct((1, D), F32)],
        compiler_params=_params(("arbitrary",)),
    )(x, gain, dh, dres)


def _loss_head(y, target):
    T, D = y.shape
    tr = 256

    def body(y_ref, t_ref, s_ref, d_ref, db_ref):
        @pl.when(pl.program_id(0) == 0)
        def _():
            s_ref[...] = jnp.zeros_like(s_ref)

        e = y_ref[...] - t_ref[...]
        s_ref[...] += jnp.sum(jnp.sum(e * e, axis=1, keepdims=True), axis=0, keepdims=True)
        d = e * (1.0 / D)
        d_ref[...] = d
        db_ref[...] = d.astype(BF16)

    row = pl.BlockSpec((tr, D), lambda i: (i, 0))
    return pl.pallas_call(
        body, name="loss_head", grid=(T // tr,), in_specs=[row, row],
        out_specs=[pl.BlockSpec((1, 1), lambda i: (0, 0)), row, row],
        out_shape=[jax.ShapeDtypeStruct((1, 1), F32), jax.ShapeDtypeStruct((T, D), F32),
                   jax.ShapeDtypeStruct((T, D), BF16)],
        compiler_params=_params(("arbitrary",)),
    )(y, target)


def _ffn_up(h, wg, wu, name, carry=None):
    (T, D), Fd = h.shape, wg.shape[1]
    tm, tn = _tile(T, 1024), _tile(Fd, 512)

    def body(h_ref, wg_ref, wu_ref, g_ref, u_ref, a_ref):
        hv = h_ref[...]
        g = _dot(hv, wg_ref[...])
        g_ref[...] = g
        sg = g * jax.nn.sigmoid(g)
        u = _dot(hv, wu_ref[...])
        u_ref[...] = u
        a_ref[...] = (sg * u).astype(BF16)

    w_spec = pl.BlockSpec((D, tn), lambda i, j: (0, j))
    o_spec = pl.BlockSpec((tm, tn), lambda i, j: (i, j))
    res = _call(
        body, name=name, grid=(T // tm, Fd // tn),
        in_specs=[pl.BlockSpec((tm, D), lambda i, j: (i, 0)), w_spec, w_spec],
        out_specs=[o_spec, o_spec, o_spec],
        out_shape=[jax.ShapeDtypeStruct((T, Fd), F32), jax.ShapeDtypeStruct((T, Fd), F32),
                   jax.ShapeDtypeStruct((T, Fd), BF16)],
        sem=("parallel", "arbitrary"), args=(h, wg, wu), carry=carry)
    return res if carry is None else (*res[:3], res[3:])


def _ffn_dact(dy, wd, gate, up, name):
    (T, D), Fd = dy.shape, wd.shape[0]
    tm, tn = _tile(T, 1024), _tile(Fd, 512)

    def body(dy_ref, wd_ref, g_ref, u_ref, dg_ref, du_ref):
        da = _dot(dy_ref[...], wd_ref[...], _NT)
        g = g_ref[...]
        sg = jax.nn.sigmoid(g)
        du_ref[...] = (da * g * sg).astype(BF16)
        dg_ref[...] = (da * u_ref[...] * sg * (1.0 + g * (1.0 - sg))).astype(BF16)

    o_spec = pl.BlockSpec((tm, tn), lambda i, j: (i, j))
    return pl.pallas_call(
        body, name=name, grid=(T // tm, Fd // tn),
        in_specs=[pl.BlockSpec((tm, D), lambda i, j: (i, 0)), pl.BlockSpec((tn, D), lambda i, j: (j, 0)),
                  o_spec, o_spec],
        out_specs=[o_spec, o_spec],
        out_shape=[jax.ShapeDtypeStruct((T, Fd), BF16), jax.ShapeDtypeStruct((T, Fd), BF16)],
        compiler_params=_params(("parallel", "arbitrary")),
    )(dy, wd, gate, up)


def _is_a(shape):
    return lax.broadcasted_iota(jnp.int32, shape, 1) % LANES < HEAD


def _split2(v):
    hi = v.astype(BF16)
    return hi, (v - hi.astype(F32)).astype(BF16)


def _split3(v):
    hi = v.astype(BF16)
    r = v - hi.astype(F32)
    mid = r.astype(BF16)
    return hi, mid, (r - mid.astype(F32)).astype(BF16)


def _dot_split(v, m, pieces, dims=_NN):
    parts = _split3(v) if pieces == 3 else _split2(v)
    out = _dot(parts[0], m, dims)
    for p in parts[1:]:
        out = out + _dot(p, m, dims)
    return out


def _head_blockdiag():
    r = lax.broadcasted_iota(jnp.int32, (LANES, LANES), 0) // HEAD
    c = lax.broadcasted_iota(jnp.int32, (LANES, LANES), 1) // HEAD
    return (r == c).astype(BF16)


def _swap_half(v):
    lane = lax.broadcasted_iota(jnp.int32, v.shape, 1)
    return jnp.where(lane % HEAD < HEAD // 2, pltpu.roll(v, LANES - HEAD // 2, axis=1), pltpu.roll(v, HEAD // 2, axis=1))


def _attn_prep_fwd(qkv, cos, sin, gq, gk, D, HKV):
    T, QW = qkv.shape
    tr = 256
    nq, nk = D // LANES, HKV // 2
    KW = HKV * LANES

    def body(x_ref, cos_ref, sin_ref, gq_ref, gk_ref, q_ref, k_ref, v_ref):
        bd = _head_blockdiag()
        cs, sn = cos_ref[...], sin_ref[...]
        isa = _is_a((tr, LANES))

        def normrope(xv, g):
            ms = _dot_split(xv * xv, bd, 2) * (1.0 / HEAD)
            xn = xv * lax.rsqrt(ms + EPS) * g
            return xn * cs + _swap_half(xn) * sn

        def dup(v):
            r = pltpu.roll(v, HEAD, axis=1)
            return jnp.where(isa, v, r), jnp.where(isa, r, v)

        for s in range(nq):
            sl = slice(s * LANES, (s + 1) * LANES)
            q_ref[:, sl] = normrope(x_ref[:, sl], gq_ref[...]).astype(BF16)
        for s in range(nk):
            ka, kb = dup(normrope(x_ref[:, D + s * LANES:D + (s + 1) * LANES], gk_ref[...]))
            k_ref[:, 2 * s * LANES:(2 * s + 1) * LANES] = ka.astype(BF16)
            k_ref[:, (2 * s + 1) * LANES:(2 * s + 2) * LANES] = kb.astype(BF16)
            va, vb = dup(x_ref[:, D + (nk + s) * LANES:D + (nk + s + 1) * LANES])
            v_ref[:, 2 * s * LANES:(2 * s + 1) * LANES] = va.astype(BF16)
            v_ref[:, (2 * s + 1) * LANES:(2 * s + 2) * LANES] = vb.astype(BF16)

    tab = pl.BlockSpec((tr, LANES), lambda i: (i, 0))
    vec = pl.BlockSpec((1, LANES), lambda i: (0, 0))
    return pl.pallas_call(
        body, name="attn_prep_fwd", grid=(T // tr,),
        in_specs=[pl.BlockSpec((tr, QW), lambda i: (i, 0)), tab, tab, vec, vec],
        out_specs=[pl.BlockSpec((tr, D), lambda i: (i, 0)), pl.BlockSpec((tr, KW), lambda i: (i, 0)),
                   pl.BlockSpec((tr, KW), lambda i: (i, 0))],
        out_shape=[jax.ShapeDtypeStruct((T, D), BF16), jax.ShapeDtypeStruct((T, KW), BF16),
                   jax.ShapeDtypeStruct((T, KW), BF16)],
        compiler_params=_params(("parallel",)),
    )(qkv, cos, sin, gq, gk)


def _attn_prep_bwd(qkv, dq, dkd, dvd, cos, sin, gq, gk, D, HKV):
    T, QW = qkv.shape
    tr = 256
    nq, nk = D // LANES, HKV // 2
    KW = HKV * LANES

    def body(x_ref, dq_ref, dk_ref, dv_ref, cos_ref, sin_ref, gq_ref, gk_ref, o_ref, dgq_ref, dgk_ref):
        @pl.when(pl.program_id(0) == 0)
        def _():
            dgq_ref[...] = jnp.zeros_like(dgq_ref)
            dgk_ref[...] = jnp.zeros_like(dgk_ref)

        bd = _head_blockdiag()
        cs, sn = cos_ref[...], sin_ref[...]
        isa = _is_a((tr, LANES))

        def back(xv, dy, g):
            rstd = lax.rsqrt(_dot_split(xv * xv, bd, 2) * (1.0 / HEAD) + EPS)
            xhat = xv * rstd
            dxn = dy * cs + _swap_half(dy * sn)
            dxh = dxn * g
            mean = _dot_split(dxh * xhat, bd, 2) * (1.0 / HEAD)
            return rstd * (dxh - xhat * mean), jnp.sum(dxn * xhat, axis=0, keepdims=True)

        def fold(s):
            a = dk_ref[:, 2 * s * LANES:(2 * s + 1) * LANES]
            b = dk_ref[:, (2 * s + 1) * LANES:(2 * s + 2) * LANES]
            return jnp.where(isa, a + pltpu.roll(a, HEAD, axis=1), b + pltpu.roll(b, HEAD, axis=1))

        def foldv(s):
            a = dv_ref[:, 2 * s * LANES:(2 * s + 1) * LANES]
            b = dv_ref[:, (2 * s + 1) * LANES:(2 * s + 2) * LANES]
            return jnp.where(isa, a + pltpu.roll(a, HEAD, axis=1), b + pltpu.roll(b, HEAD, axis=1))

        dgq = jnp.zeros((1, LANES), F32)
        for s in range(nq):
            sl = slice(s * LANES, (s + 1) * LANES)
            dx, dg = back(x_ref[:, sl], dq_ref[:, sl], gq_ref[...])
            o_ref[:, sl] = dx.astype(BF16)
            dgq = dgq + dg
        dgq_ref[...] += dgq
        dgk = jnp.zeros((1, LANES), F32)
        for s in range(nk):
            sl = slice(D + s * LANES, D + (s + 1) * LANES)
            dx, dg = back(x_ref[:, sl], fold(s), gk_ref[...])
            o_ref[:, sl] = dx.astype(BF16)
            dgk = dgk + dg
            o_ref[:, D + (nk + s) * LANES:D + (nk + s + 1) * LANES] = foldv(s).astype(BF16)
        dgk_ref[...] += dgk

    tab = pl.BlockSpec((tr, LANES), lambda i: (i, 0))
    vec = pl.BlockSpec((1, LANES), lambda i: (0, 0))
    kv = pl.BlockSpec((tr, KW), lambda i: (i, 0))
    return pl.pallas_call(
        body, name="attn_prep_bwd", grid=(T // tr,),
        in_specs=[pl.BlockSpec((tr, QW), lambda i: (i, 0)), pl.BlockSpec((tr, D), lambda i: (i, 0)), kv, kv,
                  tab, tab, vec, vec],
        out_specs=[pl.BlockSpec((tr, QW), lambda i: (i, 0)), vec, vec],
        out_shape=[jax.ShapeDtypeStruct((T, QW), BF16), jax.ShapeDtypeStruct((1, LANES), F32),
                   jax.ShapeDtypeStruct((1, LANES), F32)],
        compiler_params=_params(("arbitrary",)),
    )(qkv, dq, dkd, dvd, cos, sin, gq, gk)


def _attn_probs(qs, kw, sink_ref, n, scale):
    rows = qs.shape[0]
    qi = lax.broadcasted_iota(jnp.int32, (rows, 2 * WINDOW), 0) % WINDOW
    kj = lax.broadcasted_iota(jnp.int32, (rows, 2 * WINDOW), 1)
    valid = (kj > qi) & (kj <= qi + WINDOW) & jnp.logical_or(n > 0, kj >= WINDOW)
    isa = _is_a(kw.shape)
    out = []
    for pos in (0, 1):
        kp = jnp.where(isa if pos == 0 else ~isa, kw, jnp.zeros_like(kw))
        s = jnp.where(valid, _dot(qs, kp, _NT) * scale, -jnp.inf)
        sink = sink_ref[0, pos]
        m = jnp.maximum(jnp.max(s, axis=1, keepdims=True), sink)
        p = jnp.exp(s - m)
        ps = jnp.exp(sink - m)
        inv = 1.0 / (jnp.sum(p, axis=1, keepdims=True) + ps)
        out.append((p * inv, ps * inv, kp))
    return out


def _attn_specs(HKV):
    q = pl.BlockSpec((WINDOW, ATT_GW), lambda g, n: (n, g))
    cur = pl.BlockSpec((WINDOW, LANES), lambda g, n: (n, g))
    prev = pl.BlockSpec((WINDOW, LANES), lambda g, n: (jnp.maximum(n - 1, 0), g))
    sink = pl.BlockSpec((1, 2, ATT_GW, 1), lambda g, n: (g, 0, 0, 0))
    return q, cur, prev, sink


def _stack(ref):
    return jnp.concatenate([ref[:, i * LANES:(i + 1) * LANES] for i in range(ATT_GW // LANES)], axis=0)


def _attn_fwd(q, kd, vd, sinkcol, HKV, carry=None):
    T, D = q.shape
    nb = T // WINDOW
    scale = HEAD ** -0.5

    def body(q_ref, kp_ref, kc_ref, vp_ref, vc_ref, sink_ref, o_ref):
        n = pl.program_id(1)
        qs = _stack(q_ref)
        kw = jnp.concatenate([kp_ref[...], kc_ref[...]], axis=0)
        vw = jnp.concatenate([vp_ref[...], vc_ref[...]], axis=0)
        isa = _is_a(vw.shape)
        o = jnp.zeros((ATT_GW, LANES), F32)
        for pos, (probs, _, _) in enumerate(_attn_probs(qs, kw, sink_ref, n, scale)):
            vp = jnp.where(isa if pos == 0 else ~isa, vw, jnp.zeros_like(vw))
            o = o + _dot(probs.astype(BF16), vp)
        for i in range(ATT_GW // LANES):
            o_ref[:, i * LANES:(i + 1) * LANES] = o[i * WINDOW:(i + 1) * WINDOW].astype(BF16)

    qs_, cur, prev, sink = _attn_specs(HKV)
    res = _call(
        body, name="attn_fwd", grid=(HKV, nb), in_specs=[qs_, prev, cur, prev, cur, sink], out_specs=[qs_],
        out_shape=[jax.ShapeDtypeStruct((T, D), BF16)], sem=("parallel", "parallel"),
        args=(q, kd, kd, vd, vd, sinkcol), carry=carry)
    return res[0] if carry is None else (res[0], res[1:])


def _attn_bwd(q, kd, vd, o, do, sinkcol, HKV, carry=None):
    T, D = q.shape
    nb = T // WINDOW
    scale = HEAD ** -0.5
    KW = HKV * LANES

    def body(q_ref, kp_ref, kc_ref, vp_ref, vc_ref, o_ref, do_ref, sink_ref, dq_ref, dk_ref, dv_ref, ds_ref):
        n = pl.program_id(1)

        @pl.when(n == 0)
        def _():
            dk_ref[...] = jnp.zeros_like(dk_ref)
            dv_ref[...] = jnp.zeros_like(dv_ref)
            ds_ref[...] = jnp.zeros_like(ds_ref)

        qs = _stack(q_ref)
        dos = _stack(do_ref)
        os_ = _stack(o_ref).astype(F32)
        kw = jnp.concatenate([kp_ref[...], kc_ref[...]], axis=0)
        vw = jnp.concatenate([vp_ref[...], vc_ref[...]], axis=0)
        isa_w = _is_a(vw.shape)
        isa_q = _is_a(dos.shape)
        dd = dos * os_
        dob = dos.astype(BF16)
        dqs = jnp.zeros((ATT_GW, LANES), F32)
        dkw, dvw, dsk = [], [], []
        for pos, (probs, psink, kp) in enumerate(_attn_probs(qs, kw, sink_ref, n, scale)):
            sel_w = isa_w if pos == 0 else ~isa_w
            sel_q = isa_q if pos == 0 else ~isa_q
            delta = jnp.sum(jnp.where(sel_q, dd, 0.0), axis=1, keepdims=True)
            vp = jnp.where(sel_w, vw, jnp.zeros_like(vw))
            dp = _dot(dob, vp, _NT)
            dsb = (probs * (dp - delta) * scale).astype(BF16)
            dqs = dqs + _dot(dsb, kp)
            dkw.append(_dot(dsb, qs, _TN))
            dvw.append(_dot(probs.astype(BF16), dob, _TN))
            dsk.append(-psink * delta)
        dkw = jnp.where(isa_w, dkw[0], dkw[1])
        dvw = jnp.where(isa_w, dvw[0], dvw[1])

        @pl.when(n == 0)
        def _():
            dk_ref[0:WINDOW, :] += dkw[WINDOW:]
            dv_ref[0:WINDOW, :] += dvw[WINDOW:]

        @pl.when(n > 0)
        def _():
            start = pl.multiple_of((n - 1) * WINDOW, WINDOW)
            dk_ref[pl.ds(start, 2 * WINDOW), :] += dkw
            dv_ref[pl.ds(start, 2 * WINDOW), :] += dvw

        rows = []
        for i in range(ATT_GW // LANES):
            dq_ref[:, i * LANES:(i + 1) * LANES] = dqs[i * WINDOW:(i + 1) * WINDOW]
            for pos in (0, 1):
                s = jnp.sum(dsk[pos][i * WINDOW:(i + 1) * WINDOW], axis=0, keepdims=True)
                rows.append(jnp.broadcast_to(s, (1, LANES)))
        ds_ref[0] += jnp.concatenate(rows, axis=0)

    qs_, cur, prev, sink = _attn_specs(HKV)
    dqo = pl.BlockSpec((WINDOW, ATT_GW), lambda g, n: (n, g))
    dkv = pl.BlockSpec((T, LANES), lambda g, n: (0, g))
    res = _call(
        body, name="attn_bwd", grid=(HKV, nb),
        in_specs=[qs_, prev, cur, prev, cur, qs_, dqo, sink],
        out_specs=[dqo, dkv, dkv, pl.BlockSpec((1, ATT_GROUP, LANES), lambda g, n: (g, 0, 0))],
        out_shape=[jax.ShapeDtypeStruct((T, D), F32), jax.ShapeDtypeStruct((T, KW), F32),
                   jax.ShapeDtypeStruct((T, KW), F32), jax.ShapeDtypeStruct((HKV, ATT_GROUP, LANES), F32)],
        sem=("parallel", "arbitrary"), args=(q, kd, kd, vd, vd, o, do, sinkcol), carry=carry)
    return res if carry is None else (*res[:4], res[4:])


def _conv_fwd(zx, w, b, DI, CD):
    T = zx.shape[0]
    cw, tc = _tile(math.gcd(DI, CD), 512), 512
    off = DI // cw

    def body(cur_ref, halo_ref, w_ref, b_ref, o_ref):
        i = pl.program_id(1)
        halo = jnp.where(i > 0, halo_ref[...], 0.0)
        ext = jnp.concatenate([halo, cur_ref[...]], axis=0)
        acc = b_ref[...] + w_ref[SSM_CONV - 1:SSM_CONV, :] * ext[HALO:]
        for k in range(SSM_CONV - 1):
            acc = acc + w_ref[k:k + 1, :] * pltpu.roll(ext, SSM_CONV - 1 - k, axis=0)[HALO:]
        o_ref[...] = acc * jax.nn.sigmoid(acc)

    return pl.pallas_call(
        body, name="ssm_conv_fwd", grid=(CD // cw, T // tc),
        in_specs=[pl.BlockSpec((tc, cw), lambda j, i: (i, off + j)),
                  pl.BlockSpec((HALO, cw), lambda j, i: (jnp.maximum(i * (tc // HALO) - 1, 0), off + j)),
                  pl.BlockSpec((SSM_CONV, cw), lambda j, i: (0, j)), pl.BlockSpec((1, cw), lambda j, i: (0, j))],
        out_specs=pl.BlockSpec((tc, cw), lambda j, i: (i, j)),
        out_shape=jax.ShapeDtypeStruct((T, CD), F32),
        compiler_params=_params(("parallel", "parallel")),
    )(zx, zx, w, b)


def _conv_bwd(zx, dxc, w, b, DI, CD):
    T = zx.shape[0]
    cw, tc = _tile(math.gcd(DI, CD), 512), 512
    off = DI // cw
    nt = T // tc
    hb = tc // HALO

    def body(cur_ref, prev_ref, next_ref, d_ref, dnext_ref, w_ref, b_ref, o_ref, dw_ref, db_ref):
        i = pl.program_id(1)

        @pl.when(i == 0)
        def _():
            dw_ref[...] = jnp.zeros_like(dw_ref)
            db_ref[...] = jnp.zeros_like(db_ref)

        prev = jnp.where(i > 0, prev_ref[...], 0.0)
        ext = jnp.concatenate([prev, cur_ref[...], next_ref[...]], axis=0)
        u = b_ref[...] + w_ref[SSM_CONV - 1:SSM_CONV, :] * ext
        for k in range(SSM_CONV - 1):
            u = u + w_ref[k:k + 1, :] * pltpu.roll(ext, SSM_CONV - 1 - k, axis=0)
        u = u[HALO:]
        dnext = jnp.where(i < nt - 1, dnext_ref[...], 0.0)
        dxe = jnp.concatenate([d_ref[...], dnext], axis=0)
        sg = jax.nn.sigmoid(u)
        du = dxe * sg * (1.0 + u * (1.0 - sg))
        n_e = tc + HALO
        dx = w_ref[SSM_CONV - 1:SSM_CONV, :] * du
        for k in range(SSM_CONV - 1):
            dx = dx + w_ref[k:k + 1, :] * pltpu.roll(du, n_e - (SSM_CONV - 1 - k), axis=0)
        o_ref[...] = dx[:tc].astype(BF16)
        duc = du[:tc]
        db_ref[...] += jnp.sum(duc, axis=0, keepdims=True)
        xs = ext[:n_e]
        dws = []
        for k in range(SSM_CONV):
            sh = xs if k == SSM_CONV - 1 else pltpu.roll(xs, SSM_CONV - 1 - k, axis=0)
            dws.append(jnp.sum(duc * sh[HALO:], axis=0, keepdims=True))
        dw_ref[...] += jnp.concatenate(dws, axis=0)

    return pl.pallas_call(
        body, name="ssm_conv_bwd", grid=(CD // cw, nt),
        in_specs=[pl.BlockSpec((tc, cw), lambda j, i: (i, off + j)),
                  pl.BlockSpec((HALO, cw), lambda j, i: (jnp.maximum(i * hb - 1, 0), off + j)),
                  pl.BlockSpec((HALO, cw), lambda j, i: (jnp.minimum((i + 1) * hb, nt * hb - 1), off + j)),
                  pl.BlockSpec((tc, cw), lambda j, i: (i, j)),
                  pl.BlockSpec((HALO, cw), lambda j, i: (jnp.minimum((i + 1) * hb, nt * hb - 1), j)),
                  pl.BlockSpec((SSM_CONV, cw), lambda j, i: (0, j)), pl.BlockSpec((1, cw), lambda j, i: (0, j))],
        out_specs=[pl.BlockSpec((tc, cw), lambda j, i: (i, j)), pl.BlockSpec((SSM_CONV, cw), lambda j, i: (0, j)),
                   pl.BlockSpec((1, cw), lambda j, i: (0, j))],
        out_shape=[jax.ShapeDtypeStruct((T, CD), BF16), jax.ShapeDtypeStruct((SSM_CONV, CD), F32),
                   jax.ShapeDtypeStruct((1, CD), F32)],
        compiler_params=_params(("parallel", "arbitrary")),
    )(zx, zx, zx, dxc, dxc, w, b)


def _tri_dot(v, upper):
    L = v.shape[0]
    r = lax.broadcasted_iota(jnp.int32, (L, L), 0)
    c = lax.broadcasted_iota(jnp.int32, (L, L), 1)
    tri = ((r <= c) if upper else (r >= c)).astype(BF16)
    p = _split3(v)
    return _dot(tri, p[0]) + _dot(tri, p[1]) + _dot(tri, p[2])


def _ssd_time2(dtraw_ref, bias_ref, alog_ref, sel):
    dt = jax.nn.softplus(dtraw_ref[...] + bias_ref[...])
    acum = _tri_dot(dt * (-jnp.exp(alog_ref[...])), False)
    return dt, _dot_split(dt, sel, 3), _dot_split(acum, sel, 3)


def _decay(acs, acs_t, pos, transposed):
    L = acs.shape[0]
    r = lax.broadcasted_iota(jnp.int32, (L, L), 0)
    c = lax.broadcasted_iota(jnp.int32, (L, L), 1)
    col = acs[:, HEAD * pos:HEAD * pos + 1]
    row = acs_t[HEAD * pos:HEAD * pos + 1, :]
    if transposed:
        return jnp.exp(jnp.where(r <= c, row - col, -jnp.inf))
    return jnp.exp(jnp.where(r >= c, col - row, -jnp.inf))


def _ssd_specs(G, GW, DI, ZW):
    L = SSM_CHUNK
    grp = lambda f: pl.BlockSpec((L, GW), lambda g, c: (f(c), g))
    return dict(
        grp=grp,
        bmat=lambda f: pl.BlockSpec((L, SSM_STATE), lambda g, c: (f(c), DI // SSM_STATE + g)),
        cmat=lambda f: pl.BlockSpec((L, SSM_STATE), lambda g, c: (f(c), DI // SSM_STATE + G + g)),
        dtraw=lambda f: pl.BlockSpec((L, LANES), lambda g, c: (f(c), ZW // LANES - 1)),
        vec=pl.BlockSpec((1, LANES), lambda g, c: (0, 0)),
        gvec=pl.BlockSpec((1, GW), lambda g, c: (0, g)),
        sel=pl.BlockSpec((1, LANES, GW), lambda g, c: (g, 0, 0)),
        selt=pl.BlockSpec((1, GW, LANES), lambda g, c: (g, 0, 0)),
    )


def _ssd_fwd(zx, xc, bias, alog, sel, dskip, ng, DI, carry=None):
    T, ZW = zx.shape
    G, L = SSM_GROUPS, SSM_CHUNK
    GW = DI // G
    NS = GW // LANES
    nc = T // L
    sp = _ssd_specs(G, GW, DI, ZW)
    ident = lambda c: c

    def body(x_ref, b_ref, c_ref, z_ref, dtraw_ref, bias_ref, alog_ref, sel_ref, d_ref, ng_ref,
             y_ref, yo_ref, st_ref, state):
        c = pl.program_id(1)

        @pl.when(c == 0)
        def _():
            state[...] = jnp.zeros_like(state)

        x = x_ref[...]
        bb, cb_ = b_ref[...].astype(BF16), c_ref[...].astype(BF16)
        cbm = _dot(cb_, bb, _NT)
        _, dtx, acx = _ssd_time2(dtraw_ref, bias_ref, alog_ref, sel_ref[0])
        xdt = x * dtx
        ex = jnp.exp(acx)
        last = acx[L - 1:L, :]
        te = jnp.exp(last - acx)
        dlast = jnp.exp(last)
        isa = _is_a((L, LANES))
        for i in range(NS):
            sl = slice(i * LANES, (i + 1) * LANES)
            acs = acx[:, sl]
            acs_t = acs.T
            xs = xdt[:, sl]
            y = jnp.zeros((L, LANES), F32)
            for pos in (0, 1):
                m = (cbm * _decay(acs, acs_t, pos, False)).astype(BF16)
                y = y + _dot(m, jnp.where(isa if pos == 0 else ~isa, xs, 0.0).astype(BF16))
            st = state[i]
            st_ref[0, i] = st
            y = y + _dot(cb_, st.astype(BF16)) * ex[:, sl]
            state[i] = st * dlast[:, sl] + _dot(bb, (xs * te[:, sl]).astype(BF16), _TN)
            y_ref[:, sl] = y + d_ref[:, sl] * x[:, sl]
        z = z_ref[...]
        gated = y_ref[...] * (z * jax.nn.sigmoid(z))
        rstd = lax.rsqrt(jnp.mean(gated * gated, axis=1, keepdims=True) + EPS)
        yo_ref[...] = (gated * rstd * ng_ref[...]).astype(BF16)

    res = _call(
        body, name="ssd_fwd", grid=(G, nc),
        in_specs=[sp["grp"](ident), sp["bmat"](ident), sp["cmat"](ident), sp["grp"](ident), sp["dtraw"](ident),
                  sp["vec"], sp["vec"], sp["sel"], sp["gvec"], sp["gvec"]],
        out_specs=[sp["grp"](ident), sp["grp"](ident),
                   pl.BlockSpec((1, NS, SSM_STATE, LANES), lambda g, c: (c, g, 0, 0))],
        out_shape=[jax.ShapeDtypeStruct((T, DI), F32), jax.ShapeDtypeStruct((T, DI), BF16),
                   jax.ShapeDtypeStruct((nc, G * NS, SSM_STATE, LANES), F32)],
        scratch_shapes=[pltpu.VMEM((NS, SSM_STATE, LANES), F32)],
        sem=("parallel", "arbitrary"), args=(xc, xc, xc, zx, zx, bias, alog, sel, dskip, ng), carry=carry)
    return res if carry is None else (*res[:3], res[3:])


def _ssd_bwd(zx, xc, yssd, dyo, states, bias, alog, sel, selt, hsel, dskip, ng, DI, carry=None):
    T, ZW = zx.shape
    G, L = SSM_GROUPS, SSM_CHUNK
    GW = DI // G
    NS = GW // LANES
    nc = T // L
    sp = _ssd_specs(G, GW, DI, ZW)
    rev = lambda c: nc - 1 - c

    def body(x_ref, b_ref, c_ref, z_ref, dtraw_ref, y_ref, dyo_ref, st_ref, bias_ref, alog_ref, sel_ref, selt_ref,
             hsel_ref, d_ref, ng_ref, dz_ref, dx_ref, db_ref, dc_ref, ddt_ref, dac_ref, dd_ref, dng_ref, dstate):
        c = pl.program_id(1)

        @pl.when(c == 0)
        def _():
            dstate[...] = jnp.zeros_like(dstate)
            dd_ref[...] = jnp.zeros_like(dd_ref)
            dng_ref[...] = jnp.zeros_like(dng_ref)

        z, ys, dyo = z_ref[...], y_ref[...], dyo_ref[...]
        sg = jax.nn.sigmoid(z)
        sz = z * sg
        gated = ys * sz
        rstd = lax.rsqrt(jnp.mean(gated * gated, axis=1, keepdims=True) + EPS)
        yn = gated * rstd
        dng_ref[0] += jnp.sum(dyo * yn, axis=0, keepdims=True)
        dyn = dyo * ng_ref[...]
        dgated = rstd * (dyn - yn * jnp.mean(dyn * yn, axis=1, keepdims=True))
        g = dgated * sz
        dz_ref[...] = (dgated * ys * sg * (1.0 + z * (1.0 - sg))).astype(BF16)

        x = x_ref[...]
        dsk = d_ref[...]
        dd_ref[0] += jnp.sum(g * x, axis=0, keepdims=True)
        bb, cb_ = b_ref[...].astype(BF16), c_ref[...].astype(BF16)
        cbm = _dot(cb_, bb, _NT)
        cbt = _dot(bb, cb_, _NT)
        _, dtx, acx = _ssd_time2(dtraw_ref, bias_ref, alog_ref, sel_ref[0])
        xdt = x * dtx
        ex = jnp.exp(acx)
        last = acx[L - 1:L, :]
        te = jnp.exp(last - acx)
        dlast = jnp.exp(last)
        isa = _is_a((L, LANES))
        is_last = lax.broadcasted_iota(jnp.int32, (L, LANES), 0) == L - 1
        strict = lax.broadcasted_iota(jnp.int32, (L, L), 0) > lax.broadcasted_iota(jnp.int32, (L, L), 1)
        dcb = jnp.zeros((L, L), F32)
        dcm = jnp.zeros((L, SSM_STATE), F32)
        dbm = jnp.zeros((L, SSM_STATE), F32)
        for i in range(NS):
            sl = slice(i * LANES, (i + 1) * LANES)
            acs = acx[:, sl]
            acs_t = acs.T
            xs, gs = xdt[:, sl], g[:, sl]
            xsb = xs.astype(BF16)
            dxd = jnp.zeros((L, LANES), F32)
            dac_c = jnp.zeros((L, LANES), F32)
            for pos in (0, 1):
                gp = jnp.where(isa if pos == 0 else ~isa, gs, 0.0).astype(BF16)
                mt = (cbt * _decay(acs, acs_t, pos, True)).astype(BF16)
                dxd = dxd + _dot(mt, gp)
                dmd = _dot(gp, xsb, _NT) * _decay(acs, acs_t, pos, False)
                dcb = dcb + dmd
                q = jnp.where(strict, dmd * cbm, 0.0)
                hot = jnp.broadcast_to(hsel_ref[0, 2 * i + pos:2 * i + pos + 1, :], (L, LANES)).astype(BF16)
                dac_c = dac_c + _dot_split(q, hot, 2) - _dot_split(q, hot, 2, _TN)
            st = st_ref[0, i]
            dst = dstate[i]
            stb, dstb = st.astype(BF16), dst.astype(BF16)
            eg = (ex[:, sl] * gs).astype(BF16)
            dcm = dcm + _dot(eg, stb, _NT)
            yoff = _dot(cb_, stb) * ex[:, sl]
            w = xs * te[:, sl]
            wb = w.astype(BF16)
            dw = _dot(bb, dstb)
            dbm = dbm + _dot(wb, dstb, _NT)
            dxt = dxd + dw * te[:, sl]
            dal = dlast[:, sl] * jnp.sum(dst * st, axis=0, keepdims=True) + jnp.sum(dw * w, axis=0, keepdims=True)
            dac_l = gs * yoff - w * dw + jnp.where(is_last, dal, 0.0)
            ddt_l = dxt * x[:, sl]
            dstate[i] = dst * dlast[:, sl] + _dot(cb_, eg, _TN)
            dx_ref[:, sl] = dxt * dtx[:, sl] + dsk[:, sl] * gs
            part = _dot_split(ddt_l, selt_ref[0, sl, :], 2)
            parta = dac_c + _dot_split(dac_l, selt_ref[0, sl, :], 2)
            if i == 0:
                ddt_ref[0] = part
                dac_ref[0] = parta
            else:
                ddt_ref[0] += part
                dac_ref[0] += parta
        dcbb = dcb.astype(BF16)
        dc_ref[...] = dcm + _dot(dcbb, bb)
        db_ref[...] = dbm + _dot(dcbb, cb_, _TN)

    part_spec = pl.BlockSpec((1, L, LANES), lambda g, c: (g, rev(c), 0))
    lane_spec = pl.BlockSpec((1, 1, GW), lambda g, c: (g, 0, 0))
    bc_out = pl.BlockSpec((L, SSM_STATE), lambda g, c: (rev(c), g))
    res = _call(
        body, name="ssd_bwd", grid=(G, nc),
        in_specs=[sp["grp"](rev), sp["bmat"](rev), sp["cmat"](rev), sp["grp"](rev), sp["dtraw"](rev), sp["grp"](rev),
                  sp["grp"](rev), pl.BlockSpec((1, NS, SSM_STATE, LANES), lambda g, c: (rev(c), g, 0, 0)),
                  sp["vec"], sp["vec"], sp["sel"], sp["selt"], pl.BlockSpec((1, 8, LANES), lambda g, c: (g, 0, 0)),
                  sp["gvec"], sp["gvec"]],
        out_specs=[sp["grp"](rev), sp["grp"](rev), bc_out, bc_out, part_spec, part_spec, lane_spec, lane_spec],
        out_shape=[jax.ShapeDtypeStruct((T, DI), BF16), jax.ShapeDtypeStruct((T, DI), F32),
                   jax.ShapeDtypeStruct((T, G * SSM_STATE), F32), jax.ShapeDtypeStruct((T, G * SSM_STATE), F32),
                   jax.ShapeDtypeStruct((G, T, LANES), F32), jax.ShapeDtypeStruct((G, T, LANES), F32),
                   jax.ShapeDtypeStruct((G, 1, GW), F32), jax.ShapeDtypeStruct((G, 1, GW), F32)],
        scratch_shapes=[pltpu.VMEM((NS, SSM_STATE, LANES), F32)], sem=("parallel", "arbitrary"),
        args=(xc, xc, xc, zx, zx, yssd, dyo, states, bias, alog, sel, selt, hsel, dskip, ng), carry=carry)
    return res if carry is None else (*res[:8], res[8:])


def _ssd_dt_bwd(zx, ddt_part, dac_part, bias, alog):
    T, ZW = zx.shape
    G, L = SSM_GROUPS, SSM_CHUNK
    nc = T // L

    def body(dtraw_ref, ddt_ref, dac_ref, bias_ref, alog_ref, o_ref, dal_ref, dbias_ref):
        @pl.when(pl.program_id(0) == 0)
        def _():
            dal_ref[...] = jnp.zeros_like(dal_ref)
            dbias_ref[...] = jnp.zeros_like(dbias_ref)

        raw = dtraw_ref[...] + bias_ref[...]
        dt = jax.nn.softplus(raw)
        a = -jnp.exp(alog_ref[...])
        dac, ddt = dac_ref[0], ddt_ref[0]
        for gi in range(1, G):
            dac = dac + dac_ref[gi]
            ddt = ddt + ddt_ref[gi]
        dda = _tri_dot(dac, True)
        dal_ref[...] += jnp.sum(dda * dt, axis=0, keepdims=True) * a
        draw = (dda * a + ddt) * jax.nn.sigmoid(raw)
        dbias_ref[...] += jnp.sum(draw, axis=0, keepdims=True)
        o_ref[...] = draw.astype(BF16)

    vec = pl.BlockSpec((1, LANES), lambda c: (0, 0))
    part = pl.BlockSpec((G, L, LANES), lambda c: (0, c, 0))
    return pl.pallas_call(
        body, name="ssd_dt_bwd", grid=(nc,),
        in_specs=[pl.BlockSpec((L, LANES), lambda c: (c, ZW // LANES - 1)), part, part, vec, vec],
        out_specs=[pl.BlockSpec((L, LANES), lambda c: (c, 0)), vec, vec],
        out_shape=[jax.ShapeDtypeStruct((T, LANES), BF16), jax.ShapeDtypeStruct((1, LANES), F32),
                   jax.ShapeDtypeStruct((1, LANES), F32)],
        compiler_params=_params(("arbitrary",)),
    )(zx, ddt_part, dac_part, bias, alog)


def _all_gather(shards, name):
    n = len(shards)

    def body(*refs):
        ins, outs = refs[:n], refs[n:2 * n]
        send_sems, recv_sems, local_sems = refs[2 * n:]
        x, y, c = lax.axis_index("x"), lax.axis_index("y"), lax.axis_index("c")
        me, sibling = (x, y, c), (x, y, 1 - c)
        chips = [(1 - x, y), (x, 1 - y), (1 - x, 1 - y)]

        def copy(w, k, block, to, src=None):
            dst = outs[w].at[_slot(*block)]
            return pltpu.make_async_remote_copy(
                src_ref=dst if src is None else src, dst_ref=dst, send_sem=send_sems.at[w, k],
                recv_sem=recv_sems.at[w, k], device_id=to, device_id_type=MESH)

        mine, first, passed = [], [], []
        for w in range(n):
            cp = pltpu.make_async_copy(ins[w], outs[w].at[_slot(*me)], local_sems.at[w])
            cp.start()
            mine.append(cp)
            fw = [copy(w, 0, me, sibling, src=ins[w])]
            fw += [copy(w, 1 + j, me, (*chip, c), src=ins[w]) for j, chip in enumerate(chips)]
            for cp in fw:
                cp.start()
            first += fw
        for w in range(n):
            for j, chip in enumerate(chips):
                copy(w, 1 + j, (*chip, c), me).wait_recv()
                cp = copy(w, 4 + j, (*chip, c), sibling)
                cp.start()
                passed.append(cp)
        for w in range(n):
            copy(w, 0, sibling, me).wait_recv()
            for j, chip in enumerate(chips):
                copy(w, 4 + j, (*chip, 1 - c), me).wait_recv()
        for cp in first + passed:
            cp.wait_send()
        for cp in mine:
            cp.wait()

    any_spec = pl.BlockSpec(memory_space=pl.ANY)
    return pl.pallas_call(
        body, name=name, in_specs=[any_spec] * n, out_specs=[any_spec] * n,
        out_shape=[jax.ShapeDtypeStruct((N_DEV,) + s.shape, s.dtype) for s in shards],
        scratch_shapes=[pltpu.SemaphoreType.DMA((n, 7)), pltpu.SemaphoreType.DMA((n, 7)),
                        pltpu.SemaphoreType.DMA((n,))],
    )(*shards)


def _exchange(blocks, name):
    n = len(blocks)

    def body(*refs):
        local, sends, arrivals = _direct_copies(refs[:n], refs[n:2 * n], *refs[2 * n:])
        for cp in local + sends:
            cp.start()
        for cp in arrivals:
            cp.wait_recv()
        for cp in sends:
            cp.wait_send()
        for cp in local:
            cp.wait()

    any_spec = pl.BlockSpec(memory_space=pl.ANY)
    return pl.pallas_call(
        body, name=name, in_specs=[any_spec] * n, out_specs=[any_spec] * n,
        out_shape=[jax.ShapeDtypeStruct(b.shape, b.dtype) for b in blocks],
        scratch_shapes=[pltpu.SemaphoreType.DMA((n, 7)), pltpu.SemaphoreType.DMA((n, 7)),
                        pltpu.SemaphoreType.DMA((n,))],
    )(*blocks)


def _adamw(parts, w, m, v, name):
    R, C = w.shape
    per_row = C * (N_DEV * parts.dtype.itemsize + 7 * 4) * 2
    tr = R
    if R % 8 == 0:
        tr = 8
        for t in (16, 32, 64, 128, 256, 512):
            if R % t == 0 and t * per_row <= 24 * 1024 * 1024:
                tr = t
    c1 = 1.0 - ADAM_B1 ** ADAM_STEP
    c2 = 1.0 - ADAM_B2 ** ADAM_STEP

    def body(p_ref, w_ref, m_ref, v_ref, g_ref, d_ref, nm_ref, nv_ref):
        g = p_ref[0].astype(F32)
        for k in range(1, N_DEV):
            g = g + p_ref[k].astype(F32)
        nm = ADAM_B1 * m_ref[...] + (1.0 - ADAM_B1) * g
        nv = ADAM_B2 * v_ref[...] + (1.0 - ADAM_B2) * (g * g)
        g_ref[...] = g
        nm_ref[...] = nm
        nv_ref[...] = nv
        d_ref[...] = -ADAM_LR * ((nm / c1) / (jnp.sqrt(nv / c2) + ADAM_EPS) + ADAM_WD * w_ref[...])

    blk = pl.BlockSpec((tr, C), lambda i: (i, 0))
    out = jax.ShapeDtypeStruct((R, C), F32)
    return pl.pallas_call(
        body, name=name, grid=(R // tr,),
        in_specs=[pl.BlockSpec((N_DEV, tr, C), lambda i: (0, i, 0)), blk, blk, blk],
        out_specs=[blk, blk, blk, blk], out_shape=[out, out, out, out],
        compiler_params=_params(("parallel",)),
    )(parts, w, m, v)


def _pad_cols(a, n):
    return jnp.pad(a, ((0, 0), (0, n - a.shape[1])))


def kernel(x, positions, mixer_norm, ffn_norm, attn_w_qkv, attn_q_norm, attn_k_norm, attn_sinks, attn_w_o, ssm_w_in, ssm_conv_w, ssm_conv_b, ssm_dt_bias, ssm_a_log, ssm_d, ssm_norm, ssm_w_out, ffn_w_gate, ffn_w_up, ffn_w_down, loss_target, m_mixer_norm, m_ffn_norm, m_attn_w_qkv, m_attn_q_norm, m_attn_k_norm, m_attn_sinks, m_attn_w_o, m_ssm_w_in, m_ssm_conv_w, m_ssm_conv_b, m_ssm_dt_bias, m_ssm_a_log, m_ssm_d, m_ssm_norm, m_ssm_w_out, m_ffn_w_gate, m_ffn_w_up, m_ffn_w_down, v_mixer_norm, v_ffn_norm, v_attn_w_qkv, v_attn_q_norm, v_attn_k_norm, v_attn_sinks, v_attn_w_o, v_ssm_w_in, v_ssm_conv_w, v_ssm_conv_b, v_ssm_dt_bias, v_ssm_a_log, v_ssm_d, v_ssm_norm, v_ssm_w_out, v_ffn_w_gate, v_ffn_w_up, v_ffn_w_down):
    T, D = x.shape[1], x.shape[2]
    HQ = D // HEAD
    HKV = HQ // ATT_GROUP
    QW = (HQ + 2 * HKV) * HEAD
    DI = 2 * D
    H = DI // HEAD
    G = SSM_GROUPS
    GW = DI // G
    CD = DI + 2 * G * SSM_STATE
    ZW = DI + CD + LANES
    IW = DI + CD + H
    assert T % 512 == 0 and D % 256 == 0 and HKV % 2 == 0 and GW % LANES == 0 and H <= LANES

    weights = dict(mixer_norm=mixer_norm, ffn_norm=ffn_norm, attn_w_qkv=attn_w_qkv, attn_q_norm=attn_q_norm,
                   attn_k_norm=attn_k_norm, attn_sinks=attn_sinks, attn_w_o=attn_w_o, ssm_w_in=ssm_w_in,
                   ssm_conv_w=ssm_conv_w, ssm_conv_b=ssm_conv_b, ssm_dt_bias=ssm_dt_bias, ssm_a_log=ssm_a_log,
                   ssm_d=ssm_d, ssm_norm=ssm_norm, ssm_w_out=ssm_w_out, ffn_w_gate=ffn_w_gate, ffn_w_up=ffn_w_up,
                   ffn_w_down=ffn_w_down)
    mom_m = dict(mixer_norm=m_mixer_norm, ffn_norm=m_ffn_norm, attn_w_qkv=m_attn_w_qkv, attn_q_norm=m_attn_q_norm,
                 attn_k_norm=m_attn_k_norm, attn_sinks=m_attn_sinks, attn_w_o=m_attn_w_o, ssm_w_in=m_ssm_w_in,
                 ssm_conv_w=m_ssm_conv_w, ssm_conv_b=m_ssm_conv_b, ssm_dt_bias=m_ssm_dt_bias, ssm_a_log=m_ssm_a_log,
                 ssm_d=m_ssm_d, ssm_norm=m_ssm_norm, ssm_w_out=m_ssm_w_out, ffn_w_gate=m_ffn_w_gate,
                 ffn_w_up=m_ffn_w_up, ffn_w_down=m_ffn_w_down)
    mom_v = dict(mixer_norm=v_mixer_norm, ffn_norm=v_ffn_norm, attn_w_qkv=v_attn_w_qkv, attn_q_norm=v_attn_q_norm,
                 attn_k_norm=v_attn_k_norm, attn_sinks=v_attn_sinks, attn_w_o=v_attn_w_o, ssm_w_in=v_ssm_w_in,
                 ssm_conv_w=v_ssm_conv_w, ssm_conv_b=v_ssm_conv_b, ssm_dt_bias=v_ssm_dt_bias, ssm_a_log=v_ssm_a_log,
                 ssm_d=v_ssm_d, ssm_norm=v_ssm_norm, ssm_w_out=v_ssm_w_out, ffn_w_gate=v_ffn_w_gate,
                 ffn_w_up=v_ffn_w_up, ffn_w_down=v_ffn_w_down)
    big = ["attn_w_qkv", "attn_w_o", "ssm_w_in", "ssm_w_out", "ffn_w_gate", "ffn_w_up", "ffn_w_down"]

    def flat2(a):
        return a.reshape(-1, a.shape[-1])

    def shard(n, layer=0):
        return weights[n][layer].astype(BF16)

    def from_cols(g):
        return g.transpose(1, 0, 2).reshape(g.shape[1], N_DEV * g.shape[2])

    def from_rows(g):
        return g.reshape(N_DEV * g.shape[1], g.shape[2])

    xs = x[0]
    tgt = loss_target[0]
    inv_freq = ROPE_THETA ** (-jnp.arange(0, HEAD, 2, dtype=F32) / HEAD)
    ang = positions[0].astype(F32)[:, None] * inv_freq
    cos = jnp.tile(jnp.cos(ang), (1, 4))
    sin = jnp.tile(jnp.concatenate([-jnp.sin(ang), jnp.sin(ang)], axis=1), (1, 2))
    gq = jnp.tile(attn_q_norm, (1, 2))
    gk = jnp.tile(attn_k_norm, (1, 2))
    sinkcol = jnp.repeat(attn_sinks.reshape(HKV, ATT_GROUP // 2, 2).transpose(0, 2, 1), WINDOW, axis=2)[..., None]
    bias_p = _pad_cols(ssm_dt_bias, LANES)
    alog_p = _pad_cols(ssm_a_log, LANES)
    dskip = jnp.repeat(ssm_d, HEAD, axis=1)
    lane_head = jnp.arange(DI) // HEAD
    sel = (jnp.arange(LANES)[None, :, None] == lane_head.reshape(G, 1, GW)).astype(BF16)
    selt = sel.transpose(0, 2, 1)
    hsel = (jnp.arange(LANES)[None, None, :] == (jnp.arange(G)[:, None, None] * (H // G) + jnp.arange(8)[None, :, None])
            ).astype(BF16) * (jnp.arange(8)[None, :, None] < H // G)
    vec_w = CD // N_DEV
    small = jnp.concatenate([ssm_conv_w[0], ssm_conv_b, _pad_cols(ssm_norm, vec_w),
                             jnp.zeros((2, vec_w), F32)], axis=0)
    g_qkv, g_o, small_all = _all_gather([shard("attn_w_qkv"), shard("attn_w_o"), small], "gather_first")
    w_qkv, w_o = from_cols(g_qkv), from_rows(g_o)
    conv_w = small_all[:, :SSM_CONV].transpose(1, 0, 2).reshape(SSM_CONV, CD)
    conv_b = small_all[:, SSM_CONV].reshape(1, CD)
    ng = small_all[:, SSM_CONV + 1, :DI // N_DEV].reshape(1, DI)

    def rows_to_blocks(p):
        return p.reshape(N_DEV, p.shape[0] // N_DEV, p.shape[1])

    def cols_to_blocks(p):
        return p.reshape(p.shape[0], N_DEV, p.shape[1] // N_DEV).transpose(1, 0, 2)

    hm0 = _rms_fwd(xs, mixer_norm[0:1], "rms_fwd_m0")
    qkv = _matmul(hm0, w_qkv, mode="nn", out_dtype=F32, name="mm_qkv")
    qr, kd, vd = _attn_prep_fwd(qkv, cos, sin, gq, gk, D, HKV)
    o, got = _attn_fwd(qr, kd, vd, sinkcol, HKV,
                       carry=[shard("ffn_w_gate", 0), shard("ffn_w_up", 0), shard("ffn_w_down", 0)])
    w_gate = [from_cols(got[0]), None]
    w_up = [from_cols(got[1]), None]
    w_down = [from_rows(got[2]), None]
    x1 = _matmul(o, w_o, mode="nn", out_dtype=F32, name="mm_attn_out", add=xs)
    hf0 = _rms_fwd(x1, ffn_norm[0:1], "rms_fwd_f0")
    gate0, up0, act0, got = _ffn_up(hf0, w_gate[0], w_up[0], "ffn_up_0", carry=[shard("ssm_w_in")])
    w_in = _pad_cols(from_cols(got[0]), ZW)
    x2, got = _matmul(act0, w_down[0], mode="nn", out_dtype=F32, name="mm_ffn_down_0", add=x1,
                      carry=[shard("ssm_w_out")])
    w_out = from_rows(got[0])
    hm1 = _rms_fwd(x2, mixer_norm[1:2], "rms_fwd_m1")
    zx, got = _matmul(hm1, w_in, mode="nn", out_dtype=F32, name="mm_ssm_in",
                      carry=[shard("ffn_w_gate", 1), shard("ffn_w_up", 1)])
    w_gate[1], w_up[1] = from_cols(got[0]), from_cols(got[1])
    xc = _conv_fwd(zx, conv_w, conv_b, DI, CD)
    yssd, yout, states, got = _ssd_fwd(zx, xc, bias_p, alog_p, sel, dskip, ng, DI, carry=[shard("ffn_w_down", 1)])
    w_down[1] = from_rows(got[0])
    x3 = _matmul(yout, w_out, mode="nn", out_dtype=F32, name="mm_ssm_out", add=x2)
    hf1 = _rms_fwd(x3, ffn_norm[1:2], "rms_fwd_f1")
    gate1, up1, act1 = _ffn_up(hf1, w_gate[1], w_up[1], "ffn_up_1")
    x4 = _matmul(act1, w_down[1], mode="nn", out_dtype=F32, name="mm_ffn_down_1", add=x3)
    sq, dx4, dx4b = _loss_head(x4, tgt)
    loss = lax.psum(sq[0, 0] * (0.5 / D), ("x", "y", "c"))

    def ffn_bwd(dy, dyb, hf, gate, up, act, layer, xin, gain):
        dg, du = _ffn_dact(dyb, w_down[layer], gate, up, f"ffn_dact_{layer}")
        g_down = _matmul(act, dyb, mode="tn", out_dtype=BF16, name=f"mm_dw_down_{layer}")
        g_gate = _matmul(hf, dg, mode="tn", out_dtype=BF16, name=f"mm_dw_gate_{layer}")
        g_up = _matmul(hf, du, mode="tn", out_dtype=BF16, name=f"mm_dw_up_{layer}")
        dh = _matmul(dg, w_gate[layer], mode="nt", out_dtype=F32, name=f"mm_dh_gate_{layer}")
        dh = _matmul(du, w_up[layer], mode="nt", out_dtype=F32, name=f"mm_dh_up_{layer}", add=dh)
        dx, dxb, dgain = _rms_bwd(xin, gain, dh, dy, f"rms_bwd_f{layer}")
        return dx, dxb, dgain, [cols_to_blocks(g_gate), cols_to_blocks(g_up), rows_to_blocks(g_down)]

    dx3, dx3b, d_fn1, ffn1_blocks = ffn_bwd(dx4, dx4b, hf1, gate1, up1, act1, 1, x3, ffn_norm[1:2])
    dyo = _matmul(dx3b, w_out, mode="nt", out_dtype=F32, name="mm_dyout")
    g_wout = _matmul(yout, dx3b, mode="tn", out_dtype=BF16, name="mm_dw_ssm_out")
    dz, dxx, dbm, dcm, ddt_p, dac_p, dd_l, dng_l, got1 = _ssd_bwd(
        zx, xc, yssd, dyo, states, bias_p, alog_p, sel, selt, hsel, dskip, ng, DI,
        carry=ffn1_blocks + [rows_to_blocks(g_wout)])
    ddt_raw, d_alog, d_bias = _ssd_dt_bwd(zx, ddt_p, dac_p, bias_p, alog_p)
    dxbc, d_convw, d_convb = _conv_bwd(zx, jnp.concatenate([dxx, dbm, dcm], axis=1), conv_w, conv_b, DI, CD)
    dzx = jnp.concatenate([dz, dxbc, ddt_raw], axis=1)
    g_win = _matmul(hm1, dzx, mode="tn", out_dtype=BF16, name="mm_dw_ssm_in")[:, :IW]
    dh, got2 = _matmul(dzx, w_in, mode="nt", out_dtype=F32, name="mm_dh_ssm_in", carry=[cols_to_blocks(g_win)])
    dx2, dx2b, d_mn1 = _rms_bwd(x2, mixer_norm[1:2], dh, dx3, "rms_bwd_m1")
    dx1, dx1b, d_fn0, ffn0_blocks = ffn_bwd(dx2, dx2b, hf0, gate0, up0, act0, 0, x1, ffn_norm[0:1])
    do = _matmul(dx1b, w_o, mode="nt", out_dtype=F32, name="mm_do")
    g_wo = _matmul(o, dx1b, mode="tn", out_dtype=BF16, name="mm_dw_attn_out")
    dq, dkd, dvd, dsink, got3 = _attn_bwd(qr, kd, vd, o, do, sinkcol, HKV, carry=ffn0_blocks + [rows_to_blocks(g_wo)])
    dqkv, dgq_l, dgk_l = _attn_prep_bwd(qkv, dq, dkd, dvd, cos, sin, gq, gk, D, HKV)
    g_wqkv = _matmul(hm0, dqkv, mode="tn", out_dtype=BF16, name="mm_dw_qkv")
    dh = _matmul(dqkv, w_qkv, mode="nt", out_dtype=F32, name="mm_dh_qkv")
    dx0, _, d_mn0 = _rms_bwd(xs, mixer_norm[0:1], dh, dx1, "rms_bwd_m0")

    d_ng = dng_l.reshape(1, DI)
    vec_send = jnp.concatenate([
        d_convw.reshape(SSM_CONV, N_DEV, vec_w).transpose(1, 0, 2), d_convb.reshape(1, N_DEV, vec_w).transpose(1, 0, 2),
        _pad_cols(d_ng.reshape(N_DEV, DI // N_DEV), vec_w)[:, None, :], jnp.zeros((N_DEV, 2, vec_w), F32)], axis=1)
    d_sinks = dsink[:, :, 0].reshape(1, HQ)
    d_gq = dgq_l[:, :HEAD] + dgq_l[:, HEAD:]
    d_gk = dgk_l[:, :HEAD] + dgk_l[:, HEAD:]
    d_dskip = dd_l.reshape(H, HEAD).sum(axis=1).reshape(1, H)
    rep_names = ["mixer_norm", "ffn_norm", "attn_q_norm", "attn_k_norm", "attn_sinks", "ssm_dt_bias", "ssm_a_log",
                 "ssm_d"]
    rep_grads = [jnp.concatenate([d_mn0, d_mn1], axis=0), jnp.concatenate([d_fn0, d_fn1], axis=0), d_gq, d_gk,
                 d_sinks, d_bias[:, :H], d_alog[:, :H], d_dskip]
    rep_sizes = [weights[n].size for n in rep_names]
    rep_len = -(-sum(rep_sizes) // (8 * LANES)) * 8 * LANES

    def pack(arrs):
        flat = jnp.concatenate([a.reshape(-1) for a in arrs])
        return jnp.pad(flat, (0, rep_len - flat.shape[0])).reshape(rep_len // LANES, LANES)

    rep_send = jnp.broadcast_to(pack(rep_grads)[None], (N_DEV, rep_len // LANES, LANES))
    got4 = _exchange([cols_to_blocks(g_wqkv), vec_send, rep_send], "exchange_last")
    parts_of = {
        "attn_w_qkv": got4[0], "attn_w_o": got3[3], "ssm_w_in": got2[0], "ssm_w_out": got1[3],
        "ffn_w_gate": jnp.concatenate([got3[0], got1[0]], axis=1), "ffn_w_up": jnp.concatenate([got3[1], got1[1]], axis=1),
        "ffn_w_down": jnp.concatenate([got3[2], got1[2]], axis=1),
    }

    out = {}
    for n in big:
        res = _adamw(parts_of[n], flat2(weights[n]), flat2(mom_m[n]), flat2(mom_v[n]), f"adamw_{n}")
        out[n] = [r.reshape(weights[n].shape) for r in res]

    def vec_block(d):
        return jnp.concatenate([d["ssm_conv_w"][0], d["ssm_conv_b"], _pad_cols(d["ssm_norm"], vec_w),
                                jnp.zeros((2, vec_w), F32)], axis=0)

    res = _adamw(got4[1], vec_block(weights), vec_block(mom_m), vec_block(mom_v), "adamw_vectors")
    out["ssm_conv_w"] = [r[:SSM_CONV][None] for r in res]
    out["ssm_conv_b"] = [r[SSM_CONV:SSM_CONV + 1] for r in res]
    out["ssm_norm"] = [r[SSM_CONV + 1:SSM_CONV + 2, :DI // N_DEV] for r in res]
    res = _adamw(got4[2], pack([weights[n] for n in rep_names]), pack([mom_m[n] for n in rep_names]),
                 pack([mom_v[n] for n in rep_names]), "adamw_replicated")
    offs = 0
    for n, sz in zip(rep_names, rep_sizes):
        out[n] = [r.reshape(-1)[offs:offs + sz].reshape(weights[n].shape) for r in res]
        offs += sz

    names = list(weights)
    return (loss, dx0[None], *[out[n][0] for n in names], *[out[n][1] for n in names],
            *[out[n][2] for n in names], *[out[n][3] for n in names])
```

```python
import functools
import math

import jax
import jax.numpy as jnp
from jax import lax
from jax.experimental import pallas as pl
from jax.experimental.pallas import tpu as pltpu

F32 = jnp.float32
BF16 = jnp.bfloat16

N_DEV = 8
EPS = 1e-6
LANES = 128
HEAD = 64
ATT_GROUP = 8
ATT_GW = ATT_GROUP * HEAD
WINDOW = 128
ATT_STEP_BLOCKS = 4
ROPE_THETA = 10000.0
SSM_GROUPS = 8
SSM_STATE = 128
SSM_CONV = 4
SSM_CHUNK = 256
HALO = 8
ADAM_LR, ADAM_B1, ADAM_B2, ADAM_EPS, ADAM_WD, ADAM_STEP = 0.001, 0.9, 0.999, 1e-08, 0.01, 10
VMEM_LIMIT = 56 * 1024 * 1024
MATMUL_VMEM = 44 * 1024 * 1024
MESH = pl.DeviceIdType.MESH

_NN = (((1,), (0,)), ((), ()))
_NT = (((1,), (1,)), ((), ()))
_TN = (((0,), (0,)), ((), ()))


def _dot(a, b, dims=_NN):
    return lax.dot_general(a, b, dims, preferred_element_type=F32)


def _tile(n, cap):
    if n % LANES:
        return n
    best = LANES
    for t in range(LANES, min(n, cap) + 1, LANES):
        if n % t == 0:
            best = t
    return best


def _params(sem):
    return pltpu.CompilerParams(dimension_semantics=sem, vmem_limit_bytes=VMEM_LIMIT)


def _slot(px, py, pc):
    return 4 * px + 2 * py + pc


def _direct_copies(srcs, dsts, send_sems, recv_sems, local_sems, with_arrivals=True):
    x, y, c = lax.axis_index("x"), lax.axis_index("y"), lax.axis_index("c")
    me = _slot(x, y, c)
    peers = [(x ^ (m >> 2), y ^ ((m >> 1) & 1), c ^ (m & 1)) for m in range(1, N_DEV)]
    local, sends, arrivals = [], [], []
    for w, (src, dst) in enumerate(zip(srcs, dsts)):
        sliced = src.shape == dst.shape
        local.append(pltpu.make_async_copy(src.at[me] if sliced else src, dst.at[me], local_sems.at[w]))
        for k, peer in enumerate(peers):
            sems = dict(send_sem=send_sems.at[w, k], recv_sem=recv_sems.at[w, k], device_id=peer, device_id_type=MESH)
            sends.append(pltpu.make_async_remote_copy(
                src_ref=src.at[_slot(*peer)] if sliced else src, dst_ref=dst.at[me], **sems))
            if with_arrivals:
                arrivals.append(pltpu.make_async_remote_copy(
                    src_ref=src.at[me] if sliced else src, dst_ref=dst.at[_slot(*peer)], **sems))
    return local, sends, arrivals


def _call(body, *, name, grid, in_specs, out_specs, out_shape, sem, args, scratch_shapes=(), carry=None):
    if carry is None:
        return pl.pallas_call(body, name=name, grid=grid, in_specs=in_specs, out_specs=out_specs, out_shape=out_shape,
                              scratch_shapes=list(scratch_shapes), compiler_params=_params(sem))(*args)
    n_in, n_out, n_sc, n_c = len(in_specs), len(out_specs), len(scratch_shapes), len(carry)
    recv_shape = [jax.ShapeDtypeStruct(a.shape if a.shape[0] == N_DEV and a.ndim == 3 else (N_DEV,) + a.shape, a.dtype)
                  for a in carry]

    def wrapped(*refs):
        ins, c_in = refs[:n_in], refs[n_in:n_in + n_c]
        outs, c_out = refs[n_in + n_c:n_in + n_c + n_out], refs[n_in + n_c + n_out:n_in + 2 * n_c + n_out]
        scr = refs[n_in + 2 * n_c + n_out:n_in + 2 * n_c + n_out + n_sc]
        send_sems, recv_sems, local_sems = refs[-3:]
        first = functools.reduce(jnp.logical_and, [pl.program_id(d) == 0 for d in range(len(grid))])
        last = functools.reduce(jnp.logical_and, [pl.program_id(d) == n - 1 for d, n in enumerate(grid)])

        @pl.when(first)
        def _():
            local, sends, _ = _direct_copies(c_in, c_out, send_sems, recv_sems, local_sems, with_arrivals=False)
            for cp in local + sends:
                cp.start()

        body(*ins, *outs, *scr)

        @pl.when(last)
        def _():
            local, sends, arrivals = _direct_copies(c_in, c_out, send_sems, recv_sems, local_sems)
            for cp in arrivals:
                cp.wait_recv()
            for cp in sends:
                cp.wait_send()
            for cp in local:
                cp.wait()

    any_spec = pl.BlockSpec(memory_space=pl.ANY)
    res = pl.pallas_call(
        wrapped, name=name, grid=grid, in_specs=list(in_specs) + [any_spec] * n_c,
        out_specs=list(out_specs) + [any_spec] * n_c, out_shape=list(out_shape) + recv_shape,
        scratch_shapes=list(scratch_shapes) + [pltpu.SemaphoreType.DMA((n_c, N_DEV - 1)),
                                               pltpu.SemaphoreType.DMA((n_c, N_DEV - 1)), pltpu.SemaphoreType.DMA((n_c,))],
        compiler_params=_params(("arbitrary",) * len(grid)),
    )(*args, *carry)
    return res


def _matmul(a, b, *, mode, out_dtype, name, add=None, carry=None):
    if mode == "nn":
        (M, K), N = a.shape, b.shape[1]
    elif mode == "nt":
        (M, K), N = a.shape, b.shape[0]
    else:
        (K, M), N = a.shape, b.shape[1]
    assert a.dtype == BF16 and b.dtype == BF16
    has_add = add is not None
    tm, tn = (_tile(M, 512), _tile(N, 512)) if mode == "tn" else (_tile(M, 1024), _tile(N, 512))
    fixed = 2 * tm * tn * (jnp.dtype(out_dtype).itemsize + (4 if has_add else 0)) + tm * tn * 4
    per_k = 2 * 2 * (tm + tn) + (2 * tm if mode == "tn" else 0)
    tk = _tile(K, max(LANES, (MATMUL_VMEM - fixed) // per_k))
    nk = K // tk
    dims = _NT if mode == "nt" else _NN
    if mode == "tn":
        a_spec = pl.BlockSpec((tk, tm), lambda i, j, k: (jnp.where(j == 0, k, 0), i))
    else:
        a_spec = pl.BlockSpec((tm, tk), lambda i, j, k: (i, k))
    b_spec = pl.BlockSpec((tn, tk), lambda i, j, k: (j, k)) if mode == "nt" else pl.BlockSpec((tk, tn), lambda i, j, k: (k, j))
    o_spec = pl.BlockSpec((tm, tn), lambda i, j, k: (i, j))

    def body(*refs):
        a_ref, b_ref = refs[:2]
        add_ref = refs[2] if has_add else None
        o_ref = refs[2 + has_add]
        scratch = list(refs[3 + has_add:])
        at = scratch.pop(0) if mode == "tn" else None
        acc = scratch.pop(0) if nk > 1 else None
        j, k = pl.program_id(1), pl.program_id(2)
        if mode == "tn":
            @pl.when(j == 0)
            def _():
                at[k] = a_ref[...].T

            part = _dot(at[k], b_ref[...], dims)
        else:
            part = _dot(a_ref[...], b_ref[...], dims)

        def finish(r):
            if has_add:
                r = r + add_ref[...]
            o_ref[...] = r.astype(out_dtype)

        if nk == 1:
            finish(part)
        else:
            @pl.when(k == 0)
            def _():
                acc[...] = part

            @pl.when(jnp.logical_and(k > 0, k < nk - 1))
            def _():
                acc[...] += part

            @pl.when(k == nk - 1)
            def _():
                finish(acc[...] + part)

    scratch = ([pltpu.VMEM((nk, tm, tk), BF16)] if mode == "tn" else []) + ([pltpu.VMEM((tm, tn), F32)] if nk > 1 else [])
    res = _call(
        body, name=name, grid=(M // tm, N // tn, nk),
        in_specs=[a_spec, b_spec] + ([o_spec] if has_add else []),
        out_specs=[o_spec], out_shape=[jax.ShapeDtypeStruct((M, N), out_dtype)],
        scratch_shapes=scratch, sem=("parallel", "arbitrary", "arbitrary"),
        args=(a, b, add) if has_add else (a, b), carry=carry)
    return res[0] if carry is None else (res[0], res[1:])


def _rms_fwd(x, gain, name):
    T, D = x.shape
    tr = 256

    def body(x_ref, g_ref, h_ref):
        xv = x_ref[...]
        rstd = lax.rsqrt(jnp.mean(xv * xv, axis=1, keepdims=True) + EPS)
        h_ref[...] = (xv * rstd * g_ref[...]).astype(BF16)

    return pl.pallas_call(
        body, name=name, grid=(T // tr,),
        in_specs=[pl.BlockSpec((tr, D), lambda i: (i, 0)), pl.BlockSpec((1, D), lambda i: (0, 0))],
        out_specs=pl.BlockSpec((tr, D), lambda i: (i, 0)),
        out_shape=jax.ShapeDtypeStruct((T, D), BF16),
        compiler_params=_params(("parallel",)),
    )(x, gain)


def _rms_bwd(x, gain, dh, dres, name):
    T, D = x.shape
    tr = 256

    def body(x_ref, g_ref, dh_ref, dr_ref, dx_ref, dxb_ref, dg_ref):
        @pl.when(pl.program_id(0) == 0)
        def _():
            dg_ref[...] = jnp.zeros_like(dg_ref)

        xv = x_ref[...]
        rstd = lax.rsqrt(jnp.mean(xv * xv, axis=1, keepdims=True) + EPS)
        xhat = xv * rstd
        dy = dh_ref[...].astype(F32)
        dg_ref[...] += jnp.sum(dy * xhat, axis=0, keepdims=True)
        dxh = dy * g_ref[...]
        dx = dr_ref[...] + rstd * (dxh - xhat * jnp.mean(dxh * xhat, axis=1, keepdims=True))
        dx_ref[...] = dx
        dxb_ref[...] = dx.astype(BF16)

    row = pl.BlockSpec((tr, D), lambda i: (i, 0))
    vec = pl.BlockSpec((1, D), lambda i: (0, 0))
    return pl.pallas_call(
        body, name=name, grid=(T // tr,), in_specs=[row, vec, row, row], out_specs=[row, row, vec],
        out_shape=[jax.ShapeDtypeStruct((T, D), F32), jax.ShapeDtypeStruct((T, D), BF16),
                   jax.ShapeDtypeStruct((1, D), F32)],
        compiler_params=_params(("arbitrary",)),
    )(x, gain, dh, dres)


def _loss_head(y, target):
    T, D = y.shape
    tr = 256

    def body(y_ref, t_ref, s_ref, d_ref, db_ref):
        @pl.when(pl.program_id(0) == 0)
        def _():
            s_ref[...] = jnp.zeros_like(s_ref)

        e = y_ref[...] - t_ref[...]
        s_ref[...] += jnp.sum(jnp.sum(e * e, axis=1, keepdims=True), axis=0, keepdims=True)
        d = e * (1.0 / D)
        d_ref[...] = d
        db_ref[...] = d.astype(BF16)

    row = pl.BlockSpec((tr, D), lambda i: (i, 0))
    return pl.pallas_call(
        body, name="loss_head", grid=(T // tr,), in_specs=[row, row],
        out_specs=[pl.BlockSpec((1, 1), lambda i: (0, 0)), row, row],
        out_shape=[jax.ShapeDtypeStruct((1, 1), F32), jax.ShapeDtypeStruct((T, D), F32),
                   jax.ShapeDtypeStruct((T, D), BF16)],
        compiler_params=_params(("arbitrary",)),
    )(y, target)


def _ffn_up(h, wg, wu, name, carry=None):
    (T, D), Fd = h.shape, wg.shape[1]
    tm, tn = _tile(T, 1024), _tile(Fd, 512)

    def body(h_ref, wg_ref, wu_ref, g_ref, u_ref, a_ref):
        hv = h_ref[...]
        g = _dot(hv, wg_ref[...])
        g_ref[...] = g
        sg = g * jax.nn.sigmoid(g)
        u = _dot(hv, wu_ref[...])
        u_ref[...] = u
        a_ref[...] = (sg * u).astype(BF16)

    w_spec = pl.BlockSpec((D, tn), lambda i, j: (0, j))
    o_spec = pl.BlockSpec((tm, tn), lambda i, j: (i, j))
    res = _call(
        body, name=name, grid=(T // tm, Fd // tn),
        in_specs=[pl.BlockSpec((tm, D), lambda i, j: (i, 0)), w_spec, w_spec],
        out_specs=[o_spec, o_spec, o_spec],
        out_shape=[jax.ShapeDtypeStruct((T, Fd), F32), jax.ShapeDtypeStruct((T, Fd), F32),
                   jax.ShapeDtypeStruct((T, Fd), BF16)],
        sem=("parallel", "arbitrary"), args=(h, wg, wu), carry=carry)
    return res if carry is None else (*res[:3], res[3:])


def _ffn_dact(dy, wd, gate, up, name):
    (T, D), Fd = dy.shape, wd.shape[0]
    tm, tn = _tile(T, 1024), _tile(Fd, 512)

    def body(dy_ref, wd_ref, g_ref, u_ref, dg_ref, du_ref):
        da = _dot(dy_ref[...], wd_ref[...], _NT)
        g = g_ref[...]
        sg = jax.nn.sigmoid(g)
        du_ref[...] = (da * g * sg).astype(BF16)
        dg_ref[...] = (da * u_ref[...] * sg * (1.0 + g * (1.0 - sg))).astype(BF16)

    o_spec = pl.BlockSpec((tm, tn), lambda i, j: (i, j))
    return pl.pallas_call(
        body, name=name, grid=(T // tm, Fd // tn),
        in_specs=[pl.BlockSpec((tm, D), lambda i, j: (i, 0)), pl.BlockSpec((tn, D), lambda i, j: (j, 0)),
                  o_spec, o_spec],
        out_specs=[o_spec, o_spec],
        out_shape=[jax.ShapeDtypeStruct((T, Fd), BF16), jax.ShapeDtypeStruct((T, Fd), BF16)],
        compiler_params=_params(("parallel", "arbitrary")),
    )(dy, wd, gate, up)


def _is_a(shape):
    return lax.broadcasted_iota(jnp.int32, shape, 1) % LANES < HEAD


def _split2(v):
    hi = v.astype(BF16)
    return hi, (v - hi.astype(F32)).astype(BF16)


def _split3(v):
    hi = v.astype(BF16)
    r = v - hi.astype(F32)
    mid = r.astype(BF16)
    return hi, mid, (r - mid.astype(F32)).astype(BF16)


def _dot_split(v, m, pieces, dims=_NN):
    parts = _split3(v) if pieces == 3 else _split2(v)
    out = _dot(parts[0], m, dims)
    for p in parts[1:]:
        out = out + _dot(p, m, dims)
    return out


def _head_blockdiag():
    r = lax.broadcasted_iota(jnp.int32, (LANES, LANES), 0) // HEAD
    c = lax.broadcasted_iota(jnp.int32, (LANES, LANES), 1) // HEAD
    return (r == c).astype(BF16)


def _swap_half(v):
    lane = lax.broadcasted_iota(jnp.int32, v.shape, 1)
    return jnp.where(lane % HEAD < HEAD // 2, pltpu.roll(v, LANES - HEAD // 2, axis=1), pltpu.roll(v, HEAD // 2, axis=1))


def _attn_prep_fwd(qkv, cos, sin, gq, gk, D, HKV):
    T, QW = qkv.shape
    tr = 256
    nq, nk = D // LANES, HKV // 2
    KW = HKV * LANES

    def body(x_ref, cos_ref, sin_ref, gq_ref, gk_ref, q_ref, k_ref, v_ref):
        bd = _head_blockdiag()
        cs, sn = cos_ref[...], sin_ref[...]
        isa = _is_a((tr, LANES))

        def normrope(xv, g):
            ms = _dot_split(xv * xv, bd, 2) * (1.0 / HEAD)
            xn = xv * lax.rsqrt(ms + EPS) * g
            return xn * cs + _swap_half(xn) * sn

        def dup(v):
            r = pltpu.roll(v, HEAD, axis=1)
            return jnp.where(isa, v, r), jnp.where(isa, r, v)

        for s in range(nq):
            sl = slice(s * LANES, (s + 1) * LANES)
            q_ref[:, sl] = normrope(x_ref[:, sl], gq_ref[...]).astype(BF16)
        for s in range(nk):
            ka, kb = dup(normrope(x_ref[:, D + s * LANES:D + (s + 1) * LANES], gk_ref[...]))
            k_ref[:, 2 * s * LANES:(2 * s + 1) * LANES] = ka.astype(BF16)
            k_ref[:, (2 * s + 1) * LANES:(2 * s + 2) * LANES] = kb.astype(BF16)
            va, vb = dup(x_ref[:, D + (nk + s) * LANES:D + (nk + s + 1) * LANES])
            v_ref[:, 2 * s * LANES:(2 * s + 1) * LANES] = va.astype(BF16)
            v_ref[:, (2 * s + 1) * LANES:(2 * s + 2) * LANES] = vb.astype(BF16)

    tab = pl.BlockSpec((tr, LANES), lambda i: (i, 0))
    vec = pl.BlockSpec((1, LANES), lambda i: (0, 0))
    return pl.pallas_call(
        body, name="attn_prep_fwd", grid=(T // tr,),
        in_specs=[pl.BlockSpec((tr, QW), lambda i: (i, 0)), tab, tab, vec, vec],
        out_specs=[pl.BlockSpec((tr, D), lambda i: (i, 0)), pl.BlockSpec((tr, KW), lambda i: (i, 0)),
                   pl.BlockSpec((tr, KW), lambda i: (i, 0))],
        out_shape=[jax.ShapeDtypeStruct((T, D), BF16), jax.ShapeDtypeStruct((T, KW), BF16),
                   jax.ShapeDtypeStruct((T, KW), BF16)],
        compiler_params=_params(("parallel",)),
    )(qkv, cos, sin, gq, gk)


def _attn_prep_bwd(qkv, dq, dkd, dvd, cos, sin, gq, gk, D, HKV):
    T, QW = qkv.shape
    tr = 256
    nq, nk = D // LANES, HKV // 2
    KW = HKV * LANES

    def body(x_ref, dq_ref, dk_ref, dv_ref, cos_ref, sin_ref, gq_ref, gk_ref, o_ref, dgq_ref, dgk_ref):
        @pl.when(pl.program_id(0) == 0)
        def _():
            dgq_ref[...] = jnp.zeros_like(dgq_ref)
            dgk_ref[...] = jnp.zeros_like(dgk_ref)

        bd = _head_blockdiag()
        cs, sn = cos_ref[...], sin_ref[...]
        isa = _is_a((tr, LANES))

        def back(xv, dy, g):
            rstd = lax.rsqrt(_dot_split(xv * xv, bd, 2) * (1.0 / HEAD) + EPS)
            xhat = xv * rstd
            dxn = dy * cs + _swap_half(dy * sn)
            dxh = dxn * g
            mean = _dot_split(dxh * xhat, bd, 2) * (1.0 / HEAD)
            return rstd * (dxh - xhat * mean), jnp.sum(dxn * xhat, axis=0, keepdims=True)

        def fold(s):
            a = dk_ref[:, 2 * s * LANES:(2 * s + 1) * LANES]
            b = dk_ref[:, (2 * s + 1) * LANES:(2 * s + 2) * LANES]
            return jnp.where(isa, a + pltpu.roll(a, HEAD, axis=1), b + pltpu.roll(b, HEAD, axis=1))

        def foldv(s):
            a = dv_ref[:, 2 * s * LANES:(2 * s + 1) * LANES]
            b = dv_ref[:, (2 * s + 1) * LANES:(2 * s + 2) * LANES]
            return jnp.where(isa, a + pltpu.roll(a, HEAD, axis=1), b + pltpu.roll(b, HEAD, axis=1))

        dgq = jnp.zeros((1, LANES), F32)
        for s in range(nq):
            sl = slice(s * LANES, (s + 1) * LANES)
            dx, dg = back(x_ref[:, sl], dq_ref[:, sl], gq_ref[...])
            o_ref[:, sl] = dx.astype(BF16)
            dgq = dgq + dg
        dgq_ref[...] += dgq
        dgk = jnp.zeros((1, LANES), F32)
        for s in range(nk):
            sl = slice(D + s * LANES, D + (s + 1) * LANES)
            dx, dg = back(x_ref[:, sl], fold(s), gk_ref[...])
            o_ref[:, sl] = dx.astype(BF16)
            dgk = dgk + dg
            o_ref[:, D + (nk + s) * LANES:D + (nk + s + 1) * LANES] = foldv(s).astype(BF16)
        dgk_ref[...] += dgk

    tab = pl.BlockSpec((tr, LANES), lambda i: (i, 0))
    vec = pl.BlockSpec((1, LANES), lambda i: (0, 0))
    kv = pl.BlockSpec((tr, KW), lambda i: (i, 0))
    return pl.pallas_call(
        body, name="attn_prep_bwd", grid=(T // tr,),
        in_specs=[pl.BlockSpec((tr, QW), lambda i: (i, 0)), pl.BlockSpec((tr, D), lambda i: (i, 0)), kv, kv,
                  tab, tab, vec, vec],
        out_specs=[pl.BlockSpec((tr, QW), lambda i: (i, 0)), vec, vec],
        out_shape=[jax.ShapeDtypeStruct((T, QW), BF16), jax.ShapeDtypeStruct((1, LANES), F32),
                   jax.ShapeDtypeStruct((1, LANES), F32)],
        compiler_params=_params(("arbitrary",)),
    )(qkv, dq, dkd, dvd, cos, sin, gq, gk)


def _attn_probs(qs, kw, sink_ref, first, scale):
    rows = qs.shape[0]
    qi = lax.broadcasted_iota(jnp.int32, (rows, 2 * WINDOW), 0) % WINDOW
    kj = lax.broadcasted_iota(jnp.int32, (rows, 2 * WINDOW), 1)
    valid = (kj > qi) & (kj <= qi + WINDOW)
    if first is not False:
        valid = valid & jnp.logical_or(jnp.logical_not(first), kj >= WINDOW)
    isa = _is_a(kw.shape)
    out = []
    for pos in (0, 1):
        kp = jnp.where(isa if pos == 0 else ~isa, kw, jnp.zeros_like(kw))
        s = jnp.where(valid, _dot(qs, kp, _NT) * scale, -jnp.inf)
        sink = sink_ref[0, pos]
        m = jnp.maximum(jnp.max(s, axis=1, keepdims=True), sink)
        p = jnp.exp(s - m)
        ps = jnp.exp(sink - m)
        inv = 1.0 / (jnp.sum(p, axis=1, keepdims=True) + ps)
        out.append((p * inv, ps * inv, kp))
    return out


def _attn_specs(qb):
    q = pl.BlockSpec((qb * WINDOW, ATT_GW), lambda g, n: (n, g))
    cur = pl.BlockSpec((qb * WINDOW, LANES), lambda g, n: (n, g))
    prev = pl.BlockSpec((WINDOW, LANES), lambda g, n: (jnp.maximum(qb * n - 1, 0), g))
    sink = pl.BlockSpec((1, 2, ATT_GW, 1), lambda g, n: (g, 0, 0, 0))
    return q, cur, prev, sink


def _stack(ref, s):
    rows = slice(s * WINDOW, (s + 1) * WINDOW)
    return jnp.concatenate([ref[rows, i * LANES:(i + 1) * LANES] for i in range(ATT_GW // LANES)], axis=0)


def _attn_fwd(q, kd, vd, sinkcol, HKV, carry=None):
    T, D = q.shape
    nb = T // WINDOW
    qb = math.gcd(nb, ATT_STEP_BLOCKS)
    scale = HEAD ** -0.5

    def body(q_ref, kp_ref, kc_ref, vp_ref, vc_ref, sink_ref, o_ref):
        n = pl.program_id(1)
        kall = jnp.concatenate([kp_ref[...], kc_ref[...]], axis=0)
        vall = jnp.concatenate([vp_ref[...], vc_ref[...]], axis=0)
        isa = _is_a((2 * WINDOW, LANES))
        for s in range(qb):
            win = slice(s * WINDOW, (s + 2) * WINDOW)
            kw, vw = kall[win], vall[win]
            o = jnp.zeros((ATT_GW, LANES), F32)
            first = (n == 0) if s == 0 else False
            for pos, (probs, _, _) in enumerate(_attn_probs(_stack(q_ref, s), kw, sink_ref, first, scale)):
                vp = jnp.where(isa if pos == 0 else ~isa, vw, jnp.zeros_like(vw))
                o = o + _dot(probs.astype(BF16), vp)
            for i in range(ATT_GW // LANES):
                o_ref[s * WINDOW:(s + 1) * WINDOW, i * LANES:(i + 1) * LANES] = o[i * WINDOW:(i + 1) * WINDOW].astype(BF16)

    qs_, cur, prev, sink = _attn_specs(qb)
    res = _call(
        body, name="attn_fwd", grid=(HKV, nb // qb), in_specs=[qs_, prev, cur, prev, cur, sink], out_specs=[qs_],
        out_shape=[jax.ShapeDtypeStruct((T, D), BF16)], sem=("parallel", "parallel"),
        args=(q, kd, kd, vd, vd, sinkcol), carry=carry)
    return res[0] if carry is None else (res[0], res[1:])


def _attn_bwd(q, kd, vd, o, do, sinkcol, HKV, carry=None):
    T, D = q.shape
    nb = T // WINDOW
    qb = math.gcd(nb, ATT_STEP_BLOCKS)
    scale = HEAD ** -0.5
    KW = HKV * LANES

    def body(q_ref, kp_ref, kc_ref, vp_ref, vc_ref, o_ref, do_ref, sink_ref, dq_ref, dk_ref, dv_ref, ds_ref):
        n = pl.program_id(1)

        @pl.when(n == 0)
        def _():
            dk_ref[...] = jnp.zeros_like(dk_ref)
            dv_ref[...] = jnp.zeros_like(dv_ref)
            ds_ref[...] = jnp.zeros_like(ds_ref)

        kall = jnp.concatenate([kp_ref[...], kc_ref[...]], axis=0)
        vall = jnp.concatenate([vp_ref[...], vc_ref[...]], axis=0)
        isa_w = _is_a((2 * WINDOW, LANES))
        isa_q = _is_a((ATT_GW, LANES))
        for s in range(qb):
            win = slice(s * WINDOW, (s + 2) * WINDOW)
            kw, vw = kall[win], vall[win]
            qs = _stack(q_ref, s)
            dos = _stack(do_ref, s)
            dd = dos * _stack(o_ref, s).astype(F32)
            dob = dos.astype(BF16)
            dqs = jnp.zeros((ATT_GW, LANES), F32)
            dkw, dvw, dsk = [], [], []
            first = (n == 0) if s == 0 else False
            for pos, (probs, psink, kp) in enumerate(_attn_probs(qs, kw, sink_ref, first, scale)):
                sel_w = isa_w if pos == 0 else ~isa_w
                sel_q = isa_q if pos == 0 else ~isa_q
                delta = jnp.sum(jnp.where(sel_q, dd, 0.0), axis=1, keepdims=True)
                vp = jnp.where(sel_w, vw, jnp.zeros_like(vw))
                dp = _dot(dob, vp, _NT)
                dsb = (probs * (dp - delta) * scale).astype(BF16)
                dqs = dqs + _dot(dsb, kp)
                dkw.append(_dot(dsb, qs, _TN))
                dvw.append(_dot(probs.astype(BF16), dob, _TN))
                dsk.append(-psink * delta)
            dkw = jnp.where(isa_w, dkw[0], dkw[1])
            dvw = jnp.where(isa_w, dvw[0], dvw[1])

            def add_window(dkw=dkw, dvw=dvw, s=s):
                start = pl.multiple_of((qb * n + s - 1) * WINDOW, WINDOW)
                dk_ref[pl.ds(start, 2 * WINDOW), :] += dkw
                dv_ref[pl.ds(start, 2 * WINDOW), :] += dvw

            if s == 0:
                @pl.when(n == 0)
                def _(dkw=dkw, dvw=dvw):
                    dk_ref[0:WINDOW, :] += dkw[WINDOW:]
                    dv_ref[0:WINDOW, :] += dvw[WINDOW:]

                pl.when(n > 0)(add_window)
            else:
                add_window()

            rows = []
            for i in range(ATT_GW // LANES):
                dq_ref[s * WINDOW:(s + 1) * WINDOW, i * LANES:(i + 1) * LANES] = dqs[i * WINDOW:(i + 1) * WINDOW]
                for pos in (0, 1):
                    t = jnp.sum(dsk[pos][i * WINDOW:(i + 1) * WINDOW], axis=0, keepdims=True)
                    rows.append(jnp.broadcast_to(t, (1, LANES)))
            ds_ref[0] += jnp.concatenate(rows, axis=0)

    qs_, cur, prev, sink = _attn_specs(qb)
    dqo = pl.BlockSpec((qb * WINDOW, ATT_GW), lambda g, n: (n, g))
    dkv = pl.BlockSpec((T, LANES), lambda g, n: (0, g))
    res = _call(
        body, name="attn_bwd", grid=(HKV, nb // qb),
        in_specs=[qs_, prev, cur, prev, cur, qs_, dqo, sink],
        out_specs=[dqo, dkv, dkv, pl.BlockSpec((1, ATT_GROUP, LANES), lambda g, n: (g, 0, 0))],
        out_shape=[jax.ShapeDtypeStruct((T, D), F32), jax.ShapeDtypeStruct((T, KW), F32),
                   jax.ShapeDtypeStruct((T, KW), F32), jax.ShapeDtypeStruct((HKV, ATT_GROUP, LANES), F32)],
        sem=("parallel", "arbitrary"), args=(q, kd, kd, vd, vd, o, do, sinkcol), carry=carry)
    return res if carry is None else (*res[:4], res[4:])


def _conv_fwd(zx, w, b, DI, CD):
    T = zx.shape[0]
    cw, tc = _tile(math.gcd(DI, CD), 512), 512
    off = DI // cw

    def body(cur_ref, halo_ref, w_ref, b_ref, o_ref):
        i = pl.program_id(1)
        halo = jnp.where(i > 0, halo_ref[...], 0.0)
        ext = jnp.concatenate([halo, cur_ref[...]], axis=0)
        acc = b_ref[...] + w_ref[SSM_CONV - 1:SSM_CONV, :] * ext[HALO:]
        for k in range(SSM_CONV - 1):
            acc = acc + w_ref[k:k + 1, :] * pltpu.roll(ext, SSM_CONV - 1 - k, axis=0)[HALO:]
        o_ref[...] = acc * jax.nn.sigmoid(acc)

    return pl.pallas_call(
        body, name="ssm_conv_fwd", grid=(CD // cw, T // tc),
        in_specs=[pl.BlockSpec((tc, cw), lambda j, i: (i, off + j)),
                  pl.BlockSpec((HALO, cw), lambda j, i: (jnp.maximum(i * (tc // HALO) - 1, 0), off + j)),
                  pl.BlockSpec((SSM_CONV, cw), lambda j, i: (0, j)), pl.BlockSpec((1, cw), lambda j, i: (0, j))],
        out_specs=pl.BlockSpec((tc, cw), lambda j, i: (i, j)),
        out_shape=jax.ShapeDtypeStruct((T, CD), F32),
        compiler_params=_params(("parallel", "parallel")),
    )(zx, zx, w, b)


def _conv_bwd(zx, dxc, w, b, DI, CD):
    T = zx.shape[0]
    cw, tc = _tile(math.gcd(DI, CD), 512), 512
    off = DI // cw
    nt = T // tc
    hb = tc // HALO

    def body(cur_ref, prev_ref, next_ref, d_ref, dnext_ref, w_ref, b_ref, o_ref, dw_ref, db_ref):
        i = pl.program_id(1)

        @pl.when(i == 0)
        def _():
            dw_ref[...] = jnp.zeros_like(dw_ref)
            db_ref[...] = jnp.zeros_like(db_ref)

        prev = jnp.where(i > 0, prev_ref[...], 0.0)
        ext = jnp.concatenate([prev, cur_ref[...], next_ref[...]], axis=0)
        u = b_ref[...] + w_ref[SSM_CONV - 1:SSM_CONV, :] * ext
        for k in range(SSM_CONV - 1):
            u = u + w_ref[k:k + 1, :] * pltpu.roll(ext, SSM_CONV - 1 - k, axis=0)
        u = u[HALO:]
        dnext = jnp.where(i < nt - 1, dnext_ref[...], 0.0)
        dxe = jnp.concatenate([d_ref[...], dnext], axis=0)
        sg = jax.nn.sigmoid(u)
        du = dxe * sg * (1.0 + u * (1.0 - sg))
        n_e = tc + HALO
        dx = w_ref[SSM_CONV - 1:SSM_CONV, :] * du
        for k in range(SSM_CONV - 1):
            dx = dx + w_ref[k:k + 1, :] * pltpu.roll(du, n_e - (SSM_CONV - 1 - k), axis=0)
        o_ref[...] = dx[:tc].astype(BF16)
        duc = du[:tc]
        db_ref[...] += jnp.sum(duc, axis=0, keepdims=True)
        xs = ext[:n_e]
        dws = []
        for k in range(SSM_CONV):
            sh = xs if k == SSM_CONV - 1 else pltpu.roll(xs, SSM_CONV - 1 - k, axis=0)
            dws.append(jnp.sum(duc * sh[HALO:], axis=0, keepdims=True))
        dw_ref[...] += jnp.concatenate(dws, axis=0)

    return pl.pallas_call(
        body, name="ssm_conv_bwd", grid=(CD // cw, nt),
        in_specs=[pl.BlockSpec((tc, cw), lambda j, i: (i, off + j)),
                  pl.BlockSpec((HALO, cw), lambda j, i: (jnp.maximum(i * hb - 1, 0), off + j)),
                  pl.BlockSpec((HALO, cw), lambda j, i: (jnp.minimum((i + 1) * hb, nt * hb - 1), off + j)),
                  pl.BlockSpec((tc, cw), lambda j, i: (i, j)),
                  pl.BlockSpec((HALO, cw), lambda j, i: (jnp.minimum((i + 1) * hb, nt * hb - 1), j)),
                  pl.BlockSpec((SSM_CONV, cw), lambda j, i: (0, j)), pl.BlockSpec((1, cw), lambda j, i: (0, j))],
        out_specs=[pl.BlockSpec((tc, cw), lambda j, i: (i, j)), pl.BlockSpec((SSM_CONV, cw), lambda j, i: (0, j)),
                   pl.BlockSpec((1, cw), lambda j, i: (0, j))],
        out_shape=[jax.ShapeDtypeStruct((T, CD), BF16), jax.ShapeDtypeStruct((SSM_CONV, CD), F32),
                   jax.ShapeDtypeStruct((1, CD), F32)],
        compiler_params=_params(("parallel", "arbitrary")),
    )(zx, zx, zx, dxc, dxc, w, b)


def _tri_dot(v, upper):
    L = v.shape[0]
    r = lax.broadcasted_iota(jnp.int32, (L, L), 0)
    c = lax.broadcasted_iota(jnp.int32, (L, L), 1)
    tri = ((r <= c) if upper else (r >= c)).astype(BF16)
    p = _split3(v)
    return _dot(tri, p[0]) + _dot(tri, p[1]) + _dot(tri, p[2])


def _ssd_time2(dtraw_ref, bias_ref, alog_ref, sel):
    dt = jax.nn.softplus(dtraw_ref[...] + bias_ref[...])
    acum = _tri_dot(dt * (-jnp.exp(alog_ref[...])), False)
    return dt, _dot_split(dt, sel, 3), _dot_split(acum, sel, 3)


def _decay(acs, acs_t, pos):
    L = acs.shape[0]
    r = lax.broadcasted_iota(jnp.int32, (L, L), 0)
    c = lax.broadcasted_iota(jnp.int32, (L, L), 1)
    col = acs[:, HEAD * pos:HEAD * pos + 1]
    row = acs_t[HEAD * pos:HEAD * pos + 1, :]
    return jnp.exp(jnp.where(r >= c, col - row, -jnp.inf))


def _ssd_specs(G, GW, DI, ZW):
    L = SSM_CHUNK
    grp = lambda f: pl.BlockSpec((L, GW), lambda g, c: (f(c), g))
    return dict(
        grp=grp,
        bmat=lambda f: pl.BlockSpec((L, SSM_STATE), lambda g, c: (f(c), DI // SSM_STATE + g)),
        cmat=lambda f: pl.BlockSpec((L, SSM_STATE), lambda g, c: (f(c), DI // SSM_STATE + G + g)),
        dtraw=lambda f: pl.BlockSpec((L, LANES), lambda g, c: (f(c), (2 * DI + 2 * G * SSM_STATE) // LANES)),
        vec=pl.BlockSpec((1, LANES), lambda g, c: (0, 0)),
        gvec=pl.BlockSpec((1, GW), lambda g, c: (0, g)),
        sel=pl.BlockSpec((1, LANES, GW), lambda g, c: (g, 0, 0)),
        selt=pl.BlockSpec((1, GW, LANES), lambda g, c: (g, 0, 0)),
    )


def _ssd_fwd(zx, xc, bias, alog, sel, dskip, ng, DI, carry=None):
    T, ZW = zx.shape
    G, L = SSM_GROUPS, SSM_CHUNK
    GW = DI // G
    NS = GW // LANES
    nc = T // L
    sp = _ssd_specs(G, GW, DI, ZW)
    ident = lambda c: c

    def body(x_ref, b_ref, c_ref, z_ref, dtraw_ref, bias_ref, alog_ref, sel_ref, d_ref, ng_ref,
             y_ref, yo_ref, st_ref, state):
        c = pl.program_id(1)

        @pl.when(c == 0)
        def _():
            state[...] = jnp.zeros_like(state)

        x = x_ref[...]
        bb, cb_ = b_ref[...].astype(BF16), c_ref[...].astype(BF16)
        cbm = _dot(cb_, bb, _NT)
        _, dtx, acx = _ssd_time2(dtraw_ref, bias_ref, alog_ref, sel_ref[0])
        xdt = x * dtx
        ex = jnp.exp(acx)
        last = acx[L - 1:L, :]
        te = jnp.exp(last - acx)
        dlast = jnp.exp(last)
        isa = _is_a((L, LANES))
        for i in range(NS):
            sl = slice(i * LANES, (i + 1) * LANES)
            acs = acx[:, sl]
            acs_t = acs.T
            xs = xdt[:, sl]
            y = jnp.zeros((L, LANES), F32)
            for pos in (0, 1):
                m = (cbm * _decay(acs, acs_t, pos)).astype(BF16)
                y = y + _dot(m, jnp.where(isa if pos == 0 else ~isa, xs, 0.0).astype(BF16))
            st = state[i]
            st_ref[0, i] = st
            y = y + _dot(cb_, st.astype(BF16)) * ex[:, sl]
            state[i] = st * dlast[:, sl] + _dot(bb, (xs * te[:, sl]).astype(BF16), _TN)
            y_ref[:, sl] = y + d_ref[:, sl] * x[:, sl]
        z = z_ref[...]
        gated = y_ref[...] * (z * jax.nn.sigmoid(z))
        rstd = lax.rsqrt(jnp.mean(gated * gated, axis=1, keepdims=True) + EPS)
        yo_ref[...] = (gated * rstd * ng_ref[...]).astype(BF16)

    res = _call(
        body, name="ssd_fwd", grid=(G, nc),
        in_specs=[sp["grp"](ident), sp["bmat"](ident), sp["cmat"](ident), sp["grp"](ident), sp["dtraw"](ident),
                  sp["vec"], sp["vec"], sp["sel"], sp["gvec"], sp["gvec"]],
        out_specs=[sp["grp"](ident), sp["grp"](ident),
                   pl.BlockSpec((1, NS, SSM_STATE, LANES), lambda g, c: (c, g, 0, 0))],
        out_shape=[jax.ShapeDtypeStruct((T, DI), F32), jax.ShapeDtypeStruct((T, DI), BF16),
                   jax.ShapeDtypeStruct((nc, G * NS, SSM_STATE, LANES), F32)],
        scratch_shapes=[pltpu.VMEM((NS, SSM_STATE, LANES), F32)],
        sem=("parallel", "arbitrary"), args=(xc, xc, xc, zx, zx, bias, alog, sel, dskip, ng), carry=carry)
    return res if carry is None else (*res[:3], res[3:])


def _ssd_bwd(zx, xc, yssd, dyo, states, bias, alog, sel, selt, hsel, dskip, ng, DI, carry=None):
    T, ZW = zx.shape
    G, L = SSM_GROUPS, SSM_CHUNK
    GW = DI // G
    NS = GW // LANES
    nc = T // L
    sp = _ssd_specs(G, GW, DI, ZW)
    rev = lambda c: nc - 1 - c

    def body(x_ref, b_ref, c_ref, z_ref, dtraw_ref, y_ref, dyo_ref, st_ref, bias_ref, alog_ref, sel_ref, selt_ref,
             hsel_ref, d_ref, ng_ref, dz_ref, dx_ref, db_ref, dc_ref, ddt_ref, dac_ref, dd_ref, dng_ref, dstate):
        c = pl.program_id(1)

        @pl.when(c == 0)
        def _():
            dstate[...] = jnp.zeros_like(dstate)
            dd_ref[...] = jnp.zeros_like(dd_ref)
            dng_ref[...] = jnp.zeros_like(dng_ref)

        z, ys, dyo = z_ref[...], y_ref[...], dyo_ref[...]
        sg = jax.nn.sigmoid(z)
        sz = z * sg
        gated = ys * sz
        rstd = lax.rsqrt(jnp.mean(gated * gated, axis=1, keepdims=True) + EPS)
        yn = gated * rstd
        dng_ref[0] += jnp.sum(dyo * yn, axis=0, keepdims=True)
        dyn = dyo * ng_ref[...]
        dgated = rstd * (dyn - yn * jnp.mean(dyn * yn, axis=1, keepdims=True))
        g = dgated * sz
        dz_ref[...] = (dgated * ys * sg * (1.0 + z * (1.0 - sg))).astype(BF16)

        x = x_ref[...]
        dsk = d_ref[...]
        dd_ref[0] += jnp.sum(g * x, axis=0, keepdims=True)
        bb, cb_ = b_ref[...].astype(BF16), c_ref[...].astype(BF16)
        cbm = _dot(cb_, bb, _NT)
        _, dtx, acx = _ssd_time2(dtraw_ref, bias_ref, alog_ref, sel_ref[0])
        xdt = x * dtx
        ex = jnp.exp(acx)
        last = acx[L - 1:L, :]
        te = jnp.exp(last - acx)
        dlast = jnp.exp(last)
        isa = _is_a((L, LANES))
        is_last = lax.broadcasted_iota(jnp.int32, (L, LANES), 0) == L - 1
        strict = lax.broadcasted_iota(jnp.int32, (L, L), 0) > lax.broadcasted_iota(jnp.int32, (L, L), 1)
        dcb = jnp.zeros((L, L), F32)
        dcm = jnp.zeros((L, SSM_STATE), F32)
        dbm = jnp.zeros((L, SSM_STATE), F32)
        for i in range(NS):
            sl = slice(i * LANES, (i + 1) * LANES)
            acs = acx[:, sl]
            acs_t = acs.T
            xs, gs = xdt[:, sl], g[:, sl]
            xsb = xs.astype(BF16)
            dxd = jnp.zeros((L, LANES), F32)
            dac_c = jnp.zeros((L, LANES), F32)
            for pos in (0, 1):
                gp = jnp.where(isa if pos == 0 else ~isa, gs, 0.0).astype(BF16)
                dec = _decay(acs, acs_t, pos)
                dxd = dxd + _dot((cbm * dec).astype(BF16), gp, _TN)
                dmd = _dot(gp, xsb, _NT) * dec
                dcb = dcb + dmd
                q = jnp.where(strict, dmd * cbm, 0.0)
                hot = jnp.broadcast_to(hsel_ref[0, 2 * i + pos:2 * i + pos + 1, :], (L, LANES)).astype(BF16)
                dac_c = dac_c + _dot_split(q, hot, 2) - _dot_split(q, hot, 2, _TN)
            st = st_ref[0, i]
            dst = dstate[i]
            stb, dstb = st.astype(BF16), dst.astype(BF16)
            eg = (ex[:, sl] * gs).astype(BF16)
            dcm = dcm + _dot(eg, stb, _NT)
            yoff = _dot(cb_, stb) * ex[:, sl]
            w = xs * te[:, sl]
            wb = w.astype(BF16)
            dw = _dot(bb, dstb)
            dbm = dbm + _dot(wb, dstb, _NT)
            dxt = dxd + dw * te[:, sl]
            dal = dlast[:, sl] * jnp.sum(dst * st, axis=0, keepdims=True) + jnp.sum(dw * w, axis=0, keepdims=True)
            dac_l = gs * yoff - w * dw + jnp.where(is_last, dal, 0.0)
            ddt_l = dxt * x[:, sl]
            dstate[i] = dst * dlast[:, sl] + _dot(cb_, eg, _TN)
            dx_ref[:, sl] = dxt * dtx[:, sl] + dsk[:, sl] * gs
            part = _dot_split(ddt_l, selt_ref[0, sl, :], 2)
            parta = dac_c + _dot_split(dac_l, selt_ref[0, sl, :], 2)
            if i == 0:
                ddt_ref[0] = part
                dac_ref[0] = parta
            else:
                ddt_ref[0] += part
                dac_ref[0] += parta
        dcbb = dcb.astype(BF16)
        dc_ref[...] = dcm + _dot(dcbb, bb)
        db_ref[...] = dbm + _dot(dcbb, cb_, _TN)

    part_spec = pl.BlockSpec((1, L, LANES), lambda g, c: (g, rev(c), 0))
    lane_spec = pl.BlockSpec((1, 1, GW), lambda g, c: (g, 0, 0))
    bc_out = pl.BlockSpec((L, SSM_STATE), lambda g, c: (rev(c), g))
    res = _call(
        body, name="ssd_bwd", grid=(G, nc),
        in_specs=[sp["grp"](rev), sp["bmat"](rev), sp["cmat"](rev), sp["grp"](rev), sp["dtraw"](rev), sp["grp"](rev),
                  sp["grp"](rev), pl.BlockSpec((1, NS, SSM_STATE, LANES), lambda g, c: (rev(c), g, 0, 0)),
                  sp["vec"], sp["vec"], sp["sel"], sp["selt"], pl.BlockSpec((1, 8, LANES), lambda g, c: (g, 0, 0)),
                  sp["gvec"], sp["gvec"]],
        out_specs=[sp["grp"](rev), sp["grp"](rev), bc_out, bc_out, part_spec, part_spec, lane_spec, lane_spec],
        out_shape=[jax.ShapeDtypeStruct((T, DI), BF16), jax.ShapeDtypeStruct((T, DI), F32),
                   jax.ShapeDtypeStruct((T, G * SSM_STATE), F32), jax.ShapeDtypeStruct((T, G * SSM_STATE), F32),
                   jax.ShapeDtypeStruct((G, T, LANES), F32), jax.ShapeDtypeStruct((G, T, LANES), F32),
                   jax.ShapeDtypeStruct((G, 1, GW), F32), jax.ShapeDtypeStruct((G, 1, GW), F32)],
        scratch_shapes=[pltpu.VMEM((NS, SSM_STATE, LANES), F32)], sem=("parallel", "arbitrary"),
        args=(xc, xc, xc, zx, zx, yssd, dyo, states, bias, alog, sel, selt, hsel, dskip, ng), carry=carry)
    return res if carry is None else (*res[:8], res[8:])


def _ssd_dt_bwd(zx, ddt_part, dac_part, bias, alog, DI):
    T = zx.shape[0]
    G, L = SSM_GROUPS, SSM_CHUNK
    nc = T // L
    dt_block = (2 * DI + 2 * G * SSM_STATE) // LANES

    def body(dtraw_ref, ddt_ref, dac_ref, bias_ref, alog_ref, o_ref, dal_ref, dbias_ref):
        @pl.when(pl.program_id(0) == 0)
        def _():
            dal_ref[...] = jnp.zeros_like(dal_ref)
            dbias_ref[...] = jnp.zeros_like(dbias_ref)

        raw = dtraw_ref[...] + bias_ref[...]
        dt = jax.nn.softplus(raw)
        a = -jnp.exp(alog_ref[...])
        dac, ddt = dac_ref[0], ddt_ref[0]
        for gi in range(1, G):
            dac = dac + dac_ref[gi]
            ddt = ddt + ddt_ref[gi]
        dda = _tri_dot(dac, True)
        dal_ref[...] += jnp.sum(dda * dt, axis=0, keepdims=True) * a
        draw = (dda * a + ddt) * jax.nn.sigmoid(raw)
        dbias_ref[...] += jnp.sum(draw, axis=0, keepdims=True)
        o_ref[...] = draw.astype(BF16)

    vec = pl.BlockSpec((1, LANES), lambda c: (0, 0))
    part = pl.BlockSpec((G, L, LANES), lambda c: (0, c, 0))
    return pl.pallas_call(
        body, name="ssd_dt_bwd", grid=(nc,),
        in_specs=[pl.BlockSpec((L, LANES), lambda c: (c, dt_block)), part, part, vec, vec],
        out_specs=[pl.BlockSpec((L, LANES), lambda c: (c, 0)), vec, vec],
        out_shape=[jax.ShapeDtypeStruct((T, LANES), BF16), jax.ShapeDtypeStruct((1, LANES), F32),
                   jax.ShapeDtypeStruct((1, LANES), F32)],
        compiler_params=_params(("arbitrary",)),
    )(zx, ddt_part, dac_part, bias, alog)


def _all_gather(shards, name):
    n = len(shards)

    def body(*refs):
        ins, outs = refs[:n], refs[n:2 * n]
        send_sems, recv_sems, local_sems = refs[2 * n:]
        x, y, c = lax.axis_index("x"), lax.axis_index("y"), lax.axis_index("c")
        me, sibling = (x, y, c), (x, y, 1 - c)
        chips = [(1 - x, y), (x, 1 - y), (1 - x, 1 - y)]

        def copy(w, k, block, to, src=None):
            dst = outs[w].at[_slot(*block)]
            return pltpu.make_async_remote_copy(
                src_ref=dst if src is None else src, dst_ref=dst, send_sem=send_sems.at[w, k],
                recv_sem=recv_sems.at[w, k], device_id=to, device_id_type=MESH)

        mine, first, passed = [], [], []
        for w in range(n):
            cp = pltpu.make_async_copy(ins[w], outs[w].at[_slot(*me)], local_sems.at[w])
            cp.start()
            mine.append(cp)
            fw = [copy(w, 0, me, sibling, src=ins[w])]
            fw += [copy(w, 1 + j, me, (*chip, c), src=ins[w]) for j, chip in enumerate(chips)]
            for cp in fw:
                cp.start()
            first += fw
        for w in range(n):
            for j, chip in enumerate(chips):
                copy(w, 1 + j, (*chip, c), me).wait_recv()
                cp = copy(w, 4 + j, (*chip, c), sibling)
                cp.start()
                passed.append(cp)
        for w in range(n):
            copy(w, 0, sibling, me).wait_recv()
            for j, chip in enumerate(chips):
                copy(w, 4 + j, (*chip, 1 - c), me).wait_recv()
        for cp in first + passed:
            cp.wait_send()
        for cp in mine:
            cp.wait()

    any_spec = pl.BlockSpec(memory_space=pl.ANY)
    return pl.pallas_call(
        body, name=name, in_specs=[any_spec] * n, out_specs=[any_spec] * n,
        out_shape=[jax.ShapeDtypeStruct((N_DEV,) + s.shape, s.dtype) for s in shards],
        scratch_shapes=[pltpu.SemaphoreType.DMA((n, 7)), pltpu.SemaphoreType.DMA((n, 7)),
                        pltpu.SemaphoreType.DMA((n,))],
    )(*shards)


def _exchange(blocks, name):
    n = len(blocks)

    def body(*refs):
        local, sends, arrivals = _direct_copies(refs[:n], refs[n:2 * n], *refs[2 * n:])
        for cp in local + sends:
            cp.start()
        for cp in arrivals:
            cp.wait_recv()
        for cp in sends:
            cp.wait_send()
        for cp in local:
            cp.wait()

    any_spec = pl.BlockSpec(memory_space=pl.ANY)
    return pl.pallas_call(
        body, name=name, in_specs=[any_spec] * n, out_specs=[any_spec] * n,
        out_shape=[jax.ShapeDtypeStruct(b.shape, b.dtype) for b in blocks],
        scratch_shapes=[pltpu.SemaphoreType.DMA((n, 7)), pltpu.SemaphoreType.DMA((n, 7)),
                        pltpu.SemaphoreType.DMA((n,))],
    )(*blocks)


def _adamw(parts, w, m, v, name):
    nl = len(parts)
    R, C = parts[0].shape[1:]
    per_row = C * (N_DEV * nl * parts[0].dtype.itemsize + 7 * 4) * 2
    tr = R
    if R % 8 == 0:
        tr = 8
        for t in (16, 32, 64, 128, 256, 512):
            if R % t == 0 and t * per_row <= 24 * 1024 * 1024:
                tr = t
    nr = R // tr
    c1 = 1.0 - ADAM_B1 ** ADAM_STEP
    c2 = 1.0 - ADAM_B2 ** ADAM_STEP

    def body(*refs):
        p_refs = refs[:nl]
        w_ref, m_ref, v_ref, g_ref, d_ref, nm_ref, nv_ref = refs[nl:]
        for layer in range(nl):
            @pl.when(pl.program_id(0) == layer)
            def _(p_ref=p_refs[layer]):
                g = p_ref[0].astype(F32)
                for k in range(1, N_DEV):
                    g = g + p_ref[k].astype(F32)
                nm = ADAM_B1 * m_ref[...] + (1.0 - ADAM_B1) * g
                nv = ADAM_B2 * v_ref[...] + (1.0 - ADAM_B2) * (g * g)
                g_ref[...] = g
                nm_ref[...] = nm
                nv_ref[...] = nv
                d_ref[...] = -ADAM_LR * ((nm / c1) / (jnp.sqrt(nv / c2) + ADAM_EPS) + ADAM_WD * w_ref[...])

    def part_spec(layer):
        return pl.BlockSpec((N_DEV, tr, C), lambda l, i: (0, jnp.where(l == layer, i, jnp.where(l < layer, 0, nr - 1)), 0))

    blk = pl.BlockSpec((tr, C), lambda l, i: (l * nr + i, 0))
    out = jax.ShapeDtypeStruct((nl * R, C), F32)
    return pl.pallas_call(
        body, name=name, grid=(nl, nr),
        in_specs=[part_spec(layer) for layer in range(nl)] + [blk, blk, blk],
        out_specs=[blk, blk, blk, blk], out_shape=[out, out, out, out],
        compiler_params=_params(("arbitrary", "arbitrary")),
    )(*parts, w, m, v)


def _pad_cols(a, n):
    return jnp.pad(a, ((0, 0), (0, n - a.shape[1])))


def kernel(x, positions, mixer_norm, ffn_norm, attn_w_qkv, attn_q_norm, attn_k_norm, attn_sinks, attn_w_o, ssm_w_in, ssm_conv_w, ssm_conv_b, ssm_dt_bias, ssm_a_log, ssm_d, ssm_norm, ssm_w_out, ffn_w_gate, ffn_w_up, ffn_w_down, loss_target, m_mixer_norm, m_ffn_norm, m_attn_w_qkv, m_attn_q_norm, m_attn_k_norm, m_attn_sinks, m_attn_w_o, m_ssm_w_in, m_ssm_conv_w, m_ssm_conv_b, m_ssm_dt_bias, m_ssm_a_log, m_ssm_d, m_ssm_norm, m_ssm_w_out, m_ffn_w_gate, m_ffn_w_up, m_ffn_w_down, v_mixer_norm, v_ffn_norm, v_attn_w_qkv, v_attn_q_norm, v_attn_k_norm, v_attn_sinks, v_attn_w_o, v_ssm_w_in, v_ssm_conv_w, v_ssm_conv_b, v_ssm_dt_bias, v_ssm_a_log, v_ssm_d, v_ssm_norm, v_ssm_w_out, v_ffn_w_gate, v_ffn_w_up, v_ffn_w_down):
    T, D = x.shape[1], x.shape[2]
    HQ = D // HEAD
    HKV = HQ // ATT_GROUP
    QW = (HQ + 2 * HKV) * HEAD
    DI = 2 * D
    H = DI // HEAD
    G = SSM_GROUPS
    GW = DI // G
    CD = DI + 2 * G * SSM_STATE
    ZW = -(-(DI + CD + LANES) // 512) * 512
    IW = DI + CD + H
    assert T % 512 == 0 and D % 256 == 0 and HKV % 2 == 0 and GW % LANES == 0 and H <= LANES

    weights = dict(mixer_norm=mixer_norm, ffn_norm=ffn_norm, attn_w_qkv=attn_w_qkv, attn_q_norm=attn_q_norm,
                   attn_k_norm=attn_k_norm, attn_sinks=attn_sinks, attn_w_o=attn_w_o, ssm_w_in=ssm_w_in,
                   ssm_conv_w=ssm_conv_w, ssm_conv_b=ssm_conv_b, ssm_dt_bias=ssm_dt_bias, ssm_a_log=ssm_a_log,
                   ssm_d=ssm_d, ssm_norm=ssm_norm, ssm_w_out=ssm_w_out, ffn_w_gate=ffn_w_gate, ffn_w_up=ffn_w_up,
                   ffn_w_down=ffn_w_down)
    mom_m = dict(mixer_norm=m_mixer_norm, ffn_norm=m_ffn_norm, attn_w_qkv=m_attn_w_qkv, attn_q_norm=m_attn_q_norm,
                 attn_k_norm=m_attn_k_norm, attn_sinks=m_attn_sinks, attn_w_o=m_attn_w_o, ssm_w_in=m_ssm_w_in,
                 ssm_conv_w=m_ssm_conv_w, ssm_conv_b=m_ssm_conv_b, ssm_dt_bias=m_ssm_dt_bias, ssm_a_log=m_ssm_a_log,
                 ssm_d=m_ssm_d, ssm_norm=m_ssm_norm, ssm_w_out=m_ssm_w_out, ffn_w_gate=m_ffn_w_gate,
                 ffn_w_up=m_ffn_w_up, ffn_w_down=m_ffn_w_down)
    mom_v = dict(mixer_norm=v_mixer_norm, ffn_norm=v_ffn_norm, attn_w_qkv=v_attn_w_qkv, attn_q_norm=v_attn_q_norm,
                 attn_k_norm=v_attn_k_norm, attn_sinks=v_attn_sinks, attn_w_o=v_attn_w_o, ssm_w_in=v_ssm_w_in,
                 ssm_conv_w=v_ssm_conv_w, ssm_conv_b=v_ssm_conv_b, ssm_dt_bias=v_ssm_dt_bias, ssm_a_log=v_ssm_a_log,
                 ssm_d=v_ssm_d, ssm_norm=v_ssm_norm, ssm_w_out=v_ssm_w_out, ffn_w_gate=v_ffn_w_gate,
                 ffn_w_up=v_ffn_w_up, ffn_w_down=v_ffn_w_down)
    big = ["attn_w_qkv", "attn_w_o", "ssm_w_in", "ssm_w_out", "ffn_w_gate", "ffn_w_up", "ffn_w_down"]

    def flat2(a):
        return a.reshape(-1, a.shape[-1])

    def shard(n, layer=0):
        return weights[n][layer].astype(BF16)

    def from_cols(g):
        return g.transpose(1, 0, 2).reshape(g.shape[1], N_DEV * g.shape[2])

    def from_rows(g):
        return g.reshape(N_DEV * g.shape[1], g.shape[2])

    xs = x[0]
    tgt = loss_target[0]
    inv_freq = ROPE_THETA ** (-jnp.arange(0, HEAD, 2, dtype=F32) / HEAD)
    ang = positions[0].astype(F32)[:, None] * inv_freq
    cos = jnp.tile(jnp.cos(ang), (1, 4))
    sin = jnp.tile(jnp.concatenate([-jnp.sin(ang), jnp.sin(ang)], axis=1), (1, 2))
    gq = jnp.tile(attn_q_norm, (1, 2))
    gk = jnp.tile(attn_k_norm, (1, 2))
    sinkcol = jnp.repeat(attn_sinks.reshape(HKV, ATT_GROUP // 2, 2).transpose(0, 2, 1), WINDOW, axis=2)[..., None]
    bias_p = _pad_cols(ssm_dt_bias, LANES)
    alog_p = _pad_cols(ssm_a_log, LANES)
    dskip = jnp.repeat(ssm_d, HEAD, axis=1)
    lane_head = jnp.arange(DI) // HEAD
    sel = (jnp.arange(LANES)[None, :, None] == lane_head.reshape(G, 1, GW)).astype(BF16)
    selt = sel.transpose(0, 2, 1)
    hsel = (jnp.arange(LANES)[None, None, :] == (jnp.arange(G)[:, None, None] * (H // G) + jnp.arange(8)[None, :, None])
            ).astype(BF16) * (jnp.arange(8)[None, :, None] < H // G)
    vec_w = CD // N_DEV
    small = jnp.concatenate([ssm_conv_w[0], ssm_conv_b, _pad_cols(ssm_norm, vec_w),
                             jnp.zeros((2, vec_w), F32)], axis=0)
    g_qkv, g_o, small_all = _all_gather([shard("attn_w_qkv"), shard("attn_w_o"), small], "gather_first")
    w_qkv, w_o = from_cols(g_qkv), from_rows(g_o)
    conv_w = small_all[:, :SSM_CONV].transpose(1, 0, 2).reshape(SSM_CONV, CD)
    conv_b = small_all[:, SSM_CONV].reshape(1, CD)
    ng = small_all[:, SSM_CONV + 1, :DI // N_DEV].reshape(1, DI)

    def rows_to_blocks(p):
        return p.reshape(N_DEV, p.shape[0] // N_DEV, p.shape[1])

    def cols_to_blocks(p):
        return p.reshape(p.shape[0], N_DEV, p.shape[1] // N_DEV).transpose(1, 0, 2)

    hm0 = _rms_fwd(xs, mixer_norm[0:1], "rms_fwd_m0")
    qkv = _matmul(hm0, w_qkv, mode="nn", out_dtype=F32, name="mm_qkv")
    qr, kd, vd = _attn_prep_fwd(qkv, cos, sin, gq, gk, D, HKV)
    o, got = _attn_fwd(qr, kd, vd, sinkcol, HKV,
                       carry=[shard("ffn_w_gate", 0), shard("ffn_w_up", 0), shard("ffn_w_down", 0)])
    w_gate = [from_cols(got[0]), None]
    w_up = [from_cols(got[1]), None]
    w_down = [from_rows(got[2]), None]
    x1 = _matmul(o, w_o, mode="nn", out_dtype=F32, name="mm_attn_out", add=xs)
    hf0 = _rms_fwd(x1, ffn_norm[0:1], "rms_fwd_f0")
    gate0, up0, act0, got = _ffn_up(hf0, w_gate[0], w_up[0], "ffn_up_0", carry=[shard("ssm_w_in")])
    w_in = _pad_cols(from_cols(got[0]), ZW)
    x2, got = _matmul(act0, w_down[0], mode="nn", out_dtype=F32, name="mm_ffn_down_0", add=x1,
                      carry=[shard("ssm_w_out")])
    w_out = from_rows(got[0])
    hm1 = _rms_fwd(x2, mixer_norm[1:2], "rms_fwd_m1")
    zx, got = _matmul(hm1, w_in, mode="nn", out_dtype=F32, name="mm_ssm_in",
                      carry=[shard("ffn_w_gate", 1), shard("ffn_w_up", 1)])
    w_gate[1], w_up[1] = from_cols(got[0]), from_cols(got[1])
    xc = _conv_fwd(zx, conv_w, conv_b, DI, CD)
    yssd, yout, states, got = _ssd_fwd(zx, xc, bias_p, alog_p, sel, dskip, ng, DI, carry=[shard("ffn_w_down", 1)])
    w_down[1] = from_rows(got[0])
    x3 = _matmul(yout, w_out, mode="nn", out_dtype=F32, name="mm_ssm_out", add=x2)
    hf1 = _rms_fwd(x3, ffn_norm[1:2], "rms_fwd_f1")
    gate1, up1, act1 = _ffn_up(hf1, w_gate[1], w_up[1], "ffn_up_1")
    x4 = _matmul(act1, w_down[1], mode="nn", out_dtype=F32, name="mm_ffn_down_1", add=x3)
    sq, dx4, dx4b = _loss_head(x4, tgt)
    loss = lax.psum(sq[0, 0] * (0.5 / D), ("x", "y", "c"))

    def ffn_bwd(dy, dyb, hf, gate, up, act, layer, xin, gain):
        dg, du = _ffn_dact(dyb, w_down[layer], gate, up, f"ffn_dact_{layer}")
        g_down = _matmul(act, dyb, mode="tn", out_dtype=BF16, name=f"mm_dw_down_{layer}")
        g_gate = _matmul(hf, dg, mode="tn", out_dtype=BF16, name=f"mm_dw_gate_{layer}")
        g_up = _matmul(hf, du, mode="tn", out_dtype=BF16, name=f"mm_dw_up_{layer}")
        dh = _matmul(dg, w_gate[layer], mode="nt", out_dtype=F32, name=f"mm_dh_gate_{layer}")
        dh = _matmul(du, w_up[layer], mode="nt", out_dtype=F32, name=f"mm_dh_up_{layer}", add=dh)
        dx, dxb, dgain = _rms_bwd(xin, gain, dh, dy, f"rms_bwd_f{layer}")
        return dx, dxb, dgain, [cols_to_blocks(g_gate), cols_to_blocks(g_up), rows_to_blocks(g_down)]

    dx3, dx3b, d_fn1, ffn1_blocks = ffn_bwd(dx4, dx4b, hf1, gate1, up1, act1, 1, x3, ffn_norm[1:2])
    dyo = _matmul(dx3b, w_out, mode="nt", out_dtype=F32, name="mm_dyout")
    g_wout = _matmul(yout, dx3b, mode="tn", out_dtype=BF16, name="mm_dw_ssm_out")
    dz, dxx, dbm, dcm, ddt_p, dac_p, dd_l, dng_l, got1 = _ssd_bwd(
        zx, xc, yssd, dyo, states, bias_p, alog_p, sel, selt, hsel, dskip, ng, DI,
        carry=ffn1_blocks + [rows_to_blocks(g_wout)])
    ddt_raw, d_alog, d_bias = _ssd_dt_bwd(zx, ddt_p, dac_p, bias_p, alog_p, DI)
    dxbc, d_convw, d_convb = _conv_bwd(zx, jnp.concatenate([dxx, dbm, dcm], axis=1), conv_w, conv_b, DI, CD)
    dzx = jnp.concatenate([dz, dxbc, ddt_raw, jnp.zeros((T, ZW - DI - CD - LANES), BF16)], axis=1)
    g_win = _matmul(hm1, dzx, mode="tn", out_dtype=BF16, name="mm_dw_ssm_in")[:, :IW]
    dh, got2 = _matmul(dzx, w_in, mode="nt", out_dtype=F32, name="mm_dh_ssm_in", carry=[cols_to_blocks(g_win)])
    dx2, dx2b, d_mn1 = _rms_bwd(x2, mixer_norm[1:2], dh, dx3, "rms_bwd_m1")
    dx1, dx1b, d_fn0, ffn0_blocks = ffn_bwd(dx2, dx2b, hf0, gate0, up0, act0, 0, x1, ffn_norm[0:1])
    do = _matmul(dx1b, w_o, mode="nt", out_dtype=F32, name="mm_do")
    g_wo = _matmul(o, dx1b, mode="tn", out_dtype=BF16, name="mm_dw_attn_out")
    dq, dkd, dvd, dsink, got3 = _attn_bwd(qr, kd, vd, o, do, sinkcol, HKV, carry=ffn0_blocks + [rows_to_blocks(g_wo)])
    dqkv, dgq_l, dgk_l = _attn_prep_bwd(qkv, dq, dkd, dvd, cos, sin, gq, gk, D, HKV)
    g_wqkv = _matmul(hm0, dqkv, mode="tn", out_dtype=BF16, name="mm_dw_qkv")
    dh = _matmul(dqkv, w_qkv, mode="nt", out_dtype=F32, name="mm_dh_qkv")
    dx0, _, d_mn0 = _rms_bwd(xs, mixer_norm[0:1], dh, dx1, "rms_bwd_m0")

    d_ng = dng_l.reshape(1, DI)
    vec_send = jnp.concatenate([
        d_convw.reshape(SSM_CONV, N_DEV, vec_w).transpose(1, 0, 2), d_convb.reshape(1, N_DEV, vec_w).transpose(1, 0, 2),
        _pad_cols(d_ng.reshape(N_DEV, DI // N_DEV), vec_w)[:, None, :], jnp.zeros((N_DEV, 2, vec_w), F32)], axis=1)
    d_sinks = dsink[:, :, 0].reshape(1, HQ)
    d_gq = dgq_l[:, :HEAD] + dgq_l[:, HEAD:]
    d_gk = dgk_l[:, :HEAD] + dgk_l[:, HEAD:]
    d_dskip = dd_l.reshape(H, HEAD).sum(axis=1).reshape(1, H)
    rep_names = ["mixer_norm", "ffn_norm", "attn_q_norm", "attn_k_norm", "attn_sinks", "ssm_dt_bias", "ssm_a_log",
                 "ssm_d"]
    rep_grads = [jnp.concatenate([d_mn0, d_mn1], axis=0), jnp.concatenate([d_fn0, d_fn1], axis=0), d_gq, d_gk,
                 d_sinks, d_bias[:, :H], d_alog[:, :H], d_dskip]
    rep_sizes = [weights[n].size for n in rep_names]
    rep_len = -(-sum(rep_sizes) // (8 * LANES)) * 8 * LANES

    def pack(arrs):
        flat = jnp.concatenate([a.reshape(-1) for a in arrs])
        return jnp.pad(flat, (0, rep_len - flat.shape[0])).reshape(rep_len // LANES, LANES)

    rep_send = jnp.broadcast_to(pack(rep_grads)[None], (N_DEV, rep_len // LANES, LANES))
    got4 = _exchange([cols_to_blocks(g_wqkv), vec_send, rep_send], "exchange_last")
    parts_of = {
        "attn_w_qkv": [got4[0]], "attn_w_o": [got3[3]], "ssm_w_in": [got2[0]], "ssm_w_out": [got1[3]],
        "ffn_w_gate": [got3[0], got1[0]], "ffn_w_up": [got3[1], got1[1]], "ffn_w_down": [got3[2], got1[2]],
    }

    out = {}
    for n in big:
        res = _adamw(parts_of[n], flat2(weights[n]), flat2(mom_m[n]), flat2(mom_v[n]), f"adamw_{n}")
        out[n] = [r.reshape(weights[n].shape) for r in res]

    def vec_block(d):
        return jnp.concatenate([d["ssm_conv_w"][0], d["ssm_conv_b"], _pad_cols(d["ssm_norm"], vec_w),
                                jnp.zeros((2, vec_w), F32)], axis=0)

    res = _adamw([got4[1]], vec_block(weights), vec_block(mom_m), vec_block(mom_v), "adamw_vectors")
    out["ssm_conv_w"] = [r[:SSM_CONV][None] for r in res]
    out["ssm_conv_b"] = [r[SSM_CONV:SSM_CONV + 1] for r in res]
    out["ssm_norm"] = [r[SSM_CONV + 1:SSM_CONV + 2, :DI // N_DEV] for r in res]
    res = _adamw([got4[2]], pack([weights[n] for n in rep_names]), pack([mom_m[n] for n in rep_names]),
                 pack([mom_v[n] for n in rep_names]), "adamw_replicated")
    offs = 0
    for n, sz in zip(rep_names, rep_sizes):
        out[n] = [r.reshape(-1)[offs:offs + sz].reshape(weights[n].shape) for r in res]
        offs += sz

    names = list(weights)
    return (loss, dx0[None], *[out[n][0] for n in names], *[out[n][1] for n in names],
            *[out[n][2] for n in names], *[out[n][3] for n in names])
```

```python
import functools
import math

import jax
import jax.numpy as jnp
from jax import lax
from jax.experimental import pallas as pl
from jax.experimental.pallas import tpu as pltpu

F32 = jnp.float32
BF16 = jnp.bfloat16

N_DEV = 8
EPS = 1e-6
LANES = 128
HEAD = 64
ATT_GROUP = 8
ATT_GW = ATT_GROUP * HEAD
WINDOW = 128
ATT_STEP_BLOCKS = 4
ROPE_THETA = 10000.0
SSM_GROUPS = 8
SSM_STATE = 128
SSM_CONV = 4
SSM_CHUNK = 256
HALO = 8
ADAM_LR, ADAM_B1, ADAM_B2, ADAM_EPS, ADAM_WD, ADAM_STEP = 0.001, 0.9, 0.999, 1e-08, 0.01, 10
VMEM_LIMIT = 56 * 1024 * 1024
MATMUL_VMEM = 44 * 1024 * 1024
MESH = pl.DeviceIdType.MESH

_NN = (((1,), (0,)), ((), ()))
_NT = (((1,), (1,)), ((), ()))
_TN = (((0,), (0,)), ((), ()))


def _dot(a, b, dims=_NN):
    return lax.dot_general(a, b, dims, preferred_element_type=F32)


def _tile(n, cap):
    if n % LANES:
        return n
    best = LANES
    for t in range(LANES, min(n, cap) + 1, LANES):
        if n % t == 0:
            best = t
    return best


def _params(sem):
    return pltpu.CompilerParams(dimension_semantics=sem, vmem_limit_bytes=VMEM_LIMIT)


def _slot(px, py, pc):
    return 4 * px + 2 * py + pc


def _direct_copies(srcs, dsts, send_sems, recv_sems, local_sems, with_arrivals=True):
    x, y, c = lax.axis_index("x"), lax.axis_index("y"), lax.axis_index("c")
    me = _slot(x, y, c)
    peers = [(x ^ (m >> 2), y ^ ((m >> 1) & 1), c ^ (m & 1)) for m in range(1, N_DEV)]
    local, sends, arrivals = [], [], []
    for w, (src, dst) in enumerate(zip(srcs, dsts)):
        sliced = src.shape == dst.shape
        local.append(pltpu.make_async_copy(src.at[me] if sliced else src, dst.at[me], local_sems.at[w]))
        for k, peer in enumerate(peers):
            sems = dict(send_sem=send_sems.at[w, k], recv_sem=recv_sems.at[w, k], device_id=peer, device_id_type=MESH)
            sends.append(pltpu.make_async_remote_copy(
                src_ref=src.at[_slot(*peer)] if sliced else src, dst_ref=dst.at[me], **sems))
            if with_arrivals:
                arrivals.append(pltpu.make_async_remote_copy(
                    src_ref=src.at[me] if sliced else src, dst_ref=dst.at[_slot(*peer)], **sems))
    return local, sends, arrivals


def _gather_phases(srcs, dsts, send_sems, recv_sems, local_sems):
    x, y, c = lax.axis_index("x"), lax.axis_index("y"), lax.axis_index("c")
    me, sibling = (x, y, c), (x, y, 1 - c)
    chips = [(1 - x, y), (x, 1 - y), (1 - x, 1 - y)]
    n = len(srcs)

    def copy(w, k, block, to, src=None):
        dst = dsts[w].at[_slot(*block)]
        return pltpu.make_async_remote_copy(
            src_ref=dst if src is None else src, dst_ref=dst, send_sem=send_sems.at[w, k],
            recv_sem=recv_sems.at[w, k], device_id=to, device_id_type=MESH)

    def first_sends(w):
        return [copy(w, 0, me, sibling, src=srcs[w])] + [copy(w, 1 + j, me, (*chip, c), src=srcs[w])
                                                         for j, chip in enumerate(chips)]

    def start():
        for w in range(n):
            pltpu.make_async_copy(srcs[w], dsts[w].at[_slot(*me)], local_sems.at[w]).start()
            for cp in first_sends(w):
                cp.start()

    def forward():
        for w in range(n):
            for j, chip in enumerate(chips):
                copy(w, 1 + j, (*chip, c), me).wait_recv()
                copy(w, 4 + j, (*chip, c), sibling).start()

    def finish():
        for w in range(n):
            copy(w, 0, sibling, me).wait_recv()
            for j, chip in enumerate(chips):
                copy(w, 4 + j, (*chip, 1 - c), me).wait_recv()
        for w in range(n):
            for cp in first_sends(w) + [copy(w, 4 + j, (*chip, c), sibling) for j, chip in enumerate(chips)]:
                cp.wait_send()
            pltpu.make_async_copy(srcs[w], dsts[w].at[_slot(*me)], local_sems.at[w]).wait()

    return start, forward, finish


def _call(body, *, name, grid, in_specs, out_specs, out_shape, sem, args, scratch_shapes=(), carry=None):
    if carry is None:
        return pl.pallas_call(body, name=name, grid=grid, in_specs=in_specs, out_specs=out_specs, out_shape=out_shape,
                              scratch_shapes=list(scratch_shapes), compiler_params=_params(sem))(*args)
    n_in, n_out, n_sc, n_c = len(in_specs), len(out_specs), len(scratch_shapes), len(carry)
    gather = all(a.ndim == 2 for a in carry)
    assert gather or all(a.ndim == 3 and a.shape[0] == N_DEV for a in carry)
    recv_shape = [jax.ShapeDtypeStruct((N_DEV,) + a.shape if gather else a.shape, a.dtype) for a in carry]
    n_steps = math.prod(grid)

    def wrapped(*refs):
        ins, c_in = refs[:n_in], refs[n_in:n_in + n_c]
        outs, c_out = refs[n_in + n_c:n_in + n_c + n_out], refs[n_in + n_c + n_out:n_in + 2 * n_c + n_out]
        scr = refs[n_in + 2 * n_c + n_out:n_in + 2 * n_c + n_out + n_sc]
        sems = refs[-3:]
        step = functools.reduce(lambda acc, d: acc * grid[d] + pl.program_id(d), range(len(grid)), 0)
        if gather:
            start, forward, finish = _gather_phases(c_in, c_out, *sems)
        else:
            def start():
                local, sends, _ = _direct_copies(c_in, c_out, *sems, with_arrivals=False)
                for cp in local + sends:
                    cp.start()

            def finish():
                local, sends, arrivals = _direct_copies(c_in, c_out, *sems)
                for cp in arrivals:
                    cp.wait_recv()
                for cp in sends:
                    cp.wait_send()
                for cp in local:
                    cp.wait()

        pl.when(step == 0)(start)
        if gather:
            pl.when(step == min((3 * n_steps) // 4, n_steps - 1))(forward)
        body(*ins, *outs, *scr)
        pl.when(step == n_steps - 1)(finish)

    any_spec = pl.BlockSpec(memory_space=pl.ANY)
    res = pl.pallas_call(
        wrapped, name=name, grid=grid, in_specs=list(in_specs) + [any_spec] * n_c,
        out_specs=list(out_specs) + [any_spec] * n_c, out_shape=list(out_shape) + recv_shape,
        scratch_shapes=list(scratch_shapes) + [pltpu.SemaphoreType.DMA((n_c, N_DEV - 1)),
                                               pltpu.SemaphoreType.DMA((n_c, N_DEV - 1)), pltpu.SemaphoreType.DMA((n_c,))],
        compiler_params=_params(("arbitrary",) * len(grid)),
    )(*args, *carry)
    return res


def _matmul(a, b, *, mode, out_dtype, name, add=None, carry=None):
    if mode == "nn":
        (M, K), N = a.shape, b.shape[1]
    elif mode == "nt":
        (M, K), N = a.shape, b.shape[0]
    else:
        (K, M), N = a.shape, b.shape[1]
    assert a.dtype == BF16 and b.dtype == BF16
    has_add = add is not None
    tm, tn = (_tile(M, 512), _tile(N, 512)) if mode == "tn" else (_tile(M, 1024), _tile(N, 512))
    fixed = 2 * tm * tn * (jnp.dtype(out_dtype).itemsize + (4 if has_add else 0)) + tm * tn * 4
    per_k = 2 * 2 * (tm + tn) + (2 * tm if mode == "tn" else 0)
    tk = _tile(K, max(LANES, (MATMUL_VMEM - fixed) // per_k))
    nk = K // tk
    dims = _NT if mode == "nt" else _NN
    if mode == "tn":
        a_spec = pl.BlockSpec((tk, tm), lambda i, j, k: (jnp.where(j == 0, k, 0), i))
    else:
        a_spec = pl.BlockSpec((tm, tk), lambda i, j, k: (i, k))
    b_spec = pl.BlockSpec((tn, tk), lambda i, j, k: (j, k)) if mode == "nt" else pl.BlockSpec((tk, tn), lambda i, j, k: (k, j))
    o_spec = pl.BlockSpec((tm, tn), lambda i, j, k: (i, j))

    def body(*refs):
        a_ref, b_ref = refs[:2]
        add_ref = refs[2] if has_add else None
        o_ref = refs[2 + has_add]
        scratch = list(refs[3 + has_add:])
        at = scratch.pop(0) if mode == "tn" else None
        acc = scratch.pop(0) if nk > 1 else None
        j, k = pl.program_id(1), pl.program_id(2)
        if mode == "tn":
            @pl.when(j == 0)
            def _():
                at[k] = a_ref[...].T

            part = _dot(at[k], b_ref[...], dims)
        else:
            part = _dot(a_ref[...], b_ref[...], dims)

        def finish(r):
            if has_add:
                r = r + add_ref[...]
            o_ref[...] = r.astype(out_dtype)

        if nk == 1:
            finish(part)
        else:
            @pl.when(k == 0)
            def _():
                acc[...] = part

            @pl.when(jnp.logical_and(k > 0, k < nk - 1))
            def _():
                acc[...] += part

            @pl.when(k == nk - 1)
            def _():
                finish(acc[...] + part)

    scratch = ([pltpu.VMEM((nk, tm, tk), BF16)] if mode == "tn" else []) + ([pltpu.VMEM((tm, tn), F32)] if nk > 1 else [])
    res = _call(
        body, name=name, grid=(M // tm, N // tn, nk),
        in_specs=[a_spec, b_spec] + ([o_spec] if has_add else []),
        out_specs=[o_spec], out_shape=[jax.ShapeDtypeStruct((M, N), out_dtype)],
        scratch_shapes=scratch, sem=("parallel", "arbitrary", "arbitrary"),
        args=(a, b, add) if has_add else (a, b), carry=carry)
    return res[0] if carry is None else (res[0], res[1:])


def _rms_fwd(x, gain, name):
    T, D = x.shape
    tr = 256

    def body(x_ref, g_ref, h_ref):
        xv = x_ref[...]
        rstd = lax.rsqrt(jnp.mean(xv * xv, axis=1, keepdims=True) + EPS)
        h_ref[...] = (xv * rstd * g_ref[...]).astype(BF16)

    return pl.pallas_call(
        body, name=name, grid=(T // tr,),
        in_specs=[pl.BlockSpec((tr, D), lambda i: (i, 0)), pl.BlockSpec((1, D), lambda i: (0, 0))],
        out_specs=pl.BlockSpec((tr, D), lambda i: (i, 0)),
        out_shape=jax.ShapeDtypeStruct((T, D), BF16),
        compiler_params=_params(("parallel",)),
    )(x, gain)


def _rms_bwd(x, gain, dh, dres, name):
    T, D = x.shape
    tr = 256

    def body(x_ref, g_ref, dh_ref, dr_ref, dx_ref, dxb_ref, dg_ref):
        @pl.when(pl.program_id(0) == 0)
        def _():
            dg_ref[...] = jnp.zeros_like(dg_ref)

        xv = x_ref[...]
        rstd = lax.rsqrt(jnp.mean(xv * xv, axis=1, keepdims=True) + EPS)
        xhat = xv * rstd
        dy = dh_ref[...].astype(F32)
        dg_ref[...] += jnp.sum(dy * xhat, axis=0, keepdims=True)
        dxh = dy * g_ref[...]
        dx = dr_ref[...] + rstd * (dxh - xhat * jnp.mean(dxh * xhat, axis=1, keepdims=True))
        dx_ref[...] = dx
        dxb_ref[...] = dx.astype(BF16)

    row = pl.BlockSpec((tr, D), lambda i: (i, 0))
    vec = pl.BlockSpec((1, D), lambda i: (0, 0))
    return pl.pallas_call(
        body, name=name, grid=(T // tr,), in_specs=[row, vec, row, row], out_specs=[row, row, vec],
        out_shape=[jax.ShapeDtypeStruct((T, D), F32), jax.ShapeDtypeStruct((T, D), BF16),
                   jax.ShapeDtypeStruct((1, D), F32)],
        compiler_params=_params(("arbitrary",)),
    )(x, gain, dh, dres)


def _loss_head(y, target):
    T, D = y.shape
    tr = 256

    def body(y_ref, t_ref, s_ref, d_ref, db_ref):
        @pl.when(pl.program_id(0) == 0)
        def _():
            s_ref[...] = jnp.zeros_like(s_ref)

        e = y_ref[...] - t_ref[...]
        s_ref[...] += jnp.sum(jnp.sum(e * e, axis=1, keepdims=True), axis=0, keepdims=True)
        d = e * (1.0 / D)
        d_ref[...] = d
        db_ref[...] = d.astype(BF16)

    row = pl.BlockSpec((tr, D), lambda i: (i, 0))
    return pl.pallas_call(
        body, name="loss_head", grid=(T // tr,), in_specs=[row, row],
        out_specs=[pl.BlockSpec((1, 1), lambda i: (0, 0)), row, row],
        out_shape=[jax.ShapeDtypeStruct((1, 1), F32), jax.ShapeDtypeStruct((T, D), F32),
                   jax.ShapeDtypeStruct((T, D), BF16)],
        compiler_params=_params(("arbitrary",)),
    )(y, target)


def _ffn_up(h, wg, wu, name, carry=None):
    (T, D), Fd = h.shape, wg.shape[1]
    tm, tn = _tile(T, 1024), _tile(Fd, 512)

    def body(h_ref, wg_ref, wu_ref, g_ref, u_ref, a_ref):
        hv = h_ref[...]
        g = _dot(hv, wg_ref[...])
        g_ref[...] = g.astype(BF16)
        sg = g * jax.nn.sigmoid(g)
        u = _dot(hv, wu_ref[...])
        u_ref[...] = u.astype(BF16)
        a_ref[...] = (sg * u).astype(BF16)

    w_spec = pl.BlockSpec((D, tn), lambda i, j: (0, j))
    o_spec = pl.BlockSpec((tm, tn), lambda i, j: (i, j))
    res = _call(
        body, name=name, grid=(T // tm, Fd // tn),
        in_specs=[pl.BlockSpec((tm, D), lambda i, j: (i, 0)), w_spec, w_spec],
        out_specs=[o_spec, o_spec, o_spec],
        out_shape=[jax.ShapeDtypeStruct((T, Fd), BF16)] * 3,
        sem=("parallel", "arbitrary"), args=(h, wg, wu), carry=carry)
    return res if carry is None else (*res[:3], res[3:])


def _ffn_dact(dy, wd, gate, up, name):
    (T, D), Fd = dy.shape, wd.shape[0]
    tm, tn = _tile(T, 1024), _tile(Fd, 512)

    def body(dy_ref, wd_ref, g_ref, u_ref, dg_ref, du_ref):
        da = _dot(dy_ref[...], wd_ref[...], _NT)
        g = g_ref[...].astype(F32)
        sg = jax.nn.sigmoid(g)
        du_ref[...] = (da * g * sg).astype(BF16)
        dg_ref[...] = (da * u_ref[...].astype(F32) * sg * (1.0 + g * (1.0 - sg))).astype(BF16)

    o_spec = pl.BlockSpec((tm, tn), lambda i, j: (i, j))
    return pl.pallas_call(
        body, name=name, grid=(T // tm, Fd // tn),
        in_specs=[pl.BlockSpec((tm, D), lambda i, j: (i, 0)), pl.BlockSpec((tn, D), lambda i, j: (j, 0)),
                  o_spec, o_spec],
        out_specs=[o_spec, o_spec],
        out_shape=[jax.ShapeDtypeStruct((T, Fd), BF16), jax.ShapeDtypeStruct((T, Fd), BF16)],
        compiler_params=_params(("parallel", "arbitrary")),
    )(dy, wd, gate, up)


def _is_a(shape):
    return lax.broadcasted_iota(jnp.int32, shape, 1) % LANES < HEAD


def _split2(v):
    hi = v.astype(BF16)
    return hi, (v - hi.astype(F32)).astype(BF16)


def _split3(v):
    hi = v.astype(BF16)
    r = v - hi.astype(F32)
    mid = r.astype(BF16)
    return hi, mid, (r - mid.astype(F32)).astype(BF16)


def _dot_split(v, m, pieces, dims=_NN):
    parts = _split3(v) if pieces == 3 else _split2(v)
    out = _dot(parts[0], m, dims)
    for p in parts[1:]:
        out = out + _dot(p, m, dims)
    return out


def _head_blockdiag():
    r = lax.broadcasted_iota(jnp.int32, (LANES, LANES), 0) // HEAD
    c = lax.broadcasted_iota(jnp.int32, (LANES, LANES), 1) // HEAD
    return (r == c).astype(BF16)


def _swap_half(v):
    lane = lax.broadcasted_iota(jnp.int32, v.shape, 1)
    return jnp.where(lane % HEAD < HEAD // 2, pltpu.roll(v, LANES - HEAD // 2, axis=1), pltpu.roll(v, HEAD // 2, axis=1))


def _attn_prep_fwd(qkv, cos, sin, gq, gk, D, HKV):
    T, QW = qkv.shape
    tr = 256
    nq, nk = D // LANES, HKV // 2
    KW = HKV * LANES

    def body(x_ref, cos_ref, sin_ref, gq_ref, gk_ref, q_ref, k_ref, v_ref):
        bd = _head_blockdiag()
        cs, sn = cos_ref[...], sin_ref[...]
        isa = _is_a((tr, LANES))

        def normrope(xv, g):
            ms = _dot_split(xv * xv, bd, 2) * (1.0 / HEAD)
            xn = xv * lax.rsqrt(ms + EPS) * g
            return xn * cs + _swap_half(xn) * sn

        def dup(v):
            r = pltpu.roll(v, HEAD, axis=1)
            return jnp.where(isa, v, r), jnp.where(isa, r, v)

        for s in range(nq):
            sl = slice(s * LANES, (s + 1) * LANES)
            q_ref[:, sl] = normrope(x_ref[:, sl], gq_ref[...]).astype(BF16)
        for s in range(nk):
            ka, kb = dup(normrope(x_ref[:, D + s * LANES:D + (s + 1) * LANES], gk_ref[...]))
            k_ref[:, 2 * s * LANES:(2 * s + 1) * LANES] = ka.astype(BF16)
            k_ref[:, (2 * s + 1) * LANES:(2 * s + 2) * LANES] = kb.astype(BF16)
            va, vb = dup(x_ref[:, D + (nk + s) * LANES:D + (nk + s + 1) * LANES])
            v_ref[:, 2 * s * LANES:(2 * s + 1) * LANES] = va.astype(BF16)
            v_ref[:, (2 * s + 1) * LANES:(2 * s + 2) * LANES] = vb.astype(BF16)

    tab = pl.BlockSpec((tr, LANES), lambda i: (i, 0))
    vec = pl.BlockSpec((1, LANES), lambda i: (0, 0))
    return pl.pallas_call(
        body, name="attn_prep_fwd", grid=(T // tr,),
        in_specs=[pl.BlockSpec((tr, QW), lambda i: (i, 0)), tab, tab, vec, vec],
        out_specs=[pl.BlockSpec((tr, D), lambda i: (i, 0)), pl.BlockSpec((tr, KW), lambda i: (i, 0)),
                   pl.BlockSpec((tr, KW), lambda i: (i, 0))],
        out_shape=[jax.ShapeDtypeStruct((T, D), BF16), jax.ShapeDtypeStruct((T, KW), BF16),
                   jax.ShapeDtypeStruct((T, KW), BF16)],
        compiler_params=_params(("parallel",)),
    )(qkv, cos, sin, gq, gk)


def _attn_prep_bwd(qkv, dq, dkd, dvd, cos, sin, gq, gk, D, HKV):
    T, QW = qkv.shape
    tr = 256
    nq, nk = D // LANES, HKV // 2
    KW = HKV * LANES

    def body(x_ref, dq_ref, dk_ref, dv_ref, cos_ref, sin_ref, gq_ref, gk_ref, o_ref, dgq_ref, dgk_ref):
        @pl.when(pl.program_id(0) == 0)
        def _():
            dgq_ref[...] = jnp.zeros_like(dgq_ref)
            dgk_ref[...] = jnp.zeros_like(dgk_ref)

        bd = _head_blockdiag()
        cs, sn = cos_ref[...], sin_ref[...]
        isa = _is_a((tr, LANES))

        def back(xv, dy, g):
            rstd = lax.rsqrt(_dot_split(xv * xv, bd, 2) * (1.0 / HEAD) + EPS)
            xhat = xv * rstd
            dxn = dy * cs + _swap_half(dy * sn)
            dxh = dxn * g
            mean = _dot_split(dxh * xhat, bd, 2) * (1.0 / HEAD)
            return rstd * (dxh - xhat * mean), jnp.sum(dxn * xhat, axis=0, keepdims=True)

        def fold(s):
            a = dk_ref[:, 2 * s * LANES:(2 * s + 1) * LANES]
            b = dk_ref[:, (2 * s + 1) * LANES:(2 * s + 2) * LANES]
            return jnp.where(isa, a + pltpu.roll(a, HEAD, axis=1), b + pltpu.roll(b, HEAD, axis=1))

        def foldv(s):
            a = dv_ref[:, 2 * s * LANES:(2 * s + 1) * LANES]
            b = dv_ref[:, (2 * s + 1) * LANES:(2 * s + 2) * LANES]
            return jnp.where(isa, a + pltpu.roll(a, HEAD, axis=1), b + pltpu.roll(b, HEAD, axis=1))

        dgq = jnp.zeros((1, LANES), F32)
        for s in range(nq):
            sl = slice(s * LANES, (s + 1) * LANES)
            dx, dg = back(x_ref[:, sl], dq_ref[:, sl], gq_ref[...])
            o_ref[:, sl] = dx.astype(BF16)
            dgq = dgq + dg
        dgq_ref[...] += dgq
        dgk = jnp.zeros((1, LANES), F32)
        for s in range(nk):
            sl = slice(D + s * LANES, D + (s + 1) * LANES)
            dx, dg = back(x_ref[:, sl], fold(s), gk_ref[...])
            o_ref[:, sl] = dx.astype(BF16)
            dgk = dgk + dg
            o_ref[:, D + (nk + s) * LANES:D + (nk + s + 1) * LANES] = foldv(s).astype(BF16)
        dgk_ref[...] += dgk

    tab = pl.BlockSpec((tr, LANES), lambda i: (i, 0))
    vec = pl.BlockSpec((1, LANES), lambda i: (0, 0))
    kv = pl.BlockSpec((tr, KW), lambda i: (i, 0))
    return pl.pallas_call(
        body, name="attn_prep_bwd", grid=(T // tr,),
        in_specs=[pl.BlockSpec((tr, QW), lambda i: (i, 0)), pl.BlockSpec((tr, D), lambda i: (i, 0)), kv, kv,
                  tab, tab, vec, vec],
        out_specs=[pl.BlockSpec((tr, QW), lambda i: (i, 0)), vec, vec],
        out_shape=[jax.ShapeDtypeStruct((T, QW), BF16), jax.ShapeDtypeStruct((1, LANES), F32),
                   jax.ShapeDtypeStruct((1, LANES), F32)],
        compiler_params=_params(("arbitrary",)),
    )(qkv, dq, dkd, dvd, cos, sin, gq, gk)


def _attn_probs(qs, kw, sink_ref, first, scale):
    rows = qs.shape[0]
    qi = lax.broadcasted_iota(jnp.int32, (rows, 2 * WINDOW), 0) % WINDOW
    kj = lax.broadcasted_iota(jnp.int32, (rows, 2 * WINDOW), 1)
    valid = (kj > qi) & (kj <= qi + WINDOW)
    if first is not False:
        valid = valid & jnp.logical_or(jnp.logical_not(first), kj >= WINDOW)
    isa = _is_a(kw.shape)
    out = []
    for pos in (0, 1):
        kp = jnp.where(isa if pos == 0 else ~isa, kw, jnp.zeros_like(kw))
        s = jnp.where(valid, _dot(qs, kp, _NT) * scale, -jnp.inf)
        sink = sink_ref[0, pos]
        m = jnp.maximum(jnp.max(s, axis=1, keepdims=True), sink)
        p = jnp.exp(s - m)
        ps = jnp.exp(sink - m)
        inv = 1.0 / (jnp.sum(p, axis=1, keepdims=True) + ps)
        out.append((p * inv, ps * inv, kp))
    return out


def _attn_specs(qb):
    q = pl.BlockSpec((qb * WINDOW, ATT_GW), lambda g, n: (n, g))
    cur = pl.BlockSpec((qb * WINDOW, LANES), lambda g, n: (n, g))
    prev = pl.BlockSpec((WINDOW, LANES), lambda g, n: (jnp.maximum(qb * n - 1, 0), g))
    sink = pl.BlockSpec((1, 2, ATT_GW, 1), lambda g, n: (g, 0, 0, 0))
    return q, cur, prev, sink


def _stack(ref, s):
    rows = slice(s * WINDOW, (s + 1) * WINDOW)
    return jnp.concatenate([ref[rows, i * LANES:(i + 1) * LANES] for i in range(ATT_GW // LANES)], axis=0)


def _attn_fwd(q, kd, vd, sinkcol, HKV, carry=None):
    T, D = q.shape
    nb = T // WINDOW
    qb = math.gcd(nb, ATT_STEP_BLOCKS)
    scale = HEAD ** -0.5

    def body(q_ref, kp_ref, kc_ref, vp_ref, vc_ref, sink_ref, o_ref):
        n = pl.program_id(1)
        kall = jnp.concatenate([kp_ref[...], kc_ref[...]], axis=0)
        vall = jnp.concatenate([vp_ref[...], vc_ref[...]], axis=0)
        isa = _is_a((2 * WINDOW, LANES))
        for s in range(qb):
            win = slice(s * WINDOW, (s + 2) * WINDOW)
            kw, vw = kall[win], vall[win]
            o = jnp.zeros((ATT_GW, LANES), F32)
            first = (n == 0) if s == 0 else False
            for pos, (probs, _, _) in enumerate(_attn_probs(_stack(q_ref, s), kw, sink_ref, first, scale)):
                vp = jnp.where(isa if pos == 0 else ~isa, vw, jnp.zeros_like(vw))
                o = o + _dot(probs.astype(BF16), vp)
            for i in range(ATT_GW // LANES):
                o_ref[s * WINDOW:(s + 1) * WINDOW, i * LANES:(i + 1) * LANES] = o[i * WINDOW:(i + 1) * WINDOW].astype(BF16)

    qs_, cur, prev, sink = _attn_specs(qb)
    res = _call(
        body, name="attn_fwd", grid=(HKV, nb // qb), in_specs=[qs_, prev, cur, prev, cur, sink], out_specs=[qs_],
        out_shape=[jax.ShapeDtypeStruct((T, D), BF16)], sem=("parallel", "parallel"),
        args=(q, kd, kd, vd, vd, sinkcol), carry=carry)
    return res[0] if carry is None else (res[0], res[1:])


def _attn_bwd(q, kd, vd, o, do, sinkcol, HKV, carry=None):
    T, D = q.shape
    nb = T // WINDOW
    qb = math.gcd(nb, ATT_STEP_BLOCKS)
    scale = HEAD ** -0.5
    KW = HKV * LANES

    def body(q_ref, kp_ref, kc_ref, vp_ref, vc_ref, o_ref, do_ref, sink_ref, dq_ref, dk_ref, dv_ref, ds_ref):
        n = pl.program_id(1)

        @pl.when(n == 0)
        def _():
            dk_ref[...] = jnp.zeros_like(dk_ref)
            dv_ref[...] = jnp.zeros_like(dv_ref)
            ds_ref[...] = jnp.zeros_like(ds_ref)

        kall = jnp.concatenate([kp_ref[...], kc_ref[...]], axis=0)
        vall = jnp.concatenate([vp_ref[...], vc_ref[...]], axis=0)
        isa_w = _is_a((2 * WINDOW, LANES))
        isa_q = _is_a((ATT_GW, LANES))
        for s in range(qb):
            win = slice(s * WINDOW, (s + 2) * WINDOW)
            kw, vw = kall[win], vall[win]
            qs = _stack(q_ref, s)
            dos = _stack(do_ref, s)
            dd = dos * _stack(o_ref, s).astype(F32)
            dob = dos.astype(BF16)
            dqs = jnp.zeros((ATT_GW, LANES), F32)
            dkw, dvw, dsk = [], [], []
            first = (n == 0) if s == 0 else False
            for pos, (probs, psink, kp) in enumerate(_attn_probs(qs, kw, sink_ref, first, scale)):
                sel_w = isa_w if pos == 0 else ~isa_w
                sel_q = isa_q if pos == 0 else ~isa_q
                delta = jnp.sum(jnp.where(sel_q, dd, 0.0), axis=1, keepdims=True)
                vp = jnp.where(sel_w, vw, jnp.zeros_like(vw))
                dp = _dot(dob, vp, _NT)
                dsb = (probs * (dp - delta) * scale).astype(BF16)
                dqs = dqs + _dot(dsb, kp)
                dkw.append(_dot(dsb, qs, _TN))
                dvw.append(_dot(probs.astype(BF16), dob, _TN))
                dsk.append(-psink * delta)
            dkw = jnp.where(isa_w, dkw[0], dkw[1])
            dvw = jnp.where(isa_w, dvw[0], dvw[1])

            def add_window(dkw=dkw, dvw=dvw, s=s):
                start = pl.multiple_of((qb * n + s - 1) * WINDOW, WINDOW)
                dk_ref[pl.ds(start, 2 * WINDOW), :] += dkw
                dv_ref[pl.ds(start, 2 * WINDOW), :] += dvw

            if s == 0:
                @pl.when(n == 0)
                def _(dkw=dkw, dvw=dvw):
                    dk_ref[0:WINDOW, :] += dkw[WINDOW:]
                    dv_ref[0:WINDOW, :] += dvw[WINDOW:]

                pl.when(n > 0)(add_window)
            else:
                add_window()

            rows = []
            for i in range(ATT_GW // LANES):
                dq_ref[s * WINDOW:(s + 1) * WINDOW, i * LANES:(i + 1) * LANES] = dqs[i * WINDOW:(i + 1) * WINDOW]
                for pos in (0, 1):
                    t = jnp.sum(dsk[pos][i * WINDOW:(i + 1) * WINDOW], axis=0, keepdims=True)
                    rows.append(jnp.broadcast_to(t, (1, LANES)))
            ds_ref[0] += jnp.concatenate(rows, axis=0)

    qs_, cur, prev, sink = _attn_specs(qb)
    dqo = pl.BlockSpec((qb * WINDOW, ATT_GW), lambda g, n: (n, g))
    dkv = pl.BlockSpec((T, LANES), lambda g, n: (0, g))
    res = _call(
        body, name="attn_bwd", grid=(HKV, nb // qb),
        in_specs=[qs_, prev, cur, prev, cur, qs_, dqo, sink],
        out_specs=[dqo, dkv, dkv, pl.BlockSpec((1, ATT_GROUP, LANES), lambda g, n: (g, 0, 0))],
        out_shape=[jax.ShapeDtypeStruct((T, D), F32), jax.ShapeDtypeStruct((T, KW), F32),
                   jax.ShapeDtypeStruct((T, KW), F32), jax.ShapeDtypeStruct((HKV, ATT_GROUP, LANES), F32)],
        sem=("parallel", "arbitrary"), args=(q, kd, kd, vd, vd, o, do, sinkcol), carry=carry)
    return res if carry is None else (*res[:4], res[4:])


def _conv_fwd(zx, w, b, DI, CD):
    T = zx.shape[0]
    cw, tc = _tile(math.gcd(DI, CD), 512), 512
    off = DI // cw

    def body(cur_ref, halo_ref, w_ref, b_ref, o_ref):
        i = pl.program_id(1)
        halo = jnp.where(i > 0, halo_ref[...], 0.0)
        ext = jnp.concatenate([halo, cur_ref[...]], axis=0)
        acc = b_ref[...] + w_ref[SSM_CONV - 1:SSM_CONV, :] * ext[HALO:]
        for k in range(SSM_CONV - 1):
            acc = acc + w_ref[k:k + 1, :] * pltpu.roll(ext, SSM_CONV - 1 - k, axis=0)[HALO:]
        o_ref[...] = acc * jax.nn.sigmoid(acc)

    return pl.pallas_call(
        body, name="ssm_conv_fwd", grid=(CD // cw, T // tc),
        in_specs=[pl.BlockSpec((tc, cw), lambda j, i: (i, off + j)),
                  pl.BlockSpec((HALO, cw), lambda j, i: (jnp.maximum(i * (tc // HALO) - 1, 0), off + j)),
                  pl.BlockSpec((SSM_CONV, cw), lambda j, i: (0, j)), pl.BlockSpec((1, cw), lambda j, i: (0, j))],
        out_specs=pl.BlockSpec((tc, cw), lambda j, i: (i, j)),
        out_shape=jax.ShapeDtypeStruct((T, CD), F32),
        compiler_params=_params(("parallel", "parallel")),
    )(zx, zx, w, b)


def _conv_bwd(zx, dxc, w, b, DI, CD):
    T = zx.shape[0]
    cw, tc = _tile(math.gcd(DI, CD), 512), 512
    off = DI // cw
    nt = T // tc
    hb = tc // HALO

    def body(cur_ref, prev_ref, next_ref, d_ref, dnext_ref, w_ref, b_ref, o_ref, dw_ref, db_ref):
        i = pl.program_id(1)

        @pl.when(i == 0)
        def _():
            dw_ref[...] = jnp.zeros_like(dw_ref)
            db_ref[...] = jnp.zeros_like(db_ref)

        prev = jnp.where(i > 0, prev_ref[...], 0.0)
        ext = jnp.concatenate([prev, cur_ref[...], next_ref[...]], axis=0)
        u = b_ref[...] + w_ref[SSM_CONV - 1:SSM_CONV, :] * ext
        for k in range(SSM_CONV - 1):
            u = u + w_ref[k:k + 1, :] * pltpu.roll(ext, SSM_CONV - 1 - k, axis=0)
        u = u[HALO:]
        dnext = jnp.where(i < nt - 1, dnext_ref[...], 0.0)
        dxe = jnp.concatenate([d_ref[...], dnext], axis=0)
        sg = jax.nn.sigmoid(u)
        du = dxe * sg * (1.0 + u * (1.0 - sg))
        n_e = tc + HALO
        dx = w_ref[SSM_CONV - 1:SSM_CONV, :] * du
        for k in range(SSM_CONV - 1):
            dx = dx + w_ref[k:k + 1, :] * pltpu.roll(du, n_e - (SSM_CONV - 1 - k), axis=0)
        o_ref[...] = dx[:tc].astype(BF16)
        duc = du[:tc]
        db_ref[...] += jnp.sum(duc, axis=0, keepdims=True)
        xs = ext[:n_e]
        dws = []
        for k in range(SSM_CONV):
            sh = xs if k == SSM_CONV - 1 else pltpu.roll(xs, SSM_CONV - 1 - k, axis=0)
            dws.append(jnp.sum(duc * sh[HALO:], axis=0, keepdims=True))
        dw_ref[...] += jnp.concatenate(dws, axis=0)

    return pl.pallas_call(
        body, name="ssm_conv_bwd", grid=(CD // cw, nt),
        in_specs=[pl.BlockSpec((tc, cw), lambda j, i: (i, off + j)),
                  pl.BlockSpec((HALO, cw), lambda j, i: (jnp.maximum(i * hb - 1, 0), off + j)),
                  pl.BlockSpec((HALO, cw), lambda j, i: (jnp.minimum((i + 1) * hb, nt * hb - 1), off + j)),
                  pl.BlockSpec((tc, cw), lambda j, i: (i, j)),
                  pl.BlockSpec((HALO, cw), lambda j, i: (jnp.minimum((i + 1) * hb, nt * hb - 1), j)),
                  pl.BlockSpec((SSM_CONV, cw), lambda j, i: (0, j)), pl.BlockSpec((1, cw), lambda j, i: (0, j))],
        out_specs=[pl.BlockSpec((tc, cw), lambda j, i: (i, j)), pl.BlockSpec((SSM_CONV, cw), lambda j, i: (0, j)),
                   pl.BlockSpec((1, cw), lambda j, i: (0, j))],
        out_shape=[jax.ShapeDtypeStruct((T, CD), BF16), jax.ShapeDtypeStruct((SSM_CONV, CD), F32),
                   jax.ShapeDtypeStruct((1, CD), F32)],
        compiler_params=_params(("parallel", "arbitrary")),
    )(zx, zx, zx, dxc, dxc, w, b)


def _tri_dot(v, upper):
    L = v.shape[0]
    r = lax.broadcasted_iota(jnp.int32, (L, L), 0)
    c = lax.broadcasted_iota(jnp.int32, (L, L), 1)
    tri = ((r <= c) if upper else (r >= c)).astype(BF16)
    p = _split3(v)
    return _dot(tri, p[0]) + _dot(tri, p[1]) + _dot(tri, p[2])


def _ssd_time2(dtraw_ref, bias_ref, alog_ref, sel):
    dt = jax.nn.softplus(dtraw_ref[...] + bias_ref[...])
    acum = _tri_dot(dt * (-jnp.exp(alog_ref[...])), False)
    return dt, _dot_split(dt, sel, 3), _dot_split(acum, sel, 3)


def _decay(acs, acs_t, pos):
    L = acs.shape[0]
    r = lax.broadcasted_iota(jnp.int32, (L, L), 0)
    c = lax.broadcasted_iota(jnp.int32, (L, L), 1)
    col = acs[:, HEAD * pos:HEAD * pos + 1]
    row = acs_t[HEAD * pos:HEAD * pos + 1, :]
    return jnp.exp(jnp.where(r >= c, col - row, -jnp.inf))


def _ssd_specs(G, GW, DI, ZW):
    L = SSM_CHUNK
    grp = lambda f: pl.BlockSpec((L, GW), lambda g, c: (f(c), g))
    return dict(
        grp=grp,
        bmat=lambda f: pl.BlockSpec((L, SSM_STATE), lambda g, c: (f(c), DI // SSM_STATE + g)),
        cmat=lambda f: pl.BlockSpec((L, SSM_STATE), lambda g, c: (f(c), DI // SSM_STATE + G + g)),
        dtraw=lambda f: pl.BlockSpec((L, LANES), lambda g, c: (f(c), (2 * DI + 2 * G * SSM_STATE) // LANES)),
        vec=pl.BlockSpec((1, LANES), lambda g, c: (0, 0)),
        gvec=pl.BlockSpec((1, GW), lambda g, c: (0, g)),
        sel=pl.BlockSpec((1, LANES, GW), lambda g, c: (g, 0, 0)),
        selt=pl.BlockSpec((1, GW, LANES), lambda g, c: (g, 0, 0)),
    )


def _ssd_fwd(zx, xc, bias, alog, sel, dskip, ng, DI, carry=None):
    T, ZW = zx.shape
    G, L = SSM_GROUPS, SSM_CHUNK
    GW = DI // G
    NS = GW // LANES
    nc = T // L
    sp = _ssd_specs(G, GW, DI, ZW)
    ident = lambda c: c

    def body(x_ref, b_ref, c_ref, z_ref, dtraw_ref, bias_ref, alog_ref, sel_ref, d_ref, ng_ref,
             y_ref, yo_ref, st_ref, state):
        c = pl.program_id(1)

        @pl.when(c == 0)
        def _():
            state[...] = jnp.zeros_like(state)

        x = x_ref[...]
        bb, cb_ = b_ref[...].astype(BF16), c_ref[...].astype(BF16)
        cbm = _dot(cb_, bb, _NT)
        _, dtx, acx = _ssd_time2(dtraw_ref, bias_ref, alog_ref, sel_ref[0])
        xdt = x * dtx
        ex = jnp.exp(acx)
        last = acx[L - 1:L, :]
        te = jnp.exp(last - acx)
        dlast = jnp.exp(last)
        isa = _is_a((L, LANES))
        for i in range(NS):
            sl = slice(i * LANES, (i + 1) * LANES)
            acs = acx[:, sl]
            acs_t = acs.T
            xs = xdt[:, sl]
            y = jnp.zeros((L, LANES), F32)
            for pos in (0, 1):
                m = (cbm * _decay(acs, acs_t, pos)).astype(BF16)
                y = y + _dot(m, jnp.where(isa if pos == 0 else ~isa, xs, 0.0).astype(BF16))
            st = state[i]
            st_ref[0, i] = st
            y = y + _dot(cb_, st.astype(BF16)) * ex[:, sl]
            state[i] = st * dlast[:, sl] + _dot(bb, (xs * te[:, sl]).astype(BF16), _TN)
            y_ref[:, sl] = y + d_ref[:, sl] * x[:, sl]
        z = z_ref[...]
        gated = y_ref[...] * (z * jax.nn.sigmoid(z))
        rstd = lax.rsqrt(jnp.mean(gated * gated, axis=1, keepdims=True) + EPS)
        yo_ref[...] = (gated * rstd * ng_ref[...]).astype(BF16)

    res = _call(
        body, name="ssd_fwd", grid=(G, nc),
        in_specs=[sp["grp"](ident), sp["bmat"](ident), sp["cmat"](ident), sp["grp"](ident), sp["dtraw"](ident),
                  sp["vec"], sp["vec"], sp["sel"], sp["gvec"], sp["gvec"]],
        out_specs=[sp["grp"](ident), sp["grp"](ident),
                   pl.BlockSpec((1, NS, SSM_STATE, LANES), lambda g, c: (c, g, 0, 0))],
        out_shape=[jax.ShapeDtypeStruct((T, DI), F32), jax.ShapeDtypeStruct((T, DI), BF16),
                   jax.ShapeDtypeStruct((nc, G * NS, SSM_STATE, LANES), F32)],
        scratch_shapes=[pltpu.VMEM((NS, SSM_STATE, LANES), F32)],
        sem=("parallel", "arbitrary"), args=(xc, xc, xc, zx, zx, bias, alog, sel, dskip, ng), carry=carry)
    return res if carry is None else (*res[:3], res[3:])


def _ssd_bwd(zx, xc, yssd, dyo, states, bias, alog, sel, selt, hsel, dskip, ng, DI, carry=None):
    T, ZW = zx.shape
    G, L = SSM_GROUPS, SSM_CHUNK
    GW = DI // G
    NS = GW // LANES
    nc = T // L
    sp = _ssd_specs(G, GW, DI, ZW)
    rev = lambda c: nc - 1 - c

    def body(x_ref, b_ref, c_ref, z_ref, dtraw_ref, y_ref, dyo_ref, st_ref, bias_ref, alog_ref, sel_ref, selt_ref,
             hsel_ref, d_ref, ng_ref, dz_ref, dx_ref, db_ref, dc_ref, ddt_ref, dac_ref, dd_ref, dng_ref, dstate):
        c = pl.program_id(1)

        @pl.when(c == 0)
        def _():
            dstate[...] = jnp.zeros_like(dstate)
            dd_ref[...] = jnp.zeros_like(dd_ref)
            dng_ref[...] = jnp.zeros_like(dng_ref)

        z, ys, dyo = z_ref[...], y_ref[...], dyo_ref[...]
        sg = jax.nn.sigmoid(z)
        sz = z * sg
        gated = ys * sz
        rstd = lax.rsqrt(jnp.mean(gated * gated, axis=1, keepdims=True) + EPS)
        yn = gated * rstd
        dng_ref[0] += jnp.sum(dyo * yn, axis=0, keepdims=True)
        dyn = dyo * ng_ref[...]
        dgated = rstd * (dyn - yn * jnp.mean(dyn * yn, axis=1, keepdims=True))
        g = dgated * sz
        dz_ref[...] = (dgated * ys * sg * (1.0 + z * (1.0 - sg))).astype(BF16)

        x = x_ref[...]
        dsk = d_ref[...]
        dd_ref[0] += jnp.sum(g * x, axis=0, keepdims=True)
        bb, cb_ = b_ref[...].astype(BF16), c_ref[...].astype(BF16)
        cbm = _dot(cb_, bb, _NT)
        _, dtx, acx = _ssd_time2(dtraw_ref, bias_ref, alog_ref, sel_ref[0])
        xdt = x * dtx
        ex = jnp.exp(acx)
        last = acx[L - 1:L, :]
        te = jnp.exp(last - acx)
        dlast = jnp.exp(last)
        isa = _is_a((L, LANES))
        is_last = lax.broadcasted_iota(jnp.int32, (L, LANES), 0) == L - 1
        strict = lax.broadcasted_iota(jnp.int32, (L, L), 0) > lax.broadcasted_iota(jnp.int32, (L, L), 1)
        dcb = jnp.zeros((L, L), F32)
        dcm = jnp.zeros((L, SSM_STATE), F32)
        dbm = jnp.zeros((L, SSM_STATE), F32)
        for i in range(NS):
            sl = slice(i * LANES, (i + 1) * LANES)
            acs = acx[:, sl]
            acs_t = acs.T
            xs, gs = xdt[:, sl], g[:, sl]
            xsb = xs.astype(BF16)
            dxd = jnp.zeros((L, LANES), F32)
            dac_c = jnp.zeros((L, LANES), F32)
            for pos in (0, 1):
                gp = jnp.where(isa if pos == 0 else ~isa, gs, 0.0).astype(BF16)
                dec = _decay(acs, acs_t, pos)
                dxd = dxd + _dot((cbm * dec).astype(BF16), gp, _TN)
                dmd = _dot(gp, xsb, _NT) * dec
                dcb = dcb + dmd
                q = jnp.where(strict, dmd * cbm, 0.0)
                hot = jnp.broadcast_to(hsel_ref[0, 2 * i + pos:2 * i + pos + 1, :], (L, LANES)).astype(BF16)
                dac_c = dac_c + _dot_split(q, hot, 2) - _dot_split(q, hot, 2, _TN)
            st = st_ref[0, i]
            dst = dstate[i]
            stb, dstb = st.astype(BF16), dst.astype(BF16)
            eg = (ex[:, sl] * gs).astype(BF16)
            dcm = dcm + _dot(eg, stb, _NT)
            yoff = _dot(cb_, stb) * ex[:, sl]
            w = xs * te[:, sl]
            wb = w.astype(BF16)
            dw = _dot(bb, dstb)
            dbm = dbm + _dot(wb, dstb, _NT)
            dxt = dxd + dw * te[:, sl]
            dal = dlast[:, sl] * jnp.sum(dst * st, axis=0, keepdims=True) + jnp.sum(dw * w, axis=0, keepdims=True)
            dac_l = gs * yoff - w * dw + jnp.where(is_last, dal, 0.0)
            ddt_l = dxt * x[:, sl]
            dstate[i] = dst * dlast[:, sl] + _dot(cb_, eg, _TN)
            dx_ref[:, sl] = dxt * dtx[:, sl] + dsk[:, sl] * gs
            part = _dot_split(ddt_l, selt_ref[0, sl, :], 2)
            parta = dac_c + _dot_split(dac_l, selt_ref[0, sl, :], 2)
            if i == 0:
                ddt_ref[0] = part
                dac_ref[0] = parta
            else:
                ddt_ref[0] += part
                dac_ref[0] += parta
        dcbb = dcb.astype(BF16)
        dc_ref[...] = dcm + _dot(dcbb, bb)
        db_ref[...] = dbm + _dot(dcbb, cb_, _TN)

    part_spec = pl.BlockSpec((1, L, LANES), lambda g, c: (g, rev(c), 0))
    lane_spec = pl.BlockSpec((1, 1, GW), lambda g, c: (g, 0, 0))
    bc_out = pl.BlockSpec((L, SSM_STATE), lambda g, c: (rev(c), g))
    res = _call(
        body, name="ssd_bwd", grid=(G, nc),
        in_specs=[sp["grp"](rev), sp["bmat"](rev), sp["cmat"](rev), sp["grp"](rev), sp["dtraw"](rev), sp["grp"](rev),
                  sp["grp"](rev), pl.BlockSpec((1, NS, SSM_STATE, LANES), lambda g, c: (rev(c), g, 0, 0)),
                  sp["vec"], sp["vec"], sp["sel"], sp["selt"], pl.BlockSpec((1, 8, LANES), lambda g, c: (g, 0, 0)),
                  sp["gvec"], sp["gvec"]],
        out_specs=[sp["grp"](rev), sp["grp"](rev), bc_out, bc_out, part_spec, part_spec, lane_spec, lane_spec],
        out_shape=[jax.ShapeDtypeStruct((T, DI), BF16), jax.ShapeDtypeStruct((T, DI), F32),
                   jax.ShapeDtypeStruct((T, G * SSM_STATE), F32), jax.ShapeDtypeStruct((T, G * SSM_STATE), F32),
                   jax.ShapeDtypeStruct((G, T, LANES), F32), jax.ShapeDtypeStruct((G, T, LANES), F32),
                   jax.ShapeDtypeStruct((G, 1, GW), F32), jax.ShapeDtypeStruct((G, 1, GW), F32)],
        scratch_shapes=[pltpu.VMEM((NS, SSM_STATE, LANES), F32)], sem=("parallel", "arbitrary"),
        args=(xc, xc, xc, zx, zx, yssd, dyo, states, bias, alog, sel, selt, hsel, dskip, ng), carry=carry)
    return res if carry is None else (*res[:8], res[8:])


def _ssd_dt_bwd(zx, ddt_part, dac_part, bias, alog, DI):
    T = zx.shape[0]
    G, L = SSM_GROUPS, SSM_CHUNK
    nc = T // L
    dt_block = (2 * DI + 2 * G * SSM_STATE) // LANES

    def body(dtraw_ref, ddt_ref, dac_ref, bias_ref, alog_ref, o_ref, dal_ref, dbias_ref):
        @pl.when(pl.program_id(0) == 0)
        def _():
            dal_ref[...] = jnp.zeros_like(dal_ref)
            dbias_ref[...] = jnp.zeros_like(dbias_ref)

        raw = dtraw_ref[...] + bias_ref[...]
        dt = jax.nn.softplus(raw)
        a = -jnp.exp(alog_ref[...])
        dac, ddt = dac_ref[0], ddt_ref[0]
        for gi in range(1, G):
            dac = dac + dac_ref[gi]
            ddt = ddt + ddt_ref[gi]
        dda = _tri_dot(dac, True)
        dal_ref[...] += jnp.sum(dda * dt, axis=0, keepdims=True) * a
        draw = (dda * a + ddt) * jax.nn.sigmoid(raw)
        dbias_ref[...] += jnp.sum(draw, axis=0, keepdims=True)
        o_ref[...] = draw.astype(BF16)

    vec = pl.BlockSpec((1, LANES), lambda c: (0, 0))
    part = pl.BlockSpec((G, L, LANES), lambda c: (0, c, 0))
    return pl.pallas_call(
        body, name="ssd_dt_bwd", grid=(nc,),
        in_specs=[pl.BlockSpec((L, LANES), lambda c: (c, dt_block)), part, part, vec, vec],
        out_specs=[pl.BlockSpec((L, LANES), lambda c: (c, 0)), vec, vec],
        out_shape=[jax.ShapeDtypeStruct((T, LANES), BF16), jax.ShapeDtypeStruct((1, LANES), F32),
                   jax.ShapeDtypeStruct((1, LANES), F32)],
        compiler_params=_params(("arbitrary",)),
    )(zx, ddt_part, dac_part, bias, alog)


def _all_gather(shards, name):
    n = len(shards)

    def body(*refs):
        for phase in _gather_phases(refs[:n], refs[n:2 * n], *refs[2 * n:]):
            phase()

    any_spec = pl.BlockSpec(memory_space=pl.ANY)
    return pl.pallas_call(
        body, name=name, in_specs=[any_spec] * n, out_specs=[any_spec] * n,
        out_shape=[jax.ShapeDtypeStruct((N_DEV,) + s.shape, s.dtype) for s in shards],
        scratch_shapes=[pltpu.SemaphoreType.DMA((n, 7)), pltpu.SemaphoreType.DMA((n, 7)),
                        pltpu.SemaphoreType.DMA((n,))],
    )(*shards)


def _exchange(blocks, name):
    n = len(blocks)

    def body(*refs):
        local, sends, arrivals = _direct_copies(refs[:n], refs[n:2 * n], *refs[2 * n:])
        for cp in local + sends:
            cp.start()
        for cp in arrivals:
            cp.wait_recv()
        for cp in sends:
            cp.wait_send()
        for cp in local:
            cp.wait()

    any_spec = pl.BlockSpec(memory_space=pl.ANY)
    return pl.pallas_call(
        body, name=name, in_specs=[any_spec] * n, out_specs=[any_spec] * n,
        out_shape=[jax.ShapeDtypeStruct(b.shape, b.dtype) for b in blocks],
        scratch_shapes=[pltpu.SemaphoreType.DMA((n, 7)), pltpu.SemaphoreType.DMA((n, 7)),
                        pltpu.SemaphoreType.DMA((n,))],
    )(*blocks)


def _adamw(parts, w, m, v, name):
    nl = len(parts)
    R, C = parts[0].shape[1:]
    per_row = C * (N_DEV * nl * parts[0].dtype.itemsize + 7 * 4) * 2
    tr = R
    if R % 8 == 0:
        tr = 8
        for t in (16, 32, 64, 128, 256, 512):
            if R % t == 0 and t * per_row <= 24 * 1024 * 1024:
                tr = t
    nr = R // tr
    c1 = 1.0 - ADAM_B1 ** ADAM_STEP
    c2 = 1.0 - ADAM_B2 ** ADAM_STEP

    def body(*refs):
        p_refs = refs[:nl]
        w_ref, m_ref, v_ref, g_ref, d_ref, nm_ref, nv_ref = refs[nl:]
        for layer in range(nl):
            @pl.when(pl.program_id(0) == layer)
            def _(p_ref=p_refs[layer]):
                g = p_ref[0].astype(F32)
                for k in range(1, N_DEV):
                    g = g + p_ref[k].astype(F32)
                nm = ADAM_B1 * m_ref[...] + (1.0 - ADAM_B1) * g
                nv = ADAM_B2 * v_ref[...] + (1.0 - ADAM_B2) * (g * g)
                g_ref[...] = g
                nm_ref[...] = nm
                nv_ref[...] = nv
                d_ref[...] = -ADAM_LR * ((nm / c1) / (jnp.sqrt(nv / c2) + ADAM_EPS) + ADAM_WD * w_ref[...])

    def part_spec(layer):
        return pl.BlockSpec((N_DEV, tr, C), lambda l, i: (0, jnp.where(l == layer, i, jnp.where(l < layer, 0, nr - 1)), 0))

    blk = pl.BlockSpec((tr, C), lambda l, i: (l * nr + i, 0))
    out = jax.ShapeDtypeStruct((nl * R, C), F32)
    return pl.pallas_call(
        body, name=name, grid=(nl, nr),
        in_specs=[part_spec(layer) for layer in range(nl)] + [blk, blk, blk],
        out_specs=[blk, blk, blk, blk], out_shape=[out, out, out, out],
        compiler_params=_params(("arbitrary", "arbitrary")),
    )(*parts, w, m, v)


def _pad_cols(a, n):
    return jnp.pad(a, ((0, 0), (0, n - a.shape[1])))


def kernel(x, positions, mixer_norm, ffn_norm, attn_w_qkv, attn_q_norm, attn_k_norm, attn_sinks, attn_w_o, ssm_w_in, ssm_conv_w, ssm_conv_b, ssm_dt_bias, ssm_a_log, ssm_d, ssm_norm, ssm_w_out, ffn_w_gate, ffn_w_up, ffn_w_down, loss_target, m_mixer_norm, m_ffn_norm, m_attn_w_qkv, m_attn_q_norm, m_attn_k_norm, m_attn_sinks, m_attn_w_o, m_ssm_w_in, m_ssm_conv_w, m_ssm_conv_b, m_ssm_dt_bias, m_ssm_a_log, m_ssm_d, m_ssm_norm, m_ssm_w_out, m_ffn_w_gate, m_ffn_w_up, m_ffn_w_down, v_mixer_norm, v_ffn_norm, v_attn_w_qkv, v_attn_q_norm, v_attn_k_norm, v_attn_sinks, v_attn_w_o, v_ssm_w_in, v_ssm_conv_w, v_ssm_conv_b, v_ssm_dt_bias, v_ssm_a_log, v_ssm_d, v_ssm_norm, v_ssm_w_out, v_ffn_w_gate, v_ffn_w_up, v_ffn_w_down):
    T, D = x.shape[1], x.shape[2]
    HQ = D // HEAD
    HKV = HQ // ATT_GROUP
    QW = (HQ + 2 * HKV) * HEAD
    DI = 2 * D
    H = DI // HEAD
    G = SSM_GROUPS
    GW = DI // G
    CD = DI + 2 * G * SSM_STATE
    ZW = -(-(DI + CD + LANES) // 512) * 512
    IW = DI + CD + H
    assert T % 512 == 0 and D % 256 == 0 and HKV % 2 == 0 and GW % LANES == 0 and H <= LANES

    weights = dict(mixer_norm=mixer_norm, ffn_norm=ffn_norm, attn_w_qkv=attn_w_qkv, attn_q_norm=attn_q_norm,
                   attn_k_norm=attn_k_norm, attn_sinks=attn_sinks, attn_w_o=attn_w_o, ssm_w_in=ssm_w_in,
                   ssm_conv_w=ssm_conv_w, ssm_conv_b=ssm_conv_b, ssm_dt_bias=ssm_dt_bias, ssm_a_log=ssm_a_log,
                   ssm_d=ssm_d, ssm_norm=ssm_norm, ssm_w_out=ssm_w_out, ffn_w_gate=ffn_w_gate, ffn_w_up=ffn_w_up,
                   ffn_w_down=ffn_w_down)
    mom_m = dict(mixer_norm=m_mixer_norm, ffn_norm=m_ffn_norm, attn_w_qkv=m_attn_w_qkv, attn_q_norm=m_attn_q_norm,
                 attn_k_norm=m_attn_k_norm, attn_sinks=m_attn_sinks, attn_w_o=m_attn_w_o, ssm_w_in=m_ssm_w_in,
                 ssm_conv_w=m_ssm_conv_w, ssm_conv_b=m_ssm_conv_b, ssm_dt_bias=m_ssm_dt_bias, ssm_a_log=m_ssm_a_log,
                 ssm_d=m_ssm_d, ssm_norm=m_ssm_norm, ssm_w_out=m_ssm_w_out, ffn_w_gate=m_ffn_w_gate,
                 ffn_w_up=m_ffn_w_up, ffn_w_down=m_ffn_w_down)
    mom_v = dict(mixer_norm=v_mixer_norm, ffn_norm=v_ffn_norm, attn_w_qkv=v_attn_w_qkv, attn_q_norm=v_attn_q_norm,
                 attn_k_norm=v_attn_k_norm, attn_sinks=v_attn_sinks, attn_w_o=v_attn_w_o, ssm_w_in=v_ssm_w_in,
                 ssm_conv_w=v_ssm_conv_w, ssm_conv_b=v_ssm_conv_b, ssm_dt_bias=v_ssm_dt_bias, ssm_a_log=v_ssm_a_log,
                 ssm_d=v_ssm_d, ssm_norm=v_ssm_norm, ssm_w_out=v_ssm_w_out, ffn_w_gate=v_ffn_w_gate,
                 ffn_w_up=v_ffn_w_up, ffn_w_down=v_ffn_w_down)
    big = ["attn_w_qkv", "attn_w_o", "ssm_w_in", "ssm_w_out", "ffn_w_gate", "ffn_w_up", "ffn_w_down"]

    def flat2(a):
        return a.reshape(-1, a.shape[-1])

    def shard(n, layer=0):
        return weights[n][layer].astype(BF16)

    def from_cols(g):
        return g.transpose(1, 0, 2).reshape(g.shape[1], N_DEV * g.shape[2])

    def from_rows(g):
        return g.reshape(N_DEV * g.shape[1], g.shape[2])

    xs = x[0]
    tgt = loss_target[0]
    inv_freq = ROPE_THETA ** (-jnp.arange(0, HEAD, 2, dtype=F32) / HEAD)
    ang = positions[0].astype(F32)[:, None] * inv_freq
    cos = jnp.tile(jnp.cos(ang), (1, 4))
    sin = jnp.tile(jnp.concatenate([-jnp.sin(ang), jnp.sin(ang)], axis=1), (1, 2))
    gq = jnp.tile(attn_q_norm, (1, 2))
    gk = jnp.tile(attn_k_norm, (1, 2))
    sinkcol = jnp.repeat(attn_sinks.reshape(HKV, ATT_GROUP // 2, 2).transpose(0, 2, 1), WINDOW, axis=2)[..., None]
    bias_p = _pad_cols(ssm_dt_bias, LANES)
    alog_p = _pad_cols(ssm_a_log, LANES)
    dskip = jnp.repeat(ssm_d, HEAD, axis=1)
    lane_head = jnp.arange(DI) // HEAD
    sel = (jnp.arange(LANES)[None, :, None] == lane_head.reshape(G, 1, GW)).astype(BF16)
    selt = sel.transpose(0, 2, 1)
    hsel = (jnp.arange(LANES)[None, None, :] == (jnp.arange(G)[:, None, None] * (H // G) + jnp.arange(8)[None, :, None])
            ).astype(BF16) * (jnp.arange(8)[None, :, None] < H // G)
    vec_w = CD // N_DEV
    small = jnp.concatenate([ssm_conv_w[0], ssm_conv_b, _pad_cols(ssm_norm, vec_w),
                             jnp.zeros((2, vec_w), F32)], axis=0)
    g_qkv, g_o, small_all = _all_gather([shard("attn_w_qkv"), shard("attn_w_o"), small], "gather_first")
    w_qkv, w_o = from_cols(g_qkv), from_rows(g_o)
    conv_w = small_all[:, :SSM_CONV].transpose(1, 0, 2).reshape(SSM_CONV, CD)
    conv_b = small_all[:, SSM_CONV].reshape(1, CD)
    ng = small_all[:, SSM_CONV + 1, :DI // N_DEV].reshape(1, DI)

    def rows_to_blocks(p):
        return p.reshape(N_DEV, p.shape[0] // N_DEV, p.shape[1])

    def cols_to_blocks(p):
        return p.reshape(p.shape[0], N_DEV, p.shape[1] // N_DEV).transpose(1, 0, 2)

    hm0 = _rms_fwd(xs, mixer_norm[0:1], "rms_fwd_m0")
    qkv, got = _matmul(hm0, w_qkv, mode="nn", out_dtype=F32, name="mm_qkv", carry=[shard("ffn_w_gate", 0)])
    w_gate = [from_cols(got[0]), None]
    qr, kd, vd = _attn_prep_fwd(qkv, cos, sin, gq, gk, D, HKV)
    o, got = _attn_fwd(qr, kd, vd, sinkcol, HKV, carry=[shard("ffn_w_up", 0), shard("ffn_w_down", 0)])
    w_up = [from_cols(got[0]), None]
    w_down = [from_rows(got[1]), None]
    x1 = _matmul(o, w_o, mode="nn", out_dtype=F32, name="mm_attn_out", add=xs)
    hf0 = _rms_fwd(x1, ffn_norm[0:1], "rms_fwd_f0")
    gate0, up0, act0, got = _ffn_up(hf0, w_gate[0], w_up[0], "ffn_up_0", carry=[shard("ssm_w_in"), shard("ssm_w_out")])
    w_in = _pad_cols(from_cols(got[0]), ZW)
    w_out = from_rows(got[1])
    x2 = _matmul(act0, w_down[0], mode="nn", out_dtype=F32, name="mm_ffn_down_0", add=x1)
    hm1 = _rms_fwd(x2, mixer_norm[1:2], "rms_fwd_m1")
    zx, got = _matmul(hm1, w_in, mode="nn", out_dtype=F32, name="mm_ssm_in",
                      carry=[shard("ffn_w_gate", 1), shard("ffn_w_up", 1)])
    w_gate[1], w_up[1] = from_cols(got[0]), from_cols(got[1])
    xc = _conv_fwd(zx, conv_w, conv_b, DI, CD)
    yssd, yout, states, got = _ssd_fwd(zx, xc, bias_p, alog_p, sel, dskip, ng, DI, carry=[shard("ffn_w_down", 1)])
    w_down[1] = from_rows(got[0])
    x3 = _matmul(yout, w_out, mode="nn", out_dtype=F32, name="mm_ssm_out", add=x2)
    hf1 = _rms_fwd(x3, ffn_norm[1:2], "rms_fwd_f1")
    gate1, up1, act1 = _ffn_up(hf1, w_gate[1], w_up[1], "ffn_up_1")
    x4 = _matmul(act1, w_down[1], mode="nn", out_dtype=F32, name="mm_ffn_down_1", add=x3)
    sq, dx4, dx4b = _loss_head(x4, tgt)
    loss = lax.psum(sq[0, 0] * (0.5 / D), ("x", "y", "c"))

    def ffn_bwd(dy, dyb, hf, gate, up, act, layer, xin, gain):
        dg, du = _ffn_dact(dyb, w_down[layer], gate, up, f"ffn_dact_{layer}")
        g_down = _matmul(act, dyb, mode="tn", out_dtype=BF16, name=f"mm_dw_down_{layer}")
        g_gate = _matmul(hf, dg, mode="tn", out_dtype=BF16, name=f"mm_dw_gate_{layer}")
        g_up = _matmul(hf, du, mode="tn", out_dtype=BF16, name=f"mm_dw_up_{layer}")
        dh = _matmul(dg, w_gate[layer], mode="nt", out_dtype=F32, name=f"mm_dh_gate_{layer}")
        dh = _matmul(du, w_up[layer], mode="nt", out_dtype=F32, name=f"mm_dh_up_{layer}", add=dh)
        dx, dxb, dgain = _rms_bwd(xin, gain, dh, dy, f"rms_bwd_f{layer}")
        gate_blocks = cols_to_blocks(g_gate)
        half = gate_blocks.shape[1] // 2
        return dx, dxb, dgain, [gate_blocks[:, :half], gate_blocks[:, half:], cols_to_blocks(g_up), rows_to_blocks(g_down)]

    dx3, dx3b, d_fn1, ffn1_blocks = ffn_bwd(dx4, dx4b, hf1, gate1, up1, act1, 1, x3, ffn_norm[1:2])
    dyo = _matmul(dx3b, w_out, mode="nt", out_dtype=F32, name="mm_dyout")
    g_wout = _matmul(yout, dx3b, mode="tn", out_dtype=BF16, name="mm_dw_ssm_out")
    dz, dxx, dbm, dcm, ddt_p, dac_p, dd_l, dng_l, got1 = _ssd_bwd(
        zx, xc, yssd, dyo, states, bias_p, alog_p, sel, selt, hsel, dskip, ng, DI,
        carry=ffn1_blocks + [rows_to_blocks(g_wout)])
    ddt_raw, d_alog, d_bias = _ssd_dt_bwd(zx, ddt_p, dac_p, bias_p, alog_p, DI)
    dxbc, d_convw, d_convb = _conv_bwd(zx, jnp.concatenate([dxx, dbm, dcm], axis=1), conv_w, conv_b, DI, CD)
    dzx = jnp.concatenate([dz, dxbc, ddt_raw, jnp.zeros((T, ZW - DI - CD - LANES), BF16)], axis=1)
    g_win = _matmul(hm1, dzx, mode="tn", out_dtype=BF16, name="mm_dw_ssm_in")[:, :IW]
    dh, got2 = _matmul(dzx, w_in, mode="nt", out_dtype=F32, name="mm_dh_ssm_in", carry=[cols_to_blocks(g_win)])
    dx2, dx2b, d_mn1 = _rms_bwd(x2, mixer_norm[1:2], dh, dx3, "rms_bwd_m1")
    dx1, dx1b, d_fn0, ffn0_blocks = ffn_bwd(dx2, dx2b, hf0, gate0, up0, act0, 0, x1, ffn_norm[0:1])
    do, got_a = _matmul(dx1b, w_o, mode="nt", out_dtype=F32, name="mm_do", carry=ffn0_blocks[0:1])
    g_wo, got_b = _matmul(o, dx1b, mode="tn", out_dtype=BF16, name="mm_dw_attn_out", carry=ffn0_blocks[1:2])
    dq, dkd, dvd, dsink, got3 = _attn_bwd(qr, kd, vd, o, do, sinkcol, HKV,
                                          carry=ffn0_blocks[2:] + [rows_to_blocks(g_wo)])
    dqkv, dgq_l, dgk_l = _attn_prep_bwd(qkv, dq, dkd, dvd, cos, sin, gq, gk, D, HKV)
    g_wqkv = _matmul(hm0, dqkv, mode="tn", out_dtype=BF16, name="mm_dw_qkv")
    dh, got_c = _matmul(dqkv, w_qkv, mode="nt", out_dtype=F32, name="mm_dh_qkv", carry=[cols_to_blocks(g_wqkv)])
    dx0, _, d_mn0 = _rms_bwd(xs, mixer_norm[0:1], dh, dx1, "rms_bwd_m0")

    d_ng = dng_l.reshape(1, DI)
    vec_send = jnp.concatenate([
        d_convw.reshape(SSM_CONV, N_DEV, vec_w).transpose(1, 0, 2), d_convb.reshape(1, N_DEV, vec_w).transpose(1, 0, 2),
        _pad_cols(d_ng.reshape(N_DEV, DI // N_DEV), vec_w)[:, None, :], jnp.zeros((N_DEV, 2, vec_w), F32)], axis=1)
    d_sinks = dsink[:, :, 0].reshape(1, HQ)
    d_gq = dgq_l[:, :HEAD] + dgq_l[:, HEAD:]
    d_gk = dgk_l[:, :HEAD] + dgk_l[:, HEAD:]
    d_dskip = dd_l.reshape(H, HEAD).sum(axis=1).reshape(1, H)
    rep_names = ["mixer_norm", "ffn_norm", "attn_q_norm", "attn_k_norm", "attn_sinks", "ssm_dt_bias", "ssm_a_log",
                 "ssm_d"]
    rep_grads = [jnp.concatenate([d_mn0, d_mn1], axis=0), jnp.concatenate([d_fn0, d_fn1], axis=0), d_gq, d_gk,
                 d_sinks, d_bias[:, :H], d_alog[:, :H], d_dskip]
    rep_sizes = [weights[n].size for n in rep_names]
    rep_len = -(-sum(rep_sizes) // (8 * LANES)) * 8 * LANES

    def pack(arrs):
        flat = jnp.concatenate([a.reshape(-1) for a in arrs])
        return jnp.pad(flat, (0, rep_len - flat.shape[0])).reshape(rep_len // LANES, LANES)

    rep_send = jnp.broadcast_to(pack(rep_grads)[None], (N_DEV, rep_len // LANES, LANES))
    got4 = _exchange([vec_send, rep_send], "exchange_last")
    parts_of = {
        "attn_w_qkv": [got_c[0]], "attn_w_o": [got3[2]], "ssm_w_in": [got2[0]], "ssm_w_out": [got1[4]],
        "ffn_w_gate": [got_a[0], got_b[0], got1[0], got1[1]], "ffn_w_up": [got3[0], got1[2]],
        "ffn_w_down": [got3[1], got1[3]],
    }

    out = {}
    for n in big:
        res = _adamw(parts_of[n], flat2(weights[n]), flat2(mom_m[n]), flat2(mom_v[n]), f"adamw_{n}")
        out[n] = [r.reshape(weights[n].shape) for r in res]

    def vec_block(d):
        return jnp.concatenate([d["ssm_conv_w"][0], d["ssm_conv_b"], _pad_cols(d["ssm_norm"], vec_w),
                                jnp.zeros((2, vec_w), F32)], axis=0)

    res = _adamw([got4[0]], vec_block(weights), vec_block(mom_m), vec_block(mom_v), "adamw_vectors")
    out["ssm_conv_w"] = [r[:SSM_CONV][None] for r in res]
    out["ssm_conv_b"] = [r[SSM_CONV:SSM_CONV + 1] for r in res]
    out["ssm_norm"] = [r[SSM_CONV + 1:SSM_CONV + 2, :DI // N_DEV] for r in res]
    res = _adamw([got4[1]], pack([weights[n] for n in rep_names]), pack([mom_m[n] for n in rep_names]),
                 pack([mom_v[n] for n in rep_names]), "adamw_replicated")
    offs = 0
    for n, sz in zip(rep_names, rep_sizes):
        out[n] = [r.reshape(-1)[offs:offs + sz].reshape(weights[n].shape) for r in res]
        offs += sz

    names = list(weights)
    return (loss, dx0[None], *[out[n][0] for n in names], *[out[n][1] for n in names],
            *[out[n][2] for n in names], *[out[n][3] for n in names])
```

```python
import functools
import math

import jax
import jax.numpy as jnp
from jax import lax
from jax.experimental import pallas as pl
from jax.experimental.pallas import tpu as pltpu

F32 = jnp.float32
BF16 = jnp.bfloat16

N_DEV = 8
EPS = 1e-6
LANES = 128
HEAD = 64
ATT_GROUP = 8
ATT_GW = ATT_GROUP * HEAD
WINDOW = 128
ATT_STEP_BLOCKS = 4
ROPE_THETA = 10000.0
SSM_GROUPS = 8
SSM_STATE = 128
SSM_CONV = 4
SSM_CHUNK = 256
HALO = 8
ADAM_LR, ADAM_B1, ADAM_B2, ADAM_EPS, ADAM_WD, ADAM_STEP = 0.001, 0.9, 0.999, 1e-08, 0.01, 10
VMEM_LIMIT = 56 * 1024 * 1024
MATMUL_VMEM = 44 * 1024 * 1024
MESH = pl.DeviceIdType.MESH

_NN = (((1,), (0,)), ((), ()))
_NT = (((1,), (1,)), ((), ()))
_TN = (((0,), (0,)), ((), ()))


def _dot(a, b, dims=_NN):
    return lax.dot_general(a, b, dims, preferred_element_type=F32)


def _tile(n, cap):
    if n % LANES:
        return n
    best = LANES
    for t in range(LANES, min(n, cap) + 1, LANES):
        if n % t == 0:
            best = t
    return best


def _params(sem):
    return pltpu.CompilerParams(dimension_semantics=sem, vmem_limit_bytes=VMEM_LIMIT)


def _slot(px, py, pc):
    return 4 * px + 2 * py + pc


def _direct_copies(srcs, dsts, send_sems, recv_sems, local_sems, with_arrivals=True):
    x, y, c = lax.axis_index("x"), lax.axis_index("y"), lax.axis_index("c")
    me = _slot(x, y, c)
    peers = [(x ^ (m >> 2), y ^ ((m >> 1) & 1), c ^ (m & 1)) for m in range(1, N_DEV)]
    local, sends, arrivals = [], [], []
    for w, (src, dst) in enumerate(zip(srcs, dsts)):
        sliced = src.shape == dst.shape
        local.append(pltpu.make_async_copy(src.at[me] if sliced else src, dst.at[me], local_sems.at[w]))
        for k, peer in enumerate(peers):
            sems = dict(send_sem=send_sems.at[w, k], recv_sem=recv_sems.at[w, k], device_id=peer, device_id_type=MESH)
            sends.append(pltpu.make_async_remote_copy(
                src_ref=src.at[_slot(*peer)] if sliced else src, dst_ref=dst.at[me], **sems))
            if with_arrivals:
                arrivals.append(pltpu.make_async_remote_copy(
                    src_ref=src.at[me] if sliced else src, dst_ref=dst.at[_slot(*peer)], **sems))
    return local, sends, arrivals


def _gather_phases(srcs, dsts, send_sems, recv_sems, local_sems):
    x, y, c = lax.axis_index("x"), lax.axis_index("y"), lax.axis_index("c")
    me, sibling = (x, y, c), (x, y, 1 - c)
    chips = [(1 - x, y), (x, 1 - y), (1 - x, 1 - y)]
    n = len(srcs)

    def copy(w, k, block, to, src=None):
        dst = dsts[w].at[_slot(*block)]
        return pltpu.make_async_remote_copy(
            src_ref=dst if src is None else src, dst_ref=dst, send_sem=send_sems.at[w, k],
            recv_sem=recv_sems.at[w, k], device_id=to, device_id_type=MESH)

    def first_sends(w):
        return [copy(w, 0, me, sibling, src=srcs[w])] + [copy(w, 1 + j, me, (*chip, c), src=srcs[w])
                                                         for j, chip in enumerate(chips)]

    def start():
        for w in range(n):
            pltpu.make_async_copy(srcs[w], dsts[w].at[_slot(*me)], local_sems.at[w]).start()
            for cp in first_sends(w):
                cp.start()

    def forward():
        for w in range(n):
            for j, chip in enumerate(chips):
                copy(w, 1 + j, (*chip, c), me).wait_recv()
                copy(w, 4 + j, (*chip, c), sibling).start()

    def finish():
        for w in range(n):
            copy(w, 0, sibling, me).wait_recv()
            for j, chip in enumerate(chips):
                copy(w, 4 + j, (*chip, 1 - c), me).wait_recv()
        for w in range(n):
            for cp in first_sends(w) + [copy(w, 4 + j, (*chip, c), sibling) for j, chip in enumerate(chips)]:
                cp.wait_send()
            pltpu.make_async_copy(srcs[w], dsts[w].at[_slot(*me)], local_sems.at[w]).wait()

    return start, forward, finish


def _call(body, *, name, grid, in_specs, out_specs, out_shape, sem, args, scratch_shapes=(), carry=None):
    if carry is None:
        return pl.pallas_call(body, name=name, grid=grid, in_specs=in_specs, out_specs=out_specs, out_shape=out_shape,
                              scratch_shapes=list(scratch_shapes), compiler_params=_params(sem))(*args)
    n_in, n_out, n_sc, n_c = len(in_specs), len(out_specs), len(scratch_shapes), len(carry)
    gather = all(a.ndim == 2 for a in carry)
    assert gather or all(a.ndim == 3 and a.shape[0] == N_DEV for a in carry)
    recv_shape = [jax.ShapeDtypeStruct((N_DEV,) + a.shape if gather else a.shape, a.dtype) for a in carry]
    n_steps = math.prod(grid)

    def wrapped(*refs):
        ins, c_in = refs[:n_in], refs[n_in:n_in + n_c]
        outs, c_out = refs[n_in + n_c:n_in + n_c + n_out], refs[n_in + n_c + n_out:n_in + 2 * n_c + n_out]
        scr = refs[n_in + 2 * n_c + n_out:n_in + 2 * n_c + n_out + n_sc]
        sems = refs[-3:]
        step = functools.reduce(lambda acc, d: acc * grid[d] + pl.program_id(d), range(len(grid)), 0)
        if gather:
            start, forward, finish = _gather_phases(c_in, c_out, *sems)
        else:
            def start():
                local, sends, _ = _direct_copies(c_in, c_out, *sems, with_arrivals=False)
                for cp in local + sends:
                    cp.start()

            def finish():
                local, sends, arrivals = _direct_copies(c_in, c_out, *sems)
                for cp in arrivals:
                    cp.wait_recv()
                for cp in sends:
                    cp.wait_send()
                for cp in local:
                    cp.wait()

        pl.when(step == 0)(start)
        if gather:
            pl.when(step == min((3 * n_steps) // 4, n_steps - 1))(forward)
        body(*ins, *outs, *scr)
        pl.when(step == n_steps - 1)(finish)

    any_spec = pl.BlockSpec(memory_space=pl.ANY)
    res = pl.pallas_call(
        wrapped, name=name, grid=grid, in_specs=list(in_specs) + [any_spec] * n_c,
        out_specs=list(out_specs) + [any_spec] * n_c, out_shape=list(out_shape) + recv_shape,
        scratch_shapes=list(scratch_shapes) + [pltpu.SemaphoreType.DMA((n_c, N_DEV - 1)),
                                               pltpu.SemaphoreType.DMA((n_c, N_DEV - 1)), pltpu.SemaphoreType.DMA((n_c,))],
        compiler_params=_params(("arbitrary",) * len(grid)),
    )(*args, *carry)
    return res


def _matmul(a, b, *, mode, out_dtype, name, add=None, carry=None):
    if mode == "nn":
        (M, K), N = a.shape, b.shape[1]
    elif mode == "nt":
        (M, K), N = a.shape, b.shape[0]
    else:
        (K, M), N = a.shape, b.shape[1]
    assert a.dtype == BF16 and b.dtype == BF16
    has_add = add is not None
    tm, tn = (_tile(M, 512), _tile(N, 512)) if mode == "tn" else (_tile(M, 1024), _tile(N, 512))
    fixed = 2 * tm * tn * (jnp.dtype(out_dtype).itemsize + (4 if has_add else 0)) + tm * tn * 4
    per_k = 2 * 2 * (tm + tn) + (2 * tm if mode == "tn" else 0)
    tk = _tile(K, max(LANES, (MATMUL_VMEM - fixed) // per_k))
    nk = K // tk
    dims = _NT if mode == "nt" else _NN
    if mode == "tn":
        a_spec = pl.BlockSpec((tk, tm), lambda i, j, k: (jnp.where(j == 0, k, 0), i))
    else:
        a_spec = pl.BlockSpec((tm, tk), lambda i, j, k: (i, k))
    b_spec = pl.BlockSpec((tn, tk), lambda i, j, k: (j, k)) if mode == "nt" else pl.BlockSpec((tk, tn), lambda i, j, k: (k, j))
    o_spec = pl.BlockSpec((tm, tn), lambda i, j, k: (i, j))

    def body(*refs):
        a_ref, b_ref = refs[:2]
        add_ref = refs[2] if has_add else None
        o_ref = refs[2 + has_add]
        scratch = list(refs[3 + has_add:])
        at = scratch.pop(0) if mode == "tn" else None
        acc = scratch.pop(0) if nk > 1 else None
        j, k = pl.program_id(1), pl.program_id(2)
        if mode == "tn":
            @pl.when(j == 0)
            def _():
                at[k] = a_ref[...].T

            part = _dot(at[k], b_ref[...], dims)
        else:
            part = _dot(a_ref[...], b_ref[...], dims)

        def finish(r):
            if has_add:
                r = r + add_ref[...]
            o_ref[...] = r.astype(out_dtype)

        if nk == 1:
            finish(part)
        else:
            @pl.when(k == 0)
            def _():
                acc[...] = part

            @pl.when(jnp.logical_and(k > 0, k < nk - 1))
            def _():
                acc[...] += part

            @pl.when(k == nk - 1)
            def _():
                finish(acc[...] + part)

    scratch = ([pltpu.VMEM((nk, tm, tk), BF16)] if mode == "tn" else []) + ([pltpu.VMEM((tm, tn), F32)] if nk > 1 else [])
    res = _call(
        body, name=name, grid=(M // tm, N // tn, nk),
        in_specs=[a_spec, b_spec] + ([o_spec] if has_add else []),
        out_specs=[o_spec], out_shape=[jax.ShapeDtypeStruct((M, N), out_dtype)],
        scratch_shapes=scratch, sem=("parallel", "arbitrary", "arbitrary"),
        args=(a, b, add) if has_add else (a, b), carry=carry)
    return res[0] if carry is None else (res[0], res[1:])


def _rms_fwd(x, gain, name):
    T, D = x.shape
    tr = 256

    def body(x_ref, g_ref, h_ref):
        xv = x_ref[...]
        rstd = lax.rsqrt(jnp.mean(xv * xv, axis=1, keepdims=True) + EPS)
        h_ref[...] = (xv * rstd * g_ref[...]).astype(BF16)

    return pl.pallas_call(
        body, name=name, grid=(T // tr,),
        in_specs=[pl.BlockSpec((tr, D), lambda i: (i, 0)), pl.BlockSpec((1, D), lambda i: (0, 0))],
        out_specs=pl.BlockSpec((tr, D), lambda i: (i, 0)),
        out_shape=jax.ShapeDtypeStruct((T, D), BF16),
        compiler_params=_params(("parallel",)),
    )(x, gain)


def _rms_bwd(x, gain, dh, dres, name):
    T, D = x.shape
    tr = 256

    def body(x_ref, g_ref, dh_ref, dr_ref, dx_ref, dxb_ref, dg_ref):
        @pl.when(pl.program_id(0) == 0)
        def _():
            dg_ref[...] = jnp.zeros_like(dg_ref)

        xv = x_ref[...]
        rstd = lax.rsqrt(jnp.mean(xv * xv, axis=1, keepdims=True) + EPS)
        xhat = xv * rstd
        dy = dh_ref[...].astype(F32)
        dg_ref[...] += jnp.sum(dy * xhat, axis=0, keepdims=True)
        dxh = dy * g_ref[...]
        dx = dr_ref[...] + rstd * (dxh - xhat * jnp.mean(dxh * xhat, axis=1, keepdims=True))
        dx_ref[...] = dx
        dxb_ref[...] = dx.astype(BF16)

    row = pl.BlockSpec((tr, D), lambda i: (i, 0))
    vec = pl.BlockSpec((1, D), lambda i: (0, 0))
    return pl.pallas_call(
        body, name=name, grid=(T // tr,), in_specs=[row, vec, row, row], out_specs=[row, row, vec],
        out_shape=[jax.ShapeDtypeStruct((T, D), F32), jax.ShapeDtypeStruct((T, D), BF16),
                   jax.ShapeDtypeStruct((1, D), F32)],
        compiler_params=_params(("arbitrary",)),
    )(x, gain, dh, dres)


def _loss_head(y, target):
    T, D = y.shape
    tr = 256

    def body(y_ref, t_ref, s_ref, d_ref, db_ref):
        @pl.when(pl.program_id(0) == 0)
        def _():
            s_ref[...] = jnp.zeros_like(s_ref)

        e = y_ref[...] - t_ref[...]
        s_ref[...] += jnp.sum(jnp.sum(e * e, axis=1, keepdims=True), axis=0, keepdims=True)
        d = e * (1.0 / D)
        d_ref[...] = d
        db_ref[...] = d.astype(BF16)

    row = pl.BlockSpec((tr, D), lambda i: (i, 0))
    return pl.pallas_call(
        body, name="loss_head", grid=(T // tr,), in_specs=[row, row],
        out_specs=[pl.BlockSpec((1, 1), lambda i: (0, 0)), row, row],
        out_shape=[jax.ShapeDtypeStruct((1, 1), F32), jax.ShapeDtypeStruct((T, D), F32),
                   jax.ShapeDtypeStruct((T, D), BF16)],
        compiler_params=_params(("arbitrary",)),
    )(y, target)


def _ffn_up(h, wg, wu, name, carry=None):
    (T, D), Fd = h.shape, wg.shape[1]
    tm, tn = _tile(T, 1024), _tile(Fd, 512)

    def body(h_ref, wg_ref, wu_ref, g_ref, u_ref, a_ref):
        hv = h_ref[...]
        g = _dot(hv, wg_ref[...])
        g_ref[...] = g.astype(BF16)
        sg = g * jax.nn.sigmoid(g)
        u = _dot(hv, wu_ref[...])
        u_ref[...] = u.astype(BF16)
        a_ref[...] = (sg * u).astype(BF16)

    w_spec = pl.BlockSpec((D, tn), lambda i, j: (0, j))
    o_spec = pl.BlockSpec((tm, tn), lambda i, j: (i, j))
    res = _call(
        body, name=name, grid=(T // tm, Fd // tn),
        in_specs=[pl.BlockSpec((tm, D), lambda i, j: (i, 0)), w_spec, w_spec],
        out_specs=[o_spec, o_spec, o_spec],
        out_shape=[jax.ShapeDtypeStruct((T, Fd), BF16)] * 3,
        sem=("parallel", "arbitrary"), args=(h, wg, wu), carry=carry)
    return res if carry is None else (*res[:3], res[3:])


def _ffn_dact(dy, wd, gate, up, name):
    (T, D), Fd = dy.shape, wd.shape[0]
    tm, tn = _tile(T, 1024), _tile(Fd, 512)

    def body(dy_ref, wd_ref, g_ref, u_ref, dg_ref, du_ref):
        da = _dot(dy_ref[...], wd_ref[...], _NT)
        g = g_ref[...].astype(F32)
        sg = jax.nn.sigmoid(g)
        du_ref[...] = (da * g * sg).astype(BF16)
        dg_ref[...] = (da * u_ref[...].astype(F32) * sg * (1.0 + g * (1.0 - sg))).astype(BF16)

    o_spec = pl.BlockSpec((tm, tn), lambda i, j: (i, j))
    return pl.pallas_call(
        body, name=name, grid=(T // tm, Fd // tn),
        in_specs=[pl.BlockSpec((tm, D), lambda i, j: (i, 0)), pl.BlockSpec((tn, D), lambda i, j: (j, 0)),
                  o_spec, o_spec],
        out_specs=[o_spec, o_spec],
        out_shape=[jax.ShapeDtypeStruct((T, Fd), BF16), jax.ShapeDtypeStruct((T, Fd), BF16)],
        compiler_params=_params(("parallel", "arbitrary")),
    )(dy, wd, gate, up)


def _is_a(shape):
    return lax.broadcasted_iota(jnp.int32, shape, 1) % LANES < HEAD


def _split2(v):
    hi = v.astype(BF16)
    return hi, (v - hi.astype(F32)).astype(BF16)


def _split3(v):
    hi = v.astype(BF16)
    r = v - hi.astype(F32)
    mid = r.astype(BF16)
    return hi, mid, (r - mid.astype(F32)).astype(BF16)


def _dot_split(v, m, pieces, dims=_NN):
    parts = _split3(v) if pieces == 3 else _split2(v)
    out = _dot(parts[0], m, dims)
    for p in parts[1:]:
        out = out + _dot(p, m, dims)
    return out


def _head_blockdiag():
    r = lax.broadcasted_iota(jnp.int32, (LANES, LANES), 0) // HEAD
    c = lax.broadcasted_iota(jnp.int32, (LANES, LANES), 1) // HEAD
    return (r == c).astype(BF16)


def _swap_half(v):
    lane = lax.broadcasted_iota(jnp.int32, v.shape, 1)
    return jnp.where(lane % HEAD < HEAD // 2, pltpu.roll(v, LANES - HEAD // 2, axis=1), pltpu.roll(v, HEAD // 2, axis=1))


def _attn_prep_fwd(qkv, cos, sin, gq, gk, D, HKV):
    T, QW = qkv.shape
    tr = 256
    nq, nk = D // LANES, HKV // 2
    KW = HKV * LANES

    def body(x_ref, cos_ref, sin_ref, gq_ref, gk_ref, q_ref, k_ref, v_ref):
        bd = _head_blockdiag()
        cs, sn = cos_ref[...], sin_ref[...]
        isa = _is_a((tr, LANES))

        def normrope(xv, g):
            ms = _dot_split(xv * xv, bd, 2) * (1.0 / HEAD)
            xn = xv * lax.rsqrt(ms + EPS) * g
            return xn * cs + _swap_half(xn) * sn

        def dup(v):
            r = pltpu.roll(v, HEAD, axis=1)
            return jnp.where(isa, v, r), jnp.where(isa, r, v)

        for s in range(nq):
            sl = slice(s * LANES, (s + 1) * LANES)
            q_ref[:, sl] = normrope(x_ref[:, sl], gq_ref[...]).astype(BF16)
        for s in range(nk):
            ka, kb = dup(normrope(x_ref[:, D + s * LANES:D + (s + 1) * LANES], gk_ref[...]))
            k_ref[:, 2 * s * LANES:(2 * s + 1) * LANES] = ka.astype(BF16)
            k_ref[:, (2 * s + 1) * LANES:(2 * s + 2) * LANES] = kb.astype(BF16)
            va, vb = dup(x_ref[:, D + (nk + s) * LANES:D + (nk + s + 1) * LANES])
            v_ref[:, 2 * s * LANES:(2 * s + 1) * LANES] = va.astype(BF16)
            v_ref[:, (2 * s + 1) * LANES:(2 * s + 2) * LANES] = vb.astype(BF16)

    tab = pl.BlockSpec((tr, LANES), lambda i: (i, 0))
    vec = pl.BlockSpec((1, LANES), lambda i: (0, 0))
    return pl.pallas_call(
        body, name="attn_prep_fwd", grid=(T // tr,),
        in_specs=[pl.BlockSpec((tr, QW), lambda i: (i, 0)), tab, tab, vec, vec],
        out_specs=[pl.BlockSpec((tr, D), lambda i: (i, 0)), pl.BlockSpec((tr, KW), lambda i: (i, 0)),
                   pl.BlockSpec((tr, KW), lambda i: (i, 0))],
        out_shape=[jax.ShapeDtypeStruct((T, D), BF16), jax.ShapeDtypeStruct((T, KW), BF16),
                   jax.ShapeDtypeStruct((T, KW), BF16)],
        compiler_params=_params(("parallel",)),
    )(qkv, cos, sin, gq, gk)


def _attn_prep_bwd(qkv, dq, dkd, dvd, cos, sin, gq, gk, D, HKV):
    T, QW = qkv.shape
    tr = 256
    nq, nk = D // LANES, HKV // 2
    KW = HKV * LANES

    def body(x_ref, dq_ref, dk_ref, dv_ref, cos_ref, sin_ref, gq_ref, gk_ref, o_ref, dgq_ref, dgk_ref):
        @pl.when(pl.program_id(0) == 0)
        def _():
            dgq_ref[...] = jnp.zeros_like(dgq_ref)
            dgk_ref[...] = jnp.zeros_like(dgk_ref)

        bd = _head_blockdiag()
        cs, sn = cos_ref[...], sin_ref[...]
        isa = _is_a((tr, LANES))

        def back(xv, dy, g):
            rstd = lax.rsqrt(_dot_split(xv * xv, bd, 2) * (1.0 / HEAD) + EPS)
            xhat = xv * rstd
            dxn = dy * cs + _swap_half(dy * sn)
            dxh = dxn * g
            mean = _dot_split(dxh * xhat, bd, 2) * (1.0 / HEAD)
            return rstd * (dxh - xhat * mean), jnp.sum(dxn * xhat, axis=0, keepdims=True)

        def fold(s):
            a = dk_ref[:, 2 * s * LANES:(2 * s + 1) * LANES]
            b = dk_ref[:, (2 * s + 1) * LANES:(2 * s + 2) * LANES]
            return jnp.where(isa, a + pltpu.roll(a, HEAD, axis=1), b + pltpu.roll(b, HEAD, axis=1))

        def foldv(s):
            a = dv_ref[:, 2 * s * LANES:(2 * s + 1) * LANES]
            b = dv_ref[:, (2 * s + 1) * LANES:(2 * s + 2) * LANES]
            return jnp.where(isa, a + pltpu.roll(a, HEAD, axis=1), b + pltpu.roll(b, HEAD, axis=1))

        dgq = jnp.zeros((1, LANES), F32)
        for s in range(nq):
            sl = slice(s * LANES, (s + 1) * LANES)
            dx, dg = back(x_ref[:, sl], dq_ref[:, sl], gq_ref[...])
            o_ref[:, sl] = dx.astype(BF16)
            dgq = dgq + dg
        dgq_ref[...] += dgq
        dgk = jnp.zeros((1, LANES), F32)
        for s in range(nk):
            sl = slice(D + s * LANES, D + (s + 1) * LANES)
            dx, dg = back(x_ref[:, sl], fold(s), gk_ref[...])
            o_ref[:, sl] = dx.astype(BF16)
            dgk = dgk + dg
            o_ref[:, D + (nk + s) * LANES:D + (nk + s + 1) * LANES] = foldv(s).astype(BF16)
        dgk_ref[...] += dgk

    tab = pl.BlockSpec((tr, LANES), lambda i: (i, 0))
    vec = pl.BlockSpec((1, LANES), lambda i: (0, 0))
    kv = pl.BlockSpec((tr, KW), lambda i: (i, 0))
    return pl.pallas_call(
        body, name="attn_prep_bwd", grid=(T // tr,),
        in_specs=[pl.BlockSpec((tr, QW), lambda i: (i, 0)), pl.BlockSpec((tr, D), lambda i: (i, 0)), kv, kv,
                  tab, tab, vec, vec],
        out_specs=[pl.BlockSpec((tr, QW), lambda i: (i, 0)), vec, vec],
        out_shape=[jax.ShapeDtypeStruct((T, QW), BF16), jax.ShapeDtypeStruct((1, LANES), F32),
                   jax.ShapeDtypeStruct((1, LANES), F32)],
        compiler_params=_params(("arbitrary",)),
    )(qkv, dq, dkd, dvd, cos, sin, gq, gk)


def _attn_probs(qs, kw, sink_ref, first, scale):
    rows = qs.shape[0]
    qi = lax.broadcasted_iota(jnp.int32, (rows, 2 * WINDOW), 0) % WINDOW
    kj = lax.broadcasted_iota(jnp.int32, (rows, 2 * WINDOW), 1)
    valid = (kj > qi) & (kj <= qi + WINDOW)
    if first is not False:
        valid = valid & jnp.logical_or(jnp.logical_not(first), kj >= WINDOW)
    isa = _is_a(kw.shape)
    out = []
    for pos in (0, 1):
        kp = jnp.where(isa if pos == 0 else ~isa, kw, jnp.zeros_like(kw))
        s = jnp.where(valid, _dot(qs, kp, _NT) * scale, -jnp.inf)
        sink = sink_ref[0, pos]
        m = jnp.maximum(jnp.max(s, axis=1, keepdims=True), sink)
        p = jnp.exp(s - m)
        ps = jnp.exp(sink - m)
        inv = 1.0 / (jnp.sum(p, axis=1, keepdims=True) + ps)
        out.append((p * inv, ps * inv, kp))
    return out


def _attn_specs(qb):
    q = pl.BlockSpec((qb * WINDOW, ATT_GW), lambda g, n: (n, g))
    cur = pl.BlockSpec((qb * WINDOW, LANES), lambda g, n: (n, g))
    prev = pl.BlockSpec((WINDOW, LANES), lambda g, n: (jnp.maximum(qb * n - 1, 0), g))
    sink = pl.BlockSpec((1, 2, ATT_GW, 1), lambda g, n: (g, 0, 0, 0))
    return q, cur, prev, sink


def _stack(ref, s):
    rows = slice(s * WINDOW, (s + 1) * WINDOW)
    return jnp.concatenate([ref[rows, i * LANES:(i + 1) * LANES] for i in range(ATT_GW // LANES)], axis=0)


def _attn_fwd(q, kd, vd, sinkcol, HKV, carry=None):
    T, D = q.shape
    nb = T // WINDOW
    qb = math.gcd(nb, ATT_STEP_BLOCKS)
    scale = HEAD ** -0.5

    def body(q_ref, kp_ref, kc_ref, vp_ref, vc_ref, sink_ref, o_ref):
        n = pl.program_id(1)
        kall = jnp.concatenate([kp_ref[...], kc_ref[...]], axis=0)
        vall = jnp.concatenate([vp_ref[...], vc_ref[...]], axis=0)
        isa = _is_a((2 * WINDOW, LANES))
        for s in range(qb):
            win = slice(s * WINDOW, (s + 2) * WINDOW)
            kw, vw = kall[win], vall[win]
            o = jnp.zeros((ATT_GW, LANES), F32)
            first = (n == 0) if s == 0 else False
            for pos, (probs, _, _) in enumerate(_attn_probs(_stack(q_ref, s), kw, sink_ref, first, scale)):
                vp = jnp.where(isa if pos == 0 else ~isa, vw, jnp.zeros_like(vw))
                o = o + _dot(probs.astype(BF16), vp)
            for i in range(ATT_GW // LANES):
                o_ref[s * WINDOW:(s + 1) * WINDOW, i * LANES:(i + 1) * LANES] = o[i * WINDOW:(i + 1) * WINDOW].astype(BF16)

    qs_, cur, prev, sink = _attn_specs(qb)
    res = _call(
        body, name="attn_fwd", grid=(HKV, nb // qb), in_specs=[qs_, prev, cur, prev, cur, sink], out_specs=[qs_],
        out_shape=[jax.ShapeDtypeStruct((T, D), BF16)], sem=("parallel", "parallel"),
        args=(q, kd, kd, vd, vd, sinkcol), carry=carry)
    return res[0] if carry is None else (res[0], res[1:])


def _attn_bwd(q, kd, vd, o, do, sinkcol, HKV, carry=None):
    T, D = q.shape
    nb = T // WINDOW
    qb = math.gcd(nb, ATT_STEP_BLOCKS)
    scale = HEAD ** -0.5
    KW = HKV * LANES

    def body(q_ref, kp_ref, kc_ref, vp_ref, vc_ref, o_ref, do_ref, sink_ref, dq_ref, dk_ref, dv_ref, ds_ref):
        n = pl.program_id(1)

        @pl.when(n == 0)
        def _():
            dk_ref[...] = jnp.zeros_like(dk_ref)
            dv_ref[...] = jnp.zeros_like(dv_ref)
            ds_ref[...] = jnp.zeros_like(ds_ref)

        kall = jnp.concatenate([kp_ref[...], kc_ref[...]], axis=0)
        vall = jnp.concatenate([vp_ref[...], vc_ref[...]], axis=0)
        isa_w = _is_a((2 * WINDOW, LANES))
        isa_q = _is_a((ATT_GW, LANES))
        for s in range(qb):
            win = slice(s * WINDOW, (s + 2) * WINDOW)
            kw, vw = kall[win], vall[win]
            qs = _stack(q_ref, s)
            dos = _stack(do_ref, s)
            dd = dos * _stack(o_ref, s).astype(F32)
            dob = dos.astype(BF16)
            dqs = jnp.zeros((ATT_GW, LANES), F32)
            dkw, dvw, dsk = [], [], []
            first = (n == 0) if s == 0 else False
            for pos, (probs, psink, kp) in enumerate(_attn_probs(qs, kw, sink_ref, first, scale)):
                sel_w = isa_w if pos == 0 else ~isa_w
                sel_q = isa_q if pos == 0 else ~isa_q
                delta = jnp.sum(jnp.where(sel_q, dd, 0.0), axis=1, keepdims=True)
                vp = jnp.where(sel_w, vw, jnp.zeros_like(vw))
                dp = _dot(dob, vp, _NT)
                dsb = (probs * (dp - delta) * scale).astype(BF16)
                dqs = dqs + _dot(dsb, kp)
                dkw.append(_dot(dsb, qs, _TN))
                dvw.append(_dot(probs.astype(BF16), dob, _TN))
                dsk.append(-psink * delta)
            dkw = jnp.where(isa_w, dkw[0], dkw[1])
            dvw = jnp.where(isa_w, dvw[0], dvw[1])

            def add_window(dkw=dkw, dvw=dvw, s=s):
                start = pl.multiple_of((qb * n + s - 1) * WINDOW, WINDOW)
                dk_ref[pl.ds(start, 2 * WINDOW), :] += dkw
                dv_ref[pl.ds(start, 2 * WINDOW), :] += dvw

            if s == 0:
                @pl.when(n == 0)
                def _(dkw=dkw, dvw=dvw):
                    dk_ref[0:WINDOW, :] += dkw[WINDOW:]
                    dv_ref[0:WINDOW, :] += dvw[WINDOW:]

                pl.when(n > 0)(add_window)
            else:
                add_window()

            rows = []
            for i in range(ATT_GW // LANES):
                dq_ref[s * WINDOW:(s + 1) * WINDOW, i * LANES:(i + 1) * LANES] = dqs[i * WINDOW:(i + 1) * WINDOW]
                for pos in (0, 1):
                    t = jnp.sum(dsk[pos][i * WINDOW:(i + 1) * WINDOW], axis=0, keepdims=True)
                    rows.append(jnp.broadcast_to(t, (1, LANES)))
            ds_ref[0] += jnp.concatenate(rows, axis=0)

    qs_, cur, prev, sink = _attn_specs(qb)
    dqo = pl.BlockSpec((qb * WINDOW, ATT_GW), lambda g, n: (n, g))
    dkv = pl.BlockSpec((T, LANES), lambda g, n: (0, g))
    res = _call(
        body, name="attn_bwd", grid=(HKV, nb // qb),
        in_specs=[qs_, prev, cur, prev, cur, qs_, dqo, sink],
        out_specs=[dqo, dkv, dkv, pl.BlockSpec((1, ATT_GROUP, LANES), lambda g, n: (g, 0, 0))],
        out_shape=[jax.ShapeDtypeStruct((T, D), F32), jax.ShapeDtypeStruct((T, KW), F32),
                   jax.ShapeDtypeStruct((T, KW), F32), jax.ShapeDtypeStruct((HKV, ATT_GROUP, LANES), F32)],
        sem=("parallel", "arbitrary"), args=(q, kd, kd, vd, vd, o, do, sinkcol), carry=carry)
    return res if carry is None else (*res[:4], res[4:])


def _conv_fwd(zx, w, b, DI, CD):
    T = zx.shape[0]
    cw, tc = _tile(math.gcd(DI, CD), 512), 512
    off = DI // cw

    def body(cur_ref, halo_ref, w_ref, b_ref, o_ref):
        i = pl.program_id(1)
        halo = jnp.where(i > 0, halo_ref[...], 0.0)
        ext = jnp.concatenate([halo, cur_ref[...]], axis=0)
        acc = b_ref[...] + w_ref[SSM_CONV - 1:SSM_CONV, :] * ext[HALO:]
        for k in range(SSM_CONV - 1):
            acc = acc + w_ref[k:k + 1, :] * pltpu.roll(ext, SSM_CONV - 1 - k, axis=0)[HALO:]
        o_ref[...] = acc * jax.nn.sigmoid(acc)

    return pl.pallas_call(
        body, name="ssm_conv_fwd", grid=(CD // cw, T // tc),
        in_specs=[pl.BlockSpec((tc, cw), lambda j, i: (i, off + j)),
                  pl.BlockSpec((HALO, cw), lambda j, i: (jnp.maximum(i * (tc // HALO) - 1, 0), off + j)),
                  pl.BlockSpec((SSM_CONV, cw), lambda j, i: (0, j)), pl.BlockSpec((1, cw), lambda j, i: (0, j))],
        out_specs=pl.BlockSpec((tc, cw), lambda j, i: (i, j)),
        out_shape=jax.ShapeDtypeStruct((T, CD), F32),
        compiler_params=_params(("parallel", "parallel")),
    )(zx, zx, w, b)


def _conv_bwd(zx, dparts, dzx, w, b, DI, CD):
    T = zx.shape[0]
    cw, tc = _tile(math.gcd(DI, CD), 512), 512
    off = DI // cw
    nt = T // tc
    hb = tc // HALO
    ends = [0]
    for p in dparts:
        assert p.shape[1] % cw == 0
        ends.append(ends[-1] + p.shape[1] // cw)
    assert ends[-1] == CD // cw
    n_p = len(dparts)

    def body(*refs):
        cur_ref, prev_ref, next_ref = refs[:3]
        d_refs, dn_refs = refs[3:3 + n_p], refs[3 + n_p:3 + 2 * n_p]
        w_ref, b_ref, _, o_ref, dw_ref, db_ref = refs[3 + 2 * n_p:]
        j, i = pl.program_id(0), pl.program_id(1)

        @pl.when(i == 0)
        def _():
            dw_ref[...] = jnp.zeros_like(dw_ref)
            db_ref[...] = jnp.zeros_like(db_ref)

        def pick(prefs):
            v = prefs[n_p - 1][...]
            for p in range(n_p - 2, -1, -1):
                v = jnp.where(j < ends[p + 1], prefs[p][...], v)
            return v

        prev = jnp.where(i > 0, prev_ref[...], 0.0)
        ext = jnp.concatenate([prev, cur_ref[...], next_ref[...]], axis=0)
        u = b_ref[...] + w_ref[SSM_CONV - 1:SSM_CONV, :] * ext
        for k in range(SSM_CONV - 1):
            u = u + w_ref[k:k + 1, :] * pltpu.roll(ext, SSM_CONV - 1 - k, axis=0)
        u = u[HALO:]
        dnext = jnp.where(i < nt - 1, pick(dn_refs), 0.0)
        dxe = jnp.concatenate([pick(d_refs), dnext], axis=0)
        sg = jax.nn.sigmoid(u)
        du = dxe * sg * (1.0 + u * (1.0 - sg))
        n_e = tc + HALO
        dx = w_ref[SSM_CONV - 1:SSM_CONV, :] * du
        for k in range(SSM_CONV - 1):
            dx = dx + w_ref[k:k + 1, :] * pltpu.roll(du, n_e - (SSM_CONV - 1 - k), axis=0)
        o_ref[...] = dx[:tc].astype(BF16)
        duc = du[:tc]
        db_ref[...] += jnp.sum(duc, axis=0, keepdims=True)
        xs = ext[:n_e]
        dws = []
        for k in range(SSM_CONV):
            sh = xs if k == SSM_CONV - 1 else pltpu.roll(xs, SSM_CONV - 1 - k, axis=0)
            dws.append(jnp.sum(duc * sh[HALO:], axis=0, keepdims=True))
        dw_ref[...] += jnp.concatenate(dws, axis=0)

    def part_specs(p):
        lo, n = ends[p], ends[p + 1] - ends[p]

        def inside(j):
            return jnp.logical_and(j >= lo, j < lo + n)

        col = lambda j: jnp.clip(j - lo, 0, n - 1)
        return (pl.BlockSpec((tc, cw), lambda j, i: (jnp.where(inside(j), i, 0), col(j))),
                pl.BlockSpec((HALO, cw), lambda j, i: (jnp.where(inside(j), jnp.minimum((i + 1) * hb, nt * hb - 1), 0), col(j))))

    specs = [part_specs(p) for p in range(n_p)]
    return pl.pallas_call(
        body, name="ssm_conv_bwd", grid=(CD // cw, nt),
        in_specs=[pl.BlockSpec((tc, cw), lambda j, i: (i, off + j)),
                  pl.BlockSpec((HALO, cw), lambda j, i: (jnp.maximum(i * hb - 1, 0), off + j)),
                  pl.BlockSpec((HALO, cw), lambda j, i: (jnp.minimum((i + 1) * hb, nt * hb - 1), off + j))]
        + [s[0] for s in specs] + [s[1] for s in specs]
        + [pl.BlockSpec((SSM_CONV, cw), lambda j, i: (0, j)), pl.BlockSpec((1, cw), lambda j, i: (0, j)),
           pl.BlockSpec(memory_space=pl.ANY)],
        out_specs=[pl.BlockSpec((tc, cw), lambda j, i: (i, off + j)), pl.BlockSpec((SSM_CONV, cw), lambda j, i: (0, j)),
                   pl.BlockSpec((1, cw), lambda j, i: (0, j))],
        out_shape=[jax.ShapeDtypeStruct(dzx.shape, BF16), jax.ShapeDtypeStruct((SSM_CONV, CD), F32),
                   jax.ShapeDtypeStruct((1, CD), F32)],
        input_output_aliases={5 + 2 * n_p: 0},
        compiler_params=_params(("parallel", "arbitrary")),
    )(zx, zx, zx, *dparts, *dparts, w, b, dzx)


def _tri_dot(v, upper):
    L = v.shape[0]
    r = lax.broadcasted_iota(jnp.int32, (L, L), 0)
    c = lax.broadcasted_iota(jnp.int32, (L, L), 1)
    tri = ((r <= c) if upper else (r >= c)).astype(BF16)
    p = _split3(v)
    return _dot(tri, p[0]) + _dot(tri, p[1]) + _dot(tri, p[2])


def _ssd_time2(dtraw_ref, bias_ref, alog_ref, sel):
    dt = jax.nn.softplus(dtraw_ref[...] + bias_ref[...])
    acum = _tri_dot(dt * (-jnp.exp(alog_ref[...])), False)
    return dt, _dot_split(dt, sel, 3), _dot_split(acum, sel, 3)


def _decay(acs, acs_t, pos):
    L = acs.shape[0]
    r = lax.broadcasted_iota(jnp.int32, (L, L), 0)
    c = lax.broadcasted_iota(jnp.int32, (L, L), 1)
    col = acs[:, HEAD * pos:HEAD * pos + 1]
    row = acs_t[HEAD * pos:HEAD * pos + 1, :]
    return jnp.exp(jnp.where(r >= c, col - row, -jnp.inf))


def _ssd_specs(G, GW, DI, ZW):
    L = SSM_CHUNK
    grp = lambda f: pl.BlockSpec((L, GW), lambda g, c: (f(c), g))
    return dict(
        grp=grp,
        bmat=lambda f: pl.BlockSpec((L, SSM_STATE), lambda g, c: (f(c), DI // SSM_STATE + g)),
        cmat=lambda f: pl.BlockSpec((L, SSM_STATE), lambda g, c: (f(c), DI // SSM_STATE + G + g)),
        dtraw=lambda f: pl.BlockSpec((L, LANES), lambda g, c: (f(c), (2 * DI + 2 * G * SSM_STATE) // LANES)),
        vec=pl.BlockSpec((1, LANES), lambda g, c: (0, 0)),
        gvec=pl.BlockSpec((1, GW), lambda g, c: (0, g)),
        sel=pl.BlockSpec((1, LANES, GW), lambda g, c: (g, 0, 0)),
    )


def _ssd_fwd(zx, xc, bias, alog, sel, dskip, ng, DI, carry=None):
    T, ZW = zx.shape
    G, L = SSM_GROUPS, SSM_CHUNK
    GW = DI // G
    NS = GW // LANES
    nc = T // L
    sp = _ssd_specs(G, GW, DI, ZW)
    ident = lambda c: c

    def body(x_ref, b_ref, c_ref, z_ref, dtraw_ref, bias_ref, alog_ref, sel_ref, d_ref, ng_ref,
             y_ref, yo_ref, st_ref, state):
        c = pl.program_id(1)

        @pl.when(c == 0)
        def _():
            state[...] = jnp.zeros_like(state)

        x = x_ref[...]
        bb, cb_ = b_ref[...].astype(BF16), c_ref[...].astype(BF16)
        cbm = _dot(cb_, bb, _NT)
        _, dtx, acx = _ssd_time2(dtraw_ref, bias_ref, alog_ref, sel_ref[0])
        xdt = x * dtx
        ex = jnp.exp(acx)
        last = acx[L - 1:L, :]
        te = jnp.exp(last - acx)
        dlast = jnp.exp(last)
        isa = _is_a((L, LANES))
        for i in range(NS):
            sl = slice(i * LANES, (i + 1) * LANES)
            acs = acx[:, sl]
            acs_t = acs.T
            xs = xdt[:, sl]
            y = jnp.zeros((L, LANES), F32)
            for pos in (0, 1):
                m = (cbm * _decay(acs, acs_t, pos)).astype(BF16)
                y = y + _dot(m, jnp.where(isa if pos == 0 else ~isa, xs, 0.0).astype(BF16))
            st = state[i]
            st_ref[0, i] = st
            y = y + _dot(cb_, st.astype(BF16)) * ex[:, sl]
            state[i] = st * dlast[:, sl] + _dot(bb, (xs * te[:, sl]).astype(BF16), _TN)
            y_ref[:, sl] = y + d_ref[:, sl] * x[:, sl]
        z = z_ref[...]
        gated = y_ref[...] * (z * jax.nn.sigmoid(z))
        rstd = lax.rsqrt(jnp.mean(gated * gated, axis=1, keepdims=True) + EPS)
        yo_ref[...] = (gated * rstd * ng_ref[...]).astype(BF16)

    res = _call(
        body, name="ssd_fwd", grid=(G, nc),
        in_specs=[sp["grp"](ident), sp["bmat"](ident), sp["cmat"](ident), sp["grp"](ident), sp["dtraw"](ident),
                  sp["vec"], sp["vec"], sp["sel"], sp["gvec"], sp["gvec"]],
        out_specs=[sp["grp"](ident), sp["grp"](ident),
                   pl.BlockSpec((1, NS, SSM_STATE, LANES), lambda g, c: (c, g, 0, 0))],
        out_shape=[jax.ShapeDtypeStruct((T, DI), F32), jax.ShapeDtypeStruct((T, DI), BF16),
                   jax.ShapeDtypeStruct((nc, G * NS, SSM_STATE, LANES), F32)],
        scratch_shapes=[pltpu.VMEM((NS, SSM_STATE, LANES), F32)],
        sem=("parallel", "arbitrary"), args=(xc, xc, xc, zx, zx, bias, alog, sel, dskip, ng), carry=carry)
    return res if carry is None else (*res[:3], res[3:])


def _ssd_bwd(zx, xc, yssd, dyo, states, bias, alog, sel, dskip, ng, DI, carry=None):
    T, ZW = zx.shape
    G, L = SSM_GROUPS, SSM_CHUNK
    GW = DI // G
    NS = GW // LANES
    nc = T // L
    sp = _ssd_specs(G, GW, DI, ZW)
    rev = lambda c: nc - 1 - c

    def body(x_ref, b_ref, c_ref, z_ref, dtraw_ref, y_ref, dyo_ref, st_ref, bias_ref, alog_ref, sel_ref,
             d_ref, ng_ref, dz_ref, dx_ref, db_ref, dc_ref, ddt_ref, dac_ref, dd_ref, dng_ref, dstate):
        c = pl.program_id(1)

        @pl.when(c == 0)
        def _():
            dstate[...] = jnp.zeros_like(dstate)
            dd_ref[...] = jnp.zeros_like(dd_ref)
            dng_ref[...] = jnp.zeros_like(dng_ref)

        z, ys, dyo = z_ref[...], y_ref[...], dyo_ref[...]
        sg = jax.nn.sigmoid(z)
        sz = z * sg
        gated = ys * sz
        rstd = lax.rsqrt(jnp.mean(gated * gated, axis=1, keepdims=True) + EPS)
        yn = gated * rstd
        dng_ref[0] += jnp.sum(dyo * yn, axis=0, keepdims=True)
        dyn = dyo * ng_ref[...]
        dgated = rstd * (dyn - yn * jnp.mean(dyn * yn, axis=1, keepdims=True))
        g = dgated * sz
        dz_ref[...] = (dgated * ys * sg * (1.0 + z * (1.0 - sg))).astype(BF16)

        x = x_ref[...]
        dsk = d_ref[...]
        dd_ref[0] += jnp.sum(g * x, axis=0, keepdims=True)
        bb, cb_ = b_ref[...].astype(BF16), c_ref[...].astype(BF16)
        cbm = _dot(cb_, bb, _NT)
        _, dtx, acx = _ssd_time2(dtraw_ref, bias_ref, alog_ref, sel_ref[0])
        xdt = x * dtx
        ex = jnp.exp(acx)
        last = acx[L - 1:L, :]
        te = jnp.exp(last - acx)
        dlast = jnp.exp(last)
        isa = _is_a((L, LANES))
        is_last = lax.broadcasted_iota(jnp.int32, (L, LANES), 0) == L - 1
        strict = lax.broadcasted_iota(jnp.int32, (L, L), 0) > lax.broadcasted_iota(jnp.int32, (L, L), 1)
        lane_id = lax.broadcasted_iota(jnp.int32, (1, LANES), 1)
        row_id = lax.broadcasted_iota(jnp.int32, (8, 1), 0)
        lane_head = lax.broadcasted_iota(jnp.int32, (LANES, LANES), 0) // HEAD
        col_id = lax.broadcasted_iota(jnp.int32, (LANES, LANES), 1)
        dcb = jnp.zeros((L, L), F32)
        dcm = jnp.zeros((L, SSM_STATE), F32)
        dbm = jnp.zeros((L, SSM_STATE), F32)
        q_rows = jnp.zeros((L, LANES), F32)
        q_cols = jnp.zeros((8, L), F32)
        for i in range(NS):
            sl = slice(i * LANES, (i + 1) * LANES)
            acs = acx[:, sl]
            acs_t = acs.T
            xs, gs = xdt[:, sl], g[:, sl]
            xsb = xs.astype(BF16)
            dxd = jnp.zeros((L, LANES), F32)
            for pos in (0, 1):
                gp = jnp.where(isa if pos == 0 else ~isa, gs, 0.0).astype(BF16)
                dec = _decay(acs, acs_t, pos)
                dxd = dxd + _dot((cbm * dec).astype(BF16), gp, _TN)
                dmd = _dot(gp, xsb, _NT) * dec
                dcb = dcb + dmd
                q = jnp.where(strict, dmd * cbm, 0.0)
                q_rows = q_rows + jnp.sum(q, axis=1, keepdims=True) * (lane_id == 2 * i + pos).astype(F32)
                q_cols = q_cols + jnp.sum(q, axis=0, keepdims=True) * (row_id == 2 * i + pos).astype(F32)
            st = st_ref[0, i]
            dst = dstate[i]
            stb, dstb = st.astype(BF16), dst.astype(BF16)
            eg = (ex[:, sl] * gs).astype(BF16)
            dcm = dcm + _dot(eg, stb, _NT)
            yoff = _dot(cb_, stb) * ex[:, sl]
            w = xs * te[:, sl]
            wb = w.astype(BF16)
            dw = _dot(bb, dstb)
            dbm = dbm + _dot(wb, dstb, _NT)
            dxt = dxd + dw * te[:, sl]
            dal = dlast[:, sl] * jnp.sum(dst * st, axis=0, keepdims=True) + jnp.sum(dw * w, axis=0, keepdims=True)
            dac_l = gs * yoff - w * dw + jnp.where(is_last, dal, 0.0)
            ddt_l = dxt * x[:, sl]
            dstate[i] = dst * dlast[:, sl] + _dot(cb_, eg, _TN)
            dx_ref[:, sl] = dxt * dtx[:, sl] + dsk[:, sl] * gs
            to_head = (col_id == 2 * i + lane_head).astype(BF16)
            part = _dot_split(ddt_l, to_head, 2)
            parta = _dot_split(dac_l, to_head, 2)
            if i == 0:
                ddt_ref[0] = part
                dac_ref[0] = parta
            else:
                ddt_ref[0] += part
                dac_ref[0] += parta
        dac_ref[0] += q_rows - jnp.concatenate([q_cols, jnp.zeros((LANES - 8, L), F32)], axis=0).T
        dcbb = dcb.astype(BF16)
        dc_ref[...] = dcm + _dot(dcbb, bb)
        db_ref[...] = dbm + _dot(dcbb, cb_, _TN)

    part_spec = pl.BlockSpec((1, L, LANES), lambda g, c: (g, rev(c), 0))
    lane_spec = pl.BlockSpec((1, 1, GW), lambda g, c: (g, 0, 0))
    bc_out = pl.BlockSpec((L, SSM_STATE), lambda g, c: (rev(c), g))
    res = _call(
        body, name="ssd_bwd", grid=(G, nc),
        in_specs=[sp["grp"](rev), sp["bmat"](rev), sp["cmat"](rev), sp["grp"](rev), sp["dtraw"](rev), sp["grp"](rev),
                  sp["grp"](rev), pl.BlockSpec((1, NS, SSM_STATE, LANES), lambda g, c: (rev(c), g, 0, 0)),
                  sp["vec"], sp["vec"], sp["sel"], sp["gvec"], sp["gvec"]],
        out_specs=[sp["grp"](rev), sp["grp"](rev), bc_out, bc_out, part_spec, part_spec, lane_spec, lane_spec],
        out_shape=[jax.ShapeDtypeStruct((T, ZW), BF16), jax.ShapeDtypeStruct((T, DI), F32),
                   jax.ShapeDtypeStruct((T, G * SSM_STATE), F32), jax.ShapeDtypeStruct((T, G * SSM_STATE), F32),
                   jax.ShapeDtypeStruct((G, T, LANES), F32), jax.ShapeDtypeStruct((G, T, LANES), F32),
                   jax.ShapeDtypeStruct((G, 1, GW), F32), jax.ShapeDtypeStruct((G, 1, GW), F32)],
        scratch_shapes=[pltpu.VMEM((NS, SSM_STATE, LANES), F32)], sem=("parallel", "arbitrary"),
        args=(xc, xc, xc, zx, zx, yssd, dyo, states, bias, alog, sel, dskip, ng), carry=carry)
    return res if carry is None else (*res[:8], res[8:])


def _ssd_dt_bwd(zx, ddt_part, dac_part, dzx, bias, alog, DI):
    T, ZW = zx.shape
    G, L = SSM_GROUPS, SSM_CHUNK
    nc = T // L
    heads = DI // HEAD // G
    tail = ZW - 2 * DI - 2 * G * SSM_STATE
    dt_block = (ZW - tail) // LANES

    def body(dtraw_ref, ddt_ref, dac_ref, bias_ref, alog_ref, _, o_ref, dal_ref, dbias_ref):
        @pl.when(pl.program_id(0) == 0)
        def _():
            dal_ref[...] = jnp.zeros_like(dal_ref)
            dbias_ref[...] = jnp.zeros_like(dbias_ref)

        raw = dtraw_ref[...] + bias_ref[...]
        dt = jax.nn.softplus(raw)
        a = -jnp.exp(alog_ref[...])
        dac, ddt = dac_ref[0], ddt_ref[0]
        for gi in range(1, G):
            dac = dac + pltpu.roll(dac_ref[gi], gi * heads, axis=1)
            ddt = ddt + pltpu.roll(ddt_ref[gi], gi * heads, axis=1)
        dda = _tri_dot(dac, True)
        dal_ref[...] += jnp.sum(dda * dt, axis=0, keepdims=True) * a
        draw = (dda * a + ddt) * jax.nn.sigmoid(raw)
        dbias_ref[...] += jnp.sum(draw, axis=0, keepdims=True)
        o_ref[...] = jnp.concatenate([draw.astype(BF16), jnp.zeros((L, tail - LANES), BF16)], axis=1)

    vec = pl.BlockSpec((1, LANES), lambda c: (0, 0))
    part = pl.BlockSpec((G, L, LANES), lambda c: (0, c, 0))
    return pl.pallas_call(
        body, name="ssd_dt_bwd", grid=(nc,),
        in_specs=[pl.BlockSpec((L, LANES), lambda c: (c, dt_block)), part, part, vec, vec,
                  pl.BlockSpec(memory_space=pl.ANY)],
        out_specs=[pl.BlockSpec((L, tail), lambda c: (c, (ZW - tail) // tail)), vec, vec],
        out_shape=[jax.ShapeDtypeStruct((T, ZW), BF16), jax.ShapeDtypeStruct((1, LANES), F32),
                   jax.ShapeDtypeStruct((1, LANES), F32)],
        input_output_aliases={5: 0},
        compiler_params=_params(("arbitrary",)),
    )(zx, ddt_part, dac_part, bias, alog, dzx)


def _all_gather(shards, name):
    n = len(shards)

    def body(*refs):
        for phase in _gather_phases(refs[:n], refs[n:2 * n], *refs[2 * n:]):
            phase()

    any_spec = pl.BlockSpec(memory_space=pl.ANY)
    return pl.pallas_call(
        body, name=name, in_specs=[any_spec] * n, out_specs=[any_spec] * n,
        out_shape=[jax.ShapeDtypeStruct((N_DEV,) + s.shape, s.dtype) for s in shards],
        scratch_shapes=[pltpu.SemaphoreType.DMA((n, 7)), pltpu.SemaphoreType.DMA((n, 7)),
                        pltpu.SemaphoreType.DMA((n,))],
    )(*shards)


def _exchange(blocks, name):
    n = len(blocks)

    def body(*refs):
        local, sends, arrivals = _direct_copies(refs[:n], refs[n:2 * n], *refs[2 * n:])
        for cp in local + sends:
            cp.start()
        for cp in arrivals:
            cp.wait_recv()
        for cp in sends:
            cp.wait_send()
        for cp in local:
            cp.wait()

    any_spec = pl.BlockSpec(memory_space=pl.ANY)
    return pl.pallas_call(
        body, name=name, in_specs=[any_spec] * n, out_specs=[any_spec] * n,
        out_shape=[jax.ShapeDtypeStruct(b.shape, b.dtype) for b in blocks],
        scratch_shapes=[pltpu.SemaphoreType.DMA((n, 7)), pltpu.SemaphoreType.DMA((n, 7)),
                        pltpu.SemaphoreType.DMA((n,))],
    )(*blocks)


def _adamw(parts, w, m, v, name):
    nl = len(parts)
    R, C = parts[0].shape[1:]
    per_row = C * (N_DEV * nl * parts[0].dtype.itemsize + 7 * 4) * 2
    tr = R
    if R % 8 == 0:
        tr = 8
        for t in (16, 32, 64, 128, 256, 512):
            if R % t == 0 and t * per_row <= 24 * 1024 * 1024:
                tr = t
    nr = R // tr
    c1 = 1.0 - ADAM_B1 ** ADAM_STEP
    c2 = 1.0 - ADAM_B2 ** ADAM_STEP

    def body(*refs):
        p_refs = refs[:nl]
        w_ref, m_ref, v_ref, g_ref, d_ref, nm_ref, nv_ref = refs[nl:]
        for layer in range(nl):
            @pl.when(pl.program_id(0) == layer)
            def _(p_ref=p_refs[layer]):
                g = p_ref[0].astype(F32)
                for k in range(1, N_DEV):
                    g = g + p_ref[k].astype(F32)
                nm = ADAM_B1 * m_ref[...] + (1.0 - ADAM_B1) * g
                nv = ADAM_B2 * v_ref[...] + (1.0 - ADAM_B2) * (g * g)
                g_ref[...] = g
                nm_ref[...] = nm
                nv_ref[...] = nv
                d_ref[...] = -ADAM_LR * ((nm / c1) / (jnp.sqrt(nv / c2) + ADAM_EPS) + ADAM_WD * w_ref[...])

    def part_spec(layer):
        return pl.BlockSpec((N_DEV, tr, C), lambda l, i: (0, jnp.where(l == layer, i, jnp.where(l < layer, 0, nr - 1)), 0))

    blk = pl.BlockSpec((tr, C), lambda l, i: (l * nr + i, 0))
    out = jax.ShapeDtypeStruct((nl * R, C), F32)
    return pl.pallas_call(
        body, name=name, grid=(nl, nr),
        in_specs=[part_spec(layer) for layer in range(nl)] + [blk, blk, blk],
        out_specs=[blk, blk, blk, blk], out_shape=[out, out, out, out],
        compiler_params=_params(("arbitrary", "arbitrary")),
    )(*parts, w, m, v)


def _pad_cols(a, n):
    return jnp.pad(a, ((0, 0), (0, n - a.shape[1])))


def kernel(x, positions, mixer_norm, ffn_norm, attn_w_qkv, attn_q_norm, attn_k_norm, attn_sinks, attn_w_o, ssm_w_in, ssm_conv_w, ssm_conv_b, ssm_dt_bias, ssm_a_log, ssm_d, ssm_norm, ssm_w_out, ffn_w_gate, ffn_w_up, ffn_w_down, loss_target, m_mixer_norm, m_ffn_norm, m_attn_w_qkv, m_attn_q_norm, m_attn_k_norm, m_attn_sinks, m_attn_w_o, m_ssm_w_in, m_ssm_conv_w, m_ssm_conv_b, m_ssm_dt_bias, m_ssm_a_log, m_ssm_d, m_ssm_norm, m_ssm_w_out, m_ffn_w_gate, m_ffn_w_up, m_ffn_w_down, v_mixer_norm, v_ffn_norm, v_attn_w_qkv, v_attn_q_norm, v_attn_k_norm, v_attn_sinks, v_attn_w_o, v_ssm_w_in, v_ssm_conv_w, v_ssm_conv_b, v_ssm_dt_bias, v_ssm_a_log, v_ssm_d, v_ssm_norm, v_ssm_w_out, v_ffn_w_gate, v_ffn_w_up, v_ffn_w_down):
    T, D = x.shape[1], x.shape[2]
    HQ = D // HEAD
    HKV = HQ // ATT_GROUP
    QW = (HQ + 2 * HKV) * HEAD
    DI = 2 * D
    H = DI // HEAD
    G = SSM_GROUPS
    GW = DI // G
    CD = DI + 2 * G * SSM_STATE
    ZW = -(-(DI + CD + LANES) // 512) * 512
    IW = DI + CD + H
    assert T % 512 == 0 and D % 256 == 0 and HKV % 2 == 0 and GW % LANES == 0 and H <= LANES

    weights = dict(mixer_norm=mixer_norm, ffn_norm=ffn_norm, attn_w_qkv=attn_w_qkv, attn_q_norm=attn_q_norm,
                   attn_k_norm=attn_k_norm, attn_sinks=attn_sinks, attn_w_o=attn_w_o, ssm_w_in=ssm_w_in,
                   ssm_conv_w=ssm_conv_w, ssm_conv_b=ssm_conv_b, ssm_dt_bias=ssm_dt_bias, ssm_a_log=ssm_a_log,
                   ssm_d=ssm_d, ssm_norm=ssm_norm, ssm_w_out=ssm_w_out, ffn_w_gate=ffn_w_gate, ffn_w_up=ffn_w_up,
                   ffn_w_down=ffn_w_down)
    mom_m = dict(mixer_norm=m_mixer_norm, ffn_norm=m_ffn_norm, attn_w_qkv=m_attn_w_qkv, attn_q_norm=m_attn_q_norm,
                 attn_k_norm=m_attn_k_norm, attn_sinks=m_attn_sinks, attn_w_o=m_attn_w_o, ssm_w_in=m_ssm_w_in,
                 ssm_conv_w=m_ssm_conv_w, ssm_conv_b=m_ssm_conv_b, ssm_dt_bias=m_ssm_dt_bias, ssm_a_log=m_ssm_a_log,
                 ssm_d=m_ssm_d, ssm_norm=m_ssm_norm, ssm_w_out=m_ssm_w_out, ffn_w_gate=m_ffn_w_gate,
                 ffn_w_up=m_ffn_w_up, ffn_w_down=m_ffn_w_down)
    mom_v = dict(mixer_norm=v_mixer_norm, ffn_norm=v_ffn_norm, attn_w_qkv=v_attn_w_qkv, attn_q_norm=v_attn_q_norm,
                 attn_k_norm=v_attn_k_norm, attn_sinks=v_attn_sinks, attn_w_o=v_attn_w_o, ssm_w_in=v_ssm_w_in,
                 ssm_conv_w=v_ssm_conv_w, ssm_conv_b=v_ssm_conv_b, ssm_dt_bias=v_ssm_dt_bias, ssm_a_log=v_ssm_a_log,
                 ssm_d=v_ssm_d, ssm_norm=v_ssm_norm, ssm_w_out=v_ssm_w_out, ffn_w_gate=v_ffn_w_gate,
                 ffn_w_up=v_ffn_w_up, ffn_w_down=v_ffn_w_down)
    big = ["attn_w_qkv", "attn_w_o", "ssm_w_in", "ssm_w_out", "ffn_w_gate", "ffn_w_up", "ffn_w_down"]

    def flat2(a):
        return a.reshape(-1, a.shape[-1])

    def shard(n, layer=0):
        return weights[n][layer].astype(BF16)

    def from_cols(g):
        return g.transpose(1, 0, 2).reshape(g.shape[1], N_DEV * g.shape[2])

    def from_rows(g):
        return g.reshape(N_DEV * g.shape[1], g.shape[2])

    xs = x[0]
    tgt = loss_target[0]
    inv_freq = ROPE_THETA ** (-jnp.arange(0, HEAD, 2, dtype=F32) / HEAD)
    ang = positions[0].astype(F32)[:, None] * inv_freq
    cos = jnp.tile(jnp.cos(ang), (1, 4))
    sin = jnp.tile(jnp.concatenate([-jnp.sin(ang), jnp.sin(ang)], axis=1), (1, 2))
    gq = jnp.tile(attn_q_norm, (1, 2))
    gk = jnp.tile(attn_k_norm, (1, 2))
    sinkcol = jnp.repeat(attn_sinks.reshape(HKV, ATT_GROUP // 2, 2).transpose(0, 2, 1), WINDOW, axis=2)[..., None]
    bias_p = _pad_cols(ssm_dt_bias, LANES)
    alog_p = _pad_cols(ssm_a_log, LANES)
    dskip = jnp.repeat(ssm_d, HEAD, axis=1)
    lane_head = jnp.arange(DI) // HEAD
    sel = (jnp.arange(LANES)[None, :, None] == lane_head.reshape(G, 1, GW)).astype(BF16)
    vec_w = CD // N_DEV
    small = jnp.concatenate([ssm_conv_w[0], ssm_conv_b, _pad_cols(ssm_norm, vec_w),
                             jnp.zeros((2, vec_w), F32)], axis=0)
    g_qkv, g_o, small_all = _all_gather([shard("attn_w_qkv"), shard("attn_w_o"), small], "gather_first")
    w_qkv, w_o = from_cols(g_qkv), from_rows(g_o)
    conv_w = small_all[:, :SSM_CONV].transpose(1, 0, 2).reshape(SSM_CONV, CD)
    conv_b = small_all[:, SSM_CONV].reshape(1, CD)
    ng = small_all[:, SSM_CONV + 1, :DI // N_DEV].reshape(1, DI)

    def rows_to_blocks(p):
        return p.reshape(N_DEV, p.shape[0] // N_DEV, p.shape[1])

    def cols_to_blocks(p):
        return p.reshape(p.shape[0], N_DEV, p.shape[1] // N_DEV).transpose(1, 0, 2)

    hm0 = _rms_fwd(xs, mixer_norm[0:1], "rms_fwd_m0")
    qkv, got = _matmul(hm0, w_qkv, mode="nn", out_dtype=F32, name="mm_qkv", carry=[shard("ffn_w_gate", 0)])
    w_gate = [from_cols(got[0]), None]
    qr, kd, vd = _attn_prep_fwd(qkv, cos, sin, gq, gk, D, HKV)
    o, got = _attn_fwd(qr, kd, vd, sinkcol, HKV, carry=[shard("ffn_w_up", 0), shard("ffn_w_down", 0)])
    w_up = [from_cols(got[0]), None]
    w_down = [from_rows(got[1]), None]
    x1 = _matmul(o, w_o, mode="nn", out_dtype=F32, name="mm_attn_out", add=xs)
    hf0 = _rms_fwd(x1, ffn_norm[0:1], "rms_fwd_f0")
    gate0, up0, act0, got = _ffn_up(hf0, w_gate[0], w_up[0], "ffn_up_0", carry=[shard("ssm_w_in"), shard("ssm_w_out")])
    w_in = _pad_cols(from_cols(got[0]), ZW)
    w_out = from_rows(got[1])
    x2 = _matmul(act0, w_down[0], mode="nn", out_dtype=F32, name="mm_ffn_down_0", add=x1)
    hm1 = _rms_fwd(x2, mixer_norm[1:2], "rms_fwd_m1")
    zx, got = _matmul(hm1, w_in, mode="nn", out_dtype=F32, name="mm_ssm_in",
                      carry=[shard("ffn_w_gate", 1), shard("ffn_w_up", 1)])
    w_gate[1], w_up[1] = from_cols(got[0]), from_cols(got[1])
    xc = _conv_fwd(zx, conv_w, conv_b, DI, CD)
    yssd, yout, states, got = _ssd_fwd(zx, xc, bias_p, alog_p, sel, dskip, ng, DI, carry=[shard("ffn_w_down", 1)])
    w_down[1] = from_rows(got[0])
    x3 = _matmul(yout, w_out, mode="nn", out_dtype=F32, name="mm_ssm_out", add=x2)
    hf1 = _rms_fwd(x3, ffn_norm[1:2], "rms_fwd_f1")
    gate1, up1, act1 = _ffn_up(hf1, w_gate[1], w_up[1], "ffn_up_1")
    x4 = _matmul(act1, w_down[1], mode="nn", out_dtype=F32, name="mm_ffn_down_1", add=x3)
    sq, dx4, dx4b = _loss_head(x4, tgt)
    loss = lax.psum(sq[0, 0] * (0.5 / D), ("x", "y", "c"))

    def ffn_bwd(dy, dyb, hf, gate, up, act, layer, xin, gain):
        dg, du = _ffn_dact(dyb, w_down[layer], gate, up, f"ffn_dact_{layer}")
        g_down = _matmul(act, dyb, mode="tn", out_dtype=BF16, name=f"mm_dw_down_{layer}")
        g_gate = _matmul(hf, dg, mode="tn", out_dtype=BF16, name=f"mm_dw_gate_{layer}")
        g_up = _matmul(hf, du, mode="tn", out_dtype=BF16, name=f"mm_dw_up_{layer}")
        dh = _matmul(dg, w_gate[layer], mode="nt", out_dtype=F32, name=f"mm_dh_gate_{layer}")
        dh = _matmul(du, w_up[layer], mode="nt", out_dtype=F32, name=f"mm_dh_up_{layer}", add=dh)
        dx, dxb, dgain = _rms_bwd(xin, gain, dh, dy, f"rms_bwd_f{layer}")
        gate_blocks = cols_to_blocks(g_gate)
        half = gate_blocks.shape[1] // 2
        return dx, dxb, dgain, [gate_blocks[:, :half], gate_blocks[:, half:], cols_to_blocks(g_up), rows_to_blocks(g_down)]

    dx3, dx3b, d_fn1, ffn1_blocks = ffn_bwd(dx4, dx4b, hf1, gate1, up1, act1, 1, x3, ffn_norm[1:2])
    dyo = _matmul(dx3b, w_out, mode="nt", out_dtype=F32, name="mm_dyout")
    g_wout = _matmul(yout, dx3b, mode="tn", out_dtype=BF16, name="mm_dw_ssm_out")
    dzx, dxx, dbm, dcm, ddt_p, dac_p, dd_l, dng_l, got1 = _ssd_bwd(
        zx, xc, yssd, dyo, states, bias_p, alog_p, sel, dskip, ng, DI,
        carry=ffn1_blocks + [rows_to_blocks(g_wout)])
    dzx, d_alog, d_bias = _ssd_dt_bwd(zx, ddt_p, dac_p, dzx, bias_p, alog_p, DI)
    dzx, d_convw, d_convb = _conv_bwd(zx, [dxx, dbm, dcm], dzx, conv_w, conv_b, DI, CD)
    g_win = _matmul(hm1, dzx, mode="tn", out_dtype=BF16, name="mm_dw_ssm_in")[:, :IW]
    dh, got2 = _matmul(dzx, w_in, mode="nt", out_dtype=F32, name="mm_dh_ssm_in", carry=[cols_to_blocks(g_win)])
    dx2, dx2b, d_mn1 = _rms_bwd(x2, mixer_norm[1:2], dh, dx3, "rms_bwd_m1")
    dx1, dx1b, d_fn0, ffn0_blocks = ffn_bwd(dx2, dx2b, hf0, gate0, up0, act0, 0, x1, ffn_norm[0:1])
    do, got_a = _matmul(dx1b, w_o, mode="nt", out_dtype=F32, name="mm_do", carry=ffn0_blocks[0:1])
    g_wo, got_b = _matmul(o, dx1b, mode="tn", out_dtype=BF16, name="mm_dw_attn_out", carry=ffn0_blocks[1:2])
    dq, dkd, dvd, dsink, got3 = _attn_bwd(qr, kd, vd, o, do, sinkcol, HKV,
                                          carry=ffn0_blocks[2:] + [rows_to_blocks(g_wo)])
    dqkv, dgq_l, dgk_l = _attn_prep_bwd(qkv, dq, dkd, dvd, cos, sin, gq, gk, D, HKV)
    g_wqkv = _matmul(hm0, dqkv, mode="tn", out_dtype=BF16, name="mm_dw_qkv")
    dh, got_c = _matmul(dqkv, w_qkv, mode="nt", out_dtype=F32, name="mm_dh_qkv", carry=[cols_to_blocks(g_wqkv)])
    dx0, _, d_mn0 = _rms_bwd(xs, mixer_norm[0:1], dh, dx1, "rms_bwd_m0")

    d_ng = dng_l.reshape(1, DI)
    vec_send = jnp.concatenate([
        d_convw.reshape(SSM_CONV, N_DEV, vec_w).transpose(1, 0, 2), d_convb.reshape(1, N_DEV, vec_w).transpose(1, 0, 2),
        _pad_cols(d_ng.reshape(N_DEV, DI // N_DEV), vec_w)[:, None, :], jnp.zeros((N_DEV, 2, vec_w), F32)], axis=1)
    d_sinks = dsink[:, :, 0].reshape(1, HQ)
    d_gq = dgq_l[:, :HEAD] + dgq_l[:, HEAD:]
    d_gk = dgk_l[:, :HEAD] + dgk_l[:, HEAD:]
    d_dskip = dd_l.reshape(H, HEAD).sum(axis=1).reshape(1, H)
    rep_names = ["mixer_norm", "ffn_norm", "attn_q_norm", "attn_k_norm", "attn_sinks", "ssm_dt_bias", "ssm_a_log",
                 "ssm_d"]
    rep_grads = [jnp.concatenate([d_mn0, d_mn1], axis=0), jnp.concatenate([d_fn0, d_fn1], axis=0), d_gq, d_gk,
                 d_sinks, d_bias[:, :H], d_alog[:, :H], d_dskip]
    rep_sizes = [weights[n].size for n in rep_names]
    rep_len = -(-sum(rep_sizes) // (8 * LANES)) * 8 * LANES

    def pack(arrs):
        flat = jnp.concatenate([a.reshape(-1) for a in arrs])
        return jnp.pad(flat, (0, rep_len - flat.shape[0])).reshape(rep_len // LANES, LANES)

    rep_send = jnp.broadcast_to(pack(rep_grads)[None], (N_DEV, rep_len // LANES, LANES))
    got4 = _exchange([vec_send, rep_send], "exchange_last")
    parts_of = {
        "attn_w_qkv": [got_c[0]], "attn_w_o": [got3[2]], "ssm_w_in": [got2[0]], "ssm_w_out": [got1[4]],
        "ffn_w_gate": [got_a[0], got_b[0], got1[0], got1[1]], "ffn_w_up": [got3[0], got1[2]],
        "ffn_w_down": [got3[1], got1[3]],
    }

    out = {}
    for n in big:
        res = _adamw(parts_of[n], flat2(weights[n]), flat2(mom_m[n]), flat2(mom_v[n]), f"adamw_{n}")
        out[n] = [r.reshape(weights[n].shape) for r in res]

    def vec_block(d):
        return jnp.concatenate([d["ssm_conv_w"][0], d["ssm_conv_b"], _pad_cols(d["ssm_norm"], vec_w),
                                jnp.zeros((2, vec_w), F32)], axis=0)

    res = _adamw([got4[0]], vec_block(weights), vec_block(mom_m), vec_block(mom_v), "adamw_vectors")
    out["ssm_conv_w"] = [r[:SSM_CONV][None] for r in res]
    out["ssm_conv_b"] = [r[SSM_CONV:SSM_CONV + 1] for r in res]
    out["ssm_norm"] = [r[SSM_CONV + 1:SSM_CONV + 2, :DI // N_DEV] for r in res]
    res = _adamw([got4[1]], pack([weights[n] for n in rep_names]), pack([mom_m[n] for n in rep_names]),
                 pack([mom_v[n] for n in rep_names]), "adamw_replicated")
    offs = 0
    for n, sz in zip(rep_names, rep_sizes):
        out[n] = [r.reshape(-1)[offs:offs + sz].reshape(weights[n].shape) for r in res]
        offs += sz

    names = list(weights)
    return (loss, dx0[None], *[out[n][0] for n in names], *[out[n][1] for n in names],
            *[out[n][2] for n in names], *[out[n][3] for n in names])
```

```python
import functools
import math

import jax
import jax.numpy as jnp
from jax import lax
from jax.experimental import pallas as pl
from jax.experimental.pallas import tpu as pltpu

F32 = jnp.float32
BF16 = jnp.bfloat16

N_DEV = 8
EPS = 1e-6
LANES = 128
HEAD = 64
ATT_GROUP = 8
ATT_GW = ATT_GROUP * HEAD
WINDOW = 128
ATT_STEP_BLOCKS = 4
ROPE_THETA = 10000.0
SSM_GROUPS = 8
SSM_STATE = 128
SSM_CONV = 4
SSM_CHUNK = 256
HALO = 8
ADAM_LR, ADAM_B1, ADAM_B2, ADAM_EPS, ADAM_WD, ADAM_STEP = 0.001, 0.9, 0.999, 1e-08, 0.01, 10
VMEM_LIMIT = 56 * 1024 * 1024
MATMUL_VMEM = 44 * 1024 * 1024
MESH = pl.DeviceIdType.MESH

_NN = (((1,), (0,)), ((), ()))
_NT = (((1,), (1,)), ((), ()))
_TN = (((0,), (0,)), ((), ()))


def _dot(a, b, dims=_NN):
    return lax.dot_general(a, b, dims, preferred_element_type=F32)


def _tile(n, cap):
    if n % LANES:
        return n
    best = LANES
    for t in range(LANES, min(n, cap) + 1, LANES):
        if n % t == 0:
            best = t
    return best


def _params(sem):
    return pltpu.CompilerParams(dimension_semantics=sem, vmem_limit_bytes=VMEM_LIMIT)


def _slot(px, py, pc):
    return 4 * px + 2 * py + pc


def _direct_copies(srcs, dsts, send_sems, recv_sems, local_sems, with_arrivals=True):
    x, y, c = lax.axis_index("x"), lax.axis_index("y"), lax.axis_index("c")
    me = _slot(x, y, c)
    peers = [(x ^ (m >> 2), y ^ ((m >> 1) & 1), c ^ (m & 1)) for m in range(1, N_DEV)]
    local, sends, arrivals = [], [], []
    for w, (src, dst) in enumerate(zip(srcs, dsts)):
        sliced = src.shape == dst.shape
        local.append(pltpu.make_async_copy(src.at[me] if sliced else src, dst.at[me], local_sems.at[w]))
        for k, peer in enumerate(peers):
            sems = dict(send_sem=send_sems.at[w, k], recv_sem=recv_sems.at[w, k], device_id=peer, device_id_type=MESH)
            sends.append(pltpu.make_async_remote_copy(
                src_ref=src.at[_slot(*peer)] if sliced else src, dst_ref=dst.at[me], **sems))
            if with_arrivals:
                arrivals.append(pltpu.make_async_remote_copy(
                    src_ref=src.at[me] if sliced else src, dst_ref=dst.at[_slot(*peer)], **sems))
    return local, sends, arrivals


def _gather_phases(srcs, dsts, send_sems, recv_sems, local_sems):
    x, y, c = lax.axis_index("x"), lax.axis_index("y"), lax.axis_index("c")
    me, sibling = (x, y, c), (x, y, 1 - c)
    chips = [(1 - x, y), (x, 1 - y), (1 - x, 1 - y)]
    n = len(srcs)

    def copy(w, k, block, to, src=None):
        dst = dsts[w].at[_slot(*block)]
        return pltpu.make_async_remote_copy(
            src_ref=dst if src is None else src, dst_ref=dst, send_sem=send_sems.at[w, k],
            recv_sem=recv_sems.at[w, k], device_id=to, device_id_type=MESH)

    def first_sends(w):
        return [copy(w, 0, me, sibling, src=srcs[w])] + [copy(w, 1 + j, me, (*chip, c), src=srcs[w])
                                                         for j, chip in enumerate(chips)]

    def start():
        for w in range(n):
            pltpu.make_async_copy(srcs[w], dsts[w].at[_slot(*me)], local_sems.at[w]).start()
            for cp in first_sends(w):
                cp.start()

    def forward():
        for w in range(n):
            for j, chip in enumerate(chips):
                copy(w, 1 + j, (*chip, c), me).wait_recv()
                copy(w, 4 + j, (*chip, c), sibling).start()

    def finish():
        for w in range(n):
            copy(w, 0, sibling, me).wait_recv()
            for j, chip in enumerate(chips):
                copy(w, 4 + j, (*chip, 1 - c), me).wait_recv()
        for w in range(n):
            for cp in first_sends(w) + [copy(w, 4 + j, (*chip, c), sibling) for j, chip in enumerate(chips)]:
                cp.wait_send()
            pltpu.make_async_copy(srcs[w], dsts[w].at[_slot(*me)], local_sems.at[w]).wait()

    return start, forward, finish


def _call(body, *, name, grid, in_specs, out_specs, out_shape, sem, args, scratch_shapes=(), carry=None):
    if carry is None:
        return pl.pallas_call(body, name=name, grid=grid, in_specs=in_specs, out_specs=out_specs, out_shape=out_shape,
                              scratch_shapes=list(scratch_shapes), compiler_params=_params(sem))(*args)
    n_in, n_out, n_sc, n_c = len(in_specs), len(out_specs), len(scratch_shapes), len(carry)
    gather = all(a.ndim == 2 for a in carry)
    assert gather or all(a.ndim == 3 and a.shape[0] == N_DEV for a in carry)
    recv_shape = [jax.ShapeDtypeStruct((N_DEV,) + a.shape if gather else a.shape, a.dtype) for a in carry]
    n_steps = math.prod(grid)

    def wrapped(*refs):
        ins, c_in = refs[:n_in], refs[n_in:n_in + n_c]
        outs, c_out = refs[n_in + n_c:n_in + n_c + n_out], refs[n_in + n_c + n_out:n_in + 2 * n_c + n_out]
        scr = refs[n_in + 2 * n_c + n_out:n_in + 2 * n_c + n_out + n_sc]
        sems = refs[-3:]
        step = functools.reduce(lambda acc, d: acc * grid[d] + pl.program_id(d), range(len(grid)), 0)
        if gather:
            start, forward, finish = _gather_phases(c_in, c_out, *sems)
        else:
            def start():
                local, sends, _ = _direct_copies(c_in, c_out, *sems, with_arrivals=False)
                for cp in local + sends:
                    cp.start()

            def finish():
                local, sends, arrivals = _direct_copies(c_in, c_out, *sems)
                for cp in arrivals:
                    cp.wait_recv()
                for cp in sends:
                    cp.wait_send()
                for cp in local:
                    cp.wait()

        pl.when(step == 0)(start)
        if gather:
            pl.when(step == min((3 * n_steps) // 4, n_steps - 1))(forward)
        body(*ins, *outs, *scr)
        pl.when(step == n_steps - 1)(finish)

    any_spec = pl.BlockSpec(memory_space=pl.ANY)
    res = pl.pallas_call(
        wrapped, name=name, grid=grid, in_specs=list(in_specs) + [any_spec] * n_c,
        out_specs=list(out_specs) + [any_spec] * n_c, out_shape=list(out_shape) + recv_shape,
        scratch_shapes=list(scratch_shapes) + [pltpu.SemaphoreType.DMA((n_c, N_DEV - 1)),
                                               pltpu.SemaphoreType.DMA((n_c, N_DEV - 1)), pltpu.SemaphoreType.DMA((n_c,))],
        compiler_params=_params(("arbitrary",) * len(grid)),
    )(*args, *carry)
    return res


def _matmul(a, b, *, mode, out_dtype, name, add=None, carry=None):
    if mode == "nn":
        (M, K), N = a.shape, b.shape[1]
    elif mode == "nt":
        (M, K), N = a.shape, b.shape[0]
    else:
        (K, M), N = a.shape, b.shape[1]
    assert a.dtype == BF16 and b.dtype == BF16
    has_add = add is not None
    tn = _tile(N, 512)
    for tm in ((_tile(M, 512),) if mode == "tn" else (_tile(M, 2048), _tile(M, 1024))):
        fixed = 2 * tm * tn * (jnp.dtype(out_dtype).itemsize + (4 if has_add else 0)) + tm * tn * 4
        per_k = 2 * 2 * (tm + tn) + (2 * tm if mode == "tn" else 0)
        tk = _tile(K, max(LANES, (MATMUL_VMEM - fixed) // per_k))
        if tk == K:
            break
    nk = K // tk
    dims = _NT if mode == "nt" else _NN
    if mode == "tn":
        a_spec = pl.BlockSpec((tk, tm), lambda i, j, k: (jnp.where(j == 0, k, 0), i))
    else:
        a_spec = pl.BlockSpec((tm, tk), lambda i, j, k: (i, k))
    b_spec = pl.BlockSpec((tn, tk), lambda i, j, k: (j, k)) if mode == "nt" else pl.BlockSpec((tk, tn), lambda i, j, k: (k, j))
    o_spec = pl.BlockSpec((tm, tn), lambda i, j, k: (i, j))

    def body(*refs):
        a_ref, b_ref = refs[:2]
        add_ref = refs[2] if has_add else None
        o_ref = refs[2 + has_add]
        scratch = list(refs[3 + has_add:])
        at = scratch.pop(0) if mode == "tn" else None
        acc = scratch.pop(0) if nk > 1 else None
        j, k = pl.program_id(1), pl.program_id(2)
        if mode == "tn":
            @pl.when(j == 0)
            def _():
                at[k] = a_ref[...].T

            part = _dot(at[k], b_ref[...], dims)
        else:
            part = _dot(a_ref[...], b_ref[...], dims)

        def finish(r):
            if has_add:
                r = r + add_ref[...]
            o_ref[...] = r.astype(out_dtype)

        if nk == 1:
            finish(part)
        else:
            @pl.when(k == 0)
            def _():
                acc[...] = part

            @pl.when(jnp.logical_and(k > 0, k < nk - 1))
            def _():
                acc[...] += part

            @pl.when(k == nk - 1)
            def _():
                finish(acc[...] + part)

    scratch = ([pltpu.VMEM((nk, tm, tk), BF16)] if mode == "tn" else []) + ([pltpu.VMEM((tm, tn), F32)] if nk > 1 else [])
    res = _call(
        body, name=name, grid=(M // tm, N // tn, nk),
        in_specs=[a_spec, b_spec] + ([o_spec] if has_add else []),
        out_specs=[o_spec], out_shape=[jax.ShapeDtypeStruct((M, N), out_dtype)],
        scratch_shapes=scratch, sem=("parallel", "arbitrary", "arbitrary"),
        args=(a, b, add) if has_add else (a, b), carry=carry)
    return res[0] if carry is None else (res[0], res[1:])


def _rms_fwd(x, gain, name):
    T, D = x.shape
    tr = 256

    def body(x_ref, g_ref, h_ref):
        xv = x_ref[...]
        rstd = lax.rsqrt(jnp.mean(xv * xv, axis=1, keepdims=True) + EPS)
        h_ref[...] = (xv * rstd * g_ref[...]).astype(BF16)

    return pl.pallas_call(
        body, name=name, grid=(T // tr,),
        in_specs=[pl.BlockSpec((tr, D), lambda i: (i, 0)), pl.BlockSpec((1, D), lambda i: (0, 0))],
        out_specs=pl.BlockSpec((tr, D), lambda i: (i, 0)),
        out_shape=jax.ShapeDtypeStruct((T, D), BF16),
        compiler_params=_params(("parallel",)),
    )(x, gain)


def _rms_bwd(x, gain, dh, dres, name):
    T, D = x.shape
    tr = 256

    def body(x_ref, g_ref, dh_ref, dr_ref, dx_ref, dxb_ref, dg_ref):
        @pl.when(pl.program_id(0) == 0)
        def _():
            dg_ref[...] = jnp.zeros_like(dg_ref)

        xv = x_ref[...]
        rstd = lax.rsqrt(jnp.mean(xv * xv, axis=1, keepdims=True) + EPS)
        xhat = xv * rstd
        dy = dh_ref[...].astype(F32)
        dg_ref[...] += jnp.sum(dy * xhat, axis=0, keepdims=True)
        dxh = dy * g_ref[...]
        dx = dr_ref[...] + rstd * (dxh - xhat * jnp.mean(dxh * xhat, axis=1, keepdims=True))
        dx_ref[...] = dx
        dxb_ref[...] = dx.astype(BF16)

    row = pl.BlockSpec((tr, D), lambda i: (i, 0))
    vec = pl.BlockSpec((1, D), lambda i: (0, 0))
    return pl.pallas_call(
        body, name=name, grid=(T // tr,), in_specs=[row, vec, row, row], out_specs=[row, row, vec],
        out_shape=[jax.ShapeDtypeStruct((T, D), F32), jax.ShapeDtypeStruct((T, D), BF16),
                   jax.ShapeDtypeStruct((1, D), F32)],
        compiler_params=_params(("arbitrary",)),
    )(x, gain, dh, dres)


def _loss_head(y, target):
    T, D = y.shape
    tr = 256

    def body(y_ref, t_ref, s_ref, d_ref, db_ref):
        @pl.when(pl.program_id(0) == 0)
        def _():
            s_ref[...] = jnp.zeros_like(s_ref)

        e = y_ref[...] - t_ref[...]
        s_ref[...] += jnp.sum(jnp.sum(e * e, axis=1, keepdims=True), axis=0, keepdims=True)
        d = e * (1.0 / D)
        d_ref[...] = d
        db_ref[...] = d.astype(BF16)

    row = pl.BlockSpec((tr, D), lambda i: (i, 0))
    return pl.pallas_call(
        body, name="loss_head", grid=(T // tr,), in_specs=[row, row],
        out_specs=[pl.BlockSpec((1, 1), lambda i: (0, 0)), row, row],
        out_shape=[jax.ShapeDtypeStruct((1, 1), F32), jax.ShapeDtypeStruct((T, D), F32),
                   jax.ShapeDtypeStruct((T, D), BF16)],
        compiler_params=_params(("arbitrary",)),
    )(y, target)


def _ffn_up(h, wg, wu, name, carry=None):
    (T, D), Fd = h.shape, wg.shape[1]
    tm, tn = _tile(T, 2048), _tile(Fd, 512)

    def body(h_ref, wg_ref, wu_ref, g_ref, u_ref, a_ref):
        hv = h_ref[...]
        g = _dot(hv, wg_ref[...])
        g_ref[...] = g.astype(BF16)
        sg = g * jax.nn.sigmoid(g)
        u = _dot(hv, wu_ref[...])
        u_ref[...] = u.astype(BF16)
        a_ref[...] = (sg * u).astype(BF16)

    w_spec = pl.BlockSpec((D, tn), lambda i, j: (0, j))
    o_spec = pl.BlockSpec((tm, tn), lambda i, j: (i, j))
    res = _call(
        body, name=name, grid=(T // tm, Fd // tn),
        in_specs=[pl.BlockSpec((tm, D), lambda i, j: (i, 0)), w_spec, w_spec],
        out_specs=[o_spec, o_spec, o_spec],
        out_shape=[jax.ShapeDtypeStruct((T, Fd), BF16)] * 3,
        sem=("parallel", "arbitrary"), args=(h, wg, wu), carry=carry)
    return res if carry is None else (*res[:3], res[3:])


def _ffn_dact(dy, wd, gate, up, name):
    (T, D), Fd = dy.shape, wd.shape[0]
    tm, tn = _tile(T, 2048), _tile(Fd, 512)

    def body(dy_ref, wd_ref, g_ref, u_ref, dg_ref, du_ref):
        da = _dot(dy_ref[...], wd_ref[...], _NT)
        g = g_ref[...].astype(F32)
        sg = jax.nn.sigmoid(g)
        du_ref[...] = (da * g * sg).astype(BF16)
        dg_ref[...] = (da * u_ref[...].astype(F32) * sg * (1.0 + g * (1.0 - sg))).astype(BF16)

    o_spec = pl.BlockSpec((tm, tn), lambda i, j: (i, j))
    return pl.pallas_call(
        body, name=name, grid=(T // tm, Fd // tn),
        in_specs=[pl.BlockSpec((tm, D), lambda i, j: (i, 0)), pl.BlockSpec((tn, D), lambda i, j: (j, 0)),
                  o_spec, o_spec],
        out_specs=[o_spec, o_spec],
        out_shape=[jax.ShapeDtypeStruct((T, Fd), BF16), jax.ShapeDtypeStruct((T, Fd), BF16)],
        compiler_params=_params(("parallel", "arbitrary")),
    )(dy, wd, gate, up)


def _is_a(shape):
    return lax.broadcasted_iota(jnp.int32, shape, 1) % LANES < HEAD


def _split2(v):
    hi = v.astype(BF16)
    return hi, (v - hi.astype(F32)).astype(BF16)


def _split3(v):
    hi = v.astype(BF16)
    r = v - hi.astype(F32)
    mid = r.astype(BF16)
    return hi, mid, (r - mid.astype(F32)).astype(BF16)


def _dot_split(v, m, pieces, dims=_NN):
    parts = _split3(v) if pieces == 3 else _split2(v)
    out = _dot(parts[0], m, dims)
    for p in parts[1:]:
        out = out + _dot(p, m, dims)
    return out


def _head_blockdiag():
    r = lax.broadcasted_iota(jnp.int32, (LANES, LANES), 0) // HEAD
    c = lax.broadcasted_iota(jnp.int32, (LANES, LANES), 1) // HEAD
    return (r == c).astype(BF16)


def _swap_half(v):
    lane = lax.broadcasted_iota(jnp.int32, v.shape, 1)
    return jnp.where(lane % HEAD < HEAD // 2, pltpu.roll(v, LANES - HEAD // 2, axis=1), pltpu.roll(v, HEAD // 2, axis=1))


def _attn_prep_fwd(qkv, cos, sin, gq, gk, D, HKV):
    T, QW = qkv.shape
    tr = 256
    nq, nk = D // LANES, HKV // 2
    KW = HKV * LANES

    def body(x_ref, cos_ref, sin_ref, gq_ref, gk_ref, q_ref, k_ref, v_ref):
        bd = _head_blockdiag()
        cs, sn = cos_ref[...], sin_ref[...]
        isa = _is_a((tr, LANES))

        def normrope(xv, g):
            ms = _dot_split(xv * xv, bd, 2) * (1.0 / HEAD)
            xn = xv * lax.rsqrt(ms + EPS) * g
            return xn * cs + _swap_half(xn) * sn

        def dup(v):
            r = pltpu.roll(v, HEAD, axis=1)
            return jnp.where(isa, v, r), jnp.where(isa, r, v)

        for s in range(nq):
            sl = slice(s * LANES, (s + 1) * LANES)
            q_ref[:, sl] = normrope(x_ref[:, sl], gq_ref[...]).astype(BF16)
        for s in range(nk):
            ka, kb = dup(normrope(x_ref[:, D + s * LANES:D + (s + 1) * LANES], gk_ref[...]))
            k_ref[:, 2 * s * LANES:(2 * s + 1) * LANES] = ka.astype(BF16)
            k_ref[:, (2 * s + 1) * LANES:(2 * s + 2) * LANES] = kb.astype(BF16)
            va, vb = dup(x_ref[:, D + (nk + s) * LANES:D + (nk + s + 1) * LANES])
            v_ref[:, 2 * s * LANES:(2 * s + 1) * LANES] = va.astype(BF16)
            v_ref[:, (2 * s + 1) * LANES:(2 * s + 2) * LANES] = vb.astype(BF16)

    tab = pl.BlockSpec((tr, LANES), lambda i: (i, 0))
    vec = pl.BlockSpec((1, LANES), lambda i: (0, 0))
    return pl.pallas_call(
        body, name="attn_prep_fwd", grid=(T // tr,),
        in_specs=[pl.BlockSpec((tr, QW), lambda i: (i, 0)), tab, tab, vec, vec],
        out_specs=[pl.BlockSpec((tr, D), lambda i: (i, 0)), pl.BlockSpec((tr, KW), lambda i: (i, 0)),
                   pl.BlockSpec((tr, KW), lambda i: (i, 0))],
        out_shape=[jax.ShapeDtypeStruct((T, D), BF16), jax.ShapeDtypeStruct((T, KW), BF16),
                   jax.ShapeDtypeStruct((T, KW), BF16)],
        compiler_params=_params(("parallel",)),
    )(qkv, cos, sin, gq, gk)


def _attn_prep_bwd(qkv, dq, dkd, dvd, cos, sin, gq, gk, D, HKV):
    T, QW = qkv.shape
    tr = 256
    nq, nk = D // LANES, HKV // 2
    KW = HKV * LANES

    def body(x_ref, dq_ref, dk_ref, dv_ref, cos_ref, sin_ref, gq_ref, gk_ref, o_ref, dgq_ref, dgk_ref):
        @pl.when(pl.program_id(0) == 0)
        def _():
            dgq_ref[...] = jnp.zeros_like(dgq_ref)
            dgk_ref[...] = jnp.zeros_like(dgk_ref)

        bd = _head_blockdiag()
        cs, sn = cos_ref[...], sin_ref[...]
        isa = _is_a((tr, LANES))

        def back(xv, dy, g):
            rstd = lax.rsqrt(_dot_split(xv * xv, bd, 2) * (1.0 / HEAD) + EPS)
            xhat = xv * rstd
            dxn = dy * cs + _swap_half(dy * sn)
            dxh = dxn * g
            mean = _dot_split(dxh * xhat, bd, 2) * (1.0 / HEAD)
            return rstd * (dxh - xhat * mean), jnp.sum(dxn * xhat, axis=0, keepdims=True)

        def fold(s):
            a = dk_ref[:, 2 * s * LANES:(2 * s + 1) * LANES]
            b = dk_ref[:, (2 * s + 1) * LANES:(2 * s + 2) * LANES]
            return jnp.where(isa, a + pltpu.roll(a, HEAD, axis=1), b + pltpu.roll(b, HEAD, axis=1))

        def foldv(s):
            a = dv_ref[:, 2 * s * LANES:(2 * s + 1) * LANES]
            b = dv_ref[:, (2 * s + 1) * LANES:(2 * s + 2) * LANES]
            return jnp.where(isa, a + pltpu.roll(a, HEAD, axis=1), b + pltpu.roll(b, HEAD, axis=1))

        dgq = jnp.zeros((1, LANES), F32)
        for s in range(nq):
            sl = slice(s * LANES, (s + 1) * LANES)
            dx, dg = back(x_ref[:, sl], dq_ref[:, sl], gq_ref[...])
            o_ref[:, sl] = dx.astype(BF16)
            dgq = dgq + dg
        dgq_ref[...] += dgq
        dgk = jnp.zeros((1, LANES), F32)
        for s in range(nk):
            sl = slice(D + s * LANES, D + (s + 1) * LANES)
            dx, dg = back(x_ref[:, sl], fold(s), gk_ref[...])
            o_ref[:, sl] = dx.astype(BF16)
            dgk = dgk + dg
            o_ref[:, D + (nk + s) * LANES:D + (nk + s + 1) * LANES] = foldv(s).astype(BF16)
        dgk_ref[...] += dgk

    tab = pl.BlockSpec((tr, LANES), lambda i: (i, 0))
    vec = pl.BlockSpec((1, LANES), lambda i: (0, 0))
    kv = pl.BlockSpec((tr, KW), lambda i: (i, 0))
    return pl.pallas_call(
        body, name="attn_prep_bwd", grid=(T // tr,),
        in_specs=[pl.BlockSpec((tr, QW), lambda i: (i, 0)), pl.BlockSpec((tr, D), lambda i: (i, 0)), kv, kv,
                  tab, tab, vec, vec],
        out_specs=[pl.BlockSpec((tr, QW), lambda i: (i, 0)), vec, vec],
        out_shape=[jax.ShapeDtypeStruct((T, QW), BF16), jax.ShapeDtypeStruct((1, LANES), F32),
                   jax.ShapeDtypeStruct((1, LANES), F32)],
        compiler_params=_params(("arbitrary",)),
    )(qkv, dq, dkd, dvd, cos, sin, gq, gk)


def _attn_probs(qs, kw, sink_ref, first, scale):
    rows = qs.shape[0]
    qi = lax.broadcasted_iota(jnp.int32, (rows, 2 * WINDOW), 0) % WINDOW
    kj = lax.broadcasted_iota(jnp.int32, (rows, 2 * WINDOW), 1)
    valid = (kj > qi) & (kj <= qi + WINDOW)
    if first is not False:
        valid = valid & jnp.logical_or(jnp.logical_not(first), kj >= WINDOW)
    isa = _is_a(kw.shape)
    out = []
    for pos in (0, 1):
        kp = jnp.where(isa if pos == 0 else ~isa, kw, jnp.zeros_like(kw))
        s = jnp.where(valid, _dot(qs, kp, _NT) * scale, -jnp.inf)
        sink = sink_ref[0, pos]
        m = jnp.maximum(jnp.max(s, axis=1, keepdims=True), sink)
        p = jnp.exp(s - m)
        ps = jnp.exp(sink - m)
        inv = 1.0 / (jnp.sum(p, axis=1, keepdims=True) + ps)
        out.append((p * inv, ps * inv, kp))
    return out


def _attn_specs(qb):
    q = pl.BlockSpec((qb * WINDOW, ATT_GW), lambda g, n: (n, g))
    cur = pl.BlockSpec((qb * WINDOW, LANES), lambda g, n: (n, g))
    prev = pl.BlockSpec((WINDOW, LANES), lambda g, n: (jnp.maximum(qb * n - 1, 0), g))
    sink = pl.BlockSpec((1, 2, ATT_GW, 1), lambda g, n: (g, 0, 0, 0))
    return q, cur, prev, sink


def _stack(ref, s):
    rows = slice(s * WINDOW, (s + 1) * WINDOW)
    return jnp.concatenate([ref[rows, i * LANES:(i + 1) * LANES] for i in range(ATT_GW // LANES)], axis=0)


def _attn_fwd(q, kd, vd, sinkcol, HKV, carry=None):
    T, D = q.shape
    nb = T // WINDOW
    qb = math.gcd(nb, ATT_STEP_BLOCKS)
    scale = HEAD ** -0.5

    def body(q_ref, kp_ref, kc_ref, vp_ref, vc_ref, sink_ref, o_ref):
        n = pl.program_id(1)
        kall = jnp.concatenate([kp_ref[...], kc_ref[...]], axis=0)
        vall = jnp.concatenate([vp_ref[...], vc_ref[...]], axis=0)
        isa = _is_a((2 * WINDOW, LANES))
        for s in range(qb):
            win = slice(s * WINDOW, (s + 2) * WINDOW)
            kw, vw = kall[win], vall[win]
            o = jnp.zeros((ATT_GW, LANES), F32)
            first = (n == 0) if s == 0 else False
            for pos, (probs, _, _) in enumerate(_attn_probs(_stack(q_ref, s), kw, sink_ref, first, scale)):
                vp = jnp.where(isa if pos == 0 else ~isa, vw, jnp.zeros_like(vw))
                o = o + _dot(probs.astype(BF16), vp)
            for i in range(ATT_GW // LANES):
                o_ref[s * WINDOW:(s + 1) * WINDOW, i * LANES:(i + 1) * LANES] = o[i * WINDOW:(i + 1) * WINDOW].astype(BF16)

    qs_, cur, prev, sink = _attn_specs(qb)
    res = _call(
        body, name="attn_fwd", grid=(HKV, nb // qb), in_specs=[qs_, prev, cur, prev, cur, sink], out_specs=[qs_],
        out_shape=[jax.ShapeDtypeStruct((T, D), BF16)], sem=("parallel", "parallel"),
        args=(q, kd, kd, vd, vd, sinkcol), carry=carry)
    return res[0] if carry is None else (res[0], res[1:])


def _attn_bwd(q, kd, vd, o, do, sinkcol, HKV, carry=None):
    T, D = q.shape
    nb = T // WINDOW
    qb = math.gcd(nb, ATT_STEP_BLOCKS)
    scale = HEAD ** -0.5
    KW = HKV * LANES

    def body(q_ref, kp_ref, kc_ref, vp_ref, vc_ref, o_ref, do_ref, sink_ref, dq_ref, dk_ref, dv_ref, ds_ref):
        n = pl.program_id(1)

        @pl.when(n == 0)
        def _():
            dk_ref[...] = jnp.zeros_like(dk_ref)
            dv_ref[...] = jnp.zeros_like(dv_ref)
            ds_ref[...] = jnp.zeros_like(ds_ref)

        kall = jnp.concatenate([kp_ref[...], kc_ref[...]], axis=0)
        vall = jnp.concatenate([vp_ref[...], vc_ref[...]], axis=0)
        isa_w = _is_a((2 * WINDOW, LANES))
        isa_q = _is_a((ATT_GW, LANES))
        for s in range(qb):
            win = slice(s * WINDOW, (s + 2) * WINDOW)
            kw, vw = kall[win], vall[win]
            qs = _stack(q_ref, s)
            dos = _stack(do_ref, s)
            dd = dos * _stack(o_ref, s).astype(F32)
            dob = dos.astype(BF16)
            dqs = jnp.zeros((ATT_GW, LANES), F32)
            dkw, dvw, dsk = [], [], []
            first = (n == 0) if s == 0 else False
            for pos, (probs, psink, kp) in enumerate(_attn_probs(qs, kw, sink_ref, first, scale)):
                sel_w = isa_w if pos == 0 else ~isa_w
                sel_q = isa_q if pos == 0 else ~isa_q
                delta = jnp.sum(jnp.where(sel_q, dd, 0.0), axis=1, keepdims=True)
                vp = jnp.where(sel_w, vw, jnp.zeros_like(vw))
                dp = _dot(dob, vp, _NT)
                dsb = (probs * (dp - delta) * scale).astype(BF16)
                dqs = dqs + _dot(dsb, kp)
                dkw.append(_dot(dsb, qs, _TN))
                dvw.append(_dot(probs.astype(BF16), dob, _TN))
                dsk.append(-psink * delta)
            dkw = jnp.where(isa_w, dkw[0], dkw[1])
            dvw = jnp.where(isa_w, dvw[0], dvw[1])

            def add_window(dkw=dkw, dvw=dvw, s=s):
                start = pl.multiple_of((qb * n + s - 1) * WINDOW, WINDOW)
                dk_ref[pl.ds(start, 2 * WINDOW), :] += dkw
                dv_ref[pl.ds(start, 2 * WINDOW), :] += dvw

            if s == 0:
                @pl.when(n == 0)
                def _(dkw=dkw, dvw=dvw):
                    dk_ref[0:WINDOW, :] += dkw[WINDOW:]
                    dv_ref[0:WINDOW, :] += dvw[WINDOW:]

                pl.when(n > 0)(add_window)
            else:
                add_window()

            rows = []
            for i in range(ATT_GW // LANES):
                dq_ref[s * WINDOW:(s + 1) * WINDOW, i * LANES:(i + 1) * LANES] = dqs[i * WINDOW:(i + 1) * WINDOW]
                for pos in (0, 1):
                    t = jnp.sum(dsk[pos][i * WINDOW:(i + 1) * WINDOW], axis=0, keepdims=True)
                    rows.append(jnp.broadcast_to(t, (1, LANES)))
            ds_ref[0] += jnp.concatenate(rows, axis=0)

    qs_, cur, prev, sink = _attn_specs(qb)
    dqo = pl.BlockSpec((qb * WINDOW, ATT_GW), lambda g, n: (n, g))
    dkv = pl.BlockSpec((T, LANES), lambda g, n: (0, g))
    res = _call(
        body, name="attn_bwd", grid=(HKV, nb // qb),
        in_specs=[qs_, prev, cur, prev, cur, qs_, dqo, sink],
        out_specs=[dqo, dkv, dkv, pl.BlockSpec((1, ATT_GROUP, LANES), lambda g, n: (g, 0, 0))],
        out_shape=[jax.ShapeDtypeStruct((T, D), F32), jax.ShapeDtypeStruct((T, KW), F32),
                   jax.ShapeDtypeStruct((T, KW), F32), jax.ShapeDtypeStruct((HKV, ATT_GROUP, LANES), F32)],
        sem=("parallel", "arbitrary"), args=(q, kd, kd, vd, vd, o, do, sinkcol), carry=carry)
    return res if carry is None else (*res[:4], res[4:])


def _conv_fwd(zx, w, b, DI, CD):
    T = zx.shape[0]
    cw, tc = _tile(math.gcd(DI, CD), 512), 512
    off = DI // cw

    def body(cur_ref, halo_ref, w_ref, b_ref, o_ref):
        i = pl.program_id(1)
        halo = jnp.where(i > 0, halo_ref[...], 0.0)
        ext = jnp.concatenate([halo, cur_ref[...]], axis=0)
        acc = b_ref[...] + w_ref[SSM_CONV - 1:SSM_CONV, :] * ext[HALO:]
        for k in range(SSM_CONV - 1):
            acc = acc + w_ref[k:k + 1, :] * pltpu.roll(ext, SSM_CONV - 1 - k, axis=0)[HALO:]
        o_ref[...] = acc * jax.nn.sigmoid(acc)

    return pl.pallas_call(
        body, name="ssm_conv_fwd", grid=(CD // cw, T // tc),
        in_specs=[pl.BlockSpec((tc, cw), lambda j, i: (i, off + j)),
                  pl.BlockSpec((HALO, cw), lambda j, i: (jnp.maximum(i * (tc // HALO) - 1, 0), off + j)),
                  pl.BlockSpec((SSM_CONV, cw), lambda j, i: (0, j)), pl.BlockSpec((1, cw), lambda j, i: (0, j))],
        out_specs=pl.BlockSpec((tc, cw), lambda j, i: (i, j)),
        out_shape=jax.ShapeDtypeStruct((T, CD), F32),
        compiler_params=_params(("parallel", "parallel")),
    )(zx, zx, w, b)


def _conv_bwd(zx, dparts, dzx, w, b, DI, CD):
    T = zx.shape[0]
    cw, tc = _tile(math.gcd(DI, CD), 512), 512
    off = DI // cw
    nt = T // tc
    hb = tc // HALO
    ends = [0]
    for p in dparts:
        assert p.shape[1] % cw == 0
        ends.append(ends[-1] + p.shape[1] // cw)
    assert ends[-1] == CD // cw
    n_p = len(dparts)

    def body(*refs):
        cur_ref, prev_ref, next_ref = refs[:3]
        d_refs, dn_refs = refs[3:3 + n_p], refs[3 + n_p:3 + 2 * n_p]
        w_ref, b_ref, _, o_ref, dw_ref, db_ref = refs[3 + 2 * n_p:]
        j, i = pl.program_id(0), pl.program_id(1)

        @pl.when(i == 0)
        def _():
            dw_ref[...] = jnp.zeros_like(dw_ref)
            db_ref[...] = jnp.zeros_like(db_ref)

        def pick(prefs):
            v = prefs[n_p - 1][...]
            for p in range(n_p - 2, -1, -1):
                v = jnp.where(j < ends[p + 1], prefs[p][...], v)
            return v

        prev = jnp.where(i > 0, prev_ref[...], 0.0)
        ext = jnp.concatenate([prev, cur_ref[...], next_ref[...]], axis=0)
        u = b_ref[...] + w_ref[SSM_CONV - 1:SSM_CONV, :] * ext
        for k in range(SSM_CONV - 1):
            u = u + w_ref[k:k + 1, :] * pltpu.roll(ext, SSM_CONV - 1 - k, axis=0)
        u = u[HALO:]
        dnext = jnp.where(i < nt - 1, pick(dn_refs), 0.0)
        dxe = jnp.concatenate([pick(d_refs), dnext], axis=0)
        sg = jax.nn.sigmoid(u)
        du = dxe * sg * (1.0 + u * (1.0 - sg))
        n_e = tc + HALO
        dx = w_ref[SSM_CONV - 1:SSM_CONV, :] * du
        for k in range(SSM_CONV - 1):
            dx = dx + w_ref[k:k + 1, :] * pltpu.roll(du, n_e - (SSM_CONV - 1 - k), axis=0)
        o_ref[...] = dx[:tc].astype(BF16)
        duc = du[:tc]
        db_ref[...] += jnp.sum(duc, axis=0, keepdims=True)
        xs = ext[:n_e]
        dws = []
        for k in range(SSM_CONV):
            sh = xs if k == SSM_CONV - 1 else pltpu.roll(xs, SSM_CONV - 1 - k, axis=0)
            dws.append(jnp.sum(duc * sh[HALO:], axis=0, keepdims=True))
        dw_ref[...] += jnp.concatenate(dws, axis=0)

    def part_specs(p):
        lo, n = ends[p], ends[p + 1] - ends[p]

        def inside(j):
            return jnp.logical_and(j >= lo, j < lo + n)

        col = lambda j: jnp.clip(j - lo, 0, n - 1)
        return (pl.BlockSpec((tc, cw), lambda j, i: (jnp.where(inside(j), i, 0), col(j))),
                pl.BlockSpec((HALO, cw), lambda j, i: (jnp.where(inside(j), jnp.minimum((i + 1) * hb, nt * hb - 1), 0), col(j))))

    specs = [part_specs(p) for p in range(n_p)]
    return pl.pallas_call(
        body, name="ssm_conv_bwd", grid=(CD // cw, nt),
        in_specs=[pl.BlockSpec((tc, cw), lambda j, i: (i, off + j)),
                  pl.BlockSpec((HALO, cw), lambda j, i: (jnp.maximum(i * hb - 1, 0), off + j)),
                  pl.BlockSpec((HALO, cw), lambda j, i: (jnp.minimum((i + 1) * hb, nt * hb - 1), off + j))]
        + [s[0] for s in specs] + [s[1] for s in specs]
        + [pl.BlockSpec((SSM_CONV, cw), lambda j, i: (0, j)), pl.BlockSpec((1, cw), lambda j, i: (0, j)),
           pl.BlockSpec(memory_space=pl.ANY)],
        out_specs=[pl.BlockSpec((tc, cw), lambda j, i: (i, off + j)), pl.BlockSpec((SSM_CONV, cw), lambda j, i: (0, j)),
                   pl.BlockSpec((1, cw), lambda j, i: (0, j))],
        out_shape=[jax.ShapeDtypeStruct(dzx.shape, BF16), jax.ShapeDtypeStruct((SSM_CONV, CD), F32),
                   jax.ShapeDtypeStruct((1, CD), F32)],
        input_output_aliases={5 + 2 * n_p: 0},
        compiler_params=_params(("parallel", "arbitrary")),
    )(zx, zx, zx, *dparts, *dparts, w, b, dzx)


def _tri_dot(v, upper):
    L = v.shape[0]
    r = lax.broadcasted_iota(jnp.int32, (L, L), 0)
    c = lax.broadcasted_iota(jnp.int32, (L, L), 1)
    tri = ((r <= c) if upper else (r >= c)).astype(BF16)
    p = _split3(v)
    return _dot(tri, p[0]) + _dot(tri, p[1]) + _dot(tri, p[2])


def _ssd_time2(dtraw_ref, bias_ref, alog_ref, sel):
    dt = jax.nn.softplus(dtraw_ref[...] + bias_ref[...])
    acum = _tri_dot(dt * (-jnp.exp(alog_ref[...])), False)
    return dt, _dot_split(dt, sel, 3), _dot_split(acum, sel, 3)


def _decay(acs, acs_t, pos):
    L = acs.shape[0]
    r = lax.broadcasted_iota(jnp.int32, (L, L), 0)
    c = lax.broadcasted_iota(jnp.int32, (L, L), 1)
    col = acs[:, HEAD * pos:HEAD * pos + 1]
    row = acs_t[HEAD * pos:HEAD * pos + 1, :]
    return jnp.exp(jnp.where(r >= c, col - row, -jnp.inf))


def _ssd_specs(G, GW, DI, ZW):
    L = SSM_CHUNK
    grp = lambda f: pl.BlockSpec((L, GW), lambda g, c: (f(c), g))
    return dict(
        grp=grp,
        bmat=lambda f: pl.BlockSpec((L, SSM_STATE), lambda g, c: (f(c), DI // SSM_STATE + g)),
        cmat=lambda f: pl.BlockSpec((L, SSM_STATE), lambda g, c: (f(c), DI // SSM_STATE + G + g)),
        dtraw=lambda f: pl.BlockSpec((L, LANES), lambda g, c: (f(c), (2 * DI + 2 * G * SSM_STATE) // LANES)),
        vec=pl.BlockSpec((1, LANES), lambda g, c: (0, 0)),
        gvec=pl.BlockSpec((1, GW), lambda g, c: (0, g)),
        sel=pl.BlockSpec((1, LANES, GW), lambda g, c: (g, 0, 0)),
    )


def _ssd_fwd(zx, xc, bias, alog, sel, dskip, ng, DI, carry=None):
    T, ZW = zx.shape
    G, L = SSM_GROUPS, SSM_CHUNK
    GW = DI // G
    NS = GW // LANES
    nc = T // L
    sp = _ssd_specs(G, GW, DI, ZW)
    ident = lambda c: c

    def body(x_ref, b_ref, c_ref, z_ref, dtraw_ref, bias_ref, alog_ref, sel_ref, d_ref, ng_ref,
             y_ref, yo_ref, st_ref, state):
        c = pl.program_id(1)

        @pl.when(c == 0)
        def _():
            state[...] = jnp.zeros_like(state)

        x = x_ref[...]
        bb, cb_ = b_ref[...].astype(BF16), c_ref[...].astype(BF16)
        cbm = _dot(cb_, bb, _NT)
        _, dtx, acx = _ssd_time2(dtraw_ref, bias_ref, alog_ref, sel_ref[0])
        xdt = x * dtx
        ex = jnp.exp(acx)
        last = acx[L - 1:L, :]
        te = jnp.exp(last - acx)
        dlast = jnp.exp(last)
        isa = _is_a((L, LANES))
        for i in range(NS):
            sl = slice(i * LANES, (i + 1) * LANES)
            acs = acx[:, sl]
            acs_t = acs.T
            xs = xdt[:, sl]
            y = jnp.zeros((L, LANES), F32)
            for pos in (0, 1):
                m = (cbm * _decay(acs, acs_t, pos)).astype(BF16)
                y = y + _dot(m, jnp.where(isa if pos == 0 else ~isa, xs, 0.0).astype(BF16))
            st = state[i]
            st_ref[0, i] = st
            y = y + _dot(cb_, st.astype(BF16)) * ex[:, sl]
            state[i] = st * dlast[:, sl] + _dot(bb, (xs * te[:, sl]).astype(BF16), _TN)
            y_ref[:, sl] = y + d_ref[:, sl] * x[:, sl]
        z = z_ref[...]
        gated = y_ref[...] * (z * jax.nn.sigmoid(z))
        rstd = lax.rsqrt(jnp.mean(gated * gated, axis=1, keepdims=True) + EPS)
        yo_ref[...] = (gated * rstd * ng_ref[...]).astype(BF16)

    res = _call(
        body, name="ssd_fwd", grid=(G, nc),
        in_specs=[sp["grp"](ident), sp["bmat"](ident), sp["cmat"](ident), sp["grp"](ident), sp["dtraw"](ident),
                  sp["vec"], sp["vec"], sp["sel"], sp["gvec"], sp["gvec"]],
        out_specs=[sp["grp"](ident), sp["grp"](ident),
                   pl.BlockSpec((1, NS, SSM_STATE, LANES), lambda g, c: (c, g, 0, 0))],
        out_shape=[jax.ShapeDtypeStruct((T, DI), F32), jax.ShapeDtypeStruct((T, DI), BF16),
                   jax.ShapeDtypeStruct((nc, G * NS, SSM_STATE, LANES), F32)],
        scratch_shapes=[pltpu.VMEM((NS, SSM_STATE, LANES), F32)],
        sem=("parallel", "arbitrary"), args=(xc, xc, xc, zx, zx, bias, alog, sel, dskip, ng), carry=carry)
    return res if carry is None else (*res[:3], res[3:])


def _ssd_bwd(zx, xc, yssd, dyo, states, bias, alog, sel, dskip, ng, DI, carry=None):
    T, ZW = zx.shape
    G, L = SSM_GROUPS, SSM_CHUNK
    GW = DI // G
    NS = GW // LANES
    nc = T // L
    sp = _ssd_specs(G, GW, DI, ZW)
    rev = lambda c: nc - 1 - c

    def body(x_ref, b_ref, c_ref, z_ref, dtraw_ref, y_ref, dyo_ref, st_ref, bias_ref, alog_ref, sel_ref,
             d_ref, ng_ref, dz_ref, dx_ref, db_ref, dc_ref, ddt_ref, dac_ref, dd_ref, dng_ref, dstate):
        c = pl.program_id(1)

        @pl.when(c == 0)
        def _():
            dstate[...] = jnp.zeros_like(dstate)
            dd_ref[...] = jnp.zeros_like(dd_ref)
            dng_ref[...] = jnp.zeros_like(dng_ref)

        z, ys, dyo = z_ref[...], y_ref[...], dyo_ref[...]
        sg = jax.nn.sigmoid(z)
        sz = z * sg
        gated = ys * sz
        rstd = lax.rsqrt(jnp.mean(gated * gated, axis=1, keepdims=True) + EPS)
        yn = gated * rstd
        dng_ref[0] += jnp.sum(dyo * yn, axis=0, keepdims=True)
        dyn = dyo * ng_ref[...]
        dgated = rstd * (dyn - yn * jnp.mean(dyn * yn, axis=1, keepdims=True))
        g = dgated * sz
        dz_ref[...] = (dgated * ys * sg * (1.0 + z * (1.0 - sg))).astype(BF16)

        x = x_ref[...]
        dsk = d_ref[...]
        dd_ref[0] += jnp.sum(g * x, axis=0, keepdims=True)
        bb, cb_ = b_ref[...].astype(BF16), c_ref[...].astype(BF16)
        cbm = _dot(cb_, bb, _NT)
        _, dtx, acx = _ssd_time2(dtraw_ref, bias_ref, alog_ref, sel_ref[0])
        xdt = x * dtx
        ex = jnp.exp(acx)
        last = acx[L - 1:L, :]
        te = jnp.exp(last - acx)
        dlast = jnp.exp(last)
        isa = _is_a((L, LANES))
        is_last = lax.broadcasted_iota(jnp.int32, (L, LANES), 0) == L - 1
        strict = lax.broadcasted_iota(jnp.int32, (L, L), 0) > lax.broadcasted_iota(jnp.int32, (L, L), 1)
        lane_id = lax.broadcasted_iota(jnp.int32, (1, LANES), 1)
        row_id = lax.broadcasted_iota(jnp.int32, (8, 1), 0)
        lane_head = lax.broadcasted_iota(jnp.int32, (LANES, LANES), 0) // HEAD
        col_id = lax.broadcasted_iota(jnp.int32, (LANES, LANES), 1)
        dcb = jnp.zeros((L, L), F32)
        dcm = jnp.zeros((L, SSM_STATE), F32)
        dbm = jnp.zeros((L, SSM_STATE), F32)
        q_rows = jnp.zeros((L, LANES), F32)
        q_cols = jnp.zeros((8, L), F32)
        for i in range(NS):
            sl = slice(i * LANES, (i + 1) * LANES)
            acs = acx[:, sl]
            acs_t = acs.T
            xs, gs = xdt[:, sl], g[:, sl]
            xsb = xs.astype(BF16)
            dxd = jnp.zeros((L, LANES), F32)
            for pos in (0, 1):
                gp = jnp.where(isa if pos == 0 else ~isa, gs, 0.0).astype(BF16)
                dec = _decay(acs, acs_t, pos)
                dxd = dxd + _dot((cbm * dec).astype(BF16), gp, _TN)
                dmd = _dot(gp, xsb, _NT) * dec
                dcb = dcb + dmd
                q = jnp.where(strict, dmd * cbm, 0.0)
                q_rows = q_rows + jnp.sum(q, axis=1, keepdims=True) * (lane_id == 2 * i + pos).astype(F32)
                q_cols = q_cols + jnp.sum(q, axis=0, keepdims=True) * (row_id == 2 * i + pos).astype(F32)
            st = st_ref[0, i]
            dst = dstate[i]
            stb, dstb = st.astype(BF16), dst.astype(BF16)
            eg = (ex[:, sl] * gs).astype(BF16)
            dcm = dcm + _dot(eg, stb, _NT)
            yoff = _dot(cb_, stb) * ex[:, sl]
            w = xs * te[:, sl]
            wb = w.astype(BF16)
            dw = _dot(bb, dstb)
            dbm = dbm + _dot(wb, dstb, _NT)
            dxt = dxd + dw * te[:, sl]
            dal = dlast[:, sl] * jnp.sum(dst * st, axis=0, keepdims=True) + jnp.sum(dw * w, axis=0, keepdims=True)
            dac_l = gs * yoff - w * dw + jnp.where(is_last, dal, 0.0)
            ddt_l = dxt * x[:, sl]
            dstate[i] = dst * dlast[:, sl] + _dot(cb_, eg, _TN)
            dx_ref[:, sl] = dxt * dtx[:, sl] + dsk[:, sl] * gs
            to_head = (col_id == 2 * i + lane_head).astype(BF16)
            part = _dot_split(ddt_l, to_head, 2)
            parta = _dot_split(dac_l, to_head, 2)
            if i == 0:
                ddt_ref[0] = part
                dac_ref[0] = parta
            else:
                ddt_ref[0] += part
                dac_ref[0] += parta
        dac_ref[0] += q_rows - jnp.concatenate([q_cols, jnp.zeros((LANES - 8, L), F32)], axis=0).T
        dcbb = dcb.astype(BF16)
        dc_ref[...] = dcm + _dot(dcbb, bb)
        db_ref[...] = dbm + _dot(dcbb, cb_, _TN)

    part_spec = pl.BlockSpec((1, L, LANES), lambda g, c: (g, rev(c), 0))
    lane_spec = pl.BlockSpec((1, 1, GW), lambda g, c: (g, 0, 0))
    bc_out = pl.BlockSpec((L, SSM_STATE), lambda g, c: (rev(c), g))
    res = _call(
        body, name="ssd_bwd", grid=(G, nc),
        in_specs=[sp["grp"](rev), sp["bmat"](rev), sp["cmat"](rev), sp["grp"](rev), sp["dtraw"](rev), sp["grp"](rev),
                  sp["grp"](rev), pl.BlockSpec((1, NS, SSM_STATE, LANES), lambda g, c: (rev(c), g, 0, 0)),
                  sp["vec"], sp["vec"], sp["sel"], sp["gvec"], sp["gvec"]],
        out_specs=[sp["grp"](rev), sp["grp"](rev), bc_out, bc_out, part_spec, part_spec, lane_spec, lane_spec],
        out_shape=[jax.ShapeDtypeStruct((T, ZW), BF16), jax.ShapeDtypeStruct((T, DI), F32),
                   jax.ShapeDtypeStruct((T, G * SSM_STATE), F32), jax.ShapeDtypeStruct((T, G * SSM_STATE), F32),
                   jax.ShapeDtypeStruct((G, T, LANES), F32), jax.ShapeDtypeStruct((G, T, LANES), F32),
                   jax.ShapeDtypeStruct((G, 1, GW), F32), jax.ShapeDtypeStruct((G, 1, GW), F32)],
        scratch_shapes=[pltpu.VMEM((NS, SSM_STATE, LANES), F32)], sem=("parallel", "arbitrary"),
        args=(xc, xc, xc, zx, zx, yssd, dyo, states, bias, alog, sel, dskip, ng), carry=carry)
    return res if carry is None else (*res[:8], res[8:])


def _ssd_dt_bwd(zx, ddt_part, dac_part, dzx, bias, alog, DI):
    T, ZW = zx.shape
    G, L = SSM_GROUPS, SSM_CHUNK
    nc = T // L
    heads = DI // HEAD // G
    tail = ZW - 2 * DI - 2 * G * SSM_STATE
    dt_block = (ZW - tail) // LANES

    def body(dtraw_ref, ddt_ref, dac_ref, bias_ref, alog_ref, _, o_ref, dal_ref, dbias_ref):
        @pl.when(pl.program_id(0) == 0)
        def _():
            dal_ref[...] = jnp.zeros_like(dal_ref)
            dbias_ref[...] = jnp.zeros_like(dbias_ref)

        raw = dtraw_ref[...] + bias_ref[...]
        dt = jax.nn.softplus(raw)
        a = -jnp.exp(alog_ref[...])
        dac, ddt = dac_ref[0], ddt_ref[0]
        for gi in range(1, G):
            dac = dac + pltpu.roll(dac_ref[gi], gi * heads, axis=1)
            ddt = ddt + pltpu.roll(ddt_ref[gi], gi * heads, axis=1)
        dda = _tri_dot(dac, True)
        dal_ref[...] += jnp.sum(dda * dt, axis=0, keepdims=True) * a
        draw = (dda * a + ddt) * jax.nn.sigmoid(raw)
        dbias_ref[...] += jnp.sum(draw, axis=0, keepdims=True)
        o_ref[...] = jnp.concatenate([draw.astype(BF16), jnp.zeros((L, tail - LANES), BF16)], axis=1)

    vec = pl.BlockSpec((1, LANES), lambda c: (0, 0))
    part = pl.BlockSpec((G, L, LANES), lambda c: (0, c, 0))
    return pl.pallas_call(
        body, name="ssd_dt_bwd", grid=(nc,),
        in_specs=[pl.BlockSpec((L, LANES), lambda c: (c, dt_block)), part, part, vec, vec,
                  pl.BlockSpec(memory_space=pl.ANY)],
        out_specs=[pl.BlockSpec((L, tail), lambda c: (c, (ZW - tail) // tail)), vec, vec],
        out_shape=[jax.ShapeDtypeStruct((T, ZW), BF16), jax.ShapeDtypeStruct((1, LANES), F32),
                   jax.ShapeDtypeStruct((1, LANES), F32)],
        input_output_aliases={5: 0},
        compiler_params=_params(("arbitrary",)),
    )(zx, ddt_part, dac_part, bias, alog, dzx)


def _all_gather(shards, name):
    n = len(shards)

    def body(*refs):
        for phase in _gather_phases(refs[:n], refs[n:2 * n], *refs[2 * n:]):
            phase()

    any_spec = pl.BlockSpec(memory_space=pl.ANY)
    return pl.pallas_call(
        body, name=name, in_specs=[any_spec] * n, out_specs=[any_spec] * n,
        out_shape=[jax.ShapeDtypeStruct((N_DEV,) + s.shape, s.dtype) for s in shards],
        scratch_shapes=[pltpu.SemaphoreType.DMA((n, 7)), pltpu.SemaphoreType.DMA((n, 7)),
                        pltpu.SemaphoreType.DMA((n,))],
    )(*shards)


def _exchange(blocks, name):
    n = len(blocks)

    def body(*refs):
        local, sends, arrivals = _direct_copies(refs[:n], refs[n:2 * n], *refs[2 * n:])
        for cp in local + sends:
            cp.start()
        for cp in arrivals:
            cp.wait_recv()
        for cp in sends:
            cp.wait_send()
        for cp in local:
            cp.wait()

    any_spec = pl.BlockSpec(memory_space=pl.ANY)
    return pl.pallas_call(
        body, name=name, in_specs=[any_spec] * n, out_specs=[any_spec] * n,
        out_shape=[jax.ShapeDtypeStruct(b.shape, b.dtype) for b in blocks],
        scratch_shapes=[pltpu.SemaphoreType.DMA((n, 7)), pltpu.SemaphoreType.DMA((n, 7)),
                        pltpu.SemaphoreType.DMA((n,))],
    )(*blocks)


def _adamw(parts, w, m, v, name):
    nl = len(parts)
    R, C = parts[0].shape[1:]
    per_row = C * (N_DEV * nl * parts[0].dtype.itemsize + 7 * 4) * 2
    tr = R
    if R % 8 == 0:
        tr = 8
        for t in (16, 32, 64, 128, 256, 512):
            if R % t == 0 and t * per_row <= 24 * 1024 * 1024:
                tr = t
    nr = R // tr
    c1 = 1.0 - ADAM_B1 ** ADAM_STEP
    c2 = 1.0 - ADAM_B2 ** ADAM_STEP

    def body(*refs):
        p_refs = refs[:nl]
        w_ref, m_ref, v_ref, g_ref, d_ref, nm_ref, nv_ref = refs[nl:]
        for layer in range(nl):
            @pl.when(pl.program_id(0) == layer)
            def _(p_ref=p_refs[layer]):
                g = p_ref[0].astype(F32)
                for k in range(1, N_DEV):
                    g = g + p_ref[k].astype(F32)
                nm = ADAM_B1 * m_ref[...] + (1.0 - ADAM_B1) * g
                nv = ADAM_B2 * v_ref[...] + (1.0 - ADAM_B2) * (g * g)
                g_ref[...] = g
                nm_ref[...] = nm
                nv_ref[...] = nv
                d_ref[...] = -ADAM_LR * ((nm / c1) / (jnp.sqrt(nv / c2) + ADAM_EPS) + ADAM_WD * w_ref[...])

    def part_spec(layer):
        return pl.BlockSpec((N_DEV, tr, C), lambda l, i: (0, jnp.where(l == layer, i, jnp.where(l < layer, 0, nr - 1)), 0))

    blk = pl.BlockSpec((tr, C), lambda l, i: (l * nr + i, 0))
    out = jax.ShapeDtypeStruct((nl * R, C), F32)
    return pl.pallas_call(
        body, name=name, grid=(nl, nr),
        in_specs=[part_spec(layer) for layer in range(nl)] + [blk, blk, blk],
        out_specs=[blk, blk, blk, blk], out_shape=[out, out, out, out],
        compiler_params=_params(("arbitrary", "arbitrary")),
    )(*parts, w, m, v)


def _pad_cols(a, n):
    return jnp.pad(a, ((0, 0), (0, n - a.shape[1])))


def kernel(x, positions, mixer_norm, ffn_norm, attn_w_qkv, attn_q_norm, attn_k_norm, attn_sinks, attn_w_o, ssm_w_in, ssm_conv_w, ssm_conv_b, ssm_dt_bias, ssm_a_log, ssm_d, ssm_norm, ssm_w_out, ffn_w_gate, ffn_w_up, ffn_w_down, loss_target, m_mixer_norm, m_ffn_norm, m_attn_w_qkv, m_attn_q_norm, m_attn_k_norm, m_attn_sinks, m_attn_w_o, m_ssm_w_in, m_ssm_conv_w, m_ssm_conv_b, m_ssm_dt_bias, m_ssm_a_log, m_ssm_d, m_ssm_norm, m_ssm_w_out, m_ffn_w_gate, m_ffn_w_up, m_ffn_w_down, v_mixer_norm, v_ffn_norm, v_attn_w_qkv, v_attn_q_norm, v_attn_k_norm, v_attn_sinks, v_attn_w_o, v_ssm_w_in, v_ssm_conv_w, v_ssm_conv_b, v_ssm_dt_bias, v_ssm_a_log, v_ssm_d, v_ssm_norm, v_ssm_w_out, v_ffn_w_gate, v_ffn_w_up, v_ffn_w_down):
    T, D = x.shape[1], x.shape[2]
    HQ = D // HEAD
    HKV = HQ // ATT_GROUP
    QW = (HQ + 2 * HKV) * HEAD
    DI = 2 * D
    H = DI // HEAD
    G = SSM_GROUPS
    GW = DI // G
    CD = DI + 2 * G * SSM_STATE
    ZW = -(-(DI + CD + LANES) // 512) * 512
    IW = DI + CD + H
    assert T % 512 == 0 and D % 256 == 0 and HKV % 2 == 0 and GW % LANES == 0 and H <= LANES

    weights = dict(mixer_norm=mixer_norm, ffn_norm=ffn_norm, attn_w_qkv=attn_w_qkv, attn_q_norm=attn_q_norm,
                   attn_k_norm=attn_k_norm, attn_sinks=attn_sinks, attn_w_o=attn_w_o, ssm_w_in=ssm_w_in,
                   ssm_conv_w=ssm_conv_w, ssm_conv_b=ssm_conv_b, ssm_dt_bias=ssm_dt_bias, ssm_a_log=ssm_a_log,
                   ssm_d=ssm_d, ssm_norm=ssm_norm, ssm_w_out=ssm_w_out, ffn_w_gate=ffn_w_gate, ffn_w_up=ffn_w_up,
                   ffn_w_down=ffn_w_down)
    mom_m = dict(mixer_norm=m_mixer_norm, ffn_norm=m_ffn_norm, attn_w_qkv=m_attn_w_qkv, attn_q_norm=m_attn_q_norm,
                 attn_k_norm=m_attn_k_norm, attn_sinks=m_attn_sinks, attn_w_o=m_attn_w_o, ssm_w_in=m_ssm_w_in,
                 ssm_conv_w=m_ssm_conv_w, ssm_conv_b=m_ssm_conv_b, ssm_dt_bias=m_ssm_dt_bias, ssm_a_log=m_ssm_a_log,
                 ssm_d=m_ssm_d, ssm_norm=m_ssm_norm, ssm_w_out=m_ssm_w_out, ffn_w_gate=m_ffn_w_gate,
                 ffn_w_up=m_ffn_w_up, ffn_w_down=m_ffn_w_down)
    mom_v = dict(mixer_norm=v_mixer_norm, ffn_norm=v_ffn_norm, attn_w_qkv=v_attn_w_qkv, attn_q_norm=v_attn_q_norm,
                 attn_k_norm=v_attn_k_norm, attn_sinks=v_attn_sinks, attn_w_o=v_attn_w_o, ssm_w_in=v_ssm_w_in,
                 ssm_conv_w=v_ssm_conv_w, ssm_conv_b=v_ssm_conv_b, ssm_dt_bias=v_ssm_dt_bias, ssm_a_log=v_ssm_a_log,
                 ssm_d=v_ssm_d, ssm_norm=v_ssm_norm, ssm_w_out=v_ssm_w_out, ffn_w_gate=v_ffn_w_gate,
                 ffn_w_up=v_ffn_w_up, ffn_w_down=v_ffn_w_down)
    big = ["attn_w_qkv", "attn_w_o", "ssm_w_in", "ssm_w_out", "ffn_w_gate", "ffn_w_up", "ffn_w_down"]

    def flat2(a):
        return a.reshape(-1, a.shape[-1])

    def shard(n, layer=0):
        return weights[n][layer].astype(BF16)

    def from_cols(g):
        return g.transpose(1, 0, 2).reshape(g.shape[1], N_DEV * g.shape[2])

    def from_rows(g):
        return g.reshape(N_DEV * g.shape[1], g.shape[2])

    xs = x[0]
    tgt = loss_target[0]
    inv_freq = ROPE_THETA ** (-jnp.arange(0, HEAD, 2, dtype=F32) / HEAD)
    ang = positions[0].astype(F32)[:, None] * inv_freq
    cos = jnp.tile(jnp.cos(ang), (1, 4))
    sin = jnp.tile(jnp.concatenate([-jnp.sin(ang), jnp.sin(ang)], axis=1), (1, 2))
    gq = jnp.tile(attn_q_norm, (1, 2))
    gk = jnp.tile(attn_k_norm, (1, 2))
    sinkcol = jnp.repeat(attn_sinks.reshape(HKV, ATT_GROUP // 2, 2).transpose(0, 2, 1), WINDOW, axis=2)[..., None]
    bias_p = _pad_cols(ssm_dt_bias, LANES)
    alog_p = _pad_cols(ssm_a_log, LANES)
    dskip = jnp.repeat(ssm_d, HEAD, axis=1)
    lane_head = jnp.arange(DI) // HEAD
    sel = (jnp.arange(LANES)[None, :, None] == lane_head.reshape(G, 1, GW)).astype(BF16)
    vec_w = CD // N_DEV
    small = jnp.concatenate([ssm_conv_w[0], ssm_conv_b, _pad_cols(ssm_norm, vec_w),
                             jnp.zeros((2, vec_w), F32)], axis=0)
    g_qkv, g_o, small_all = _all_gather([shard("attn_w_qkv"), shard("attn_w_o"), small], "gather_first")
    w_qkv, w_o = from_cols(g_qkv), from_rows(g_o)
    conv_w = small_all[:, :SSM_CONV].transpose(1, 0, 2).reshape(SSM_CONV, CD)
    conv_b = small_all[:, SSM_CONV].reshape(1, CD)
    ng = small_all[:, SSM_CONV + 1, :DI // N_DEV].reshape(1, DI)

    def rows_to_blocks(p):
        return p.reshape(N_DEV, p.shape[0] // N_DEV, p.shape[1])

    def cols_to_blocks(p):
        return p.reshape(p.shape[0], N_DEV, p.shape[1] // N_DEV).transpose(1, 0, 2)

    hm0 = _rms_fwd(xs, mixer_norm[0:1], "rms_fwd_m0")
    qkv, got = _matmul(hm0, w_qkv, mode="nn", out_dtype=F32, name="mm_qkv", carry=[shard("ffn_w_gate", 0)])
    w_gate = [from_cols(got[0]), None]
    qr, kd, vd = _attn_prep_fwd(qkv, cos, sin, gq, gk, D, HKV)
    o, got = _attn_fwd(qr, kd, vd, sinkcol, HKV, carry=[shard("ffn_w_up", 0), shard("ffn_w_down", 0)])
    w_up = [from_cols(got[0]), None]
    w_down = [from_rows(got[1]), None]
    x1 = _matmul(o, w_o, mode="nn", out_dtype=F32, name="mm_attn_out", add=xs)
    hf0 = _rms_fwd(x1, ffn_norm[0:1], "rms_fwd_f0")
    gate0, up0, act0, got = _ffn_up(hf0, w_gate[0], w_up[0], "ffn_up_0", carry=[shard("ssm_w_in"), shard("ssm_w_out")])
    w_in = _pad_cols(from_cols(got[0]), ZW)
    w_out = from_rows(got[1])
    x2 = _matmul(act0, w_down[0], mode="nn", out_dtype=F32, name="mm_ffn_down_0", add=x1)
    hm1 = _rms_fwd(x2, mixer_norm[1:2], "rms_fwd_m1")
    zx, got = _matmul(hm1, w_in, mode="nn", out_dtype=F32, name="mm_ssm_in",
                      carry=[shard("ffn_w_gate", 1), shard("ffn_w_up", 1)])
    w_gate[1], w_up[1] = from_cols(got[0]), from_cols(got[1])
    xc = _conv_fwd(zx, conv_w, conv_b, DI, CD)
    yssd, yout, states, got = _ssd_fwd(zx, xc, bias_p, alog_p, sel, dskip, ng, DI, carry=[shard("ffn_w_down", 1)])
    w_down[1] = from_rows(got[0])
    x3 = _matmul(yout, w_out, mode="nn", out_dtype=F32, name="mm_ssm_out", add=x2)
    hf1 = _rms_fwd(x3, ffn_norm[1:2], "rms_fwd_f1")
    gate1, up1, act1 = _ffn_up(hf1, w_gate[1], w_up[1], "ffn_up_1")
    x4 = _matmul(act1, w_down[1], mode="nn", out_dtype=F32, name="mm_ffn_down_1", add=x3)
    sq, dx4, dx4b = _loss_head(x4, tgt)
    loss = lax.psum(sq[0, 0] * (0.5 / D), ("x", "y", "c"))

    def halves(blocks):
        half = blocks.shape[1] // 2
        return blocks[:, :half], blocks[:, half:]

    def ffn_bwd(dy, dyb, hf, gate, up, act, layer, xin, gain):
        dg, du = _ffn_dact(dyb, w_down[layer], gate, up, f"ffn_dact_{layer}")
        g_down = _matmul(act, dyb, mode="tn", out_dtype=BF16, name=f"mm_dw_down_{layer}")
        down_a, down_b = halves(rows_to_blocks(g_down))
        g_gate, got_da = _matmul(hf, dg, mode="tn", out_dtype=BF16, name=f"mm_dw_gate_{layer}", carry=[down_a])
        g_up, got_db = _matmul(hf, du, mode="tn", out_dtype=BF16, name=f"mm_dw_up_{layer}", carry=[down_b])
        gate_a, gate_b = halves(cols_to_blocks(g_gate))
        dh, got_ga = _matmul(dg, w_gate[layer], mode="nt", out_dtype=F32, name=f"mm_dh_gate_{layer}", carry=[gate_a])
        dh, got_gb = _matmul(du, w_up[layer], mode="nt", out_dtype=F32, name=f"mm_dh_up_{layer}", add=dh,
                             carry=[gate_b])
        dx, dxb, dgain = _rms_bwd(xin, gain, dh, dy, f"rms_bwd_f{layer}")
        return dx, dxb, dgain, cols_to_blocks(g_up), dict(down=[got_da[0], got_db[0]], gate=[got_ga[0], got_gb[0]])

    dx3, dx3b, d_fn1, up1_blocks, ffn1_got = ffn_bwd(dx4, dx4b, hf1, gate1, up1, act1, 1, x3, ffn_norm[1:2])
    dyo = _matmul(dx3b, w_out, mode="nt", out_dtype=F32, name="mm_dyout")
    g_wout = _matmul(yout, dx3b, mode="tn", out_dtype=BF16, name="mm_dw_ssm_out")
    dzx, dxx, dbm, dcm, ddt_p, dac_p, dd_l, dng_l, got1 = _ssd_bwd(
        zx, xc, yssd, dyo, states, bias_p, alog_p, sel, dskip, ng, DI,
        carry=[up1_blocks, rows_to_blocks(g_wout)])
    dzx, d_alog, d_bias = _ssd_dt_bwd(zx, ddt_p, dac_p, dzx, bias_p, alog_p, DI)
    dzx, d_convw, d_convb = _conv_bwd(zx, [dxx, dbm, dcm], dzx, conv_w, conv_b, DI, CD)
    g_win = _matmul(hm1, dzx, mode="tn", out_dtype=BF16, name="mm_dw_ssm_in")[:, :IW]
    dh, got2 = _matmul(dzx, w_in, mode="nt", out_dtype=F32, name="mm_dh_ssm_in", carry=[cols_to_blocks(g_win)])
    dx2, dx2b, d_mn1 = _rms_bwd(x2, mixer_norm[1:2], dh, dx3, "rms_bwd_m1")
    dx1, dx1b, d_fn0, up0_blocks, ffn0_got = ffn_bwd(dx2, dx2b, hf0, gate0, up0, act0, 0, x1, ffn_norm[0:1])
    do = _matmul(dx1b, w_o, mode="nt", out_dtype=F32, name="mm_do")
    g_wo = _matmul(o, dx1b, mode="tn", out_dtype=BF16, name="mm_dw_attn_out")
    dq, dkd, dvd, dsink, got3 = _attn_bwd(qr, kd, vd, o, do, sinkcol, HKV, carry=[up0_blocks, rows_to_blocks(g_wo)])
    dqkv, dgq_l, dgk_l = _attn_prep_bwd(qkv, dq, dkd, dvd, cos, sin, gq, gk, D, HKV)
    g_wqkv = _matmul(hm0, dqkv, mode="tn", out_dtype=BF16, name="mm_dw_qkv")
    dh, got_c = _matmul(dqkv, w_qkv, mode="nt", out_dtype=F32, name="mm_dh_qkv", carry=[cols_to_blocks(g_wqkv)])
    dx0, _, d_mn0 = _rms_bwd(xs, mixer_norm[0:1], dh, dx1, "rms_bwd_m0")

    d_ng = dng_l.reshape(1, DI)
    vec_send = jnp.concatenate([
        d_convw.reshape(SSM_CONV, N_DEV, vec_w).transpose(1, 0, 2), d_convb.reshape(1, N_DEV, vec_w).transpose(1, 0, 2),
        _pad_cols(d_ng.reshape(N_DEV, DI // N_DEV), vec_w)[:, None, :], jnp.zeros((N_DEV, 2, vec_w), F32)], axis=1)
    d_sinks = dsink[:, :, 0].reshape(1, HQ)
    d_gq = dgq_l[:, :HEAD] + dgq_l[:, HEAD:]
    d_gk = dgk_l[:, :HEAD] + dgk_l[:, HEAD:]
    d_dskip = dd_l.reshape(H, HEAD).sum(axis=1).reshape(1, H)
    rep_names = ["mixer_norm", "ffn_norm", "attn_q_norm", "attn_k_norm", "attn_sinks", "ssm_dt_bias", "ssm_a_log",
                 "ssm_d"]
    rep_grads = [jnp.concatenate([d_mn0, d_mn1], axis=0), jnp.concatenate([d_fn0, d_fn1], axis=0), d_gq, d_gk,
                 d_sinks, d_bias[:, :H], d_alog[:, :H], d_dskip]
    rep_sizes = [weights[n].size for n in rep_names]
    rep_len = -(-sum(rep_sizes) // (8 * LANES)) * 8 * LANES

    def pack(arrs):
        flat = jnp.concatenate([a.reshape(-1) for a in arrs])
        return jnp.pad(flat, (0, rep_len - flat.shape[0])).reshape(rep_len // LANES, LANES)

    rep_send = jnp.broadcast_to(pack(rep_grads)[None], (N_DEV, rep_len // LANES, LANES))
    got4 = _exchange([vec_send, rep_send], "exchange_last")
    parts_of = {
        "attn_w_qkv": [got_c[0]], "attn_w_o": [got3[1]], "ssm_w_in": [got2[0]], "ssm_w_out": [got1[1]],
        "ffn_w_gate": ffn0_got["gate"] + ffn1_got["gate"], "ffn_w_up": [got3[0], got1[0]],
        "ffn_w_down": ffn0_got["down"] + ffn1_got["down"],
    }

    out = {}
    for n in big:
        res = _adamw(parts_of[n], flat2(weights[n]), flat2(mom_m[n]), flat2(mom_v[n]), f"adamw_{n}")
        out[n] = [r.reshape(weights[n].shape) for r in res]

    def vec_block(d):
        return jnp.concatenate([d["ssm_conv_w"][0], d["ssm_conv_b"], _pad_cols(d["ssm_norm"], vec_w),
                                jnp.zeros((2, vec_w), F32)], axis=0)

    res = _adamw([got4[0]], vec_block(weights), vec_block(mom_m), vec_block(mom_v), "adamw_vectors")
    out["ssm_conv_w"] = [r[:SSM_CONV][None] for r in res]
    out["ssm_conv_b"] = [r[SSM_CONV:SSM_CONV + 1] for r in res]
    out["ssm_norm"] = [r[SSM_CONV + 1:SSM_CONV + 2, :DI // N_DEV] for r in res]
    res = _adamw([got4[1]], pack([weights[n] for n in rep_names]), pack([mom_m[n] for n in rep_names]),
                 pack([mom_v[n] for n in rep_names]), "adamw_replicated")
    offs = 0
    for n, sz in zip(rep_names, rep_sizes):
        out[n] = [r.reshape(-1)[offs:offs + sz].reshape(weights[n].shape) for r in res]
        offs += sz

    names = list(weights)
    return (loss, dx0[None], *[out[n][0] for n in names], *[out[n][1] for n in names],
            *[out[n][2] for n in names], *[out[n][3] for n in names])
```

```python
import functools
import math

import jax
import jax.numpy as jnp
from jax import lax
from jax.experimental import pallas as pl
from jax.experimental.pallas import tpu as pltpu

F32 = jnp.float32
BF16 = jnp.bfloat16

N_DEV = 8
EPS = 1e-6
LANES = 128
HEAD = 64
ATT_GROUP = 8
ATT_GW = ATT_GROUP * HEAD
WINDOW = 128
ATT_STEP_BLOCKS = 8
ROPE_THETA = 10000.0
SSM_GROUPS = 8
SSM_STATE = 128
SSM_CONV = 4
SSM_CHUNK = 256
HALO = 8
ADAM_LR, ADAM_B1, ADAM_B2, ADAM_EPS, ADAM_WD, ADAM_STEP = 0.001, 0.9, 0.999, 1e-08, 0.01, 10
VMEM_LIMIT = 56 * 1024 * 1024
MATMUL_VMEM = 44 * 1024 * 1024
MESH = pl.DeviceIdType.MESH

_NN = (((1,), (0,)), ((), ()))
_NT = (((1,), (1,)), ((), ()))
_TN = (((0,), (0,)), ((), ()))


def _dot(a, b, dims=_NN):
    return lax.dot_general(a, b, dims, preferred_element_type=F32)


def _tile(n, cap):
    if n % LANES:
        return n
    best = LANES
    for t in range(LANES, min(n, cap) + 1, LANES):
        if n % t == 0:
            best = t
    return best


def _params(sem):
    return pltpu.CompilerParams(dimension_semantics=sem, vmem_limit_bytes=VMEM_LIMIT)


def _slot(px, py, pc):
    return 4 * px + 2 * py + pc


def _direct_copies(srcs, dsts, send_sems, recv_sems, local_sems, with_arrivals=True):
    x, y, c = lax.axis_index("x"), lax.axis_index("y"), lax.axis_index("c")
    me = _slot(x, y, c)
    peers = [(x ^ (m >> 2), y ^ ((m >> 1) & 1), c ^ (m & 1)) for m in range(1, N_DEV)]
    local, sends, arrivals = [], [], []
    for w, (src, dst) in enumerate(zip(srcs, dsts)):
        sliced = src.shape == dst.shape
        local.append(pltpu.make_async_copy(src.at[me] if sliced else src, dst.at[me], local_sems.at[w]))
        for k, peer in enumerate(peers):
            sems = dict(send_sem=send_sems.at[w, k], recv_sem=recv_sems.at[w, k], device_id=peer, device_id_type=MESH)
            sends.append(pltpu.make_async_remote_copy(
                src_ref=src.at[_slot(*peer)] if sliced else src, dst_ref=dst.at[me], **sems))
            if with_arrivals:
                arrivals.append(pltpu.make_async_remote_copy(
                    src_ref=src.at[me] if sliced else src, dst_ref=dst.at[_slot(*peer)], **sems))
    return local, sends, arrivals


def _gather_phases(srcs, dsts, send_sems, recv_sems, local_sems):
    x, y, c = lax.axis_index("x"), lax.axis_index("y"), lax.axis_index("c")
    me, sibling = (x, y, c), (x, y, 1 - c)
    chips = [(1 - x, y), (x, 1 - y), (1 - x, 1 - y)]
    n = len(srcs)

    def copy(w, k, block, to, src=None):
        dst = dsts[w].at[_slot(*block)]
        return pltpu.make_async_remote_copy(
            src_ref=dst if src is None else src, dst_ref=dst, send_sem=send_sems.at[w, k],
            recv_sem=recv_sems.at[w, k], device_id=to, device_id_type=MESH)

    def first_sends(w):
        return [copy(w, 0, me, sibling, src=srcs[w])] + [copy(w, 1 + j, me, (*chip, c), src=srcs[w])
                                                         for j, chip in enumerate(chips)]

    def start():
        for w in range(n):
            pltpu.make_async_copy(srcs[w], dsts[w].at[_slot(*me)], local_sems.at[w]).start()
            for cp in first_sends(w):
                cp.start()

    def forward():
        for w in range(n):
            for j, chip in enumerate(chips):
                copy(w, 1 + j, (*chip, c), me).wait_recv()
                copy(w, 4 + j, (*chip, c), sibling).start()

    def finish():
        for w in range(n):
            copy(w, 0, sibling, me).wait_recv()
            for j, chip in enumerate(chips):
                copy(w, 4 + j, (*chip, 1 - c), me).wait_recv()
        for w in range(n):
            for cp in first_sends(w) + [copy(w, 4 + j, (*chip, c), sibling) for j, chip in enumerate(chips)]:
                cp.wait_send()
            pltpu.make_async_copy(srcs[w], dsts[w].at[_slot(*me)], local_sems.at[w]).wait()

    return start, forward, finish


def _call(body, *, name, grid, in_specs, out_specs, out_shape, sem, args, scratch_shapes=(), carry=None):
    if carry is None:
        return pl.pallas_call(body, name=name, grid=grid, in_specs=in_specs, out_specs=out_specs, out_shape=out_shape,
                              scratch_shapes=list(scratch_shapes), compiler_params=_params(sem))(*args)
    n_in, n_out, n_sc, n_c = len(in_specs), len(out_specs), len(scratch_shapes), len(carry)
    gather = all(a.ndim == 2 for a in carry)
    assert gather or all(a.ndim == 3 and a.shape[0] == N_DEV for a in carry)
    recv_shape = [jax.ShapeDtypeStruct((N_DEV,) + a.shape if gather else a.shape, a.dtype) for a in carry]
    n_steps = math.prod(grid)

    def wrapped(*refs):
        ins, c_in = refs[:n_in], refs[n_in:n_in + n_c]
        outs, c_out = refs[n_in + n_c:n_in + n_c + n_out], refs[n_in + n_c + n_out:n_in + 2 * n_c + n_out]
        scr = refs[n_in + 2 * n_c + n_out:n_in + 2 * n_c + n_out + n_sc]
        sems = refs[-3:]
        step = functools.reduce(lambda acc, d: acc * grid[d] + pl.program_id(d), range(len(grid)), 0)
        if gather:
            start, forward, finish = _gather_phases(c_in, c_out, *sems)
        else:
            def start():
                local, sends, _ = _direct_copies(c_in, c_out, *sems, with_arrivals=False)
                for cp in local + sends:
                    cp.start()

            def finish():
                local, sends, arrivals = _direct_copies(c_in, c_out, *sems)
                for cp in arrivals:
                    cp.wait_recv()
                for cp in sends:
                    cp.wait_send()
                for cp in local:
                    cp.wait()

        pl.when(step == 0)(start)
        if gather:
            pl.when(step == min((3 * n_steps) // 4, n_steps - 1))(forward)
        body(*ins, *outs, *scr)
        pl.when(step == n_steps - 1)(finish)

    any_spec = pl.BlockSpec(memory_space=pl.ANY)
    res = pl.pallas_call(
        wrapped, name=name, grid=grid, in_specs=list(in_specs) + [any_spec] * n_c,
        out_specs=list(out_specs) + [any_spec] * n_c, out_shape=list(out_shape) + recv_shape,
        scratch_shapes=list(scratch_shapes) + [pltpu.SemaphoreType.DMA((n_c, N_DEV - 1)),
                                               pltpu.SemaphoreType.DMA((n_c, N_DEV - 1)), pltpu.SemaphoreType.DMA((n_c,))],
        compiler_params=_params(("arbitrary",) * len(grid)),
    )(*args, *carry)
    return res


def _matmul(a, b, *, mode, out_dtype, name, add=None, carry=None):
    if mode == "nn":
        (M, K), N = a.shape, b.shape[1]
    elif mode == "nt":
        (M, K), N = a.shape, b.shape[0]
    else:
        (K, M), N = a.shape, b.shape[1]
    assert a.dtype == BF16 and b.dtype == BF16
    has_add = add is not None
    tn = _tile(N, 512)
    for tm in ((_tile(M, 512),) if mode == "tn" else (_tile(M, 2048), _tile(M, 1024))):
        fixed = 2 * tm * tn * (jnp.dtype(out_dtype).itemsize + (4 if has_add else 0)) + tm * tn * 4
        per_k = 2 * 2 * (tm + tn) + (2 * tm if mode == "tn" else 0)
        tk = _tile(K, max(LANES, (MATMUL_VMEM - fixed) // per_k))
        if tk == K:
            break
    nk = K // tk
    dims = _NT if mode == "nt" else _NN
    if mode == "tn":
        a_spec = pl.BlockSpec((tk, tm), lambda i, j, k: (jnp.where(j == 0, k, 0), i))
    else:
        a_spec = pl.BlockSpec((tm, tk), lambda i, j, k: (i, k))
    b_spec = pl.BlockSpec((tn, tk), lambda i, j, k: (j, k)) if mode == "nt" else pl.BlockSpec((tk, tn), lambda i, j, k: (k, j))
    o_spec = pl.BlockSpec((tm, tn), lambda i, j, k: (i, j))

    def body(*refs):
        a_ref, b_ref = refs[:2]
        add_ref = refs[2] if has_add else None
        o_ref = refs[2 + has_add]
        scratch = list(refs[3 + has_add:])
        at = scratch.pop(0) if mode == "tn" else None
        acc = scratch.pop(0) if nk > 1 else None
        j, k = pl.program_id(1), pl.program_id(2)
        if mode == "tn":
            @pl.when(j == 0)
            def _():
                at[k] = a_ref[...].T

            part = _dot(at[k], b_ref[...], dims)
        else:
            part = _dot(a_ref[...], b_ref[...], dims)

        def finish(r):
            if has_add:
                r = r + add_ref[...]
            o_ref[...] = r.astype(out_dtype)

        if nk == 1:
            finish(part)
        else:
            @pl.when(k == 0)
            def _():
                acc[...] = part

            @pl.when(jnp.logical_and(k > 0, k < nk - 1))
            def _():
                acc[...] += part

            @pl.when(k == nk - 1)
            def _():
                finish(acc[...] + part)

    scratch = ([pltpu.VMEM((nk, tm, tk), BF16)] if mode == "tn" else []) + ([pltpu.VMEM((tm, tn), F32)] if nk > 1 else [])
    res = _call(
        body, name=name, grid=(M // tm, N // tn, nk),
        in_specs=[a_spec, b_spec] + ([o_spec] if has_add else []),
        out_specs=[o_spec], out_shape=[jax.ShapeDtypeStruct((M, N), out_dtype)],
        scratch_shapes=scratch, sem=("parallel", "arbitrary", "arbitrary"),
        args=(a, b, add) if has_add else (a, b), carry=carry)
    return res[0] if carry is None else (res[0], res[1:])


def _rms_fwd(x, gain, name, carry=None):
    T, D = x.shape
    tr = 256

    def body(x_ref, g_ref, h_ref):
        xv = x_ref[...]
        rstd = lax.rsqrt(jnp.mean(xv * xv, axis=1, keepdims=True) + EPS)
        h_ref[...] = (xv * rstd * g_ref[...]).astype(BF16)

    res = _call(
        body, name=name, grid=(T // tr,),
        in_specs=[pl.BlockSpec((tr, D), lambda i: (i, 0)), pl.BlockSpec((1, D), lambda i: (0, 0))],
        out_specs=[pl.BlockSpec((tr, D), lambda i: (i, 0))],
        out_shape=[jax.ShapeDtypeStruct((T, D), BF16)], sem=("parallel",), args=(x, gain), carry=carry)
    return res[0] if carry is None else (res[0], res[1:])


def _rms_bwd(x, gain, dh, dres, name):
    T, D = x.shape
    tr = 256

    def body(x_ref, g_ref, dh_ref, dr_ref, dx_ref, dxb_ref, dg_ref):
        @pl.when(pl.program_id(0) == 0)
        def _():
            dg_ref[...] = jnp.zeros_like(dg_ref)

        xv = x_ref[...]
        rstd = lax.rsqrt(jnp.mean(xv * xv, axis=1, keepdims=True) + EPS)
        xhat = xv * rstd
        dy = dh_ref[...].astype(F32)
        dg_ref[...] += jnp.sum(dy * xhat, axis=0, keepdims=True)
        dxh = dy * g_ref[...]
        dx = dr_ref[...] + rstd * (dxh - xhat * jnp.mean(dxh * xhat, axis=1, keepdims=True))
        dx_ref[...] = dx
        dxb_ref[...] = dx.astype(BF16)

    row = pl.BlockSpec((tr, D), lambda i: (i, 0))
    vec = pl.BlockSpec((1, D), lambda i: (0, 0))
    return pl.pallas_call(
        body, name=name, grid=(T // tr,), in_specs=[row, vec, row, row], out_specs=[row, row, vec],
        out_shape=[jax.ShapeDtypeStruct((T, D), F32), jax.ShapeDtypeStruct((T, D), BF16),
                   jax.ShapeDtypeStruct((1, D), F32)],
        compiler_params=_params(("arbitrary",)),
    )(x, gain, dh, dres)


def _loss_head(y, target):
    T, D = y.shape
    tr = 256

    def body(y_ref, t_ref, s_ref, d_ref, db_ref):
        @pl.when(pl.program_id(0) == 0)
        def _():
            s_ref[...] = jnp.zeros_like(s_ref)

        e = y_ref[...] - t_ref[...]
        s_ref[...] += jnp.sum(jnp.sum(e * e, axis=1, keepdims=True), axis=0, keepdims=True)
        d = e * (1.0 / D)
        d_ref[...] = d
        db_ref[...] = d.astype(BF16)

    row = pl.BlockSpec((tr, D), lambda i: (i, 0))
    return pl.pallas_call(
        body, name="loss_head", grid=(T // tr,), in_specs=[row, row],
        out_specs=[pl.BlockSpec((1, 1), lambda i: (0, 0)), row, row],
        out_shape=[jax.ShapeDtypeStruct((1, 1), F32), jax.ShapeDtypeStruct((T, D), F32),
                   jax.ShapeDtypeStruct((T, D), BF16)],
        compiler_params=_params(("arbitrary",)),
    )(y, target)


def _ffn_up(h, wg, wu, name, carry=None):
    (T, D), Fd = h.shape, wg.shape[1]
    tm, tn = _tile(T, 2048), _tile(Fd, 512)

    def body(h_ref, wg_ref, wu_ref, fg_ref, fu_ref, a_ref):
        hv = h_ref[...]
        g = _dot(hv, wg_ref[...])
        s = jax.nn.sigmoid(g)
        silu = g * s
        fu_ref[...] = silu.astype(BF16)
        u = _dot(hv, wu_ref[...])
        fg_ref[...] = (u * (s + silu * (1.0 - s))).astype(BF16)
        a_ref[...] = (silu * u).astype(BF16)

    w_spec = pl.BlockSpec((D, tn), lambda i, j: (0, j))
    o_spec = pl.BlockSpec((tm, tn), lambda i, j: (i, j))
    res = _call(
        body, name=name, grid=(T // tm, Fd // tn),
        in_specs=[pl.BlockSpec((tm, D), lambda i, j: (i, 0)), w_spec, w_spec],
        out_specs=[o_spec, o_spec, o_spec],
        out_shape=[jax.ShapeDtypeStruct((T, Fd), BF16)] * 3,
        sem=("parallel", "arbitrary"), args=(h, wg, wu), carry=carry)
    return res if carry is None else (*res[:3], res[3:])


def _ffn_dact(dy, wd, fgate, fup, name):
    (T, D), Fd = dy.shape, wd.shape[0]
    tm, tn = _tile(T, 2048), _tile(Fd, 512)

    def body(dy_ref, wd_ref, fg_ref, fu_ref, dg_ref, du_ref):
        da = _dot(dy_ref[...], wd_ref[...], _NT)
        du_ref[...] = (da * fu_ref[...].astype(F32)).astype(BF16)
        dg_ref[...] = (da * fg_ref[...].astype(F32)).astype(BF16)

    o_spec = pl.BlockSpec((tm, tn), lambda i, j: (i, j))
    return pl.pallas_call(
        body, name=name, grid=(T // tm, Fd // tn),
        in_specs=[pl.BlockSpec((tm, D), lambda i, j: (i, 0)), pl.BlockSpec((tn, D), lambda i, j: (j, 0)),
                  o_spec, o_spec],
        out_specs=[o_spec, o_spec],
        out_shape=[jax.ShapeDtypeStruct((T, Fd), BF16), jax.ShapeDtypeStruct((T, Fd), BF16)],
        compiler_params=_params(("parallel", "arbitrary")),
    )(dy, wd, fgate, fup)


def _is_a(shape):
    return lax.broadcasted_iota(jnp.int32, shape, 1) % LANES < HEAD


def _split2(v):
    hi = v.astype(BF16)
    return hi, (v - hi.astype(F32)).astype(BF16)


def _split3(v):
    hi = v.astype(BF16)
    r = v - hi.astype(F32)
    mid = r.astype(BF16)
    return hi, mid, (r - mid.astype(F32)).astype(BF16)


def _dot_split(v, m, pieces, dims=_NN):
    parts = _split3(v) if pieces == 3 else _split2(v)
    out = _dot(parts[0], m, dims)
    for p in parts[1:]:
        out = out + _dot(p, m, dims)
    return out


def _head_blockdiag():
    r = lax.broadcasted_iota(jnp.int32, (LANES, LANES), 0) // HEAD
    c = lax.broadcasted_iota(jnp.int32, (LANES, LANES), 1) // HEAD
    return (r == c).astype(BF16)


def _swap_half(v):
    lane = lax.broadcasted_iota(jnp.int32, v.shape, 1)
    return jnp.where(lane % HEAD < HEAD // 2, pltpu.roll(v, LANES - HEAD // 2, axis=1), pltpu.roll(v, HEAD // 2, axis=1))


def _attn_prep_fwd(qkv, cos, sin, gq, gk, D, HKV):
    T, QW = qkv.shape
    tr = 256
    nq, nk = D // LANES, HKV // 2
    KW = HKV * LANES

    def body(x_ref, cos_ref, sin_ref, gq_ref, gk_ref, q_ref, k_ref, v_ref):
        bd = _head_blockdiag()
        cs, sn = cos_ref[...], sin_ref[...]
        isa = _is_a((tr, LANES))

        def normrope(xv, g):
            ms = _dot_split(xv * xv, bd, 2) * (1.0 / HEAD)
            xn = xv * lax.rsqrt(ms + EPS) * g
            return xn * cs + _swap_half(xn) * sn

        def dup(v):
            r = pltpu.roll(v, HEAD, axis=1)
            return jnp.where(isa, v, r), jnp.where(isa, r, v)

        for s in range(nq):
            sl = slice(s * LANES, (s + 1) * LANES)
            q_ref[:, sl] = normrope(x_ref[:, sl], gq_ref[...]).astype(BF16)
        for s in range(nk):
            ka, kb = dup(normrope(x_ref[:, D + s * LANES:D + (s + 1) * LANES], gk_ref[...]))
            k_ref[:, 2 * s * LANES:(2 * s + 1) * LANES] = ka.astype(BF16)
            k_ref[:, (2 * s + 1) * LANES:(2 * s + 2) * LANES] = kb.astype(BF16)
            va, vb = dup(x_ref[:, D + (nk + s) * LANES:D + (nk + s + 1) * LANES])
            v_ref[:, 2 * s * LANES:(2 * s + 1) * LANES] = va.astype(BF16)
            v_ref[:, (2 * s + 1) * LANES:(2 * s + 2) * LANES] = vb.astype(BF16)

    tab = pl.BlockSpec((tr, LANES), lambda i: (i, 0))
    vec = pl.BlockSpec((1, LANES), lambda i: (0, 0))
    return pl.pallas_call(
        body, name="attn_prep_fwd", grid=(T // tr,),
        in_specs=[pl.BlockSpec((tr, QW), lambda i: (i, 0)), tab, tab, vec, vec],
        out_specs=[pl.BlockSpec((tr, D), lambda i: (i, 0)), pl.BlockSpec((tr, KW), lambda i: (i, 0)),
                   pl.BlockSpec((tr, KW), lambda i: (i, 0))],
        out_shape=[jax.ShapeDtypeStruct((T, D), BF16), jax.ShapeDtypeStruct((T, KW), BF16),
                   jax.ShapeDtypeStruct((T, KW), BF16)],
        compiler_params=_params(("parallel",)),
    )(qkv, cos, sin, gq, gk)


def _attn_prep_bwd(qkv, dq, dkd, dvd, cos, sin, gq, gk, D, HKV):
    T, QW = qkv.shape
    tr = 256
    nq, nk = D // LANES, HKV // 2
    KW = HKV * LANES

    def body(x_ref, dq_ref, dk_ref, dv_ref, cos_ref, sin_ref, gq_ref, gk_ref, o_ref, dgq_ref, dgk_ref):
        @pl.when(pl.program_id(0) == 0)
        def _():
            dgq_ref[...] = jnp.zeros_like(dgq_ref)
            dgk_ref[...] = jnp.zeros_like(dgk_ref)

        bd = _head_blockdiag()
        cs, sn = cos_ref[...], sin_ref[...]
        isa = _is_a((tr, LANES))

        def back(xv, dy, g):
            rstd = lax.rsqrt(_dot_split(xv * xv, bd, 2) * (1.0 / HEAD) + EPS)
            xhat = xv * rstd
            dxn = dy * cs + _swap_half(dy * sn)
            dxh = dxn * g
            mean = _dot_split(dxh * xhat, bd, 2) * (1.0 / HEAD)
            return rstd * (dxh - xhat * mean), jnp.sum(dxn * xhat, axis=0, keepdims=True)

        def fold(s):
            a = dk_ref[:, 2 * s * LANES:(2 * s + 1) * LANES]
            b = dk_ref[:, (2 * s + 1) * LANES:(2 * s + 2) * LANES]
            return jnp.where(isa, a + pltpu.roll(a, HEAD, axis=1), b + pltpu.roll(b, HEAD, axis=1))

        def foldv(s):
            a = dv_ref[:, 2 * s * LANES:(2 * s + 1) * LANES]
            b = dv_ref[:, (2 * s + 1) * LANES:(2 * s + 2) * LANES]
            return jnp.where(isa, a + pltpu.roll(a, HEAD, axis=1), b + pltpu.roll(b, HEAD, axis=1))

        dgq = jnp.zeros((1, LANES), F32)
        for s in range(nq):
            sl = slice(s * LANES, (s + 1) * LANES)
            dx, dg = back(x_ref[:, sl], dq_ref[:, sl], gq_ref[...])
            o_ref[:, sl] = dx.astype(BF16)
            dgq = dgq + dg
        dgq_ref[...] += dgq
        dgk = jnp.zeros((1, LANES), F32)
        for s in range(nk):
            sl = slice(D + s * LANES, D + (s + 1) * LANES)
            dx, dg = back(x_ref[:, sl], fold(s), gk_ref[...])
            o_ref[:, sl] = dx.astype(BF16)
            dgk = dgk + dg
            o_ref[:, D + (nk + s) * LANES:D + (nk + s + 1) * LANES] = foldv(s).astype(BF16)
        dgk_ref[...] += dgk

    tab = pl.BlockSpec((tr, LANES), lambda i: (i, 0))
    vec = pl.BlockSpec((1, LANES), lambda i: (0, 0))
    kv = pl.BlockSpec((tr, KW), lambda i: (i, 0))
    return pl.pallas_call(
        body, name="attn_prep_bwd", grid=(T // tr,),
        in_specs=[pl.BlockSpec((tr, QW), lambda i: (i, 0)), pl.BlockSpec((tr, D), lambda i: (i, 0)), kv, kv,
                  tab, tab, vec, vec],
        out_specs=[pl.BlockSpec((tr, QW), lambda i: (i, 0)), vec, vec],
        out_shape=[jax.ShapeDtypeStruct((T, QW), BF16), jax.ShapeDtypeStruct((1, LANES), F32),
                   jax.ShapeDtypeStruct((1, LANES), F32)],
        compiler_params=_params(("arbitrary",)),
    )(qkv, dq, dkd, dvd, cos, sin, gq, gk)


def _attn_probs(qs, kw, sink_ref, first, scale):
    rows = qs.shape[0]
    qi = lax.broadcasted_iota(jnp.int32, (rows, 2 * WINDOW), 0) % WINDOW
    kj = lax.broadcasted_iota(jnp.int32, (rows, 2 * WINDOW), 1)
    valid = (kj > qi) & (kj <= qi + WINDOW)
    if first is not False:
        valid = valid & jnp.logical_or(jnp.logical_not(first), kj >= WINDOW)
    isa = _is_a(kw.shape)
    out = []
    for pos in (0, 1):
        kp = jnp.where(isa if pos == 0 else ~isa, kw, jnp.zeros_like(kw))
        s = jnp.where(valid, _dot(qs, kp, _NT) * scale, -jnp.inf)
        sink = sink_ref[0, pos]
        m = jnp.maximum(jnp.max(s, axis=1, keepdims=True), sink)
        p = jnp.exp(s - m)
        ps = jnp.exp(sink - m)
        inv = 1.0 / (jnp.sum(p, axis=1, keepdims=True) + ps)
        out.append((p * inv, ps * inv, kp))
    return out


def _attn_specs(qb):
    q = pl.BlockSpec((qb * WINDOW, ATT_GW), lambda g, n: (n, g))
    cur = pl.BlockSpec((qb * WINDOW, LANES), lambda g, n: (n, g))
    prev = pl.BlockSpec((WINDOW, LANES), lambda g, n: (jnp.maximum(qb * n - 1, 0), g))
    sink = pl.BlockSpec((1, 2, ATT_GW, 1), lambda g, n: (g, 0, 0, 0))
    return q, cur, prev, sink


def _stack(ref, s):
    rows = slice(s * WINDOW, (s + 1) * WINDOW)
    return jnp.concatenate([ref[rows, i * LANES:(i + 1) * LANES] for i in range(ATT_GW // LANES)], axis=0)


def _attn_fwd(q, kd, vd, sinkcol, HKV, carry=None):
    T, D = q.shape
    nb = T // WINDOW
    qb = math.gcd(nb, ATT_STEP_BLOCKS)
    scale = HEAD ** -0.5

    def body(q_ref, kp_ref, kc_ref, vp_ref, vc_ref, sink_ref, o_ref):
        n = pl.program_id(1)
        kall = jnp.concatenate([kp_ref[...], kc_ref[...]], axis=0)
        vall = jnp.concatenate([vp_ref[...], vc_ref[...]], axis=0)
        isa = _is_a((2 * WINDOW, LANES))
        for s in range(qb):
            win = slice(s * WINDOW, (s + 2) * WINDOW)
            kw, vw = kall[win], vall[win]
            o = jnp.zeros((ATT_GW, LANES), F32)
            first = (n == 0) if s == 0 else False
            for pos, (probs, _, _) in enumerate(_attn_probs(_stack(q_ref, s), kw, sink_ref, first, scale)):
                vp = jnp.where(isa if pos == 0 else ~isa, vw, jnp.zeros_like(vw))
                o = o + _dot(probs.astype(BF16), vp)
            for i in range(ATT_GW // LANES):
                o_ref[s * WINDOW:(s + 1) * WINDOW, i * LANES:(i + 1) * LANES] = o[i * WINDOW:(i + 1) * WINDOW].astype(BF16)

    qs_, cur, prev, sink = _attn_specs(qb)
    res = _call(
        body, name="attn_fwd", grid=(HKV, nb // qb), in_specs=[qs_, prev, cur, prev, cur, sink], out_specs=[qs_],
        out_shape=[jax.ShapeDtypeStruct((T, D), BF16)], sem=("parallel", "parallel"),
        args=(q, kd, kd, vd, vd, sinkcol), carry=carry)
    return res[0] if carry is None else (res[0], res[1:])


def _attn_bwd(q, kd, vd, o, do, sinkcol, HKV, carry=None):
    T, D = q.shape
    nb = T // WINDOW
    qb = math.gcd(nb, ATT_STEP_BLOCKS)
    scale = HEAD ** -0.5
    KW = HKV * LANES

    def body(q_ref, kp_ref, kc_ref, vp_ref, vc_ref, o_ref, do_ref, sink_ref, dq_ref, dk_ref, dv_ref, ds_ref):
        n = pl.program_id(1)

        @pl.when(n == 0)
        def _():
            dk_ref[...] = jnp.zeros_like(dk_ref)
            dv_ref[...] = jnp.zeros_like(dv_ref)
            ds_ref[...] = jnp.zeros_like(ds_ref)

        kall = jnp.concatenate([kp_ref[...], kc_ref[...]], axis=0)
        vall = jnp.concatenate([vp_ref[...], vc_ref[...]], axis=0)
        isa_w = _is_a((2 * WINDOW, LANES))
        isa_q = _is_a((ATT_GW, LANES))
        for s in range(qb):
            win = slice(s * WINDOW, (s + 2) * WINDOW)
            kw, vw = kall[win], vall[win]
            qs = _stack(q_ref, s)
            dos = _stack(do_ref, s)
            dd = dos * _stack(o_ref, s).astype(F32)
            dob = dos.astype(BF16)
            dqs = jnp.zeros((ATT_GW, LANES), F32)
            dkw, dvw, dsk = [], [], []
            first = (n == 0) if s == 0 else False
            for pos, (probs, psink, kp) in enumerate(_attn_probs(qs, kw, sink_ref, first, scale)):
                sel_w = isa_w if pos == 0 else ~isa_w
                sel_q = isa_q if pos == 0 else ~isa_q
                delta = jnp.sum(jnp.where(sel_q, dd, 0.0), axis=1, keepdims=True)
                vp = jnp.where(sel_w, vw, jnp.zeros_like(vw))
                dp = _dot(dob, vp, _NT)
                dsb = (probs * (dp - delta) * scale).astype(BF16)
                dqs = dqs + _dot(dsb, kp)
                dkw.append(_dot(dsb, qs, _TN))
                dvw.append(_dot(probs.astype(BF16), dob, _TN))
                dsk.append(-psink * delta)
            dkw = jnp.where(isa_w, dkw[0], dkw[1])
            dvw = jnp.where(isa_w, dvw[0], dvw[1])

            def add_window(dkw=dkw, dvw=dvw, s=s):
                start = pl.multiple_of((qb * n + s - 1) * WINDOW, WINDOW)
                dk_ref[pl.ds(start, 2 * WINDOW), :] += dkw
                dv_ref[pl.ds(start, 2 * WINDOW), :] += dvw

            if s == 0:
                @pl.when(n == 0)
                def _(dkw=dkw, dvw=dvw):
                    dk_ref[0:WINDOW, :] += dkw[WINDOW:]
                    dv_ref[0:WINDOW, :] += dvw[WINDOW:]

                pl.when(n > 0)(add_window)
            else:
                add_window()

            rows = []
            for i in range(ATT_GW // LANES):
                dq_ref[s * WINDOW:(s + 1) * WINDOW, i * LANES:(i + 1) * LANES] = dqs[i * WINDOW:(i + 1) * WINDOW]
                for pos in (0, 1):
                    t = jnp.sum(dsk[pos][i * WINDOW:(i + 1) * WINDOW], axis=0, keepdims=True)
                    rows.append(jnp.broadcast_to(t, (1, LANES)))
            ds_ref[0] += jnp.concatenate(rows, axis=0)

    qs_, cur, prev, sink = _attn_specs(qb)
    dqo = pl.BlockSpec((qb * WINDOW, ATT_GW), lambda g, n: (n, g))
    dkv = pl.BlockSpec((T, LANES), lambda g, n: (0, g))
    res = _call(
        body, name="attn_bwd", grid=(HKV, nb // qb),
        in_specs=[qs_, prev, cur, prev, cur, qs_, dqo, sink],
        out_specs=[dqo, dkv, dkv, pl.BlockSpec((1, ATT_GROUP, LANES), lambda g, n: (g, 0, 0))],
        out_shape=[jax.ShapeDtypeStruct((T, D), F32), jax.ShapeDtypeStruct((T, KW), F32),
                   jax.ShapeDtypeStruct((T, KW), F32), jax.ShapeDtypeStruct((HKV, ATT_GROUP, LANES), F32)],
        sem=("parallel", "arbitrary"), args=(q, kd, kd, vd, vd, o, do, sinkcol), carry=carry)
    return res if carry is None else (*res[:4], res[4:])


def _conv_fwd(zx, w, b, DI, CD):
    T = zx.shape[0]
    cw, tc = _tile(math.gcd(DI, CD), 512), 512
    off = DI // cw

    def body(cur_ref, halo_ref, w_ref, b_ref, o_ref):
        i = pl.program_id(1)
        halo = jnp.where(i > 0, halo_ref[...], 0.0)
        ext = jnp.concatenate([halo, cur_ref[...]], axis=0)
        acc = b_ref[...] + w_ref[SSM_CONV - 1:SSM_CONV, :] * ext[HALO:]
        for k in range(SSM_CONV - 1):
            acc = acc + w_ref[k:k + 1, :] * pltpu.roll(ext, SSM_CONV - 1 - k, axis=0)[HALO:]
        o_ref[...] = acc * jax.nn.sigmoid(acc)

    return pl.pallas_call(
        body, name="ssm_conv_fwd", grid=(CD // cw, T // tc),
        in_specs=[pl.BlockSpec((tc, cw), lambda j, i: (i, off + j)),
                  pl.BlockSpec((HALO, cw), lambda j, i: (jnp.maximum(i * (tc // HALO) - 1, 0), off + j)),
                  pl.BlockSpec((SSM_CONV, cw), lambda j, i: (0, j)), pl.BlockSpec((1, cw), lambda j, i: (0, j))],
        out_specs=pl.BlockSpec((tc, cw), lambda j, i: (i, j)),
        out_shape=jax.ShapeDtypeStruct((T, CD), F32),
        compiler_params=_params(("parallel", "parallel")),
    )(zx, zx, w, b)


def _conv_bwd(zx, dparts, dzx, w, b, DI, CD):
    T = zx.shape[0]
    cw, tc = _tile(math.gcd(DI, CD), 512), 512
    off = DI // cw
    nt = T // tc
    hb = tc // HALO
    ends = [0]
    for p in dparts:
        assert p.shape[1] % cw == 0
        ends.append(ends[-1] + p.shape[1] // cw)
    assert ends[-1] == CD // cw
    n_p = len(dparts)

    def body(*refs):
        cur_ref, prev_ref, next_ref = refs[:3]
        d_refs, dn_refs = refs[3:3 + n_p], refs[3 + n_p:3 + 2 * n_p]
        w_ref, b_ref, _, o_ref, dw_ref, db_ref = refs[3 + 2 * n_p:]
        j, i = pl.program_id(0), pl.program_id(1)

        @pl.when(i == 0)
        def _():
            dw_ref[...] = jnp.zeros_like(dw_ref)
            db_ref[...] = jnp.zeros_like(db_ref)

        def pick(prefs):
            v = prefs[n_p - 1][...]
            for p in range(n_p - 2, -1, -1):
                v = jnp.where(j < ends[p + 1], prefs[p][...], v)
            return v

        prev = jnp.where(i > 0, prev_ref[...], 0.0)
        ext = jnp.concatenate([prev, cur_ref[...], next_ref[...]], axis=0)
        u = b_ref[...] + w_ref[SSM_CONV - 1:SSM_CONV, :] * ext
        for k in range(SSM_CONV - 1):
            u = u + w_ref[k:k + 1, :] * pltpu.roll(ext, SSM_CONV - 1 - k, axis=0)
        u = u[HALO:]
        dnext = jnp.where(i < nt - 1, pick(dn_refs), 0.0)
        dxe = jnp.concatenate([pick(d_refs), dnext], axis=0)
        sg = jax.nn.sigmoid(u)
        du = dxe * sg * (1.0 + u * (1.0 - sg))
        n_e = tc + HALO
        dx = w_ref[SSM_CONV - 1:SSM_CONV, :] * du
        for k in range(SSM_CONV - 1):
            dx = dx + w_ref[k:k + 1, :] * pltpu.roll(du, n_e - (SSM_CONV - 1 - k), axis=0)
        o_ref[...] = dx[:tc].astype(BF16)
        duc = du[:tc]
        db_ref[...] += jnp.sum(duc, axis=0, keepdims=True)
        xs = ext[:n_e]
        dws = []
        for k in range(SSM_CONV):
            sh = xs if k == SSM_CONV - 1 else pltpu.roll(xs, SSM_CONV - 1 - k, axis=0)
            dws.append(jnp.sum(duc * sh[HALO:], axis=0, keepdims=True))
        dw_ref[...] += jnp.concatenate(dws, axis=0)

    def part_specs(p):
        lo, n = ends[p], ends[p + 1] - ends[p]

        def inside(j):
            return jnp.logical_and(j >= lo, j < lo + n)

        col = lambda j: jnp.clip(j - lo, 0, n - 1)
        return (pl.BlockSpec((tc, cw), lambda j, i: (jnp.where(inside(j), i, 0), col(j))),
                pl.BlockSpec((HALO, cw), lambda j, i: (jnp.where(inside(j), jnp.minimum((i + 1) * hb, nt * hb - 1), 0), col(j))))

    specs = [part_specs(p) for p in range(n_p)]
    return pl.pallas_call(
        body, name="ssm_conv_bwd", grid=(CD // cw, nt),
        in_specs=[pl.BlockSpec((tc, cw), lambda j, i: (i, off + j)),
                  pl.BlockSpec((HALO, cw), lambda j, i: (jnp.maximum(i * hb - 1, 0), off + j)),
                  pl.BlockSpec((HALO, cw), lambda j, i: (jnp.minimum((i + 1) * hb, nt * hb - 1), off + j))]
        + [s[0] for s in specs] + [s[1] for s in specs]
        + [pl.BlockSpec((SSM_CONV, cw), lambda j, i: (0, j)), pl.BlockSpec((1, cw), lambda j, i: (0, j)),
           pl.BlockSpec(memory_space=pl.ANY)],
        out_specs=[pl.BlockSpec((tc, cw), lambda j, i: (i, off + j)), pl.BlockSpec((SSM_CONV, cw), lambda j, i: (0, j)),
                   pl.BlockSpec((1, cw), lambda j, i: (0, j))],
        out_shape=[jax.ShapeDtypeStruct(dzx.shape, BF16), jax.ShapeDtypeStruct((SSM_CONV, CD), F32),
                   jax.ShapeDtypeStruct((1, CD), F32)],
        input_output_aliases={5 + 2 * n_p: 0},
        compiler_params=_params(("parallel", "arbitrary")),
    )(zx, zx, zx, *dparts, *dparts, w, b, dzx)


def _tri_dot(v, upper):
    L = v.shape[0]
    r = lax.broadcasted_iota(jnp.int32, (L, L), 0)
    c = lax.broadcasted_iota(jnp.int32, (L, L), 1)
    tri = ((r <= c) if upper else (r >= c)).astype(BF16)
    p = _split3(v)
    return _dot(tri, p[0]) + _dot(tri, p[1]) + _dot(tri, p[2])


def _ssd_time2(dtraw_ref, bias_ref, alog_ref, sel):
    dt = jax.nn.softplus(dtraw_ref[...] + bias_ref[...])
    acum = _tri_dot(dt * (-jnp.exp(alog_ref[...])), False)
    return dt, _dot_split(dt, sel, 3), _dot_split(acum, sel, 3)


def _decay(acs, acs_t, pos):
    L = acs.shape[0]
    r = lax.broadcasted_iota(jnp.int32, (L, L), 0)
    c = lax.broadcasted_iota(jnp.int32, (L, L), 1)
    col = acs[:, HEAD * pos:HEAD * pos + 1]
    row = acs_t[HEAD * pos:HEAD * pos + 1, :]
    return jnp.exp(jnp.where(r >= c, col - row, -jnp.inf))


def _ssd_specs(G, GW, DI, ZW):
    L = SSM_CHUNK
    grp = lambda f: pl.BlockSpec((L, GW), lambda g, c: (f(c), g))
    return dict(
        grp=grp,
        bmat=lambda f: pl.BlockSpec((L, SSM_STATE), lambda g, c: (f(c), DI // SSM_STATE + g)),
        cmat=lambda f: pl.BlockSpec((L, SSM_STATE), lambda g, c: (f(c), DI // SSM_STATE + G + g)),
        dtraw=lambda f: pl.BlockSpec((L, LANES), lambda g, c: (f(c), (2 * DI + 2 * G * SSM_STATE) // LANES)),
        vec=pl.BlockSpec((1, LANES), lambda g, c: (0, 0)),
        gvec=pl.BlockSpec((1, GW), lambda g, c: (0, g)),
        sel=pl.BlockSpec((1, LANES, GW), lambda g, c: (g, 0, 0)),
    )


def _ssd_fwd(zx, xc, bias, alog, sel, dskip, ng, DI, carry=None):
    T, ZW = zx.shape
    G, L = SSM_GROUPS, SSM_CHUNK
    GW = DI // G
    NS = GW // LANES
    nc = T // L
    sp = _ssd_specs(G, GW, DI, ZW)
    ident = lambda c: c

    def body(x_ref, b_ref, c_ref, z_ref, dtraw_ref, bias_ref, alog_ref, sel_ref, d_ref, ng_ref,
             y_ref, yo_ref, st_ref, state):
        c = pl.program_id(1)

        @pl.when(c == 0)
        def _():
            state[...] = jnp.zeros_like(state)

        x = x_ref[...]
        bb, cb_ = b_ref[...].astype(BF16), c_ref[...].astype(BF16)
        cbm = _dot(cb_, bb, _NT)
        _, dtx, acx = _ssd_time2(dtraw_ref, bias_ref, alog_ref, sel_ref[0])
        xdt = x * dtx
        ex = jnp.exp(acx)
        last = acx[L - 1:L, :]
        te = jnp.exp(last - acx)
        dlast = jnp.exp(last)
        isa = _is_a((L, LANES))
        for i in range(NS):
            sl = slice(i * LANES, (i + 1) * LANES)
            acs = acx[:, sl]
            acs_t = acs.T
            xs = xdt[:, sl]
            y = jnp.zeros((L, LANES), F32)
            for pos in (0, 1):
                m = (cbm * _decay(acs, acs_t, pos)).astype(BF16)
                y = y + _dot(m, jnp.where(isa if pos == 0 else ~isa, xs, 0.0).astype(BF16))
            st = state[i]
            st_ref[0, i] = st
            y = y + _dot(cb_, st.astype(BF16)) * ex[:, sl]
            state[i] = st * dlast[:, sl] + _dot(bb, (xs * te[:, sl]).astype(BF16), _TN)
            y_ref[:, sl] = y + d_ref[:, sl] * x[:, sl]
        z = z_ref[...]
        gated = y_ref[...] * (z * jax.nn.sigmoid(z))
        rstd = lax.rsqrt(jnp.mean(gated * gated, axis=1, keepdims=True) + EPS)
        yo_ref[...] = (gated * rstd * ng_ref[...]).astype(BF16)

    res = _call(
        body, name="ssd_fwd", grid=(G, nc),
        in_specs=[sp["grp"](ident), sp["bmat"](ident), sp["cmat"](ident), sp["grp"](ident), sp["dtraw"](ident),
                  sp["vec"], sp["vec"], sp["sel"], sp["gvec"], sp["gvec"]],
        out_specs=[sp["grp"](ident), sp["grp"](ident),
                   pl.BlockSpec((1, NS, SSM_STATE, LANES), lambda g, c: (c, g, 0, 0))],
        out_shape=[jax.ShapeDtypeStruct((T, DI), F32), jax.ShapeDtypeStruct((T, DI), BF16),
                   jax.ShapeDtypeStruct((nc, G * NS, SSM_STATE, LANES), F32)],
        scratch_shapes=[pltpu.VMEM((NS, SSM_STATE, LANES), F32)],
        sem=("parallel", "arbitrary"), args=(xc, xc, xc, zx, zx, bias, alog, sel, dskip, ng), carry=carry)
    return res if carry is None else (*res[:3], res[3:])


def _ssd_bwd(zx, xc, yssd, dyo, states, bias, alog, sel, dskip, ng, DI, carry=None):
    T, ZW = zx.shape
    G, L = SSM_GROUPS, SSM_CHUNK
    GW = DI // G
    NS = GW // LANES
    nc = T // L
    sp = _ssd_specs(G, GW, DI, ZW)
    rev = lambda c: nc - 1 - c

    def body(x_ref, b_ref, c_ref, z_ref, dtraw_ref, y_ref, dyo_ref, st_ref, bias_ref, alog_ref, sel_ref,
             d_ref, ng_ref, dz_ref, dx_ref, db_ref, dc_ref, ddt_ref, dac_ref, dd_ref, dng_ref, dstate):
        c = pl.program_id(1)

        @pl.when(c == 0)
        def _():
            dstate[...] = jnp.zeros_like(dstate)
            dd_ref[...] = jnp.zeros_like(dd_ref)
            dng_ref[...] = jnp.zeros_like(dng_ref)

        z, ys, dyo = z_ref[...], y_ref[...], dyo_ref[...]
        sg = jax.nn.sigmoid(z)
        sz = z * sg
        gated = ys * sz
        rstd = lax.rsqrt(jnp.mean(gated * gated, axis=1, keepdims=True) + EPS)
        yn = gated * rstd
        dng_ref[0] += jnp.sum(dyo * yn, axis=0, keepdims=True)
        dyn = dyo * ng_ref[...]
        dgated = rstd * (dyn - yn * jnp.mean(dyn * yn, axis=1, keepdims=True))
        g = dgated * sz
        dz_ref[...] = (dgated * ys * sg * (1.0 + z * (1.0 - sg))).astype(BF16)

        x = x_ref[...]
        dsk = d_ref[...]
        dd_ref[0] += jnp.sum(g * x, axis=0, keepdims=True)
        bb, cb_ = b_ref[...].astype(BF16), c_ref[...].astype(BF16)
        cbm = _dot(cb_, bb, _NT)
        _, dtx, acx = _ssd_time2(dtraw_ref, bias_ref, alog_ref, sel_ref[0])
        xdt = x * dtx
        ex = jnp.exp(acx)
        last = acx[L - 1:L, :]
        te = jnp.exp(last - acx)
        dlast = jnp.exp(last)
        isa = _is_a((L, LANES))
        is_last = lax.broadcasted_iota(jnp.int32, (L, LANES), 0) == L - 1
        strict = lax.broadcasted_iota(jnp.int32, (L, L), 0) > lax.broadcasted_iota(jnp.int32, (L, L), 1)
        lane_id = lax.broadcasted_iota(jnp.int32, (1, LANES), 1)
        row_id = lax.broadcasted_iota(jnp.int32, (8, 1), 0)
        lane_head = lax.broadcasted_iota(jnp.int32, (LANES, LANES), 0) // HEAD
        col_id = lax.broadcasted_iota(jnp.int32, (LANES, LANES), 1)
        dcb = jnp.zeros((L, L), F32)
        dcm = jnp.zeros((L, SSM_STATE), F32)
        dbm = jnp.zeros((L, SSM_STATE), F32)
        q_rows = jnp.zeros((L, LANES), F32)
        q_cols = jnp.zeros((8, L), F32)
        for i in range(NS):
            sl = slice(i * LANES, (i + 1) * LANES)
            acs = acx[:, sl]
            acs_t = acs.T
            xs, gs = xdt[:, sl], g[:, sl]
            xsb = xs.astype(BF16)
            dxd = jnp.zeros((L, LANES), F32)
            for pos in (0, 1):
                gp = jnp.where(isa if pos == 0 else ~isa, gs, 0.0).astype(BF16)
                dec = _decay(acs, acs_t, pos)
                dxd = dxd + _dot((cbm * dec).astype(BF16), gp, _TN)
                dmd = _dot(gp, xsb, _NT) * dec
                dcb = dcb + dmd
                q = jnp.where(strict, dmd * cbm, 0.0)
                q_rows = q_rows + jnp.sum(q, axis=1, keepdims=True) * (lane_id == 2 * i + pos).astype(F32)
                q_cols = q_cols + jnp.sum(q, axis=0, keepdims=True) * (row_id == 2 * i + pos).astype(F32)
            st = st_ref[0, i]
            dst = dstate[i]
            stb, dstb = st.astype(BF16), dst.astype(BF16)
            eg = (ex[:, sl] * gs).astype(BF16)
            dcm = dcm + _dot(eg, stb, _NT)
            yoff = _dot(cb_, stb) * ex[:, sl]
            w = xs * te[:, sl]
            wb = w.astype(BF16)
            dw = _dot(bb, dstb)
            dbm = dbm + _dot(wb, dstb, _NT)
            dxt = dxd + dw * te[:, sl]
            dal = dlast[:, sl] * jnp.sum(dst * st, axis=0, keepdims=True) + jnp.sum(dw * w, axis=0, keepdims=True)
            dac_l = gs * yoff - w * dw + jnp.where(is_last, dal, 0.0)
            ddt_l = dxt * x[:, sl]
            dstate[i] = dst * dlast[:, sl] + _dot(cb_, eg, _TN)
            dx_ref[:, sl] = dxt * dtx[:, sl] + dsk[:, sl] * gs
            to_head = (col_id == 2 * i + lane_head).astype(BF16)
            part = _dot_split(ddt_l, to_head, 2)
            parta = _dot_split(dac_l, to_head, 2)
            if i == 0:
                ddt_ref[0] = part
                dac_ref[0] = parta
            else:
                ddt_ref[0] += part
                dac_ref[0] += parta
        dac_ref[0] += q_rows - jnp.concatenate([q_cols, jnp.zeros((LANES - 8, L), F32)], axis=0).T
        dcbb = dcb.astype(BF16)
        dc_ref[...] = dcm + _dot(dcbb, bb)
        db_ref[...] = dbm + _dot(dcbb, cb_, _TN)

    part_spec = pl.BlockSpec((1, L, LANES), lambda g, c: (g, rev(c), 0))
    lane_spec = pl.BlockSpec((1, 1, GW), lambda g, c: (g, 0, 0))
    bc_out = pl.BlockSpec((L, SSM_STATE), lambda g, c: (rev(c), g))
    res = _call(
        body, name="ssd_bwd", grid=(G, nc),
        in_specs=[sp["grp"](rev), sp["bmat"](rev), sp["cmat"](rev), sp["grp"](rev), sp["dtraw"](rev), sp["grp"](rev),
                  sp["grp"](rev), pl.BlockSpec((1, NS, SSM_STATE, LANES), lambda g, c: (rev(c), g, 0, 0)),
                  sp["vec"], sp["vec"], sp["sel"], sp["gvec"], sp["gvec"]],
        out_specs=[sp["grp"](rev), sp["grp"](rev), bc_out, bc_out, part_spec, part_spec, lane_spec, lane_spec],
        out_shape=[jax.ShapeDtypeStruct((T, ZW), BF16), jax.ShapeDtypeStruct((T, DI), F32),
                   jax.ShapeDtypeStruct((T, G * SSM_STATE), F32), jax.ShapeDtypeStruct((T, G * SSM_STATE), F32),
                   jax.ShapeDtypeStruct((G, T, LANES), F32), jax.ShapeDtypeStruct((G, T, LANES), F32),
                   jax.ShapeDtypeStruct((G, 1, GW), F32), jax.ShapeDtypeStruct((G, 1, GW), F32)],
        scratch_shapes=[pltpu.VMEM((NS, SSM_STATE, LANES), F32)], sem=("parallel", "arbitrary"),
        args=(xc, xc, xc, zx, zx, yssd, dyo, states, bias, alog, sel, dskip, ng), carry=carry)
    return res if carry is None else (*res[:8], res[8:])


def _ssd_dt_bwd(zx, ddt_part, dac_part, dzx, bias, alog, DI):
    T, ZW = zx.shape
    G, L = SSM_GROUPS, SSM_CHUNK
    nc = T // L
    heads = DI // HEAD // G
    tail = ZW - 2 * DI - 2 * G * SSM_STATE
    dt_block = (ZW - tail) // LANES

    def body(dtraw_ref, ddt_ref, dac_ref, bias_ref, alog_ref, _, o_ref, dal_ref, dbias_ref):
        @pl.when(pl.program_id(0) == 0)
        def _():
            dal_ref[...] = jnp.zeros_like(dal_ref)
            dbias_ref[...] = jnp.zeros_like(dbias_ref)

        raw = dtraw_ref[...] + bias_ref[...]
        dt = jax.nn.softplus(raw)
        a = -jnp.exp(alog_ref[...])
        dac, ddt = dac_ref[0], ddt_ref[0]
        for gi in range(1, G):
            dac = dac + pltpu.roll(dac_ref[gi], gi * heads, axis=1)
            ddt = ddt + pltpu.roll(ddt_ref[gi], gi * heads, axis=1)
        dda = _tri_dot(dac, True)
        dal_ref[...] += jnp.sum(dda * dt, axis=0, keepdims=True) * a
        draw = (dda * a + ddt) * jax.nn.sigmoid(raw)
        dbias_ref[...] += jnp.sum(draw, axis=0, keepdims=True)
        o_ref[...] = jnp.concatenate([draw.astype(BF16), jnp.zeros((L, tail - LANES), BF16)], axis=1)

    vec = pl.BlockSpec((1, LANES), lambda c: (0, 0))
    part = pl.BlockSpec((G, L, LANES), lambda c: (0, c, 0))
    return pl.pallas_call(
        body, name="ssd_dt_bwd", grid=(nc,),
        in_specs=[pl.BlockSpec((L, LANES), lambda c: (c, dt_block)), part, part, vec, vec,
                  pl.BlockSpec(memory_space=pl.ANY)],
        out_specs=[pl.BlockSpec((L, tail), lambda c: (c, (ZW - tail) // tail)), vec, vec],
        out_shape=[jax.ShapeDtypeStruct((T, ZW), BF16), jax.ShapeDtypeStruct((1, LANES), F32),
                   jax.ShapeDtypeStruct((1, LANES), F32)],
        input_output_aliases={5: 0},
        compiler_params=_params(("arbitrary",)),
    )(zx, ddt_part, dac_part, bias, alog, dzx)


def _exchange(blocks, name):
    n = len(blocks)

    def body(*refs):
        local, sends, arrivals = _direct_copies(refs[:n], refs[n:2 * n], *refs[2 * n:])
        for cp in local + sends:
            cp.start()
        for cp in arrivals:
            cp.wait_recv()
        for cp in sends:
            cp.wait_send()
        for cp in local:
            cp.wait()

    any_spec = pl.BlockSpec(memory_space=pl.ANY)
    return pl.pallas_call(
        body, name=name, in_specs=[any_spec] * n, out_specs=[any_spec] * n,
        out_shape=[jax.ShapeDtypeStruct(b.shape, b.dtype) for b in blocks],
        scratch_shapes=[pltpu.SemaphoreType.DMA((n, 7)), pltpu.SemaphoreType.DMA((n, 7)),
                        pltpu.SemaphoreType.DMA((n,))],
    )(*blocks)


def _adamw(parts, w, m, v, name):
    nl = len(parts)
    R, C = parts[0].shape[1:]
    per_row = C * (N_DEV * nl * parts[0].dtype.itemsize + 7 * 4) * 2
    tr = R
    if R % 8 == 0:
        tr = 8
        for t in (16, 32, 64, 128, 256, 512):
            if R % t == 0 and t * per_row <= 24 * 1024 * 1024:
                tr = t
    nr = R // tr
    c1 = 1.0 - ADAM_B1 ** ADAM_STEP
    c2 = 1.0 - ADAM_B2 ** ADAM_STEP

    def body(*refs):
        p_refs = refs[:nl]
        w_ref, m_ref, v_ref, g_ref, d_ref, nm_ref, nv_ref = refs[nl:]
        for layer in range(nl):
            @pl.when(pl.program_id(0) == layer)
            def _(p_ref=p_refs[layer]):
                g = p_ref[0].astype(F32)
                for k in range(1, N_DEV):
                    g = g + p_ref[k].astype(F32)
                nm = ADAM_B1 * m_ref[...] + (1.0 - ADAM_B1) * g
                nv = ADAM_B2 * v_ref[...] + (1.0 - ADAM_B2) * (g * g)
                g_ref[...] = g
                nm_ref[...] = nm
                nv_ref[...] = nv
                d_ref[...] = -ADAM_LR * ((nm / c1) / (jnp.sqrt(nv / c2) + ADAM_EPS) + ADAM_WD * w_ref[...])

    def part_spec(layer):
        return pl.BlockSpec((N_DEV, tr, C), lambda l, i: (0, jnp.where(l == layer, i, jnp.where(l < layer, 0, nr - 1)), 0))

    blk = pl.BlockSpec((tr, C), lambda l, i: (l * nr + i, 0))
    out = jax.ShapeDtypeStruct((nl * R, C), F32)
    return pl.pallas_call(
        body, name=name, grid=(nl, nr),
        in_specs=[part_spec(layer) for layer in range(nl)] + [blk, blk, blk],
        out_specs=[blk, blk, blk, blk], out_shape=[out, out, out, out],
        compiler_params=_params(("arbitrary", "arbitrary")),
    )(*parts, w, m, v)


def _pad_cols(a, n):
    return jnp.pad(a, ((0, 0), (0, n - a.shape[1])))


def kernel(x, positions, mixer_norm, ffn_norm, attn_w_qkv, attn_q_norm, attn_k_norm, attn_sinks, attn_w_o, ssm_w_in, ssm_conv_w, ssm_conv_b, ssm_dt_bias, ssm_a_log, ssm_d, ssm_norm, ssm_w_out, ffn_w_gate, ffn_w_up, ffn_w_down, loss_target, m_mixer_norm, m_ffn_norm, m_attn_w_qkv, m_attn_q_norm, m_attn_k_norm, m_attn_sinks, m_attn_w_o, m_ssm_w_in, m_ssm_conv_w, m_ssm_conv_b, m_ssm_dt_bias, m_ssm_a_log, m_ssm_d, m_ssm_norm, m_ssm_w_out, m_ffn_w_gate, m_ffn_w_up, m_ffn_w_down, v_mixer_norm, v_ffn_norm, v_attn_w_qkv, v_attn_q_norm, v_attn_k_norm, v_attn_sinks, v_attn_w_o, v_ssm_w_in, v_ssm_conv_w, v_ssm_conv_b, v_ssm_dt_bias, v_ssm_a_log, v_ssm_d, v_ssm_norm, v_ssm_w_out, v_ffn_w_gate, v_ffn_w_up, v_ffn_w_down):
    T, D = x.shape[1], x.shape[2]
    HQ = D // HEAD
    HKV = HQ // ATT_GROUP
    QW = (HQ + 2 * HKV) * HEAD
    DI = 2 * D
    H = DI // HEAD
    G = SSM_GROUPS
    GW = DI // G
    CD = DI + 2 * G * SSM_STATE
    ZW = -(-(DI + CD + LANES) // 512) * 512
    IW = DI + CD + H
    assert T % 512 == 0 and D % 256 == 0 and HKV % 2 == 0 and GW % LANES == 0 and H <= LANES

    weights = dict(mixer_norm=mixer_norm, ffn_norm=ffn_norm, attn_w_qkv=attn_w_qkv, attn_q_norm=attn_q_norm,
                   attn_k_norm=attn_k_norm, attn_sinks=attn_sinks, attn_w_o=attn_w_o, ssm_w_in=ssm_w_in,
                   ssm_conv_w=ssm_conv_w, ssm_conv_b=ssm_conv_b, ssm_dt_bias=ssm_dt_bias, ssm_a_log=ssm_a_log,
                   ssm_d=ssm_d, ssm_norm=ssm_norm, ssm_w_out=ssm_w_out, ffn_w_gate=ffn_w_gate, ffn_w_up=ffn_w_up,
                   ffn_w_down=ffn_w_down)
    mom_m = dict(mixer_norm=m_mixer_norm, ffn_norm=m_ffn_norm, attn_w_qkv=m_attn_w_qkv, attn_q_norm=m_attn_q_norm,
                 attn_k_norm=m_attn_k_norm, attn_sinks=m_attn_sinks, attn_w_o=m_attn_w_o, ssm_w_in=m_ssm_w_in,
                 ssm_conv_w=m_ssm_conv_w, ssm_conv_b=m_ssm_conv_b, ssm_dt_bias=m_ssm_dt_bias, ssm_a_log=m_ssm_a_log,
                 ssm_d=m_ssm_d, ssm_norm=m_ssm_norm, ssm_w_out=m_ssm_w_out, ffn_w_gate=m_ffn_w_gate,
                 ffn_w_up=m_ffn_w_up, ffn_w_down=m_ffn_w_down)
    mom_v = dict(mixer_norm=v_mixer_norm, ffn_norm=v_ffn_norm, attn_w_qkv=v_attn_w_qkv, attn_q_norm=v_attn_q_norm,
                 attn_k_norm=v_attn_k_norm, attn_sinks=v_attn_sinks, attn_w_o=v_attn_w_o, ssm_w_in=v_ssm_w_in,
                 ssm_conv_w=v_ssm_conv_w, ssm_conv_b=v_ssm_conv_b, ssm_dt_bias=v_ssm_dt_bias, ssm_a_log=v_ssm_a_log,
                 ssm_d=v_ssm_d, ssm_norm=v_ssm_norm, ssm_w_out=v_ssm_w_out, ffn_w_gate=v_ffn_w_gate,
                 ffn_w_up=v_ffn_w_up, ffn_w_down=v_ffn_w_down)
    big = ["attn_w_qkv", "attn_w_o", "ssm_w_in", "ssm_w_out", "ffn_w_gate", "ffn_w_up", "ffn_w_down"]

    def flat2(a):
        return a.reshape(-1, a.shape[-1])

    def shard(n, layer=0):
        return weights[n][layer].astype(BF16)

    def from_cols(g):
        return g.transpose(1, 0, 2).reshape(g.shape[1], N_DEV * g.shape[2])

    def from_rows(g):
        return g.reshape(N_DEV * g.shape[1], g.shape[2])

    xs = x[0]
    tgt = loss_target[0]
    inv_freq = ROPE_THETA ** (-jnp.arange(0, HEAD, 2, dtype=F32) / HEAD)
    ang = positions[0].astype(F32)[:, None] * inv_freq
    cos = jnp.tile(jnp.cos(ang), (1, 4))
    sin = jnp.tile(jnp.concatenate([-jnp.sin(ang), jnp.sin(ang)], axis=1), (1, 2))
    gq = jnp.tile(attn_q_norm, (1, 2))
    gk = jnp.tile(attn_k_norm, (1, 2))
    sinkcol = jnp.repeat(attn_sinks.reshape(HKV, ATT_GROUP // 2, 2).transpose(0, 2, 1), WINDOW, axis=2)[..., None]
    bias_p = _pad_cols(ssm_dt_bias, LANES)
    alog_p = _pad_cols(ssm_a_log, LANES)
    dskip = jnp.repeat(ssm_d, HEAD, axis=1)
    lane_head = jnp.arange(DI) // HEAD
    sel = (jnp.arange(LANES)[None, :, None] == lane_head.reshape(G, 1, GW)).astype(BF16)
    vec_w = CD // N_DEV
    small = jnp.concatenate([ssm_conv_w[0], ssm_conv_b, _pad_cols(ssm_norm, vec_w),
                             jnp.zeros((2, vec_w), F32)], axis=0)

    def rows_to_blocks(p):
        return p.reshape(N_DEV, p.shape[0] // N_DEV, p.shape[1])

    def cols_to_blocks(p):
        return p.reshape(p.shape[0], N_DEV, p.shape[1] // N_DEV).transpose(1, 0, 2)

    hm0, got = _rms_fwd(xs, mixer_norm[0:1], "rms_fwd_m0", carry=[shard("attn_w_qkv")])
    w_qkv = from_cols(got[0])
    qkv, got = _matmul(hm0, w_qkv, mode="nn", out_dtype=F32, name="mm_qkv",
                       carry=[shard("attn_w_o"), shard("ffn_w_gate", 0), small])
    w_o = from_rows(got[0])
    w_gate = [from_cols(got[1]), None]
    small_all = got[2]
    conv_w = small_all[:, :SSM_CONV].transpose(1, 0, 2).reshape(SSM_CONV, CD)
    conv_b = small_all[:, SSM_CONV].reshape(1, CD)
    ng = small_all[:, SSM_CONV + 1, :DI // N_DEV].reshape(1, DI)
    qr, kd, vd = _attn_prep_fwd(qkv, cos, sin, gq, gk, D, HKV)
    o, got = _attn_fwd(qr, kd, vd, sinkcol, HKV, carry=[shard("ffn_w_up", 0), shard("ffn_w_down", 0)])
    w_up = [from_cols(got[0]), None]
    w_down = [from_rows(got[1]), None]
    x1 = _matmul(o, w_o, mode="nn", out_dtype=F32, name="mm_attn_out", add=xs)
    hf0 = _rms_fwd(x1, ffn_norm[0:1], "rms_fwd_f0")
    gate0, up0, act0, got = _ffn_up(hf0, w_gate[0], w_up[0], "ffn_up_0", carry=[shard("ssm_w_in"), shard("ssm_w_out")])
    w_in = _pad_cols(from_cols(got[0]), ZW)
    w_out = from_rows(got[1])
    x2 = _matmul(act0, w_down[0], mode="nn", out_dtype=F32, name="mm_ffn_down_0", add=x1)
    hm1 = _rms_fwd(x2, mixer_norm[1:2], "rms_fwd_m1")
    zx, got = _matmul(hm1, w_in, mode="nn", out_dtype=F32, name="mm_ssm_in",
                      carry=[shard("ffn_w_gate", 1), shard("ffn_w_up", 1)])
    w_gate[1], w_up[1] = from_cols(got[0]), from_cols(got[1])
    xc = _conv_fwd(zx, conv_w, conv_b, DI, CD)
    yssd, yout, states, got = _ssd_fwd(zx, xc, bias_p, alog_p, sel, dskip, ng, DI, carry=[shard("ffn_w_down", 1)])
    w_down[1] = from_rows(got[0])
    x3 = _matmul(yout, w_out, mode="nn", out_dtype=F32, name="mm_ssm_out", add=x2)
    hf1 = _rms_fwd(x3, ffn_norm[1:2], "rms_fwd_f1")
    gate1, up1, act1 = _ffn_up(hf1, w_gate[1], w_up[1], "ffn_up_1")
    x4 = _matmul(act1, w_down[1], mode="nn", out_dtype=F32, name="mm_ffn_down_1", add=x3)
    sq, dx4, dx4b = _loss_head(x4, tgt)
    loss = lax.psum(sq[0, 0] * (0.5 / D), ("x", "y", "c"))

    def halves(blocks):
        half = blocks.shape[1] // 2
        return blocks[:, :half], blocks[:, half:]

    def ffn_bwd(dy, dyb, hf, gate, up, act, layer, xin, gain):
        dg, du = _ffn_dact(dyb, w_down[layer], gate, up, f"ffn_dact_{layer}")
        g_down = _matmul(act, dyb, mode="tn", out_dtype=BF16, name=f"mm_dw_down_{layer}")
        down_a, down_b = halves(rows_to_blocks(g_down))
        g_gate, got_da = _matmul(hf, dg, mode="tn", out_dtype=BF16, name=f"mm_dw_gate_{layer}", carry=[down_a])
        g_up, got_db = _matmul(hf, du, mode="tn", out_dtype=BF16, name=f"mm_dw_up_{layer}", carry=[down_b])
        gate_a, gate_b = halves(cols_to_blocks(g_gate))
        dh, got_ga = _matmul(dg, w_gate[layer], mode="nt", out_dtype=F32, name=f"mm_dh_gate_{layer}", carry=[gate_a])
        dh, got_gb = _matmul(du, w_up[layer], mode="nt", out_dtype=F32, name=f"mm_dh_up_{layer}", add=dh,
                             carry=[gate_b])
        dx, dxb, dgain = _rms_bwd(xin, gain, dh, dy, f"rms_bwd_f{layer}")
        return dx, dxb, dgain, cols_to_blocks(g_up), dict(down=[got_da[0], got_db[0]], gate=[got_ga[0], got_gb[0]])

    dx3, dx3b, d_fn1, up1_blocks, ffn1_got = ffn_bwd(dx4, dx4b, hf1, gate1, up1, act1, 1, x3, ffn_norm[1:2])
    dyo = _matmul(dx3b, w_out, mode="nt", out_dtype=F32, name="mm_dyout")
    g_wout = _matmul(yout, dx3b, mode="tn", out_dtype=BF16, name="mm_dw_ssm_out")
    dzx, dxx, dbm, dcm, ddt_p, dac_p, dd_l, dng_l, got1 = _ssd_bwd(
        zx, xc, yssd, dyo, states, bias_p, alog_p, sel, dskip, ng, DI,
        carry=[up1_blocks, rows_to_blocks(g_wout)])
    dzx, d_alog, d_bias = _ssd_dt_bwd(zx, ddt_p, dac_p, dzx, bias_p, alog_p, DI)
    dzx, d_convw, d_convb = _conv_bwd(zx, [dxx, dbm, dcm], dzx, conv_w, conv_b, DI, CD)
    g_win = _matmul(hm1, dzx, mode="tn", out_dtype=BF16, name="mm_dw_ssm_in")[:, :IW]
    dh, got2 = _matmul(dzx, w_in, mode="nt", out_dtype=F32, name="mm_dh_ssm_in", carry=[cols_to_blocks(g_win)])
    dx2, dx2b, d_mn1 = _rms_bwd(x2, mixer_norm[1:2], dh, dx3, "rms_bwd_m1")
    dx1, dx1b, d_fn0, up0_blocks, ffn0_got = ffn_bwd(dx2, dx2b, hf0, gate0, up0, act0, 0, x1, ffn_norm[0:1])
    do = _matmul(dx1b, w_o, mode="nt", out_dtype=F32, name="mm_do")
    g_wo = _matmul(o, dx1b, mode="tn", out_dtype=BF16, name="mm_dw_attn_out")
    dq, dkd, dvd, dsink, got3 = _attn_bwd(qr, kd, vd, o, do, sinkcol, HKV, carry=[up0_blocks, rows_to_blocks(g_wo)])
    dqkv, dgq_l, dgk_l = _attn_prep_bwd(qkv, dq, dkd, dvd, cos, sin, gq, gk, D, HKV)
    g_wqkv = _matmul(hm0, dqkv, mode="tn", out_dtype=BF16, name="mm_dw_qkv")
    dh, got_c = _matmul(dqkv, w_qkv, mode="nt", out_dtype=F32, name="mm_dh_qkv", carry=[cols_to_blocks(g_wqkv)])
    dx0, _, d_mn0 = _rms_bwd(xs, mixer_norm[0:1], dh, dx1, "rms_bwd_m0")

    d_ng = dng_l.reshape(1, DI)
    vec_send = jnp.concatenate([
        d_convw.reshape(SSM_CONV, N_DEV, vec_w).transpose(1, 0, 2), d_convb.reshape(1, N_DEV, vec_w).transpose(1, 0, 2),
        _pad_cols(d_ng.reshape(N_DEV, DI // N_DEV), vec_w)[:, None, :], jnp.zeros((N_DEV, 2, vec_w), F32)], axis=1)
    d_sinks = dsink[:, :, 0].reshape(1, HQ)
    d_gq = dgq_l[:, :HEAD] + dgq_l[:, HEAD:]
    d_gk = dgk_l[:, :HEAD] + dgk_l[:, HEAD:]
    d_dskip = dd_l.reshape(H, HEAD).sum(axis=1).reshape(1, H)
    rep_names = ["mixer_norm", "ffn_norm", "attn_q_norm", "attn_k_norm", "attn_sinks", "ssm_dt_bias", "ssm_a_log",
                 "ssm_d"]
    rep_grads = [jnp.concatenate([d_mn0, d_mn1], axis=0), jnp.concatenate([d_fn0, d_fn1], axis=0), d_gq, d_gk,
                 d_sinks, d_bias[:, :H], d_alog[:, :H], d_dskip]
    rep_sizes = [weights[n].size for n in rep_names]
    rep_len = -(-sum(rep_sizes) // (8 * LANES)) * 8 * LANES

    def pack(arrs):
        flat = jnp.concatenate([a.reshape(-1) for a in arrs])
        return jnp.pad(flat, (0, rep_len - flat.shape[0])).reshape(rep_len // LANES, LANES)

    rep_send = jnp.broadcast_to(pack(rep_grads)[None], (N_DEV, rep_len // LANES, LANES))
    got4 = _exchange([vec_send, rep_send], "exchange_last")
    parts_of = {
        "attn_w_qkv": [got_c[0]], "attn_w_o": [got3[1]], "ssm_w_in": [got2[0]], "ssm_w_out": [got1[1]],
        "ffn_w_gate": ffn0_got["gate"] + ffn1_got["gate"], "ffn_w_up": [got3[0], got1[0]],
        "ffn_w_down": ffn0_got["down"] + ffn1_got["down"],
    }

    out = {}
    for n in big:
        res = _adamw(parts_of[n], flat2(weights[n]), flat2(mom_m[n]), flat2(mom_v[n]), f"adamw_{n}")
        out[n] = [r.reshape(weights[n].shape) for r in res]

    def vec_block(d):
        return jnp.concatenate([d["ssm_conv_w"][0], d["ssm_conv_b"], _pad_cols(d["ssm_norm"], vec_w),
                                jnp.zeros((2, vec_w), F32)], axis=0)

    res = _adamw([got4[0]], vec_block(weights), vec_block(mom_m), vec_block(mom_v), "adamw_vectors")
    out["ssm_conv_w"] = [r[:SSM_CONV][None] for r in res]
    out["ssm_conv_b"] = [r[SSM_CONV:SSM_CONV + 1] for r in res]
    out["ssm_norm"] = [r[SSM_CONV + 1:SSM_CONV + 2, :DI // N_DEV] for r in res]
    res = _adamw([got4[1]], pack([weights[n] for n in rep_names]), pack([mom_m[n] for n in rep_names]),
                 pack([mom_v[n] for n in rep_names]), "adamw_replicated")
    offs = 0
    for n, sz in zip(rep_names, rep_sizes):
        out[n] = [r.reshape(-1)[offs:offs + sz].reshape(weights[n].shape) for r in res]
        offs += sz

    names = list(weights)
    return (loss, dx0[None], *[out[n][0] for n in names], *[out[n][1] for n in names],
            *[out[n][2] for n in names], *[out[n][3] for n in names])
```

```python
import functools
import math

import jax
import jax.numpy as jnp
from jax import lax
from jax.experimental import pallas as pl
from jax.experimental.pallas import tpu as pltpu

F32 = jnp.float32
BF16 = jnp.bfloat16

N_DEV = 8
EPS = 1e-6
LANES = 128
HEAD = 64
ATT_GROUP = 8
ATT_GW = ATT_GROUP * HEAD
WINDOW = 128
ATT_STEP_BLOCKS = 8
ROPE_THETA = 10000.0
SSM_GROUPS = 8
SSM_STATE = 128
SSM_CONV = 4
SSM_CHUNK = 256
HALO = 8
CONV_ROWS = 1024
ADAM_LR, ADAM_B1, ADAM_B2, ADAM_EPS, ADAM_WD, ADAM_STEP = 0.001, 0.9, 0.999, 1e-08, 0.01, 10
VMEM_LIMIT = 56 * 1024 * 1024
MATMUL_VMEM = 44 * 1024 * 1024
MESH = pl.DeviceIdType.MESH

_NN = (((1,), (0,)), ((), ()))
_NT = (((1,), (1,)), ((), ()))
_TN = (((0,), (0,)), ((), ()))


def _dot(a, b, dims=_NN):
    return lax.dot_general(a, b, dims, preferred_element_type=F32)


def _tile(n, cap):
    if n % LANES:
        return n
    best = LANES
    for t in range(LANES, min(n, cap) + 1, LANES):
        if n % t == 0:
            best = t
    return best


def _params(sem):
    return pltpu.CompilerParams(dimension_semantics=sem, vmem_limit_bytes=VMEM_LIMIT)


def _slot(px, py, pc):
    return 4 * px + 2 * py + pc


def _direct_copies(srcs, dsts, send_sems, recv_sems, local_sems, with_arrivals=True):
    x, y, c = lax.axis_index("x"), lax.axis_index("y"), lax.axis_index("c")
    me = _slot(x, y, c)
    peers = [(x ^ (m >> 2), y ^ ((m >> 1) & 1), c ^ (m & 1)) for m in range(1, N_DEV)]
    local, sends, arrivals = [], [], []
    for w, (src, dst) in enumerate(zip(srcs, dsts)):
        sliced = src.shape == dst.shape
        local.append(pltpu.make_async_copy(src.at[me] if sliced else src, dst.at[me], local_sems.at[w]))
        for k, peer in enumerate(peers):
            sems = dict(send_sem=send_sems.at[w, k], recv_sem=recv_sems.at[w, k], device_id=peer, device_id_type=MESH)
            sends.append(pltpu.make_async_remote_copy(
                src_ref=src.at[_slot(*peer)] if sliced else src, dst_ref=dst.at[me], **sems))
            if with_arrivals:
                arrivals.append(pltpu.make_async_remote_copy(
                    src_ref=src.at[me] if sliced else src, dst_ref=dst.at[_slot(*peer)], **sems))
    return local, sends, arrivals


def _gather_phases(srcs, dsts, send_sems, recv_sems, local_sems):
    x, y, c = lax.axis_index("x"), lax.axis_index("y"), lax.axis_index("c")
    me, sibling = (x, y, c), (x, y, 1 - c)
    chips = [(1 - x, y), (x, 1 - y), (1 - x, 1 - y)]
    n = len(srcs)

    def copy(w, k, block, to, src=None):
        dst = dsts[w].at[_slot(*block)]
        return pltpu.make_async_remote_copy(
            src_ref=dst if src is None else src, dst_ref=dst, send_sem=send_sems.at[w, k],
            recv_sem=recv_sems.at[w, k], device_id=to, device_id_type=MESH)

    def first_sends(w):
        return [copy(w, 0, me, sibling, src=srcs[w])] + [copy(w, 1 + j, me, (*chip, c), src=srcs[w])
                                                         for j, chip in enumerate(chips)]

    def start():
        for w in range(n):
            pltpu.make_async_copy(srcs[w], dsts[w].at[_slot(*me)], local_sems.at[w]).start()
            for cp in first_sends(w):
                cp.start()

    def forward():
        for w in range(n):
            for j, chip in enumerate(chips):
                copy(w, 1 + j, (*chip, c), me).wait_recv()
                copy(w, 4 + j, (*chip, c), sibling).start()

    def finish():
        for w in range(n):
            copy(w, 0, sibling, me).wait_recv()
            for j, chip in enumerate(chips):
                copy(w, 4 + j, (*chip, 1 - c), me).wait_recv()
        for w in range(n):
            for cp in first_sends(w) + [copy(w, 4 + j, (*chip, c), sibling) for j, chip in enumerate(chips)]:
                cp.wait_send()
            pltpu.make_async_copy(srcs[w], dsts[w].at[_slot(*me)], local_sems.at[w]).wait()

    return start, forward, finish


def _call(body, *, name, grid, in_specs, out_specs, out_shape, sem, args, scratch_shapes=(), carry=None):
    if carry is None:
        return pl.pallas_call(body, name=name, grid=grid, in_specs=in_specs, out_specs=out_specs, out_shape=out_shape,
                              scratch_shapes=list(scratch_shapes), compiler_params=_params(sem))(*args)
    n_in, n_out, n_sc, n_c = len(in_specs), len(out_specs), len(scratch_shapes), len(carry)
    gather = all(a.ndim == 2 for a in carry)
    assert gather or all(a.ndim == 3 and a.shape[0] == N_DEV for a in carry)
    recv_shape = [jax.ShapeDtypeStruct((N_DEV,) + a.shape if gather else a.shape, a.dtype) for a in carry]
    n_steps = math.prod(grid)

    def wrapped(*refs):
        ins, c_in = refs[:n_in], refs[n_in:n_in + n_c]
        outs, c_out = refs[n_in + n_c:n_in + n_c + n_out], refs[n_in + n_c + n_out:n_in + 2 * n_c + n_out]
        scr = refs[n_in + 2 * n_c + n_out:n_in + 2 * n_c + n_out + n_sc]
        sems = refs[-3:]
        step = functools.reduce(lambda acc, d: acc * grid[d] + pl.program_id(d), range(len(grid)), 0)
        if gather:
            start, forward, finish = _gather_phases(c_in, c_out, *sems)
        else:
            def start():
                local, sends, _ = _direct_copies(c_in, c_out, *sems, with_arrivals=False)
                for cp in local + sends:
                    cp.start()

            def finish():
                local, sends, arrivals = _direct_copies(c_in, c_out, *sems)
                for cp in arrivals:
                    cp.wait_recv()
                for cp in sends:
                    cp.wait_send()
                for cp in local:
                    cp.wait()

        pl.when(step == 0)(start)
        if gather:
            pl.when(step == min((3 * n_steps) // 4, n_steps - 1))(forward)
        body(*ins, *outs, *scr)
        pl.when(step == n_steps - 1)(finish)

    any_spec = pl.BlockSpec(memory_space=pl.ANY)
    res = pl.pallas_call(
        wrapped, name=name, grid=grid, in_specs=list(in_specs) + [any_spec] * n_c,
        out_specs=list(out_specs) + [any_spec] * n_c, out_shape=list(out_shape) + recv_shape,
        scratch_shapes=list(scratch_shapes) + [pltpu.SemaphoreType.DMA((n_c, N_DEV - 1)),
                                               pltpu.SemaphoreType.DMA((n_c, N_DEV - 1)), pltpu.SemaphoreType.DMA((n_c,))],
        compiler_params=_params(("arbitrary",) * len(grid)),
    )(*args, *carry)
    return res


def _matmul(a, b, *, mode, out_dtype, name, add=None, carry=None):
    if mode == "nn":
        (M, K), N = a.shape, b.shape[1]
    elif mode == "nt":
        (M, K), N = a.shape, b.shape[0]
    else:
        (K, M), N = a.shape, b.shape[1]
    assert a.dtype == BF16 and b.dtype == BF16
    has_add = add is not None
    tn = _tile(N, 512)
    for tm in ((_tile(M, 512),) if mode == "tn" else (_tile(M, 2048), _tile(M, 1024))):
        fixed = 2 * tm * tn * (jnp.dtype(out_dtype).itemsize + (4 if has_add else 0)) + tm * tn * 4
        per_k = 2 * 2 * (tm + tn) + (2 * tm if mode == "tn" else 0)
        tk = _tile(K, max(LANES, (MATMUL_VMEM - fixed) // per_k))
        if tk == K:
            break
    nk = K // tk
    dims = _NT if mode == "nt" else _NN
    if mode == "tn":
        a_spec = pl.BlockSpec((tk, tm), lambda i, j, k: (jnp.where(j == 0, k, 0), i))
    else:
        a_spec = pl.BlockSpec((tm, tk), lambda i, j, k: (i, k))
    b_spec = pl.BlockSpec((tn, tk), lambda i, j, k: (j, k)) if mode == "nt" else pl.BlockSpec((tk, tn), lambda i, j, k: (k, j))
    o_spec = pl.BlockSpec((tm, tn), lambda i, j, k: (i, j))

    def body(*refs):
        a_ref, b_ref = refs[:2]
        add_ref = refs[2] if has_add else None
        o_ref = refs[2 + has_add]
        scratch = list(refs[3 + has_add:])
        at = scratch.pop(0) if mode == "tn" else None
        acc = scratch.pop(0) if nk > 1 else None
        j, k = pl.program_id(1), pl.program_id(2)
        if mode == "tn":
            @pl.when(j == 0)
            def _():
                at[k] = a_ref[...].T

            part = _dot(at[k], b_ref[...], dims)
        else:
            part = _dot(a_ref[...], b_ref[...], dims)

        def finish(r):
            if has_add:
                r = r + add_ref[...]
            o_ref[...] = r.astype(out_dtype)

        if nk == 1:
            finish(part)
        else:
            @pl.when(k == 0)
            def _():
                acc[...] = part

            @pl.when(jnp.logical_and(k > 0, k < nk - 1))
            def _():
                acc[...] += part

            @pl.when(k == nk - 1)
            def _():
                finish(acc[...] + part)

    scratch = ([pltpu.VMEM((nk, tm, tk), BF16)] if mode == "tn" else []) + ([pltpu.VMEM((tm, tn), F32)] if nk > 1 else [])
    res = _call(
        body, name=name, grid=(M // tm, N // tn, nk),
        in_specs=[a_spec, b_spec] + ([o_spec] if has_add else []),
        out_specs=[o_spec], out_shape=[jax.ShapeDtypeStruct((M, N), out_dtype)],
        scratch_shapes=scratch, sem=("parallel", "arbitrary", "arbitrary"),
        args=(a, b, add) if has_add else (a, b), carry=carry)
    return res[0] if carry is None else (res[0], res[1:])


def _rms_fwd(x, gain, name, carry=None):
    T, D = x.shape
    tr = 256

    def body(x_ref, g_ref, h_ref):
        xv = x_ref[...]
        rstd = lax.rsqrt(jnp.mean(xv * xv, axis=1, keepdims=True) + EPS)
        h_ref[...] = (xv * rstd * g_ref[...]).astype(BF16)

    res = _call(
        body, name=name, grid=(T // tr,),
        in_specs=[pl.BlockSpec((tr, D), lambda i: (i, 0)), pl.BlockSpec((1, D), lambda i: (0, 0))],
        out_specs=[pl.BlockSpec((tr, D), lambda i: (i, 0))],
        out_shape=[jax.ShapeDtypeStruct((T, D), BF16)], sem=("parallel",), args=(x, gain), carry=carry)
    return res[0] if carry is None else (res[0], res[1:])


def _rms_bwd(x, gain, dh, dres, name):
    T, D = x.shape
    tr = 256

    def body(x_ref, g_ref, dh_ref, dr_ref, dx_ref, dxb_ref, dg_ref):
        @pl.when(pl.program_id(0) == 0)
        def _():
            dg_ref[...] = jnp.zeros_like(dg_ref)

        xv = x_ref[...]
        rstd = lax.rsqrt(jnp.mean(xv * xv, axis=1, keepdims=True) + EPS)
        xhat = xv * rstd
        dy = dh_ref[...].astype(F32)
        dg_ref[...] += jnp.sum(dy * xhat, axis=0, keepdims=True)
        dxh = dy * g_ref[...]
        dx = dr_ref[...] + rstd * (dxh - xhat * jnp.mean(dxh * xhat, axis=1, keepdims=True))
        dx_ref[...] = dx
        dxb_ref[...] = dx.astype(BF16)

    row = pl.BlockSpec((tr, D), lambda i: (i, 0))
    vec = pl.BlockSpec((1, D), lambda i: (0, 0))
    return pl.pallas_call(
        body, name=name, grid=(T // tr,), in_specs=[row, vec, row, row], out_specs=[row, row, vec],
        out_shape=[jax.ShapeDtypeStruct((T, D), F32), jax.ShapeDtypeStruct((T, D), BF16),
                   jax.ShapeDtypeStruct((1, D), F32)],
        compiler_params=_params(("arbitrary",)),
    )(x, gain, dh, dres)


def _loss_head(y, target):
    T, D = y.shape
    tr = 256

    def body(y_ref, t_ref, s_ref, d_ref, db_ref):
        @pl.when(pl.program_id(0) == 0)
        def _():
            s_ref[...] = jnp.zeros_like(s_ref)

        e = y_ref[...] - t_ref[...]
        s_ref[...] += jnp.sum(jnp.sum(e * e, axis=1, keepdims=True), axis=0, keepdims=True)
        d = e * (1.0 / D)
        d_ref[...] = d
        db_ref[...] = d.astype(BF16)

    row = pl.BlockSpec((tr, D), lambda i: (i, 0))
    return pl.pallas_call(
        body, name="loss_head", grid=(T // tr,), in_specs=[row, row],
        out_specs=[pl.BlockSpec((1, 1), lambda i: (0, 0)), row, row],
        out_shape=[jax.ShapeDtypeStruct((1, 1), F32), jax.ShapeDtypeStruct((T, D), F32),
                   jax.ShapeDtypeStruct((T, D), BF16)],
        compiler_params=_params(("arbitrary",)),
    )(y, target)


def _ffn_up(h, wg, wu, name, carry=None):
    (T, D), Fd = h.shape, wg.shape[1]
    tm, tn = _tile(T, 2048), _tile(Fd, 512)

    def body(h_ref, wg_ref, wu_ref, fg_ref, fu_ref, a_ref):
        hv = h_ref[...]
        g = _dot(hv, wg_ref[...])
        s = jax.nn.sigmoid(g)
        silu = g * s
        fu_ref[...] = silu.astype(BF16)
        u = _dot(hv, wu_ref[...])
        fg_ref[...] = (u * (s + silu * (1.0 - s))).astype(BF16)
        a_ref[...] = (silu * u).astype(BF16)

    w_spec = pl.BlockSpec((D, tn), lambda i, j: (0, j))
    o_spec = pl.BlockSpec((tm, tn), lambda i, j: (i, j))
    res = _call(
        body, name=name, grid=(T // tm, Fd // tn),
        in_specs=[pl.BlockSpec((tm, D), lambda i, j: (i, 0)), w_spec, w_spec],
        out_specs=[o_spec, o_spec, o_spec],
        out_shape=[jax.ShapeDtypeStruct((T, Fd), BF16)] * 3,
        sem=("parallel", "arbitrary"), args=(h, wg, wu), carry=carry)
    return res if carry is None else (*res[:3], res[3:])


def _ffn_dact(dy, wd, fgate, fup, name):
    (T, D), Fd = dy.shape, wd.shape[0]
    tm, tn = _tile(T, 2048), _tile(Fd, 512)

    def body(dy_ref, wd_ref, fg_ref, fu_ref, dg_ref, du_ref):
        da = _dot(dy_ref[...], wd_ref[...], _NT)
        du_ref[...] = (da * fu_ref[...].astype(F32)).astype(BF16)
        dg_ref[...] = (da * fg_ref[...].astype(F32)).astype(BF16)

    o_spec = pl.BlockSpec((tm, tn), lambda i, j: (i, j))
    return pl.pallas_call(
        body, name=name, grid=(T // tm, Fd // tn),
        in_specs=[pl.BlockSpec((tm, D), lambda i, j: (i, 0)), pl.BlockSpec((tn, D), lambda i, j: (j, 0)),
                  o_spec, o_spec],
        out_specs=[o_spec, o_spec],
        out_shape=[jax.ShapeDtypeStruct((T, Fd), BF16), jax.ShapeDtypeStruct((T, Fd), BF16)],
        compiler_params=_params(("parallel", "arbitrary")),
    )(dy, wd, fgate, fup)


def _is_a(shape):
    return lax.broadcasted_iota(jnp.int32, shape, 1) % LANES < HEAD


def _split2(v):
    hi = v.astype(BF16)
    return hi, (v - hi.astype(F32)).astype(BF16)


def _split3(v):
    hi = v.astype(BF16)
    r = v - hi.astype(F32)
    mid = r.astype(BF16)
    return hi, mid, (r - mid.astype(F32)).astype(BF16)


def _dot_split(v, m, pieces, dims=_NN):
    parts = _split3(v) if pieces == 3 else _split2(v)
    out = _dot(parts[0], m, dims)
    for p in parts[1:]:
        out = out + _dot(p, m, dims)
    return out


def _head_blockdiag():
    r = lax.broadcasted_iota(jnp.int32, (LANES, LANES), 0) // HEAD
    c = lax.broadcasted_iota(jnp.int32, (LANES, LANES), 1) // HEAD
    return (r == c).astype(BF16)


def _swap_half(v):
    lane = lax.broadcasted_iota(jnp.int32, v.shape, 1)
    return jnp.where(lane % HEAD < HEAD // 2, pltpu.roll(v, LANES - HEAD // 2, axis=1), pltpu.roll(v, HEAD // 2, axis=1))


def _attn_prep_fwd(qkv, cos, sin, gq, gk, D, HKV):
    T, QW = qkv.shape
    tr = 256
    nq, nk = D // LANES, HKV // 2
    KW = HKV * LANES

    def body(x_ref, cos_ref, sin_ref, gq_ref, gk_ref, q_ref, k_ref, v_ref):
        bd = _head_blockdiag()
        cs, sn = cos_ref[...], sin_ref[...]
        isa = _is_a((tr, LANES))

        def normrope(xv, g):
            ms = _dot_split(xv * xv, bd, 2) * (1.0 / HEAD)
            xn = xv * lax.rsqrt(ms + EPS) * g
            return xn * cs + _swap_half(xn) * sn

        def dup(v):
            r = pltpu.roll(v, HEAD, axis=1)
            return jnp.where(isa, v, r), jnp.where(isa, r, v)

        for s in range(nq):
            sl = slice(s * LANES, (s + 1) * LANES)
            q_ref[:, sl] = normrope(x_ref[:, sl], gq_ref[...]).astype(BF16)
        for s in range(nk):
            ka, kb = dup(normrope(x_ref[:, D + s * LANES:D + (s + 1) * LANES], gk_ref[...]))
            k_ref[:, 2 * s * LANES:(2 * s + 1) * LANES] = ka.astype(BF16)
            k_ref[:, (2 * s + 1) * LANES:(2 * s + 2) * LANES] = kb.astype(BF16)
            va, vb = dup(x_ref[:, D + (nk + s) * LANES:D + (nk + s + 1) * LANES])
            v_ref[:, 2 * s * LANES:(2 * s + 1) * LANES] = va.astype(BF16)
            v_ref[:, (2 * s + 1) * LANES:(2 * s + 2) * LANES] = vb.astype(BF16)

    tab = pl.BlockSpec((tr, LANES), lambda i: (i, 0))
    vec = pl.BlockSpec((1, LANES), lambda i: (0, 0))
    return pl.pallas_call(
        body, name="attn_prep_fwd", grid=(T // tr,),
        in_specs=[pl.BlockSpec((tr, QW), lambda i: (i, 0)), tab, tab, vec, vec],
        out_specs=[pl.BlockSpec((tr, D), lambda i: (i, 0)), pl.BlockSpec((tr, KW), lambda i: (i, 0)),
                   pl.BlockSpec((tr, KW), lambda i: (i, 0))],
        out_shape=[jax.ShapeDtypeStruct((T, D), BF16), jax.ShapeDtypeStruct((T, KW), BF16),
                   jax.ShapeDtypeStruct((T, KW), BF16)],
        compiler_params=_params(("parallel",)),
    )(qkv, cos, sin, gq, gk)


def _attn_prep_bwd(qkv, dq, dkd, dvd, cos, sin, gq, gk, D, HKV):
    T, QW = qkv.shape
    tr = 256
    nq, nk = D // LANES, HKV // 2
    KW = HKV * LANES

    def body(x_ref, dq_ref, dk_ref, dv_ref, cos_ref, sin_ref, gq_ref, gk_ref, o_ref, dgq_ref, dgk_ref):
        @pl.when(pl.program_id(0) == 0)
        def _():
            dgq_ref[...] = jnp.zeros_like(dgq_ref)
            dgk_ref[...] = jnp.zeros_like(dgk_ref)

        bd = _head_blockdiag()
        cs, sn = cos_ref[...], sin_ref[...]
        isa = _is_a((tr, LANES))

        def back(xv, dy, g):
            rstd = lax.rsqrt(_dot_split(xv * xv, bd, 2) * (1.0 / HEAD) + EPS)
            xhat = xv * rstd
            dxn = dy * cs + _swap_half(dy * sn)
            dxh = dxn * g
            mean = _dot_split(dxh * xhat, bd, 2) * (1.0 / HEAD)
            return rstd * (dxh - xhat * mean), jnp.sum(dxn * xhat, axis=0, keepdims=True)

        def fold(s):
            a = dk_ref[:, 2 * s * LANES:(2 * s + 1) * LANES]
            b = dk_ref[:, (2 * s + 1) * LANES:(2 * s + 2) * LANES]
            return jnp.where(isa, a + pltpu.roll(a, HEAD, axis=1), b + pltpu.roll(b, HEAD, axis=1))

        def foldv(s):
            a = dv_ref[:, 2 * s * LANES:(2 * s + 1) * LANES]
            b = dv_ref[:, (2 * s + 1) * LANES:(2 * s + 2) * LANES]
            return jnp.where(isa, a + pltpu.roll(a, HEAD, axis=1), b + pltpu.roll(b, HEAD, axis=1))

        dgq = jnp.zeros((1, LANES), F32)
        for s in range(nq):
            sl = slice(s * LANES, (s + 1) * LANES)
            dx, dg = back(x_ref[:, sl], dq_ref[:, sl], gq_ref[...])
            o_ref[:, sl] = dx.astype(BF16)
            dgq = dgq + dg
        dgq_ref[...] += dgq
        dgk = jnp.zeros((1, LANES), F32)
        for s in range(nk):
            sl = slice(D + s * LANES, D + (s + 1) * LANES)
            dx, dg = back(x_ref[:, sl], fold(s), gk_ref[...])
            o_ref[:, sl] = dx.astype(BF16)
            dgk = dgk + dg
            o_ref[:, D + (nk + s) * LANES:D + (nk + s + 1) * LANES] = foldv(s).astype(BF16)
        dgk_ref[...] += dgk

    tab = pl.BlockSpec((tr, LANES), lambda i: (i, 0))
    vec = pl.BlockSpec((1, LANES), lambda i: (0, 0))
    kv = pl.BlockSpec((tr, KW), lambda i: (i, 0))
    return pl.pallas_call(
        body, name="attn_prep_bwd", grid=(T // tr,),
        in_specs=[pl.BlockSpec((tr, QW), lambda i: (i, 0)), pl.BlockSpec((tr, D), lambda i: (i, 0)), kv, kv,
                  tab, tab, vec, vec],
        out_specs=[pl.BlockSpec((tr, QW), lambda i: (i, 0)), vec, vec],
        out_shape=[jax.ShapeDtypeStruct((T, QW), BF16), jax.ShapeDtypeStruct((1, LANES), F32),
                   jax.ShapeDtypeStruct((1, LANES), F32)],
        compiler_params=_params(("arbitrary",)),
    )(qkv, dq, dkd, dvd, cos, sin, gq, gk)


def _attn_probs(qs, kw, sink_ref, first, scale):
    rows = qs.shape[0]
    qi = lax.broadcasted_iota(jnp.int32, (rows, 2 * WINDOW), 0) % WINDOW
    kj = lax.broadcasted_iota(jnp.int32, (rows, 2 * WINDOW), 1)
    valid = (kj > qi) & (kj <= qi + WINDOW)
    if first is not False:
        valid = valid & jnp.logical_or(jnp.logical_not(first), kj >= WINDOW)
    isa = _is_a(kw.shape)
    out = []
    for pos in (0, 1):
        kp = jnp.where(isa if pos == 0 else ~isa, kw, jnp.zeros_like(kw))
        s = jnp.where(valid, _dot(qs, kp, _NT) * scale, -jnp.inf)
        sink = sink_ref[0, pos]
        m = jnp.maximum(jnp.max(s, axis=1, keepdims=True), sink)
        p = jnp.exp(s - m)
        ps = jnp.exp(sink - m)
        inv = 1.0 / (jnp.sum(p, axis=1, keepdims=True) + ps)
        out.append((p * inv, ps * inv, kp))
    return out


def _attn_specs(qb):
    q = pl.BlockSpec((qb * WINDOW, ATT_GW), lambda g, n: (n, g))
    cur = pl.BlockSpec((qb * WINDOW, LANES), lambda g, n: (n, g))
    prev = pl.BlockSpec((WINDOW, LANES), lambda g, n: (jnp.maximum(qb * n - 1, 0), g))
    sink = pl.BlockSpec((1, 2, ATT_GW, 1), lambda g, n: (g, 0, 0, 0))
    return q, cur, prev, sink


def _stack(ref, s):
    rows = slice(s * WINDOW, (s + 1) * WINDOW)
    return jnp.concatenate([ref[rows, i * LANES:(i + 1) * LANES] for i in range(ATT_GW // LANES)], axis=0)


def _attn_fwd(q, kd, vd, sinkcol, HKV, carry=None):
    T, D = q.shape
    nb = T // WINDOW
    qb = math.gcd(nb, ATT_STEP_BLOCKS)
    scale = HEAD ** -0.5

    def body(q_ref, kp_ref, kc_ref, vp_ref, vc_ref, sink_ref, o_ref):
        n = pl.program_id(1)
        kall = jnp.concatenate([kp_ref[...], kc_ref[...]], axis=0)
        vall = jnp.concatenate([vp_ref[...], vc_ref[...]], axis=0)
        isa = _is_a((2 * WINDOW, LANES))
        for s in range(qb):
            win = slice(s * WINDOW, (s + 2) * WINDOW)
            kw, vw = kall[win], vall[win]
            o = jnp.zeros((ATT_GW, LANES), F32)
            first = (n == 0) if s == 0 else False
            for pos, (probs, _, _) in enumerate(_attn_probs(_stack(q_ref, s), kw, sink_ref, first, scale)):
                vp = jnp.where(isa if pos == 0 else ~isa, vw, jnp.zeros_like(vw))
                o = o + _dot(probs.astype(BF16), vp)
            for i in range(ATT_GW // LANES):
                o_ref[s * WINDOW:(s + 1) * WINDOW, i * LANES:(i + 1) * LANES] = o[i * WINDOW:(i + 1) * WINDOW].astype(BF16)

    qs_, cur, prev, sink = _attn_specs(qb)
    res = _call(
        body, name="attn_fwd", grid=(HKV, nb // qb), in_specs=[qs_, prev, cur, prev, cur, sink], out_specs=[qs_],
        out_shape=[jax.ShapeDtypeStruct((T, D), BF16)], sem=("parallel", "parallel"),
        args=(q, kd, kd, vd, vd, sinkcol), carry=carry)
    return res[0] if carry is None else (res[0], res[1:])


def _attn_bwd(q, kd, vd, o, do, sinkcol, HKV, carry=None):
    T, D = q.shape
    nb = T // WINDOW
    qb = math.gcd(nb, ATT_STEP_BLOCKS)
    scale = HEAD ** -0.5
    KW = HKV * LANES

    def body(q_ref, kp_ref, kc_ref, vp_ref, vc_ref, o_ref, do_ref, sink_ref, dq_ref, dk_ref, dv_ref, ds_ref):
        n = pl.program_id(1)

        @pl.when(n == 0)
        def _():
            dk_ref[...] = jnp.zeros_like(dk_ref)
            dv_ref[...] = jnp.zeros_like(dv_ref)
            ds_ref[...] = jnp.zeros_like(ds_ref)

        kall = jnp.concatenate([kp_ref[...], kc_ref[...]], axis=0)
        vall = jnp.concatenate([vp_ref[...], vc_ref[...]], axis=0)
        isa_w = _is_a((2 * WINDOW, LANES))
        isa_q = _is_a((ATT_GW, LANES))
        for s in range(qb):
            win = slice(s * WINDOW, (s + 2) * WINDOW)
            kw, vw = kall[win], vall[win]
            qs = _stack(q_ref, s)
            dos = _stack(do_ref, s)
            dd = dos * _stack(o_ref, s).astype(F32)
            dob = dos.astype(BF16)
            dqs = jnp.zeros((ATT_GW, LANES), F32)
            dkw, dvw, dsk = [], [], []
            first = (n == 0) if s == 0 else False
            for pos, (probs, psink, kp) in enumerate(_attn_probs(qs, kw, sink_ref, first, scale)):
                sel_w = isa_w if pos == 0 else ~isa_w
                sel_q = isa_q if pos == 0 else ~isa_q
                delta = jnp.sum(jnp.where(sel_q, dd, 0.0), axis=1, keepdims=True)
                vp = jnp.where(sel_w, vw, jnp.zeros_like(vw))
                dp = _dot(dob, vp, _NT)
                dsb = (probs * (dp - delta) * scale).astype(BF16)
                dqs = dqs + _dot(dsb, kp)
                dkw.append(_dot(dsb, qs, _TN))
                dvw.append(_dot(probs.astype(BF16), dob, _TN))
                dsk.append(-psink * delta)
            dkw = jnp.where(isa_w, dkw[0], dkw[1])
            dvw = jnp.where(isa_w, dvw[0], dvw[1])

            def add_window(dkw=dkw, dvw=dvw, s=s):
                start = pl.multiple_of((qb * n + s - 1) * WINDOW, WINDOW)
                dk_ref[pl.ds(start, 2 * WINDOW), :] += dkw
                dv_ref[pl.ds(start, 2 * WINDOW), :] += dvw

            if s == 0:
                @pl.when(n == 0)
                def _(dkw=dkw, dvw=dvw):
                    dk_ref[0:WINDOW, :] += dkw[WINDOW:]
                    dv_ref[0:WINDOW, :] += dvw[WINDOW:]

                pl.when(n > 0)(add_window)
            else:
                add_window()

            rows = []
            for i in range(ATT_GW // LANES):
                dq_ref[s * WINDOW:(s + 1) * WINDOW, i * LANES:(i + 1) * LANES] = dqs[i * WINDOW:(i + 1) * WINDOW]
                for pos in (0, 1):
                    t = jnp.sum(dsk[pos][i * WINDOW:(i + 1) * WINDOW], axis=0, keepdims=True)
                    rows.append(jnp.broadcast_to(t, (1, LANES)))
            ds_ref[0] += jnp.concatenate(rows, axis=0)

    qs_, cur, prev, sink = _attn_specs(qb)
    dqo = pl.BlockSpec((qb * WINDOW, ATT_GW), lambda g, n: (n, g))
    dkv = pl.BlockSpec((T, LANES), lambda g, n: (0, g))
    res = _call(
        body, name="attn_bwd", grid=(HKV, nb // qb),
        in_specs=[qs_, prev, cur, prev, cur, qs_, dqo, sink],
        out_specs=[dqo, dkv, dkv, pl.BlockSpec((1, ATT_GROUP, LANES), lambda g, n: (g, 0, 0))],
        out_shape=[jax.ShapeDtypeStruct((T, D), F32), jax.ShapeDtypeStruct((T, KW), F32),
                   jax.ShapeDtypeStruct((T, KW), F32), jax.ShapeDtypeStruct((HKV, ATT_GROUP, LANES), F32)],
        sem=("parallel", "arbitrary"), args=(q, kd, kd, vd, vd, o, do, sinkcol), carry=carry)
    return res if carry is None else (*res[:4], res[4:])


def _conv_fwd(zx, w, b, DI, CD):
    T = zx.shape[0]
    cw, tc = _tile(math.gcd(DI, CD), 512), math.gcd(T, CONV_ROWS)
    off = DI // cw

    def body(cur_ref, halo_ref, w_ref, b_ref, o_ref):
        i = pl.program_id(1)
        halo = jnp.where(i > 0, halo_ref[...], 0.0)
        ext = jnp.concatenate([halo, cur_ref[...]], axis=0)
        acc = b_ref[...] + w_ref[SSM_CONV - 1:SSM_CONV, :] * ext[HALO:]
        for k in range(SSM_CONV - 1):
            acc = acc + w_ref[k:k + 1, :] * pltpu.roll(ext, SSM_CONV - 1 - k, axis=0)[HALO:]
        o_ref[...] = acc * jax.nn.sigmoid(acc)

    return pl.pallas_call(
        body, name="ssm_conv_fwd", grid=(CD // cw, T // tc),
        in_specs=[pl.BlockSpec((tc, cw), lambda j, i: (i, off + j)),
                  pl.BlockSpec((HALO, cw), lambda j, i: (jnp.maximum(i * (tc // HALO) - 1, 0), off + j)),
                  pl.BlockSpec((SSM_CONV, cw), lambda j, i: (0, j)), pl.BlockSpec((1, cw), lambda j, i: (0, j))],
        out_specs=pl.BlockSpec((tc, cw), lambda j, i: (i, j)),
        out_shape=jax.ShapeDtypeStruct((T, CD), F32),
        compiler_params=_params(("parallel", "parallel")),
    )(zx, zx, w, b)


def _conv_bwd(zx, dparts, dzx, w, b, DI, CD):
    T = zx.shape[0]
    cw, tc = _tile(math.gcd(DI, CD), 512), math.gcd(T, CONV_ROWS)
    off = DI // cw
    nt = T // tc
    hb = tc // HALO
    ends = [0]
    for p in dparts:
        assert p.shape[1] % cw == 0
        ends.append(ends[-1] + p.shape[1] // cw)
    assert ends[-1] == CD // cw
    n_p = len(dparts)

    def body(*refs):
        cur_ref, prev_ref, next_ref = refs[:3]
        d_refs, dn_refs = refs[3:3 + n_p], refs[3 + n_p:3 + 2 * n_p]
        w_ref, b_ref, _, o_ref, dw_ref, db_ref = refs[3 + 2 * n_p:]
        j, i = pl.program_id(0), pl.program_id(1)

        @pl.when(i == 0)
        def _():
            dw_ref[...] = jnp.zeros_like(dw_ref)
            db_ref[...] = jnp.zeros_like(db_ref)

        def pick(prefs):
            v = prefs[n_p - 1][...]
            for p in range(n_p - 2, -1, -1):
                v = jnp.where(j < ends[p + 1], prefs[p][...], v)
            return v

        prev = jnp.where(i > 0, prev_ref[...], 0.0)
        ext = jnp.concatenate([prev, cur_ref[...], next_ref[...]], axis=0)
        u = b_ref[...] + w_ref[SSM_CONV - 1:SSM_CONV, :] * ext
        for k in range(SSM_CONV - 1):
            u = u + w_ref[k:k + 1, :] * pltpu.roll(ext, SSM_CONV - 1 - k, axis=0)
        u = u[HALO:]
        dnext = jnp.where(i < nt - 1, pick(dn_refs), 0.0)
        dxe = jnp.concatenate([pick(d_refs), dnext], axis=0)
        sg = jax.nn.sigmoid(u)
        du = dxe * sg * (1.0 + u * (1.0 - sg))
        n_e = tc + HALO
        dx = w_ref[SSM_CONV - 1:SSM_CONV, :] * du
        for k in range(SSM_CONV - 1):
            dx = dx + w_ref[k:k + 1, :] * pltpu.roll(du, n_e - (SSM_CONV - 1 - k), axis=0)
        o_ref[...] = dx[:tc].astype(BF16)
        duc = du[:tc]
        db_ref[...] += jnp.sum(duc, axis=0, keepdims=True)
        xs = ext[:n_e]
        dws = []
        for k in range(SSM_CONV):
            sh = xs if k == SSM_CONV - 1 else pltpu.roll(xs, SSM_CONV - 1 - k, axis=0)
            dws.append(jnp.sum(duc * sh[HALO:], axis=0, keepdims=True))
        dw_ref[...] += jnp.concatenate(dws, axis=0)

    def part_specs(p):
        lo, n = ends[p], ends[p + 1] - ends[p]

        def inside(j):
            return jnp.logical_and(j >= lo, j < lo + n)

        col = lambda j: jnp.clip(j - lo, 0, n - 1)
        return (pl.BlockSpec((tc, cw), lambda j, i: (jnp.where(inside(j), i, 0), col(j))),
                pl.BlockSpec((HALO, cw), lambda j, i: (jnp.where(inside(j), jnp.minimum((i + 1) * hb, nt * hb - 1), 0), col(j))))

    specs = [part_specs(p) for p in range(n_p)]
    return pl.pallas_call(
        body, name="ssm_conv_bwd", grid=(CD // cw, nt),
        in_specs=[pl.BlockSpec((tc, cw), lambda j, i: (i, off + j)),
                  pl.BlockSpec((HALO, cw), lambda j, i: (jnp.maximum(i * hb - 1, 0), off + j)),
                  pl.BlockSpec((HALO, cw), lambda j, i: (jnp.minimum((i + 1) * hb, nt * hb - 1), off + j))]
        + [s[0] for s in specs] + [s[1] for s in specs]
        + [pl.BlockSpec((SSM_CONV, cw), lambda j, i: (0, j)), pl.BlockSpec((1, cw), lambda j, i: (0, j)),
           pl.BlockSpec(memory_space=pl.ANY)],
        out_specs=[pl.BlockSpec((tc, cw), lambda j, i: (i, off + j)), pl.BlockSpec((SSM_CONV, cw), lambda j, i: (0, j)),
                   pl.BlockSpec((1, cw), lambda j, i: (0, j))],
        out_shape=[jax.ShapeDtypeStruct(dzx.shape, BF16), jax.ShapeDtypeStruct((SSM_CONV, CD), F32),
                   jax.ShapeDtypeStruct((1, CD), F32)],
        input_output_aliases={5 + 2 * n_p: 0},
        compiler_params=_params(("parallel", "arbitrary")),
    )(zx, zx, zx, *dparts, *dparts, w, b, dzx)


def _tri_dot(v, upper):
    L = v.shape[0]
    r = lax.broadcasted_iota(jnp.int32, (L, L), 0)
    c = lax.broadcasted_iota(jnp.int32, (L, L), 1)
    tri = ((r <= c) if upper else (r >= c)).astype(BF16)
    p = _split3(v)
    return _dot(tri, p[0]) + _dot(tri, p[1]) + _dot(tri, p[2])


def _ssd_time2(dtraw_ref, bias_ref, alog_ref, sel):
    dt = jax.nn.softplus(dtraw_ref[...] + bias_ref[...])
    acum = _tri_dot(dt * (-jnp.exp(alog_ref[...])), False)
    return dt, _dot_split(dt, sel, 3), _dot_split(acum, sel, 3)


def _decay(acs, acs_t, pos):
    L = acs.shape[0]
    r = lax.broadcasted_iota(jnp.int32, (L, L), 0)
    c = lax.broadcasted_iota(jnp.int32, (L, L), 1)
    col = acs[:, HEAD * pos:HEAD * pos + 1]
    row = acs_t[HEAD * pos:HEAD * pos + 1, :]
    return jnp.exp(jnp.where(r >= c, col - row, -jnp.inf))


def _ssd_specs(G, GW, DI, ZW):
    L = SSM_CHUNK
    grp = lambda f: pl.BlockSpec((L, GW), lambda g, c: (f(c), g))
    return dict(
        grp=grp,
        bmat=lambda f: pl.BlockSpec((L, SSM_STATE), lambda g, c: (f(c), DI // SSM_STATE + g)),
        cmat=lambda f: pl.BlockSpec((L, SSM_STATE), lambda g, c: (f(c), DI // SSM_STATE + G + g)),
        dtraw=lambda f: pl.BlockSpec((L, LANES), lambda g, c: (f(c), (2 * DI + 2 * G * SSM_STATE) // LANES)),
        vec=pl.BlockSpec((1, LANES), lambda g, c: (0, 0)),
        gvec=pl.BlockSpec((1, GW), lambda g, c: (0, g)),
        sel=pl.BlockSpec((1, LANES, GW), lambda g, c: (g, 0, 0)),
    )


def _ssd_fwd(zx, xc, bias, alog, sel, dskip, ng, DI, carry=None):
    T, ZW = zx.shape
    G, L = SSM_GROUPS, SSM_CHUNK
    GW = DI // G
    NS = GW // LANES
    nc = T // L
    sp = _ssd_specs(G, GW, DI, ZW)
    ident = lambda c: c

    def body(x_ref, b_ref, c_ref, z_ref, dtraw_ref, bias_ref, alog_ref, sel_ref, d_ref, ng_ref,
             y_ref, yo_ref, st_ref, state):
        c = pl.program_id(1)

        @pl.when(c == 0)
        def _():
            state[...] = jnp.zeros_like(state)

        x = x_ref[...]
        bb, cb_ = b_ref[...].astype(BF16), c_ref[...].astype(BF16)
        cbm = _dot(cb_, bb, _NT)
        _, dtx, acx = _ssd_time2(dtraw_ref, bias_ref, alog_ref, sel_ref[0])
        xdt = x * dtx
        ex = jnp.exp(acx)
        last = acx[L - 1:L, :]
        te = jnp.exp(last - acx)
        dlast = jnp.exp(last)
        isa = _is_a((L, LANES))
        for i in range(NS):
            sl = slice(i * LANES, (i + 1) * LANES)
            acs = acx[:, sl]
            acs_t = acs.T
            xs = xdt[:, sl]
            y = jnp.zeros((L, LANES), F32)
            for pos in (0, 1):
                m = (cbm * _decay(acs, acs_t, pos)).astype(BF16)
                y = y + _dot(m, jnp.where(isa if pos == 0 else ~isa, xs, 0.0).astype(BF16))
            st = state[i]
            st_ref[0, i] = st
            y = y + _dot(cb_, st.astype(BF16)) * ex[:, sl]
            state[i] = st * dlast[:, sl] + _dot(bb, (xs * te[:, sl]).astype(BF16), _TN)
            y_ref[:, sl] = y + d_ref[:, sl] * x[:, sl]
        z = z_ref[...]
        gated = y_ref[...] * (z * jax.nn.sigmoid(z))
        rstd = lax.rsqrt(jnp.mean(gated * gated, axis=1, keepdims=True) + EPS)
        yo_ref[...] = (gated * rstd * ng_ref[...]).astype(BF16)

    res = _call(
        body, name="ssd_fwd", grid=(G, nc),
        in_specs=[sp["grp"](ident), sp["bmat"](ident), sp["cmat"](ident), sp["grp"](ident), sp["dtraw"](ident),
                  sp["vec"], sp["vec"], sp["sel"], sp["gvec"], sp["gvec"]],
        out_specs=[sp["grp"](ident), sp["grp"](ident),
                   pl.BlockSpec((1, NS, SSM_STATE, LANES), lambda g, c: (c, g, 0, 0))],
        out_shape=[jax.ShapeDtypeStruct((T, DI), F32), jax.ShapeDtypeStruct((T, DI), BF16),
                   jax.ShapeDtypeStruct((nc, G * NS, SSM_STATE, LANES), F32)],
        scratch_shapes=[pltpu.VMEM((NS, SSM_STATE, LANES), F32)],
        sem=("parallel", "arbitrary"), args=(xc, xc, xc, zx, zx, bias, alog, sel, dskip, ng), carry=carry)
    return res if carry is None else (*res[:3], res[3:])


def _ssd_bwd(zx, xc, yssd, dyo, states, bias, alog, sel, dskip, ng, DI, carry=None):
    T, ZW = zx.shape
    G, L = SSM_GROUPS, SSM_CHUNK
    GW = DI // G
    NS = GW // LANES
    nc = T // L
    sp = _ssd_specs(G, GW, DI, ZW)
    rev = lambda c: nc - 1 - c

    def body(x_ref, b_ref, c_ref, z_ref, dtraw_ref, y_ref, dyo_ref, st_ref, bias_ref, alog_ref, sel_ref,
             d_ref, ng_ref, dz_ref, dx_ref, db_ref, dc_ref, ddt_ref, dac_ref, dd_ref, dng_ref, dstate):
        c = pl.program_id(1)

        @pl.when(c == 0)
        def _():
            dstate[...] = jnp.zeros_like(dstate)
            dd_ref[...] = jnp.zeros_like(dd_ref)
            dng_ref[...] = jnp.zeros_like(dng_ref)

        z, ys, dyo = z_ref[...], y_ref[...], dyo_ref[...]
        sg = jax.nn.sigmoid(z)
        sz = z * sg
        gated = ys * sz
        rstd = lax.rsqrt(jnp.mean(gated * gated, axis=1, keepdims=True) + EPS)
        yn = gated * rstd
        dng_ref[0] += jnp.sum(dyo * yn, axis=0, keepdims=True)
        dyn = dyo * ng_ref[...]
        dgated = rstd * (dyn - yn * jnp.mean(dyn * yn, axis=1, keepdims=True))
        g = dgated * sz
        dz_ref[...] = (dgated * ys * sg * (1.0 + z * (1.0 - sg))).astype(BF16)

        x = x_ref[...]
        dsk = d_ref[...]
        dd_ref[0] += jnp.sum(g * x, axis=0, keepdims=True)
        bb, cb_ = b_ref[...].astype(BF16), c_ref[...].astype(BF16)
        cbm = _dot(cb_, bb, _NT)
        _, dtx, acx = _ssd_time2(dtraw_ref, bias_ref, alog_ref, sel_ref[0])
        xdt = x * dtx
        ex = jnp.exp(acx)
        last = acx[L - 1:L, :]
        te = jnp.exp(last - acx)
        dlast = jnp.exp(last)
        isa = _is_a((L, LANES))
        is_last = lax.broadcasted_iota(jnp.int32, (L, LANES), 0) == L - 1
        strict = lax.broadcasted_iota(jnp.int32, (L, L), 0) > lax.broadcasted_iota(jnp.int32, (L, L), 1)
        lane_id = lax.broadcasted_iota(jnp.int32, (1, LANES), 1)
        row_id = lax.broadcasted_iota(jnp.int32, (8, 1), 0)
        lane_head = lax.broadcasted_iota(jnp.int32, (LANES, LANES), 0) // HEAD
        col_id = lax.broadcasted_iota(jnp.int32, (LANES, LANES), 1)
        dcb = jnp.zeros((L, L), F32)
        dcm = jnp.zeros((L, SSM_STATE), F32)
        dbm = jnp.zeros((L, SSM_STATE), F32)
        q_rows = jnp.zeros((L, LANES), F32)
        q_cols = jnp.zeros((8, L), F32)
        for i in range(NS):
            sl = slice(i * LANES, (i + 1) * LANES)
            acs = acx[:, sl]
            acs_t = acs.T
            xs, gs = xdt[:, sl], g[:, sl]
            xsb = xs.astype(BF16)
            dxd = jnp.zeros((L, LANES), F32)
            for pos in (0, 1):
                gp = jnp.where(isa if pos == 0 else ~isa, gs, 0.0).astype(BF16)
                dec = _decay(acs, acs_t, pos)
                dxd = dxd + _dot((cbm * dec).astype(BF16), gp, _TN)
                dmd = _dot(gp, xsb, _NT) * dec
                dcb = dcb + dmd
                q = jnp.where(strict, dmd * cbm, 0.0)
                q_rows = q_rows + jnp.sum(q, axis=1, keepdims=True) * (lane_id == 2 * i + pos).astype(F32)
                q_cols = q_cols + jnp.sum(q, axis=0, keepdims=True) * (row_id == 2 * i + pos).astype(F32)
            st = st_ref[0, i]
            dst = dstate[i]
            stb, dstb = st.astype(BF16), dst.astype(BF16)
            eg = (ex[:, sl] * gs).astype(BF16)
            dcm = dcm + _dot(eg, stb, _NT)
            yoff = _dot(cb_, stb) * ex[:, sl]
            w = xs * te[:, sl]
            wb = w.astype(BF16)
            dw = _dot(bb, dstb)
            dbm = dbm + _dot(wb, dstb, _NT)
            dxt = dxd + dw * te[:, sl]
            dal = dlast[:, sl] * jnp.sum(dst * st, axis=0, keepdims=True) + jnp.sum(dw * w, axis=0, keepdims=True)
            dac_l = gs * yoff - w * dw + jnp.where(is_last, dal, 0.0)
            ddt_l = dxt * x[:, sl]
            dstate[i] = dst * dlast[:, sl] + _dot(cb_, eg, _TN)
            dx_ref[:, sl] = dxt * dtx[:, sl] + dsk[:, sl] * gs
            to_head = (col_id == 2 * i + lane_head).astype(BF16)
            part = _dot_split(ddt_l, to_head, 2)
            parta = _dot_split(dac_l, to_head, 2)
            if i == 0:
                ddt_ref[0] = part
                dac_ref[0] = parta
            else:
                ddt_ref[0] += part
                dac_ref[0] += parta
        dac_ref[0] += q_rows - jnp.concatenate([q_cols, jnp.zeros((LANES - 8, L), F32)], axis=0).T
        dcbb = dcb.astype(BF16)
        dc_ref[...] = dcm + _dot(dcbb, bb)
        db_ref[...] = dbm + _dot(dcbb, cb_, _TN)

    part_spec = pl.BlockSpec((1, L, LANES), lambda g, c: (g, rev(c), 0))
    lane_spec = pl.BlockSpec((1, 1, GW), lambda g, c: (g, 0, 0))
    bc_out = pl.BlockSpec((L, SSM_STATE), lambda g, c: (rev(c), g))
    res = _call(
        body, name="ssd_bwd", grid=(G, nc),
        in_specs=[sp["grp"](rev), sp["bmat"](rev), sp["cmat"](rev), sp["grp"](rev), sp["dtraw"](rev), sp["grp"](rev),
                  sp["grp"](rev), pl.BlockSpec((1, NS, SSM_STATE, LANES), lambda g, c: (rev(c), g, 0, 0)),
                  sp["vec"], sp["vec"], sp["sel"], sp["gvec"], sp["gvec"]],
        out_specs=[sp["grp"](rev), sp["grp"](rev), bc_out, bc_out, part_spec, part_spec, lane_spec, lane_spec],
        out_shape=[jax.ShapeDtypeStruct((T, ZW), BF16), jax.ShapeDtypeStruct((T, DI), F32),
                   jax.ShapeDtypeStruct((T, G * SSM_STATE), F32), jax.ShapeDtypeStruct((T, G * SSM_STATE), F32),
                   jax.ShapeDtypeStruct((G, T, LANES), F32), jax.ShapeDtypeStruct((G, T, LANES), F32),
                   jax.ShapeDtypeStruct((G, 1, GW), F32), jax.ShapeDtypeStruct((G, 1, GW), F32)],
        scratch_shapes=[pltpu.VMEM((NS, SSM_STATE, LANES), F32)], sem=("parallel", "arbitrary"),
        args=(xc, xc, xc, zx, zx, yssd, dyo, states, bias, alog, sel, dskip, ng), carry=carry)
    return res if carry is None else (*res[:8], res[8:])


def _ssd_dt_bwd(zx, ddt_part, dac_part, dzx, bias, alog, DI):
    T, ZW = zx.shape
    G, L = SSM_GROUPS, SSM_CHUNK
    nc = T // L
    heads = DI // HEAD // G
    tail = ZW - 2 * DI - 2 * G * SSM_STATE
    dt_block = (ZW - tail) // LANES

    def body(dtraw_ref, ddt_ref, dac_ref, bias_ref, alog_ref, _, o_ref, dal_ref, dbias_ref):
        @pl.when(pl.program_id(0) == 0)
        def _():
            dal_ref[...] = jnp.zeros_like(dal_ref)
            dbias_ref[...] = jnp.zeros_like(dbias_ref)

        raw = dtraw_ref[...] + bias_ref[...]
        dt = jax.nn.softplus(raw)
        a = -jnp.exp(alog_ref[...])
        dac, ddt = dac_ref[0], ddt_ref[0]
        for gi in range(1, G):
            dac = dac + pltpu.roll(dac_ref[gi], gi * heads, axis=1)
            ddt = ddt + pltpu.roll(ddt_ref[gi], gi * heads, axis=1)
        dda = _tri_dot(dac, True)
        dal_ref[...] += jnp.sum(dda * dt, axis=0, keepdims=True) * a
        draw = (dda * a + ddt) * jax.nn.sigmoid(raw)
        dbias_ref[...] += jnp.sum(draw, axis=0, keepdims=True)
        o_ref[...] = jnp.concatenate([draw.astype(BF16), jnp.zeros((L, tail - LANES), BF16)], axis=1)

    vec = pl.BlockSpec((1, LANES), lambda c: (0, 0))
    part = pl.BlockSpec((G, L, LANES), lambda c: (0, c, 0))
    return pl.pallas_call(
        body, name="ssd_dt_bwd", grid=(nc,),
        in_specs=[pl.BlockSpec((L, LANES), lambda c: (c, dt_block)), part, part, vec, vec,
                  pl.BlockSpec(memory_space=pl.ANY)],
        out_specs=[pl.BlockSpec((L, tail), lambda c: (c, (ZW - tail) // tail)), vec, vec],
        out_shape=[jax.ShapeDtypeStruct((T, ZW), BF16), jax.ShapeDtypeStruct((1, LANES), F32),
                   jax.ShapeDtypeStruct((1, LANES), F32)],
        input_output_aliases={5: 0},
        compiler_params=_params(("arbitrary",)),
    )(zx, ddt_part, dac_part, bias, alog, dzx)


def _exchange(blocks, name):
    n = len(blocks)

    def body(*refs):
        local, sends, arrivals = _direct_copies(refs[:n], refs[n:2 * n], *refs[2 * n:])
        for cp in local + sends:
            cp.start()
        for cp in arrivals:
            cp.wait_recv()
        for cp in sends:
            cp.wait_send()
        for cp in local:
            cp.wait()

    any_spec = pl.BlockSpec(memory_space=pl.ANY)
    return pl.pallas_call(
        body, name=name, in_specs=[any_spec] * n, out_specs=[any_spec] * n,
        out_shape=[jax.ShapeDtypeStruct(b.shape, b.dtype) for b in blocks],
        scratch_shapes=[pltpu.SemaphoreType.DMA((n, 7)), pltpu.SemaphoreType.DMA((n, 7)),
                        pltpu.SemaphoreType.DMA((n,))],
    )(*blocks)


def _adamw(parts, w, m, v, name):
    nl = len(parts)
    R, C = parts[0].shape[1:]
    per_row = C * (N_DEV * nl * parts[0].dtype.itemsize + 7 * 4) * 2
    tr = R
    if R % 8 == 0:
        tr = 8
        for t in (16, 32, 64, 128, 256, 512):
            if R % t == 0 and t * per_row <= 24 * 1024 * 1024:
                tr = t
    nr = R // tr
    c1 = 1.0 - ADAM_B1 ** ADAM_STEP
    c2 = 1.0 - ADAM_B2 ** ADAM_STEP

    def body(*refs):
        p_refs = refs[:nl]
        w_ref, m_ref, v_ref, g_ref, d_ref, nm_ref, nv_ref = refs[nl:]
        for layer in range(nl):
            @pl.when(pl.program_id(0) == layer)
            def _(p_ref=p_refs[layer]):
                g = p_ref[0].astype(F32)
                for k in range(1, N_DEV):
                    g = g + p_ref[k].astype(F32)
                nm = ADAM_B1 * m_ref[...] + (1.0 - ADAM_B1) * g
                nv = ADAM_B2 * v_ref[...] + (1.0 - ADAM_B2) * (g * g)
                g_ref[...] = g
                nm_ref[...] = nm
                nv_ref[...] = nv
                d_ref[...] = -ADAM_LR * ((nm / c1) / (jnp.sqrt(nv / c2) + ADAM_EPS) + ADAM_WD * w_ref[...])

    def part_spec(layer):
        return pl.BlockSpec((N_DEV, tr, C), lambda l, i: (0, jnp.where(l == layer, i, jnp.where(l < layer, 0, nr - 1)), 0))

    blk = pl.BlockSpec((tr, C), lambda l, i: (l * nr + i, 0))
    out = jax.ShapeDtypeStruct((nl * R, C), F32)
    return pl.pallas_call(
        body, name=name, grid=(nl, nr),
        in_specs=[part_spec(layer) for layer in range(nl)] + [blk, blk, blk],
        out_specs=[blk, blk, blk, blk], out_shape=[out, out, out, out],
        compiler_params=_params(("arbitrary", "arbitrary")),
    )(*parts, w, m, v)


def _pad_cols(a, n):
    return jnp.pad(a, ((0, 0), (0, n - a.shape[1])))


def kernel(x, positions, mixer_norm, ffn_norm, attn_w_qkv, attn_q_norm, attn_k_norm, attn_sinks, attn_w_o, ssm_w_in, ssm_conv_w, ssm_conv_b, ssm_dt_bias, ssm_a_log, ssm_d, ssm_norm, ssm_w_out, ffn_w_gate, ffn_w_up, ffn_w_down, loss_target, m_mixer_norm, m_ffn_norm, m_attn_w_qkv, m_attn_q_norm, m_attn_k_norm, m_attn_sinks, m_attn_w_o, m_ssm_w_in, m_ssm_conv_w, m_ssm_conv_b, m_ssm_dt_bias, m_ssm_a_log, m_ssm_d, m_ssm_norm, m_ssm_w_out, m_ffn_w_gate, m_ffn_w_up, m_ffn_w_down, v_mixer_norm, v_ffn_norm, v_attn_w_qkv, v_attn_q_norm, v_attn_k_norm, v_attn_sinks, v_attn_w_o, v_ssm_w_in, v_ssm_conv_w, v_ssm_conv_b, v_ssm_dt_bias, v_ssm_a_log, v_ssm_d, v_ssm_norm, v_ssm_w_out, v_ffn_w_gate, v_ffn_w_up, v_ffn_w_down):
    T, D = x.shape[1], x.shape[2]
    HQ = D // HEAD
    HKV = HQ // ATT_GROUP
    QW = (HQ + 2 * HKV) * HEAD
    DI = 2 * D
    H = DI // HEAD
    G = SSM_GROUPS
    GW = DI // G
    CD = DI + 2 * G * SSM_STATE
    ZW = -(-(DI + CD + LANES) // 512) * 512
    IW = DI + CD + H
    assert T % 512 == 0 and D % 256 == 0 and HKV % 2 == 0 and GW % LANES == 0 and H <= LANES

    weights = dict(mixer_norm=mixer_norm, ffn_norm=ffn_norm, attn_w_qkv=attn_w_qkv, attn_q_norm=attn_q_norm,
                   attn_k_norm=attn_k_norm, attn_sinks=attn_sinks, attn_w_o=attn_w_o, ssm_w_in=ssm_w_in,
                   ssm_conv_w=ssm_conv_w, ssm_conv_b=ssm_conv_b, ssm_dt_bias=ssm_dt_bias, ssm_a_log=ssm_a_log,
                   ssm_d=ssm_d, ssm_norm=ssm_norm, ssm_w_out=ssm_w_out, ffn_w_gate=ffn_w_gate, ffn_w_up=ffn_w_up,
                   ffn_w_down=ffn_w_down)
    mom_m = dict(mixer_norm=m_mixer_norm, ffn_norm=m_ffn_norm, attn_w_qkv=m_attn_w_qkv, attn_q_norm=m_attn_q_norm,
                 attn_k_norm=m_attn_k_norm, attn_sinks=m_attn_sinks, attn_w_o=m_attn_w_o, ssm_w_in=m_ssm_w_in,
                 ssm_conv_w=m_ssm_conv_w, ssm_conv_b=m_ssm_conv_b, ssm_dt_bias=m_ssm_dt_bias, ssm_a_log=m_ssm_a_log,
                 ssm_d=m_ssm_d, ssm_norm=m_ssm_norm, ssm_w_out=m_ssm_w_out, ffn_w_gate=m_ffn_w_gate,
                 ffn_w_up=m_ffn_w_up, ffn_w_down=m_ffn_w_down)
    mom_v = dict(mixer_norm=v_mixer_norm, ffn_norm=v_ffn_norm, attn_w_qkv=v_attn_w_qkv, attn_q_norm=v_attn_q_norm,
                 attn_k_norm=v_attn_k_norm, attn_sinks=v_attn_sinks, attn_w_o=v_attn_w_o, ssm_w_in=v_ssm_w_in,
                 ssm_conv_w=v_ssm_conv_w, ssm_conv_b=v_ssm_conv_b, ssm_dt_bias=v_ssm_dt_bias, ssm_a_log=v_ssm_a_log,
                 ssm_d=v_ssm_d, ssm_norm=v_ssm_norm, ssm_w_out=v_ssm_w_out, ffn_w_gate=v_ffn_w_gate,
                 ffn_w_up=v_ffn_w_up, ffn_w_down=v_ffn_w_down)
    big = ["attn_w_qkv", "attn_w_o", "ssm_w_in", "ssm_w_out", "ffn_w_gate", "ffn_w_up", "ffn_w_down"]

    def flat2(a):
        return a.reshape(-1, a.shape[-1])

    def shard(n, layer=0):
        return weights[n][layer].astype(BF16)

    def from_cols(g):
        return g.transpose(1, 0, 2).reshape(g.shape[1], N_DEV * g.shape[2])

    def from_rows(g):
        return g.reshape(N_DEV * g.shape[1], g.shape[2])

    xs = x[0]
    tgt = loss_target[0]
    inv_freq = ROPE_THETA ** (-jnp.arange(0, HEAD, 2, dtype=F32) / HEAD)
    ang = positions[0].astype(F32)[:, None] * inv_freq
    cos = jnp.tile(jnp.cos(ang), (1, 4))
    sin = jnp.tile(jnp.concatenate([-jnp.sin(ang), jnp.sin(ang)], axis=1), (1, 2))
    gq = jnp.tile(attn_q_norm, (1, 2))
    gk = jnp.tile(attn_k_norm, (1, 2))
    sinkcol = jnp.repeat(attn_sinks.reshape(HKV, ATT_GROUP // 2, 2).transpose(0, 2, 1), WINDOW, axis=2)[..., None]
    bias_p = _pad_cols(ssm_dt_bias, LANES)
    alog_p = _pad_cols(ssm_a_log, LANES)
    dskip = jnp.repeat(ssm_d, HEAD, axis=1)
    lane_head = jnp.arange(DI) // HEAD
    sel = (jnp.arange(LANES)[None, :, None] == lane_head.reshape(G, 1, GW)).astype(BF16)
    vec_w = CD // N_DEV
    small = jnp.concatenate([ssm_conv_w[0], ssm_conv_b, _pad_cols(ssm_norm, vec_w),
                             jnp.zeros((2, vec_w), F32)], axis=0)

    def rows_to_blocks(p):
        return p.reshape(N_DEV, p.shape[0] // N_DEV, p.shape[1])

    def cols_to_blocks(p):
        return p.reshape(p.shape[0], N_DEV, p.shape[1] // N_DEV).transpose(1, 0, 2)

    hm0, got = _rms_fwd(xs, mixer_norm[0:1], "rms_fwd_m0", carry=[shard("attn_w_qkv")])
    w_qkv = from_cols(got[0])
    qkv, got = _matmul(hm0, w_qkv, mode="nn", out_dtype=F32, name="mm_qkv",
                       carry=[shard("attn_w_o"), shard("ffn_w_gate", 0), small])
    w_o = from_rows(got[0])
    w_gate = [from_cols(got[1]), None]
    small_all = got[2]
    conv_w = small_all[:, :SSM_CONV].transpose(1, 0, 2).reshape(SSM_CONV, CD)
    conv_b = small_all[:, SSM_CONV].reshape(1, CD)
    ng = small_all[:, SSM_CONV + 1, :DI // N_DEV].reshape(1, DI)
    qr, kd, vd = _attn_prep_fwd(qkv, cos, sin, gq, gk, D, HKV)
    o, got = _attn_fwd(qr, kd, vd, sinkcol, HKV, carry=[shard("ffn_w_up", 0), shard("ffn_w_down", 0)])
    w_up = [from_cols(got[0]), None]
    w_down = [from_rows(got[1]), None]
    x1 = _matmul(o, w_o, mode="nn", out_dtype=F32, name="mm_attn_out", add=xs)
    hf0 = _rms_fwd(x1, ffn_norm[0:1], "rms_fwd_f0")
    gate0, up0, act0, got = _ffn_up(hf0, w_gate[0], w_up[0], "ffn_up_0", carry=[shard("ssm_w_in"), shard("ssm_w_out")])
    w_in = _pad_cols(from_cols(got[0]), ZW)
    w_out = from_rows(got[1])
    x2 = _matmul(act0, w_down[0], mode="nn", out_dtype=F32, name="mm_ffn_down_0", add=x1)
    hm1 = _rms_fwd(x2, mixer_norm[1:2], "rms_fwd_m1")
    zx, got = _matmul(hm1, w_in, mode="nn", out_dtype=F32, name="mm_ssm_in",
                      carry=[shard("ffn_w_gate", 1), shard("ffn_w_up", 1)])
    w_gate[1], w_up[1] = from_cols(got[0]), from_cols(got[1])
    xc = _conv_fwd(zx, conv_w, conv_b, DI, CD)
    yssd, yout, states, got = _ssd_fwd(zx, xc, bias_p, alog_p, sel, dskip, ng, DI, carry=[shard("ffn_w_down", 1)])
    w_down[1] = from_rows(got[0])
    x3 = _matmul(yout, w_out, mode="nn", out_dtype=F32, name="mm_ssm_out", add=x2)
    hf1 = _rms_fwd(x3, ffn_norm[1:2], "rms_fwd_f1")
    gate1, up1, act1 = _ffn_up(hf1, w_gate[1], w_up[1], "ffn_up_1")
    x4 = _matmul(act1, w_down[1], mode="nn", out_dtype=F32, name="mm_ffn_down_1", add=x3)
    sq, dx4, dx4b = _loss_head(x4, tgt)
    loss = lax.psum(sq[0, 0] * (0.5 / D), ("x", "y", "c"))

    def halves(blocks):
        half = blocks.shape[1] // 2
        return blocks[:, :half], blocks[:, half:]

    def ffn_bwd(dy, dyb, hf, gate, up, act, layer, xin, gain):
        dg, du = _ffn_dact(dyb, w_down[layer], gate, up, f"ffn_dact_{layer}")
        g_down = _matmul(act, dyb, mode="tn", out_dtype=BF16, name=f"mm_dw_down_{layer}")
        down_a, down_b = halves(rows_to_blocks(g_down))
        g_gate, got_da = _matmul(hf, dg, mode="tn", out_dtype=BF16, name=f"mm_dw_gate_{layer}", carry=[down_a])
        g_up, got_db = _matmul(hf, du, mode="tn", out_dtype=BF16, name=f"mm_dw_up_{layer}", carry=[down_b])
        gate_a, gate_b = halves(cols_to_blocks(g_gate))
        dh, got_ga = _matmul(dg, w_gate[layer], mode="nt", out_dtype=F32, name=f"mm_dh_gate_{layer}", carry=[gate_a])
        dh, got_gb = _matmul(du, w_up[layer], mode="nt", out_dtype=F32, name=f"mm_dh_up_{layer}", add=dh,
                             carry=[gate_b])
        dx, dxb, dgain = _rms_bwd(xin, gain, dh, dy, f"rms_bwd_f{layer}")
        return dx, dxb, dgain, cols_to_blocks(g_up), dict(down=[got_da[0], got_db[0]], gate=[got_ga[0], got_gb[0]])

    dx3, dx3b, d_fn1, up1_blocks, ffn1_got = ffn_bwd(dx4, dx4b, hf1, gate1, up1, act1, 1, x3, ffn_norm[1:2])
    dyo = _matmul(dx3b, w_out, mode="nt", out_dtype=F32, name="mm_dyout")
    g_wout = _matmul(yout, dx3b, mode="tn", out_dtype=BF16, name="mm_dw_ssm_out")
    dzx, dxx, dbm, dcm, ddt_p, dac_p, dd_l, dng_l, got1 = _ssd_bwd(
        zx, xc, yssd, dyo, states, bias_p, alog_p, sel, dskip, ng, DI,
        carry=[up1_blocks, rows_to_blocks(g_wout)])
    dzx, d_alog, d_bias = _ssd_dt_bwd(zx, ddt_p, dac_p, dzx, bias_p, alog_p, DI)
    dzx, d_convw, d_convb = _conv_bwd(zx, [dxx, dbm, dcm], dzx, conv_w, conv_b, DI, CD)
    g_win = _matmul(hm1, dzx, mode="tn", out_dtype=BF16, name="mm_dw_ssm_in")[:, :IW]
    dh, got2 = _matmul(dzx, w_in, mode="nt", out_dtype=F32, name="mm_dh_ssm_in", carry=[cols_to_blocks(g_win)])
    dx2, dx2b, d_mn1 = _rms_bwd(x2, mixer_norm[1:2], dh, dx3, "rms_bwd_m1")
    dx1, dx1b, d_fn0, up0_blocks, ffn0_got = ffn_bwd(dx2, dx2b, hf0, gate0, up0, act0, 0, x1, ffn_norm[0:1])
    do = _matmul(dx1b, w_o, mode="nt", out_dtype=F32, name="mm_do")
    g_wo = _matmul(o, dx1b, mode="tn", out_dtype=BF16, name="mm_dw_attn_out")
    dq, dkd, dvd, dsink, got3 = _attn_bwd(qr, kd, vd, o, do, sinkcol, HKV, carry=[up0_blocks, rows_to_blocks(g_wo)])
    dqkv, dgq_l, dgk_l = _attn_prep_bwd(qkv, dq, dkd, dvd, cos, sin, gq, gk, D, HKV)
    g_wqkv = _matmul(hm0, dqkv, mode="tn", out_dtype=BF16, name="mm_dw_qkv")
    dh, got_c = _matmul(dqkv, w_qkv, mode="nt", out_dtype=F32, name="mm_dh_qkv", carry=[cols_to_blocks(g_wqkv)])
    dx0, _, d_mn0 = _rms_bwd(xs, mixer_norm[0:1], dh, dx1, "rms_bwd_m0")

    d_ng = dng_l.reshape(1, DI)
    vec_send = jnp.concatenate([
        d_convw.reshape(SSM_CONV, N_DEV, vec_w).transpose(1, 0, 2), d_convb.reshape(1, N_DEV, vec_w).transpose(1, 0, 2),
        _pad_cols(d_ng.reshape(N_DEV, DI // N_DEV), vec_w)[:, None, :], jnp.zeros((N_DEV, 2, vec_w), F32)], axis=1)
    d_sinks = dsink[:, :, 0].reshape(1, HQ)
    d_gq = dgq_l[:, :HEAD] + dgq_l[:, HEAD:]
    d_gk = dgk_l[:, :HEAD] + dgk_l[:, HEAD:]
    d_dskip = dd_l.reshape(H, HEAD).sum(axis=1).reshape(1, H)
    rep_names = ["mixer_norm", "ffn_norm", "attn_q_norm", "attn_k_norm", "attn_sinks", "ssm_dt_bias", "ssm_a_log",
                 "ssm_d"]
    rep_grads = [jnp.concatenate([d_mn0, d_mn1], axis=0), jnp.concatenate([d_fn0, d_fn1], axis=0), d_gq, d_gk,
                 d_sinks, d_bias[:, :H], d_alog[:, :H], d_dskip]
    rep_sizes = [weights[n].size for n in rep_names]
    rep_len = -(-sum(rep_sizes) // (8 * LANES)) * 8 * LANES

    def pack(arrs):
        flat = jnp.concatenate([a.reshape(-1) for a in arrs])
        return jnp.pad(flat, (0, rep_len - flat.shape[0])).reshape(rep_len // LANES, LANES)

    rep_send = jnp.broadcast_to(pack(rep_grads)[None], (N_DEV, rep_len // LANES, LANES))
    got4 = _exchange([vec_send, rep_send], "exchange_last")
    parts_of = {
        "attn_w_qkv": [got_c[0]], "attn_w_o": [got3[1]], "ssm_w_in": [got2[0]], "ssm_w_out": [got1[1]],
        "ffn_w_gate": ffn0_got["gate"] + ffn1_got["gate"], "ffn_w_up": [got3[0], got1[0]],
        "ffn_w_down": ffn0_got["down"] + ffn1_got["down"],
    }

    out = {}
    for n in big:
        res = _adamw(parts_of[n], flat2(weights[n]), flat2(mom_m[n]), flat2(mom_v[n]), f"adamw_{n}")
        out[n] = [r.reshape(weights[n].shape) for r in res]

    def vec_block(d):
        return jnp.concatenate([d["ssm_conv_w"][0], d["ssm_conv_b"], _pad_cols(d["ssm_norm"], vec_w),
                                jnp.zeros((2, vec_w), F32)], axis=0)

    res = _adamw([got4[0]], vec_block(weights), vec_block(mom_m), vec_block(mom_v), "adamw_vectors")
    out["ssm_conv_w"] = [r[:SSM_CONV][None] for r in res]
    out["ssm_conv_b"] = [r[SSM_CONV:SSM_CONV + 1] for r in res]
    out["ssm_norm"] = [r[SSM_CONV + 1:SSM_CONV + 2, :DI // N_DEV] for r in res]
    res = _adamw([got4[1]], pack([weights[n] for n in rep_names]), pack([mom_m[n] for n in rep_names]),
                 pack([mom_v[n] for n in rep_names]), "adamw_replicated")
    offs = 0
    for n, sz in zip(rep_names, rep_sizes):
        out[n] = [r.reshape(-1)[offs:offs + sz].reshape(weights[n].shape) for r in res]
        offs += sz

    names = list(weights)
    return (loss, dx0[None], *[out[n][0] for n in names], *[out[n][1] for n in names],
            *[out[n][2] for n in names], *[out[n][3] for n in names])
```

```python
import functools
import math

import jax
import jax.numpy as jnp
from jax import lax
from jax.experimental import pallas as pl
from jax.experimental.pallas import tpu as pltpu

F32 = jnp.float32
BF16 = jnp.bfloat16

N_DEV = 8
EPS = 1e-6
LANES = 128
HEAD = 64
ATT_GROUP = 8
ATT_GW = ATT_GROUP * HEAD
WINDOW = 128
ATT_STEP_BLOCKS = 16
ROPE_THETA = 10000.0
SSM_GROUPS = 8
SSM_STATE = 128
SSM_CONV = 4
SSM_CHUNK = 256
HALO = 8
CONV_ROWS = 1024
ADAM_LR, ADAM_B1, ADAM_B2, ADAM_EPS, ADAM_WD, ADAM_STEP = 0.001, 0.9, 0.999, 1e-08, 0.01, 10
VMEM_LIMIT = 56 * 1024 * 1024
MATMUL_VMEM = 44 * 1024 * 1024
MESH = pl.DeviceIdType.MESH

_NN = (((1,), (0,)), ((), ()))
_NT = (((1,), (1,)), ((), ()))
_TN = (((0,), (0,)), ((), ()))


def _dot(a, b, dims=_NN):
    return lax.dot_general(a, b, dims, preferred_element_type=F32)


def _tile(n, cap):
    if n % LANES:
        return n
    best = LANES
    for t in range(LANES, min(n, cap) + 1, LANES):
        if n % t == 0:
            best = t
    return best


def _params(sem):
    return pltpu.CompilerParams(dimension_semantics=sem, vmem_limit_bytes=VMEM_LIMIT)


def _slot(px, py, pc):
    return 4 * px + 2 * py + pc


def _direct_copies(srcs, dsts, send_sems, recv_sems, local_sems, with_arrivals=True):
    x, y, c = lax.axis_index("x"), lax.axis_index("y"), lax.axis_index("c")
    me = _slot(x, y, c)
    peers = [(x ^ (m >> 2), y ^ ((m >> 1) & 1), c ^ (m & 1)) for m in range(1, N_DEV)]
    local, sends, arrivals = [], [], []
    for w, (src, dst) in enumerate(zip(srcs, dsts)):
        sliced = src.shape == dst.shape
        local.append(pltpu.make_async_copy(src.at[me] if sliced else src, dst.at[me], local_sems.at[w]))
        for k, peer in enumerate(peers):
            sems = dict(send_sem=send_sems.at[w, k], recv_sem=recv_sems.at[w, k], device_id=peer, device_id_type=MESH)
            sends.append(pltpu.make_async_remote_copy(
                src_ref=src.at[_slot(*peer)] if sliced else src, dst_ref=dst.at[me], **sems))
            if with_arrivals:
                arrivals.append(pltpu.make_async_remote_copy(
                    src_ref=src.at[me] if sliced else src, dst_ref=dst.at[_slot(*peer)], **sems))
    return local, sends, arrivals


def _gather_phases(srcs, dsts, send_sems, recv_sems, local_sems):
    x, y, c = lax.axis_index("x"), lax.axis_index("y"), lax.axis_index("c")
    me, sibling = (x, y, c), (x, y, 1 - c)
    chips = [(1 - x, y), (x, 1 - y), (1 - x, 1 - y)]
    n = len(srcs)

    def copy(w, k, block, to, src=None):
        dst = dsts[w].at[_slot(*block)]
        return pltpu.make_async_remote_copy(
            src_ref=dst if src is None else src, dst_ref=dst, send_sem=send_sems.at[w, k],
            recv_sem=recv_sems.at[w, k], device_id=to, device_id_type=MESH)

    def first_sends(w):
        return [copy(w, 0, me, sibling, src=srcs[w])] + [copy(w, 1 + j, me, (*chip, c), src=srcs[w])
                                                         for j, chip in enumerate(chips)]

    def start():
        for w in range(n):
            pltpu.make_async_copy(srcs[w], dsts[w].at[_slot(*me)], local_sems.at[w]).start()
            for cp in first_sends(w):
                cp.start()

    def forward():
        for w in range(n):
            for j, chip in enumerate(chips):
                copy(w, 1 + j, (*chip, c), me).wait_recv()
                copy(w, 4 + j, (*chip, c), sibling).start()

    def finish():
        for w in range(n):
            copy(w, 0, sibling, me).wait_recv()
            for j, chip in enumerate(chips):
                copy(w, 4 + j, (*chip, 1 - c), me).wait_recv()
        for w in range(n):
            for cp in first_sends(w) + [copy(w, 4 + j, (*chip, c), sibling) for j, chip in enumerate(chips)]:
                cp.wait_send()
            pltpu.make_async_copy(srcs[w], dsts[w].at[_slot(*me)], local_sems.at[w]).wait()

    return start, forward, finish


def _call(body, *, name, grid, in_specs, out_specs, out_shape, sem, args, scratch_shapes=(), carry=None):
    if carry is None:
        return pl.pallas_call(body, name=name, grid=grid, in_specs=in_specs, out_specs=out_specs, out_shape=out_shape,
                              scratch_shapes=list(scratch_shapes), compiler_params=_params(sem))(*args)
    n_in, n_out, n_sc, n_c = len(in_specs), len(out_specs), len(scratch_shapes), len(carry)
    gather = all(a.ndim == 2 for a in carry)
    assert gather or all(a.ndim == 3 and a.shape[0] == N_DEV for a in carry)
    recv_shape = [jax.ShapeDtypeStruct((N_DEV,) + a.shape if gather else a.shape, a.dtype) for a in carry]
    n_steps = math.prod(grid)

    def wrapped(*refs):
        ins, c_in = refs[:n_in], refs[n_in:n_in + n_c]
        outs, c_out = refs[n_in + n_c:n_in + n_c + n_out], refs[n_in + n_c + n_out:n_in + 2 * n_c + n_out]
        scr = refs[n_in + 2 * n_c + n_out:n_in + 2 * n_c + n_out + n_sc]
        sems = refs[-3:]
        step = functools.reduce(lambda acc, d: acc * grid[d] + pl.program_id(d), range(len(grid)), 0)
        if gather:
            start, forward, finish = _gather_phases(c_in, c_out, *sems)
        else:
            def start():
                local, sends, _ = _direct_copies(c_in, c_out, *sems, with_arrivals=False)
                for cp in local + sends:
                    cp.start()

            def finish():
                local, sends, arrivals = _direct_copies(c_in, c_out, *sems)
                for cp in arrivals:
                    cp.wait_recv()
                for cp in sends:
                    cp.wait_send()
                for cp in local:
                    cp.wait()

        pl.when(step == 0)(start)
        if gather:
            pl.when(step == min((3 * n_steps) // 4, n_steps - 1))(forward)
        body(*ins, *outs, *scr)
        pl.when(step == n_steps - 1)(finish)

    any_spec = pl.BlockSpec(memory_space=pl.ANY)
    res = pl.pallas_call(
        wrapped, name=name, grid=grid, in_specs=list(in_specs) + [any_spec] * n_c,
        out_specs=list(out_specs) + [any_spec] * n_c, out_shape=list(out_shape) + recv_shape,
        scratch_shapes=list(scratch_shapes) + [pltpu.SemaphoreType.DMA((n_c, N_DEV - 1)),
                                               pltpu.SemaphoreType.DMA((n_c, N_DEV - 1)), pltpu.SemaphoreType.DMA((n_c,))],
        compiler_params=_params(("arbitrary",) * len(grid)),
    )(*args, *carry)
    return res


def _matmul(a, b, *, mode, out_dtype, name, add=None, carry=None):
    if mode == "nn":
        (M, K), N = a.shape, b.shape[1]
    elif mode == "nt":
        (M, K), N = a.shape, b.shape[0]
    else:
        (K, M), N = a.shape, b.shape[1]
    assert a.dtype == BF16 and b.dtype == BF16
    has_add = add is not None
    tn = _tile(N, 512)
    for tm in ((_tile(M, 512),) if mode == "tn" else (_tile(M, 2048), _tile(M, 1024))):
        fixed = 2 * tm * tn * (jnp.dtype(out_dtype).itemsize + (4 if has_add else 0)) + tm * tn * 4
        per_k = 2 * 2 * (tm + tn) + (2 * tm if mode == "tn" else 0)
        tk = _tile(K, max(LANES, (MATMUL_VMEM - fixed) // per_k))
        if tk == K:
            break
    nk = K // tk
    dims = _NT if mode == "nt" else _NN
    if mode == "tn":
        a_spec = pl.BlockSpec((tk, tm), lambda i, j, k: (jnp.where(j == 0, k, 0), i))
    else:
        a_spec = pl.BlockSpec((tm, tk), lambda i, j, k: (i, k))
    b_spec = pl.BlockSpec((tn, tk), lambda i, j, k: (j, k)) if mode == "nt" else pl.BlockSpec((tk, tn), lambda i, j, k: (k, j))
    o_spec = pl.BlockSpec((tm, tn), lambda i, j, k: (i, j))

    def body(*refs):
        a_ref, b_ref = refs[:2]
        add_ref = refs[2] if has_add else None
        o_ref = refs[2 + has_add]
        scratch = list(refs[3 + has_add:])
        at = scratch.pop(0) if mode == "tn" else None
        acc = scratch.pop(0) if nk > 1 else None
        j, k = pl.program_id(1), pl.program_id(2)
        if mode == "tn":
            @pl.when(j == 0)
            def _():
                at[k] = a_ref[...].T

            part = _dot(at[k], b_ref[...], dims)
        else:
            part = _dot(a_ref[...], b_ref[...], dims)

        def finish(r):
            if has_add:
                r = r + add_ref[...]
            o_ref[...] = r.astype(out_dtype)

        if nk == 1:
            finish(part)
        else:
            @pl.when(k == 0)
            def _():
                acc[...] = part

            @pl.when(jnp.logical_and(k > 0, k < nk - 1))
            def _():
                acc[...] += part

            @pl.when(k == nk - 1)
            def _():
                finish(acc[...] + part)

    scratch = ([pltpu.VMEM((nk, tm, tk), BF16)] if mode == "tn" else []) + ([pltpu.VMEM((tm, tn), F32)] if nk > 1 else [])
    res = _call(
        body, name=name, grid=(M // tm, N // tn, nk),
        in_specs=[a_spec, b_spec] + ([o_spec] if has_add else []),
        out_specs=[o_spec], out_shape=[jax.ShapeDtypeStruct((M, N), out_dtype)],
        scratch_shapes=scratch, sem=("parallel", "arbitrary", "arbitrary"),
        args=(a, b, add) if has_add else (a, b), carry=carry)
    return res[0] if carry is None else (res[0], res[1:])


def _rms_fwd(x, gain, name, carry=None):
    T, D = x.shape
    tr = 256

    def body(x_ref, g_ref, h_ref):
        xv = x_ref[...]
        rstd = lax.rsqrt(jnp.mean(xv * xv, axis=1, keepdims=True) + EPS)
        h_ref[...] = (xv * rstd * g_ref[...]).astype(BF16)

    res = _call(
        body, name=name, grid=(T // tr,),
        in_specs=[pl.BlockSpec((tr, D), lambda i: (i, 0)), pl.BlockSpec((1, D), lambda i: (0, 0))],
        out_specs=[pl.BlockSpec((tr, D), lambda i: (i, 0))],
        out_shape=[jax.ShapeDtypeStruct((T, D), BF16)], sem=("parallel",), args=(x, gain), carry=carry)
    return res[0] if carry is None else (res[0], res[1:])


def _rms_bwd(x, gain, dh, dres, name):
    T, D = x.shape
    tr = 256

    def body(x_ref, g_ref, dh_ref, dr_ref, dx_ref, dxb_ref, dg_ref):
        @pl.when(pl.program_id(0) == 0)
        def _():
            dg_ref[...] = jnp.zeros_like(dg_ref)

        xv = x_ref[...]
        rstd = lax.rsqrt(jnp.mean(xv * xv, axis=1, keepdims=True) + EPS)
        xhat = xv * rstd
        dy = dh_ref[...].astype(F32)
        dg_ref[...] += jnp.sum(dy * xhat, axis=0, keepdims=True)
        dxh = dy * g_ref[...]
        dx = dr_ref[...] + rstd * (dxh - xhat * jnp.mean(dxh * xhat, axis=1, keepdims=True))
        dx_ref[...] = dx
        dxb_ref[...] = dx.astype(BF16)

    row = pl.BlockSpec((tr, D), lambda i: (i, 0))
    vec = pl.BlockSpec((1, D), lambda i: (0, 0))
    return pl.pallas_call(
        body, name=name, grid=(T // tr,), in_specs=[row, vec, row, row], out_specs=[row, row, vec],
        out_shape=[jax.ShapeDtypeStruct((T, D), F32), jax.ShapeDtypeStruct((T, D), BF16),
                   jax.ShapeDtypeStruct((1, D), F32)],
        compiler_params=_params(("arbitrary",)),
    )(x, gain, dh, dres)


def _loss_head(y, target):
    T, D = y.shape
    tr = 256

    def body(y_ref, t_ref, s_ref, d_ref, db_ref):
        @pl.when(pl.program_id(0) == 0)
        def _():
            s_ref[...] = jnp.zeros_like(s_ref)

        e = y_ref[...] - t_ref[...]
        s_ref[...] += jnp.sum(jnp.sum(e * e, axis=1, keepdims=True), axis=0, keepdims=True)
        d = e * (1.0 / D)
        d_ref[...] = d
        db_ref[...] = d.astype(BF16)

    row = pl.BlockSpec((tr, D), lambda i: (i, 0))
    return pl.pallas_call(
        body, name="loss_head", grid=(T // tr,), in_specs=[row, row],
        out_specs=[pl.BlockSpec((1, 1), lambda i: (0, 0)), row, row],
        out_shape=[jax.ShapeDtypeStruct((1, 1), F32), jax.ShapeDtypeStruct((T, D), F32),
                   jax.ShapeDtypeStruct((T, D), BF16)],
        compiler_params=_params(("arbitrary",)),
    )(y, target)


def _ffn_up(h, wg, wu, name, carry=None):
    (T, D), Fd = h.shape, wg.shape[1]
    tm, tn = _tile(T, 2048), _tile(Fd, 512)

    def body(h_ref, wg_ref, wu_ref, fg_ref, fu_ref, a_ref):
        hv = h_ref[...]
        g = _dot(hv, wg_ref[...])
        s = jax.nn.sigmoid(g)
        silu = g * s
        fu_ref[...] = silu.astype(BF16)
        u = _dot(hv, wu_ref[...])
        fg_ref[...] = (u * (s + silu * (1.0 - s))).astype(BF16)
        a_ref[...] = (silu * u).astype(BF16)

    w_spec = pl.BlockSpec((D, tn), lambda i, j: (0, j))
    o_spec = pl.BlockSpec((tm, tn), lambda i, j: (i, j))
    res = _call(
        body, name=name, grid=(T // tm, Fd // tn),
        in_specs=[pl.BlockSpec((tm, D), lambda i, j: (i, 0)), w_spec, w_spec],
        out_specs=[o_spec, o_spec, o_spec],
        out_shape=[jax.ShapeDtypeStruct((T, Fd), BF16)] * 3,
        sem=("parallel", "arbitrary"), args=(h, wg, wu), carry=carry)
    return res if carry is None else (*res[:3], res[3:])


def _ffn_dact(dy, wd, fgate, fup, name):
    (T, D), Fd = dy.shape, wd.shape[0]
    tm, tn = _tile(T, 2048), _tile(Fd, 512)

    def body(dy_ref, wd_ref, fg_ref, fu_ref, dg_ref, du_ref):
        da = _dot(dy_ref[...], wd_ref[...], _NT)
        du_ref[...] = (da * fu_ref[...].astype(F32)).astype(BF16)
        dg_ref[...] = (da * fg_ref[...].astype(F32)).astype(BF16)

    o_spec = pl.BlockSpec((tm, tn), lambda i, j: (i, j))
    return pl.pallas_call(
        body, name=name, grid=(T // tm, Fd // tn),
        in_specs=[pl.BlockSpec((tm, D), lambda i, j: (i, 0)), pl.BlockSpec((tn, D), lambda i, j: (j, 0)),
                  o_spec, o_spec],
        out_specs=[o_spec, o_spec],
        out_shape=[jax.ShapeDtypeStruct((T, Fd), BF16), jax.ShapeDtypeStruct((T, Fd), BF16)],
        compiler_params=_params(("parallel", "arbitrary")),
    )(dy, wd, fgate, fup)


def _is_a(shape):
    return lax.broadcasted_iota(jnp.int32, shape, 1) % LANES < HEAD


def _split2(v):
    hi = v.astype(BF16)
    return hi, (v - hi.astype(F32)).astype(BF16)


def _split3(v):
    hi = v.astype(BF16)
    r = v - hi.astype(F32)
    mid = r.astype(BF16)
    return hi, mid, (r - mid.astype(F32)).astype(BF16)


def _dot_split(v, m, pieces, dims=_NN):
    parts = _split3(v) if pieces == 3 else _split2(v)
    out = _dot(parts[0], m, dims)
    for p in parts[1:]:
        out = out + _dot(p, m, dims)
    return out


def _head_blockdiag():
    r = lax.broadcasted_iota(jnp.int32, (LANES, LANES), 0) // HEAD
    c = lax.broadcasted_iota(jnp.int32, (LANES, LANES), 1) // HEAD
    return (r == c).astype(BF16)


def _swap_half(v):
    lane = lax.broadcasted_iota(jnp.int32, v.shape, 1)
    return jnp.where(lane % HEAD < HEAD // 2, pltpu.roll(v, LANES - HEAD // 2, axis=1), pltpu.roll(v, HEAD // 2, axis=1))


def _attn_prep_fwd(qkv, cos, sin, gq, gk, D, HKV):
    T, QW = qkv.shape
    tr = 256
    nq, nk = D // LANES, HKV // 2
    KW = HKV * LANES

    def body(x_ref, cos_ref, sin_ref, gq_ref, gk_ref, q_ref, k_ref, v_ref):
        bd = _head_blockdiag()
        cs, sn = cos_ref[...], sin_ref[...]
        isa = _is_a((tr, LANES))

        def normrope(xv, g):
            ms = _dot_split(xv * xv, bd, 2) * (1.0 / HEAD)
            xn = xv * lax.rsqrt(ms + EPS) * g
            return xn * cs + _swap_half(xn) * sn

        def dup(v):
            r = pltpu.roll(v, HEAD, axis=1)
            return jnp.where(isa, v, r), jnp.where(isa, r, v)

        for s in range(nq):
            sl = slice(s * LANES, (s + 1) * LANES)
            q_ref[:, sl] = normrope(x_ref[:, sl], gq_ref[...]).astype(BF16)
        for s in range(nk):
            ka, kb = dup(normrope(x_ref[:, D + s * LANES:D + (s + 1) * LANES], gk_ref[...]))
            k_ref[:, 2 * s * LANES:(2 * s + 1) * LANES] = ka.astype(BF16)
            k_ref[:, (2 * s + 1) * LANES:(2 * s + 2) * LANES] = kb.astype(BF16)
            va, vb = dup(x_ref[:, D + (nk + s) * LANES:D + (nk + s + 1) * LANES])
            v_ref[:, 2 * s * LANES:(2 * s + 1) * LANES] = va.astype(BF16)
            v_ref[:, (2 * s + 1) * LANES:(2 * s + 2) * LANES] = vb.astype(BF16)

    tab = pl.BlockSpec((tr, LANES), lambda i: (i, 0))
    vec = pl.BlockSpec((1, LANES), lambda i: (0, 0))
    return pl.pallas_call(
        body, name="attn_prep_fwd", grid=(T // tr,),
        in_specs=[pl.BlockSpec((tr, QW), lambda i: (i, 0)), tab, tab, vec, vec],
        out_specs=[pl.BlockSpec((tr, D), lambda i: (i, 0)), pl.BlockSpec((tr, KW), lambda i: (i, 0)),
                   pl.BlockSpec((tr, KW), lambda i: (i, 0))],
        out_shape=[jax.ShapeDtypeStruct((T, D), BF16), jax.ShapeDtypeStruct((T, KW), BF16),
                   jax.ShapeDtypeStruct((T, KW), BF16)],
        compiler_params=_params(("parallel",)),
    )(qkv, cos, sin, gq, gk)


def _attn_prep_bwd(qkv, dq, dkd, dvd, cos, sin, gq, gk, D, HKV):
    T, QW = qkv.shape
    tr = 256
    nq, nk = D // LANES, HKV // 2
    KW = HKV * LANES

    def body(x_ref, dq_ref, dk_ref, dv_ref, cos_ref, sin_ref, gq_ref, gk_ref, o_ref, dgq_ref, dgk_ref):
        @pl.when(pl.program_id(0) == 0)
        def _():
            dgq_ref[...] = jnp.zeros_like(dgq_ref)
            dgk_ref[...] = jnp.zeros_like(dgk_ref)

        bd = _head_blockdiag()
        cs, sn = cos_ref[...], sin_ref[...]
        isa = _is_a((tr, LANES))

        def back(xv, dy, g):
            rstd = lax.rsqrt(_dot_split(xv * xv, bd, 2) * (1.0 / HEAD) + EPS)
            xhat = xv * rstd
            dxn = dy * cs + _swap_half(dy * sn)
            dxh = dxn * g
            mean = _dot_split(dxh * xhat, bd, 2) * (1.0 / HEAD)
            return rstd * (dxh - xhat * mean), jnp.sum(dxn * xhat, axis=0, keepdims=True)

        def fold(s):
            a = dk_ref[:, 2 * s * LANES:(2 * s + 1) * LANES]
            b = dk_ref[:, (2 * s + 1) * LANES:(2 * s + 2) * LANES]
            return jnp.where(isa, a + pltpu.roll(a, HEAD, axis=1), b + pltpu.roll(b, HEAD, axis=1))

        def foldv(s):
            a = dv_ref[:, 2 * s * LANES:(2 * s + 1) * LANES]
            b = dv_ref[:, (2 * s + 1) * LANES:(2 * s + 2) * LANES]
            return jnp.where(isa, a + pltpu.roll(a, HEAD, axis=1), b + pltpu.roll(b, HEAD, axis=1))

        dgq = jnp.zeros((1, LANES), F32)
        for s in range(nq):
            sl = slice(s * LANES, (s + 1) * LANES)
            dx, dg = back(x_ref[:, sl], dq_ref[:, sl], gq_ref[...])
            o_ref[:, sl] = dx.astype(BF16)
            dgq = dgq + dg
        dgq_ref[...] += dgq
        dgk = jnp.zeros((1, LANES), F32)
        for s in range(nk):
            sl = slice(D + s * LANES, D + (s + 1) * LANES)
            dx, dg = back(x_ref[:, sl], fold(s), gk_ref[...])
            o_ref[:, sl] = dx.astype(BF16)
            dgk = dgk + dg
            o_ref[:, D + (nk + s) * LANES:D + (nk + s + 1) * LANES] = foldv(s).astype(BF16)
        dgk_ref[...] += dgk

    tab = pl.BlockSpec((tr, LANES), lambda i: (i, 0))
    vec = pl.BlockSpec((1, LANES), lambda i: (0, 0))
    kv = pl.BlockSpec((tr, KW), lambda i: (i, 0))
    return pl.pallas_call(
        body, name="attn_prep_bwd", grid=(T // tr,),
        in_specs=[pl.BlockSpec((tr, QW), lambda i: (i, 0)), pl.BlockSpec((tr, D), lambda i: (i, 0)), kv, kv,
                  tab, tab, vec, vec],
        out_specs=[pl.BlockSpec((tr, QW), lambda i: (i, 0)), vec, vec],
        out_shape=[jax.ShapeDtypeStruct((T, QW), BF16), jax.ShapeDtypeStruct((1, LANES), F32),
                   jax.ShapeDtypeStruct((1, LANES), F32)],
        compiler_params=_params(("arbitrary",)),
    )(qkv, dq, dkd, dvd, cos, sin, gq, gk)


def _attn_probs(qs, kw, sink_ref, first, scale):
    rows = qs.shape[0]
    qi = lax.broadcasted_iota(jnp.int32, (rows, 2 * WINDOW), 0) % WINDOW
    kj = lax.broadcasted_iota(jnp.int32, (rows, 2 * WINDOW), 1)
    valid = (kj > qi) & (kj <= qi + WINDOW)
    if first is not False:
        valid = valid & jnp.logical_or(jnp.logical_not(first), kj >= WINDOW)
    isa = _is_a(kw.shape)
    out = []
    for pos in (0, 1):
        kp = jnp.where(isa if pos == 0 else ~isa, kw, jnp.zeros_like(kw))
        s = jnp.where(valid, _dot(qs, kp, _NT) * scale, -jnp.inf)
        sink = sink_ref[0, pos]
        m = jnp.maximum(jnp.max(s, axis=1, keepdims=True), sink)
        p = jnp.exp(s - m)
        ps = jnp.exp(sink - m)
        inv = 1.0 / (jnp.sum(p, axis=1, keepdims=True) + ps)
        out.append((p * inv, ps * inv, kp))
    return out


def _attn_specs(qb):
    q = pl.BlockSpec((qb * WINDOW, ATT_GW), lambda g, n: (n, g))
    cur = pl.BlockSpec((qb * WINDOW, LANES), lambda g, n: (n, g))
    prev = pl.BlockSpec((WINDOW, LANES), lambda g, n: (jnp.maximum(qb * n - 1, 0), g))
    sink = pl.BlockSpec((1, 2, ATT_GW, 1), lambda g, n: (g, 0, 0, 0))
    return q, cur, prev, sink


def _stack(ref, s):
    rows = slice(s * WINDOW, (s + 1) * WINDOW)
    return jnp.concatenate([ref[rows, i * LANES:(i + 1) * LANES] for i in range(ATT_GW // LANES)], axis=0)


def _attn_fwd(q, kd, vd, sinkcol, HKV, carry=None):
    T, D = q.shape
    nb = T // WINDOW
    qb = math.gcd(nb, ATT_STEP_BLOCKS)
    scale = HEAD ** -0.5

    def body(q_ref, kp_ref, kc_ref, vp_ref, vc_ref, sink_ref, o_ref):
        n = pl.program_id(1)
        kall = jnp.concatenate([kp_ref[...], kc_ref[...]], axis=0)
        vall = jnp.concatenate([vp_ref[...], vc_ref[...]], axis=0)
        isa = _is_a((2 * WINDOW, LANES))
        for s in range(qb):
            win = slice(s * WINDOW, (s + 2) * WINDOW)
            kw, vw = kall[win], vall[win]
            o = jnp.zeros((ATT_GW, LANES), F32)
            first = (n == 0) if s == 0 else False
            for pos, (probs, _, _) in enumerate(_attn_probs(_stack(q_ref, s), kw, sink_ref, first, scale)):
                vp = jnp.where(isa if pos == 0 else ~isa, vw, jnp.zeros_like(vw))
                o = o + _dot(probs.astype(BF16), vp)
            for i in range(ATT_GW // LANES):
                o_ref[s * WINDOW:(s + 1) * WINDOW, i * LANES:(i + 1) * LANES] = o[i * WINDOW:(i + 1) * WINDOW].astype(BF16)

    qs_, cur, prev, sink = _attn_specs(qb)
    res = _call(
        body, name="attn_fwd", grid=(HKV, nb // qb), in_specs=[qs_, prev, cur, prev, cur, sink], out_specs=[qs_],
        out_shape=[jax.ShapeDtypeStruct((T, D), BF16)], sem=("parallel", "parallel"),
        args=(q, kd, kd, vd, vd, sinkcol), carry=carry)
    return res[0] if carry is None else (res[0], res[1:])


def _attn_bwd(q, kd, vd, o, do, sinkcol, HKV, carry=None):
    T, D = q.shape
    nb = T // WINDOW
    qb = math.gcd(nb, ATT_STEP_BLOCKS)
    scale = HEAD ** -0.5
    KW = HKV * LANES

    def body(q_ref, kp_ref, kc_ref, vp_ref, vc_ref, o_ref, do_ref, sink_ref, dq_ref, dk_ref, dv_ref, ds_ref):
        n = pl.program_id(1)

        @pl.when(n == 0)
        def _():
            dk_ref[...] = jnp.zeros_like(dk_ref)
            dv_ref[...] = jnp.zeros_like(dv_ref)
            ds_ref[...] = jnp.zeros_like(ds_ref)

        kall = jnp.concatenate([kp_ref[...], kc_ref[...]], axis=0)
        vall = jnp.concatenate([vp_ref[...], vc_ref[...]], axis=0)
        isa_w = _is_a((2 * WINDOW, LANES))
        isa_q = _is_a((ATT_GW, LANES))
        for s in range(qb):
            win = slice(s * WINDOW, (s + 2) * WINDOW)
            kw, vw = kall[win], vall[win]
            qs = _stack(q_ref, s)
            dos = _stack(do_ref, s)
            dd = dos * _stack(o_ref, s).astype(F32)
            dob = dos.astype(BF16)
            dqs = jnp.zeros((ATT_GW, LANES), F32)
            dkw, dvw, dsk = [], [], []
            first = (n == 0) if s == 0 else False
            for pos, (probs, psink, kp) in enumerate(_attn_probs(qs, kw, sink_ref, first, scale)):
                sel_w = isa_w if pos == 0 else ~isa_w
                sel_q = isa_q if pos == 0 else ~isa_q
                delta = jnp.sum(jnp.where(sel_q, dd, 0.0), axis=1, keepdims=True)
                vp = jnp.where(sel_w, vw, jnp.zeros_like(vw))
                dp = _dot(dob, vp, _NT)
                dsb = (probs * (dp - delta) * scale).astype(BF16)
                dqs = dqs + _dot(dsb, kp)
                dkw.append(_dot(dsb, qs, _TN))
                dvw.append(_dot(probs.astype(BF16), dob, _TN))
                dsk.append(-psink * delta)
            dkw = jnp.where(isa_w, dkw[0], dkw[1])
            dvw = jnp.where(isa_w, dvw[0], dvw[1])

            def add_window(dkw=dkw, dvw=dvw, s=s):
                start = pl.multiple_of((qb * n + s - 1) * WINDOW, WINDOW)
                dk_ref[pl.ds(start, 2 * WINDOW), :] += dkw
                dv_ref[pl.ds(start, 2 * WINDOW), :] += dvw

            if s == 0:
                @pl.when(n == 0)
                def _(dkw=dkw, dvw=dvw):
                    dk_ref[0:WINDOW, :] += dkw[WINDOW:]
                    dv_ref[0:WINDOW, :] += dvw[WINDOW:]

                pl.when(n > 0)(add_window)
            else:
                add_window()

            rows = []
            for i in range(ATT_GW // LANES):
                dq_ref[s * WINDOW:(s + 1) * WINDOW, i * LANES:(i + 1) * LANES] = dqs[i * WINDOW:(i + 1) * WINDOW]
                for pos in (0, 1):
                    t = jnp.sum(dsk[pos][i * WINDOW:(i + 1) * WINDOW], axis=0, keepdims=True)
                    rows.append(jnp.broadcast_to(t, (1, LANES)))
            ds_ref[0] += jnp.concatenate(rows, axis=0)

    qs_, cur, prev, sink = _attn_specs(qb)
    dqo = pl.BlockSpec((qb * WINDOW, ATT_GW), lambda g, n: (n, g))
    dkv = pl.BlockSpec((T, LANES), lambda g, n: (0, g))
    res = _call(
        body, name="attn_bwd", grid=(HKV, nb // qb),
        in_specs=[qs_, prev, cur, prev, cur, qs_, dqo, sink],
        out_specs=[dqo, dkv, dkv, pl.BlockSpec((1, ATT_GROUP, LANES), lambda g, n: (g, 0, 0))],
        out_shape=[jax.ShapeDtypeStruct((T, D), F32), jax.ShapeDtypeStruct((T, KW), F32),
                   jax.ShapeDtypeStruct((T, KW), F32), jax.ShapeDtypeStruct((HKV, ATT_GROUP, LANES), F32)],
        sem=("parallel", "arbitrary"), args=(q, kd, kd, vd, vd, o, do, sinkcol), carry=carry)
    return res if carry is None else (*res[:4], res[4:])


def _conv_fwd(zx, w, b, DI, CD):
    T = zx.shape[0]
    cw, tc = _tile(math.gcd(DI, CD), 512), math.gcd(T, CONV_ROWS)
    off = DI // cw

    def body(cur_ref, halo_ref, w_ref, b_ref, o_ref):
        i = pl.program_id(1)
        halo = jnp.where(i > 0, halo_ref[...], 0.0)
        ext = jnp.concatenate([halo, cur_ref[...]], axis=0)
        acc = b_ref[...] + w_ref[SSM_CONV - 1:SSM_CONV, :] * ext[HALO:]
        for k in range(SSM_CONV - 1):
            acc = acc + w_ref[k:k + 1, :] * pltpu.roll(ext, SSM_CONV - 1 - k, axis=0)[HALO:]
        o_ref[...] = acc * jax.nn.sigmoid(acc)

    return pl.pallas_call(
        body, name="ssm_conv_fwd", grid=(CD // cw, T // tc),
        in_specs=[pl.BlockSpec((tc, cw), lambda j, i: (i, off + j)),
                  pl.BlockSpec((HALO, cw), lambda j, i: (jnp.maximum(i * (tc // HALO) - 1, 0), off + j)),
                  pl.BlockSpec((SSM_CONV, cw), lambda j, i: (0, j)), pl.BlockSpec((1, cw), lambda j, i: (0, j))],
        out_specs=pl.BlockSpec((tc, cw), lambda j, i: (i, j)),
        out_shape=jax.ShapeDtypeStruct((T, CD), F32),
        compiler_params=_params(("parallel", "parallel")),
    )(zx, zx, w, b)


def _conv_bwd(zx, dparts, dzx, w, b, DI, CD):
    T = zx.shape[0]
    cw, tc = _tile(math.gcd(DI, CD), 512), math.gcd(T, CONV_ROWS)
    off = DI // cw
    nt = T // tc
    hb = tc // HALO
    ends = [0]
    for p in dparts:
        assert p.shape[1] % cw == 0
        ends.append(ends[-1] + p.shape[1] // cw)
    assert ends[-1] == CD // cw
    n_p = len(dparts)

    def body(*refs):
        cur_ref, prev_ref, next_ref = refs[:3]
        d_refs, dn_refs = refs[3:3 + n_p], refs[3 + n_p:3 + 2 * n_p]
        w_ref, b_ref, _, o_ref, dw_ref, db_ref = refs[3 + 2 * n_p:]
        j, i = pl.program_id(0), pl.program_id(1)

        @pl.when(i == 0)
        def _():
            dw_ref[...] = jnp.zeros_like(dw_ref)
            db_ref[...] = jnp.zeros_like(db_ref)

        def pick(prefs):
            v = prefs[n_p - 1][...]
            for p in range(n_p - 2, -1, -1):
                v = jnp.where(j < ends[p + 1], prefs[p][...], v)
            return v

        prev = jnp.where(i > 0, prev_ref[...], 0.0)
        ext = jnp.concatenate([prev, cur_ref[...], next_ref[...]], axis=0)
        u = b_ref[...] + w_ref[SSM_CONV - 1:SSM_CONV, :] * ext
        for k in range(SSM_CONV - 1):
            u = u + w_ref[k:k + 1, :] * pltpu.roll(ext, SSM_CONV - 1 - k, axis=0)
        u = u[HALO:]
        dnext = jnp.where(i < nt - 1, pick(dn_refs), 0.0)
        dxe = jnp.concatenate([pick(d_refs), dnext], axis=0)
        sg = jax.nn.sigmoid(u)
        du = dxe * sg * (1.0 + u * (1.0 - sg))
        n_e = tc + HALO
        dx = w_ref[SSM_CONV - 1:SSM_CONV, :] * du
        for k in range(SSM_CONV - 1):
            dx = dx + w_ref[k:k + 1, :] * pltpu.roll(du, n_e - (SSM_CONV - 1 - k), axis=0)
        o_ref[...] = dx[:tc].astype(BF16)
        duc = du[:tc]
        db_ref[...] += jnp.sum(duc, axis=0, keepdims=True)
        xs = ext[:n_e]
        dws = []
        for k in range(SSM_CONV):
            sh = xs if k == SSM_CONV - 1 else pltpu.roll(xs, SSM_CONV - 1 - k, axis=0)
            dws.append(jnp.sum(duc * sh[HALO:], axis=0, keepdims=True))
        dw_ref[...] += jnp.concatenate(dws, axis=0)

    def part_specs(p):
        lo, n = ends[p], ends[p + 1] - ends[p]

        def inside(j):
            return jnp.logical_and(j >= lo, j < lo + n)

        col = lambda j: jnp.clip(j - lo, 0, n - 1)
        return (pl.BlockSpec((tc, cw), lambda j, i: (jnp.where(inside(j), i, 0), col(j))),
                pl.BlockSpec((HALO, cw), lambda j, i: (jnp.where(inside(j), jnp.minimum((i + 1) * hb, nt * hb - 1), 0), col(j))))

    specs = [part_specs(p) for p in range(n_p)]
    return pl.pallas_call(
        body, name="ssm_conv_bwd", grid=(CD // cw, nt),
        in_specs=[pl.BlockSpec((tc, cw), lambda j, i: (i, off + j)),
                  pl.BlockSpec((HALO, cw), lambda j, i: (jnp.maximum(i * hb - 1, 0), off + j)),
                  pl.BlockSpec((HALO, cw), lambda j, i: (jnp.minimum((i + 1) * hb, nt * hb - 1), off + j))]
        + [s[0] for s in specs] + [s[1] for s in specs]
        + [pl.BlockSpec((SSM_CONV, cw), lambda j, i: (0, j)), pl.BlockSpec((1, cw), lambda j, i: (0, j)),
           pl.BlockSpec(memory_space=pl.ANY)],
        out_specs=[pl.BlockSpec((tc, cw), lambda j, i: (i, off + j)), pl.BlockSpec((SSM_CONV, cw), lambda j, i: (0, j)),
                   pl.BlockSpec((1, cw), lambda j, i: (0, j))],
        out_shape=[jax.ShapeDtypeStruct(dzx.shape, BF16), jax.ShapeDtypeStruct((SSM_CONV, CD), F32),
                   jax.ShapeDtypeStruct((1, CD), F32)],
        input_output_aliases={5 + 2 * n_p: 0},
        compiler_params=_params(("parallel", "arbitrary")),
    )(zx, zx, zx, *dparts, *dparts, w, b, dzx)


def _tri_dot(v, upper):
    L = v.shape[0]
    r = lax.broadcasted_iota(jnp.int32, (L, L), 0)
    c = lax.broadcasted_iota(jnp.int32, (L, L), 1)
    tri = ((r <= c) if upper else (r >= c)).astype(BF16)
    p = _split3(v)
    return _dot(tri, p[0]) + _dot(tri, p[1]) + _dot(tri, p[2])


def _ssd_time2(dtraw_ref, bias_ref, alog_ref, sel):
    dt = jax.nn.softplus(dtraw_ref[...] + bias_ref[...])
    acum = _tri_dot(dt * (-jnp.exp(alog_ref[...])), False)
    return dt, _dot_split(dt, sel, 3), _dot_split(acum, sel, 3)


def _decay(acs, acs_t, pos):
    L = acs.shape[0]
    r = lax.broadcasted_iota(jnp.int32, (L, L), 0)
    c = lax.broadcasted_iota(jnp.int32, (L, L), 1)
    col = acs[:, HEAD * pos:HEAD * pos + 1]
    row = acs_t[HEAD * pos:HEAD * pos + 1, :]
    return jnp.exp(jnp.where(r >= c, col - row, -jnp.inf))


def _ssd_specs(G, GW, DI, ZW):
    L = SSM_CHUNK
    grp = lambda f: pl.BlockSpec((L, GW), lambda g, c: (f(c), g))
    return dict(
        grp=grp,
        bmat=lambda f: pl.BlockSpec((L, SSM_STATE), lambda g, c: (f(c), DI // SSM_STATE + g)),
        cmat=lambda f: pl.BlockSpec((L, SSM_STATE), lambda g, c: (f(c), DI // SSM_STATE + G + g)),
        dtraw=lambda f: pl.BlockSpec((L, LANES), lambda g, c: (f(c), (2 * DI + 2 * G * SSM_STATE) // LANES)),
        vec=pl.BlockSpec((1, LANES), lambda g, c: (0, 0)),
        gvec=pl.BlockSpec((1, GW), lambda g, c: (0, g)),
        sel=pl.BlockSpec((1, LANES, GW), lambda g, c: (g, 0, 0)),
    )


def _ssd_fwd(zx, xc, bias, alog, sel, dskip, ng, DI, carry=None):
    T, ZW = zx.shape
    G, L = SSM_GROUPS, SSM_CHUNK
    GW = DI // G
    NS = GW // LANES
    nc = T // L
    sp = _ssd_specs(G, GW, DI, ZW)
    ident = lambda c: c

    def body(x_ref, b_ref, c_ref, z_ref, dtraw_ref, bias_ref, alog_ref, sel_ref, d_ref, ng_ref,
             y_ref, yo_ref, st_ref, state):
        c = pl.program_id(1)

        @pl.when(c == 0)
        def _():
            state[...] = jnp.zeros_like(state)

        x = x_ref[...]
        bb, cb_ = b_ref[...].astype(BF16), c_ref[...].astype(BF16)
        cbm = _dot(cb_, bb, _NT)
        _, dtx, acx = _ssd_time2(dtraw_ref, bias_ref, alog_ref, sel_ref[0])
        xdt = x * dtx
        ex = jnp.exp(acx)
        last = acx[L - 1:L, :]
        te = jnp.exp(last - acx)
        dlast = jnp.exp(last)
        isa = _is_a((L, LANES))
        for i in range(NS):
            sl = slice(i * LANES, (i + 1) * LANES)
            acs = acx[:, sl]
            acs_t = acs.T
            xs = xdt[:, sl]
            y = jnp.zeros((L, LANES), F32)
            for pos in (0, 1):
                m = (cbm * _decay(acs, acs_t, pos)).astype(BF16)
                y = y + _dot(m, jnp.where(isa if pos == 0 else ~isa, xs, 0.0).astype(BF16))
            st = state[i]
            st_ref[0, i] = st
            y = y + _dot(cb_, st.astype(BF16)) * ex[:, sl]
            state[i] = st * dlast[:, sl] + _dot(bb, (xs * te[:, sl]).astype(BF16), _TN)
            y_ref[:, sl] = y + d_ref[:, sl] * x[:, sl]
        z = z_ref[...]
        gated = y_ref[...] * (z * jax.nn.sigmoid(z))
        rstd = lax.rsqrt(jnp.mean(gated * gated, axis=1, keepdims=True) + EPS)
        yo_ref[...] = (gated * rstd * ng_ref[...]).astype(BF16)

    res = _call(
        body, name="ssd_fwd", grid=(G, nc),
        in_specs=[sp["grp"](ident), sp["bmat"](ident), sp["cmat"](ident), sp["grp"](ident), sp["dtraw"](ident),
                  sp["vec"], sp["vec"], sp["sel"], sp["gvec"], sp["gvec"]],
        out_specs=[sp["grp"](ident), sp["grp"](ident),
                   pl.BlockSpec((1, NS, SSM_STATE, LANES), lambda g, c: (c, g, 0, 0))],
        out_shape=[jax.ShapeDtypeStruct((T, DI), F32), jax.ShapeDtypeStruct((T, DI), BF16),
                   jax.ShapeDtypeStruct((nc, G * NS, SSM_STATE, LANES), F32)],
        scratch_shapes=[pltpu.VMEM((NS, SSM_STATE, LANES), F32)],
        sem=("parallel", "arbitrary"), args=(xc, xc, xc, zx, zx, bias, alog, sel, dskip, ng), carry=carry)
    return res if carry is None else (*res[:3], res[3:])


def _ssd_bwd(zx, xc, yssd, dyo, states, bias, alog, sel, dskip, ng, DI, carry=None):
    T, ZW = zx.shape
    G, L = SSM_GROUPS, SSM_CHUNK
    GW = DI // G
    NS = GW // LANES
    nc = T // L
    sp = _ssd_specs(G, GW, DI, ZW)
    rev = lambda c: nc - 1 - c

    def body(x_ref, b_ref, c_ref, z_ref, dtraw_ref, y_ref, dyo_ref, st_ref, bias_ref, alog_ref, sel_ref,
             d_ref, ng_ref, dz_ref, dx_ref, db_ref, dc_ref, ddt_ref, dac_ref, dd_ref, dng_ref, dstate):
        c = pl.program_id(1)

        @pl.when(c == 0)
        def _():
            dstate[...] = jnp.zeros_like(dstate)
            dd_ref[...] = jnp.zeros_like(dd_ref)
            dng_ref[...] = jnp.zeros_like(dng_ref)

        z, ys, dyo = z_ref[...], y_ref[...], dyo_ref[...]
        sg = jax.nn.sigmoid(z)
        sz = z * sg
        gated = ys * sz
        rstd = lax.rsqrt(jnp.mean(gated * gated, axis=1, keepdims=True) + EPS)
        yn = gated * rstd
        dng_ref[0] += jnp.sum(dyo * yn, axis=0, keepdims=True)
        dyn = dyo * ng_ref[...]
        dgated = rstd * (dyn - yn * jnp.mean(dyn * yn, axis=1, keepdims=True))
        g = dgated * sz
        dz_ref[...] = (dgated * ys * sg * (1.0 + z * (1.0 - sg))).astype(BF16)

        x = x_ref[...]
        dsk = d_ref[...]
        dd_ref[0] += jnp.sum(g * x, axis=0, keepdims=True)
        bb, cb_ = b_ref[...].astype(BF16), c_ref[...].astype(BF16)
        cbm = _dot(cb_, bb, _NT)
        _, dtx, acx = _ssd_time2(dtraw_ref, bias_ref, alog_ref, sel_ref[0])
        xdt = x * dtx
        ex = jnp.exp(acx)
        last = acx[L - 1:L, :]
        te = jnp.exp(last - acx)
        dlast = jnp.exp(last)
        isa = _is_a((L, LANES))
        is_last = lax.broadcasted_iota(jnp.int32, (L, LANES), 0) == L - 1
        strict = lax.broadcasted_iota(jnp.int32, (L, L), 0) > lax.broadcasted_iota(jnp.int32, (L, L), 1)
        lane_id = lax.broadcasted_iota(jnp.int32, (1, LANES), 1)
        row_id = lax.broadcasted_iota(jnp.int32, (8, 1), 0)
        lane_head = lax.broadcasted_iota(jnp.int32, (LANES, LANES), 0) // HEAD
        col_id = lax.broadcasted_iota(jnp.int32, (LANES, LANES), 1)
        dcb = jnp.zeros((L, L), F32)
        dcm = jnp.zeros((L, SSM_STATE), F32)
        dbm = jnp.zeros((L, SSM_STATE), F32)
        q_rows = jnp.zeros((L, LANES), F32)
        q_cols = jnp.zeros((8, L), F32)
        for i in range(NS):
            sl = slice(i * LANES, (i + 1) * LANES)
            acs = acx[:, sl]
            acs_t = acs.T
            xs, gs = xdt[:, sl], g[:, sl]
            xsb = xs.astype(BF16)
            dxd = jnp.zeros((L, LANES), F32)
            for pos in (0, 1):
                gp = jnp.where(isa if pos == 0 else ~isa, gs, 0.0).astype(BF16)
                dec = _decay(acs, acs_t, pos)
                dxd = dxd + _dot((cbm * dec).astype(BF16), gp, _TN)
                dmd = _dot(gp, xsb, _NT) * dec
                dcb = dcb + dmd
                q = jnp.where(strict, dmd * cbm, 0.0)
                q_rows = q_rows + jnp.sum(q, axis=1, keepdims=True) * (lane_id == 2 * i + pos).astype(F32)
                q_cols = q_cols + jnp.sum(q, axis=0, keepdims=True) * (row_id == 2 * i + pos).astype(F32)
            st = st_ref[0, i]
            dst = dstate[i]
            stb, dstb = st.astype(BF16), dst.astype(BF16)
            eg = (ex[:, sl] * gs).astype(BF16)
            dcm = dcm + _dot(eg, stb, _NT)
            yoff = _dot(cb_, stb) * ex[:, sl]
            w = xs * te[:, sl]
            wb = w.astype(BF16)
            dw = _dot(bb, dstb)
            dbm = dbm + _dot(wb, dstb, _NT)
            dxt = dxd + dw * te[:, sl]
            dal = dlast[:, sl] * jnp.sum(dst * st, axis=0, keepdims=True) + jnp.sum(dw * w, axis=0, keepdims=True)
            dac_l = gs * yoff - w * dw + jnp.where(is_last, dal, 0.0)
            ddt_l = dxt * x[:, sl]
            dstate[i] = dst * dlast[:, sl] + _dot(cb_, eg, _TN)
            dx_ref[:, sl] = dxt * dtx[:, sl] + dsk[:, sl] * gs
            to_head = (col_id == 2 * i + lane_head).astype(BF16)
            part = _dot_split(ddt_l, to_head, 2)
            parta = _dot_split(dac_l, to_head, 2)
            if i == 0:
                ddt_ref[0] = part
                dac_ref[0] = parta
            else:
                ddt_ref[0] += part
                dac_ref[0] += parta
        dac_ref[0] += q_rows - jnp.concatenate([q_cols, jnp.zeros((LANES - 8, L), F32)], axis=0).T
        dcbb = dcb.astype(BF16)
        dc_ref[...] = dcm + _dot(dcbb, bb)
        db_ref[...] = dbm + _dot(dcbb, cb_, _TN)

    part_spec = pl.BlockSpec((1, L, LANES), lambda g, c: (g, rev(c), 0))
    lane_spec = pl.BlockSpec((1, 1, GW), lambda g, c: (g, 0, 0))
    bc_out = pl.BlockSpec((L, SSM_STATE), lambda g, c: (rev(c), g))
    res = _call(
        body, name="ssd_bwd", grid=(G, nc),
        in_specs=[sp["grp"](rev), sp["bmat"](rev), sp["cmat"](rev), sp["grp"](rev), sp["dtraw"](rev), sp["grp"](rev),
                  sp["grp"](rev), pl.BlockSpec((1, NS, SSM_STATE, LANES), lambda g, c: (rev(c), g, 0, 0)),
                  sp["vec"], sp["vec"], sp["sel"], sp["gvec"], sp["gvec"]],
        out_specs=[sp["grp"](rev), sp["grp"](rev), bc_out, bc_out, part_spec, part_spec, lane_spec, lane_spec],
        out_shape=[jax.ShapeDtypeStruct((T, ZW), BF16), jax.ShapeDtypeStruct((T, DI), F32),
                   jax.ShapeDtypeStruct((T, G * SSM_STATE), F32), jax.ShapeDtypeStruct((T, G * SSM_STATE), F32),
                   jax.ShapeDtypeStruct((G, T, LANES), F32), jax.ShapeDtypeStruct((G, T, LANES), F32),
                   jax.ShapeDtypeStruct((G, 1, GW), F32), jax.ShapeDtypeStruct((G, 1, GW), F32)],
        scratch_shapes=[pltpu.VMEM((NS, SSM_STATE, LANES), F32)], sem=("parallel", "arbitrary"),
        args=(xc, xc, xc, zx, zx, yssd, dyo, states, bias, alog, sel, dskip, ng), carry=carry)
    return res if carry is None else (*res[:8], res[8:])


def _ssd_dt_bwd(zx, ddt_part, dac_part, dzx, bias, alog, DI):
    T, ZW = zx.shape
    G, L = SSM_GROUPS, SSM_CHUNK
    nc = T // L
    heads = DI // HEAD // G
    tail = ZW - 2 * DI - 2 * G * SSM_STATE
    dt_block = (ZW - tail) // LANES

    def body(dtraw_ref, ddt_ref, dac_ref, bias_ref, alog_ref, _, o_ref, dal_ref, dbias_ref):
        @pl.when(pl.program_id(0) == 0)
        def _():
            dal_ref[...] = jnp.zeros_like(dal_ref)
            dbias_ref[...] = jnp.zeros_like(dbias_ref)

        raw = dtraw_ref[...] + bias_ref[...]
        dt = jax.nn.softplus(raw)
        a = -jnp.exp(alog_ref[...])
        dac, ddt = dac_ref[0], ddt_ref[0]
        for gi in range(1, G):
            dac = dac + pltpu.roll(dac_ref[gi], gi * heads, axis=1)
            ddt = ddt + pltpu.roll(ddt_ref[gi], gi * heads, axis=1)
        dda = _tri_dot(dac, True)
        dal_ref[...] += jnp.sum(dda * dt, axis=0, keepdims=True) * a
        draw = (dda * a + ddt) * jax.nn.sigmoid(raw)
        dbias_ref[...] += jnp.sum(draw, axis=0, keepdims=True)
        o_ref[...] = jnp.concatenate([draw.astype(BF16), jnp.zeros((L, tail - LANES), BF16)], axis=1)

    vec = pl.BlockSpec((1, LANES), lambda c: (0, 0))
    part = pl.BlockSpec((G, L, LANES), lambda c: (0, c, 0))
    return pl.pallas_call(
        body, name="ssd_dt_bwd", grid=(nc,),
        in_specs=[pl.BlockSpec((L, LANES), lambda c: (c, dt_block)), part, part, vec, vec,
                  pl.BlockSpec(memory_space=pl.ANY)],
        out_specs=[pl.BlockSpec((L, tail), lambda c: (c, (ZW - tail) // tail)), vec, vec],
        out_shape=[jax.ShapeDtypeStruct((T, ZW), BF16), jax.ShapeDtypeStruct((1, LANES), F32),
                   jax.ShapeDtypeStruct((1, LANES), F32)],
        input_output_aliases={5: 0},
        compiler_params=_params(("arbitrary",)),
    )(zx, ddt_part, dac_part, bias, alog, dzx)


def _exchange(blocks, name):
    n = len(blocks)

    def body(*refs):
        local, sends, arrivals = _direct_copies(refs[:n], refs[n:2 * n], *refs[2 * n:])
        for cp in local + sends:
            cp.start()
        for cp in arrivals:
            cp.wait_recv()
        for cp in sends:
            cp.wait_send()
        for cp in local:
            cp.wait()

    any_spec = pl.BlockSpec(memory_space=pl.ANY)
    return pl.pallas_call(
        body, name=name, in_specs=[any_spec] * n, out_specs=[any_spec] * n,
        out_shape=[jax.ShapeDtypeStruct(b.shape, b.dtype) for b in blocks],
        scratch_shapes=[pltpu.SemaphoreType.DMA((n, 7)), pltpu.SemaphoreType.DMA((n, 7)),
                        pltpu.SemaphoreType.DMA((n,))],
    )(*blocks)


def _adamw(parts, w, m, v, name):
    nl = len(parts)
    R, C = parts[0].shape[1:]
    per_row = C * (N_DEV * nl * parts[0].dtype.itemsize + 7 * 4) * 2
    tr = R
    if R % 8 == 0:
        tr = 8
        for t in (16, 32, 64, 128, 256, 512):
            if R % t == 0 and t * per_row <= 24 * 1024 * 1024:
                tr = t
    nr = R // tr
    c1 = 1.0 - ADAM_B1 ** ADAM_STEP
    c2 = 1.0 - ADAM_B2 ** ADAM_STEP

    def body(*refs):
        p_refs = refs[:nl]
        w_ref, m_ref, v_ref, g_ref, d_ref, nm_ref, nv_ref = refs[nl:]
        for layer in range(nl):
            @pl.when(pl.program_id(0) == layer)
            def _(p_ref=p_refs[layer]):
                g = p_ref[0].astype(F32)
                for k in range(1, N_DEV):
                    g = g + p_ref[k].astype(F32)
                nm = ADAM_B1 * m_ref[...] + (1.0 - ADAM_B1) * g
                nv = ADAM_B2 * v_ref[...] + (1.0 - ADAM_B2) * (g * g)
                g_ref[...] = g
                nm_ref[...] = nm
                nv_ref[...] = nv
                d_ref[...] = -ADAM_LR * ((nm / c1) / (jnp.sqrt(nv / c2) + ADAM_EPS) + ADAM_WD * w_ref[...])

    def part_spec(layer):
        return pl.BlockSpec((N_DEV, tr, C), lambda l, i: (0, jnp.where(l == layer, i, jnp.where(l < layer, 0, nr - 1)), 0))

    blk = pl.BlockSpec((tr, C), lambda l, i: (l * nr + i, 0))
    out = jax.ShapeDtypeStruct((nl * R, C), F32)
    return pl.pallas_call(
        body, name=name, grid=(nl, nr),
        in_specs=[part_spec(layer) for layer in range(nl)] + [blk, blk, blk],
        out_specs=[blk, blk, blk, blk], out_shape=[out, out, out, out],
        compiler_params=_params(("arbitrary", "arbitrary")),
    )(*parts, w, m, v)


def _pad_cols(a, n):
    return jnp.pad(a, ((0, 0), (0, n - a.shape[1])))


def kernel(x, positions, mixer_norm, ffn_norm, attn_w_qkv, attn_q_norm, attn_k_norm, attn_sinks, attn_w_o, ssm_w_in, ssm_conv_w, ssm_conv_b, ssm_dt_bias, ssm_a_log, ssm_d, ssm_norm, ssm_w_out, ffn_w_gate, ffn_w_up, ffn_w_down, loss_target, m_mixer_norm, m_ffn_norm, m_attn_w_qkv, m_attn_q_norm, m_attn_k_norm, m_attn_sinks, m_attn_w_o, m_ssm_w_in, m_ssm_conv_w, m_ssm_conv_b, m_ssm_dt_bias, m_ssm_a_log, m_ssm_d, m_ssm_norm, m_ssm_w_out, m_ffn_w_gate, m_ffn_w_up, m_ffn_w_down, v_mixer_norm, v_ffn_norm, v_attn_w_qkv, v_attn_q_norm, v_attn_k_norm, v_attn_sinks, v_attn_w_o, v_ssm_w_in, v_ssm_conv_w, v_ssm_conv_b, v_ssm_dt_bias, v_ssm_a_log, v_ssm_d, v_ssm_norm, v_ssm_w_out, v_ffn_w_gate, v_ffn_w_up, v_ffn_w_down):
    T, D = x.shape[1], x.shape[2]
    HQ = D // HEAD
    HKV = HQ // ATT_GROUP
    QW = (HQ + 2 * HKV) * HEAD
    DI = 2 * D
    H = DI // HEAD
    G = SSM_GROUPS
    GW = DI // G
    CD = DI + 2 * G * SSM_STATE
    ZW = -(-(DI + CD + LANES) // 512) * 512
    IW = DI + CD + H
    assert T % 512 == 0 and D % 256 == 0 and HKV % 2 == 0 and GW % LANES == 0 and H <= LANES

    weights = dict(mixer_norm=mixer_norm, ffn_norm=ffn_norm, attn_w_qkv=attn_w_qkv, attn_q_norm=attn_q_norm,
                   attn_k_norm=attn_k_norm, attn_sinks=attn_sinks, attn_w_o=attn_w_o, ssm_w_in=ssm_w_in,
                   ssm_conv_w=ssm_conv_w, ssm_conv_b=ssm_conv_b, ssm_dt_bias=ssm_dt_bias, ssm_a_log=ssm_a_log,
                   ssm_d=ssm_d, ssm_norm=ssm_norm, ssm_w_out=ssm_w_out, ffn_w_gate=ffn_w_gate, ffn_w_up=ffn_w_up,
                   ffn_w_down=ffn_w_down)
    mom_m = dict(mixer_norm=m_mixer_norm, ffn_norm=m_ffn_norm, attn_w_qkv=m_attn_w_qkv, attn_q_norm=m_attn_q_norm,
                 attn_k_norm=m_attn_k_norm, attn_sinks=m_attn_sinks, attn_w_o=m_attn_w_o, ssm_w_in=m_ssm_w_in,
                 ssm_conv_w=m_ssm_conv_w, ssm_conv_b=m_ssm_conv_b, ssm_dt_bias=m_ssm_dt_bias, ssm_a_log=m_ssm_a_log,
                 ssm_d=m_ssm_d, ssm_norm=m_ssm_norm, ssm_w_out=m_ssm_w_out, ffn_w_gate=m_ffn_w_gate,
                 ffn_w_up=m_ffn_w_up, ffn_w_down=m_ffn_w_down)
    mom_v = dict(mixer_norm=v_mixer_norm, ffn_norm=v_ffn_norm, attn_w_qkv=v_attn_w_qkv, attn_q_norm=v_attn_q_norm,
                 attn_k_norm=v_attn_k_norm, attn_sinks=v_attn_sinks, attn_w_o=v_attn_w_o, ssm_w_in=v_ssm_w_in,
                 ssm_conv_w=v_ssm_conv_w, ssm_conv_b=v_ssm_conv_b, ssm_dt_bias=v_ssm_dt_bias, ssm_a_log=v_ssm_a_log,
                 ssm_d=v_ssm_d, ssm_norm=v_ssm_norm, ssm_w_out=v_ssm_w_out, ffn_w_gate=v_ffn_w_gate,
                 ffn_w_up=v_ffn_w_up, ffn_w_down=v_ffn_w_down)
    big = ["attn_w_qkv", "attn_w_o", "ssm_w_in", "ssm_w_out", "ffn_w_gate", "ffn_w_up", "ffn_w_down"]

    def flat2(a):
        return a.reshape(-1, a.shape[-1])

    def shard(n, layer=0):
        return weights[n][layer].astype(BF16)

    def from_cols(g):
        return g.transpose(1, 0, 2).reshape(g.shape[1], N_DEV * g.shape[2])

    def from_rows(g):
        return g.reshape(N_DEV * g.shape[1], g.shape[2])

    xs = x[0]
    tgt = loss_target[0]
    inv_freq = ROPE_THETA ** (-jnp.arange(0, HEAD, 2, dtype=F32) / HEAD)
    ang = positions[0].astype(F32)[:, None] * inv_freq
    cos = jnp.tile(jnp.cos(ang), (1, 4))
    sin = jnp.tile(jnp.concatenate([-jnp.sin(ang), jnp.sin(ang)], axis=1), (1, 2))
    gq = jnp.tile(attn_q_norm, (1, 2))
    gk = jnp.tile(attn_k_norm, (1, 2))
    sinkcol = jnp.repeat(attn_sinks.reshape(HKV, ATT_GROUP // 2, 2).transpose(0, 2, 1), WINDOW, axis=2)[..., None]
    bias_p = _pad_cols(ssm_dt_bias, LANES)
    alog_p = _pad_cols(ssm_a_log, LANES)
    dskip = jnp.repeat(ssm_d, HEAD, axis=1)
    lane_head = jnp.arange(DI) // HEAD
    sel = (jnp.arange(LANES)[None, :, None] == lane_head.reshape(G, 1, GW)).astype(BF16)
    vec_w = CD // N_DEV
    small = jnp.concatenate([ssm_conv_w[0], ssm_conv_b, _pad_cols(ssm_norm, vec_w),
                             jnp.zeros((2, vec_w), F32)], axis=0)

    def rows_to_blocks(p):
        return p.reshape(N_DEV, p.shape[0] // N_DEV, p.shape[1])

    def cols_to_blocks(p):
        return p.reshape(p.shape[0], N_DEV, p.shape[1] // N_DEV).transpose(1, 0, 2)

    hm0, got = _rms_fwd(xs, mixer_norm[0:1], "rms_fwd_m0", carry=[shard("attn_w_qkv")])
    w_qkv = from_cols(got[0])
    qkv, got = _matmul(hm0, w_qkv, mode="nn", out_dtype=F32, name="mm_qkv",
                       carry=[shard("attn_w_o"), shard("ffn_w_gate", 0), small])
    w_o = from_rows(got[0])
    w_gate = [from_cols(got[1]), None]
    small_all = got[2]
    conv_w = small_all[:, :SSM_CONV].transpose(1, 0, 2).reshape(SSM_CONV, CD)
    conv_b = small_all[:, SSM_CONV].reshape(1, CD)
    ng = small_all[:, SSM_CONV + 1, :DI // N_DEV].reshape(1, DI)
    qr, kd, vd = _attn_prep_fwd(qkv, cos, sin, gq, gk, D, HKV)
    o, got = _attn_fwd(qr, kd, vd, sinkcol, HKV, carry=[shard("ffn_w_up", 0)])
    w_up = [from_cols(got[0]), None]
    x1, got = _matmul(o, w_o, mode="nn", out_dtype=F32, name="mm_attn_out", add=xs, carry=[shard("ffn_w_down", 0)])
    w_down = [from_rows(got[0]), None]
    hf0 = _rms_fwd(x1, ffn_norm[0:1], "rms_fwd_f0")
    gate0, up0, act0, got = _ffn_up(hf0, w_gate[0], w_up[0], "ffn_up_0", carry=[shard("ssm_w_in"), shard("ssm_w_out")])
    w_in = _pad_cols(from_cols(got[0]), ZW)
    w_out = from_rows(got[1])
    x2 = _matmul(act0, w_down[0], mode="nn", out_dtype=F32, name="mm_ffn_down_0", add=x1)
    hm1 = _rms_fwd(x2, mixer_norm[1:2], "rms_fwd_m1")
    zx, got = _matmul(hm1, w_in, mode="nn", out_dtype=F32, name="mm_ssm_in",
                      carry=[shard("ffn_w_gate", 1), shard("ffn_w_up", 1)])
    w_gate[1], w_up[1] = from_cols(got[0]), from_cols(got[1])
    xc = _conv_fwd(zx, conv_w, conv_b, DI, CD)
    yssd, yout, states, got = _ssd_fwd(zx, xc, bias_p, alog_p, sel, dskip, ng, DI, carry=[shard("ffn_w_down", 1)])
    w_down[1] = from_rows(got[0])
    x3 = _matmul(yout, w_out, mode="nn", out_dtype=F32, name="mm_ssm_out", add=x2)
    hf1 = _rms_fwd(x3, ffn_norm[1:2], "rms_fwd_f1")
    gate1, up1, act1 = _ffn_up(hf1, w_gate[1], w_up[1], "ffn_up_1")
    x4 = _matmul(act1, w_down[1], mode="nn", out_dtype=F32, name="mm_ffn_down_1", add=x3)
    sq, dx4, dx4b = _loss_head(x4, tgt)
    loss = lax.psum(sq[0, 0] * (0.5 / D), ("x", "y", "c"))

    def halves(blocks):
        half = blocks.shape[1] // 2
        return blocks[:, :half], blocks[:, half:]

    def ffn_bwd(dy, dyb, hf, gate, up, act, layer, xin, gain):
        dg, du = _ffn_dact(dyb, w_down[layer], gate, up, f"ffn_dact_{layer}")
        g_down = _matmul(act, dyb, mode="tn", out_dtype=BF16, name=f"mm_dw_down_{layer}")
        down_a, down_b = halves(rows_to_blocks(g_down))
        g_gate, got_da = _matmul(hf, dg, mode="tn", out_dtype=BF16, name=f"mm_dw_gate_{layer}", carry=[down_a])
        g_up, got_db = _matmul(hf, du, mode="tn", out_dtype=BF16, name=f"mm_dw_up_{layer}", carry=[down_b])
        gate_a, gate_b = halves(cols_to_blocks(g_gate))
        dh, got_ga = _matmul(dg, w_gate[layer], mode="nt", out_dtype=F32, name=f"mm_dh_gate_{layer}", carry=[gate_a])
        dh, got_gb = _matmul(du, w_up[layer], mode="nt", out_dtype=F32, name=f"mm_dh_up_{layer}", add=dh,
                             carry=[gate_b])
        dx, dxb, dgain = _rms_bwd(xin, gain, dh, dy, f"rms_bwd_f{layer}")
        return dx, dxb, dgain, cols_to_blocks(g_up), dict(down=[got_da[0], got_db[0]], gate=[got_ga[0], got_gb[0]])

    dx3, dx3b, d_fn1, up1_blocks, ffn1_got = ffn_bwd(dx4, dx4b, hf1, gate1, up1, act1, 1, x3, ffn_norm[1:2])
    dyo = _matmul(dx3b, w_out, mode="nt", out_dtype=F32, name="mm_dyout")
    g_wout = _matmul(yout, dx3b, mode="tn", out_dtype=BF16, name="mm_dw_ssm_out")
    dzx, dxx, dbm, dcm, ddt_p, dac_p, dd_l, dng_l, got1 = _ssd_bwd(
        zx, xc, yssd, dyo, states, bias_p, alog_p, sel, dskip, ng, DI,
        carry=[up1_blocks, rows_to_blocks(g_wout)])
    dzx, d_alog, d_bias = _ssd_dt_bwd(zx, ddt_p, dac_p, dzx, bias_p, alog_p, DI)
    dzx, d_convw, d_convb = _conv_bwd(zx, [dxx, dbm, dcm], dzx, conv_w, conv_b, DI, CD)
    g_win = _matmul(hm1, dzx, mode="tn", out_dtype=BF16, name="mm_dw_ssm_in")[:, :IW]
    dh, got2 = _matmul(dzx, w_in, mode="nt", out_dtype=F32, name="mm_dh_ssm_in", carry=[cols_to_blocks(g_win)])
    dx2, dx2b, d_mn1 = _rms_bwd(x2, mixer_norm[1:2], dh, dx3, "rms_bwd_m1")
    dx1, dx1b, d_fn0, up0_blocks, ffn0_got = ffn_bwd(dx2, dx2b, hf0, gate0, up0, act0, 0, x1, ffn_norm[0:1])
    do = _matmul(dx1b, w_o, mode="nt", out_dtype=F32, name="mm_do")
    g_wo = _matmul(o, dx1b, mode="tn", out_dtype=BF16, name="mm_dw_attn_out")
    dq, dkd, dvd, dsink, got3 = _attn_bwd(qr, kd, vd, o, do, sinkcol, HKV, carry=[up0_blocks, rows_to_blocks(g_wo)])
    dqkv, dgq_l, dgk_l = _attn_prep_bwd(qkv, dq, dkd, dvd, cos, sin, gq, gk, D, HKV)
    g_wqkv = _matmul(hm0, dqkv, mode="tn", out_dtype=BF16, name="mm_dw_qkv")
    dh, got_c = _matmul(dqkv, w_qkv, mode="nt", out_dtype=F32, name="mm_dh_qkv", carry=[cols_to_blocks(g_wqkv)])
    dx0, _, d_mn0 = _rms_bwd(xs, mixer_norm[0:1], dh, dx1, "rms_bwd_m0")

    d_ng = dng_l.reshape(1, DI)
    vec_send = jnp.concatenate([
        d_convw.reshape(SSM_CONV, N_DEV, vec_w).transpose(1, 0, 2), d_convb.reshape(1, N_DEV, vec_w).transpose(1, 0, 2),
        _pad_cols(d_ng.reshape(N_DEV, DI // N_DEV), vec_w)[:, None, :], jnp.zeros((N_DEV, 2, vec_w), F32)], axis=1)
    d_sinks = dsink[:, :, 0].reshape(1, HQ)
    d_gq = dgq_l[:, :HEAD] + dgq_l[:, HEAD:]
    d_gk = dgk_l[:, :HEAD] + dgk_l[:, HEAD:]
    d_dskip = dd_l.reshape(H, HEAD).sum(axis=1).reshape(1, H)
    rep_names = ["mixer_norm", "ffn_norm", "attn_q_norm", "attn_k_norm", "attn_sinks", "ssm_dt_bias", "ssm_a_log",
                 "ssm_d"]
    rep_grads = [jnp.concatenate([d_mn0, d_mn1], axis=0), jnp.concatenate([d_fn0, d_fn1], axis=0), d_gq, d_gk,
                 d_sinks, d_bias[:, :H], d_alog[:, :H], d_dskip]
    rep_sizes = [weights[n].size for n in rep_names]
    rep_len = -(-sum(rep_sizes) // (8 * LANES)) * 8 * LANES

    def pack(arrs):
        flat = jnp.concatenate([a.reshape(-1) for a in arrs])
        return jnp.pad(flat, (0, rep_len - flat.shape[0])).reshape(rep_len // LANES, LANES)

    rep_send = jnp.broadcast_to(pack(rep_grads)[None], (N_DEV, rep_len // LANES, LANES))
    got4 = _exchange([vec_send, rep_send], "exchange_last")
    parts_of = {
        "attn_w_qkv": [got_c[0]], "attn_w_o": [got3[1]], "ssm_w_in": [got2[0]], "ssm_w_out": [got1[1]],
        "ffn_w_gate": ffn0_got["gate"] + ffn1_got["gate"], "ffn_w_up": [got3[0], got1[0]],
        "ffn_w_down": ffn0_got["down"] + ffn1_got["down"],
    }

    out = {}
    for n in big:
        res = _adamw(parts_of[n], flat2(weights[n]), flat2(mom_m[n]), flat2(mom_v[n]), f"adamw_{n}")
        out[n] = [r.reshape(weights[n].shape) for r in res]

    def vec_block(d):
        return jnp.concatenate([d["ssm_conv_w"][0], d["ssm_conv_b"], _pad_cols(d["ssm_norm"], vec_w),
                                jnp.zeros((2, vec_w), F32)], axis=0)

    res = _adamw([got4[0]], vec_block(weights), vec_block(mom_m), vec_block(mom_v), "adamw_vectors")
    out["ssm_conv_w"] = [r[:SSM_CONV][None] for r in res]
    out["ssm_conv_b"] = [r[SSM_CONV:SSM_CONV + 1] for r in res]
    out["ssm_norm"] = [r[SSM_CONV + 1:SSM_CONV + 2, :DI // N_DEV] for r in res]
    res = _adamw([got4[1]], pack([weights[n] for n in rep_names]), pack([mom_m[n] for n in rep_names]),
                 pack([mom_v[n] for n in rep_names]), "adamw_replicated")
    offs = 0
    for n, sz in zip(rep_names, rep_sizes):
        out[n] = [r.reshape(-1)[offs:offs + sz].reshape(weights[n].shape) for r in res]
        offs += sz

    names = list(weights)
    return (loss, dx0[None], *[out[n][0] for n in names], *[out[n][1] for n in names],
            *[out[n][2] for n in names], *[out[n][3] for n in names])
```

```python
import functools
import math

import jax
import jax.numpy as jnp
from jax import lax
from jax.experimental import pallas as pl
from jax.experimental.pallas import tpu as pltpu

F32 = jnp.float32
BF16 = jnp.bfloat16

N_DEV = 8
EPS = 1e-6
LANES = 128
HEAD = 64
ATT_GROUP = 8
ATT_GW = ATT_GROUP * HEAD
WINDOW = 128
ATT_STEP_BLOCKS = 16
ROPE_THETA = 10000.0
SSM_GROUPS = 8
SSM_STATE = 128
SSM_CONV = 4
SSM_CHUNK = 256
HALO = 8
CONV_ROWS = 1024
ADAM_LR, ADAM_B1, ADAM_B2, ADAM_EPS, ADAM_WD, ADAM_STEP = 0.001, 0.9, 0.999, 1e-08, 0.01, 10
VMEM_LIMIT = 56 * 1024 * 1024
MATMUL_VMEM = 44 * 1024 * 1024
MESH = pl.DeviceIdType.MESH

_NN = (((1,), (0,)), ((), ()))
_NT = (((1,), (1,)), ((), ()))
_TN = (((0,), (0,)), ((), ()))


def _dot(a, b, dims=_NN):
    return lax.dot_general(a, b, dims, preferred_element_type=F32)


def _tile(n, cap):
    if n % LANES:
        return n
    best = LANES
    for t in range(LANES, min(n, cap) + 1, LANES):
        if n % t == 0:
            best = t
    return best


def _params(sem):
    return pltpu.CompilerParams(dimension_semantics=sem, vmem_limit_bytes=VMEM_LIMIT)


def _slot(px, py, pc):
    return 4 * px + 2 * py + pc


def _direct_copies(srcs, dsts, send_sems, recv_sems, local_sems, with_arrivals=True):
    x, y, c = lax.axis_index("x"), lax.axis_index("y"), lax.axis_index("c")
    me = _slot(x, y, c)
    peers = [(x ^ (m >> 2), y ^ ((m >> 1) & 1), c ^ (m & 1)) for m in range(1, N_DEV)]
    local, sends, arrivals = [], [], []
    for w, (src, dst) in enumerate(zip(srcs, dsts)):
        sliced = src.shape == dst.shape
        local.append(pltpu.make_async_copy(src.at[me] if sliced else src, dst.at[me], local_sems.at[w]))
        for k, peer in enumerate(peers):
            sems = dict(send_sem=send_sems.at[w, k], recv_sem=recv_sems.at[w, k], device_id=peer, device_id_type=MESH)
            sends.append(pltpu.make_async_remote_copy(
                src_ref=src.at[_slot(*peer)] if sliced else src, dst_ref=dst.at[me], **sems))
            if with_arrivals:
                arrivals.append(pltpu.make_async_remote_copy(
                    src_ref=src.at[me] if sliced else src, dst_ref=dst.at[_slot(*peer)], **sems))
    return local, sends, arrivals


def _gather_phases(srcs, dsts, send_sems, recv_sems, local_sems):
    x, y, c = lax.axis_index("x"), lax.axis_index("y"), lax.axis_index("c")
    me, sibling = (x, y, c), (x, y, 1 - c)
    chips = [(1 - x, y), (x, 1 - y), (1 - x, 1 - y)]
    n = len(srcs)

    def copy(w, k, block, to, src=None):
        dst = dsts[w].at[_slot(*block)]
        return pltpu.make_async_remote_copy(
            src_ref=dst if src is None else src, dst_ref=dst, send_sem=send_sems.at[w, k],
            recv_sem=recv_sems.at[w, k], device_id=to, device_id_type=MESH)

    def first_sends(w):
        return [copy(w, 0, me, sibling, src=srcs[w])] + [copy(w, 1 + j, me, (*chip, c), src=srcs[w])
                                                         for j, chip in enumerate(chips)]

    def start():
        for w in range(n):
            pltpu.make_async_copy(srcs[w], dsts[w].at[_slot(*me)], local_sems.at[w]).start()
            for cp in first_sends(w):
                cp.start()

    def forward():
        for w in range(n):
            for j, chip in enumerate(chips):
                copy(w, 1 + j, (*chip, c), me).wait_recv()
                copy(w, 4 + j, (*chip, c), sibling).start()

    def finish():
        for w in range(n):
            copy(w, 0, sibling, me).wait_recv()
            for j, chip in enumerate(chips):
                copy(w, 4 + j, (*chip, 1 - c), me).wait_recv()
        for w in range(n):
            for cp in first_sends(w) + [copy(w, 4 + j, (*chip, c), sibling) for j, chip in enumerate(chips)]:
                cp.wait_send()
            pltpu.make_async_copy(srcs[w], dsts[w].at[_slot(*me)], local_sems.at[w]).wait()

    return start, forward, finish


def _call(body, *, name, grid, in_specs, out_specs, out_shape, sem, args, scratch_shapes=(), carry=None):
    if carry is None:
        return pl.pallas_call(body, name=name, grid=grid, in_specs=in_specs, out_specs=out_specs, out_shape=out_shape,
                              scratch_shapes=list(scratch_shapes), compiler_params=_params(sem))(*args)
    n_in, n_out, n_sc, n_c = len(in_specs), len(out_specs), len(scratch_shapes), len(carry)
    gather = all(a.ndim == 2 for a in carry)
    assert gather or all(a.ndim == 3 and a.shape[0] == N_DEV for a in carry)
    recv_shape = [jax.ShapeDtypeStruct((N_DEV,) + a.shape if gather else a.shape, a.dtype) for a in carry]
    n_steps = math.prod(grid)

    def wrapped(*refs):
        ins, c_in = refs[:n_in], refs[n_in:n_in + n_c]
        outs, c_out = refs[n_in + n_c:n_in + n_c + n_out], refs[n_in + n_c + n_out:n_in + 2 * n_c + n_out]
        scr = refs[n_in + 2 * n_c + n_out:n_in + 2 * n_c + n_out + n_sc]
        sems = refs[-3:]
        step = functools.reduce(lambda acc, d: acc * grid[d] + pl.program_id(d), range(len(grid)), 0)
        if gather:
            start, forward, finish = _gather_phases(c_in, c_out, *sems)
        else:
            def start():
                local, sends, _ = _direct_copies(c_in, c_out, *sems, with_arrivals=False)
                for cp in local + sends:
                    cp.start()

            def finish():
                local, sends, arrivals = _direct_copies(c_in, c_out, *sems)
                for cp in arrivals:
                    cp.wait_recv()
                for cp in sends:
                    cp.wait_send()
                for cp in local:
                    cp.wait()

        pl.when(step == 0)(start)
        if gather:
            pl.when(step == min((3 * n_steps) // 4, n_steps - 1))(forward)
        body(*ins, *outs, *scr)
        pl.when(step == n_steps - 1)(finish)

    any_spec = pl.BlockSpec(memory_space=pl.ANY)
    res = pl.pallas_call(
        wrapped, name=name, grid=grid, in_specs=list(in_specs) + [any_spec] * n_c,
        out_specs=list(out_specs) + [any_spec] * n_c, out_shape=list(out_shape) + recv_shape,
        scratch_shapes=list(scratch_shapes) + [pltpu.SemaphoreType.DMA((n_c, N_DEV - 1)),
                                               pltpu.SemaphoreType.DMA((n_c, N_DEV - 1)), pltpu.SemaphoreType.DMA((n_c,))],
        compiler_params=_params(("arbitrary",) * len(grid)),
    )(*args, *carry)
    return res


def _matmul(a, b, *, mode, out_dtype, name, add=None, carry=None):
    if mode == "nn":
        (M, K), N = a.shape, b.shape[1]
    elif mode == "nt":
        (M, K), N = a.shape, b.shape[0]
    else:
        (K, M), N = a.shape, b.shape[1]
    assert a.dtype == BF16 and b.dtype == BF16
    has_add = add is not None
    tn = _tile(N, 512)
    for tm in ((_tile(M, 512),) if mode == "tn" else (_tile(M, 2048), _tile(M, 1024))):
        fixed = 2 * tm * tn * (jnp.dtype(out_dtype).itemsize + (4 if has_add else 0)) + tm * tn * 4
        per_k = 2 * 2 * (tm + tn) + (2 * tm if mode == "tn" else 0)
        tk = _tile(K, max(LANES, (MATMUL_VMEM - fixed) // per_k))
        if tk == K:
            break
    nk = K // tk
    dims = _NT if mode == "nt" else _NN
    if mode == "tn":
        a_spec = pl.BlockSpec((tk, tm), lambda i, j, k: (jnp.where(j == 0, k, 0), i))
    else:
        a_spec = pl.BlockSpec((tm, tk), lambda i, j, k: (i, k))
    b_spec = pl.BlockSpec((tn, tk), lambda i, j, k: (j, k)) if mode == "nt" else pl.BlockSpec((tk, tn), lambda i, j, k: (k, j))
    o_spec = pl.BlockSpec((tm, tn), lambda i, j, k: (i, j))

    def body(*refs):
        a_ref, b_ref = refs[:2]
        add_ref = refs[2] if has_add else None
        o_ref = refs[2 + has_add]
        scratch = list(refs[3 + has_add:])
        at = scratch.pop(0) if mode == "tn" else None
        acc = scratch.pop(0) if nk > 1 else None
        j, k = pl.program_id(1), pl.program_id(2)
        if mode == "tn":
            @pl.when(j == 0)
            def _():
                at[k] = a_ref[...].T

            part = _dot(at[k], b_ref[...], dims)
        else:
            part = _dot(a_ref[...], b_ref[...], dims)

        def finish(r):
            if has_add:
                r = r + add_ref[...]
            o_ref[...] = r.astype(out_dtype)

        if nk == 1:
            finish(part)
        else:
            @pl.when(k == 0)
            def _():
                acc[...] = part

            @pl.when(jnp.logical_and(k > 0, k < nk - 1))
            def _():
                acc[...] += part

            @pl.when(k == nk - 1)
            def _():
                finish(acc[...] + part)

    scratch = ([pltpu.VMEM((nk, tm, tk), BF16)] if mode == "tn" else []) + ([pltpu.VMEM((tm, tn), F32)] if nk > 1 else [])
    res = _call(
        body, name=name, grid=(M // tm, N // tn, nk),
        in_specs=[a_spec, b_spec] + ([o_spec] if has_add else []),
        out_specs=[o_spec], out_shape=[jax.ShapeDtypeStruct((M, N), out_dtype)],
        scratch_shapes=scratch, sem=("parallel", "arbitrary", "arbitrary"),
        args=(a, b, add) if has_add else (a, b), carry=carry)
    return res[0] if carry is None else (res[0], res[1:])


def _rms_fwd(x, gain, name, carry=None):
    T, D = x.shape
    tr = 256

    def body(x_ref, g_ref, h_ref):
        xv = x_ref[...]
        rstd = lax.rsqrt(jnp.mean(xv * xv, axis=1, keepdims=True) + EPS)
        h_ref[...] = (xv * rstd * g_ref[...]).astype(BF16)

    res = _call(
        body, name=name, grid=(T // tr,),
        in_specs=[pl.BlockSpec((tr, D), lambda i: (i, 0)), pl.BlockSpec((1, D), lambda i: (0, 0))],
        out_specs=[pl.BlockSpec((tr, D), lambda i: (i, 0))],
        out_shape=[jax.ShapeDtypeStruct((T, D), BF16)], sem=("parallel",), args=(x, gain), carry=carry)
    return res[0] if carry is None else (res[0], res[1:])


def _rms_bwd(x, gain, dh, dres, name):
    T, D = x.shape
    tr = 256

    def body(x_ref, g_ref, dh_ref, dr_ref, dx_ref, dxb_ref, dg_ref):
        @pl.when(pl.program_id(0) == 0)
        def _():
            dg_ref[...] = jnp.zeros_like(dg_ref)

        xv = x_ref[...]
        rstd = lax.rsqrt(jnp.mean(xv * xv, axis=1, keepdims=True) + EPS)
        xhat = xv * rstd
        dy = dh_ref[...].astype(F32)
        dg_ref[...] += jnp.sum(dy * xhat, axis=0, keepdims=True)
        dxh = dy * g_ref[...]
        dx = dr_ref[...] + rstd * (dxh - xhat * jnp.mean(dxh * xhat, axis=1, keepdims=True))
        dx_ref[...] = dx
        dxb_ref[...] = dx.astype(BF16)

    row = pl.BlockSpec((tr, D), lambda i: (i, 0))
    vec = pl.BlockSpec((1, D), lambda i: (0, 0))
    return pl.pallas_call(
        body, name=name, grid=(T // tr,), in_specs=[row, vec, row, row], out_specs=[row, row, vec],
        out_shape=[jax.ShapeDtypeStruct((T, D), F32), jax.ShapeDtypeStruct((T, D), BF16),
                   jax.ShapeDtypeStruct((1, D), F32)],
        compiler_params=_params(("arbitrary",)),
    )(x, gain, dh, dres)


def _loss_head(y, target):
    T, D = y.shape
    tr = 256

    def body(y_ref, t_ref, s_ref, d_ref, db_ref):
        @pl.when(pl.program_id(0) == 0)
        def _():
            s_ref[...] = jnp.zeros_like(s_ref)

        e = y_ref[...] - t_ref[...]
        s_ref[...] += jnp.sum(jnp.sum(e * e, axis=1, keepdims=True), axis=0, keepdims=True)
        d = e * (1.0 / D)
        d_ref[...] = d
        db_ref[...] = d.astype(BF16)

    row = pl.BlockSpec((tr, D), lambda i: (i, 0))
    return pl.pallas_call(
        body, name="loss_head", grid=(T // tr,), in_specs=[row, row],
        out_specs=[pl.BlockSpec((1, 1), lambda i: (0, 0)), row, row],
        out_shape=[jax.ShapeDtypeStruct((1, 1), F32), jax.ShapeDtypeStruct((T, D), F32),
                   jax.ShapeDtypeStruct((T, D), BF16)],
        compiler_params=_params(("arbitrary",)),
    )(y, target)


def _ffn_up(h, wg, wu, name, carry=None):
    (T, D), Fd = h.shape, wg.shape[1]
    tm, tn = _tile(T, 1024), _tile(Fd, 512)

    def body(h_ref, wg_ref, wu_ref, fg_ref, fu_ref, a_ref):
        hv = h_ref[...]
        g = _dot(hv, wg_ref[...])
        s = jax.nn.sigmoid(g)
        silu = g * s
        fu_ref[...] = silu.astype(BF16)
        u = _dot(hv, wu_ref[...])
        fg_ref[...] = (u * (s + silu * (1.0 - s))).astype(BF16)
        a_ref[...] = (silu * u).astype(BF16)

    w_spec = pl.BlockSpec((D, tn), lambda i, j: (0, j))
    o_spec = pl.BlockSpec((tm, tn), lambda i, j: (i, j))
    res = _call(
        body, name=name, grid=(T // tm, Fd // tn),
        in_specs=[pl.BlockSpec((tm, D), lambda i, j: (i, 0)), w_spec, w_spec],
        out_specs=[o_spec, o_spec, o_spec],
        out_shape=[jax.ShapeDtypeStruct((T, Fd), BF16)] * 3,
        sem=("parallel", "arbitrary"), args=(h, wg, wu), carry=carry)
    return res if carry is None else (*res[:3], res[3:])


def _ffn_dact(dy, wd, fgate, fup, name):
    (T, D), Fd = dy.shape, wd.shape[0]
    tm, tn = _tile(T, 2048), _tile(Fd, 512)

    def body(dy_ref, wd_ref, fg_ref, fu_ref, dg_ref, du_ref):
        da = _dot(dy_ref[...], wd_ref[...], _NT)
        du_ref[...] = (da * fu_ref[...].astype(F32)).astype(BF16)
        dg_ref[...] = (da * fg_ref[...].astype(F32)).astype(BF16)

    o_spec = pl.BlockSpec((tm, tn), lambda i, j: (i, j))
    return pl.pallas_call(
        body, name=name, grid=(T // tm, Fd // tn),
        in_specs=[pl.BlockSpec((tm, D), lambda i, j: (i, 0)), pl.BlockSpec((tn, D), lambda i, j: (j, 0)),
                  o_spec, o_spec],
        out_specs=[o_spec, o_spec],
        out_shape=[jax.ShapeDtypeStruct((T, Fd), BF16), jax.ShapeDtypeStruct((T, Fd), BF16)],
        compiler_params=_params(("parallel", "arbitrary")),
    )(dy, wd, fgate, fup)


def _is_a(shape):
    return lax.broadcasted_iota(jnp.int32, shape, 1) % LANES < HEAD


def _split2(v):
    hi = v.astype(BF16)
    return hi, (v - hi.astype(F32)).astype(BF16)


def _split3(v):
    hi = v.astype(BF16)
    r = v - hi.astype(F32)
    mid = r.astype(BF16)
    return hi, mid, (r - mid.astype(F32)).astype(BF16)


def _dot_split(v, m, pieces, dims=_NN):
    parts = _split3(v) if pieces == 3 else _split2(v)
    out = _dot(parts[0], m, dims)
    for p in parts[1:]:
        out = out + _dot(p, m, dims)
    return out


def _head_blockdiag():
    r = lax.broadcasted_iota(jnp.int32, (LANES, LANES), 0) // HEAD
    c = lax.broadcasted_iota(jnp.int32, (LANES, LANES), 1) // HEAD
    return (r == c).astype(BF16)


def _swap_half(v):
    lane = lax.broadcasted_iota(jnp.int32, v.shape, 1)
    return jnp.where(lane % HEAD < HEAD // 2, pltpu.roll(v, LANES - HEAD // 2, axis=1), pltpu.roll(v, HEAD // 2, axis=1))


def _attn_prep_fwd(qkv, cos, sin, gq, gk, D, HKV):
    T, QW = qkv.shape
    tr = 256
    nq, nk = D // LANES, HKV // 2
    KW = HKV * LANES

    def body(x_ref, cos_ref, sin_ref, gq_ref, gk_ref, q_ref, k_ref, v_ref):
        bd = _head_blockdiag()
        cs, sn = cos_ref[...], sin_ref[...]
        isa = _is_a((tr, LANES))

        def normrope(xv, g):
            ms = _dot_split(xv * xv, bd, 2) * (1.0 / HEAD)
            xn = xv * lax.rsqrt(ms + EPS) * g
            return xn * cs + _swap_half(xn) * sn

        def dup(v):
            r = pltpu.roll(v, HEAD, axis=1)
            return jnp.where(isa, v, r), jnp.where(isa, r, v)

        for s in range(nq):
            sl = slice(s * LANES, (s + 1) * LANES)
            q_ref[:, sl] = normrope(x_ref[:, sl], gq_ref[...]).astype(BF16)
        for s in range(nk):
            ka, kb = dup(normrope(x_ref[:, D + s * LANES:D + (s + 1) * LANES], gk_ref[...]))
            k_ref[:, 2 * s * LANES:(2 * s + 1) * LANES] = ka.astype(BF16)
            k_ref[:, (2 * s + 1) * LANES:(2 * s + 2) * LANES] = kb.astype(BF16)
            va, vb = dup(x_ref[:, D + (nk + s) * LANES:D + (nk + s + 1) * LANES])
            v_ref[:, 2 * s * LANES:(2 * s + 1) * LANES] = va.astype(BF16)
            v_ref[:, (2 * s + 1) * LANES:(2 * s + 2) * LANES] = vb.astype(BF16)

    tab = pl.BlockSpec((tr, LANES), lambda i: (i, 0))
    vec = pl.BlockSpec((1, LANES), lambda i: (0, 0))
    return pl.pallas_call(
        body, name="attn_prep_fwd", grid=(T // tr,),
        in_specs=[pl.BlockSpec((tr, QW), lambda i: (i, 0)), tab, tab, vec, vec],
        out_specs=[pl.BlockSpec((tr, D), lambda i: (i, 0)), pl.BlockSpec((tr, KW), lambda i: (i, 0)),
                   pl.BlockSpec((tr, KW), lambda i: (i, 0))],
        out_shape=[jax.ShapeDtypeStruct((T, D), BF16), jax.ShapeDtypeStruct((T, KW), BF16),
                   jax.ShapeDtypeStruct((T, KW), BF16)],
        compiler_params=_params(("parallel",)),
    )(qkv, cos, sin, gq, gk)


def _attn_prep_bwd(qkv, dq, dkd, dvd, cos, sin, gq, gk, D, HKV):
    T, QW = qkv.shape
    tr = 256
    nq, nk = D // LANES, HKV // 2
    KW = HKV * LANES

    def body(x_ref, dq_ref, dk_ref, dv_ref, cos_ref, sin_ref, gq_ref, gk_ref, o_ref, dgq_ref, dgk_ref):
        @pl.when(pl.program_id(0) == 0)
        def _():
            dgq_ref[...] = jnp.zeros_like(dgq_ref)
            dgk_ref[...] = jnp.zeros_like(dgk_ref)

        bd = _head_blockdiag()
        cs, sn = cos_ref[...], sin_ref[...]
        isa = _is_a((tr, LANES))

        def back(xv, dy, g):
            rstd = lax.rsqrt(_dot_split(xv * xv, bd, 2) * (1.0 / HEAD) + EPS)
            xhat = xv * rstd
            dxn = dy * cs + _swap_half(dy * sn)
            dxh = dxn * g
            mean = _dot_split(dxh * xhat, bd, 2) * (1.0 / HEAD)
            return rstd * (dxh - xhat * mean), jnp.sum(dxn * xhat, axis=0, keepdims=True)

        def fold(s):
            a = dk_ref[:, 2 * s * LANES:(2 * s + 1) * LANES]
            b = dk_ref[:, (2 * s + 1) * LANES:(2 * s + 2) * LANES]
            return jnp.where(isa, a + pltpu.roll(a, HEAD, axis=1), b + pltpu.roll(b, HEAD, axis=1))

        def foldv(s):
            a = dv_ref[:, 2 * s * LANES:(2 * s + 1) * LANES]
            b = dv_ref[:, (2 * s + 1) * LANES:(2 * s + 2) * LANES]
            return jnp.where(isa, a + pltpu.roll(a, HEAD, axis=1), b + pltpu.roll(b, HEAD, axis=1))

        dgq = jnp.zeros((1, LANES), F32)
        for s in range(nq):
            sl = slice(s * LANES, (s + 1) * LANES)
            dx, dg = back(x_ref[:, sl], dq_ref[:, sl], gq_ref[...])
            o_ref[:, sl] = dx.astype(BF16)
            dgq = dgq + dg
        dgq_ref[...] += dgq
        dgk = jnp.zeros((1, LANES), F32)
        for s in range(nk):
            sl = slice(D + s * LANES, D + (s + 1) * LANES)
            dx, dg = back(x_ref[:, sl], fold(s), gk_ref[...])
            o_ref[:, sl] = dx.astype(BF16)
            dgk = dgk + dg
            o_ref[:, D + (nk + s) * LANES:D + (nk + s + 1) * LANES] = foldv(s).astype(BF16)
        dgk_ref[...] += dgk

    tab = pl.BlockSpec((tr, LANES), lambda i: (i, 0))
    vec = pl.BlockSpec((1, LANES), lambda i: (0, 0))
    kv = pl.BlockSpec((tr, KW), lambda i: (i, 0))
    return pl.pallas_call(
        body, name="attn_prep_bwd", grid=(T // tr,),
        in_specs=[pl.BlockSpec((tr, QW), lambda i: (i, 0)), pl.BlockSpec((tr, D), lambda i: (i, 0)), kv, kv,
                  tab, tab, vec, vec],
        out_specs=[pl.BlockSpec((tr, QW), lambda i: (i, 0)), vec, vec],
        out_shape=[jax.ShapeDtypeStruct((T, QW), BF16), jax.ShapeDtypeStruct((1, LANES), F32),
                   jax.ShapeDtypeStruct((1, LANES), F32)],
        compiler_params=_params(("arbitrary",)),
    )(qkv, dq, dkd, dvd, cos, sin, gq, gk)


def _attn_probs(qs, kw, sink_ref, first, scale):
    rows = qs.shape[0]
    qi = lax.broadcasted_iota(jnp.int32, (rows, 2 * WINDOW), 0) % WINDOW
    kj = lax.broadcasted_iota(jnp.int32, (rows, 2 * WINDOW), 1)
    valid = (kj > qi) & (kj <= qi + WINDOW)
    if first is not False:
        valid = valid & jnp.logical_or(jnp.logical_not(first), kj >= WINDOW)
    isa = _is_a(kw.shape)
    out = []
    for pos in (0, 1):
        kp = jnp.where(isa if pos == 0 else ~isa, kw, jnp.zeros_like(kw))
        s = jnp.where(valid, _dot(qs, kp, _NT) * scale, -jnp.inf)
        sink = sink_ref[0, pos]
        m = jnp.maximum(jnp.max(s, axis=1, keepdims=True), sink)
        p = jnp.exp(s - m)
        ps = jnp.exp(sink - m)
        inv = 1.0 / (jnp.sum(p, axis=1, keepdims=True) + ps)
        out.append((p * inv, ps * inv, kp))
    return out


def _attn_specs(qb):
    q = pl.BlockSpec((qb * WINDOW, ATT_GW), lambda g, n: (n, g))
    cur = pl.BlockSpec((qb * WINDOW, LANES), lambda g, n: (n, g))
    prev = pl.BlockSpec((WINDOW, LANES), lambda g, n: (jnp.maximum(qb * n - 1, 0), g))
    sink = pl.BlockSpec((1, 2, ATT_GW, 1), lambda g, n: (g, 0, 0, 0))
    return q, cur, prev, sink


def _stack(ref, s):
    rows = slice(s * WINDOW, (s + 1) * WINDOW)
    return jnp.concatenate([ref[rows, i * LANES:(i + 1) * LANES] for i in range(ATT_GW // LANES)], axis=0)


def _attn_fwd(q, kd, vd, sinkcol, HKV, carry=None):
    T, D = q.shape
    nb = T // WINDOW
    qb = math.gcd(nb, ATT_STEP_BLOCKS)
    scale = HEAD ** -0.5

    def body(q_ref, kp_ref, kc_ref, vp_ref, vc_ref, sink_ref, o_ref):
        n = pl.program_id(1)
        kall = jnp.concatenate([kp_ref[...], kc_ref[...]], axis=0)
        vall = jnp.concatenate([vp_ref[...], vc_ref[...]], axis=0)
        isa = _is_a((2 * WINDOW, LANES))
        for s in range(qb):
            win = slice(s * WINDOW, (s + 2) * WINDOW)
            kw, vw = kall[win], vall[win]
            o = jnp.zeros((ATT_GW, LANES), F32)
            first = (n == 0) if s == 0 else False
            for pos, (probs, _, _) in enumerate(_attn_probs(_stack(q_ref, s), kw, sink_ref, first, scale)):
                vp = jnp.where(isa if pos == 0 else ~isa, vw, jnp.zeros_like(vw))
                o = o + _dot(probs.astype(BF16), vp)
            for i in range(ATT_GW // LANES):
                o_ref[s * WINDOW:(s + 1) * WINDOW, i * LANES:(i + 1) * LANES] = o[i * WINDOW:(i + 1) * WINDOW].astype(BF16)

    qs_, cur, prev, sink = _attn_specs(qb)
    res = _call(
        body, name="attn_fwd", grid=(HKV, nb // qb), in_specs=[qs_, prev, cur, prev, cur, sink], out_specs=[qs_],
        out_shape=[jax.ShapeDtypeStruct((T, D), BF16)], sem=("parallel", "parallel"),
        args=(q, kd, kd, vd, vd, sinkcol), carry=carry)
    return res[0] if carry is None else (res[0], res[1:])


def _attn_bwd(q, kd, vd, o, do, sinkcol, HKV, carry=None):
    T, D = q.shape
    nb = T // WINDOW
    qb = math.gcd(nb, ATT_STEP_BLOCKS)
    scale = HEAD ** -0.5
    KW = HKV * LANES

    def body(q_ref, kp_ref, kc_ref, vp_ref, vc_ref, o_ref, do_ref, sink_ref, dq_ref, dk_ref, dv_ref, ds_ref):
        n = pl.program_id(1)

        @pl.when(n == 0)
        def _():
            dk_ref[...] = jnp.zeros_like(dk_ref)
            dv_ref[...] = jnp.zeros_like(dv_ref)
            ds_ref[...] = jnp.zeros_like(ds_ref)

        kall = jnp.concatenate([kp_ref[...], kc_ref[...]], axis=0)
        vall = jnp.concatenate([vp_ref[...], vc_ref[...]], axis=0)
        isa_w = _is_a((2 * WINDOW, LANES))
        isa_q = _is_a((ATT_GW, LANES))
        for s in range(qb):
            win = slice(s * WINDOW, (s + 2) * WINDOW)
            kw, vw = kall[win], vall[win]
            qs = _stack(q_ref, s)
            dos = _stack(do_ref, s)
            dd = dos * _stack(o_ref, s).astype(F32)
            dob = dos.astype(BF16)
            dqs = jnp.zeros((ATT_GW, LANES), F32)
            dkw, dvw, dsk = [], [], []
            first = (n == 0) if s == 0 else False
            for pos, (probs, psink, kp) in enumerate(_attn_probs(qs, kw, sink_ref, first, scale)):
                sel_w = isa_w if pos == 0 else ~isa_w
                sel_q = isa_q if pos == 0 else ~isa_q
                delta = jnp.sum(jnp.where(sel_q, dd, 0.0), axis=1, keepdims=True)
                vp = jnp.where(sel_w, vw, jnp.zeros_like(vw))
                dp = _dot(dob, vp, _NT)
                dsb = (probs * (dp - delta) * scale).astype(BF16)
                dqs = dqs + _dot(dsb, kp)
                dkw.append(_dot(dsb, qs, _TN))
                dvw.append(_dot(probs.astype(BF16), dob, _TN))
                dsk.append(-psink * delta)
            dkw = jnp.where(isa_w, dkw[0], dkw[1])
            dvw = jnp.where(isa_w, dvw[0], dvw[1])

            def add_window(dkw=dkw, dvw=dvw, s=s):
                start = pl.multiple_of((qb * n + s - 1) * WINDOW, WINDOW)
                dk_ref[pl.ds(start, 2 * WINDOW), :] += dkw
                dv_ref[pl.ds(start, 2 * WINDOW), :] += dvw

            if s == 0:
                @pl.when(n == 0)
                def _(dkw=dkw, dvw=dvw):
                    dk_ref[0:WINDOW, :] += dkw[WINDOW:]
                    dv_ref[0:WINDOW, :] += dvw[WINDOW:]

                pl.when(n > 0)(add_window)
            else:
                add_window()

            rows = []
            for i in range(ATT_GW // LANES):
                dq_ref[s * WINDOW:(s + 1) * WINDOW, i * LANES:(i + 1) * LANES] = dqs[i * WINDOW:(i + 1) * WINDOW]
                for pos in (0, 1):
                    t = jnp.sum(dsk[pos][i * WINDOW:(i + 1) * WINDOW], axis=0, keepdims=True)
                    rows.append(jnp.broadcast_to(t, (1, LANES)))
            ds_ref[0] += jnp.concatenate(rows, axis=0)

    qs_, cur, prev, sink = _attn_specs(qb)
    dqo = pl.BlockSpec((qb * WINDOW, ATT_GW), lambda g, n: (n, g))
    dkv = pl.BlockSpec((T, LANES), lambda g, n: (0, g))
    res = _call(
        body, name="attn_bwd", grid=(HKV, nb // qb),
        in_specs=[qs_, prev, cur, prev, cur, qs_, dqo, sink],
        out_specs=[dqo, dkv, dkv, pl.BlockSpec((1, ATT_GROUP, LANES), lambda g, n: (g, 0, 0))],
        out_shape=[jax.ShapeDtypeStruct((T, D), F32), jax.ShapeDtypeStruct((T, KW), F32),
                   jax.ShapeDtypeStruct((T, KW), F32), jax.ShapeDtypeStruct((HKV, ATT_GROUP, LANES), F32)],
        sem=("parallel", "arbitrary"), args=(q, kd, kd, vd, vd, o, do, sinkcol), carry=carry)
    return res if carry is None else (*res[:4], res[4:])


def _conv_fwd(zx, w, b, DI, CD):
    T = zx.shape[0]
    cw, tc = _tile(math.gcd(DI, CD), 512), math.gcd(T, CONV_ROWS)
    off = DI // cw

    def body(cur_ref, halo_ref, w_ref, b_ref, o_ref):
        i = pl.program_id(1)
        halo = jnp.where(i > 0, halo_ref[...], 0.0)
        ext = jnp.concatenate([halo, cur_ref[...]], axis=0)
        acc = b_ref[...] + w_ref[SSM_CONV - 1:SSM_CONV, :] * ext[HALO:]
        for k in range(SSM_CONV - 1):
            acc = acc + w_ref[k:k + 1, :] * pltpu.roll(ext, SSM_CONV - 1 - k, axis=0)[HALO:]
        o_ref[...] = acc * jax.nn.sigmoid(acc)

    return pl.pallas_call(
        body, name="ssm_conv_fwd", grid=(CD // cw, T // tc),
        in_specs=[pl.BlockSpec((tc, cw), lambda j, i: (i, off + j)),
                  pl.BlockSpec((HALO, cw), lambda j, i: (jnp.maximum(i * (tc // HALO) - 1, 0), off + j)),
                  pl.BlockSpec((SSM_CONV, cw), lambda j, i: (0, j)), pl.BlockSpec((1, cw), lambda j, i: (0, j))],
        out_specs=pl.BlockSpec((tc, cw), lambda j, i: (i, j)),
        out_shape=jax.ShapeDtypeStruct((T, CD), F32),
        compiler_params=_params(("parallel", "parallel")),
    )(zx, zx, w, b)


def _conv_bwd(zx, dparts, dzx, w, b, DI, CD):
    T = zx.shape[0]
    cw, tc = _tile(math.gcd(DI, CD), 512), math.gcd(T, CONV_ROWS)
    off = DI // cw
    nt = T // tc
    hb = tc // HALO
    ends = [0]
    for p in dparts:
        assert p.shape[1] % cw == 0
        ends.append(ends[-1] + p.shape[1] // cw)
    assert ends[-1] == CD // cw
    n_p = len(dparts)

    def body(*refs):
        cur_ref, prev_ref, next_ref = refs[:3]
        d_refs, dn_refs = refs[3:3 + n_p], refs[3 + n_p:3 + 2 * n_p]
        w_ref, b_ref, _, o_ref, dw_ref, db_ref = refs[3 + 2 * n_p:]
        j, i = pl.program_id(0), pl.program_id(1)

        @pl.when(i == 0)
        def _():
            dw_ref[...] = jnp.zeros_like(dw_ref)
            db_ref[...] = jnp.zeros_like(db_ref)

        def pick(prefs):
            v = prefs[n_p - 1][...]
            for p in range(n_p - 2, -1, -1):
                v = jnp.where(j < ends[p + 1], prefs[p][...], v)
            return v

        prev = jnp.where(i > 0, prev_ref[...], 0.0)
        ext = jnp.concatenate([prev, cur_ref[...], next_ref[...]], axis=0)
        u = b_ref[...] + w_ref[SSM_CONV - 1:SSM_CONV, :] * ext
        for k in range(SSM_CONV - 1):
            u = u + w_ref[k:k + 1, :] * pltpu.roll(ext, SSM_CONV - 1 - k, axis=0)
        u = u[HALO:]
        dnext = jnp.where(i < nt - 1, pick(dn_refs), 0.0)
        dxe = jnp.concatenate([pick(d_refs), dnext], axis=0)
        sg = jax.nn.sigmoid(u)
        du = dxe * sg * (1.0 + u * (1.0 - sg))
        n_e = tc + HALO
        dx = w_ref[SSM_CONV - 1:SSM_CONV, :] * du
        for k in range(SSM_CONV - 1):
            dx = dx + w_ref[k:k + 1, :] * pltpu.roll(du, n_e - (SSM_CONV - 1 - k), axis=0)
        o_ref[...] = dx[:tc].astype(BF16)
        duc = du[:tc]
        db_ref[...] += jnp.sum(duc, axis=0, keepdims=True)
        xs = ext[:n_e]
        dws = []
        for k in range(SSM_CONV):
            sh = xs if k == SSM_CONV - 1 else pltpu.roll(xs, SSM_CONV - 1 - k, axis=0)
            dws.append(jnp.sum(duc * sh[HALO:], axis=0, keepdims=True))
        dw_ref[...] += jnp.concatenate(dws, axis=0)

    def part_specs(p):
        lo, n = ends[p], ends[p + 1] - ends[p]

        def inside(j):
            return jnp.logical_and(j >= lo, j < lo + n)

        col = lambda j: jnp.clip(j - lo, 0, n - 1)
        return (pl.BlockSpec((tc, cw), lambda j, i: (jnp.where(inside(j), i, 0), col(j))),
                pl.BlockSpec((HALO, cw), lambda j, i: (jnp.where(inside(j), jnp.minimum((i + 1) * hb, nt * hb - 1), 0), col(j))))

    specs = [part_specs(p) for p in range(n_p)]
    return pl.pallas_call(
        body, name="ssm_conv_bwd", grid=(CD // cw, nt),
        in_specs=[pl.BlockSpec((tc, cw), lambda j, i: (i, off + j)),
                  pl.BlockSpec((HALO, cw), lambda j, i: (jnp.maximum(i * hb - 1, 0), off + j)),
                  pl.BlockSpec((HALO, cw), lambda j, i: (jnp.minimum((i + 1) * hb, nt * hb - 1), off + j))]
        + [s[0] for s in specs] + [s[1] for s in specs]
        + [pl.BlockSpec((SSM_CONV, cw), lambda j, i: (0, j)), pl.BlockSpec((1, cw), lambda j, i: (0, j)),
           pl.BlockSpec(memory_space=pl.ANY)],
        out_specs=[pl.BlockSpec((tc, cw), lambda j, i: (i, off + j)), pl.BlockSpec((SSM_CONV, cw), lambda j, i: (0, j)),
                   pl.BlockSpec((1, cw), lambda j, i: (0, j))],
        out_shape=[jax.ShapeDtypeStruct(dzx.shape, BF16), jax.ShapeDtypeStruct((SSM_CONV, CD), F32),
                   jax.ShapeDtypeStruct((1, CD), F32)],
        input_output_aliases={5 + 2 * n_p: 0},
        compiler_params=_params(("parallel", "arbitrary")),
    )(zx, zx, zx, *dparts, *dparts, w, b, dzx)


def _tri_dot(v, upper):
    L = v.shape[0]
    r = lax.broadcasted_iota(jnp.int32, (L, L), 0)
    c = lax.broadcasted_iota(jnp.int32, (L, L), 1)
    tri = ((r <= c) if upper else (r >= c)).astype(BF16)
    p = _split3(v)
    return _dot(tri, p[0]) + _dot(tri, p[1]) + _dot(tri, p[2])


def _ssd_time2(dtraw_ref, bias_ref, alog_ref, sel):
    dt = jax.nn.softplus(dtraw_ref[...] + bias_ref[...])
    acum = _tri_dot(dt * (-jnp.exp(alog_ref[...])), False)
    return dt, _dot_split(dt, sel, 3), _dot_split(acum, sel, 3)


def _decay(acs, acs_t, pos):
    L = acs.shape[0]
    r = lax.broadcasted_iota(jnp.int32, (L, L), 0)
    c = lax.broadcasted_iota(jnp.int32, (L, L), 1)
    col = acs[:, HEAD * pos:HEAD * pos + 1]
    row = acs_t[HEAD * pos:HEAD * pos + 1, :]
    return jnp.exp(jnp.where(r >= c, col - row, -jnp.inf))


def _ssd_specs(G, GW, DI, ZW):
    L = SSM_CHUNK
    grp = lambda f: pl.BlockSpec((L, GW), lambda g, c: (f(c), g))
    return dict(
        grp=grp,
        bmat=lambda f: pl.BlockSpec((L, SSM_STATE), lambda g, c: (f(c), DI // SSM_STATE + g)),
        cmat=lambda f: pl.BlockSpec((L, SSM_STATE), lambda g, c: (f(c), DI // SSM_STATE + G + g)),
        dtraw=lambda f: pl.BlockSpec((L, LANES), lambda g, c: (f(c), (2 * DI + 2 * G * SSM_STATE) // LANES)),
        vec=pl.BlockSpec((1, LANES), lambda g, c: (0, 0)),
        gvec=pl.BlockSpec((1, GW), lambda g, c: (0, g)),
        sel=pl.BlockSpec((1, LANES, GW), lambda g, c: (g, 0, 0)),
    )


def _ssd_fwd(zx, xc, bias, alog, sel, dskip, ng, DI, carry=None):
    T, ZW = zx.shape
    G, L = SSM_GROUPS, SSM_CHUNK
    GW = DI // G
    NS = GW // LANES
    nc = T // L
    sp = _ssd_specs(G, GW, DI, ZW)
    ident = lambda c: c

    def body(x_ref, b_ref, c_ref, z_ref, dtraw_ref, bias_ref, alog_ref, sel_ref, d_ref, ng_ref,
             y_ref, yo_ref, st_ref, state):
        c = pl.program_id(1)

        @pl.when(c == 0)
        def _():
            state[...] = jnp.zeros_like(state)

        x = x_ref[...]
        bb, cb_ = b_ref[...].astype(BF16), c_ref[...].astype(BF16)
        cbm = _dot(cb_, bb, _NT)
        _, dtx, acx = _ssd_time2(dtraw_ref, bias_ref, alog_ref, sel_ref[0])
        xdt = x * dtx
        ex = jnp.exp(acx)
        last = acx[L - 1:L, :]
        te = jnp.exp(last - acx)
        dlast = jnp.exp(last)
        isa = _is_a((L, LANES))
        for i in range(NS):
            sl = slice(i * LANES, (i + 1) * LANES)
            acs = acx[:, sl]
            acs_t = acs.T
            xs = xdt[:, sl]
            y = jnp.zeros((L, LANES), F32)
            for pos in (0, 1):
                m = (cbm * _decay(acs, acs_t, pos)).astype(BF16)
                y = y + _dot(m, jnp.where(isa if pos == 0 else ~isa, xs, 0.0).astype(BF16))
            st = state[i]
            st_ref[0, i] = st
            y = y + _dot(cb_, st.astype(BF16)) * ex[:, sl]
            state[i] = st * dlast[:, sl] + _dot(bb, (xs * te[:, sl]).astype(BF16), _TN)
            y_ref[:, sl] = y + d_ref[:, sl] * x[:, sl]
        z = z_ref[...]
        gated = y_ref[...] * (z * jax.nn.sigmoid(z))
        rstd = lax.rsqrt(jnp.mean(gated * gated, axis=1, keepdims=True) + EPS)
        yo_ref[...] = (gated * rstd * ng_ref[...]).astype(BF16)

    res = _call(
        body, name="ssd_fwd", grid=(G, nc),
        in_specs=[sp["grp"](ident), sp["bmat"](ident), sp["cmat"](ident), sp["grp"](ident), sp["dtraw"](ident),
                  sp["vec"], sp["vec"], sp["sel"], sp["gvec"], sp["gvec"]],
        out_specs=[sp["grp"](ident), sp["grp"](ident),
                   pl.BlockSpec((1, NS, SSM_STATE, LANES), lambda g, c: (c, g, 0, 0))],
        out_shape=[jax.ShapeDtypeStruct((T, DI), F32), jax.ShapeDtypeStruct((T, DI), BF16),
                   jax.ShapeDtypeStruct((nc, G * NS, SSM_STATE, LANES), F32)],
        scratch_shapes=[pltpu.VMEM((NS, SSM_STATE, LANES), F32)],
        sem=("parallel", "arbitrary"), args=(xc, xc, xc, zx, zx, bias, alog, sel, dskip, ng), carry=carry)
    return res if carry is None else (*res[:3], res[3:])


def _ssd_bwd(zx, xc, yssd, dyo, states, bias, alog, sel, dskip, ng, DI, carry=None):
    T, ZW = zx.shape
    G, L = SSM_GROUPS, SSM_CHUNK
    GW = DI // G
    NS = GW // LANES
    nc = T // L
    sp = _ssd_specs(G, GW, DI, ZW)
    rev = lambda c: nc - 1 - c

    def body(x_ref, b_ref, c_ref, z_ref, dtraw_ref, y_ref, dyo_ref, st_ref, bias_ref, alog_ref, sel_ref,
             d_ref, ng_ref, dz_ref, dx_ref, db_ref, dc_ref, ddt_ref, dac_ref, dd_ref, dng_ref, dstate):
        c = pl.program_id(1)

        @pl.when(c == 0)
        def _():
            dstate[...] = jnp.zeros_like(dstate)
            dd_ref[...] = jnp.zeros_like(dd_ref)
            dng_ref[...] = jnp.zeros_like(dng_ref)

        z, ys, dyo = z_ref[...], y_ref[...], dyo_ref[...]
        sg = jax.nn.sigmoid(z)
        sz = z * sg
        gated = ys * sz
        rstd = lax.rsqrt(jnp.mean(gated * gated, axis=1, keepdims=True) + EPS)
        yn = gated * rstd
        dng_ref[0] += jnp.sum(dyo * yn, axis=0, keepdims=True)
        dyn = dyo * ng_ref[...]
        dgated = rstd * (dyn - yn * jnp.mean(dyn * yn, axis=1, keepdims=True))
        g = dgated * sz
        dz_ref[...] = (dgated * ys * sg * (1.0 + z * (1.0 - sg))).astype(BF16)

        x = x_ref[...]
        dsk = d_ref[...]
        dd_ref[0] += jnp.sum(g * x, axis=0, keepdims=True)
        bb, cb_ = b_ref[...].astype(BF16), c_ref[...].astype(BF16)
        cbm = _dot(cb_, bb, _NT)
        _, dtx, acx = _ssd_time2(dtraw_ref, bias_ref, alog_ref, sel_ref[0])
        xdt = x * dtx
        ex = jnp.exp(acx)
        last = acx[L - 1:L, :]
        te = jnp.exp(last - acx)
        dlast = jnp.exp(last)
        isa = _is_a((L, LANES))
        is_last = lax.broadcasted_iota(jnp.int32, (L, LANES), 0) == L - 1
        strict = lax.broadcasted_iota(jnp.int32, (L, L), 0) > lax.broadcasted_iota(jnp.int32, (L, L), 1)
        lane_id = lax.broadcasted_iota(jnp.int32, (1, LANES), 1)
        row_id = lax.broadcasted_iota(jnp.int32, (8, 1), 0)
        lane_head = lax.broadcasted_iota(jnp.int32, (LANES, LANES), 0) // HEAD
        col_id = lax.broadcasted_iota(jnp.int32, (LANES, LANES), 1)
        dcb = jnp.zeros((L, L), F32)
        dcm = jnp.zeros((L, SSM_STATE), F32)
        dbm = jnp.zeros((L, SSM_STATE), F32)
        q_rows = jnp.zeros((L, LANES), F32)
        q_cols = jnp.zeros((8, L), F32)
        for i in range(NS):
            sl = slice(i * LANES, (i + 1) * LANES)
            acs = acx[:, sl]
            acs_t = acs.T
            xs, gs = xdt[:, sl], g[:, sl]
            xsb = xs.astype(BF16)
            dxd = jnp.zeros((L, LANES), F32)
            for pos in (0, 1):
                gp = jnp.where(isa if pos == 0 else ~isa, gs, 0.0).astype(BF16)
                dec = _decay(acs, acs_t, pos)
                dxd = dxd + _dot((cbm * dec).astype(BF16), gp, _TN)
                dmd = _dot(gp, xsb, _NT) * dec
                dcb = dcb + dmd
                q = jnp.where(strict, dmd * cbm, 0.0)
                q_rows = q_rows + jnp.sum(q, axis=1, keepdims=True) * (lane_id == 2 * i + pos).astype(F32)
                q_cols = q_cols + jnp.sum(q, axis=0, keepdims=True) * (row_id == 2 * i + pos).astype(F32)
            st = st_ref[0, i]
            dst = dstate[i]
            stb, dstb = st.astype(BF16), dst.astype(BF16)
            eg = (ex[:, sl] * gs).astype(BF16)
            dcm = dcm + _dot(eg, stb, _NT)
            yoff = _dot(cb_, stb) * ex[:, sl]
            w = xs * te[:, sl]
            wb = w.astype(BF16)
            dw = _dot(bb, dstb)
            dbm = dbm + _dot(wb, dstb, _NT)
            dxt = dxd + dw * te[:, sl]
            dal = dlast[:, sl] * jnp.sum(dst * st, axis=0, keepdims=True) + jnp.sum(dw * w, axis=0, keepdims=True)
            dac_l = gs * yoff - w * dw + jnp.where(is_last, dal, 0.0)
            ddt_l = dxt * x[:, sl]
            dstate[i] = dst * dlast[:, sl] + _dot(cb_, eg, _TN)
            dx_ref[:, sl] = dxt * dtx[:, sl] + dsk[:, sl] * gs
            to_head = (col_id == 2 * i + lane_head).astype(BF16)
            part = _dot_split(ddt_l, to_head, 2)
            parta = _dot_split(dac_l, to_head, 2)
            if i == 0:
                ddt_ref[0] = part
                dac_ref[0] = parta
            else:
                ddt_ref[0] += part
                dac_ref[0] += parta
        dac_ref[0] += q_rows - jnp.concatenate([q_cols, jnp.zeros((LANES - 8, L), F32)], axis=0).T
        dcbb = dcb.astype(BF16)
        dc_ref[...] = dcm + _dot(dcbb, bb)
        db_ref[...] = dbm + _dot(dcbb, cb_, _TN)

    part_spec = pl.BlockSpec((1, L, LANES), lambda g, c: (g, rev(c), 0))
    lane_spec = pl.BlockSpec((1, 1, GW), lambda g, c: (g, 0, 0))
    bc_out = pl.BlockSpec((L, SSM_STATE), lambda g, c: (rev(c), g))
    res = _call(
        body, name="ssd_bwd", grid=(G, nc),
        in_specs=[sp["grp"](rev), sp["bmat"](rev), sp["cmat"](rev), sp["grp"](rev), sp["dtraw"](rev), sp["grp"](rev),
                  sp["grp"](rev), pl.BlockSpec((1, NS, SSM_STATE, LANES), lambda g, c: (rev(c), g, 0, 0)),
                  sp["vec"], sp["vec"], sp["sel"], sp["gvec"], sp["gvec"]],
        out_specs=[sp["grp"](rev), sp["grp"](rev), bc_out, bc_out, part_spec, part_spec, lane_spec, lane_spec],
        out_shape=[jax.ShapeDtypeStruct((T, ZW), BF16), jax.ShapeDtypeStruct((T, DI), F32),
                   jax.ShapeDtypeStruct((T, G * SSM_STATE), F32), jax.ShapeDtypeStruct((T, G * SSM_STATE), F32),
                   jax.ShapeDtypeStruct((G, T, LANES), F32), jax.ShapeDtypeStruct((G, T, LANES), F32),
                   jax.ShapeDtypeStruct((G, 1, GW), F32), jax.ShapeDtypeStruct((G, 1, GW), F32)],
        scratch_shapes=[pltpu.VMEM((NS, SSM_STATE, LANES), F32)], sem=("parallel", "arbitrary"),
        args=(xc, xc, xc, zx, zx, yssd, dyo, states, bias, alog, sel, dskip, ng), carry=carry)
    return res if carry is None else (*res[:8], res[8:])


def _ssd_dt_bwd(zx, ddt_part, dac_part, dzx, bias, alog, DI):
    T, ZW = zx.shape
    G, L = SSM_GROUPS, SSM_CHUNK
    nc = T // L
    heads = DI // HEAD // G
    tail = ZW - 2 * DI - 2 * G * SSM_STATE
    dt_block = (ZW - tail) // LANES

    def body(dtraw_ref, ddt_ref, dac_ref, bias_ref, alog_ref, _, o_ref, dal_ref, dbias_ref):
        @pl.when(pl.program_id(0) == 0)
        def _():
            dal_ref[...] = jnp.zeros_like(dal_ref)
            dbias_ref[...] = jnp.zeros_like(dbias_ref)

        raw = dtraw_ref[...] + bias_ref[...]
        dt = jax.nn.softplus(raw)
        a = -jnp.exp(alog_ref[...])
        dac, ddt = dac_ref[0], ddt_ref[0]
        for gi in range(1, G):
            dac = dac + pltpu.roll(dac_ref[gi], gi * heads, axis=1)
            ddt = ddt + pltpu.roll(ddt_ref[gi], gi * heads, axis=1)
        dda = _tri_dot(dac, True)
        dal_ref[...] += jnp.sum(dda * dt, axis=0, keepdims=True) * a
        draw = (dda * a + ddt) * jax.nn.sigmoid(raw)
        dbias_ref[...] += jnp.sum(draw, axis=0, keepdims=True)
        o_ref[...] = jnp.concatenate([draw.astype(BF16), jnp.zeros((L, tail - LANES), BF16)], axis=1)

    vec = pl.BlockSpec((1, LANES), lambda c: (0, 0))
    part = pl.BlockSpec((G, L, LANES), lambda c: (0, c, 0))
    return pl.pallas_call(
        body, name="ssd_dt_bwd", grid=(nc,),
        in_specs=[pl.BlockSpec((L, LANES), lambda c: (c, dt_block)), part, part, vec, vec,
                  pl.BlockSpec(memory_space=pl.ANY)],
        out_specs=[pl.BlockSpec((L, tail), lambda c: (c, (ZW - tail) // tail)), vec, vec],
        out_shape=[jax.ShapeDtypeStruct((T, ZW), BF16), jax.ShapeDtypeStruct((1, LANES), F32),
                   jax.ShapeDtypeStruct((1, LANES), F32)],
        input_output_aliases={5: 0},
        compiler_params=_params(("arbitrary",)),
    )(zx, ddt_part, dac_part, bias, alog, dzx)


def _exchange(blocks, name):
    n = len(blocks)

    def body(*refs):
        local, sends, arrivals = _direct_copies(refs[:n], refs[n:2 * n], *refs[2 * n:])
        for cp in local + sends:
            cp.start()
        for cp in arrivals:
            cp.wait_recv()
        for cp in sends:
            cp.wait_send()
        for cp in local:
            cp.wait()

    any_spec = pl.BlockSpec(memory_space=pl.ANY)
    return pl.pallas_call(
        body, name=name, in_specs=[any_spec] * n, out_specs=[any_spec] * n,
        out_shape=[jax.ShapeDtypeStruct(b.shape, b.dtype) for b in blocks],
        scratch_shapes=[pltpu.SemaphoreType.DMA((n, 7)), pltpu.SemaphoreType.DMA((n, 7)),
                        pltpu.SemaphoreType.DMA((n,))],
    )(*blocks)


def _adamw(parts, w, m, v, name):
    nl = len(parts)
    R, C = parts[0].shape[1:]
    per_row = C * (N_DEV * nl * parts[0].dtype.itemsize + 7 * 4) * 2
    tr = R
    if R % 8 == 0:
        tr = 8
        for t in (16, 32, 64, 128, 256, 512):
            if R % t == 0 and t * per_row <= 24 * 1024 * 1024:
                tr = t
    nr = R // tr
    c1 = 1.0 - ADAM_B1 ** ADAM_STEP
    c2 = 1.0 - ADAM_B2 ** ADAM_STEP

    def body(*refs):
        p_refs = refs[:nl]
        w_ref, m_ref, v_ref, g_ref, d_ref, nm_ref, nv_ref = refs[nl:]
        for layer in range(nl):
            @pl.when(pl.program_id(0) == layer)
            def _(p_ref=p_refs[layer]):
                g = p_ref[0].astype(F32)
                for k in range(1, N_DEV):
                    g = g + p_ref[k].astype(F32)
                nm = ADAM_B1 * m_ref[...] + (1.0 - ADAM_B1) * g
                nv = ADAM_B2 * v_ref[...] + (1.0 - ADAM_B2) * (g * g)
                g_ref[...] = g
                nm_ref[...] = nm
                nv_ref[...] = nv
                d_ref[...] = -ADAM_LR * ((nm / c1) / (jnp.sqrt(nv / c2) + ADAM_EPS) + ADAM_WD * w_ref[...])

    def part_spec(layer):
        return pl.BlockSpec((N_DEV, tr, C), lambda l, i: (0, jnp.where(l == layer, i, jnp.where(l < layer, 0, nr - 1)), 0))

    blk = pl.BlockSpec((tr, C), lambda l, i: (l * nr + i, 0))
    out = jax.ShapeDtypeStruct((nl * R, C), F32)
    return pl.pallas_call(
        body, name=name, grid=(nl, nr),
        in_specs=[part_spec(layer) for layer in range(nl)] + [blk, blk, blk],
        out_specs=[blk, blk, blk, blk], out_shape=[out, out, out, out],
        compiler_params=_params(("arbitrary", "arbitrary")),
    )(*parts, w, m, v)


def _pad_cols(a, n):
    return jnp.pad(a, ((0, 0), (0, n - a.shape[1])))


def kernel(x, positions, mixer_norm, ffn_norm, attn_w_qkv, attn_q_norm, attn_k_norm, attn_sinks, attn_w_o, ssm_w_in, ssm_conv_w, ssm_conv_b, ssm_dt_bias, ssm_a_log, ssm_d, ssm_norm, ssm_w_out, ffn_w_gate, ffn_w_up, ffn_w_down, loss_target, m_mixer_norm, m_ffn_norm, m_attn_w_qkv, m_attn_q_norm, m_attn_k_norm, m_attn_sinks, m_attn_w_o, m_ssm_w_in, m_ssm_conv_w, m_ssm_conv_b, m_ssm_dt_bias, m_ssm_a_log, m_ssm_d, m_ssm_norm, m_ssm_w_out, m_ffn_w_gate, m_ffn_w_up, m_ffn_w_down, v_mixer_norm, v_ffn_norm, v_attn_w_qkv, v_attn_q_norm, v_attn_k_norm, v_attn_sinks, v_attn_w_o, v_ssm_w_in, v_ssm_conv_w, v_ssm_conv_b, v_ssm_dt_bias, v_ssm_a_log, v_ssm_d, v_ssm_norm, v_ssm_w_out, v_ffn_w_gate, v_ffn_w_up, v_ffn_w_down):
    T, D = x.shape[1], x.shape[2]
    HQ = D // HEAD
    HKV = HQ // ATT_GROUP
    QW = (HQ + 2 * HKV) * HEAD
    DI = 2 * D
    H = DI // HEAD
    G = SSM_GROUPS
    GW = DI // G
    CD = DI + 2 * G * SSM_STATE
    ZW = -(-(DI + CD + LANES) // 512) * 512
    IW = DI + CD + H
    assert T % 512 == 0 and D % 256 == 0 and HKV % 2 == 0 and GW % LANES == 0 and H <= LANES

    weights = dict(mixer_norm=mixer_norm, ffn_norm=ffn_norm, attn_w_qkv=attn_w_qkv, attn_q_norm=attn_q_norm,
                   attn_k_norm=attn_k_norm, attn_sinks=attn_sinks, attn_w_o=attn_w_o, ssm_w_in=ssm_w_in,
                   ssm_conv_w=ssm_conv_w, ssm_conv_b=ssm_conv_b, ssm_dt_bias=ssm_dt_bias, ssm_a_log=ssm_a_log,
                   ssm_d=ssm_d, ssm_norm=ssm_norm, ssm_w_out=ssm_w_out, ffn_w_gate=ffn_w_gate, ffn_w_up=ffn_w_up,
                   ffn_w_down=ffn_w_down)
    mom_m = dict(mixer_norm=m_mixer_norm, ffn_norm=m_ffn_norm, attn_w_qkv=m_attn_w_qkv, attn_q_norm=m_attn_q_norm,
                 attn_k_norm=m_attn_k_norm, attn_sinks=m_attn_sinks, attn_w_o=m_attn_w_o, ssm_w_in=m_ssm_w_in,
                 ssm_conv_w=m_ssm_conv_w, ssm_conv_b=m_ssm_conv_b, ssm_dt_bias=m_ssm_dt_bias, ssm_a_log=m_ssm_a_log,
                 ssm_d=m_ssm_d, ssm_norm=m_ssm_norm, ssm_w_out=m_ssm_w_out, ffn_w_gate=m_ffn_w_gate,
                 ffn_w_up=m_ffn_w_up, ffn_w_down=m_ffn_w_down)
    mom_v = dict(mixer_norm=v_mixer_norm, ffn_norm=v_ffn_norm, attn_w_qkv=v_attn_w_qkv, attn_q_norm=v_attn_q_norm,
                 attn_k_norm=v_attn_k_norm, attn_sinks=v_attn_sinks, attn_w_o=v_attn_w_o, ssm_w_in=v_ssm_w_in,
                 ssm_conv_w=v_ssm_conv_w, ssm_conv_b=v_ssm_conv_b, ssm_dt_bias=v_ssm_dt_bias, ssm_a_log=v_ssm_a_log,
                 ssm_d=v_ssm_d, ssm_norm=v_ssm_norm, ssm_w_out=v_ssm_w_out, ffn_w_gate=v_ffn_w_gate,
                 ffn_w_up=v_ffn_w_up, ffn_w_down=v_ffn_w_down)
    big = ["attn_w_qkv", "attn_w_o", "ssm_w_in", "ssm_w_out", "ffn_w_gate", "ffn_w_up", "ffn_w_down"]

    def flat2(a):
        return a.reshape(-1, a.shape[-1])

    def shard(n, layer=0):
        return weights[n][layer].astype(BF16)

    def from_cols(g):
        return g.transpose(1, 0, 2).reshape(g.shape[1], N_DEV * g.shape[2])

    def from_rows(g):
        return g.reshape(N_DEV * g.shape[1], g.shape[2])

    xs = x[0]
    tgt = loss_target[0]
    inv_freq = ROPE_THETA ** (-jnp.arange(0, HEAD, 2, dtype=F32) / HEAD)
    ang = positions[0].astype(F32)[:, None] * inv_freq
    cos = jnp.tile(jnp.cos(ang), (1, 4))
    sin = jnp.tile(jnp.concatenate([-jnp.sin(ang), jnp.sin(ang)], axis=1), (1, 2))
    gq = jnp.tile(attn_q_norm, (1, 2))
    gk = jnp.tile(attn_k_norm, (1, 2))
    sinkcol = jnp.repeat(attn_sinks.reshape(HKV, ATT_GROUP // 2, 2).transpose(0, 2, 1), WINDOW, axis=2)[..., None]
    bias_p = _pad_cols(ssm_dt_bias, LANES)
    alog_p = _pad_cols(ssm_a_log, LANES)
    dskip = jnp.repeat(ssm_d, HEAD, axis=1)
    lane_head = jnp.arange(DI) // HEAD
    sel = (jnp.arange(LANES)[None, :, None] == lane_head.reshape(G, 1, GW)).astype(BF16)
    vec_w = CD // N_DEV
    small = jnp.concatenate([ssm_conv_w[0], ssm_conv_b, _pad_cols(ssm_norm, vec_w),
                             jnp.zeros((2, vec_w), F32)], axis=0)

    def rows_to_blocks(p):
        return p.reshape(N_DEV, p.shape[0] // N_DEV, p.shape[1])

    def cols_to_blocks(p):
        return p.reshape(p.shape[0], N_DEV, p.shape[1] // N_DEV).transpose(1, 0, 2)

    hm0, got = _rms_fwd(xs, mixer_norm[0:1], "rms_fwd_m0", carry=[shard("attn_w_qkv")])
    w_qkv = from_cols(got[0])
    qkv, got = _matmul(hm0, w_qkv, mode="nn", out_dtype=F32, name="mm_qkv",
                       carry=[shard("attn_w_o"), shard("ffn_w_gate", 0), small])
    w_o = from_rows(got[0])
    w_gate = [from_cols(got[1]), None]
    small_all = got[2]
    conv_w = small_all[:, :SSM_CONV].transpose(1, 0, 2).reshape(SSM_CONV, CD)
    conv_b = small_all[:, SSM_CONV].reshape(1, CD)
    ng = small_all[:, SSM_CONV + 1, :DI // N_DEV].reshape(1, DI)
    qr, kd, vd = _attn_prep_fwd(qkv, cos, sin, gq, gk, D, HKV)
    o, got = _attn_fwd(qr, kd, vd, sinkcol, HKV, carry=[shard("ffn_w_up", 0)])
    w_up = [from_cols(got[0]), None]
    x1, got = _matmul(o, w_o, mode="nn", out_dtype=F32, name="mm_attn_out", add=xs, carry=[shard("ffn_w_down", 0)])
    w_down = [from_rows(got[0]), None]
    hf0 = _rms_fwd(x1, ffn_norm[0:1], "rms_fwd_f0")
    gate0, up0, act0, got = _ffn_up(hf0, w_gate[0], w_up[0], "ffn_up_0", carry=[shard("ssm_w_in"), shard("ssm_w_out")])
    w_in = _pad_cols(from_cols(got[0]), ZW)
    w_out = from_rows(got[1])
    x2 = _matmul(act0, w_down[0], mode="nn", out_dtype=F32, name="mm_ffn_down_0", add=x1)
    hm1 = _rms_fwd(x2, mixer_norm[1:2], "rms_fwd_m1")
    zx, got = _matmul(hm1, w_in, mode="nn", out_dtype=F32, name="mm_ssm_in",
                      carry=[shard("ffn_w_gate", 1), shard("ffn_w_up", 1)])
    w_gate[1], w_up[1] = from_cols(got[0]), from_cols(got[1])
    xc = _conv_fwd(zx, conv_w, conv_b, DI, CD)
    yssd, yout, states, got = _ssd_fwd(zx, xc, bias_p, alog_p, sel, dskip, ng, DI, carry=[shard("ffn_w_down", 1)])
    w_down[1] = from_rows(got[0])
    x3 = _matmul(yout, w_out, mode="nn", out_dtype=F32, name="mm_ssm_out", add=x2)
    hf1 = _rms_fwd(x3, ffn_norm[1:2], "rms_fwd_f1")
    gate1, up1, act1 = _ffn_up(hf1, w_gate[1], w_up[1], "ffn_up_1")
    x4 = _matmul(act1, w_down[1], mode="nn", out_dtype=F32, name="mm_ffn_down_1", add=x3)
    sq, dx4, dx4b = _loss_head(x4, tgt)
    loss = lax.psum(sq[0, 0] * (0.5 / D), ("x", "y", "c"))

    def halves(blocks):
        half = blocks.shape[1] // 2
        return blocks[:, :half], blocks[:, half:]

    def ffn_bwd(dy, dyb, hf, gate, up, act, layer, xin, gain):
        dg, du = _ffn_dact(dyb, w_down[layer], gate, up, f"ffn_dact_{layer}")
        g_down = _matmul(act, dyb, mode="tn", out_dtype=BF16, name=f"mm_dw_down_{layer}")
        down_a, down_b = halves(rows_to_blocks(g_down))
        g_gate, got_da = _matmul(hf, dg, mode="tn", out_dtype=BF16, name=f"mm_dw_gate_{layer}", carry=[down_a])
        g_up, got_db = _matmul(hf, du, mode="tn", out_dtype=BF16, name=f"mm_dw_up_{layer}", carry=[down_b])
        gate_a, gate_b = halves(cols_to_blocks(g_gate))
        dh, got_ga = _matmul(dg, w_gate[layer], mode="nt", out_dtype=F32, name=f"mm_dh_gate_{layer}", carry=[gate_a])
        dh, got_gb = _matmul(du, w_up[layer], mode="nt", out_dtype=F32, name=f"mm_dh_up_{layer}", add=dh,
                             carry=[gate_b])
        dx, dxb, dgain = _rms_bwd(xin, gain, dh, dy, f"rms_bwd_f{layer}")
        return dx, dxb, dgain, cols_to_blocks(g_up), dict(down=[got_da[0], got_db[0]], gate=[got_ga[0], got_gb[0]])

    dx3, dx3b, d_fn1, up1_blocks, ffn1_got = ffn_bwd(dx4, dx4b, hf1, gate1, up1, act1, 1, x3, ffn_norm[1:2])
    dyo = _matmul(dx3b, w_out, mode="nt", out_dtype=F32, name="mm_dyout")
    g_wout = _matmul(yout, dx3b, mode="tn", out_dtype=BF16, name="mm_dw_ssm_out")
    dzx, dxx, dbm, dcm, ddt_p, dac_p, dd_l, dng_l, got1 = _ssd_bwd(
        zx, xc, yssd, dyo, states, bias_p, alog_p, sel, dskip, ng, DI,
        carry=[up1_blocks, rows_to_blocks(g_wout)])
    dzx, d_alog, d_bias = _ssd_dt_bwd(zx, ddt_p, dac_p, dzx, bias_p, alog_p, DI)
    dzx, d_convw, d_convb = _conv_bwd(zx, [dxx, dbm, dcm], dzx, conv_w, conv_b, DI, CD)
    g_win = _matmul(hm1, dzx, mode="tn", out_dtype=BF16, name="mm_dw_ssm_in")[:, :IW]
    dh, got2 = _matmul(dzx, w_in, mode="nt", out_dtype=F32, name="mm_dh_ssm_in", carry=[cols_to_blocks(g_win)])
    dx2, dx2b, d_mn1 = _rms_bwd(x2, mixer_norm[1:2], dh, dx3, "rms_bwd_m1")
    dx1, dx1b, d_fn0, up0_blocks, ffn0_got = ffn_bwd(dx2, dx2b, hf0, gate0, up0, act0, 0, x1, ffn_norm[0:1])
    do = _matmul(dx1b, w_o, mode="nt", out_dtype=F32, name="mm_do")
    g_wo = _matmul(o, dx1b, mode="tn", out_dtype=BF16, name="mm_dw_attn_out")
    dq, dkd, dvd, dsink, got3 = _attn_bwd(qr, kd, vd, o, do, sinkcol, HKV, carry=[up0_blocks, rows_to_blocks(g_wo)])
    dqkv, dgq_l, dgk_l = _attn_prep_bwd(qkv, dq, dkd, dvd, cos, sin, gq, gk, D, HKV)
    g_wqkv = _matmul(hm0, dqkv, mode="tn", out_dtype=BF16, name="mm_dw_qkv")
    dh, got_c = _matmul(dqkv, w_qkv, mode="nt", out_dtype=F32, name="mm_dh_qkv", carry=[cols_to_blocks(g_wqkv)])
    dx0, _, d_mn0 = _rms_bwd(xs, mixer_norm[0:1], dh, dx1, "rms_bwd_m0")

    d_ng = dng_l.reshape(1, DI)
    vec_send = jnp.concatenate([
        d_convw.reshape(SSM_CONV, N_DEV, vec_w).transpose(1, 0, 2), d_convb.reshape(1, N_DEV, vec_w).transpose(1, 0, 2),
        _pad_cols(d_ng.reshape(N_DEV, DI // N_DEV), vec_w)[:, None, :], jnp.zeros((N_DEV, 2, vec_w), F32)], axis=1)
    d_sinks = dsink[:, :, 0].reshape(1, HQ)
    d_gq = dgq_l[:, :HEAD] + dgq_l[:, HEAD:]
    d_gk = dgk_l[:, :HEAD] + dgk_l[:, HEAD:]
    d_dskip = dd_l.reshape(H, HEAD).sum(axis=1).reshape(1, H)
    rep_names = ["mixer_norm", "ffn_norm", "attn_q_norm", "attn_k_norm", "attn_sinks", "ssm_dt_bias", "ssm_a_log",
                 "ssm_d"]
    rep_grads = [jnp.concatenate([d_mn0, d_mn1], axis=0), jnp.concatenate([d_fn0, d_fn1], axis=0), d_gq, d_gk,
                 d_sinks, d_bias[:, :H], d_alog[:, :H], d_dskip]
    rep_sizes = [weights[n].size for n in rep_names]
    rep_len = -(-sum(rep_sizes) // (8 * LANES)) * 8 * LANES

    def pack(arrs):
        flat = jnp.concatenate([a.reshape(-1) for a in arrs])
        return jnp.pad(flat, (0, rep_len - flat.shape[0])).reshape(rep_len // LANES, LANES)

    rep_send = jnp.broadcast_to(pack(rep_grads)[None], (N_DEV, rep_len // LANES, LANES))
    got4 = _exchange([vec_send, rep_send], "exchange_last")
    parts_of = {
        "attn_w_qkv": [got_c[0]], "attn_w_o": [got3[1]], "ssm_w_in": [got2[0]], "ssm_w_out": [got1[1]],
        "ffn_w_gate": ffn0_got["gate"] + ffn1_got["gate"], "ffn_w_up": [got3[0], got1[0]],
        "ffn_w_down": ffn0_got["down"] + ffn1_got["down"],
    }

    out = {}
    for n in big:
        res = _adamw(parts_of[n], flat2(weights[n]), flat2(mom_m[n]), flat2(mom_v[n]), f"adamw_{n}")
        out[n] = [r.reshape(weights[n].shape) for r in res]

    def vec_block(d):
        return jnp.concatenate([d["ssm_conv_w"][0], d["ssm_conv_b"], _pad_cols(d["ssm_norm"], vec_w),
                                jnp.zeros((2, vec_w), F32)], axis=0)

    res = _adamw([got4[0]], vec_block(weights), vec_block(mom_m), vec_block(mom_v), "adamw_vectors")
    out["ssm_conv_w"] = [r[:SSM_CONV][None] for r in res]
    out["ssm_conv_b"] = [r[SSM_CONV:SSM_CONV + 1] for r in res]
    out["ssm_norm"] = [r[SSM_CONV + 1:SSM_CONV + 2, :DI // N_DEV] for r in res]
    res = _adamw([got4[1]], pack([weights[n] for n in rep_names]), pack([mom_m[n] for n in rep_names]),
                 pack([mom_v[n] for n in rep_names]), "adamw_replicated")
    offs = 0
    for n, sz in zip(rep_names, rep_sizes):
        out[n] = [r.reshape(-1)[offs:offs + sz].reshape(weights[n].shape) for r in res]
        offs += sz

    names = list(weights)
    return (loss, dx0[None], *[out[n][0] for n in names], *[out[n][1] for n in names],
            *[out[n][2] for n in names], *[out[n][3] for n in names])
```

```python
import functools
import math

import jax
import jax.numpy as jnp
from jax import lax
from jax.experimental import pallas as pl
from jax.experimental.pallas import tpu as pltpu

F32 = jnp.float32
BF16 = jnp.bfloat16

N_DEV = 8
EPS = 1e-6
LANES = 128
HEAD = 64
ATT_GROUP = 8
ATT_GW = ATT_GROUP * HEAD
WINDOW = 128
ATT_STEP_BLOCKS = 16
ROPE_THETA = 10000.0
SSM_GROUPS = 8
SSM_STATE = 128
SSM_CONV = 4
SSM_CHUNK = 256
HALO = 8
CONV_ROWS = 1024
ADAM_LR, ADAM_B1, ADAM_B2, ADAM_EPS, ADAM_WD, ADAM_STEP = 0.001, 0.9, 0.999, 1e-08, 0.01, 10
VMEM_LIMIT = 56 * 1024 * 1024
MATMUL_VMEM = 44 * 1024 * 1024
MESH = pl.DeviceIdType.MESH

_NN = (((1,), (0,)), ((), ()))
_NT = (((1,), (1,)), ((), ()))
_TN = (((0,), (0,)), ((), ()))


def _dot(a, b, dims=_NN):
    return lax.dot_general(a, b, dims, preferred_element_type=F32)


def _tile(n, cap):
    if n % LANES:
        return n
    best = LANES
    for t in range(LANES, min(n, cap) + 1, LANES):
        if n % t == 0:
            best = t
    return best


def _params(sem):
    return pltpu.CompilerParams(dimension_semantics=sem, vmem_limit_bytes=VMEM_LIMIT)


def _slot(px, py, pc):
    return 4 * px + 2 * py + pc


def _direct_copies(srcs, dsts, send_sems, recv_sems, local_sems, with_arrivals=True):
    x, y, c = lax.axis_index("x"), lax.axis_index("y"), lax.axis_index("c")
    me = _slot(x, y, c)
    peers = [(x ^ (m >> 2), y ^ ((m >> 1) & 1), c ^ (m & 1)) for m in range(1, N_DEV)]
    local, sends, arrivals = [], [], []
    for w, (src, dst) in enumerate(zip(srcs, dsts)):
        sliced = src.shape == dst.shape
        local.append(pltpu.make_async_copy(src.at[me] if sliced else src, dst.at[me], local_sems.at[w]))
        for k, peer in enumerate(peers):
            sems = dict(send_sem=send_sems.at[w, k], recv_sem=recv_sems.at[w, k], device_id=peer, device_id_type=MESH)
            sends.append(pltpu.make_async_remote_copy(
                src_ref=src.at[_slot(*peer)] if sliced else src, dst_ref=dst.at[me], **sems))
            if with_arrivals:
                arrivals.append(pltpu.make_async_remote_copy(
                    src_ref=src.at[me] if sliced else src, dst_ref=dst.at[_slot(*peer)], **sems))
    return local, sends, arrivals


def _gather_phases(srcs, dsts, send_sems, recv_sems, local_sems):
    x, y, c = lax.axis_index("x"), lax.axis_index("y"), lax.axis_index("c")
    me, sibling = (x, y, c), (x, y, 1 - c)
    chips = [(1 - x, y), (x, 1 - y), (1 - x, 1 - y)]
    n = len(srcs)

    def copy(w, k, block, to, src=None):
        dst = dsts[w].at[_slot(*block)]
        return pltpu.make_async_remote_copy(
            src_ref=dst if src is None else src, dst_ref=dst, send_sem=send_sems.at[w, k],
            recv_sem=recv_sems.at[w, k], device_id=to, device_id_type=MESH)

    def first_sends(w):
        return [copy(w, 0, me, sibling, src=srcs[w])] + [copy(w, 1 + j, me, (*chip, c), src=srcs[w])
                                                         for j, chip in enumerate(chips)]

    def start():
        for w in range(n):
            pltpu.make_async_copy(srcs[w], dsts[w].at[_slot(*me)], local_sems.at[w]).start()
            for cp in first_sends(w):
                cp.start()

    def forward():
        for w in range(n):
            for j, chip in enumerate(chips):
                copy(w, 1 + j, (*chip, c), me).wait_recv()
                copy(w, 4 + j, (*chip, c), sibling).start()

    def finish():
        for w in range(n):
            copy(w, 0, sibling, me).wait_recv()
            for j, chip in enumerate(chips):
                copy(w, 4 + j, (*chip, 1 - c), me).wait_recv()
        for w in range(n):
            for cp in first_sends(w) + [copy(w, 4 + j, (*chip, c), sibling) for j, chip in enumerate(chips)]:
                cp.wait_send()
            pltpu.make_async_copy(srcs[w], dsts[w].at[_slot(*me)], local_sems.at[w]).wait()

    return start, forward, finish


def _call(body, *, name, grid, in_specs, out_specs, out_shape, sem, args, scratch_shapes=(), carry=None):
    if carry is None:
        return pl.pallas_call(body, name=name, grid=grid, in_specs=in_specs, out_specs=out_specs, out_shape=out_shape,
                              scratch_shapes=list(scratch_shapes), compiler_params=_params(sem))(*args)
    n_in, n_out, n_sc, n_c = len(in_specs), len(out_specs), len(scratch_shapes), len(carry)
    gather = all(a.ndim == 2 for a in carry)
    assert gather or all(a.ndim == 3 and a.shape[0] == N_DEV for a in carry)
    recv_shape = [jax.ShapeDtypeStruct((N_DEV,) + a.shape if gather else a.shape, a.dtype) for a in carry]
    n_steps = math.prod(grid)

    def wrapped(*refs):
        ins, c_in = refs[:n_in], refs[n_in:n_in + n_c]
        outs, c_out = refs[n_in + n_c:n_in + n_c + n_out], refs[n_in + n_c + n_out:n_in + 2 * n_c + n_out]
        scr = refs[n_in + 2 * n_c + n_out:n_in + 2 * n_c + n_out + n_sc]
        sems = refs[-3:]
        step = functools.reduce(lambda acc, d: acc * grid[d] + pl.program_id(d), range(len(grid)), 0)
        if gather:
            start, forward, finish = _gather_phases(c_in, c_out, *sems)
        else:
            def start():
                local, sends, _ = _direct_copies(c_in, c_out, *sems, with_arrivals=False)
                for cp in local + sends:
                    cp.start()

            def finish():
                local, sends, arrivals = _direct_copies(c_in, c_out, *sems)
                for cp in arrivals:
                    cp.wait_recv()
                for cp in sends:
                    cp.wait_send()
                for cp in local:
                    cp.wait()

        pl.when(step == 0)(start)
        if gather:
            pl.when(step == min((3 * n_steps) // 4, n_steps - 1))(forward)
        body(*ins, *outs, *scr)
        pl.when(step == n_steps - 1)(finish)

    any_spec = pl.BlockSpec(memory_space=pl.ANY)
    res = pl.pallas_call(
        wrapped, name=name, grid=grid, in_specs=list(in_specs) + [any_spec] * n_c,
        out_specs=list(out_specs) + [any_spec] * n_c, out_shape=list(out_shape) + recv_shape,
        scratch_shapes=list(scratch_shapes) + [pltpu.SemaphoreType.DMA((n_c, N_DEV - 1)),
                                               pltpu.SemaphoreType.DMA((n_c, N_DEV - 1)), pltpu.SemaphoreType.DMA((n_c,))],
        compiler_params=_params(("arbitrary",) * len(grid)),
    )(*args, *carry)
    return res


def _matmul(a, b, *, mode, out_dtype, name, add=None, carry=None, second=None):
    if mode == "nn":
        (M, K), N = a.shape, b.shape[1]
    elif mode == "nt":
        (M, K), N = a.shape, b.shape[0]
    else:
        (K, M), N = a.shape, b.shape[1]
    assert a.dtype == BF16 and b.dtype == BF16
    has_add = add is not None
    pairs = 1 if second is None else 2
    assert pairs == 1 or (mode != "tn" and second[0].shape == a.shape and second[1].shape == b.shape)
    tn = _tile(N, 512)
    for tm in ((_tile(M, 512),) if mode == "tn" else (_tile(M, 2048), _tile(M, 1024))):
        fixed = 2 * tm * tn * (jnp.dtype(out_dtype).itemsize + (4 if has_add else 0)) + tm * tn * 4
        per_k = pairs * 2 * 2 * (tm + tn) + (2 * tm if mode == "tn" else 0)
        tk = _tile(K, max(LANES, (MATMUL_VMEM - fixed) // per_k))
        if tk == K:
            break
    nk = K // tk
    dims = _NT if mode == "nt" else _NN
    if mode == "tn":
        a_spec = pl.BlockSpec((tk, tm), lambda i, j, k: (jnp.where(j == 0, k, 0), i))
    else:
        a_spec = pl.BlockSpec((tm, tk), lambda i, j, k: (i, k))
    b_spec = pl.BlockSpec((tn, tk), lambda i, j, k: (j, k)) if mode == "nt" else pl.BlockSpec((tk, tn), lambda i, j, k: (k, j))
    o_spec = pl.BlockSpec((tm, tn), lambda i, j, k: (i, j))

    n_ab = 2 * pairs

    def body(*refs):
        a_ref, b_ref = refs[:2]
        add_ref = refs[n_ab] if has_add else None
        o_ref = refs[n_ab + has_add]
        scratch = list(refs[n_ab + 1 + has_add:])
        at = scratch.pop(0) if mode == "tn" else None
        acc = scratch.pop(0) if nk > 1 else None
        j, k = pl.program_id(1), pl.program_id(2)
        if mode == "tn":
            @pl.when(j == 0)
            def _():
                at[k] = a_ref[...].T

            part = _dot(at[k], b_ref[...], dims)
        else:
            part = _dot(a_ref[...], b_ref[...], dims)
            if pairs == 2:
                part = part + _dot(refs[2][...], refs[3][...], dims)

        def finish(r):
            if has_add:
                r = r + add_ref[...]
            o_ref[...] = r.astype(out_dtype)

        if nk == 1:
            finish(part)
        else:
            @pl.when(k == 0)
            def _():
                acc[...] = part

            @pl.when(jnp.logical_and(k > 0, k < nk - 1))
            def _():
                acc[...] += part

            @pl.when(k == nk - 1)
            def _():
                finish(acc[...] + part)

    scratch = ([pltpu.VMEM((nk, tm, tk), BF16)] if mode == "tn" else []) + ([pltpu.VMEM((tm, tn), F32)] if nk > 1 else [])
    res = _call(
        body, name=name, grid=(M // tm, N // tn, nk),
        in_specs=[a_spec, b_spec] * pairs + ([o_spec] if has_add else []),
        out_specs=[o_spec], out_shape=[jax.ShapeDtypeStruct((M, N), out_dtype)],
        scratch_shapes=scratch, sem=("parallel", "arbitrary", "arbitrary"),
        args=(a, b) + (tuple(second) if pairs == 2 else ()) + ((add,) if has_add else ()), carry=carry)
    return res[0] if carry is None else (res[0], res[1:])


def _rms_fwd(x, gain, name, carry=None):
    T, D = x.shape
    tr = 256

    def body(x_ref, g_ref, h_ref):
        xv = x_ref[...]
        rstd = lax.rsqrt(jnp.mean(xv * xv, axis=1, keepdims=True) + EPS)
        h_ref[...] = (xv * rstd * g_ref[...]).astype(BF16)

    res = _call(
        body, name=name, grid=(T // tr,),
        in_specs=[pl.BlockSpec((tr, D), lambda i: (i, 0)), pl.BlockSpec((1, D), lambda i: (0, 0))],
        out_specs=[pl.BlockSpec((tr, D), lambda i: (i, 0))],
        out_shape=[jax.ShapeDtypeStruct((T, D), BF16)], sem=("parallel",), args=(x, gain), carry=carry)
    return res[0] if carry is None else (res[0], res[1:])


def _rms_bwd(x, gain, dh, dres, name):
    T, D = x.shape
    tr = 256

    def body(x_ref, g_ref, dh_ref, dr_ref, dx_ref, dxb_ref, dg_ref):
        @pl.when(pl.program_id(0) == 0)
        def _():
            dg_ref[...] = jnp.zeros_like(dg_ref)

        xv = x_ref[...]
        rstd = lax.rsqrt(jnp.mean(xv * xv, axis=1, keepdims=True) + EPS)
        xhat = xv * rstd
        dy = dh_ref[...].astype(F32)
        dg_ref[...] += jnp.sum(dy * xhat, axis=0, keepdims=True)
        dxh = dy * g_ref[...]
        dx = dr_ref[...] + rstd * (dxh - xhat * jnp.mean(dxh * xhat, axis=1, keepdims=True))
        dx_ref[...] = dx
        dxb_ref[...] = dx.astype(BF16)

    row = pl.BlockSpec((tr, D), lambda i: (i, 0))
    vec = pl.BlockSpec((1, D), lambda i: (0, 0))
    return pl.pallas_call(
        body, name=name, grid=(T // tr,), in_specs=[row, vec, row, row], out_specs=[row, row, vec],
        out_shape=[jax.ShapeDtypeStruct((T, D), F32), jax.ShapeDtypeStruct((T, D), BF16),
                   jax.ShapeDtypeStruct((1, D), F32)],
        compiler_params=_params(("arbitrary",)),
    )(x, gain, dh, dres)


def _loss_head(y, target):
    T, D = y.shape
    tr = 256

    def body(y_ref, t_ref, s_ref, d_ref, db_ref):
        @pl.when(pl.program_id(0) == 0)
        def _():
            s_ref[...] = jnp.zeros_like(s_ref)

        e = y_ref[...] - t_ref[...]
        s_ref[...] += jnp.sum(jnp.sum(e * e, axis=1, keepdims=True), axis=0, keepdims=True)
        d = e * (1.0 / D)
        d_ref[...] = d
        db_ref[...] = d.astype(BF16)

    row = pl.BlockSpec((tr, D), lambda i: (i, 0))
    return pl.pallas_call(
        body, name="loss_head", grid=(T // tr,), in_specs=[row, row],
        out_specs=[pl.BlockSpec((1, 1), lambda i: (0, 0)), row, row],
        out_shape=[jax.ShapeDtypeStruct((1, 1), F32), jax.ShapeDtypeStruct((T, D), F32),
                   jax.ShapeDtypeStruct((T, D), BF16)],
        compiler_params=_params(("arbitrary",)),
    )(y, target)


def _ffn_up(h, wg, wu, name, carry=None):
    (T, D), Fd = h.shape, wg.shape[1]
    tm, tn = _tile(T, 1024), _tile(Fd, 512)

    def body(h_ref, wg_ref, wu_ref, fg_ref, fu_ref, a_ref):
        hv = h_ref[...]
        g = _dot(hv, wg_ref[...])
        s = jax.nn.sigmoid(g)
        silu = g * s
        fu_ref[...] = silu.astype(BF16)
        u = _dot(hv, wu_ref[...])
        fg_ref[...] = (u * (s + silu * (1.0 - s))).astype(BF16)
        a_ref[...] = (silu * u).astype(BF16)

    w_spec = pl.BlockSpec((D, tn), lambda i, j: (0, j))
    o_spec = pl.BlockSpec((tm, tn), lambda i, j: (i, j))
    res = _call(
        body, name=name, grid=(T // tm, Fd // tn),
        in_specs=[pl.BlockSpec((tm, D), lambda i, j: (i, 0)), w_spec, w_spec],
        out_specs=[o_spec, o_spec, o_spec],
        out_shape=[jax.ShapeDtypeStruct((T, Fd), BF16)] * 3,
        sem=("parallel", "arbitrary"), args=(h, wg, wu), carry=carry)
    return res if carry is None else (*res[:3], res[3:])


def _ffn_dact(dy, wd, fgate, fup, name):
    (T, D), Fd = dy.shape, wd.shape[0]
    tm, tn = _tile(T, 2048), _tile(Fd, 512)

    def body(dy_ref, wd_ref, fg_ref, fu_ref, dg_ref, du_ref):
        da = _dot(dy_ref[...], wd_ref[...], _NT)
        du_ref[...] = (da * fu_ref[...].astype(F32)).astype(BF16)
        dg_ref[...] = (da * fg_ref[...].astype(F32)).astype(BF16)

    o_spec = pl.BlockSpec((tm, tn), lambda i, j: (i, j))
    return pl.pallas_call(
        body, name=name, grid=(T // tm, Fd // tn),
        in_specs=[pl.BlockSpec((tm, D), lambda i, j: (i, 0)), pl.BlockSpec((tn, D), lambda i, j: (j, 0)),
                  o_spec, o_spec],
        out_specs=[o_spec, o_spec],
        out_shape=[jax.ShapeDtypeStruct((T, Fd), BF16), jax.ShapeDtypeStruct((T, Fd), BF16)],
        compiler_params=_params(("parallel", "arbitrary")),
    )(dy, wd, fgate, fup)


def _is_a(shape):
    return lax.broadcasted_iota(jnp.int32, shape, 1) % LANES < HEAD


def _split2(v):
    hi = v.astype(BF16)
    return hi, (v - hi.astype(F32)).astype(BF16)


def _split3(v):
    hi = v.astype(BF16)
    r = v - hi.astype(F32)
    mid = r.astype(BF16)
    return hi, mid, (r - mid.astype(F32)).astype(BF16)


def _dot_split(v, m, pieces, dims=_NN):
    parts = _split3(v) if pieces == 3 else _split2(v)
    out = _dot(parts[0], m, dims)
    for p in parts[1:]:
        out = out + _dot(p, m, dims)
    return out


def _head_blockdiag():
    r = lax.broadcasted_iota(jnp.int32, (LANES, LANES), 0) // HEAD
    c = lax.broadcasted_iota(jnp.int32, (LANES, LANES), 1) // HEAD
    return (r == c).astype(BF16)


def _swap_half(v):
    lane = lax.broadcasted_iota(jnp.int32, v.shape, 1)
    return jnp.where(lane % HEAD < HEAD // 2, pltpu.roll(v, LANES - HEAD // 2, axis=1), pltpu.roll(v, HEAD // 2, axis=1))


def _attn_prep_fwd(qkv, cos, sin, gq, gk, D, HKV):
    T, QW = qkv.shape
    tr = 256
    nq, nk = D // LANES, HKV // 2
    KW = HKV * LANES

    def body(x_ref, cos_ref, sin_ref, gq_ref, gk_ref, q_ref, k_ref, v_ref):
        bd = _head_blockdiag()
        cs, sn = cos_ref[...], sin_ref[...]
        isa = _is_a((tr, LANES))

        def normrope(xv, g):
            ms = _dot_split(xv * xv, bd, 2) * (1.0 / HEAD)
            xn = xv * lax.rsqrt(ms + EPS) * g
            return xn * cs + _swap_half(xn) * sn

        def dup(v):
            r = pltpu.roll(v, HEAD, axis=1)
            return jnp.where(isa, v, r), jnp.where(isa, r, v)

        for s in range(nq):
            sl = slice(s * LANES, (s + 1) * LANES)
            q_ref[:, sl] = normrope(x_ref[:, sl], gq_ref[...]).astype(BF16)
        for s in range(nk):
            ka, kb = dup(normrope(x_ref[:, D + s * LANES:D + (s + 1) * LANES], gk_ref[...]))
            k_ref[:, 2 * s * LANES:(2 * s + 1) * LANES] = ka.astype(BF16)
            k_ref[:, (2 * s + 1) * LANES:(2 * s + 2) * LANES] = kb.astype(BF16)
            va, vb = dup(x_ref[:, D + (nk + s) * LANES:D + (nk + s + 1) * LANES])
            v_ref[:, 2 * s * LANES:(2 * s + 1) * LANES] = va.astype(BF16)
            v_ref[:, (2 * s + 1) * LANES:(2 * s + 2) * LANES] = vb.astype(BF16)

    tab = pl.BlockSpec((tr, LANES), lambda i: (i, 0))
    vec = pl.BlockSpec((1, LANES), lambda i: (0, 0))
    return pl.pallas_call(
        body, name="attn_prep_fwd", grid=(T // tr,),
        in_specs=[pl.BlockSpec((tr, QW), lambda i: (i, 0)), tab, tab, vec, vec],
        out_specs=[pl.BlockSpec((tr, D), lambda i: (i, 0)), pl.BlockSpec((tr, KW), lambda i: (i, 0)),
                   pl.BlockSpec((tr, KW), lambda i: (i, 0))],
        out_shape=[jax.ShapeDtypeStruct((T, D), BF16), jax.ShapeDtypeStruct((T, KW), BF16),
                   jax.ShapeDtypeStruct((T, KW), BF16)],
        compiler_params=_params(("parallel",)),
    )(qkv, cos, sin, gq, gk)


def _attn_prep_bwd(qkv, dq, dkd, dvd, cos, sin, gq, gk, D, HKV):
    T, QW = qkv.shape
    tr = 256
    nq, nk = D // LANES, HKV // 2
    KW = HKV * LANES

    def body(x_ref, dq_ref, dk_ref, dv_ref, cos_ref, sin_ref, gq_ref, gk_ref, o_ref, dgq_ref, dgk_ref):
        @pl.when(pl.program_id(0) == 0)
        def _():
            dgq_ref[...] = jnp.zeros_like(dgq_ref)
            dgk_ref[...] = jnp.zeros_like(dgk_ref)

        bd = _head_blockdiag()
        cs, sn = cos_ref[...], sin_ref[...]
        isa = _is_a((tr, LANES))

        def back(xv, dy, g):
            rstd = lax.rsqrt(_dot_split(xv * xv, bd, 2) * (1.0 / HEAD) + EPS)
            xhat = xv * rstd
            dxn = dy * cs + _swap_half(dy * sn)
            dxh = dxn * g
            mean = _dot_split(dxh * xhat, bd, 2) * (1.0 / HEAD)
            return rstd * (dxh - xhat * mean), jnp.sum(dxn * xhat, axis=0, keepdims=True)

        def fold(s):
            a = dk_ref[:, 2 * s * LANES:(2 * s + 1) * LANES]
            b = dk_ref[:, (2 * s + 1) * LANES:(2 * s + 2) * LANES]
            return jnp.where(isa, a + pltpu.roll(a, HEAD, axis=1), b + pltpu.roll(b, HEAD, axis=1))

        def foldv(s):
            a = dv_ref[:, 2 * s * LANES:(2 * s + 1) * LANES]
            b = dv_ref[:, (2 * s + 1) * LANES:(2 * s + 2) * LANES]
            return jnp.where(isa, a + pltpu.roll(a, HEAD, axis=1), b + pltpu.roll(b, HEAD, axis=1))

        dgq = jnp.zeros((1, LANES), F32)
        for s in range(nq):
            sl = slice(s * LANES, (s + 1) * LANES)
            dx, dg = back(x_ref[:, sl], dq_ref[:, sl], gq_ref[...])
            o_ref[:, sl] = dx.astype(BF16)
            dgq = dgq + dg
        dgq_ref[...] += dgq
        dgk = jnp.zeros((1, LANES), F32)
        for s in range(nk):
            sl = slice(D + s * LANES, D + (s + 1) * LANES)
            dx, dg = back(x_ref[:, sl], fold(s), gk_ref[...])
            o_ref[:, sl] = dx.astype(BF16)
            dgk = dgk + dg
            o_ref[:, D + (nk + s) * LANES:D + (nk + s + 1) * LANES] = foldv(s).astype(BF16)
        dgk_ref[...] += dgk

    tab = pl.BlockSpec((tr, LANES), lambda i: (i, 0))
    vec = pl.BlockSpec((1, LANES), lambda i: (0, 0))
    kv = pl.BlockSpec((tr, KW), lambda i: (i, 0))
    return pl.pallas_call(
        body, name="attn_prep_bwd", grid=(T // tr,),
        in_specs=[pl.BlockSpec((tr, QW), lambda i: (i, 0)), pl.BlockSpec((tr, D), lambda i: (i, 0)), kv, kv,
                  tab, tab, vec, vec],
        out_specs=[pl.BlockSpec((tr, QW), lambda i: (i, 0)), vec, vec],
        out_shape=[jax.ShapeDtypeStruct((T, QW), BF16), jax.ShapeDtypeStruct((1, LANES), F32),
                   jax.ShapeDtypeStruct((1, LANES), F32)],
        compiler_params=_params(("arbitrary",)),
    )(qkv, dq, dkd, dvd, cos, sin, gq, gk)


def _attn_probs(qs, kw, sink_ref, first, scale):
    rows = qs.shape[0]
    qi = lax.broadcasted_iota(jnp.int32, (rows, 2 * WINDOW), 0) % WINDOW
    kj = lax.broadcasted_iota(jnp.int32, (rows, 2 * WINDOW), 1)
    valid = (kj > qi) & (kj <= qi + WINDOW)
    if first is not False:
        valid = valid & jnp.logical_or(jnp.logical_not(first), kj >= WINDOW)
    isa = _is_a(kw.shape)
    out = []
    for pos in (0, 1):
        kp = jnp.where(isa if pos == 0 else ~isa, kw, jnp.zeros_like(kw))
        s = jnp.where(valid, _dot(qs, kp, _NT) * scale, -jnp.inf)
        sink = sink_ref[0, pos]
        m = jnp.maximum(jnp.max(s, axis=1, keepdims=True), sink)
        p = jnp.exp(s - m)
        ps = jnp.exp(sink - m)
        inv = 1.0 / (jnp.sum(p, axis=1, keepdims=True) + ps)
        out.append((p * inv, ps * inv, kp))
    return out


def _attn_specs(qb):
    q = pl.BlockSpec((qb * WINDOW, ATT_GW), lambda g, n: (n, g))
    cur = pl.BlockSpec((qb * WINDOW, LANES), lambda g, n: (n, g))
    prev = pl.BlockSpec((WINDOW, LANES), lambda g, n: (jnp.maximum(qb * n - 1, 0), g))
    sink = pl.BlockSpec((1, 2, ATT_GW, 1), lambda g, n: (g, 0, 0, 0))
    return q, cur, prev, sink


def _stack(ref, s):
    rows = slice(s * WINDOW, (s + 1) * WINDOW)
    return jnp.concatenate([ref[rows, i * LANES:(i + 1) * LANES] for i in range(ATT_GW // LANES)], axis=0)


def _attn_fwd(q, kd, vd, sinkcol, HKV, carry=None):
    T, D = q.shape
    nb = T // WINDOW
    qb = math.gcd(nb, ATT_STEP_BLOCKS)
    scale = HEAD ** -0.5

    def body(q_ref, kp_ref, kc_ref, vp_ref, vc_ref, sink_ref, o_ref):
        n = pl.program_id(1)
        kall = jnp.concatenate([kp_ref[...], kc_ref[...]], axis=0)
        vall = jnp.concatenate([vp_ref[...], vc_ref[...]], axis=0)
        isa = _is_a((2 * WINDOW, LANES))
        for s in range(qb):
            win = slice(s * WINDOW, (s + 2) * WINDOW)
            kw, vw = kall[win], vall[win]
            o = jnp.zeros((ATT_GW, LANES), F32)
            first = (n == 0) if s == 0 else False
            for pos, (probs, _, _) in enumerate(_attn_probs(_stack(q_ref, s), kw, sink_ref, first, scale)):
                vp = jnp.where(isa if pos == 0 else ~isa, vw, jnp.zeros_like(vw))
                o = o + _dot(probs.astype(BF16), vp)
            for i in range(ATT_GW // LANES):
                o_ref[s * WINDOW:(s + 1) * WINDOW, i * LANES:(i + 1) * LANES] = o[i * WINDOW:(i + 1) * WINDOW].astype(BF16)

    qs_, cur, prev, sink = _attn_specs(qb)
    res = _call(
        body, name="attn_fwd", grid=(HKV, nb // qb), in_specs=[qs_, prev, cur, prev, cur, sink], out_specs=[qs_],
        out_shape=[jax.ShapeDtypeStruct((T, D), BF16)], sem=("parallel", "parallel"),
        args=(q, kd, kd, vd, vd, sinkcol), carry=carry)
    return res[0] if carry is None else (res[0], res[1:])


def _attn_bwd(q, kd, vd, o, do, sinkcol, HKV, carry=None):
    T, D = q.shape
    nb = T // WINDOW
    qb = math.gcd(nb, ATT_STEP_BLOCKS)
    scale = HEAD ** -0.5
    KW = HKV * LANES

    def body(q_ref, kp_ref, kc_ref, vp_ref, vc_ref, o_ref, do_ref, sink_ref, dq_ref, dk_ref, dv_ref, ds_ref):
        n = pl.program_id(1)

        @pl.when(n == 0)
        def _():
            dk_ref[...] = jnp.zeros_like(dk_ref)
            dv_ref[...] = jnp.zeros_like(dv_ref)
            ds_ref[...] = jnp.zeros_like(ds_ref)

        kall = jnp.concatenate([kp_ref[...], kc_ref[...]], axis=0)
        vall = jnp.concatenate([vp_ref[...], vc_ref[...]], axis=0)
        isa_w = _is_a((2 * WINDOW, LANES))
        isa_q = _is_a((ATT_GW, LANES))
        for s in range(qb):
            win = slice(s * WINDOW, (s + 2) * WINDOW)
            kw, vw = kall[win], vall[win]
            qs = _stack(q_ref, s)
            dos = _stack(do_ref, s)
            dd = dos * _stack(o_ref, s).astype(F32)
            dob = dos.astype(BF16)
            dqs = jnp.zeros((ATT_GW, LANES), F32)
            dkw, dvw, dsk = [], [], []
            first = (n == 0) if s == 0 else False
            for pos, (probs, psink, kp) in enumerate(_attn_probs(qs, kw, sink_ref, first, scale)):
                sel_w = isa_w if pos == 0 else ~isa_w
                sel_q = isa_q if pos == 0 else ~isa_q
                delta = jnp.sum(jnp.where(sel_q, dd, 0.0), axis=1, keepdims=True)
                vp = jnp.where(sel_w, vw, jnp.zeros_like(vw))
                dp = _dot(dob, vp, _NT)
                dsb = (probs * (dp - delta) * scale).astype(BF16)
                dqs = dqs + _dot(dsb, kp)
                dkw.append(_dot(dsb, qs, _TN))
                dvw.append(_dot(probs.astype(BF16), dob, _TN))
                dsk.append(-psink * delta)
            dkw = jnp.where(isa_w, dkw[0], dkw[1])
            dvw = jnp.where(isa_w, dvw[0], dvw[1])

            def add_window(dkw=dkw, dvw=dvw, s=s):
                start = pl.multiple_of((qb * n + s - 1) * WINDOW, WINDOW)
                dk_ref[pl.ds(start, 2 * WINDOW), :] += dkw
                dv_ref[pl.ds(start, 2 * WINDOW), :] += dvw

            if s == 0:
                @pl.when(n == 0)
                def _(dkw=dkw, dvw=dvw):
                    dk_ref[0:WINDOW, :] += dkw[WINDOW:]
                    dv_ref[0:WINDOW, :] += dvw[WINDOW:]

                pl.when(n > 0)(add_window)
            else:
                add_window()

            rows = []
            for i in range(ATT_GW // LANES):
                dq_ref[s * WINDOW:(s + 1) * WINDOW, i * LANES:(i + 1) * LANES] = dqs[i * WINDOW:(i + 1) * WINDOW]
                for pos in (0, 1):
                    t = jnp.sum(dsk[pos][i * WINDOW:(i + 1) * WINDOW], axis=0, keepdims=True)
                    rows.append(jnp.broadcast_to(t, (1, LANES)))
            ds_ref[0] += jnp.concatenate(rows, axis=0)

    qs_, cur, prev, sink = _attn_specs(qb)
    dqo = pl.BlockSpec((qb * WINDOW, ATT_GW), lambda g, n: (n, g))
    dkv = pl.BlockSpec((T, LANES), lambda g, n: (0, g))
    res = _call(
        body, name="attn_bwd", grid=(HKV, nb // qb),
        in_specs=[qs_, prev, cur, prev, cur, qs_, dqo, sink],
        out_specs=[dqo, dkv, dkv, pl.BlockSpec((1, ATT_GROUP, LANES), lambda g, n: (g, 0, 0))],
        out_shape=[jax.ShapeDtypeStruct((T, D), F32), jax.ShapeDtypeStruct((T, KW), F32),
                   jax.ShapeDtypeStruct((T, KW), F32), jax.ShapeDtypeStruct((HKV, ATT_GROUP, LANES), F32)],
        sem=("parallel", "arbitrary"), args=(q, kd, kd, vd, vd, o, do, sinkcol), carry=carry)
    return res if carry is None else (*res[:4], res[4:])


def _conv_fwd(zx, w, b, DI, CD):
    T = zx.shape[0]
    cw, tc = _tile(math.gcd(DI, CD), 512), math.gcd(T, CONV_ROWS)
    off = DI // cw

    def body(cur_ref, halo_ref, w_ref, b_ref, o_ref):
        i = pl.program_id(1)
        halo = jnp.where(i > 0, halo_ref[...], 0.0)
        ext = jnp.concatenate([halo, cur_ref[...]], axis=0)
        acc = b_ref[...] + w_ref[SSM_CONV - 1:SSM_CONV, :] * ext[HALO:]
        for k in range(SSM_CONV - 1):
            acc = acc + w_ref[k:k + 1, :] * pltpu.roll(ext, SSM_CONV - 1 - k, axis=0)[HALO:]
        o_ref[...] = acc * jax.nn.sigmoid(acc)

    return pl.pallas_call(
        body, name="ssm_conv_fwd", grid=(CD // cw, T // tc),
        in_specs=[pl.BlockSpec((tc, cw), lambda j, i: (i, off + j)),
                  pl.BlockSpec((HALO, cw), lambda j, i: (jnp.maximum(i * (tc // HALO) - 1, 0), off + j)),
                  pl.BlockSpec((SSM_CONV, cw), lambda j, i: (0, j)), pl.BlockSpec((1, cw), lambda j, i: (0, j))],
        out_specs=pl.BlockSpec((tc, cw), lambda j, i: (i, j)),
        out_shape=jax.ShapeDtypeStruct((T, CD), F32),
        compiler_params=_params(("parallel", "parallel")),
    )(zx, zx, w, b)


def _conv_bwd(zx, dparts, dzx, w, b, DI, CD):
    T = zx.shape[0]
    cw, tc = _tile(math.gcd(DI, CD), 512), math.gcd(T, CONV_ROWS)
    off = DI // cw
    nt = T // tc
    hb = tc // HALO
    ends = [0]
    for p in dparts:
        assert p.shape[1] % cw == 0
        ends.append(ends[-1] + p.shape[1] // cw)
    assert ends[-1] == CD // cw
    n_p = len(dparts)

    def body(*refs):
        cur_ref, prev_ref, next_ref = refs[:3]
        d_refs, dn_refs = refs[3:3 + n_p], refs[3 + n_p:3 + 2 * n_p]
        w_ref, b_ref, _, o_ref, dw_ref, db_ref = refs[3 + 2 * n_p:]
        j, i = pl.program_id(0), pl.program_id(1)

        @pl.when(i == 0)
        def _():
            dw_ref[...] = jnp.zeros_like(dw_ref)
            db_ref[...] = jnp.zeros_like(db_ref)

        def pick(prefs):
            v = prefs[n_p - 1][...]
            for p in range(n_p - 2, -1, -1):
                v = jnp.where(j < ends[p + 1], prefs[p][...], v)
            return v

        prev = jnp.where(i > 0, prev_ref[...], 0.0)
        ext = jnp.concatenate([prev, cur_ref[...], next_ref[...]], axis=0)
        u = b_ref[...] + w_ref[SSM_CONV - 1:SSM_CONV, :] * ext
        for k in range(SSM_CONV - 1):
            u = u + w_ref[k:k + 1, :] * pltpu.roll(ext, SSM_CONV - 1 - k, axis=0)
        u = u[HALO:]
        dnext = jnp.where(i < nt - 1, pick(dn_refs), 0.0)
        dxe = jnp.concatenate([pick(d_refs), dnext], axis=0)
        sg = jax.nn.sigmoid(u)
        du = dxe * sg * (1.0 + u * (1.0 - sg))
        n_e = tc + HALO
        dx = w_ref[SSM_CONV - 1:SSM_CONV, :] * du
        for k in range(SSM_CONV - 1):
            dx = dx + w_ref[k:k + 1, :] * pltpu.roll(du, n_e - (SSM_CONV - 1 - k), axis=0)
        o_ref[...] = dx[:tc].astype(BF16)
        duc = du[:tc]
        db_ref[...] += jnp.sum(duc, axis=0, keepdims=True)
        xs = ext[:n_e]
        dws = []
        for k in range(SSM_CONV):
            sh = xs if k == SSM_CONV - 1 else pltpu.roll(xs, SSM_CONV - 1 - k, axis=0)
            dws.append(jnp.sum(duc * sh[HALO:], axis=0, keepdims=True))
        dw_ref[...] += jnp.concatenate(dws, axis=0)

    def part_specs(p):
        lo, n = ends[p], ends[p + 1] - ends[p]

        def inside(j):
            return jnp.logical_and(j >= lo, j < lo + n)

        col = lambda j: jnp.clip(j - lo, 0, n - 1)
        return (pl.BlockSpec((tc, cw), lambda j, i: (jnp.where(inside(j), i, 0), col(j))),
                pl.BlockSpec((HALO, cw), lambda j, i: (jnp.where(inside(j), jnp.minimum((i + 1) * hb, nt * hb - 1), 0), col(j))))

    specs = [part_specs(p) for p in range(n_p)]
    return pl.pallas_call(
        body, name="ssm_conv_bwd", grid=(CD // cw, nt),
        in_specs=[pl.BlockSpec((tc, cw), lambda j, i: (i, off + j)),
                  pl.BlockSpec((HALO, cw), lambda j, i: (jnp.maximum(i * hb - 1, 0), off + j)),
                  pl.BlockSpec((HALO, cw), lambda j, i: (jnp.minimum((i + 1) * hb, nt * hb - 1), off + j))]
        + [s[0] for s in specs] + [s[1] for s in specs]
        + [pl.BlockSpec((SSM_CONV, cw), lambda j, i: (0, j)), pl.BlockSpec((1, cw), lambda j, i: (0, j)),
           pl.BlockSpec(memory_space=pl.ANY)],
        out_specs=[pl.BlockSpec((tc, cw), lambda j, i: (i, off + j)), pl.BlockSpec((SSM_CONV, cw), lambda j, i: (0, j)),
                   pl.BlockSpec((1, cw), lambda j, i: (0, j))],
        out_shape=[jax.ShapeDtypeStruct(dzx.shape, BF16), jax.ShapeDtypeStruct((SSM_CONV, CD), F32),
                   jax.ShapeDtypeStruct((1, CD), F32)],
        input_output_aliases={5 + 2 * n_p: 0},
        compiler_params=_params(("parallel", "arbitrary")),
    )(zx, zx, zx, *dparts, *dparts, w, b, dzx)


def _tri_dot(v, upper):
    L = v.shape[0]
    r = lax.broadcasted_iota(jnp.int32, (L, L), 0)
    c = lax.broadcasted_iota(jnp.int32, (L, L), 1)
    tri = ((r <= c) if upper else (r >= c)).astype(BF16)
    p = _split3(v)
    return _dot(tri, p[0]) + _dot(tri, p[1]) + _dot(tri, p[2])


def _ssd_time2(dtraw_ref, bias_ref, alog_ref, sel):
    dt = jax.nn.softplus(dtraw_ref[...] + bias_ref[...])
    acum = _tri_dot(dt * (-jnp.exp(alog_ref[...])), False)
    return dt, _dot_split(dt, sel, 3), _dot_split(acum, sel, 3)


def _decay(acs, acs_t, pos):
    L = acs.shape[0]
    r = lax.broadcasted_iota(jnp.int32, (L, L), 0)
    c = lax.broadcasted_iota(jnp.int32, (L, L), 1)
    col = acs[:, HEAD * pos:HEAD * pos + 1]
    row = acs_t[HEAD * pos:HEAD * pos + 1, :]
    return jnp.exp(jnp.where(r >= c, col - row, -jnp.inf))


def _ssd_specs(G, GW, DI, ZW):
    L = SSM_CHUNK
    grp = lambda f: pl.BlockSpec((L, GW), lambda g, c: (f(c), g))
    return dict(
        grp=grp,
        bmat=lambda f: pl.BlockSpec((L, SSM_STATE), lambda g, c: (f(c), DI // SSM_STATE + g)),
        cmat=lambda f: pl.BlockSpec((L, SSM_STATE), lambda g, c: (f(c), DI // SSM_STATE + G + g)),
        dtraw=lambda f: pl.BlockSpec((L, LANES), lambda g, c: (f(c), (2 * DI + 2 * G * SSM_STATE) // LANES)),
        vec=pl.BlockSpec((1, LANES), lambda g, c: (0, 0)),
        gvec=pl.BlockSpec((1, GW), lambda g, c: (0, g)),
        sel=pl.BlockSpec((1, LANES, GW), lambda g, c: (g, 0, 0)),
    )


def _ssd_fwd(zx, xc, bias, alog, sel, dskip, ng, DI, carry=None):
    T, ZW = zx.shape
    G, L = SSM_GROUPS, SSM_CHUNK
    GW = DI // G
    NS = GW // LANES
    nc = T // L
    sp = _ssd_specs(G, GW, DI, ZW)
    ident = lambda c: c

    def body(x_ref, b_ref, c_ref, z_ref, dtraw_ref, bias_ref, alog_ref, sel_ref, d_ref, ng_ref,
             y_ref, yo_ref, st_ref, state):
        c = pl.program_id(1)

        @pl.when(c == 0)
        def _():
            state[...] = jnp.zeros_like(state)

        x = x_ref[...]
        bb, cb_ = b_ref[...].astype(BF16), c_ref[...].astype(BF16)
        cbm = _dot(cb_, bb, _NT)
        _, dtx, acx = _ssd_time2(dtraw_ref, bias_ref, alog_ref, sel_ref[0])
        xdt = x * dtx
        ex = jnp.exp(acx)
        last = acx[L - 1:L, :]
        te = jnp.exp(last - acx)
        dlast = jnp.exp(last)
        isa = _is_a((L, LANES))
        for i in range(NS):
            sl = slice(i * LANES, (i + 1) * LANES)
            acs = acx[:, sl]
            acs_t = acs.T
            xs = xdt[:, sl]
            y = jnp.zeros((L, LANES), F32)
            for pos in (0, 1):
                m = (cbm * _decay(acs, acs_t, pos)).astype(BF16)
                y = y + _dot(m, jnp.where(isa if pos == 0 else ~isa, xs, 0.0).astype(BF16))
            st = state[i]
            st_ref[0, i] = st
            y = y + _dot(cb_, st.astype(BF16)) * ex[:, sl]
            state[i] = st * dlast[:, sl] + _dot(bb, (xs * te[:, sl]).astype(BF16), _TN)
            y_ref[:, sl] = y + d_ref[:, sl] * x[:, sl]
        z = z_ref[...]
        gated = y_ref[...] * (z * jax.nn.sigmoid(z))
        rstd = lax.rsqrt(jnp.mean(gated * gated, axis=1, keepdims=True) + EPS)
        yo_ref[...] = (gated * rstd * ng_ref[...]).astype(BF16)

    res = _call(
        body, name="ssd_fwd", grid=(G, nc),
        in_specs=[sp["grp"](ident), sp["bmat"](ident), sp["cmat"](ident), sp["grp"](ident), sp["dtraw"](ident),
                  sp["vec"], sp["vec"], sp["sel"], sp["gvec"], sp["gvec"]],
        out_specs=[sp["grp"](ident), sp["grp"](ident),
                   pl.BlockSpec((1, NS, SSM_STATE, LANES), lambda g, c: (c, g, 0, 0))],
        out_shape=[jax.ShapeDtypeStruct((T, DI), F32), jax.ShapeDtypeStruct((T, DI), BF16),
                   jax.ShapeDtypeStruct((nc, G * NS, SSM_STATE, LANES), F32)],
        scratch_shapes=[pltpu.VMEM((NS, SSM_STATE, LANES), F32)],
        sem=("parallel", "arbitrary"), args=(xc, xc, xc, zx, zx, bias, alog, sel, dskip, ng), carry=carry)
    return res if carry is None else (*res[:3], res[3:])


def _ssd_bwd(zx, xc, yssd, dyo, states, bias, alog, sel, dskip, ng, DI, carry=None):
    T, ZW = zx.shape
    G, L = SSM_GROUPS, SSM_CHUNK
    GW = DI // G
    NS = GW // LANES
    nc = T // L
    sp = _ssd_specs(G, GW, DI, ZW)
    rev = lambda c: nc - 1 - c

    def body(x_ref, b_ref, c_ref, z_ref, dtraw_ref, y_ref, dyo_ref, st_ref, bias_ref, alog_ref, sel_ref,
             d_ref, ng_ref, dz_ref, dx_ref, db_ref, dc_ref, ddt_ref, dac_ref, dd_ref, dng_ref, dstate):
        c = pl.program_id(1)

        @pl.when(c == 0)
        def _():
            dstate[...] = jnp.zeros_like(dstate)
            dd_ref[...] = jnp.zeros_like(dd_ref)
            dng_ref[...] = jnp.zeros_like(dng_ref)

        z, ys, dyo = z_ref[...], y_ref[...], dyo_ref[...]
        sg = jax.nn.sigmoid(z)
        sz = z * sg
        gated = ys * sz
        rstd = lax.rsqrt(jnp.mean(gated * gated, axis=1, keepdims=True) + EPS)
        yn = gated * rstd
        dng_ref[0] += jnp.sum(dyo * yn, axis=0, keepdims=True)
        dyn = dyo * ng_ref[...]
        dgated = rstd * (dyn - yn * jnp.mean(dyn * yn, axis=1, keepdims=True))
        g = dgated * sz
        dz_ref[...] = (dgated * ys * sg * (1.0 + z * (1.0 - sg))).astype(BF16)

        x = x_ref[...]
        dsk = d_ref[...]
        dd_ref[0] += jnp.sum(g * x, axis=0, keepdims=True)
        bb, cb_ = b_ref[...].astype(BF16), c_ref[...].astype(BF16)
        cbm = _dot(cb_, bb, _NT)
        _, dtx, acx = _ssd_time2(dtraw_ref, bias_ref, alog_ref, sel_ref[0])
        xdt = x * dtx
        ex = jnp.exp(acx)
        last = acx[L - 1:L, :]
        te = jnp.exp(last - acx)
        dlast = jnp.exp(last)
        isa = _is_a((L, LANES))
        is_last = lax.broadcasted_iota(jnp.int32, (L, LANES), 0) == L - 1
        strict = lax.broadcasted_iota(jnp.int32, (L, L), 0) > lax.broadcasted_iota(jnp.int32, (L, L), 1)
        lane_id = lax.broadcasted_iota(jnp.int32, (1, LANES), 1)
        row_id = lax.broadcasted_iota(jnp.int32, (8, 1), 0)
        lane_head = lax.broadcasted_iota(jnp.int32, (LANES, LANES), 0) // HEAD
        col_id = lax.broadcasted_iota(jnp.int32, (LANES, LANES), 1)
        dcb = jnp.zeros((L, L), F32)
        dcm = jnp.zeros((L, SSM_STATE), F32)
        dbm = jnp.zeros((L, SSM_STATE), F32)
        q_rows = jnp.zeros((L, LANES), F32)
        q_cols = jnp.zeros((8, L), F32)
        for i in range(NS):
            sl = slice(i * LANES, (i + 1) * LANES)
            acs = acx[:, sl]
            acs_t = acs.T
            xs, gs = xdt[:, sl], g[:, sl]
            xsb = xs.astype(BF16)
            dxd = jnp.zeros((L, LANES), F32)
            for pos in (0, 1):
                gp = jnp.where(isa if pos == 0 else ~isa, gs, 0.0).astype(BF16)
                dec = _decay(acs, acs_t, pos)
                dxd = dxd + _dot((cbm * dec).astype(BF16), gp, _TN)
                dmd = _dot(gp, xsb, _NT) * dec
                dcb = dcb + dmd
                q = jnp.where(strict, dmd * cbm, 0.0)
                q_rows = q_rows + jnp.sum(q, axis=1, keepdims=True) * (lane_id == 2 * i + pos).astype(F32)
                q_cols = q_cols + jnp.sum(q, axis=0, keepdims=True) * (row_id == 2 * i + pos).astype(F32)
            st = st_ref[0, i]
            dst = dstate[i]
            stb, dstb = st.astype(BF16), dst.astype(BF16)
            eg = (ex[:, sl] * gs).astype(BF16)
            dcm = dcm + _dot(eg, stb, _NT)
            yoff = _dot(cb_, stb) * ex[:, sl]
            w = xs * te[:, sl]
            wb = w.astype(BF16)
            dw = _dot(bb, dstb)
            dbm = dbm + _dot(wb, dstb, _NT)
            dxt = dxd + dw * te[:, sl]
            dal = dlast[:, sl] * jnp.sum(dst * st, axis=0, keepdims=True) + jnp.sum(dw * w, axis=0, keepdims=True)
            dac_l = gs * yoff - w * dw + jnp.where(is_last, dal, 0.0)
            ddt_l = dxt * x[:, sl]
            dstate[i] = dst * dlast[:, sl] + _dot(cb_, eg, _TN)
            dx_ref[:, sl] = dxt * dtx[:, sl] + dsk[:, sl] * gs
            to_head = (col_id == 2 * i + lane_head).astype(BF16)
            part = _dot_split(ddt_l, to_head, 2)
            parta = _dot_split(dac_l, to_head, 2)
            if i == 0:
                ddt_ref[0] = part
                dac_ref[0] = parta
            else:
                ddt_ref[0] += part
                dac_ref[0] += parta
        dac_ref[0] += q_rows - jnp.concatenate([q_cols, jnp.zeros((LANES - 8, L), F32)], axis=0).T
        dcbb = dcb.astype(BF16)
        dc_ref[...] = dcm + _dot(dcbb, bb)
        db_ref[...] = dbm + _dot(dcbb, cb_, _TN)

    part_spec = pl.BlockSpec((1, L, LANES), lambda g, c: (g, rev(c), 0))
    lane_spec = pl.BlockSpec((1, 1, GW), lambda g, c: (g, 0, 0))
    bc_out = pl.BlockSpec((L, SSM_STATE), lambda g, c: (rev(c), g))
    res = _call(
        body, name="ssd_bwd", grid=(G, nc),
        in_specs=[sp["grp"](rev), sp["bmat"](rev), sp["cmat"](rev), sp["grp"](rev), sp["dtraw"](rev), sp["grp"](rev),
                  sp["grp"](rev), pl.BlockSpec((1, NS, SSM_STATE, LANES), lambda g, c: (rev(c), g, 0, 0)),
                  sp["vec"], sp["vec"], sp["sel"], sp["gvec"], sp["gvec"]],
        out_specs=[sp["grp"](rev), sp["grp"](rev), bc_out, bc_out, part_spec, part_spec, lane_spec, lane_spec],
        out_shape=[jax.ShapeDtypeStruct((T, ZW), BF16), jax.ShapeDtypeStruct((T, DI), F32),
                   jax.ShapeDtypeStruct((T, G * SSM_STATE), F32), jax.ShapeDtypeStruct((T, G * SSM_STATE), F32),
                   jax.ShapeDtypeStruct((G, T, LANES), F32), jax.ShapeDtypeStruct((G, T, LANES), F32),
                   jax.ShapeDtypeStruct((G, 1, GW), F32), jax.ShapeDtypeStruct((G, 1, GW), F32)],
        scratch_shapes=[pltpu.VMEM((NS, SSM_STATE, LANES), F32)], sem=("parallel", "arbitrary"),
        args=(xc, xc, xc, zx, zx, yssd, dyo, states, bias, alog, sel, dskip, ng), carry=carry)
    return res if carry is None else (*res[:8], res[8:])


def _ssd_dt_bwd(zx, ddt_part, dac_part, dzx, bias, alog, DI):
    T, ZW = zx.shape
    G, L = SSM_GROUPS, SSM_CHUNK
    nc = T // L
    heads = DI // HEAD // G
    tail = ZW - 2 * DI - 2 * G * SSM_STATE
    dt_block = (ZW - tail) // LANES

    def body(dtraw_ref, ddt_ref, dac_ref, bias_ref, alog_ref, _, o_ref, dal_ref, dbias_ref):
        @pl.when(pl.program_id(0) == 0)
        def _():
            dal_ref[...] = jnp.zeros_like(dal_ref)
            dbias_ref[...] = jnp.zeros_like(dbias_ref)

        raw = dtraw_ref[...] + bias_ref[...]
        dt = jax.nn.softplus(raw)
        a = -jnp.exp(alog_ref[...])
        dac, ddt = dac_ref[0], ddt_ref[0]
        for gi in range(1, G):
            dac = dac + pltpu.roll(dac_ref[gi], gi * heads, axis=1)
            ddt = ddt + pltpu.roll(ddt_ref[gi], gi * heads, axis=1)
        dda = _tri_dot(dac, True)
        dal_ref[...] += jnp.sum(dda * dt, axis=0, keepdims=True) * a
        draw = (dda * a + ddt) * jax.nn.sigmoid(raw)
        dbias_ref[...] += jnp.sum(draw, axis=0, keepdims=True)
        o_ref[...] = jnp.concatenate([draw.astype(BF16), jnp.zeros((L, tail - LANES), BF16)], axis=1)

    vec = pl.BlockSpec((1, LANES), lambda c: (0, 0))
    part = pl.BlockSpec((G, L, LANES), lambda c: (0, c, 0))
    return pl.pallas_call(
        body, name="ssd_dt_bwd", grid=(nc,),
        in_specs=[pl.BlockSpec((L, LANES), lambda c: (c, dt_block)), part, part, vec, vec,
                  pl.BlockSpec(memory_space=pl.ANY)],
        out_specs=[pl.BlockSpec((L, tail), lambda c: (c, (ZW - tail) // tail)), vec, vec],
        out_shape=[jax.ShapeDtypeStruct((T, ZW), BF16), jax.ShapeDtypeStruct((1, LANES), F32),
                   jax.ShapeDtypeStruct((1, LANES), F32)],
        input_output_aliases={5: 0},
        compiler_params=_params(("arbitrary",)),
    )(zx, ddt_part, dac_part, bias, alog, dzx)


def _exchange(blocks, name):
    n = len(blocks)

    def body(*refs):
        local, sends, arrivals = _direct_copies(refs[:n], refs[n:2 * n], *refs[2 * n:])
        for cp in local + sends:
            cp.start()
        for cp in arrivals:
            cp.wait_recv()
        for cp in sends:
            cp.wait_send()
        for cp in local:
            cp.wait()

    any_spec = pl.BlockSpec(memory_space=pl.ANY)
    return pl.pallas_call(
        body, name=name, in_specs=[any_spec] * n, out_specs=[any_spec] * n,
        out_shape=[jax.ShapeDtypeStruct(b.shape, b.dtype) for b in blocks],
        scratch_shapes=[pltpu.SemaphoreType.DMA((n, 7)), pltpu.SemaphoreType.DMA((n, 7)),
                        pltpu.SemaphoreType.DMA((n,))],
    )(*blocks)


def _adamw(parts, w, m, v, name):
    nl = len(parts)
    R, C = parts[0].shape[1:]
    per_row = C * (N_DEV * nl * parts[0].dtype.itemsize + 7 * 4) * 2
    tr = R
    if R % 8 == 0:
        tr = 8
        for t in (16, 32, 64, 128, 256, 512):
            if R % t == 0 and t * per_row <= 24 * 1024 * 1024:
                tr = t
    nr = R // tr
    c1 = 1.0 - ADAM_B1 ** ADAM_STEP
    c2 = 1.0 - ADAM_B2 ** ADAM_STEP

    def body(*refs):
        p_refs = refs[:nl]
        w_ref, m_ref, v_ref, g_ref, d_ref, nm_ref, nv_ref = refs[nl:]
        for layer in range(nl):
            @pl.when(pl.program_id(0) == layer)
            def _(p_ref=p_refs[layer]):
                g = p_ref[0].astype(F32)
                for k in range(1, N_DEV):
                    g = g + p_ref[k].astype(F32)
                nm = ADAM_B1 * m_ref[...] + (1.0 - ADAM_B1) * g
                nv = ADAM_B2 * v_ref[...] + (1.0 - ADAM_B2) * (g * g)
                g_ref[...] = g
                nm_ref[...] = nm
                nv_ref[...] = nv
                d_ref[...] = -ADAM_LR * ((nm / c1) / (jnp.sqrt(nv / c2) + ADAM_EPS) + ADAM_WD * w_ref[...])

    def part_spec(layer):
        return pl.BlockSpec((N_DEV, tr, C), lambda l, i: (0, jnp.where(l == layer, i, jnp.where(l < layer, 0, nr - 1)), 0))

    blk = pl.BlockSpec((tr, C), lambda l, i: (l * nr + i, 0))
    out = jax.ShapeDtypeStruct((nl * R, C), F32)
    return pl.pallas_call(
        body, name=name, grid=(nl, nr),
        in_specs=[part_spec(layer) for layer in range(nl)] + [blk, blk, blk],
        out_specs=[blk, blk, blk, blk], out_shape=[out, out, out, out],
        compiler_params=_params(("arbitrary", "arbitrary")),
    )(*parts, w, m, v)


def _pad_cols(a, n):
    return jnp.pad(a, ((0, 0), (0, n - a.shape[1])))


def kernel(x, positions, mixer_norm, ffn_norm, attn_w_qkv, attn_q_norm, attn_k_norm, attn_sinks, attn_w_o, ssm_w_in, ssm_conv_w, ssm_conv_b, ssm_dt_bias, ssm_a_log, ssm_d, ssm_norm, ssm_w_out, ffn_w_gate, ffn_w_up, ffn_w_down, loss_target, m_mixer_norm, m_ffn_norm, m_attn_w_qkv, m_attn_q_norm, m_attn_k_norm, m_attn_sinks, m_attn_w_o, m_ssm_w_in, m_ssm_conv_w, m_ssm_conv_b, m_ssm_dt_bias, m_ssm_a_log, m_ssm_d, m_ssm_norm, m_ssm_w_out, m_ffn_w_gate, m_ffn_w_up, m_ffn_w_down, v_mixer_norm, v_ffn_norm, v_attn_w_qkv, v_attn_q_norm, v_attn_k_norm, v_attn_sinks, v_attn_w_o, v_ssm_w_in, v_ssm_conv_w, v_ssm_conv_b, v_ssm_dt_bias, v_ssm_a_log, v_ssm_d, v_ssm_norm, v_ssm_w_out, v_ffn_w_gate, v_ffn_w_up, v_ffn_w_down):
    T, D = x.shape[1], x.shape[2]
    HQ = D // HEAD
    HKV = HQ // ATT_GROUP
    QW = (HQ + 2 * HKV) * HEAD
    DI = 2 * D
    H = DI // HEAD
    G = SSM_GROUPS
    GW = DI // G
    CD = DI + 2 * G * SSM_STATE
    ZW = -(-(DI + CD + LANES) // 512) * 512
    IW = DI + CD + H
    assert T % 512 == 0 and D % 256 == 0 and HKV % 2 == 0 and GW % LANES == 0 and H <= LANES

    weights = dict(mixer_norm=mixer_norm, ffn_norm=ffn_norm, attn_w_qkv=attn_w_qkv, attn_q_norm=attn_q_norm,
                   attn_k_norm=attn_k_norm, attn_sinks=attn_sinks, attn_w_o=attn_w_o, ssm_w_in=ssm_w_in,
                   ssm_conv_w=ssm_conv_w, ssm_conv_b=ssm_conv_b, ssm_dt_bias=ssm_dt_bias, ssm_a_log=ssm_a_log,
                   ssm_d=ssm_d, ssm_norm=ssm_norm, ssm_w_out=ssm_w_out, ffn_w_gate=ffn_w_gate, ffn_w_up=ffn_w_up,
                   ffn_w_down=ffn_w_down)
    mom_m = dict(mixer_norm=m_mixer_norm, ffn_norm=m_ffn_norm, attn_w_qkv=m_attn_w_qkv, attn_q_norm=m_attn_q_norm,
                 attn_k_norm=m_attn_k_norm, attn_sinks=m_attn_sinks, attn_w_o=m_attn_w_o, ssm_w_in=m_ssm_w_in,
                 ssm_conv_w=m_ssm_conv_w, ssm_conv_b=m_ssm_conv_b, ssm_dt_bias=m_ssm_dt_bias, ssm_a_log=m_ssm_a_log,
                 ssm_d=m_ssm_d, ssm_norm=m_ssm_norm, ssm_w_out=m_ssm_w_out, ffn_w_gate=m_ffn_w_gate,
                 ffn_w_up=m_ffn_w_up, ffn_w_down=m_ffn_w_down)
    mom_v = dict(mixer_norm=v_mixer_norm, ffn_norm=v_ffn_norm, attn_w_qkv=v_attn_w_qkv, attn_q_norm=v_attn_q_norm,
                 attn_k_norm=v_attn_k_norm, attn_sinks=v_attn_sinks, attn_w_o=v_attn_w_o, ssm_w_in=v_ssm_w_in,
                 ssm_conv_w=v_ssm_conv_w, ssm_conv_b=v_ssm_conv_b, ssm_dt_bias=v_ssm_dt_bias, ssm_a_log=v_ssm_a_log,
                 ssm_d=v_ssm_d, ssm_norm=v_ssm_norm, ssm_w_out=v_ssm_w_out, ffn_w_gate=v_ffn_w_gate,
                 ffn_w_up=v_ffn_w_up, ffn_w_down=v_ffn_w_down)
    big = ["attn_w_qkv", "attn_w_o", "ssm_w_in", "ssm_w_out", "ffn_w_gate", "ffn_w_up", "ffn_w_down"]

    def flat2(a):
        return a.reshape(-1, a.shape[-1])

    def shard(n, layer=0):
        return weights[n][layer].astype(BF16)

    def from_cols(g):
        return g.transpose(1, 0, 2).reshape(g.shape[1], N_DEV * g.shape[2])

    def from_rows(g):
        return g.reshape(N_DEV * g.shape[1], g.shape[2])

    xs = x[0]
    tgt = loss_target[0]
    inv_freq = ROPE_THETA ** (-jnp.arange(0, HEAD, 2, dtype=F32) / HEAD)
    ang = positions[0].astype(F32)[:, None] * inv_freq
    cos = jnp.tile(jnp.cos(ang), (1, 4))
    sin = jnp.tile(jnp.concatenate([-jnp.sin(ang), jnp.sin(ang)], axis=1), (1, 2))
    gq = jnp.tile(attn_q_norm, (1, 2))
    gk = jnp.tile(attn_k_norm, (1, 2))
    sinkcol = jnp.repeat(attn_sinks.reshape(HKV, ATT_GROUP // 2, 2).transpose(0, 2, 1), WINDOW, axis=2)[..., None]
    bias_p = _pad_cols(ssm_dt_bias, LANES)
    alog_p = _pad_cols(ssm_a_log, LANES)
    dskip = jnp.repeat(ssm_d, HEAD, axis=1)
    lane_head = jnp.arange(DI) // HEAD
    sel = (jnp.arange(LANES)[None, :, None] == lane_head.reshape(G, 1, GW)).astype(BF16)
    vec_w = CD // N_DEV
    small = jnp.concatenate([ssm_conv_w[0], ssm_conv_b, _pad_cols(ssm_norm, vec_w),
                             jnp.zeros((2, vec_w), F32)], axis=0)

    def rows_to_blocks(p):
        return p.reshape(N_DEV, p.shape[0] // N_DEV, p.shape[1])

    def cols_to_blocks(p):
        return p.reshape(p.shape[0], N_DEV, p.shape[1] // N_DEV).transpose(1, 0, 2)

    hm0, got = _rms_fwd(xs, mixer_norm[0:1], "rms_fwd_m0", carry=[shard("attn_w_qkv")])
    w_qkv = from_cols(got[0])
    qkv, got = _matmul(hm0, w_qkv, mode="nn", out_dtype=F32, name="mm_qkv",
                       carry=[shard("attn_w_o"), shard("ffn_w_gate", 0), small])
    w_o = from_rows(got[0])
    w_gate = [from_cols(got[1]), None]
    small_all = got[2]
    conv_w = small_all[:, :SSM_CONV].transpose(1, 0, 2).reshape(SSM_CONV, CD)
    conv_b = small_all[:, SSM_CONV].reshape(1, CD)
    ng = small_all[:, SSM_CONV + 1, :DI // N_DEV].reshape(1, DI)
    qr, kd, vd = _attn_prep_fwd(qkv, cos, sin, gq, gk, D, HKV)
    o, got = _attn_fwd(qr, kd, vd, sinkcol, HKV, carry=[shard("ffn_w_up", 0)])
    w_up = [from_cols(got[0]), None]
    x1, got = _matmul(o, w_o, mode="nn", out_dtype=F32, name="mm_attn_out", add=xs, carry=[shard("ffn_w_down", 0)])
    w_down = [from_rows(got[0]), None]
    hf0 = _rms_fwd(x1, ffn_norm[0:1], "rms_fwd_f0")
    gate0, up0, act0, got = _ffn_up(hf0, w_gate[0], w_up[0], "ffn_up_0", carry=[shard("ssm_w_in"), shard("ssm_w_out")])
    w_in = _pad_cols(from_cols(got[0]), ZW)
    w_out = from_rows(got[1])
    x2 = _matmul(act0, w_down[0], mode="nn", out_dtype=F32, name="mm_ffn_down_0", add=x1)
    hm1 = _rms_fwd(x2, mixer_norm[1:2], "rms_fwd_m1")
    zx, got = _matmul(hm1, w_in, mode="nn", out_dtype=F32, name="mm_ssm_in",
                      carry=[shard("ffn_w_gate", 1), shard("ffn_w_up", 1)])
    w_gate[1], w_up[1] = from_cols(got[0]), from_cols(got[1])
    xc = _conv_fwd(zx, conv_w, conv_b, DI, CD)
    yssd, yout, states, got = _ssd_fwd(zx, xc, bias_p, alog_p, sel, dskip, ng, DI, carry=[shard("ffn_w_down", 1)])
    w_down[1] = from_rows(got[0])
    x3 = _matmul(yout, w_out, mode="nn", out_dtype=F32, name="mm_ssm_out", add=x2)
    hf1 = _rms_fwd(x3, ffn_norm[1:2], "rms_fwd_f1")
    gate1, up1, act1 = _ffn_up(hf1, w_gate[1], w_up[1], "ffn_up_1")
    x4 = _matmul(act1, w_down[1], mode="nn", out_dtype=F32, name="mm_ffn_down_1", add=x3)
    sq, dx4, dx4b = _loss_head(x4, tgt)
    loss = lax.psum(sq[0, 0] * (0.5 / D), ("x", "y", "c"))

    def halves(blocks):
        half = blocks.shape[1] // 2
        return blocks[:, :half], blocks[:, half:]

    def ffn_bwd(dy, dyb, hf, gate, up, act, layer, xin, gain):
        dg, du = _ffn_dact(dyb, w_down[layer], gate, up, f"ffn_dact_{layer}")
        g_down = _matmul(act, dyb, mode="tn", out_dtype=BF16, name=f"mm_dw_down_{layer}")
        down_a, down_b = halves(rows_to_blocks(g_down))
        g_gate, got_da = _matmul(hf, dg, mode="tn", out_dtype=BF16, name=f"mm_dw_gate_{layer}", carry=[down_a])
        g_up, got_db = _matmul(hf, du, mode="tn", out_dtype=BF16, name=f"mm_dw_up_{layer}", carry=[down_b])
        gate_a, gate_b = halves(cols_to_blocks(g_gate))
        dh, got_g = _matmul(dg, w_gate[layer], mode="nt", out_dtype=F32, name=f"mm_dh_ffn_{layer}",
                            second=(du, w_up[layer]), carry=[gate_a, gate_b])
        dx, dxb, dgain = _rms_bwd(xin, gain, dh, dy, f"rms_bwd_f{layer}")
        return dx, dxb, dgain, cols_to_blocks(g_up), dict(down=[got_da[0], got_db[0]], gate=[got_g[0], got_g[1]])

    dx3, dx3b, d_fn1, up1_blocks, ffn1_got = ffn_bwd(dx4, dx4b, hf1, gate1, up1, act1, 1, x3, ffn_norm[1:2])
    dyo = _matmul(dx3b, w_out, mode="nt", out_dtype=F32, name="mm_dyout")
    g_wout = _matmul(yout, dx3b, mode="tn", out_dtype=BF16, name="mm_dw_ssm_out")
    dzx, dxx, dbm, dcm, ddt_p, dac_p, dd_l, dng_l, got1 = _ssd_bwd(
        zx, xc, yssd, dyo, states, bias_p, alog_p, sel, dskip, ng, DI,
        carry=[up1_blocks, rows_to_blocks(g_wout)])
    dzx, d_alog, d_bias = _ssd_dt_bwd(zx, ddt_p, dac_p, dzx, bias_p, alog_p, DI)
    dzx, d_convw, d_convb = _conv_bwd(zx, [dxx, dbm, dcm], dzx, conv_w, conv_b, DI, CD)
    g_win = _matmul(hm1, dzx, mode="tn", out_dtype=BF16, name="mm_dw_ssm_in")[:, :IW]
    dh, got2 = _matmul(dzx, w_in, mode="nt", out_dtype=F32, name="mm_dh_ssm_in", carry=[cols_to_blocks(g_win)])
    dx2, dx2b, d_mn1 = _rms_bwd(x2, mixer_norm[1:2], dh, dx3, "rms_bwd_m1")
    dx1, dx1b, d_fn0, up0_blocks, ffn0_got = ffn_bwd(dx2, dx2b, hf0, gate0, up0, act0, 0, x1, ffn_norm[0:1])
    do = _matmul(dx1b, w_o, mode="nt", out_dtype=F32, name="mm_do")
    g_wo = _matmul(o, dx1b, mode="tn", out_dtype=BF16, name="mm_dw_attn_out")
    dq, dkd, dvd, dsink, got3 = _attn_bwd(qr, kd, vd, o, do, sinkcol, HKV, carry=[up0_blocks, rows_to_blocks(g_wo)])
    dqkv, dgq_l, dgk_l = _attn_prep_bwd(qkv, dq, dkd, dvd, cos, sin, gq, gk, D, HKV)
    g_wqkv = _matmul(hm0, dqkv, mode="tn", out_dtype=BF16, name="mm_dw_qkv")
    dh, got_c = _matmul(dqkv, w_qkv, mode="nt", out_dtype=F32, name="mm_dh_qkv", carry=[cols_to_blocks(g_wqkv)])
    dx0, _, d_mn0 = _rms_bwd(xs, mixer_norm[0:1], dh, dx1, "rms_bwd_m0")

    d_ng = dng_l.reshape(1, DI)
    vec_send = jnp.concatenate([
        d_convw.reshape(SSM_CONV, N_DEV, vec_w).transpose(1, 0, 2), d_convb.reshape(1, N_DEV, vec_w).transpose(1, 0, 2),
        _pad_cols(d_ng.reshape(N_DEV, DI // N_DEV), vec_w)[:, None, :], jnp.zeros((N_DEV, 2, vec_w), F32)], axis=1)
    d_sinks = dsink[:, :, 0].reshape(1, HQ)
    d_gq = dgq_l[:, :HEAD] + dgq_l[:, HEAD:]
    d_gk = dgk_l[:, :HEAD] + dgk_l[:, HEAD:]
    d_dskip = dd_l.reshape(H, HEAD).sum(axis=1).reshape(1, H)
    rep_names = ["mixer_norm", "ffn_norm", "attn_q_norm", "attn_k_norm", "attn_sinks", "ssm_dt_bias", "ssm_a_log",
                 "ssm_d"]
    rep_grads = [jnp.concatenate([d_mn0, d_mn1], axis=0), jnp.concatenate([d_fn0, d_fn1], axis=0), d_gq, d_gk,
                 d_sinks, d_bias[:, :H], d_alog[:, :H], d_dskip]
    rep_sizes = [weights[n].size for n in rep_names]
    rep_len = -(-sum(rep_sizes) // (8 * LANES)) * 8 * LANES

    def pack(arrs):
        flat = jnp.concatenate([a.reshape(-1) for a in arrs])
        return jnp.pad(flat, (0, rep_len - flat.shape[0])).reshape(rep_len // LANES, LANES)

    rep_send = jnp.broadcast_to(pack(rep_grads)[None], (N_DEV, rep_len // LANES, LANES))
    got4 = _exchange([vec_send, rep_send], "exchange_last")
    parts_of = {
        "attn_w_qkv": [got_c[0]], "attn_w_o": [got3[1]], "ssm_w_in": [got2[0]], "ssm_w_out": [got1[1]],
        "ffn_w_gate": ffn0_got["gate"] + ffn1_got["gate"], "ffn_w_up": [got3[0], got1[0]],
        "ffn_w_down": ffn0_got["down"] + ffn1_got["down"],
    }

    out = {}
    for n in big:
        res = _adamw(parts_of[n], flat2(weights[n]), flat2(mom_m[n]), flat2(mom_v[n]), f"adamw_{n}")
        out[n] = [r.reshape(weights[n].shape) for r in res]

    def vec_block(d):
        return jnp.concatenate([d["ssm_conv_w"][0], d["ssm_conv_b"], _pad_cols(d["ssm_norm"], vec_w),
                                jnp.zeros((2, vec_w), F32)], axis=0)

    res = _adamw([got4[0]], vec_block(weights), vec_block(mom_m), vec_block(mom_v), "adamw_vectors")
    out["ssm_conv_w"] = [r[:SSM_CONV][None] for r in res]
    out["ssm_conv_b"] = [r[SSM_CONV:SSM_CONV + 1] for r in res]
    out["ssm_norm"] = [r[SSM_CONV + 1:SSM_CONV + 2, :DI // N_DEV] for r in res]
    res = _adamw([got4[1]], pack([weights[n] for n in rep_names]), pack([mom_m[n] for n in rep_names]),
                 pack([mom_v[n] for n in rep_names]), "adamw_replicated")
    offs = 0
    for n, sz in zip(rep_names, rep_sizes):
        out[n] = [r.reshape(-1)[offs:offs + sz].reshape(weights[n].shape) for r in res]
        offs += sz

    names = list(weights)
    return (loss, dx0[None], *[out[n][0] for n in names], *[out[n][1] for n in names],
            *[out[n][2] for n in names], *[out[n][3] for n in names])
```
